```python
import math
import jax, jax.numpy as jnp
from jax import lax
import numpy as np

D_MODEL = 2048
BATCH = 8
SEQ = 2048
DEPTH = 4

MLA_HEADS = 8
MLA_NOPE = 128
MLA_ROPE = 64
MLA_V = 128
MLA_Q_LORA = 512
MLA_KV_LORA = 256
MLA_WIDTH = MLA_HEADS * MLA_V
ROPE_THETA = 10000.0

SSM_WIDTH = D_MODEL // 4
SSM_GROUP = 16
SSM_GROUPS = SSM_WIDTH // SSM_GROUP
SSM_STATE = 64

DIL_WIDTH = D_MODEL // 4
DIL_HEAD_DIM = 64
DIL_HEADS = DIL_WIDTH // DIL_HEAD_DIM
DIL_PATTERNS = ((128, 1), (512, 4), (2048, 16))

BLOCK = 128
MIX_WIDTH = MLA_WIDTH + SSM_WIDTH + DIL_WIDTH
IN_SPLITS = (MLA_Q_LORA, MLA_KV_LORA, MLA_ROPE, SSM_WIDTH, DIL_WIDTH, DIL_WIDTH, DIL_WIDTH)
IN_WIDTH = sum(IN_SPLITS)
D_FF = ((8 * D_MODEL + 3 * 256 - 1) // (3 * 256)) * 256
NORM_EPS = 1e-6

kernel_name = "hymba_mla_s5_dilated_hybrid"


def rms_norm(x, g):
    xf = x.astype(jnp.float32)
    y = xf * lax.rsqrt(jnp.mean(xf * xf, axis=-1, keepdims=True) + NORM_EPS)
    return (y * g.astype(jnp.float32)).astype(x.dtype)


def apply_rope(x, pos):
    half = x.shape[-1] // 2
    inv_freq = ROPE_THETA ** (-jnp.arange(half, dtype=jnp.float32) / half)
    ang = pos.astype(jnp.float32)[:, None] * inv_freq[None, :]
    cos = jnp.cos(ang)[None, :, None, :]
    sin = jnp.sin(ang)[None, :, None, :]
    xf = x.astype(jnp.float32)
    x1, x2 = xf[..., :half], xf[..., half:]
    return jnp.concatenate([x1 * cos - x2 * sin, x2 * cos + x1 * sin], axis=-1).astype(x.dtype)


def mla_mixer(c_q, c_kv, k_rope, g_q, w_uq, g_kv, w_ukv):
    B, S, _ = c_q.shape
    pos = jnp.arange(S)
    q = (rms_norm(c_q, g_q) @ w_uq).reshape(B, S, MLA_HEADS, MLA_NOPE + MLA_ROPE)
    q_nope = q[..., :MLA_NOPE]
    q_pe = apply_rope(q[..., MLA_NOPE:], pos)
    kv = (rms_norm(c_kv, g_kv) @ w_ukv).reshape(B, S, MLA_HEADS, MLA_NOPE + MLA_V)
    k_nope, v = kv[..., :MLA_NOPE], kv[..., MLA_NOPE:]
    k_pe = apply_rope(k_rope[:, :, None, :], pos)[:, :, 0]
    scale = (MLA_NOPE + MLA_ROPE) ** -0.5
    nb = S // BLOCK
    qn_b = q_nope.reshape(B, nb, BLOCK, MLA_HEADS, MLA_NOPE).transpose(1, 0, 2, 3, 4)
    qp_b = q_pe.reshape(B, nb, BLOCK, MLA_HEADS, MLA_ROPE).transpose(1, 0, 2, 3, 4)
    kpos = jnp.arange(S)

    def one_block(args):
        b, qn, qp = args
        s = (jnp.einsum('bqhd,bkhd->bhqk', qn, k_nope).astype(jnp.float32)
             + jnp.einsum('bqhd,bkd->bhqk', qp, k_pe).astype(jnp.float32)) * scale
        qpos = b * BLOCK + jnp.arange(BLOCK)
        causal = qpos[:, None] >= kpos[None, :]
        s = jnp.where(causal[None, None], s, -jnp.inf)
        p = jax.nn.softmax(s, axis=-1).astype(v.dtype)
        return jnp.einsum('bhqk,bkhd->bqhd', p, v)

    out = lax.map(one_block, (jnp.arange(nb), qn_b, qp_b))
    return out.transpose(1, 0, 2, 3, 4).reshape(B, S, MLA_WIDTH)


def s5_mixer(u, a_re, a_im, b_re, b_im, c_re, c_im, d_skip, log_dt, w_glu, b_glu):
    B, S, _ = u.shape
    f32 = jnp.float32
    uf = u.astype(f32).reshape(B, S, SSM_GROUPS, SSM_GROUP)
    lam = lax.complex(jnp.minimum(a_re.astype(f32), -1e-4), a_im.astype(f32))
    dt = jnp.exp(log_dt.astype(f32))[:, None]
    a_bar = jnp.exp(lam * dt)
    b_cplx = lax.complex(b_re.astype(f32), b_im.astype(f32))
    b_bar = ((a_bar - 1.0) / lam)[..., None] * b_cplx
    bu = jnp.einsum('gnp,bsgp->bsgn', b_bar, uf.astype(jnp.complex64))
    a_seq = jnp.broadcast_to(a_bar, bu.shape)

    def combine(left, right):
        a_l, h_l = left
        a_r, h_r = right
        return a_r * a_l, a_r * h_l + h_r

    _, h = lax.associative_scan(combine, (a_seq, bu), axis=1)
    c_cplx = lax.complex(c_re.astype(f32), c_im.astype(f32))
    y = jnp.einsum('gpn,bsgn->bsgp', c_cplx, h).real + d_skip.astype(f32) * uf
    y = jax.nn.gelu(y.reshape(B, S, SSM_WIDTH))
    z = y @ w_glu.astype(f32) + b_glu.astype(f32)
    out = z[..., :SSM_WIDTH] * jax.nn.sigmoid(z[..., SSM_WIDTH:])
    return out.astype(u.dtype)


def strided_fold(x, dil):
    B, S = x.shape[:2]
    rest = x.shape[2:]
    return x.reshape(B, S // dil, dil, *rest).swapaxes(1, 2).reshape(B * dil, S // dil, *rest)


def strided_unfold(x, batch, dil):
    L = x.shape[1]
    rest = x.shape[2:]
    return x.reshape(batch, dil, L, *rest).swapaxes(1, 2).reshape(batch, L * dil, *rest)


def banded_window_attention(q, k, v, span):
    Z, L, H, D = q.shape
    nb = -(-L // BLOCK)
    Lp = nb * BLOCK
    pad = ((0, 0), (0, Lp - L), (0, 0), (0, 0))
    qb, kb, vb = [jnp.pad(t, pad).reshape(Z, nb, BLOCK, H, D) for t in (q, k, v)]

    def with_prev(t):
        prev = jnp.pad(t, ((0, 0), (1, 0), (0, 0), (0, 0), (0, 0)))[:, :-1]
        return jnp.concatenate([prev, t], axis=2)

    kk, vv = with_prev(kb), with_prev(vb)
    s = jnp.einsum('znqhd,znkhd->znhqk', qb, kk).astype(jnp.float32) * (D ** -0.5)
    qpos = jnp.arange(nb)[:, None] * BLOCK + jnp.arange(BLOCK)[None, :]
    kpos = (jnp.arange(nb)[:, None] - 1) * BLOCK + jnp.arange(2 * BLOCK)[None, :]
    dist = qpos[:, :, None] - kpos[:, None, :]
    mask = (dist >= 0) & (dist <= span) & (kpos[:, None, :] >= 0)
    s = jnp.where(mask[None, :, None], s, -jnp.inf)
    m = jnp.max(s, axis=-1, keepdims=True)
    p = jnp.exp(s - m)
    l = jnp.sum(p, axis=-1, keepdims=True)
    o = jnp.einsum('znhqk,znkhd->znqhd', (p / l).astype(v.dtype), vv)
    lse = (m + jnp.log(l))[..., 0].transpose(0, 1, 3, 2).reshape(Z, Lp, H)
    return o.reshape(Z, Lp, H, D)[:, :L], lse[:, :L]


def dilated_mixer(qd, kd, vd):
    B, S, _ = qd.shape
    q, k, v = [t.reshape(B, S, DIL_HEADS, DIL_HEAD_DIM) for t in (qd, kd, vd)]
    outs, lses = [], []
    for window, dil in DIL_PATTERNS:
        o, lse = banded_window_attention(strided_fold(q, dil), strided_fold(k, dil),
                                         strided_fold(v, dil), window // dil)
        outs.append(strided_unfold(o, B, dil).astype(jnp.float32))
        lses.append(strided_unfold(lse, B, dil))
    wts = jax.nn.softmax(jnp.stack(lses, axis=0), axis=0)
    out = wts[0][..., None] * outs[0] + wts[1][..., None] * outs[1] + wts[2][..., None] * outs[2]
    return out.reshape(B, S, DIL_WIDTH).astype(qd.dtype)


def _fwd_setup_inputs(seed: int = 0) -> dict:
    key = jax.random.key(seed)
    ks = jax.random.split(key, 32)
    L = DEPTH
    nrm = lambda k, shape, scale: jax.random.normal(k, shape, jnp.float32) * scale
    gain = lambda k, n: 1.0 + 0.02 * jax.random.normal(k, (L, n), jnp.float32)
    out_scale = (2 * DEPTH) ** -0.5
    n_idx = jnp.arange(SSM_STATE, dtype=jnp.float32)
    return {
        "x": jax.random.normal(ks[0], (BATCH, SEQ, D_MODEL), jnp.float32),
        "g_mix": gain(ks[1], D_MODEL),
        "w_in": nrm(ks[2], (L, D_MODEL, IN_WIDTH), D_MODEL ** -0.5),
        "g_q": gain(ks[3], MLA_Q_LORA),
        "w_uq": nrm(ks[4], (L, MLA_Q_LORA, MLA_HEADS * (MLA_NOPE + MLA_ROPE)), MLA_Q_LORA ** -0.5),
        "g_kv": gain(ks[5], MLA_KV_LORA),
        "w_ukv": nrm(ks[6], (L, MLA_KV_LORA, MLA_HEADS * (MLA_NOPE + MLA_V)), MLA_KV_LORA ** -0.5),
        "a_re": -0.5 + nrm(ks[7], (L, SSM_GROUPS, SSM_STATE), 0.01),
        "a_im": math.pi * n_idx + nrm(ks[8], (L, SSM_GROUPS, SSM_STATE), 0.01),
        "b_re": nrm(ks[9], (L, SSM_GROUPS, SSM_STATE, SSM_GROUP), (2 * SSM_GROUP) ** -0.5),
        "b_im": nrm(ks[10], (L, SSM_GROUPS, SSM_STATE, SSM_GROUP), (2 * SSM_GROUP) ** -0.5),
        "c_re": nrm(ks[11], (L, SSM_GROUPS, SSM_GROUP, SSM_STATE), 0.5),
        "c_im": nrm(ks[12], (L, SSM_GROUPS, SSM_GROUP, SSM_STATE), 0.5),
        "d_skip": nrm(ks[13], (L, SSM_GROUPS, SSM_GROUP), 1.0),
        "log_dt": jax.random.uniform(ks[14], (L, SSM_GROUPS), jnp.float32,
                                     math.log(1e-3), math.log(1e-1)),
        "w_glu": nrm(ks[15], (L, SSM_WIDTH, 2 * SSM_WIDTH), SSM_WIDTH ** -0.5),
        "b_glu": nrm(ks[16], (L, 2 * SSM_WIDTH), 0.01),
        "g_out_mla": gain(ks[17], MLA_WIDTH),
        "g_out_ssm": gain(ks[18], SSM_WIDTH),
        "g_out_dil": gain(ks[19], DIL_WIDTH),
        "w_o": nrm(ks[20], (L, MIX_WIDTH, D_MODEL), MIX_WIDTH ** -0.5 * out_scale),
        "g_ffn": gain(ks[21], D_MODEL),
        "w_gate": nrm(ks[22], (L, D_MODEL, D_FF), D_MODEL ** -0.5),
        "w_up": nrm(ks[23], (L, D_MODEL, D_FF), D_MODEL ** -0.5),
        "w_down": nrm(ks[24], (L, D_FF, D_MODEL), D_FF ** -0.5 * out_scale),
        "g_final": 1.0 + 0.02 * jax.random.normal(ks[25], (D_MODEL,), jnp.float32),
    }


def _fwd_reference(x, g_mix, w_in, g_q, w_uq, g_kv, w_ukv, a_re, a_im, b_re, b_im, c_re, c_im,
              d_skip, log_dt, w_glu, b_glu, g_out_mla, g_out_ssm, g_out_dil, w_o,
              g_ffn, w_gate, w_up, w_down, g_final):
    split_at = np.cumsum(IN_SPLITS)[:-1].tolist()
    for l in range(DEPTH):
        h = rms_norm(x, g_mix[l])
        proj = h @ w_in[l]
        c_q, c_kv, k_rope, u, qd, kd, vd = jnp.split(proj, split_at, axis=-1)
        y_mla = mla_mixer(c_q, c_kv, k_rope, g_q[l], w_uq[l], g_kv[l], w_ukv[l])
        y_ssm = s5_mixer(u, a_re[l], a_im[l], b_re[l], b_im[l], c_re[l], c_im[l],
                         d_skip[l], log_dt[l], w_glu[l], b_glu[l])
        y_dil = dilated_mixer(qd, kd, vd)
        y = jnp.concatenate([rms_norm(y_mla, g_out_mla[l]),
                             rms_norm(y_ssm, g_out_ssm[l]),
                             rms_norm(y_dil, g_out_dil[l])], axis=-1)
        x = x + y @ w_o[l]
        h = rms_norm(x, g_ffn[l])
        x = x + (jax.nn.silu(h @ w_gate[l]) * (h @ w_up[l])) @ w_down[l]
    return rms_norm(x, g_final)


import jax as _jax
import jax.numpy as _jnp

TWIN_FORMAT = 'train_step'
FWD_PARAMS = ['x', 'g_mix', 'w_in', 'g_q', 'w_uq', 'g_kv', 'w_ukv', 'a_re', 'a_im', 'b_re', 'b_im', 'c_re', 'c_im', 'd_skip', 'log_dt', 'w_glu', 'b_glu', 'g_out_mla', 'g_out_ssm', 'g_out_dil', 'w_o', 'g_ffn', 'w_gate', 'w_up', 'w_down', 'g_final']
TWIN_WEIGHTS = ['g_mix', 'w_in', 'g_q', 'w_uq', 'g_kv', 'w_ukv', 'a_re', 'a_im', 'b_re', 'b_im', 'c_re', 'c_im', 'd_skip', 'log_dt', 'w_glu', 'b_glu', 'g_out_mla', 'g_out_ssm', 'g_out_dil', 'w_o', 'g_ffn', 'w_gate', 'w_up', 'w_down', 'g_final']
TWIN_DIFF_INPUT = 'x'
TWIN_INPUTS = ['x', 'g_mix', 'w_in', 'g_q', 'w_uq', 'g_kv', 'w_ukv', 'a_re', 'a_im', 'b_re', 'b_im', 'c_re', 'c_im', 'd_skip', 'log_dt', 'w_glu', 'b_glu', 'g_out_mla', 'g_out_ssm', 'g_out_dil', 'w_o', 'g_ffn', 'w_gate', 'w_up', 'w_down', 'g_final', 'loss_target', 'm_g_mix', 'm_w_in', 'm_g_q', 'm_w_uq', 'm_g_kv', 'm_w_ukv', 'm_a_re', 'm_a_im', 'm_b_re', 'm_b_im', 'm_c_re', 'm_c_im', 'm_d_skip', 'm_log_dt', 'm_w_glu', 'm_b_glu', 'm_g_out_mla', 'm_g_out_ssm', 'm_g_out_dil', 'm_w_o', 'm_g_ffn', 'm_w_gate', 'm_w_up', 'm_w_down', 'm_g_final', 'v_g_mix', 'v_w_in', 'v_g_q', 'v_w_uq', 'v_g_kv', 'v_w_ukv', 'v_a_re', 'v_a_im', 'v_b_re', 'v_b_im', 'v_c_re', 'v_c_im', 'v_d_skip', 'v_log_dt', 'v_w_glu', 'v_b_glu', 'v_g_out_mla', 'v_g_out_ssm', 'v_g_out_dil', 'v_w_o', 'v_g_ffn', 'v_w_gate', 'v_w_up', 'v_w_down', 'v_g_final']
TWIN_OUTPUTS = ['loss', 'grad_x', 'grad_g_mix', 'grad_w_in', 'grad_g_q', 'grad_w_uq', 'grad_g_kv', 'grad_w_ukv', 'grad_a_re', 'grad_a_im', 'grad_b_re', 'grad_b_im', 'grad_c_re', 'grad_c_im', 'grad_d_skip', 'grad_log_dt', 'grad_w_glu', 'grad_b_glu', 'grad_g_out_mla', 'grad_g_out_ssm', 'grad_g_out_dil', 'grad_w_o', 'grad_g_ffn', 'grad_w_gate', 'grad_w_up', 'grad_w_down', 'grad_g_final', 'delta_g_mix', 'delta_w_in', 'delta_g_q', 'delta_w_uq', 'delta_g_kv', 'delta_w_ukv', 'delta_a_re', 'delta_a_im', 'delta_b_re', 'delta_b_im', 'delta_c_re', 'delta_c_im', 'delta_d_skip', 'delta_log_dt', 'delta_w_glu', 'delta_b_glu', 'delta_g_out_mla', 'delta_g_out_ssm', 'delta_g_out_dil', 'delta_w_o', 'delta_g_ffn', 'delta_w_gate', 'delta_w_up', 'delta_w_down', 'delta_g_final', 'new_m_g_mix', 'new_m_w_in', 'new_m_g_q', 'new_m_w_uq', 'new_m_g_kv', 'new_m_w_ukv', 'new_m_a_re', 'new_m_a_im', 'new_m_b_re', 'new_m_b_im', 'new_m_c_re', 'new_m_c_im', 'new_m_d_skip', 'new_m_log_dt', 'new_m_w_glu', 'new_m_b_glu', 'new_m_g_out_mla', 'new_m_g_out_ssm', 'new_m_g_out_dil', 'new_m_w_o', 'new_m_g_ffn', 'new_m_w_gate', 'new_m_w_up', 'new_m_w_down', 'new_m_g_final', 'new_v_g_mix', 'new_v_w_in', 'new_v_g_q', 'new_v_w_uq', 'new_v_g_kv', 'new_v_w_ukv', 'new_v_a_re', 'new_v_a_im', 'new_v_b_re', 'new_v_b_im', 'new_v_c_re', 'new_v_c_im', 'new_v_d_skip', 'new_v_log_dt', 'new_v_w_glu', 'new_v_b_glu', 'new_v_g_out_mla', 'new_v_g_out_ssm', 'new_v_g_out_dil', 'new_v_w_o', 'new_v_g_ffn', 'new_v_w_gate', 'new_v_w_up', 'new_v_w_down', 'new_v_g_final']
TWIN_LEAF_KINDS = {'loss': 'loss', 'grad_x': 'grad_x', 'grad_g_mix': 'grad_w', 'grad_w_in': 'grad_w', 'grad_g_q': 'grad_w', 'grad_w_uq': 'grad_w', 'grad_g_kv': 'grad_w', 'grad_w_ukv': 'grad_w', 'grad_a_re': 'grad_w', 'grad_a_im': 'grad_w', 'grad_b_re': 'grad_w', 'grad_b_im': 'grad_w', 'grad_c_re': 'grad_w', 'grad_c_im': 'grad_w', 'grad_d_skip': 'grad_w', 'grad_log_dt': 'grad_w', 'grad_w_glu': 'grad_w', 'grad_b_glu': 'grad_w', 'grad_g_out_mla': 'grad_w', 'grad_g_out_ssm': 'grad_w', 'grad_g_out_dil': 'grad_w', 'grad_w_o': 'grad_w', 'grad_g_ffn': 'grad_w', 'grad_w_gate': 'grad_w', 'grad_w_up': 'grad_w', 'grad_w_down': 'grad_w', 'grad_g_final': 'grad_w', 'delta_g_mix': 'delta_w', 'delta_w_in': 'delta_w', 'delta_g_q': 'delta_w', 'delta_w_uq': 'delta_w', 'delta_g_kv': 'delta_w', 'delta_w_ukv': 'delta_w', 'delta_a_re': 'delta_w', 'delta_a_im': 'delta_w', 'delta_b_re': 'delta_w', 'delta_b_im': 'delta_w', 'delta_c_re': 'delta_w', 'delta_c_im': 'delta_w', 'delta_d_skip': 'delta_w', 'delta_log_dt': 'delta_w', 'delta_w_glu': 'delta_w', 'delta_b_glu': 'delta_w', 'delta_g_out_mla': 'delta_w', 'delta_g_out_ssm': 'delta_w', 'delta_g_out_dil': 'delta_w', 'delta_w_o': 'delta_w', 'delta_g_ffn': 'delta_w', 'delta_w_gate': 'delta_w', 'delta_w_up': 'delta_w', 'delta_w_down': 'delta_w', 'delta_g_final': 'delta_w', 'new_m_g_mix': 'new_m', 'new_m_w_in': 'new_m', 'new_m_g_q': 'new_m', 'new_m_w_uq': 'new_m', 'new_m_g_kv': 'new_m', 'new_m_w_ukv': 'new_m', 'new_m_a_re': 'new_m', 'new_m_a_im': 'new_m', 'new_m_b_re': 'new_m', 'new_m_b_im': 'new_m', 'new_m_c_re': 'new_m', 'new_m_c_im': 'new_m', 'new_m_d_skip': 'new_m', 'new_m_log_dt': 'new_m', 'new_m_w_glu': 'new_m', 'new_m_b_glu': 'new_m', 'new_m_g_out_mla': 'new_m', 'new_m_g_out_ssm': 'new_m', 'new_m_g_out_dil': 'new_m', 'new_m_w_o': 'new_m', 'new_m_g_ffn': 'new_m', 'new_m_w_gate': 'new_m', 'new_m_w_up': 'new_m', 'new_m_w_down': 'new_m', 'new_m_g_final': 'new_m', 'new_v_g_mix': 'new_v', 'new_v_w_in': 'new_v', 'new_v_g_q': 'new_v', 'new_v_w_uq': 'new_v', 'new_v_g_kv': 'new_v', 'new_v_w_ukv': 'new_v', 'new_v_a_re': 'new_v', 'new_v_a_im': 'new_v', 'new_v_b_re': 'new_v', 'new_v_b_im': 'new_v', 'new_v_c_re': 'new_v', 'new_v_c_im': 'new_v', 'new_v_d_skip': 'new_v', 'new_v_log_dt': 'new_v', 'new_v_w_glu': 'new_v', 'new_v_b_glu': 'new_v', 'new_v_g_out_mla': 'new_v', 'new_v_g_out_ssm': 'new_v', 'new_v_g_out_dil': 'new_v', 'new_v_w_o': 'new_v', 'new_v_g_ffn': 'new_v', 'new_v_w_gate': 'new_v', 'new_v_w_up': 'new_v', 'new_v_w_down': 'new_v', 'new_v_g_final': 'new_v'}


def _forward(args):
    return _fwd_reference(*[args[k] for k in FWD_PARAMS])


def _output_shape():
    out = _jax.eval_shape(lambda: _forward(_fwd_setup_inputs(0)))
    return out.shape, out.dtype

N_MICROBATCH = 1
ADAM_LR = 0.001
ADAM_B1 = 0.9
ADAM_B2 = 0.999
ADAM_EPS = 1e-08
ADAM_WD = 0.01
ADAM_STEP = 10
PER_EXAMPLE_BATCH_AXIS = {'x': 0, 'loss_target': 0}
SHARED_INPUTS = []
_WEIGHT_DTYPES = {'g_mix': _jnp.float32, 'w_in': _jnp.float32, 'g_q': _jnp.float32, 'w_uq': _jnp.float32, 'g_kv': _jnp.float32, 'w_ukv': _jnp.float32, 'a_re': _jnp.float32, 'a_im': _jnp.float32, 'b_re': _jnp.float32, 'b_im': _jnp.float32, 'c_re': _jnp.float32, 'c_im': _jnp.float32, 'd_skip': _jnp.float32, 'log_dt': _jnp.float32, 'w_glu': _jnp.float32, 'b_glu': _jnp.float32, 'g_out_mla': _jnp.float32, 'g_out_ssm': _jnp.float32, 'g_out_dil': _jnp.float32, 'w_o': _jnp.float32, 'g_ffn': _jnp.float32, 'w_gate': _jnp.float32, 'w_up': _jnp.float32, 'w_down': _jnp.float32, 'g_final': _jnp.float32}
MOMENT_SCALE = {'g_mix': 2.928574e-02, 'w_in': 2.510360e-02, 'g_q': 1.960069e-02, 'w_uq': 1.113923e-02, 'g_kv': 6.837914e-02, 'w_ukv': 2.275876e-02, 'a_re': 7.051590e-03, 'a_im': 6.828140e-03, 'b_re': 4.322422e-03, 'b_im': 4.287481e-03, 'c_re': 1.530647e-03, 'c_im': 1.591922e-03, 'd_skip': 2.965551e-02, 'log_dt': 4.934073e+00, 'w_glu': 2.096110e-02, 'b_glu': 4.991617e-02, 'g_out_mla': 3.131306e-02, 'g_out_ssm': 2.934127e-02, 'g_out_dil': 2.663449e-02, 'w_o': 8.313791e-02, 'g_ffn': 1.640048e-02, 'w_gate': 7.122514e-03, 'w_up': 6.923150e-03, 'w_down': 3.249247e-02, 'g_final': 8.062369e+00}


def _to_microbatches(a, axis):
    t = _jnp.moveaxis(a, axis, 0)
    t = t.reshape((N_MICROBATCH, t.shape[0] // N_MICROBATCH) + t.shape[1:])
    return _jnp.moveaxis(t, 1, axis + 1)


def setup_inputs(seed: int = 0) -> dict:
    inp = _fwd_setup_inputs(seed)
    key = _jax.random.fold_in(_jax.random.key(seed), 7919)
    shape, _ = _output_shape()
    out = dict(inp)
    out["loss_target"] = _jax.random.normal(_jax.random.fold_in(key, 0), shape, _jnp.float32)
    for i, name in enumerate(TWIN_WEIGHTS):
        w = inp[name].astype(_jnp.float32)
        if MOMENT_SCALE is None:
            s = _jnp.sqrt(_jnp.mean(_jnp.square(w)) + 1e-30)
        else:
            s = MOMENT_SCALE[name]
        km, kv = _jax.random.split(_jax.random.fold_in(key, i + 1))
        out[name] = w
        out["m_" + name] = s * _jax.random.normal(km, w.shape, _jnp.float32)
        out["v_" + name] = (s * s) * _jax.random.uniform(kv, w.shape, _jnp.float32, 0.5, 1.5)
    if N_MICROBATCH > 1:
        for name, axis in PER_EXAMPLE_BATCH_AXIS.items():
            out[name] = _to_microbatches(out[name], axis)
    return {'x': out['x'], 'g_mix': out['g_mix'], 'w_in': out['w_in'], 'g_q': out['g_q'], 'w_uq': out['w_uq'], 'g_kv': out['g_kv'], 'w_ukv': out['w_ukv'], 'a_re': out['a_re'], 'a_im': out['a_im'], 'b_re': out['b_re'], 'b_im': out['b_im'], 'c_re': out['c_re'], 'c_im': out['c_im'], 'd_skip': out['d_skip'], 'log_dt': out['log_dt'], 'w_glu': out['w_glu'], 'b_glu': out['b_glu'], 'g_out_mla': out['g_out_mla'], 'g_out_ssm': out['g_out_ssm'], 'g_out_dil': out['g_out_dil'], 'w_o': out['w_o'], 'g_ffn': out['g_ffn'], 'w_gate': out['w_gate'], 'w_up': out['w_up'], 'w_down': out['w_down'], 'g_final': out['g_final'], 'loss_target': out['loss_target'], 'm_g_mix': out['m_g_mix'], 'm_w_in': out['m_w_in'], 'm_g_q': out['m_g_q'], 'm_w_uq': out['m_w_uq'], 'm_g_kv': out['m_g_kv'], 'm_w_ukv': out['m_w_ukv'], 'm_a_re': out['m_a_re'], 'm_a_im': out['m_a_im'], 'm_b_re': out['m_b_re'], 'm_b_im': out['m_b_im'], 'm_c_re': out['m_c_re'], 'm_c_im': out['m_c_im'], 'm_d_skip': out['m_d_skip'], 'm_log_dt': out['m_log_dt'], 'm_w_glu': out['m_w_glu'], 'm_b_glu': out['m_b_glu'], 'm_g_out_mla': out['m_g_out_mla'], 'm_g_out_ssm': out['m_g_out_ssm'], 'm_g_out_dil': out['m_g_out_dil'], 'm_w_o': out['m_w_o'], 'm_g_ffn': out['m_g_ffn'], 'm_w_gate': out['m_w_gate'], 'm_w_up': out['m_w_up'], 'm_w_down': out['m_w_down'], 'm_g_final': out['m_g_final'], 'v_g_mix': out['v_g_mix'], 'v_w_in': out['v_w_in'], 'v_g_q': out['v_g_q'], 'v_w_uq': out['v_w_uq'], 'v_g_kv': out['v_g_kv'], 'v_w_ukv': out['v_w_ukv'], 'v_a_re': out['v_a_re'], 'v_a_im': out['v_a_im'], 'v_b_re': out['v_b_re'], 'v_b_im': out['v_b_im'], 'v_c_re': out['v_c_re'], 'v_c_im': out['v_c_im'], 'v_d_skip': out['v_d_skip'], 'v_log_dt': out['v_log_dt'], 'v_w_glu': out['v_w_glu'], 'v_b_glu': out['v_b_glu'], 'v_g_out_mla': out['v_g_out_mla'], 'v_g_out_ssm': out['v_g_out_ssm'], 'v_g_out_dil': out['v_g_out_dil'], 'v_w_o': out['v_w_o'], 'v_g_ffn': out['v_g_ffn'], 'v_w_gate': out['v_w_gate'], 'v_w_up': out['v_w_up'], 'v_w_down': out['v_w_down'], 'v_g_final': out['v_g_final']}


def _loss(weights, diff, rest, loss_target):
    with _jax.named_scope("forward"):
        args = {**rest, TWIN_DIFF_INPUT: diff, **{k: w.astype(_WEIGHT_DTYPES[k]) for k, w in weights.items()}}
        y = _forward(args)
    with _jax.named_scope("loss_head"):
        err = _jnp.square(y.astype(_jnp.float32) - loss_target)
        return 0.5 * _jnp.sum(_jnp.mean(err, axis=-1)) if err.ndim else 0.5 * err


def _adamw(w, g, m, v):
    m = ADAM_B1 * m + (1.0 - ADAM_B1) * g
    v = ADAM_B2 * v + (1.0 - ADAM_B2) * _jnp.square(g)
    m_hat = m / (1.0 - ADAM_B1 ** ADAM_STEP)
    v_hat = v / (1.0 - ADAM_B2 ** ADAM_STEP)
    delta = -ADAM_LR * (m_hat / (_jnp.sqrt(v_hat) + ADAM_EPS) + ADAM_WD * w)
    return delta, m, v


def reference(x, g_mix, w_in, g_q, w_uq, g_kv, w_ukv, a_re, a_im, b_re, b_im, c_re, c_im, d_skip, log_dt, w_glu, b_glu, g_out_mla, g_out_ssm, g_out_dil, w_o, g_ffn, w_gate, w_up, w_down, g_final, loss_target, m_g_mix, m_w_in, m_g_q, m_w_uq, m_g_kv, m_w_ukv, m_a_re, m_a_im, m_b_re, m_b_im, m_c_re, m_c_im, m_d_skip, m_log_dt, m_w_glu, m_b_glu, m_g_out_mla, m_g_out_ssm, m_g_out_dil, m_w_o, m_g_ffn, m_w_gate, m_w_up, m_w_down, m_g_final, v_g_mix, v_w_in, v_g_q, v_w_uq, v_g_kv, v_w_ukv, v_a_re, v_a_im, v_b_re, v_b_im, v_c_re, v_c_im, v_d_skip, v_log_dt, v_w_glu, v_b_glu, v_g_out_mla, v_g_out_ssm, v_g_out_dil, v_w_o, v_g_ffn, v_w_gate, v_w_up, v_w_down, v_g_final):
    given = dict(x=x, g_mix=g_mix, w_in=w_in, g_q=g_q, w_uq=w_uq, g_kv=g_kv, w_ukv=w_ukv, a_re=a_re, a_im=a_im, b_re=b_re, b_im=b_im, c_re=c_re, c_im=c_im, d_skip=d_skip, log_dt=log_dt, w_glu=w_glu, b_glu=b_glu, g_out_mla=g_out_mla, g_out_ssm=g_out_ssm, g_out_dil=g_out_dil, w_o=w_o, g_ffn=g_ffn, w_gate=w_gate, w_up=w_up, w_down=w_down, g_final=g_final, loss_target=loss_target, m_g_mix=m_g_mix, m_w_in=m_w_in, m_g_q=m_g_q, m_w_uq=m_w_uq, m_g_kv=m_g_kv, m_w_ukv=m_w_ukv, m_a_re=m_a_re, m_a_im=m_a_im, m_b_re=m_b_re, m_b_im=m_b_im, m_c_re=m_c_re, m_c_im=m_c_im, m_d_skip=m_d_skip, m_log_dt=m_log_dt, m_w_glu=m_w_glu, m_b_glu=m_b_glu, m_g_out_mla=m_g_out_mla, m_g_out_ssm=m_g_out_ssm, m_g_out_dil=m_g_out_dil, m_w_o=m_w_o, m_g_ffn=m_g_ffn, m_w_gate=m_w_gate, m_w_up=m_w_up, m_w_down=m_w_down, m_g_final=m_g_final, v_g_mix=v_g_mix, v_w_in=v_w_in, v_g_q=v_g_q, v_w_uq=v_w_uq, v_g_kv=v_g_kv, v_w_ukv=v_w_ukv, v_a_re=v_a_re, v_a_im=v_a_im, v_b_re=v_b_re, v_b_im=v_b_im, v_c_re=v_c_re, v_c_im=v_c_im, v_d_skip=v_d_skip, v_log_dt=v_log_dt, v_w_glu=v_w_glu, v_b_glu=v_b_glu, v_g_out_mla=v_g_out_mla, v_g_out_ssm=v_g_out_ssm, v_g_out_dil=v_g_out_dil, v_w_o=v_w_o, v_g_ffn=v_g_ffn, v_w_gate=v_w_gate, v_w_up=v_w_up, v_w_down=v_w_down, v_g_final=v_g_final)
    weights = {n: given[n] for n in TWIN_WEIGHTS}
    shared = {n: given[n] for n in SHARED_INPUTS}
    per_example = {n: given[n] for n in ['x']}
    grad_fn = _jax.value_and_grad(_loss, argnums=(0, 1))

    def one_microbatch(ex, loss_target):
        ex = dict(ex)
        diff = ex.pop(TWIN_DIFF_INPUT)
        return grad_fn(weights, diff, {**shared, **ex}, loss_target)

    if N_MICROBATCH == 1:
        loss, (grad_w, grad_x) = one_microbatch(per_example, given["loss_target"])
    else:
        def body(carry, xs):
            loss_sum, grad_sum = carry
            l_k, (gw_k, gx_k) = one_microbatch(xs[0], xs[1])
            with _jax.named_scope("update"):
                return (loss_sum + l_k, _jax.tree.map(_jnp.add, grad_sum, gw_k)), gx_k

        init = (_jnp.zeros((), _jnp.float32), _jax.tree.map(_jnp.zeros_like, weights))
        (loss, grad_w), grad_x = _jax.lax.scan(body, init, (per_example, given["loss_target"]))
    with _jax.named_scope("update"):
        delta_w, new_m, new_v = {}, {}, {}
        for n in TWIN_WEIGHTS:
            delta_w[n], new_m[n], new_v[n] = _adamw(weights[n], grad_w[n], given["m_" + n], given["v_" + n])
    return (loss, grad_x, *[grad_w[n] for n in TWIN_WEIGHTS], *[delta_w[n] for n in TWIN_WEIGHTS],
            *[new_m[n] for n in TWIN_WEIGHTS], *[new_v[n] for n in TWIN_WEIGHTS])
```

```python
import functools
import math

import numpy as np
import jax
import jax.numpy as jnp
from jax import lax
from jax.experimental import pallas as pl
from jax.experimental.pallas import tpu as pltpu

F32 = jnp.float32
BF16 = jnp.bfloat16
MESH = pl.DeviceIdType.MESH

NORM_EPS = 1e-6
MLA_NOPE, MLA_ROPE, MLA_V = 128, 64, 128
SSM_GROUP, SSM_STATE = 16, 64
DIL_HEAD = 64
DIL_PATTERNS = ((128, 1), (512, 4), (2048, 16))
ROPE_THETA = 10000.0
ADAM_LR, ADAM_B1, ADAM_B2, ADAM_EPS, ADAM_WD, ADAM_STEP = 0.001, 0.9, 0.999, 1e-08, 0.01, 10
N_CHIPS = 4

LANES = 128
SUBLANES_BF16 = 16
VMEM_LIMIT = 48 * 1024 * 1024
ROW_BUDGET = 20 * 1024 * 1024
NEG = -1e30


def _cparams(sem=None):
    return pltpu.CompilerParams(dimension_semantics=sem, vmem_limit_bytes=VMEM_LIMIT)


def _pick(n, cap, q, off=0):
    best = None
    for d in range(q, min(n, cap) + 1, q):
        if n % d == 0 and off % d == 0:
            best = d
    if best is None or (best * 4 <= min(cap, n) and n <= 3072 and off % n == 0):
        assert off % n == 0, (n, off)
        return n
    return best


_DOT_DIMS = {"nn": (((1,), (0,)), ((), ())), "nt": (((1,), (1,)), ((), ())), "tn": (((0,), (0,)), ((), ()))}


def matmul(name, a, b, mode, *, M, N, K, a_off=(0, 0), b_off=(0, 0), a_lead=None, b_lead=None,
           add=None, out_dtype=F32, tm_cap=1024, tn_cap=1024, tk_cap=512):
    if mode == "tn":
        a_row_q, (a_ro, a_co) = 128, (a_off[1], a_off[0])
    else:
        a_row_q, (a_ro, a_co) = 8, a_off
    tm = _pick(M, tm_cap, a_row_q, a_ro)
    if mode == "nt":
        b_no, b_ko = b_off
    else:
        b_ko, b_no = b_off
    tn = _pick(N, tn_cap, 128, b_no)
    if tn > tn_cap:
        tm = _pick(M, tm_cap // 2, a_row_q, a_ro)
    tk = _pick(K, tk_cap, 128, math.gcd(a_co, b_ko) if (a_co or b_ko) else 0)
    assert a_co % tk == 0 and b_ko % tk == 0
    nk = K // tk
    dn = _DOT_DIMS[mode]

    def lead(spec_shape, lead_idx, imap):
        if lead_idx is None:
            return pl.BlockSpec(spec_shape, imap)
        return pl.BlockSpec((None,) + spec_shape, lambda i, j, k: (lead_idx,) + imap(i, j, k))

    if mode == "tn":
        a_spec = lead((tk, tm), a_lead, lambda i, j, k: (k + a_co // tk, i + a_ro // tm))
    else:
        a_spec = lead((tm, tk), a_lead, lambda i, j, k: (i + a_ro // tm, k + a_co // tk))
    if mode == "nt":
        b_spec = lead((tn, tk), b_lead, lambda i, j, k: (j + b_no // tn, k + b_ko // tk))
    else:
        b_spec = lead((tk, tn), b_lead, lambda i, j, k: (k + b_ko // tk, j + b_no // tn))
    o_spec = pl.BlockSpec((tm, tn), lambda i, j, k: (i, j))
    has_add = add is not None

    def body(*refs):
        a_ref, b_ref = refs[0], refs[1]
        add_ref = refs[2] if has_add else None
        o_ref = refs[3] if has_add else refs[2]
        acc_ref = refs[-1]
        k = pl.program_id(2)
        part = lax.dot_general(a_ref[...].astype(BF16), b_ref[...].astype(BF16), dn, preferred_element_type=F32)

        @pl.when(k == 0)
        def _():
            acc_ref[...] = part

        @pl.when(k > 0)
        def _():
            acc_ref[...] += part

        @pl.when(k == nk - 1)
        def _():
            r = acc_ref[...]
            if has_add:
                r = r + add_ref[...].astype(F32)
            o_ref[...] = r.astype(o_ref.dtype)

    in_specs = [a_spec, b_spec] + ([o_spec] if has_add else [])
    args = (a, b) + ((add,) if has_add else ())
    return pl.pallas_call(
        body, name=name, out_shape=jax.ShapeDtypeStruct((M, N), out_dtype),
        grid=(M // tm, N // tn, nk), in_specs=in_specs, out_specs=o_spec,
        scratch_shapes=[pltpu.VMEM((tm, tn), F32)],
        compiler_params=_cparams(("parallel", "parallel", "arbitrary")),
    )(*args)


def rowwise(name, fn, rows, vecs, outs, sums=(), *, M):
    rows = [tuple(r) + (0,) * (4 - len(r)) for r in rows]
    nr, nv, no, ns = len(rows), len(vecs), len(outs), len(sums)
    per_row = sum(w * a.dtype.itemsize for a, w, _, _ in rows) + sum(w * jnp.dtype(d).itemsize for w, d in outs)
    tr = _pick(M, max(8, min(512, ROW_BUDGET // (2 * per_row))), 16 if M % 16 == 0 else 8)

    def body(*refs):
        i = pl.program_id(0)
        res = fn(*[r[...] for r in refs[:nr + nv]])
        o_refs = refs[nr + nv:nr + nv + no]
        s_refs = refs[nr + nv + no:]
        for ref, val in zip(o_refs, res[:no]):
            ref[...] = val.astype(ref.dtype)
        if ns:
            @pl.when(i == 0)
            def _():
                for ref in s_refs:
                    ref[...] = jnp.zeros(ref.shape, F32)

            for ref, val in zip(s_refs, res[no:]):
                ref[...] += val

    in_specs = [pl.BlockSpec((tr, w), functools.partial(lambda i, cb, rb: (i + rb, cb), cb=off // w, rb=roff // tr)) for _, w, off, roff in rows]
    for _, w, off, roff in rows:
        assert off % w == 0 and roff % tr == 0
    in_specs += [pl.BlockSpec(v.shape, functools.partial(lambda i, nd: (0,) * nd, nd=v.ndim)) for v in vecs]
    out_specs = [pl.BlockSpec((tr, w), lambda i: (i, 0)) for w, _ in outs]
    out_specs += [pl.BlockSpec((1, w), lambda i: (0, 0)) for w in sums]
    out_shape = [jax.ShapeDtypeStruct((M, w), d) for w, d in outs] + [jax.ShapeDtypeStruct((1, w), F32) for w in sums]
    return pl.pallas_call(
        body, name=name, out_shape=out_shape, grid=(M // tr,), in_specs=in_specs, out_specs=out_specs,
        compiler_params=_cparams(("arbitrary",) if ns else ("parallel",)),
    )(*[r[0] for r in rows], *vecs)


def _rms(x, g):
    xf = x.astype(F32)
    return xf * lax.rsqrt(jnp.mean(xf * xf, axis=-1, keepdims=True) + NORM_EPS) * g


def _gelu(y):
    return 0.5 * y * (1.0 + jnp.tanh(math.sqrt(2.0 / math.pi) * (y + 0.044715 * (y * y * y))))


def _colsum(v):
    return jnp.sum(v, axis=0, keepdims=True)


def rms_fwd(name, x, width, off, g, *, M):
    return rowwise(name, lambda xb, gb: (_rms(xb, gb),), [(x, width, off)], [g], [(width, BF16)], M=M)[0]


def rms_bwd(name, x, width, off, g, dy, resid=None, *, M, out_dtype=F32):
    def fn(xb, dyb, *rest):
        gb = rest[-1]
        _, vjp = jax.vjp(_rms, xb.astype(F32), gb)
        dx, dg = vjp(dyb.astype(F32))
        if resid is not None:
            dx = dx + rest[0]
        return dx, dg

    rows = [(x, width, off), (dy, width, 0)] + ([(resid, width, 0)] if resid is not None else [])
    return rowwise(name, fn, rows, [g], [(width, out_dtype)], [width], M=M)


def _swap_halves(x, half):
    w = x.shape[-1]
    lane = lax.broadcasted_iota(jnp.int32, x.shape, x.ndim - 1)
    first = (lane % (2 * half)) < half
    return jnp.where(first, pltpu.roll(x, w - half, x.ndim - 1), pltpu.roll(x, half, x.ndim - 1))


def _attn_weight(kind, delta):
    if kind == "causal":
        return (delta >= 0).astype(F32)
    w = jnp.zeros(delta.shape, F32)
    for window, dil in DIL_PATTERNS:
        ok = (delta >= 0) & (delta <= window)
        if dil > 1:
            ok = ok & ((delta & (dil - 1)) == 0)
        w = w + ok.astype(F32)
    return w


def _dot(a, b, mode):
    return lax.dot_general(a, b, _DOT_DIMS[mode], preferred_element_type=F32)


def attention_fwd(name, qa, qa_off, ka, ka_off, v, v_off, *, da, dv, pairs, scale, kind, M, qb=None, qb_off=0, kb=None):
    tq = min(256, M)
    tk = min(512, M)
    has_b = qb is not None
    dr = MLA_ROPE

    def body(*refs):
        if has_b:
            qa_ref, ka_ref, v_ref, qb_ref, kb_ref, o_ref, lse_ref = refs
        else:
            qa_ref, ka_ref, v_ref, o_ref, lse_ref = refs
        i = pl.program_id(1)
        t0 = i * tq
        nkb = (t0 + tq + tk - 1) // tk
        o_parts, lse_parts = [], []
        for hh in range(2):
            q1 = qa_ref[:, hh * da:(hh + 1) * da].astype(BF16)
            q2 = qb_ref[:, hh * dr:(hh + 1) * dr].astype(BF16) if has_b else None

            def step(kbi, carry, hh=hh, q1=q1, q2=q2):
                m, l, acc = carry
                ks = pl.multiple_of(kbi * tk, tk)
                k1 = ka_ref[pl.ds(ks, tk), hh * da:(hh + 1) * da].astype(BF16)
                s = _dot(q1, k1, "nt")
                if has_b:
                    s = s + _dot(q2, kb_ref[pl.ds(ks, tk), 0:dr].astype(BF16), "nt")
                s = s * scale
                delta = (t0 + lax.broadcasted_iota(jnp.int32, (tq, tk), 0)) - (ks + lax.broadcasted_iota(jnp.int32, (tq, tk), 1))
                w = _attn_weight(kind, delta)
                s = jnp.where(w > 0, s, NEG)
                m_new = jnp.maximum(m, jnp.max(s, axis=1, keepdims=True))
                alpha = jnp.exp(m - m_new)
                p = w * jnp.exp(s - m_new)
                l = alpha * l + jnp.sum(p, axis=1, keepdims=True)
                vv = v_ref[pl.ds(ks, tk), hh * dv:(hh + 1) * dv].astype(BF16)
                acc = alpha * acc + _dot(p.astype(BF16), vv, "nn")
                return m_new, l, acc

            m, l, acc = lax.fori_loop(0, nkb, step, (jnp.full((tq, 1), NEG, F32), jnp.zeros((tq, 1), F32), jnp.zeros((tq, dv), F32)))
            o_parts.append(acc / l)
            lse_parts.append(m + jnp.log(l))
        o_ref[...] = jnp.concatenate(o_parts, axis=1)
        lane = lax.broadcasted_iota(jnp.int32, (tq, LANES), 1)
        lse_ref[...] = jnp.where(lane == 0, lse_parts[0], jnp.where(lane == 1, lse_parts[1], 0.0))

    assert qa_off % (2 * da) == 0 and ka_off % (2 * da) == 0 and v_off % (2 * dv) == 0
    in_specs = [
        pl.BlockSpec((tq, 2 * da), lambda hp, i: (i, qa_off // (2 * da) + hp)),
        pl.BlockSpec((M, 2 * da), lambda hp, i: (0, ka_off // (2 * da) + hp)),
        pl.BlockSpec((M, 2 * dv), lambda hp, i: (0, v_off // (2 * dv) + hp)),
    ]
    args = [qa, ka, v]
    if has_b:
        assert qb_off % LANES == 0
        in_specs += [pl.BlockSpec((tq, LANES), lambda hp, i: (i, qb_off // LANES + hp)),
                     pl.BlockSpec((M, LANES), lambda hp, i: (0, 0))]
        args += [qb, kb]
    out_specs = [pl.BlockSpec((tq, 2 * dv), lambda hp, i: (i, hp)),
                 pl.BlockSpec((None, tq, LANES), lambda hp, i: (hp, i, 0))]
    out_shape = [jax.ShapeDtypeStruct((M, pairs * 2 * dv), F32), jax.ShapeDtypeStruct((pairs, M, LANES), F32)]
    return pl.pallas_call(
        body, name=name, out_shape=out_shape, grid=(pairs, M // tq), in_specs=in_specs, out_specs=out_specs,
        compiler_params=_cparams(("parallel", "arbitrary")),
    )(*args)


def attention_bwd(name, qa, qa_off, ka, ka_off, v, v_off, o, do, lse, *, da, dv, pairs, scale, kind, M,
                  qb=None, qb_off=0, kb=None):
    tq = min(256, M)
    tk = min(256, M)
    has_b = qb is not None
    dr = MLA_ROPE

    def body(*refs):
        if has_b:
            qa_ref, ka_ref, v_ref, o_ref, do_ref, lse_ref, qb_ref, kb_ref, dqa_ref, dka_ref, dv_ref, dqb_ref, dkb_ref = refs
        else:
            qa_ref, ka_ref, v_ref, o_ref, do_ref, lse_ref, dqa_ref, dka_ref, dv_ref = refs
        hp = pl.program_id(0)
        i = pl.program_id(1)
        t0 = i * tq
        nkb = (t0 + tq + tk - 1) // tk

        @pl.when(i == 0)
        def _():
            dka_ref[...] = jnp.zeros(dka_ref.shape, F32)
            dv_ref[...] = jnp.zeros(dv_ref.shape, F32)

        if has_b:
            @pl.when((i == 0) & (hp == 0))
            def _():
                dkb_ref[...] = jnp.zeros(dkb_ref.shape, F32)

        dq1_parts, dq2_parts = [], []
        for hh in range(2):
            q1 = qa_ref[:, hh * da:(hh + 1) * da].astype(BF16)
            q2 = qb_ref[:, hh * dr:(hh + 1) * dr].astype(BF16) if has_b else None
            do_h = do_ref[:, hh * dv:(hh + 1) * dv]
            o_h = o_ref[:, hh * dv:(hh + 1) * dv]
            rowdot = jnp.sum(do_h * o_h, axis=1, keepdims=True)
            do_bf = do_h.astype(BF16)
            lse_h = lse_ref[:, hh:hh + 1]

            def step(kbi, carry, hh=hh, q1=q1, q2=q2, do_bf=do_bf, rowdot=rowdot, lse_h=lse_h):
                dq1, dq2 = carry
                ks = pl.multiple_of(kbi * tk, tk)
                k1 = ka_ref[pl.ds(ks, tk), hh * da:(hh + 1) * da].astype(BF16)
                s = _dot(q1, k1, "nt")
                if has_b:
                    k2 = kb_ref[pl.ds(ks, tk), 0:dr].astype(BF16)
                    s = s + _dot(q2, k2, "nt")
                s = s * scale
                delta = (t0 + lax.broadcasted_iota(jnp.int32, (tq, tk), 0)) - (ks + lax.broadcasted_iota(jnp.int32, (tq, tk), 1))
                w = _attn_weight(kind, delta)
                p = w * jnp.exp(jnp.where(w > 0, s, NEG) - lse_h)
                vv = v_ref[pl.ds(ks, tk), hh * dv:(hh + 1) * dv].astype(BF16)
                dp = _dot(do_bf, vv, "nt")
                ds = (p * (dp - rowdot) * scale).astype(BF16)
                dq1 = dq1 + _dot(ds, k1, "nn")
                dka_ref[pl.ds(ks, tk), hh * da:(hh + 1) * da] += _dot(ds, q1, "tn")
                dv_ref[pl.ds(ks, tk), hh * dv:(hh + 1) * dv] += _dot(p.astype(BF16), do_bf, "tn")
                if has_b:
                    dq2 = dq2 + _dot(ds, k2, "nn")
                    dkb_ref[pl.ds(ks, tk), 0:dr] += _dot(ds, q2, "tn")
                return dq1, dq2

            dq1, dq2 = lax.fori_loop(0, nkb, step, (jnp.zeros((tq, da), F32), jnp.zeros((tq, dr), F32)))
            dq1_parts.append(dq1)
            dq2_parts.append(dq2)
        dqa_ref[...] = jnp.concatenate(dq1_parts, axis=1).astype(dqa_ref.dtype)
        if has_b:
            dqb_ref[...] = jnp.concatenate(dq2_parts, axis=1).astype(dqb_ref.dtype)

    in_specs = [
        pl.BlockSpec((tq, 2 * da), lambda hp, i: (i, qa_off // (2 * da) + hp)),
        pl.BlockSpec((M, 2 * da), lambda hp, i: (0, ka_off // (2 * da) + hp)),
        pl.BlockSpec((M, 2 * dv), lambda hp, i: (0, v_off // (2 * dv) + hp)),
        pl.BlockSpec((tq, 2 * dv), lambda hp, i: (i, hp)),
        pl.BlockSpec((tq, 2 * dv), lambda hp, i: (i, hp)),
        pl.BlockSpec((None, tq, LANES), lambda hp, i: (hp, i, 0)),
    ]
    args = [qa, ka, v, o, do, lse]
    out_specs = [pl.BlockSpec((tq, 2 * da), lambda hp, i: (i, hp)),
                 pl.BlockSpec((M, 2 * da), lambda hp, i: (0, hp)),
                 pl.BlockSpec((M, 2 * dv), lambda hp, i: (0, hp))]
    out_shape = [jax.ShapeDtypeStruct((M, pairs * 2 * da), BF16),
                 jax.ShapeDtypeStruct((M, pairs * 2 * da), F32),
                 jax.ShapeDtypeStruct((M, pairs * 2 * dv), F32)]
    if has_b:
        in_specs += [pl.BlockSpec((tq, LANES), lambda hp, i: (i, qb_off // LANES + hp)),
                     pl.BlockSpec((M, LANES), lambda hp, i: (0, 0))]
        args += [qb, kb]
        out_specs += [pl.BlockSpec((tq, LANES), lambda hp, i: (i, hp)), pl.BlockSpec((M, LANES), lambda hp, i: (0, 0))]
        out_shape += [jax.ShapeDtypeStruct((M, pairs * LANES), F32), jax.ShapeDtypeStruct((M, LANES), F32)]
    return pl.pallas_call(
        body, name=name, out_shape=out_shape, grid=(pairs, M // tq), in_specs=in_specs, out_specs=out_specs,
        compiler_params=_cparams(("arbitrary", "arbitrary")),
    )(*args)


def ssm_scan(name, xcat, acat, *, M, reverse=False, hcat=None):
    C2 = xcat.shape[1]
    cb = LANES
    tb = min(128, M)
    nblk = M // tb
    with_da = hcat is not None

    def body(*refs):
        if with_da:
            x_ref, a_ref, h_ref, o_ref, da_ref, p_ref = refs
        else:
            x_ref, a_ref, o_ref, p_ref = refs
        ar, ai = a_ref[:, :cb], a_ref[:, cb:]
        row = lax.broadcasted_iota(jnp.int32, (tb, cb), 0)

        def logscan(xr, xi):
            pr, pi = ar, ai
            d = 1
            while d < tb:
                shift = tb - d if reverse else d
                keep = (row < tb - d) if reverse else (row >= d)
                sr = jnp.where(keep, pltpu.roll(xr, shift, 0), 0.0)
                si = jnp.where(keep, pltpu.roll(xi, shift, 0), 0.0)
                xr, xi = xr + pr * sr - pi * si, xi + pr * si + pi * sr
                pr, pi = pr * pr - pi * pi, 2.0 * pr * pi
                d *= 2
            return xr, xi

        seed = row == (tb - 1 if reverse else 0)
        p0r, p0i = logscan(jnp.where(seed, ar, 0.0), jnp.where(seed, ai, 0.0))
        p_ref[:, :cb] = p0r
        p_ref[:, cb:] = p0i
        sub = lax.broadcasted_iota(jnp.int32, (8, cb), 0)
        edge = 0 if reverse else tb - 8
        pick = sub == (0 if reverse else 7)

        def blk(b, carry):
            cr, ci = carry
            bb = (nblk - 1 - b) if reverse else b
            t0 = pl.multiple_of(bb * tb, tb)
            hr, hi = logscan(x_ref[pl.ds(t0, tb), :cb], x_ref[pl.ds(t0, tb), cb:])
            pr, pi = p_ref[:, :cb], p_ref[:, cb:]
            o_ref[pl.ds(t0, tb), :cb] = hr + pr * cr - pi * ci
            o_ref[pl.ds(t0, tb), cb:] = hi + pr * ci + pi * cr
            te = pl.multiple_of(t0 + edge, 8)
            ncr = jnp.sum(jnp.where(pick, o_ref[pl.ds(te, 8), :cb], 0.0), axis=0, keepdims=True)
            nci = jnp.sum(jnp.where(pick, o_ref[pl.ds(te, 8), cb:], 0.0), axis=0, keepdims=True)
            return ncr, nci

        lax.fori_loop(0, nblk, blk, (jnp.zeros((1, cb), F32), jnp.zeros((1, cb), F32)))
        if with_da:
            first = lax.broadcasted_iota(jnp.int32, (M, cb), 0) >= 1
            hpr = jnp.where(first, pltpu.roll(h_ref[:, :cb], 1, 0), 0.0)
            hpi = jnp.where(first, pltpu.roll(h_ref[:, cb:], 1, 0), 0.0)
            lr, li = o_ref[:, :cb], o_ref[:, cb:]
            da_ref[:, :cb] = _colsum(lr * hpr + li * hpi)
            da_ref[:, cb:] = _colsum(li * hpr - lr * hpi)

    blk_spec = pl.BlockSpec((M, 2 * cb), lambda j: (0, j))
    vec_spec = pl.BlockSpec((1, 2 * cb), lambda j: (0, j))
    in_specs = [blk_spec, vec_spec] + ([blk_spec] if with_da else [])
    out_specs = [blk_spec] + ([vec_spec] if with_da else [])
    out_shape = [jax.ShapeDtypeStruct((M, C2), F32)] + ([jax.ShapeDtypeStruct((1, C2), F32)] if with_da else [])
    args = [xcat, acat] + ([hcat] if with_da else [])
    res = pl.pallas_call(
        body, name=name, out_shape=out_shape, grid=(C2 // (2 * cb),), in_specs=in_specs, out_specs=out_specs,
        scratch_shapes=[pltpu.VMEM((tb, 2 * cb), F32)], compiler_params=_cparams(("parallel",)),
    )(*args)
    return res if with_da else res[0]


def _ssm_param_fn(a_re, a_im, ldt, b_re, b_im):
    lr, li = jnp.minimum(a_re, -1e-4), a_im
    dt = jnp.exp(ldt)
    e, ang = jnp.exp(lr * dt), li * dt
    ar, ai = e * jnp.cos(ang), e * jnp.sin(ang)
    den = lr * lr + li * li
    nr, ni = ar - 1.0, ai
    cr, ci = (nr * lr + ni * li) / den, (ni * lr - nr * li) / den
    return ar, ai, cr * b_re - ci * b_im, cr * b_im + ci * b_re


def _whole(shape):
    return pl.BlockSpec(shape, functools.partial(lambda nd: (0,) * nd, nd=len(shape)))


def ssm_param_fwd(name, a_re, a_im, ldt, b_re, b_im):
    def body(*refs):
        res = _ssm_param_fn(*[r[...] for r in refs[:5]])
        for ref, val in zip(refs[5:], res):
            ref[...] = val

    ins = [a_re, a_im, ldt, b_re, b_im]
    outs = [a_re, a_re, b_re, b_re]
    return pl.pallas_call(
        body, name=name, out_shape=[jax.ShapeDtypeStruct(t.shape, F32) for t in outs],
        in_specs=[_whole(t.shape) for t in ins], out_specs=[_whole(t.shape) for t in outs], compiler_params=_cparams(),
    )(*ins)


def ssm_param_bwd(name, a_re, a_im, ldt, b_re, b_im, d_ar, d_ai, d_bbr, d_bbi):
    def body(*refs):
        _, vjp = jax.vjp(_ssm_param_fn, *[r[...] for r in refs[:5]])
        res = vjp(tuple(r[...] for r in refs[5:9]))
        for ref, val in zip(refs[9:], res):
            ref[...] = val

    ins = [a_re, a_im, ldt, b_re, b_im, d_ar, d_ai, d_bbr, d_bbi]
    outs = [a_re, a_im, ldt, b_re, b_im]
    return pl.pallas_call(
        body, name=name, out_shape=[jax.ShapeDtypeStruct(t.shape, F32) for t in outs],
        in_specs=[_whole(t.shape) for t in ins], out_specs=[_whole(t.shape) for t in outs], compiler_params=_cparams(),
    )(*ins)


ANY = pl.BlockSpec(memory_space=pl.ANY)


def _place():
    x, y, c = lax.axis_index("x"), lax.axis_index("y"), lax.axis_index("c")
    chips = [(1 - x, y), (x, 1 - y), (1 - x, 1 - y)]
    return x, y, c, chips


def all_gather_halves(name, w):
    def body(w_ref, out_ref, send_sems, recv_sems, fsend_sems, frecv_sems, local_sem):
        x, y, c, chips = _place()
        me = 2 * x + y
        sib = (x, y, 1 - c)
        local = pltpu.make_async_copy(w_ref, out_ref.at[me], local_sem)
        local.start()

        def ici(k, chip):
            return pltpu.make_async_remote_copy(src_ref=w_ref.at[c], dst_ref=out_ref.at[me, c], send_sem=send_sems.at[k],
                                                recv_sem=recv_sems.at[k], device_id=(chip[0], chip[1], c), device_id_type=MESH)

        def arrival(k, chip, half):
            return out_ref.at[2 * chip[0] + chip[1], half]

        sends = [ici(k, chip) for k, chip in enumerate(chips)]
        for cp in sends:
            cp.start()
        passes = []
        for k, chip in enumerate(chips):
            pltpu.make_async_remote_copy(src_ref=w_ref.at[c], dst_ref=arrival(k, chip, c), send_sem=send_sems.at[k],
                                         recv_sem=recv_sems.at[k], device_id=(chip[0], chip[1], c), device_id_type=MESH).wait_recv()
            cp = pltpu.make_async_remote_copy(src_ref=arrival(k, chip, c), dst_ref=arrival(k, chip, c), send_sem=fsend_sems.at[k],
                                              recv_sem=frecv_sems.at[k], device_id=sib, device_id_type=MESH)
            cp.start()
            passes.append(cp)
        for k, chip in enumerate(chips):
            pltpu.make_async_remote_copy(src_ref=arrival(k, chip, 1 - c), dst_ref=arrival(k, chip, 1 - c), send_sem=fsend_sems.at[k],
                                         recv_sem=frecv_sems.at[k], device_id=sib, device_id_type=MESH).wait_recv()
        for cp in sends + passes:
            cp.wait_send()
        local.wait()

    return pl.pallas_call(
        body, name=name, out_shape=jax.ShapeDtypeStruct((N_CHIPS,) + w.shape, w.dtype), in_specs=[ANY], out_specs=ANY,
        scratch_shapes=[pltpu.SemaphoreType.DMA((3,))] * 4 + [pltpu.SemaphoreType.DMA(())],
    )(w)


def swap_with_sibling(name, src, pick_other_half):
    shape = src.shape[1:] if pick_other_half else src.shape

    def body(src_ref, out_ref, ssem, rsem):
        x, y, c, _ = _place()
        cp = pltpu.make_async_remote_copy(src_ref=src_ref.at[1 - c] if pick_other_half else src_ref, dst_ref=out_ref,
                                          send_sem=ssem, recv_sem=rsem, device_id=(x, y, 1 - c), device_id_type=MESH)
        cp.start()
        cp.wait()

    return pl.pallas_call(
        body, name=name, out_shape=jax.ShapeDtypeStruct(shape, src.dtype), in_specs=[ANY], out_specs=ANY,
        scratch_shapes=[pltpu.SemaphoreType.DMA(()), pltpu.SemaphoreType.DMA(())],
    )(src)


def exchange_chips(name, src, per_chip):
    shape = src.shape[1:] if per_chip else src.shape

    def body(src_ref, out_ref, send_sems, recv_sems):
        x, y, c, chips = _place()
        cps = []
        for k, chip in enumerate(chips):
            s = src_ref.at[2 * chip[0] + chip[1]] if per_chip else src_ref
            cps.append(pltpu.make_async_remote_copy(src_ref=s, dst_ref=out_ref.at[k], send_sem=send_sems.at[k], recv_sem=recv_sems.at[k],
                                                    device_id=(chip[0], chip[1], c), device_id_type=MESH))
        for cp in cps:
            cp.start()
        for cp in cps:
            cp.wait()

    return pl.pallas_call(
        body, name=name, out_shape=jax.ShapeDtypeStruct((3,) + shape, src.dtype), in_specs=[ANY], out_specs=ANY,
        scratch_shapes=[pltpu.SemaphoreType.DMA((3,)), pltpu.SemaphoreType.DMA((3,))],
    )(src)


def share_halves(name, half):
    def body(h_ref, out_ref, ssem, rsem, local_sem):
        x, y, c, _ = _place()
        local = pltpu.make_async_copy(h_ref, out_ref.at[c], local_sem)
        local.start()
        cp = pltpu.make_async_remote_copy(src_ref=h_ref, dst_ref=out_ref.at[c], send_sem=ssem, recv_sem=rsem,
                                          device_id=(x, y, 1 - c), device_id_type=MESH)
        cp.start()
        pltpu.make_async_remote_copy(src_ref=h_ref, dst_ref=out_ref.at[1 - c], send_sem=ssem, recv_sem=rsem,
                                     device_id=(x, y, 1 - c), device_id_type=MESH).wait_recv()
        cp.wait_send()
        local.wait()

    return pl.pallas_call(
        body, name=name, out_shape=jax.ShapeDtypeStruct((2,) + half.shape, half.dtype), in_specs=[ANY], out_specs=ANY,
        scratch_shapes=[pltpu.SemaphoreType.DMA(()), pltpu.SemaphoreType.DMA(()), pltpu.SemaphoreType.DMA(())],
    )(half)


def pair_sum(name, p, got):
    _, _, rh, cw = p.shape
    tr = _pick(rh, max(16, min(512, ROW_BUDGET // (2 * cw * 10))), 16)

    def body(p_ref, got_ref, s_ref, own_ref):
        j = pl.program_id(1)
        tot = p_ref[...].astype(F32) + got_ref[...].astype(F32)
        s_ref[...] = tot.astype(BF16)

        @pl.when(j == 2 * lax.axis_index("x") + lax.axis_index("y"))
        def _():
            own_ref[...] = tot

    return pl.pallas_call(
        body, name=name, grid=(rh // tr, N_CHIPS),
        in_specs=[pl.BlockSpec((None, None, tr, cw), lambda i, j: (lax.axis_index("c"), j, i, 0)),
                  pl.BlockSpec((None, tr, cw), lambda i, j: (j, i, 0))],
        out_specs=[pl.BlockSpec((None, tr, cw), lambda i, j: (j, i, 0)),
                   pl.BlockSpec((tr, cw), lambda i, j: (i, 0))],
        out_shape=[jax.ShapeDtypeStruct((N_CHIPS, rh, cw), BF16), jax.ShapeDtypeStruct((rh, cw), F32)],
        compiler_params=_cparams(("arbitrary", "arbitrary")),
    )(p, got)


def reduce_scatter(tag, p):
    _, _, rh, cw = p.shape
    got = swap_with_sibling(f"rs_pair_{tag}", p, True)
    s, own = pair_sum(f"rs_pairsum_{tag}", p, got)
    parts = exchange_chips(f"rs_chips_{tag}", s, True)
    parts = parts.reshape(3 * rh, cw)
    half = rowwise(f"rs_sum_{tag}", lambda o, a, b, c: (((o + a.astype(F32)) + b.astype(F32)) + c.astype(F32),),
                   [(own, cw, 0), (parts, cw, 0, 0), (parts, cw, 0, rh), (parts, cw, 0, 2 * rh)], [], [(cw, F32)], M=rh)[0]
    return share_halves(f"rs_share_{tag}", half)


def all_reduce_small(buf):
    r = buf.shape[0]
    got = swap_with_sibling("ar_pair", buf, False)
    chip = rowwise("ar_pairsum", lambda a, b: (a + b,), [(buf, LANES, 0), (got, LANES, 0)], [], [(LANES, F32)], M=r)[0]
    parts = exchange_chips("ar_chips", chip, False).reshape(3 * r, LANES)
    return rowwise("ar_sum", lambda o, fx, fy, fxy: ((o + fy) + (fx + fxy),),
                   [(chip, LANES, 0), (parts, LANES, 0, 0), (parts, LANES, 0, r), (parts, LANES, 0, 2 * r)], [], [(LANES, F32)], M=r)[0]


def _adam_fn(w, g, m, v):
    m = ADAM_B1 * m + (1.0 - ADAM_B1) * g
    v = ADAM_B2 * v + (1.0 - ADAM_B2) * (g * g)
    m_hat = m / (1.0 - ADAM_B1 ** ADAM_STEP)
    v_hat = v / (1.0 - ADAM_B2 ** ADAM_STEP)
    return -ADAM_LR * (m_hat / (jnp.sqrt(v_hat) + ADAM_EPS) + ADAM_WD * w), m, v


def adamw(name, w, g, m, v):
    r, cw = w.shape
    return rowwise(name, _adam_fn, [(t, cw, 0) for t in (w, g, m, v)], [], [(cw, F32)] * 3, M=r)


class Dims:
    def __init__(self, x, g_q, g_kv, g_out_mla, g_out_ssm, g_out_dil, ff):
        self.M, self.D = x.shape[-2], x.shape[-1]
        self.QL, self.KVL = g_q.shape[-1], g_kv.shape[-1]
        self.MW, self.SW, self.DW = g_out_mla.shape[-1], g_out_ssm.shape[-1], g_out_dil.shape[-1]
        self.H = self.MW // MLA_V
        self.FF = ff
        self.G = self.SW // SSM_GROUP
        self.C = self.G * SSM_STATE
        self.o_cq, self.o_u = 0, self.QL
        self.o_qd = self.o_u + self.SW
        self.o_kd = self.o_qd + self.DW
        self.o_vd = self.o_kd + self.DW
        self.o_ckv = self.o_vd + self.DW
        self.o_kr = self.o_ckv + self.KVL
        self.PW = -(-(self.o_kr + MLA_ROPE) // LANES) * LANES
        assert self.o_u % self.SW == 0 and self.o_qd % LANES == 0 and self.o_ckv % self.KVL == 0 and self.o_kr % LANES == 0
        assert self.H % 2 == 0 and self.DW % LANES == 0 and self.C % LANES == 0
        self.QW = self.H * (MLA_NOPE + MLA_ROPE)
        self.KVW = self.H * (MLA_NOPE + MLA_V)
        sizes = [self.QL, self.KVL, MLA_ROPE, self.SW, self.DW, self.DW, self.DW]
        starts = np.concatenate([[0], np.cumsum(sizes)[:-1]])
        self.ref_cols = {n: (int(s), int(z)) for n, s, z in zip(["cq", "ckv", "kr", "u", "qd", "kd", "vd"], starts, sizes)}
        self.INW = int(sum(sizes))
        self.new_order = ["cq", "u", "qd", "kd", "vd", "ckv", "kr"]


def _regroup_in(dm, w):
    parts = [w[..., dm.ref_cols[n][0]:dm.ref_cols[n][0] + dm.ref_cols[n][1]] for n in dm.new_order]
    pad = dm.PW - dm.INW
    return jnp.concatenate(parts + [jnp.zeros(w.shape[:-1] + (pad,), w.dtype)], axis=-1)


def _ungroup_in(dm, w):
    off, pieces = 0, {}
    for n in dm.new_order:
        pieces[n] = w[..., off:off + dm.ref_cols[n][1]]
        off += dm.ref_cols[n][1]
    return jnp.concatenate([pieces[n] for n in ["cq", "ckv", "kr", "u", "qd", "kd", "vd"]], axis=-1)


def _split_heads(w, h, d1):
    t = w.reshape(w.shape[:-1] + (h, -1))
    return jnp.concatenate([t[..., :d1].reshape(w.shape[:-1] + (-1,)), t[..., d1:].reshape(w.shape[:-1] + (-1,))], axis=-1)


def _merge_heads(w, h, d1):
    a = w[..., :h * d1].reshape(w.shape[:-1] + (h, d1))
    b = w[..., h * d1:].reshape(w.shape[:-1] + (h, -1))
    return jnp.concatenate([a, b], axis=-1).reshape(w.shape[:-1] + (-1,))


def _cat_cols(re, im):
    r, c = re.shape
    return jnp.stack([re.reshape(r, c // LANES, LANES), im.reshape(r, c // LANES, LANES)], axis=2).reshape(r, 2 * c)


def _uncat_cols(cat):
    r, c2 = cat.shape
    t = cat.reshape(r, c2 // (2 * LANES), 2, LANES)
    return t[:, :, 0].reshape(r, c2 // 2), t[:, :, 1].reshape(r, c2 // 2)


def _block_diag(t, g):
    _, a, b = t.shape
    eye = jnp.eye(g, dtype=bool)[:, None, :, None]
    return jnp.where(eye, t[:, :, None, :], 0).reshape(g * a, g * b)


def _diag_blocks(m, g):
    a, b = m.shape[0] // g, m.shape[1] // g
    eye = jnp.eye(g, dtype=m.dtype)[:, None, :, None]
    return jnp.sum(m.reshape(g, a, g, b) * eye, axis=2)


def _rope_tables(dm):
    half = MLA_ROPE // 2
    inv_freq = ROPE_THETA ** (-jnp.arange(half, dtype=F32) / half)
    ang = jnp.arange(dm.M, dtype=F32)[:, None] * inv_freq[None, :]
    cos = jnp.concatenate([jnp.cos(ang), jnp.cos(ang)], axis=1)
    sin = jnp.concatenate([-jnp.sin(ang), jnp.sin(ang)], axis=1)
    return jnp.tile(cos, (1, dm.H)), jnp.tile(sin, (1, dm.H)), jnp.tile(cos, (1, LANES // MLA_ROPE)), jnp.tile(sin, (1, LANES // MLA_ROPE))


def _rope(x, cos, sin):
    return x * cos + _swap_halves(x, MLA_ROPE // 2) * sin


def _rope_t(d, cos, sin):
    return d * cos + _swap_halves(d * sin, MLA_ROPE // 2)


def _ssm_layer_params(dm, a_re, a_im, log_dt, b_re, b_im):
    flat = lambda t: t.reshape(1, dm.C)
    ldt = jnp.repeat(log_dt, SSM_STATE).reshape(1, dm.C)
    bt = lambda t: jnp.transpose(t, (2, 0, 1)).reshape(SSM_GROUP, dm.C)
    return flat(a_re), flat(a_im), ldt, bt(b_re), bt(b_im)


def layer_forward(dm, l, x, lw, sp, tabs):
    M, D = dm.M, dm.D
    n = lambda s: f"{s}_l{l}"
    sv = {"x_in": x}
    h1 = rms_fwd(n("rms_mix"), x, D, 0, lw["g_mix"], M=M)
    proj = matmul(n("in_proj"), h1, lw["w_in"], "nn", b_lead=l, M=M, N=dm.PW, K=D)
    sv.update(h1=h1, proj=proj)
    cqn = rms_fwd(n("rms_q"), proj, dm.QL, dm.o_cq, lw["g_q"], M=M)
    q = matmul(n("q_up"), cqn, lw["w_uq"], "nn", b_lead=l, M=M, N=dm.QW, K=dm.QL)
    ckvn = rms_fwd(n("rms_kv"), proj, dm.KVL, dm.o_ckv, lw["g_kv"], M=M)
    kv = matmul(n("kv_up"), ckvn, lw["w_ukv"], "nn", b_lead=l, M=M, N=dm.KVW, K=dm.KVL, out_dtype=BF16)
    cosq, sinq, cosk, sink = tabs
    nw = dm.H * MLA_NOPE

    def rope_fn(qb, kb, cq, sq, ck, sk):
        return jnp.concatenate([qb[:, :nw], _rope(qb[:, nw:], cq, sq)], axis=1), _rope(kb, ck, sk)

    pw = dm.H * MLA_ROPE
    q_bf, kpe = rowwise(n("rope"), rope_fn, [(q, dm.QW, 0), (proj, LANES, dm.o_kr), (cosq, pw, 0), (sinq, pw, 0), (cosk, LANES, 0), (sink, LANES, 0)],
                        [], [(dm.QW, BF16), (LANES, BF16)], M=M)
    mla_scale = (MLA_NOPE + MLA_ROPE) ** -0.5
    o_mla, lse_mla = attention_fwd(n("mla_fwd"), q_bf, 0, kv, 0, kv, nw, da=MLA_NOPE, dv=MLA_V, pairs=dm.H // 2, scale=mla_scale,
                                   kind="causal", M=M, qb=q_bf, qb_off=nw, kb=kpe)
    sv.update(cqn=cqn, ckvn=ckvn, q_bf=q_bf, kv=kv, kpe=kpe, o_mla=o_mla, lse_mla=lse_mla)
    bu = matmul(n("ssm_bu"), proj, sp["bcat"], "nn", M=M, N=2 * dm.C, K=dm.SW, a_off=(0, dm.o_u))
    hcat = ssm_scan(n("ssm_scan"), bu, sp["acat"], M=M)
    ylin = matmul(n("ssm_y"), hcat, sp["ccat"], "nn", M=M, N=dm.SW, K=2 * dm.C)
    yg = rowwise(n("ssm_gelu"), lambda y, u, d: (_gelu(y + d * u),), [(ylin, dm.SW, 0), (proj, dm.SW, dm.o_u)], [lw["d_skip"]],
                 [(dm.SW, BF16)], M=M)[0]
    z = matmul(n("ssm_glu"), yg, lw["w_glu"], "nn", b_lead=l, M=M, N=2 * dm.SW, K=dm.SW)
    sw = dm.SW

    def glu_fn(zb, b):
        zz = zb + b
        return (zz[:, :sw] * jax.nn.sigmoid(zz[:, sw:]),)

    o_ssm = rowwise(n("ssm_gate"), glu_fn, [(z, 2 * sw, 0)], [lw["b_glu"]], [(sw, F32)], M=M)[0]
    sv.update(hcat=hcat, ylin=ylin, yg=yg, z=z, o_ssm=o_ssm)
    o_dil, lse_dil = attention_fwd(n("dil_fwd"), proj, dm.o_qd, proj, dm.o_kd, proj, dm.o_vd, da=DIL_HEAD, dv=DIL_HEAD, pairs=dm.DW // LANES,
                                   scale=DIL_HEAD ** -0.5, kind="dilated", M=M)
    sv.update(o_dil=o_dil, lse_dil=lse_dil)
    yn = rowwise(n("out_norm"), lambda a, b, c, ga, gb, gc: (jnp.concatenate([_rms(a, ga), _rms(b, gb), _rms(c, gc)], axis=1),),
                 [(o_mla, dm.MW, 0), (o_ssm, dm.SW, 0), (o_dil, dm.DW, 0)], [lw["g_out_mla"], lw["g_out_ssm"], lw["g_out_dil"]],
                 [(D, BF16)], M=M)[0]
    x_mid = matmul(n("out_proj"), yn, lw["w_o"], "nn", b_lead=l, M=M, N=D, K=D, add=x)
    h2 = rms_fwd(n("rms_ffn"), x_mid, D, 0, lw["g_ffn"], M=M)
    gate = matmul(n("ffn_gate"), h2, lw["w_gate"], "nn", b_lead=l, M=M, N=dm.FF, K=D)
    up = matmul(n("ffn_up"), h2, lw["w_up"], "nn", b_lead=l, M=M, N=dm.FF, K=D)
    act = rowwise(n("ffn_act"), lambda g, u: (g * jax.nn.sigmoid(g) * u,), [(gate, dm.FF, 0), (up, dm.FF, 0)], [], [(dm.FF, BF16)], M=M)[0]
    x_out = matmul(n("ffn_down"), act, lw["w_down"], "nn", b_lead=l, M=M, N=D, K=dm.FF, add=x_mid)
    sv.update(yn=yn, x_mid=x_mid, h2=h2, gate=gate, up=up, act=act)
    return x_out, sv


def layer_backward(dm, l, dx, lw, sp, tabs, sv):
    M, D = dm.M, dm.D
    n = lambda s: f"{s}_l{l}"
    g = {}
    dact = matmul(n("ffn_down_dx"), dx, lw["w_down"], "nt", b_lead=l, M=M, N=dm.FF, K=D)
    g["w_down"] = matmul(n("ffn_down_dw"), sv["act"], dx, "tn", M=dm.FF, N=D, K=M, out_dtype=BF16)

    def act_bwd(gb, ub, db):
        _, vjp = jax.vjp(lambda a, b: a * jax.nn.sigmoid(a) * b, gb, ub)
        return vjp(db)

    dgate, dup = rowwise(n("ffn_act_bwd"), act_bwd, [(sv["gate"], dm.FF, 0), (sv["up"], dm.FF, 0), (dact, dm.FF, 0)], [],
                         [(dm.FF, BF16), (dm.FF, BF16)], M=M)
    dh2 = matmul(n("ffn_gate_dx"), dgate, lw["w_gate"], "nt", b_lead=l, M=M, N=D, K=dm.FF)
    dh2 = matmul(n("ffn_up_dx"), dup, lw["w_up"], "nt", b_lead=l, M=M, N=D, K=dm.FF, add=dh2)
    g["w_gate"] = matmul(n("ffn_gate_dw"), sv["h2"], dgate, "tn", M=D, N=dm.FF, K=M, out_dtype=BF16)
    g["w_up"] = matmul(n("ffn_up_dw"), sv["h2"], dup, "tn", M=D, N=dm.FF, K=M, out_dtype=BF16)
    dx_mid, g["g_ffn"] = rms_bwd(n("rms_ffn_bwd"), sv["x_mid"], D, 0, lw["g_ffn"], dh2, dx, M=M)
    dyn = matmul(n("out_proj_dx"), dx_mid, lw["w_o"], "nt", b_lead=l, M=M, N=D, K=D)
    g["w_o"] = matmul(n("out_proj_dw"), sv["yn"], dx_mid, "tn", M=D, N=D, K=M, out_dtype=BF16)
    mw, sw, dw = dm.MW, dm.SW, dm.DW

    def out_norm_bwd(a, b, c, dy, ga, gb, gc):
        res, sums = [], []
        for t, gg, lo, hi in ((a, ga, 0, mw), (b, gb, mw, mw + sw), (c, gc, mw + sw, mw + sw + dw)):
            _, vjp = jax.vjp(_rms, t, gg)
            dt, dg = vjp(dy[:, lo:hi])
            res.append(dt)
            sums.append(dg)
        return res + sums

    do_mla, do_ssm, do_dil, g["g_out_mla"], g["g_out_ssm"], g["g_out_dil"] = rowwise(
        n("out_norm_bwd"), out_norm_bwd, [(sv["o_mla"], mw, 0), (sv["o_ssm"], sw, 0), (sv["o_dil"], dw, 0), (dyn, D, 0)],
        [lw["g_out_mla"], lw["g_out_ssm"], lw["g_out_dil"]], [(mw, F32), (sw, F32), (dw, F32)], [mw, sw, dw], M=M)
    proj = sv["proj"]
    dqd, dkd, dvd = attention_bwd(n("dil_bwd"), proj, dm.o_qd, proj, dm.o_kd, proj, dm.o_vd, sv["o_dil"], do_dil, sv["lse_dil"],
                                  da=DIL_HEAD, dv=DIL_HEAD, pairs=dw // LANES, scale=DIL_HEAD ** -0.5, kind="dilated", M=M)
    def glu_bwd(zb, db, b):
        _, vjp = jax.vjp(lambda zz, bb: (zz + bb)[:, :sw] * jax.nn.sigmoid((zz + bb)[:, sw:]), zb, b)
        return vjp(db)

    dz, g["b_glu"] = rowwise(n("ssm_gate_bwd"), glu_bwd, [(sv["z"], 2 * sw, 0), (do_ssm, sw, 0)], [lw["b_glu"]], [(2 * sw, BF16)], [2 * sw], M=M)
    dyg = matmul(n("ssm_glu_dx"), dz, lw["w_glu"], "nt", b_lead=l, M=M, N=sw, K=2 * sw)
    g["w_glu"] = matmul(n("ssm_glu_dw"), sv["yg"], dz, "tn", M=sw, N=2 * sw, K=M, out_dtype=BF16)

    def gelu_bwd(y, u, dy, d):
        _, vjp = jax.vjp(lambda yy, uu, dd: _gelu(yy + dd * uu), y, u, d)
        return vjp(dy)

    dylin, du1, g["d_skip"] = rowwise(n("ssm_gelu_bwd"), gelu_bwd, [(sv["ylin"], sw, 0), (proj, sw, dm.o_u), (dyg, sw, 0)], [lw["d_skip"]],
                                      [(sw, BF16), (sw, F32)], [sw], M=M)
    seed = matmul(n("ssm_y_dx"), dylin, sp["ccat"], "nt", M=M, N=2 * dm.C, K=sw)
    d_ccat = matmul(n("ssm_y_dw"), sv["hcat"], dylin, "tn", M=2 * dm.C, N=sw, K=M)
    lam, d_acat = ssm_scan(n("ssm_scan_bwd"), seed, sp["acat_conj"], M=M, reverse=True, hcat=sv["hcat"])
    du = matmul(n("ssm_bu_dx"), lam, sp["bcat"], "nt", M=M, N=sw, K=2 * dm.C, add=du1, out_dtype=BF16)
    d_bcat = matmul(n("ssm_bu_dw"), proj, lam, "tn", M=sw, N=2 * dm.C, K=M, a_off=(0, dm.o_u))
    g["ssm_raw"] = (d_acat, d_bcat, d_ccat)
    nw = dm.H * MLA_NOPE
    dqn, dkn, dv_, dqp, dkp = attention_bwd(n("mla_bwd"), sv["q_bf"], 0, sv["kv"], 0, sv["kv"], nw, sv["o_mla"], do_mla, sv["lse_mla"],
                                            da=MLA_NOPE, dv=MLA_V, pairs=dm.H // 2, scale=(MLA_NOPE + MLA_ROPE) ** -0.5, kind="causal", M=M,
                                            qb=sv["q_bf"], qb_off=nw, kb=sv["kpe"])
    cosq, sinq, cosk, sink = tabs
    pw = dm.H * MLA_ROPE
    dqp_u, dkr = rowwise(n("rope_bwd"), lambda a, b, cq, sq, ck, sk: (_rope_t(a, cq, sq), _rope_t(b, ck, sk)),
                         [(dqp, pw, 0), (dkp, LANES, 0), (cosq, pw, 0), (sinq, pw, 0), (cosk, LANES, 0), (sink, LANES, 0)], [],
                         [(pw, BF16), (LANES, BF16)], M=M)
    dq = jnp.concatenate([dqn, dqp_u], axis=1)
    dkv = jnp.concatenate([dkn.astype(BF16), dv_.astype(BF16)], axis=1)
    dcqn = matmul(n("q_up_dx"), dq, lw["w_uq"], "nt", b_lead=l, M=M, N=dm.QL, K=dm.QW)
    g["w_uq"] = matmul(n("q_up_dw"), sv["cqn"], dq, "tn", M=dm.QL, N=dm.QW, K=M, out_dtype=BF16)
    dckvn = matmul(n("kv_up_dx"), dkv, lw["w_ukv"], "nt", b_lead=l, M=M, N=dm.KVL, K=dm.KVW)
    g["w_ukv"] = matmul(n("kv_up_dw"), sv["ckvn"], dkv, "tn", M=dm.KVL, N=dm.KVW, K=M, out_dtype=BF16)
    dcq, g["g_q"] = rms_bwd(n("rms_q_bwd"), proj, dm.QL, dm.o_cq, lw["g_q"], dcqn, M=M, out_dtype=BF16)
    dckv, g["g_kv"] = rms_bwd(n("rms_kv_bwd"), proj, dm.KVL, dm.o_ckv, lw["g_kv"], dckvn, M=M, out_dtype=BF16)
    dproj = jnp.concatenate([dcq, du, dqd, dkd.astype(BF16), dvd.astype(BF16), dckv, dkr], axis=1)
    dh1 = matmul(n("in_proj_dx"), dproj, lw["w_in"], "nt", b_lead=l, M=M, N=D, K=dm.PW)
    g["w_in"] = matmul(n("in_proj_dw"), sv["h1"], dproj, "tn", M=D, N=dm.PW, K=M, out_dtype=BF16)
    dx_in, g["g_mix"] = rms_bwd(n("rms_mix_bwd"), sv["x_in"], D, 0, lw["g_mix"], dh1, dx_mid, M=M)
    return dx_in, g


def local_step(dm, x, target, full, small):
    L = small["g_mix"].shape[0]
    tabs = _rope_tables(dm)
    lws, sps, raws = [], [], []
    for l in range(L):
        lw = dict(full)
        for k in ("g_mix", "g_q", "g_kv", "b_glu", "g_out_mla", "g_out_ssm", "g_out_dil", "g_ffn"):
            lw[k] = small[k][l].reshape(1, -1)
        lw["d_skip"] = small["d_skip"][l].reshape(1, dm.SW)
        raw = _ssm_layer_params(dm, small["a_re"][l], small["a_im"][l], small["log_dt"][l], small["b_re"][l], small["b_im"][l])
        ar, ai, bbr, bbi = ssm_param_fwd(f"ssm_param_l{l}", *raw)
        g_ = dm.G
        bd = lambda t: _block_diag(jnp.transpose(t.reshape(SSM_GROUP, g_, SSM_STATE), (1, 0, 2)), g_)
        cd = lambda t: _block_diag(jnp.transpose(t, (0, 2, 1)), g_)
        cre, cim = cd(small["c_re"][l]), cd(small["c_im"][l])
        sp = {"acat": _cat_cols(ar, ai), "acat_conj": _cat_cols(ar, -ai),
              "bcat": _cat_cols(bd(bbr), bd(bbi)).astype(BF16),
              "ccat": _cat_cols(cre.T, -cim.T).T.astype(BF16)}
        lws.append(lw)
        sps.append(sp)
        raws.append(raw)
    saved = []
    h = x
    for l in range(L):
        h, sv = layer_forward(dm, l, h, lws[l], sps[l], tabs)
        saved.append(sv)
    D = dm.D

    def loss_fn(xb, tb, gb):
        y, vjp = jax.vjp(_rms, xb, gb)
        err = y - tb
        dxb, dg = vjp(err * (1.0 / D))
        part = 0.5 * jnp.sum(jnp.mean(err * err, axis=-1, keepdims=True), axis=0, keepdims=True)
        lane = lax.broadcasted_iota(jnp.int32, (1, LANES), 1)
        return dxb, dg, jnp.where(lane == 0, part, 0.0)

    dx, g_final, loss_part = rowwise("loss", loss_fn, [(h, D, 0), (target, D, 0)], [small["g_final"].reshape(1, D)], [(D, F32)], [D, LANES], M=dm.M)
    grads = [None] * L
    for l in reversed(range(L)):
        dx, g = layer_backward(dm, l, dx, lws[l], sps[l], tabs, saved[l])
        d_acat, d_bcat, d_ccat = g.pop("ssm_raw")
        d_ar, d_ai = _uncat_cols(d_acat)
        dbr, dbi = _uncat_cols(d_bcat)
        g_ = dm.G
        to_rows = lambda t: jnp.transpose(_diag_blocks(t, g_), (1, 0, 2)).reshape(SSM_GROUP, dm.C)
        da_re, da_im, dldt, db_re, db_im = ssm_param_bwd(f"ssm_param_bwd_l{l}", *raws[l], d_ar, d_ai, to_rows(dbr), to_rows(dbi))
        dcr, dci = _uncat_cols(d_ccat.T)
        g["a_re"], g["a_im"] = da_re.reshape(g_, SSM_STATE), da_im.reshape(g_, SSM_STATE)
        g["log_dt"] = jnp.sum(dldt.reshape(g_, SSM_STATE), axis=1)
        from_rows = lambda t: jnp.transpose(t.reshape(SSM_GROUP, g_, SSM_STATE), (1, 2, 0))
        g["b_re"], g["b_im"] = from_rows(db_re), from_rows(db_im)
        g["c_re"] = jnp.transpose(_diag_blocks(dcr.T, g_), (0, 2, 1))
        g["c_im"] = -jnp.transpose(_diag_blocks(dci.T, g_), (0, 2, 1))
        g["d_skip"] = g["d_skip"].reshape(g_, SSM_GROUP)
        grads[l] = g
    return loss_part, dx, g_final, grads


SHARDED = ["w_in", "w_uq", "w_ukv", "w_glu", "w_o", "w_gate", "w_up", "w_down"]
ROW_SHARDED = ("w_o", "w_down")
SMALL = ["g_mix", "g_q", "g_kv", "a_re", "a_im", "b_re", "b_im", "c_re", "c_im", "d_skip", "log_dt", "b_glu",
         "g_out_mla", "g_out_ssm", "g_out_dil", "g_ffn", "g_final"]
ORDER = ["g_mix", "w_in", "g_q", "w_uq", "g_kv", "w_ukv", "a_re", "a_im", "b_re", "b_im", "c_re", "c_im", "d_skip", "log_dt",
         "w_glu", "b_glu", "g_out_mla", "g_out_ssm", "g_out_dil", "w_o", "g_ffn", "w_gate", "w_up", "w_down", "g_final"]


def gather_weights(shards):
    full = {}
    for k in SHARDED:
        w = shards[k]
        L, K, nn = w.shape
        rows = L * K
        assert rows % (2 * SUBLANES_BF16) == 0
        wb = rowwise(f"cast_{k}", lambda t: (t,), [(w.reshape(rows, nn), nn, 0)], [], [(nn, BF16)], M=rows)[0]
        got = all_gather_halves(f"ag_{k}", wb.reshape(2, rows // 2, nn)).reshape(N_CHIPS, L, K, nn)
        if k in ROW_SHARDED:
            full[k] = jnp.transpose(got, (1, 0, 2, 3)).reshape(L, N_CHIPS * K, nn)
        else:
            full[k] = jnp.transpose(got, (1, 2, 0, 3)).reshape(L, K, N_CHIPS * nn)
    return full


def scatter_grads(name, per_layer):
    gfull = jnp.stack(per_layer)
    L, K, N = gfull.shape
    if name in ROW_SHARDED:
        k = K // N_CHIPS
        p = jnp.transpose(gfull.reshape(L, N_CHIPS, k, N), (1, 0, 2, 3))
        shard_shape = (L, k, N)
    else:
        nn = N // N_CHIPS
        p = jnp.transpose(gfull.reshape(L, K, N_CHIPS, nn), (2, 0, 1, 3))
        shard_shape = (L, K, nn)
    rows, cw = shard_shape[0] * shard_shape[1], shard_shape[2]
    p = jnp.transpose(p.reshape(N_CHIPS, 2, rows // 2, cw), (1, 0, 2, 3))
    return reduce_scatter(name, p).reshape(shard_shape)


def kernel(x, g_mix, w_in, g_q, w_uq, g_kv, w_ukv, a_re, a_im, b_re, b_im, c_re, c_im, d_skip, log_dt, w_glu, b_glu, g_out_mla, g_out_ssm, g_out_dil, w_o, g_ffn, w_gate, w_up, w_down, g_final, loss_target, m_g_mix, m_w_in, m_g_q, m_w_uq, m_g_kv, m_w_ukv, m_a_re, m_a_im, m_b_re, m_b_im, m_c_re, m_c_im, m_d_skip, m_log_dt, m_w_glu, m_b_glu, m_g_out_mla, m_g_out_ssm, m_g_out_dil, m_w_o, m_g_ffn, m_w_gate, m_w_up, m_w_down, m_g_final, v_g_mix, v_w_in, v_g_q, v_w_uq, v_g_kv, v_w_ukv, v_a_re, v_a_im, v_b_re, v_b_im, v_c_re, v_c_im, v_d_skip, v_log_dt, v_w_glu, v_b_glu, v_g_out_mla, v_g_out_ssm, v_g_out_dil, v_w_o, v_g_ffn, v_w_gate, v_w_up, v_w_down, v_g_final):
    args = locals()
    w = {k: args[k] for k in ORDER}
    mom = {k: args["m_" + k] for k in ORDER}
    var = {k: args["v_" + k] for k in ORDER}
    dm = Dims(x, g_q, g_kv, g_out_mla, g_out_ssm, g_out_dil, w_gate.shape[-1] * N_CHIPS)
    L = g_mix.shape[0]

    full = gather_weights({k: w[k] for k in SHARDED})
    full["w_in"] = _regroup_in(dm, full["w_in"])
    full["w_uq"] = _split_heads(full["w_uq"], dm.H, MLA_NOPE)
    full["w_ukv"] = _split_heads(full["w_ukv"], dm.H, MLA_NOPE)

    loss_part, dx, g_final_part, grads = local_step(dm, x.reshape(dm.M, dm.D), loss_target.reshape(dm.M, dm.D), full, {k: w[k] for k in SMALL})

    for l in range(L):
        grads[l]["w_in"] = _ungroup_in(dm, grads[l]["w_in"])
        grads[l]["w_uq"] = _merge_heads(grads[l]["w_uq"], dm.H, MLA_NOPE)
        grads[l]["w_ukv"] = _merge_heads(grads[l]["w_ukv"], dm.H, MLA_NOPE)
    gsum = {k: scatter_grads(k, [grads[l][k] for l in range(L)]) for k in SHARDED}
    small_names = [k for k in SMALL if k != "g_final"]
    pieces = [jnp.stack([grads[l][k] for l in range(L)]).reshape(-1) for k in small_names] + [g_final_part.reshape(-1), loss_part.reshape(-1)]
    sizes = [int(p.shape[0]) for p in pieces]
    total = sum(sizes)
    rows = -(-total // (LANES * 16)) * 16
    pack = lambda ps: jnp.concatenate(ps + [jnp.zeros((rows * LANES - total,), F32)]).reshape(rows, LANES)
    red = all_reduce_small(pack(pieces))
    flat = red.reshape(-1)
    offs = np.concatenate([[0], np.cumsum(sizes)]).astype(int)
    names = small_names + ["g_final"]
    for i, k in enumerate(names):
        gsum[k] = flat[offs[i]:offs[i + 1]].reshape(w[k].shape)
    loss = flat[offs[len(names)]]

    delta, new_m, new_v = {}, {}, {}
    for k in SHARDED:
        shp = w[k].shape
        two_d = lambda t: t.reshape(shp[0] * shp[1], shp[2])
        d_, m_, v_ = adamw(f"adam_{k}", two_d(w[k]), two_d(gsum[k]), two_d(mom[k]), two_d(var[k]))
        delta[k], new_m[k], new_v[k] = d_.reshape(shp), m_.reshape(shp), v_.reshape(shp)
    sm_sizes = sizes[:len(names)]
    sm_total = sum(sm_sizes)
    packs = lambda d: jnp.concatenate([d[k].reshape(-1) for k in names] + [jnp.zeros((rows * LANES - sm_total,), F32)]).reshape(rows, LANES)
    gs = jnp.concatenate([flat[:sm_total], jnp.zeros((rows * LANES - sm_total,), F32)]).reshape(rows, LANES)
    d_, m_, v_ = adamw("adam_small", packs(w), gs, packs(mom), packs(var))
    for i, k in enumerate(names):
        sl = slice(offs[i], offs[i + 1])
        delta[k], new_m[k], new_v[k] = (t.reshape(-1)[sl].reshape(w[k].shape) for t in (d_, m_, v_))

    return (loss, dx.reshape(x.shape), *[gsum[k] for k in ORDER], *[delta[k] for k in ORDER],
            *[new_m[k] for k in ORDER], *[new_v[k] for k in ORDER])
```

```python
import functools
import math

import numpy as np
import jax
import jax.numpy as jnp
from jax import lax
from jax.experimental import pallas as pl
from jax.experimental.pallas import tpu as pltpu

F32 = jnp.float32
BF16 = jnp.bfloat16
MESH = pl.DeviceIdType.MESH

NORM_EPS = 1e-6
MLA_NOPE, MLA_ROPE, MLA_V = 128, 64, 128
SSM_GROUP, SSM_STATE = 16, 64
DIL_HEAD = 64
DIL_PATTERNS = ((128, 1), (512, 4), (2048, 16))
ROPE_THETA = 10000.0
ADAM_LR, ADAM_B1, ADAM_B2, ADAM_EPS, ADAM_WD, ADAM_STEP = 0.001, 0.9, 0.999, 1e-08, 0.01, 10
N_CHIPS = 4

LANES = 128
SUBLANES_BF16 = 16
VMEM_LIMIT = 56 * 1024 * 1024
ROW_BUDGET = 20 * 1024 * 1024
MM_BUDGET = 40 * 1024 * 1024
NEG = -1e30


def _cparams(sem=None):
    return pltpu.CompilerParams(dimension_semantics=sem, vmem_limit_bytes=VMEM_LIMIT)


def _pick(n, cap, q, off=0):
    best = None
    for d in range(q, min(n, cap) + 1, q):
        if n % d == 0 and off % d == 0:
            best = d
    if best is None or (best * 4 <= min(cap, n) and n <= 3072 and off % n == 0):
        assert off % n == 0, (n, off)
        return n
    return best


_DOT_DIMS = {"nn": (((1,), (0,)), ((), ())), "nt": (((1,), (1,)), ((), ())), "tn": (((0,), (0,)), ((), ()))}


def _divs(n, q, within=None, off=0):
    return [d for d in range(q, n + 1, q) if n % d == 0 and off % d == 0 and (within is None or within % d == 0)] or [n]


def _mm_tiles(M, N, K, tms, tns, tks, ab, bb, ob):
    best = None
    for tk in tks:
        nk = K // tk
        for tn in tns:
            for tm in tms:
                if 2 * (tm * tk * ab + tk * tn * bb + tm * tn * ob) + tm * tn * 4 * (2 if nk > 1 else 1) > MM_BUDGET:
                    continue
                steps = (M // tm) * (N // tn) * nk
                hbm = M * K * ab * (1 if nk == 1 else N // tn) + K * N * bb * (M // tm) + M * N * ob
                cost = steps * 0.35e-6 + hbm / 3.0e12 + (nk - 1) * M * N * 12 / 4.0e12
                if best is None or cost < best[0]:
                    best = (cost, tm, tn, tk)
    assert best is not None, (M, N, K)
    return best[1:]


def matmul(name, a, b, mode, *, M, N, K, a_off=(0, 0), b_off=(0, 0), b_lead=None, w=None, add=None, out_dtype=F32, into=None):
    tn_mode = mode == "tn"
    a_ro, a_co = (a_off[1], a_off[0]) if tn_mode else a_off
    b_no, b_ko = b_off if mode == "nt" else (b_off[1], b_off[0])
    n_within = k_within = None
    if w is not None:
        kind, wl, shard = w
        if (kind, mode) in (("cols", "nn"), ("rows", "nt")):
            n_within = shard
        else:
            k_within = shard
    if into is not None and into[0] == "cols":
        n_within = into[3]
    tms = [d for d in _divs(M, 128 if tn_mode else SUBLANES_BF16, None, a_ro) if d <= 1024]
    tns = [d for d in _divs(N, LANES, n_within, b_no) if d <= 2048]
    tks = _divs(K, SUBLANES_BF16 if tn_mode else LANES, k_within, math.gcd(a_co, b_ko))
    ob = jnp.dtype(out_dtype).itemsize + (add.dtype.itemsize if add is not None else 0)
    tm, tn, tk = _mm_tiles(M, N, K, tms, tns, tks, a.dtype.itemsize, b.dtype.itemsize, ob)
    nk = K // tk
    dn = _DOT_DIMS[mode]

    if tn_mode:
        a_spec = pl.BlockSpec((tk, tm), lambda i, j, k: (k + a_co // tk, i + a_ro // tm))
    else:
        a_spec = pl.BlockSpec((tm, tk), lambda i, j, k: (i + a_ro // tm, k + a_co // tk))
    b_blk = (tn, tk) if mode == "nt" else (tk, tn)
    if w is not None:
        per = shard // (tn if n_within is not None else tk)
        if (kind, mode) == ("cols", "nn"):
            imap = lambda i, j, k: (j // per, wl, k, j % per)
        elif (kind, mode) == ("cols", "nt"):
            imap = lambda i, j, k: (k // per, wl, j, k % per)
        elif (kind, mode) == ("rows", "nn"):
            imap = lambda i, j, k: (k // per, wl, k % per, j)
        else:
            imap = lambda i, j, k: (j // per, wl, j % per, k)
        b_spec = pl.BlockSpec((None, None) + b_blk, imap)
    else:
        if mode == "nt":
            imap = lambda i, j, k: (j + b_no // tn, k + b_ko // tk)
        else:
            imap = lambda i, j, k: (k + b_ko // tk, j + b_no // tn)
        if b_lead is None:
            b_spec = pl.BlockSpec(b_blk, imap)
        else:
            b_spec = pl.BlockSpec((None,) + b_blk, lambda i, j, k: (b_lead,) + imap(i, j, k))
    o_plain = pl.BlockSpec((tm, tn), lambda i, j, k: (i, j))
    if into is None:
        o_spec, out_shape = o_plain, jax.ShapeDtypeStruct((M, N), out_dtype)
    else:
        buf, il = into[1], into[2]
        assert buf.dtype == jnp.dtype(out_dtype)
        if into[0] == "cols":
            oper = into[3] // tn
            o_spec = pl.BlockSpec((None, None, tm, tn), lambda i, j, k: (il, j // oper, i, j % oper))
        else:
            o_spec = pl.BlockSpec((None, tm, tn), lambda i, j, k: (il, i, j))
        out_shape = jax.ShapeDtypeStruct(buf.shape, buf.dtype)
    has_add = add is not None
    chained = into is not None and not isinstance(into[1], jax.ShapeDtypeStruct)
    n_in = 2 + has_add + chained

    def body(*refs):
        a_ref, b_ref = refs[0], refs[1]
        add_ref = refs[2] if has_add else None
        o_ref = refs[n_in]
        part = lax.dot_general(a_ref[...].astype(BF16), b_ref[...].astype(BF16), dn, preferred_element_type=F32)

        def finish(r):
            if has_add:
                r = r + add_ref[...].astype(F32)
            o_ref[...] = r.astype(o_ref.dtype)

        if nk == 1:
            finish(part)
        else:
            acc_ref = refs[-1]
            k = pl.program_id(2)

            @pl.when(k == 0)
            def _():
                acc_ref[...] = part

            @pl.when((k > 0) & (k < nk - 1))
            def _():
                acc_ref[...] += part

            @pl.when(k == nk - 1)
            def _():
                finish(acc_ref[...] + part)

    in_specs = [a_spec, b_spec] + ([o_plain] if has_add else []) + ([ANY] if chained else [])
    args = (a, b) + ((add,) if has_add else ()) + ((into[1],) if chained else ())
    return pl.pallas_call(
        body, name=name, out_shape=out_shape, grid=(M // tm, N // tn, nk), in_specs=in_specs, out_specs=o_spec,
        scratch_shapes=[pltpu.VMEM((tm, tn), F32)] if nk > 1 else [],
        input_output_aliases={n_in - 1: 0} if chained else {},
        compiler_params=_cparams(("parallel", "parallel", "arbitrary")),
    )(*args)


def rowwise(name, fn, rows, vecs, outs, sums=(), *, M):
    rows = [tuple(r) + (0,) * (4 - len(r)) for r in rows]
    nr, nv, no, ns = len(rows), len(vecs), len(outs), len(sums)
    per_row = sum(w * a.dtype.itemsize for a, w, _, _ in rows) + sum(w * jnp.dtype(d).itemsize for w, d in outs)
    tr = _pick(M, max(8, min(512, ROW_BUDGET // (2 * per_row))), 16 if M % 16 == 0 else 8)

    def body(*refs):
        i = pl.program_id(0)
        res = fn(*[r[...] for r in refs[:nr + nv]])
        o_refs = refs[nr + nv:nr + nv + no]
        s_refs = refs[nr + nv + no:]
        for ref, val in zip(o_refs, res[:no]):
            ref[...] = val.astype(ref.dtype)
        if ns:
            @pl.when(i == 0)
            def _():
                for ref in s_refs:
                    ref[...] = jnp.zeros(ref.shape, F32)

            for ref, val in zip(s_refs, res[no:]):
                ref[...] += val

    in_specs = [pl.BlockSpec((tr, w), functools.partial(lambda i, cb, rb: (i + rb, cb), cb=off // w, rb=roff // tr)) for _, w, off, roff in rows]
    for _, w, off, roff in rows:
        assert off % w == 0 and roff % tr == 0
    in_specs += [pl.BlockSpec(v.shape, functools.partial(lambda i, nd: (0,) * nd, nd=v.ndim)) for v in vecs]
    out_specs = [pl.BlockSpec((tr, w), lambda i: (i, 0)) for w, _ in outs]
    out_specs += [pl.BlockSpec((1, w), lambda i: (0, 0)) for w in sums]
    out_shape = [jax.ShapeDtypeStruct((M, w), d) for w, d in outs] + [jax.ShapeDtypeStruct((1, w), F32) for w in sums]
    return pl.pallas_call(
        body, name=name, out_shape=out_shape, grid=(M // tr,), in_specs=in_specs, out_specs=out_specs,
        compiler_params=_cparams(("arbitrary",) if ns else ("parallel",)),
    )(*[r[0] for r in rows], *vecs)


def _rms(x, g):
    xf = x.astype(F32)
    return xf * lax.rsqrt(jnp.mean(xf * xf, axis=-1, keepdims=True) + NORM_EPS) * g


def _gelu(y):
    return 0.5 * y * (1.0 + jnp.tanh(math.sqrt(2.0 / math.pi) * (y + 0.044715 * (y * y * y))))


def _colsum(v):
    return jnp.sum(v, axis=0, keepdims=True)


def rms_fwd(name, x, width, off, g, *, M):
    return rowwise(name, lambda xb, gb: (_rms(xb, gb),), [(x, width, off)], [g], [(width, BF16)], M=M)[0]


def rms_bwd(name, x, width, off, g, dy, resid=None, *, M, out_dtype=F32):
    def fn(xb, dyb, *rest):
        gb = rest[-1]
        _, vjp = jax.vjp(_rms, xb.astype(F32), gb)
        dx, dg = vjp(dyb.astype(F32))
        if resid is not None:
            dx = dx + rest[0]
        return dx, dg

    rows = [(x, width, off), (dy, width, 0)] + ([(resid, width, 0)] if resid is not None else [])
    return rowwise(name, fn, rows, [g], [(width, out_dtype)], [width], M=M)


def _swap_halves(x, half):
    w = x.shape[-1]
    lane = lax.broadcasted_iota(jnp.int32, x.shape, x.ndim - 1)
    first = (lane % (2 * half)) < half
    return jnp.where(first, pltpu.roll(x, w - half, x.ndim - 1), pltpu.roll(x, half, x.ndim - 1))


def _attn_weight(kind, delta):
    if kind == "causal":
        return (delta >= 0).astype(F32)
    w = jnp.zeros(delta.shape, F32)
    for window, dil in DIL_PATTERNS:
        ok = (delta >= 0) & (delta <= window)
        if dil > 1:
            ok = ok & ((delta & (dil - 1)) == 0)
        w = w + ok.astype(F32)
    return w


def _dot(a, b, mode):
    return lax.dot_general(a, b, _DOT_DIMS[mode], preferred_element_type=F32)


def attention_fwd(name, qa, qa_off, ka, ka_off, v, v_off, *, da, dv, pairs, scale, kind, M, qb=None, qb_off=0, kb=None):
    tq = min(256, M)
    tk = min(512, M)
    has_b = qb is not None
    dr = MLA_ROPE

    def body(*refs):
        if has_b:
            qa_ref, ka_ref, v_ref, qb_ref, kb_ref, o_ref, lse_ref = refs
        else:
            qa_ref, ka_ref, v_ref, o_ref, lse_ref = refs
        i = pl.program_id(1)
        t0 = i * tq
        nkb = (t0 + tq + tk - 1) // tk
        o_parts, lse_parts = [], []
        for hh in range(2):
            q1 = qa_ref[:, hh * da:(hh + 1) * da].astype(BF16)
            q2 = qb_ref[:, hh * dr:(hh + 1) * dr].astype(BF16) if has_b else None

            def step(kbi, carry, hh=hh, q1=q1, q2=q2):
                m, l, acc = carry
                ks = pl.multiple_of(kbi * tk, tk)
                k1 = ka_ref[pl.ds(ks, tk), hh * da:(hh + 1) * da].astype(BF16)
                s = _dot(q1, k1, "nt")
                if has_b:
                    s = s + _dot(q2, kb_ref[pl.ds(ks, tk), 0:dr].astype(BF16), "nt")
                s = s * scale
                delta = (t0 + lax.broadcasted_iota(jnp.int32, (tq, tk), 0)) - (ks + lax.broadcasted_iota(jnp.int32, (tq, tk), 1))
                w = _attn_weight(kind, delta)
                s = jnp.where(w > 0, s, NEG)
                m_new = jnp.maximum(m, jnp.max(s, axis=1, keepdims=True))
                alpha = jnp.exp(m - m_new)
                p = w * jnp.exp(s - m_new)
                l = alpha * l + jnp.sum(p, axis=1, keepdims=True)
                vv = v_ref[pl.ds(ks, tk), hh * dv:(hh + 1) * dv].astype(BF16)
                acc = alpha * acc + _dot(p.astype(BF16), vv, "nn")
                return m_new, l, acc

            m, l, acc = lax.fori_loop(0, nkb, step, (jnp.full((tq, 1), NEG, F32), jnp.zeros((tq, 1), F32), jnp.zeros((tq, dv), F32)))
            o_parts.append(acc / l)
            lse_parts.append(m + jnp.log(l))
        o_ref[...] = jnp.concatenate(o_parts, axis=1)
        lane = lax.broadcasted_iota(jnp.int32, (tq, LANES), 1)
        lse_ref[...] = jnp.where(lane == 0, lse_parts[0], jnp.where(lane == 1, lse_parts[1], 0.0))

    assert qa_off % (2 * da) == 0 and ka_off % (2 * da) == 0 and v_off % (2 * dv) == 0
    in_specs = [
        pl.BlockSpec((tq, 2 * da), lambda hp, i: (i, qa_off // (2 * da) + hp)),
        pl.BlockSpec((M, 2 * da), lambda hp, i: (0, ka_off // (2 * da) + hp)),
        pl.BlockSpec((M, 2 * dv), lambda hp, i: (0, v_off // (2 * dv) + hp)),
    ]
    args = [qa, ka, v]
    if has_b:
        assert qb_off % LANES == 0
        in_specs += [pl.BlockSpec((tq, LANES), lambda hp, i: (i, qb_off // LANES + hp)),
                     pl.BlockSpec((M, LANES), lambda hp, i: (0, 0))]
        args += [qb, kb]
    out_specs = [pl.BlockSpec((tq, 2 * dv), lambda hp, i: (i, hp)),
                 pl.BlockSpec((None, tq, LANES), lambda hp, i: (hp, i, 0))]
    out_shape = [jax.ShapeDtypeStruct((M, pairs * 2 * dv), F32), jax.ShapeDtypeStruct((pairs, M, LANES), F32)]
    return pl.pallas_call(
        body, name=name, out_shape=out_shape, grid=(pairs, M // tq), in_specs=in_specs, out_specs=out_specs,
        compiler_params=_cparams(("parallel", "arbitrary")),
    )(*args)


def attention_bwd(name, qa, qa_off, ka, ka_off, v, v_off, o, do, lse, *, da, dv, pairs, scale, kind, M,
                  qb=None, qb_off=0, kb=None):
    tq = min(256, M)
    tk = min(256, M)
    has_b = qb is not None
    dr = MLA_ROPE

    def body(*refs):
        if has_b:
            qa_ref, ka_ref, v_ref, o_ref, do_ref, lse_ref, qb_ref, kb_ref, dqa_ref, dka_ref, dv_ref, dqb_ref, dkb_ref = refs
        else:
            qa_ref, ka_ref, v_ref, o_ref, do_ref, lse_ref, dqa_ref, dka_ref, dv_ref = refs
        hp = pl.program_id(0)
        i = pl.program_id(1)
        t0 = i * tq
        nkb = (t0 + tq + tk - 1) // tk

        @pl.when(i == 0)
        def _():
            dka_ref[...] = jnp.zeros(dka_ref.shape, F32)
            dv_ref[...] = jnp.zeros(dv_ref.shape, F32)

        if has_b:
            @pl.when((i == 0) & (hp == 0))
            def _():
                dkb_ref[...] = jnp.zeros(dkb_ref.shape, F32)

        dq1_parts, dq2_parts = [], []
        for hh in range(2):
            q1 = qa_ref[:, hh * da:(hh + 1) * da].astype(BF16)
            q2 = qb_ref[:, hh * dr:(hh + 1) * dr].astype(BF16) if has_b else None
            do_h = do_ref[:, hh * dv:(hh + 1) * dv]
            o_h = o_ref[:, hh * dv:(hh + 1) * dv]
            rowdot = jnp.sum(do_h * o_h, axis=1, keepdims=True)
            do_bf = do_h.astype(BF16)
            lse_h = lse_ref[:, hh:hh + 1]

            def step(kbi, carry, hh=hh, q1=q1, q2=q2, do_bf=do_bf, rowdot=rowdot, lse_h=lse_h):
                dq1, dq2 = carry
                ks = pl.multiple_of(kbi * tk, tk)
                k1 = ka_ref[pl.ds(ks, tk), hh * da:(hh + 1) * da].astype(BF16)
                s = _dot(q1, k1, "nt")
                if has_b:
                    k2 = kb_ref[pl.ds(ks, tk), 0:dr].astype(BF16)
                    s = s + _dot(q2, k2, "nt")
                s = s * scale
                delta = (t0 + lax.broadcasted_iota(jnp.int32, (tq, tk), 0)) - (ks + lax.broadcasted_iota(jnp.int32, (tq, tk), 1))
                w = _attn_weight(kind, delta)
                p = w * jnp.exp(jnp.where(w > 0, s, NEG) - lse_h)
                vv = v_ref[pl.ds(ks, tk), hh * dv:(hh + 1) * dv].astype(BF16)
                dp = _dot(do_bf, vv, "nt")
                ds = (p * (dp - rowdot) * scale).astype(BF16)
                dq1 = dq1 + _dot(ds, k1, "nn")
                dka_ref[pl.ds(ks, tk), hh * da:(hh + 1) * da] += _dot(ds, q1, "tn")
                dv_ref[pl.ds(ks, tk), hh * dv:(hh + 1) * dv] += _dot(p.astype(BF16), do_bf, "tn")
                if has_b:
                    dq2 = dq2 + _dot(ds, k2, "nn")
                    dkb_ref[pl.ds(ks, tk), 0:dr] += _dot(ds, q2, "tn")
                return dq1, dq2

            dq1, dq2 = lax.fori_loop(0, nkb, step, (jnp.zeros((tq, da), F32), jnp.zeros((tq, dr), F32)))
            dq1_parts.append(dq1)
            dq2_parts.append(dq2)
        dqa_ref[...] = jnp.concatenate(dq1_parts, axis=1).astype(dqa_ref.dtype)
        if has_b:
            dqb_ref[...] = jnp.concatenate(dq2_parts, axis=1).astype(dqb_ref.dtype)

    in_specs = [
        pl.BlockSpec((tq, 2 * da), lambda hp, i: (i, qa_off // (2 * da) + hp)),
        pl.BlockSpec((M, 2 * da), lambda hp, i: (0, ka_off // (2 * da) + hp)),
        pl.BlockSpec((M, 2 * dv), lambda hp, i: (0, v_off // (2 * dv) + hp)),
        pl.BlockSpec((tq, 2 * dv), lambda hp, i: (i, hp)),
        pl.BlockSpec((tq, 2 * dv), lambda hp, i: (i, hp)),
        pl.BlockSpec((None, tq, LANES), lambda hp, i: (hp, i, 0)),
    ]
    args = [qa, ka, v, o, do, lse]
    out_specs = [pl.BlockSpec((tq, 2 * da), lambda hp, i: (i, hp)),
                 pl.BlockSpec((M, 2 * da), lambda hp, i: (0, hp)),
                 pl.BlockSpec((M, 2 * dv), lambda hp, i: (0, hp))]
    out_shape = [jax.ShapeDtypeStruct((M, pairs * 2 * da), BF16),
                 jax.ShapeDtypeStruct((M, pairs * 2 * da), F32),
                 jax.ShapeDtypeStruct((M, pairs * 2 * dv), F32)]
    if has_b:
        in_specs += [pl.BlockSpec((tq, LANES), lambda hp, i: (i, qb_off // LANES + hp)),
                     pl.BlockSpec((M, LANES), lambda hp, i: (0, 0))]
        args += [qb, kb]
        out_specs += [pl.BlockSpec((tq, LANES), lambda hp, i: (i, hp)), pl.BlockSpec((M, LANES), lambda hp, i: (0, 0))]
        out_shape += [jax.ShapeDtypeStruct((M, pairs * LANES), F32), jax.ShapeDtypeStruct((M, LANES), F32)]
    return pl.pallas_call(
        body, name=name, out_shape=out_shape, grid=(pairs, M // tq), in_specs=in_specs, out_specs=out_specs,
        compiler_params=_cparams(("arbitrary", "arbitrary")),
    )(*args)


def ssm_scan(name, xcat, acat, *, M, reverse=False, hcat=None):
    C2 = xcat.shape[1]
    cb = LANES
    tb = min(128, M)
    nblk = M // tb
    with_da = hcat is not None

    def body(*refs):
        if with_da:
            x_ref, a_ref, h_ref, o_ref, da_ref, p_ref = refs
        else:
            x_ref, a_ref, o_ref, p_ref = refs
        ar, ai = a_ref[:, :cb], a_ref[:, cb:]
        row = lax.broadcasted_iota(jnp.int32, (tb, cb), 0)

        def logscan(xr, xi):
            pr, pi = ar, ai
            d = 1
            while d < tb:
                shift = tb - d if reverse else d
                keep = (row < tb - d) if reverse else (row >= d)
                sr = jnp.where(keep, pltpu.roll(xr, shift, 0), 0.0)
                si = jnp.where(keep, pltpu.roll(xi, shift, 0), 0.0)
                xr, xi = xr + pr * sr - pi * si, xi + pr * si + pi * sr
                pr, pi = pr * pr - pi * pi, 2.0 * pr * pi
                d *= 2
            return xr, xi

        seed = row == (tb - 1 if reverse else 0)
        p0r, p0i = logscan(jnp.where(seed, ar, 0.0), jnp.where(seed, ai, 0.0))
        p_ref[:, :cb] = p0r
        p_ref[:, cb:] = p0i
        sub = lax.broadcasted_iota(jnp.int32, (8, cb), 0)
        edge = 0 if reverse else tb - 8
        pick = sub == (0 if reverse else 7)

        def blk(b, carry):
            cr, ci = carry
            bb = (nblk - 1 - b) if reverse else b
            t0 = pl.multiple_of(bb * tb, tb)
            hr, hi = logscan(x_ref[pl.ds(t0, tb), :cb], x_ref[pl.ds(t0, tb), cb:])
            pr, pi = p_ref[:, :cb], p_ref[:, cb:]
            o_ref[pl.ds(t0, tb), :cb] = hr + pr * cr - pi * ci
            o_ref[pl.ds(t0, tb), cb:] = hi + pr * ci + pi * cr
            te = pl.multiple_of(t0 + edge, 8)
            ncr = jnp.sum(jnp.where(pick, o_ref[pl.ds(te, 8), :cb], 0.0), axis=0, keepdims=True)
            nci = jnp.sum(jnp.where(pick, o_ref[pl.ds(te, 8), cb:], 0.0), axis=0, keepdims=True)
            return ncr, nci

        lax.fori_loop(0, nblk, blk, (jnp.zeros((1, cb), F32), jnp.zeros((1, cb), F32)))
        if with_da:
            first = lax.broadcasted_iota(jnp.int32, (M, cb), 0) >= 1
            hpr = jnp.where(first, pltpu.roll(h_ref[:, :cb], 1, 0), 0.0)
            hpi = jnp.where(first, pltpu.roll(h_ref[:, cb:], 1, 0), 0.0)
            lr, li = o_ref[:, :cb], o_ref[:, cb:]
            da_ref[:, :cb] = _colsum(lr * hpr + li * hpi)
            da_ref[:, cb:] = _colsum(li * hpr - lr * hpi)

    blk_spec = pl.BlockSpec((M, 2 * cb), lambda j: (0, j))
    vec_spec = pl.BlockSpec((1, 2 * cb), lambda j: (0, j))
    in_specs = [blk_spec, vec_spec] + ([blk_spec] if with_da else [])
    out_specs = [blk_spec] + ([vec_spec] if with_da else [])
    out_shape = [jax.ShapeDtypeStruct((M, C2), F32)] + ([jax.ShapeDtypeStruct((1, C2), F32)] if with_da else [])
    args = [xcat, acat] + ([hcat] if with_da else [])
    res = pl.pallas_call(
        body, name=name, out_shape=out_shape, grid=(C2 // (2 * cb),), in_specs=in_specs, out_specs=out_specs,
        scratch_shapes=[pltpu.VMEM((tb, 2 * cb), F32)], compiler_params=_cparams(("parallel",)),
    )(*args)
    return res if with_da else res[0]


def _ssm_param_fn(a_re, a_im, ldt, b_re, b_im):
    lr, li = jnp.minimum(a_re, -1e-4), a_im
    dt = jnp.exp(ldt)
    e, ang = jnp.exp(lr * dt), li * dt
    ar, ai = e * jnp.cos(ang), e * jnp.sin(ang)
    den = lr * lr + li * li
    nr, ni = ar - 1.0, ai
    cr, ci = (nr * lr + ni * li) / den, (ni * lr - nr * li) / den
    return ar, ai, cr * b_re - ci * b_im, cr * b_im + ci * b_re


def _whole(shape):
    return pl.BlockSpec(shape, functools.partial(lambda nd: (0,) * nd, nd=len(shape)))


def ssm_param_fwd(name, a_re, a_im, ldt, b_re, b_im):
    def body(*refs):
        res = _ssm_param_fn(*[r[...] for r in refs[:5]])
        for ref, val in zip(refs[5:], res):
            ref[...] = val

    ins = [a_re, a_im, ldt, b_re, b_im]
    outs = [a_re, a_re, b_re, b_re]
    return pl.pallas_call(
        body, name=name, out_shape=[jax.ShapeDtypeStruct(t.shape, F32) for t in outs],
        in_specs=[_whole(t.shape) for t in ins], out_specs=[_whole(t.shape) for t in outs], compiler_params=_cparams(),
    )(*ins)


def ssm_param_bwd(name, a_re, a_im, ldt, b_re, b_im, d_ar, d_ai, d_bbr, d_bbi):
    def body(*refs):
        _, vjp = jax.vjp(_ssm_param_fn, *[r[...] for r in refs[:5]])
        res = vjp(tuple(r[...] for r in refs[5:9]))
        for ref, val in zip(refs[9:], res):
            ref[...] = val

    ins = [a_re, a_im, ldt, b_re, b_im, d_ar, d_ai, d_bbr, d_bbi]
    outs = [a_re, a_im, ldt, b_re, b_im]
    return pl.pallas_call(
        body, name=name, out_shape=[jax.ShapeDtypeStruct(t.shape, F32) for t in outs],
        in_specs=[_whole(t.shape) for t in ins], out_specs=[_whole(t.shape) for t in outs], compiler_params=_cparams(),
    )(*ins)


ANY = pl.BlockSpec(memory_space=pl.ANY)


def _place():
    x, y, c = lax.axis_index("x"), lax.axis_index("y"), lax.axis_index("c")
    chips = [(1 - x, y), (x, 1 - y), (1 - x, 1 - y)]
    return x, y, c, chips


def cast_into_slot(name, w):
    L, K, nn = w.shape
    tr = _pick(K, max(16, min(512, ROW_BUDGET // (2 * nn * 6))), SUBLANES_BF16)

    def body(w_ref, o_ref):
        o_ref[...] = w_ref[...].astype(BF16)

    return pl.pallas_call(
        body, name=name, out_shape=jax.ShapeDtypeStruct((N_CHIPS, L, K, nn), BF16), grid=(L, K // tr),
        in_specs=[pl.BlockSpec((None, tr, nn), lambda l, i: (l, i, 0))],
        out_specs=pl.BlockSpec((None, None, tr, nn), lambda l, i: (2 * lax.axis_index("x") + lax.axis_index("y"), l, i, 0)),
        compiler_params=_cparams(("parallel", "parallel")),
    )(w)


def all_gather_halves(name, buf):
    def body(in_ref, out_ref, send_sems, recv_sems, fsend_sems, frecv_sems):
        del in_ref
        x, y, c, chips = _place()
        me = 2 * x + y
        sib = (x, y, 1 - c)

        def piece(chip_idx, half):
            return out_ref.at[chip_idx, half]

        def ici(k, chip):
            return pltpu.make_async_remote_copy(src_ref=piece(me, c), dst_ref=piece(me, c), send_sem=send_sems.at[k],
                                                recv_sem=recv_sems.at[k], device_id=(chip[0], chip[1], c), device_id_type=MESH)

        sends = [ici(k, chip) for k, chip in enumerate(chips)]
        for cp in sends:
            cp.start()
        passes = []
        for k, chip in enumerate(chips):
            there = 2 * chip[0] + chip[1]
            pltpu.make_async_remote_copy(src_ref=piece(there, c), dst_ref=piece(there, c), send_sem=send_sems.at[k],
                                         recv_sem=recv_sems.at[k], device_id=(chip[0], chip[1], c), device_id_type=MESH).wait_recv()
            cp = pltpu.make_async_remote_copy(src_ref=piece(there, c), dst_ref=piece(there, c), send_sem=fsend_sems.at[k],
                                              recv_sem=frecv_sems.at[k], device_id=sib, device_id_type=MESH)
            cp.start()
            passes.append(cp)
        for k, chip in enumerate(chips):
            there = 2 * chip[0] + chip[1]
            pltpu.make_async_remote_copy(src_ref=piece(there, 1 - c), dst_ref=piece(there, 1 - c), send_sem=fsend_sems.at[k],
                                         recv_sem=frecv_sems.at[k], device_id=sib, device_id_type=MESH).wait_recv()
        for cp in sends + passes:
            cp.wait_send()

    return pl.pallas_call(
        body, name=name, out_shape=jax.ShapeDtypeStruct(buf.shape, buf.dtype), in_specs=[ANY], out_specs=ANY,
        scratch_shapes=[pltpu.SemaphoreType.DMA((3,))] * 4, input_output_aliases={0: 0},
    )(buf)


def swap_with_sibling(name, src, pick_other_half):
    shape = src.shape[1:] if pick_other_half else src.shape

    def body(src_ref, out_ref, ssem, rsem):
        x, y, c, _ = _place()
        cp = pltpu.make_async_remote_copy(src_ref=src_ref.at[1 - c] if pick_other_half else src_ref, dst_ref=out_ref,
                                          send_sem=ssem, recv_sem=rsem, device_id=(x, y, 1 - c), device_id_type=MESH)
        cp.start()
        cp.wait()

    return pl.pallas_call(
        body, name=name, out_shape=jax.ShapeDtypeStruct(shape, src.dtype), in_specs=[ANY], out_specs=ANY,
        scratch_shapes=[pltpu.SemaphoreType.DMA(()), pltpu.SemaphoreType.DMA(())],
    )(src)


def exchange_chips(name, src, per_chip):
    shape = src.shape[1:] if per_chip else src.shape

    def body(src_ref, out_ref, send_sems, recv_sems):
        x, y, c, chips = _place()
        cps = []
        for k, chip in enumerate(chips):
            s = src_ref.at[2 * chip[0] + chip[1]] if per_chip else src_ref
            cps.append(pltpu.make_async_remote_copy(src_ref=s, dst_ref=out_ref.at[k], send_sem=send_sems.at[k], recv_sem=recv_sems.at[k],
                                                    device_id=(chip[0], chip[1], c), device_id_type=MESH))
        for cp in cps:
            cp.start()
        for cp in cps:
            cp.wait()

    return pl.pallas_call(
        body, name=name, out_shape=jax.ShapeDtypeStruct((3,) + shape, src.dtype), in_specs=[ANY], out_specs=ANY,
        scratch_shapes=[pltpu.SemaphoreType.DMA((3,)), pltpu.SemaphoreType.DMA((3,))],
    )(src)


def pair_sum(name, p, got):
    _, lh, _, ks, cw = p.shape
    tr = _pick(ks, max(16, min(512, ROW_BUDGET // (2 * cw * 10))), SUBLANES_BF16)

    def body(p_ref, got_ref, s_ref, own_ref):
        j = pl.program_id(2)
        tot = p_ref[...].astype(F32) + got_ref[...].astype(F32)
        s_ref[...] = tot.astype(BF16)

        @pl.when(j == 2 * lax.axis_index("x") + lax.axis_index("y"))
        def _():
            own_ref[...] = tot

    return pl.pallas_call(
        body, name=name, grid=(lh, ks // tr, N_CHIPS),
        in_specs=[pl.BlockSpec((None, None, None, tr, cw), lambda ll, i, j: (lax.axis_index("c"), ll, j, i, 0)),
                  pl.BlockSpec((None, None, tr, cw), lambda ll, i, j: (ll, j, i, 0))],
        out_specs=[pl.BlockSpec((None, None, tr, cw), lambda ll, i, j: (j, ll, i, 0)),
                   pl.BlockSpec((None, tr, cw), lambda ll, i, j: (ll, i, 0))],
        out_shape=[jax.ShapeDtypeStruct((N_CHIPS, lh, ks, cw), BF16), jax.ShapeDtypeStruct((lh, ks, cw), F32)],
        compiler_params=_cparams(("arbitrary", "arbitrary", "arbitrary")),
    )(p, got)


def reduce_scatter(tag, gb):
    L, _, ks, cw = gb.shape
    lh = L // 2
    rh = lh * ks
    p = gb.reshape(2, lh, N_CHIPS, ks, cw)
    got = swap_with_sibling(f"rs_pair_{tag}", p, True)
    s, own = pair_sum(f"rs_pairsum_{tag}", p, got)
    parts = exchange_chips(f"rs_chips_{tag}", s, True).reshape(3 * rh, cw)
    mine = rowwise(f"rs_sum_{tag}", lambda o, a, b, c: (((o + a.astype(F32)) + b.astype(F32)) + c.astype(F32),),
                   [(own.reshape(rh, cw), cw, 0), (parts, cw, 0, 0), (parts, cw, 0, rh), (parts, cw, 0, 2 * rh)], [], [(cw, F32)], M=rh)[0]
    theirs = swap_with_sibling(f"rs_share_{tag}", mine, False)
    return mine, theirs


def all_reduce_small(buf):
    r = buf.shape[0]
    got = swap_with_sibling("ar_pair", buf, False)
    chip = rowwise("ar_pairsum", lambda a, b: (a + b,), [(buf, LANES, 0), (got, LANES, 0)], [], [(LANES, F32)], M=r)[0]
    parts = exchange_chips("ar_chips", chip, False).reshape(3 * r, LANES)
    return rowwise("ar_sum", lambda o, fx, fy, fxy: ((o + fy) + (fx + fxy),),
                   [(chip, LANES, 0), (parts, LANES, 0, 0), (parts, LANES, 0, r), (parts, LANES, 0, 2 * r)], [], [(LANES, F32)], M=r)[0]


def _adam_fn(w, g, m, v):
    m = ADAM_B1 * m + (1.0 - ADAM_B1) * g
    v = ADAM_B2 * v + (1.0 - ADAM_B2) * (g * g)
    m_hat = m / (1.0 - ADAM_B1 ** ADAM_STEP)
    v_hat = v / (1.0 - ADAM_B2 ** ADAM_STEP)
    return -ADAM_LR * (m_hat / (jnp.sqrt(v_hat) + ADAM_EPS) + ADAM_WD * w), m, v


def adamw(name, w, g, m, v):
    r, cw = w.shape
    return rowwise(name, _adam_fn, [(t, cw, 0) for t in (w, g, m, v)], [], [(cw, F32)] * 3, M=r)


def adamw_halves(name, w, m, v, mine, theirs):
    r, cw = w.shape
    rh = r // 2
    tr = _pick(rh, max(8, min(512, ROW_BUDGET // (2 * cw * 36))), 8)
    nb = rh // tr

    def body(w_ref, m_ref, v_ref, mine_ref, theirs_ref, g_ref, d_ref, nm_ref, nv_ref):
        g = jnp.where(pl.program_id(0) == lax.axis_index("c"), mine_ref[...], theirs_ref[...])
        g_ref[...] = g
        d_ref[...], nm_ref[...], nv_ref[...] = _adam_fn(w_ref[...], g, m_ref[...], v_ref[...])

    full = pl.BlockSpec((tr, cw), lambda h, i: (h * nb + i, 0))
    half = pl.BlockSpec((tr, cw), lambda h, i: (i, 0))
    return pl.pallas_call(
        body, name=name, grid=(2, nb), in_specs=[full, full, full, half, half], out_specs=[full] * 4,
        out_shape=[jax.ShapeDtypeStruct((r, cw), F32)] * 4, compiler_params=_cparams(("parallel", "parallel")),
    )(w, m, v, mine, theirs)


class Dims:
    def __init__(self, x, g_q, g_kv, g_out_mla, g_out_ssm, g_out_dil, ff):
        self.M, self.D = x.shape[-2], x.shape[-1]
        self.QL, self.KVL = g_q.shape[-1], g_kv.shape[-1]
        self.MW, self.SW, self.DW = g_out_mla.shape[-1], g_out_ssm.shape[-1], g_out_dil.shape[-1]
        self.H = self.MW // MLA_V
        self.FF = ff
        self.G = self.SW // SSM_GROUP
        self.C = self.G * SSM_STATE
        self.o_cq, self.o_u = 0, self.QL
        self.o_qd = self.o_u + self.SW
        self.o_kd = self.o_qd + self.DW
        self.o_vd = self.o_kd + self.DW
        self.o_ckv = self.o_vd + self.DW
        self.o_kr = self.o_ckv + self.KVL
        self.PW = -(-(self.o_kr + MLA_ROPE) // (4 * LANES)) * (4 * LANES)
        assert self.o_u % self.SW == 0 and self.o_qd % LANES == 0 and self.o_ckv % self.KVL == 0 and self.o_kr % LANES == 0
        assert self.H % 2 == 0 and self.DW % LANES == 0 and self.C % LANES == 0
        self.QW = self.H * (MLA_NOPE + MLA_ROPE)
        self.KVW = self.H * (MLA_NOPE + MLA_V)
        sizes = [self.QL, self.KVL, MLA_ROPE, self.SW, self.DW, self.DW, self.DW]
        starts = np.concatenate([[0], np.cumsum(sizes)[:-1]])
        self.ref_cols = {n: (int(s), int(z)) for n, s, z in zip(["cq", "ckv", "kr", "u", "qd", "kd", "vd"], starts, sizes)}
        self.INW = int(sum(sizes))
        self.new_order = ["cq", "u", "qd", "kd", "vd", "ckv", "kr"]


def _regroup_in(dm, w):
    parts = [w[..., dm.ref_cols[n][0]:dm.ref_cols[n][0] + dm.ref_cols[n][1]] for n in dm.new_order]
    pad = dm.PW - dm.INW
    return jnp.concatenate(parts + [jnp.zeros(w.shape[:-1] + (pad,), w.dtype)], axis=-1)


def _ungroup_in(dm, w):
    off, pieces = 0, {}
    for n in dm.new_order:
        pieces[n] = w[..., off:off + dm.ref_cols[n][1]]
        off += dm.ref_cols[n][1]
    return jnp.concatenate([pieces[n] for n in ["cq", "ckv", "kr", "u", "qd", "kd", "vd"]], axis=-1)


def _split_heads(w, h, d1):
    t = w.reshape(w.shape[:-1] + (h, -1))
    return jnp.concatenate([t[..., :d1].reshape(w.shape[:-1] + (-1,)), t[..., d1:].reshape(w.shape[:-1] + (-1,))], axis=-1)


def _merge_heads(w, h, d1):
    a = w[..., :h * d1].reshape(w.shape[:-1] + (h, d1))
    b = w[..., h * d1:].reshape(w.shape[:-1] + (h, -1))
    return jnp.concatenate([a, b], axis=-1).reshape(w.shape[:-1] + (-1,))


def _cat_cols(re, im):
    r, c = re.shape
    return jnp.stack([re.reshape(r, c // LANES, LANES), im.reshape(r, c // LANES, LANES)], axis=2).reshape(r, 2 * c)


def _uncat_cols(cat):
    r, c2 = cat.shape
    t = cat.reshape(r, c2 // (2 * LANES), 2, LANES)
    return t[:, :, 0].reshape(r, c2 // 2), t[:, :, 1].reshape(r, c2 // 2)


def _block_diag(t, g):
    _, a, b = t.shape
    eye = jnp.eye(g, dtype=bool)[:, None, :, None]
    return jnp.where(eye, t[:, :, None, :], 0).reshape(g * a, g * b)


def _diag_blocks(m, g):
    a, b = m.shape[0] // g, m.shape[1] // g
    eye = jnp.eye(g, dtype=m.dtype)[:, None, :, None]
    return jnp.sum(m.reshape(g, a, g, b) * eye, axis=2)


def _rope_tables(dm):
    half = MLA_ROPE // 2
    inv_freq = ROPE_THETA ** (-jnp.arange(half, dtype=F32) / half)
    ang = jnp.arange(dm.M, dtype=F32)[:, None] * inv_freq[None, :]
    cos = jnp.concatenate([jnp.cos(ang), jnp.cos(ang)], axis=1)
    sin = jnp.concatenate([-jnp.sin(ang), jnp.sin(ang)], axis=1)
    return jnp.tile(cos, (1, dm.H)), jnp.tile(sin, (1, dm.H)), jnp.tile(cos, (1, LANES // MLA_ROPE)), jnp.tile(sin, (1, LANES // MLA_ROPE))


def _rope(x, cos, sin):
    return x * cos + _swap_halves(x, MLA_ROPE // 2) * sin


def _rope_t(d, cos, sin):
    return d * cos + _swap_halves(d * sin, MLA_ROPE // 2)


def _ssm_layer_params(dm, a_re, a_im, log_dt, b_re, b_im):
    flat = lambda t: t.reshape(1, dm.C)
    ldt = jnp.repeat(log_dt, SSM_STATE).reshape(1, dm.C)
    bt = lambda t: jnp.transpose(t, (2, 0, 1)).reshape(SSM_GROUP, dm.C)
    return flat(a_re), flat(a_im), ldt, bt(b_re), bt(b_im)


def layer_forward(dm, l, x, lw, sp, tabs):
    M, D = dm.M, dm.D
    n = lambda s: f"{s}_l{l}"
    sv = {"x_in": x}
    h1 = rms_fwd(n("rms_mix"), x, D, 0, lw["g_mix"], M=M)
    proj = matmul(n("in_proj"), h1, lw["w_in"], "nn", b_lead=l, M=M, N=dm.PW, K=D)
    sv.update(h1=h1, proj=proj)
    cqn = rms_fwd(n("rms_q"), proj, dm.QL, dm.o_cq, lw["g_q"], M=M)
    q = matmul(n("q_up"), cqn, lw["w_uq"], "nn", b_lead=l, M=M, N=dm.QW, K=dm.QL)
    ckvn = rms_fwd(n("rms_kv"), proj, dm.KVL, dm.o_ckv, lw["g_kv"], M=M)
    kv = matmul(n("kv_up"), ckvn, lw["w_ukv"], "nn", b_lead=l, M=M, N=dm.KVW, K=dm.KVL, out_dtype=BF16)
    cosq, sinq, cosk, sink = tabs
    nw = dm.H * MLA_NOPE

    def rope_fn(qb, kb, cq, sq, ck, sk):
        return jnp.concatenate([qb[:, :nw], _rope(qb[:, nw:], cq, sq)], axis=1), _rope(kb, ck, sk)

    pw = dm.H * MLA_ROPE
    q_bf, kpe = rowwise(n("rope"), rope_fn, [(q, dm.QW, 0), (proj, LANES, dm.o_kr), (cosq, pw, 0), (sinq, pw, 0), (cosk, LANES, 0), (sink, LANES, 0)],
                        [], [(dm.QW, BF16), (LANES, BF16)], M=M)
    mla_scale = (MLA_NOPE + MLA_ROPE) ** -0.5
    o_mla, lse_mla = attention_fwd(n("mla_fwd"), q_bf, 0, kv, 0, kv, nw, da=MLA_NOPE, dv=MLA_V, pairs=dm.H // 2, scale=mla_scale,
                                   kind="causal", M=M, qb=q_bf, qb_off=nw, kb=kpe)
    sv.update(cqn=cqn, ckvn=ckvn, q_bf=q_bf, kv=kv, kpe=kpe, o_mla=o_mla, lse_mla=lse_mla)
    bu = matmul(n("ssm_bu"), proj, sp["bcat"], "nn", M=M, N=2 * dm.C, K=dm.SW, a_off=(0, dm.o_u))
    hcat = ssm_scan(n("ssm_scan"), bu, sp["acat"], M=M)
    ylin = matmul(n("ssm_y"), hcat, sp["ccat"], "nn", M=M, N=dm.SW, K=2 * dm.C)
    yg = rowwise(n("ssm_gelu"), lambda y, u, d: (_gelu(y + d * u),), [(ylin, dm.SW, 0), (proj, dm.SW, dm.o_u)], [lw["d_skip"]],
                 [(dm.SW, BF16)], M=M)[0]
    z = matmul(n("ssm_glu"), yg, lw["w_glu"], "nn", b_lead=l, M=M, N=2 * dm.SW, K=dm.SW)
    sw = dm.SW

    def glu_fn(zb, b):
        zz = zb + b
        return (zz[:, :sw] * jax.nn.sigmoid(zz[:, sw:]),)

    o_ssm = rowwise(n("ssm_gate"), glu_fn, [(z, 2 * sw, 0)], [lw["b_glu"]], [(sw, F32)], M=M)[0]
    sv.update(hcat=hcat, ylin=ylin, yg=yg, z=z, o_ssm=o_ssm)
    o_dil, lse_dil = attention_fwd(n("dil_fwd"), proj, dm.o_qd, proj, dm.o_kd, proj, dm.o_vd, da=DIL_HEAD, dv=DIL_HEAD, pairs=dm.DW // LANES,
                                   scale=DIL_HEAD ** -0.5, kind="dilated", M=M)
    sv.update(o_dil=o_dil, lse_dil=lse_dil)
    yn = rowwise(n("out_norm"), lambda a, b, c, ga, gb, gc: (jnp.concatenate([_rms(a, ga), _rms(b, gb), _rms(c, gc)], axis=1),),
                 [(o_mla, dm.MW, 0), (o_ssm, dm.SW, 0), (o_dil, dm.DW, 0)], [lw["g_out_mla"], lw["g_out_ssm"], lw["g_out_dil"]],
                 [(D, BF16)], M=M)[0]
    x_mid = matmul(n("out_proj"), yn, lw["w_o"], "nn", w=("rows", l, D // N_CHIPS), M=M, N=D, K=D, add=x)
    h2 = rms_fwd(n("rms_ffn"), x_mid, D, 0, lw["g_ffn"], M=M)
    ffs = dm.FF // N_CHIPS
    gate = matmul(n("ffn_gate"), h2, lw["w_gate"], "nn", w=("cols", l, ffs), M=M, N=dm.FF, K=D)
    up = matmul(n("ffn_up"), h2, lw["w_up"], "nn", w=("cols", l, ffs), M=M, N=dm.FF, K=D)
    act = rowwise(n("ffn_act"), lambda g, u: (g * jax.nn.sigmoid(g) * u,), [(gate, dm.FF, 0), (up, dm.FF, 0)], [], [(dm.FF, BF16)], M=M)[0]
    x_out = matmul(n("ffn_down"), act, lw["w_down"], "nn", w=("rows", l, ffs), M=M, N=D, K=dm.FF, add=x_mid)
    sv.update(yn=yn, x_mid=x_mid, h2=h2, gate=gate, up=up, act=act)
    return x_out, sv


def layer_backward(dm, l, dx, lw, sp, tabs, sv, gb):
    M, D = dm.M, dm.D
    n = lambda s: f"{s}_l{l}"
    g = {}
    gb = dict(gb)
    ffs = dm.FF // N_CHIPS
    dact = matmul(n("ffn_down_dx"), dx, lw["w_down"], "nt", w=("rows", l, ffs), M=M, N=dm.FF, K=D)
    gb["w_down"] = matmul(n("ffn_down_dw"), sv["act"], dx, "tn", M=dm.FF, N=D, K=M, out_dtype=BF16, into=("plain", gb["w_down"], l))

    def act_bwd(gb, ub, db):
        _, vjp = jax.vjp(lambda a, b: a * jax.nn.sigmoid(a) * b, gb, ub)
        return vjp(db)

    dgate, dup = rowwise(n("ffn_act_bwd"), act_bwd, [(sv["gate"], dm.FF, 0), (sv["up"], dm.FF, 0), (dact, dm.FF, 0)], [],
                         [(dm.FF, BF16), (dm.FF, BF16)], M=M)
    dh2 = matmul(n("ffn_gate_dx"), dgate, lw["w_gate"], "nt", w=("cols", l, ffs), M=M, N=D, K=dm.FF)
    dh2 = matmul(n("ffn_up_dx"), dup, lw["w_up"], "nt", w=("cols", l, ffs), M=M, N=D, K=dm.FF, add=dh2)
    gb["w_gate"] = matmul(n("ffn_gate_dw"), sv["h2"], dgate, "tn", M=D, N=dm.FF, K=M, out_dtype=BF16, into=("cols", gb["w_gate"], l, ffs))
    gb["w_up"] = matmul(n("ffn_up_dw"), sv["h2"], dup, "tn", M=D, N=dm.FF, K=M, out_dtype=BF16, into=("cols", gb["w_up"], l, ffs))
    dx_mid, g["g_ffn"] = rms_bwd(n("rms_ffn_bwd"), sv["x_mid"], D, 0, lw["g_ffn"], dh2, dx, M=M)
    dyn = matmul(n("out_proj_dx"), dx_mid, lw["w_o"], "nt", w=("rows", l, D // N_CHIPS), M=M, N=D, K=D)
    gb["w_o"] = matmul(n("out_proj_dw"), sv["yn"], dx_mid, "tn", M=D, N=D, K=M, out_dtype=BF16, into=("plain", gb["w_o"], l))
    mw, sw, dw = dm.MW, dm.SW, dm.DW

    def out_norm_bwd(a, b, c, dy, ga, gb, gc):
        res, sums = [], []
        for t, gg, lo, hi in ((a, ga, 0, mw), (b, gb, mw, mw + sw), (c, gc, mw + sw, mw + sw + dw)):
            _, vjp = jax.vjp(_rms, t, gg)
            dt, dg = vjp(dy[:, lo:hi])
            res.append(dt)
            sums.append(dg)
        return res + sums

    do_mla, do_ssm, do_dil, g["g_out_mla"], g["g_out_ssm"], g["g_out_dil"] = rowwise(
        n("out_norm_bwd"), out_norm_bwd, [(sv["o_mla"], mw, 0), (sv["o_ssm"], sw, 0), (sv["o_dil"], dw, 0), (dyn, D, 0)],
        [lw["g_out_mla"], lw["g_out_ssm"], lw["g_out_dil"]], [(mw, F32), (sw, F32), (dw, F32)], [mw, sw, dw], M=M)
    proj = sv["proj"]
    dqd, dkd, dvd = attention_bwd(n("dil_bwd"), proj, dm.o_qd, proj, dm.o_kd, proj, dm.o_vd, sv["o_dil"], do_dil, sv["lse_dil"],
                                  da=DIL_HEAD, dv=DIL_HEAD, pairs=dw // LANES, scale=DIL_HEAD ** -0.5, kind="dilated", M=M)
    def glu_bwd(zb, db, b):
        _, vjp = jax.vjp(lambda zz, bb: (zz + bb)[:, :sw] * jax.nn.sigmoid((zz + bb)[:, sw:]), zb, b)
        return vjp(db)

    dz, g["b_glu"] = rowwise(n("ssm_gate_bwd"), glu_bwd, [(sv["z"], 2 * sw, 0), (do_ssm, sw, 0)], [lw["b_glu"]], [(2 * sw, BF16)], [2 * sw], M=M)
    dyg = matmul(n("ssm_glu_dx"), dz, lw["w_glu"], "nt", b_lead=l, M=M, N=sw, K=2 * sw)
    g["w_glu"] = matmul(n("ssm_glu_dw"), sv["yg"], dz, "tn", M=sw, N=2 * sw, K=M, out_dtype=BF16)

    def gelu_bwd(y, u, dy, d):
        _, vjp = jax.vjp(lambda yy, uu, dd: _gelu(yy + dd * uu), y, u, d)
        return vjp(dy)

    dylin, du1, g["d_skip"] = rowwise(n("ssm_gelu_bwd"), gelu_bwd, [(sv["ylin"], sw, 0), (proj, sw, dm.o_u), (dyg, sw, 0)], [lw["d_skip"]],
                                      [(sw, BF16), (sw, F32)], [sw], M=M)
    seed = matmul(n("ssm_y_dx"), dylin, sp["ccat"], "nt", M=M, N=2 * dm.C, K=sw)
    d_ccat = matmul(n("ssm_y_dw"), sv["hcat"], dylin, "tn", M=2 * dm.C, N=sw, K=M)
    lam, d_acat = ssm_scan(n("ssm_scan_bwd"), seed, sp["acat_conj"], M=M, reverse=True, hcat=sv["hcat"])
    du = matmul(n("ssm_bu_dx"), lam, sp["bcat"], "nt", M=M, N=sw, K=2 * dm.C, add=du1, out_dtype=BF16)
    d_bcat = matmul(n("ssm_bu_dw"), proj, lam, "tn", M=sw, N=2 * dm.C, K=M, a_off=(0, dm.o_u))
    g["ssm_raw"] = (d_acat, d_bcat, d_ccat)
    nw = dm.H * MLA_NOPE
    dqn, dkn, dv_, dqp, dkp = attention_bwd(n("mla_bwd"), sv["q_bf"], 0, sv["kv"], 0, sv["kv"], nw, sv["o_mla"], do_mla, sv["lse_mla"],
                                            da=MLA_NOPE, dv=MLA_V, pairs=dm.H // 2, scale=(MLA_NOPE + MLA_ROPE) ** -0.5, kind="causal", M=M,
                                            qb=sv["q_bf"], qb_off=nw, kb=sv["kpe"])
    cosq, sinq, cosk, sink = tabs
    pw = dm.H * MLA_ROPE
    dqp_u, dkr = rowwise(n("rope_bwd"), lambda a, b, cq, sq, ck, sk: (_rope_t(a, cq, sq), _rope_t(b, ck, sk)),
                         [(dqp, pw, 0), (dkp, LANES, 0), (cosq, pw, 0), (sinq, pw, 0), (cosk, LANES, 0), (sink, LANES, 0)], [],
                         [(pw, BF16), (LANES, BF16)], M=M)
    dq = jnp.concatenate([dqn, dqp_u], axis=1)
    dkv = jnp.concatenate([dkn.astype(BF16), dv_.astype(BF16)], axis=1)
    dcqn = matmul(n("q_up_dx"), dq, lw["w_uq"], "nt", b_lead=l, M=M, N=dm.QL, K=dm.QW)
    g["w_uq"] = matmul(n("q_up_dw"), sv["cqn"], dq, "tn", M=dm.QL, N=dm.QW, K=M, out_dtype=BF16)
    dckvn = matmul(n("kv_up_dx"), dkv, lw["w_ukv"], "nt", b_lead=l, M=M, N=dm.KVL, K=dm.KVW)
    g["w_ukv"] = matmul(n("kv_up_dw"), sv["ckvn"], dkv, "tn", M=dm.KVL, N=dm.KVW, K=M, out_dtype=BF16)
    dcq, g["g_q"] = rms_bwd(n("rms_q_bwd"), proj, dm.QL, dm.o_cq, lw["g_q"], dcqn, M=M, out_dtype=BF16)
    dckv, g["g_kv"] = rms_bwd(n("rms_kv_bwd"), proj, dm.KVL, dm.o_ckv, lw["g_kv"], dckvn, M=M, out_dtype=BF16)
    pad = jnp.zeros((M, dm.PW - dm.o_kr - LANES), BF16)
    dproj = jnp.concatenate([dcq, du, dqd, dkd.astype(BF16), dvd.astype(BF16), dckv, dkr, pad], axis=1)
    dh1 = matmul(n("in_proj_dx"), dproj, lw["w_in"], "nt", b_lead=l, M=M, N=D, K=dm.PW)
    g["w_in"] = matmul(n("in_proj_dw"), sv["h1"], dproj, "tn", M=D, N=dm.PW, K=M, out_dtype=BF16)
    dx_in, g["g_mix"] = rms_bwd(n("rms_mix_bwd"), sv["x_in"], D, 0, lw["g_mix"], dh1, dx_mid, M=M)
    return dx_in, g, gb


def local_step(dm, x, target, full, small):
    L = small["g_mix"].shape[0]
    tabs = _rope_tables(dm)
    lws, sps, raws = [], [], []
    for l in range(L):
        lw = dict(full)
        for k in ("g_mix", "g_q", "g_kv", "b_glu", "g_out_mla", "g_out_ssm", "g_out_dil", "g_ffn"):
            lw[k] = small[k][l].reshape(1, -1)
        lw["d_skip"] = small["d_skip"][l].reshape(1, dm.SW)
        raw = _ssm_layer_params(dm, small["a_re"][l], small["a_im"][l], small["log_dt"][l], small["b_re"][l], small["b_im"][l])
        ar, ai, bbr, bbi = ssm_param_fwd(f"ssm_param_l{l}", *raw)
        g_ = dm.G
        bd = lambda t: _block_diag(jnp.transpose(t.reshape(SSM_GROUP, g_, SSM_STATE), (1, 0, 2)), g_)
        cd = lambda t: _block_diag(jnp.transpose(t, (0, 2, 1)), g_)
        cre, cim = cd(small["c_re"][l]), cd(small["c_im"][l])
        sp = {"acat": _cat_cols(ar, ai), "acat_conj": _cat_cols(ar, -ai),
              "bcat": _cat_cols(bd(bbr), bd(bbi)).astype(BF16),
              "ccat": _cat_cols(cre.T, -cim.T).T.astype(BF16)}
        lws.append(lw)
        sps.append(sp)
        raws.append(raw)
    saved = []
    h = x
    for l in range(L):
        h, sv = layer_forward(dm, l, h, lws[l], sps[l], tabs)
        saved.append(sv)
    D = dm.D

    def loss_fn(xb, tb, gb):
        y, vjp = jax.vjp(_rms, xb, gb)
        err = y - tb
        dxb, dg = vjp(err * (1.0 / D))
        part = 0.5 * jnp.sum(jnp.mean(err * err, axis=-1, keepdims=True), axis=0, keepdims=True)
        lane = lax.broadcasted_iota(jnp.int32, (1, LANES), 1)
        return dxb, dg, jnp.where(lane == 0, part, 0.0)

    dx, g_final, loss_part = rowwise("loss", loss_fn, [(h, D, 0), (target, D, 0)], [small["g_final"].reshape(1, D)], [(D, F32)], [D, LANES], M=dm.M)
    grads = [None] * L
    ffs, ds = dm.FF // N_CHIPS, D // N_CHIPS
    gb = {"w_gate": jax.ShapeDtypeStruct((L, N_CHIPS, D, ffs), BF16), "w_up": jax.ShapeDtypeStruct((L, N_CHIPS, D, ffs), BF16),
          "w_down": jax.ShapeDtypeStruct((L, dm.FF, D), BF16), "w_o": jax.ShapeDtypeStruct((L, D, D), BF16)}
    for l in reversed(range(L)):
        dx, g, gb = layer_backward(dm, l, dx, lws[l], sps[l], tabs, saved[l], gb)
        d_acat, d_bcat, d_ccat = g.pop("ssm_raw")
        d_ar, d_ai = _uncat_cols(d_acat)
        dbr, dbi = _uncat_cols(d_bcat)
        g_ = dm.G
        to_rows = lambda t: jnp.transpose(_diag_blocks(t, g_), (1, 0, 2)).reshape(SSM_GROUP, dm.C)
        da_re, da_im, dldt, db_re, db_im = ssm_param_bwd(f"ssm_param_bwd_l{l}", *raws[l], d_ar, d_ai, to_rows(dbr), to_rows(dbi))
        dcr, dci = _uncat_cols(d_ccat.T)
        g["a_re"], g["a_im"] = da_re.reshape(g_, SSM_STATE), da_im.reshape(g_, SSM_STATE)
        g["log_dt"] = jnp.sum(dldt.reshape(g_, SSM_STATE), axis=1)
        from_rows = lambda t: jnp.transpose(t.reshape(SSM_GROUP, g_, SSM_STATE), (1, 2, 0))
        g["b_re"], g["b_im"] = from_rows(db_re), from_rows(db_im)
        g["c_re"] = jnp.transpose(_diag_blocks(dcr.T, g_), (0, 2, 1))
        g["c_im"] = -jnp.transpose(_diag_blocks(dci.T, g_), (0, 2, 1))
        g["d_skip"] = g["d_skip"].reshape(g_, SSM_GROUP)
        grads[l] = g
    gb["w_down"] = gb["w_down"].reshape(L, N_CHIPS, ffs, D)
    gb["w_o"] = gb["w_o"].reshape(L, N_CHIPS, ds, D)
    return loss_part, dx, g_final, grads, gb


SHARDED = ["w_in", "w_uq", "w_ukv", "w_glu", "w_o", "w_gate", "w_up", "w_down"]
ROW_SHARDED = ("w_o", "w_down")
SMALL = ["g_mix", "g_q", "g_kv", "a_re", "a_im", "b_re", "b_im", "c_re", "c_im", "d_skip", "log_dt", "b_glu",
         "g_out_mla", "g_out_ssm", "g_out_dil", "g_ffn", "g_final"]
ORDER = ["g_mix", "w_in", "g_q", "w_uq", "g_kv", "w_ukv", "a_re", "a_im", "b_re", "b_im", "c_re", "c_im", "d_skip", "log_dt",
         "w_glu", "b_glu", "g_out_mla", "g_out_ssm", "g_out_dil", "w_o", "g_ffn", "w_gate", "w_up", "w_down", "g_final"]


def gather_weights(shards):
    got = {}
    for k in SHARDED:
        L, K, nn = shards[k].shape
        assert L % 2 == 0
        buf = cast_into_slot(f"cast_{k}", shards[k])
        got[k] = all_gather_halves(f"ag_{k}", buf.reshape(N_CHIPS, 2, L // 2, K, nn)).reshape(N_CHIPS, L, K, nn)
    return got


def _whole_cols(got):
    j, L, K, nn = got.shape
    return jnp.transpose(got, (1, 2, 0, 3)).reshape(L, K, j * nn)


def _shard_cols(per_layer):
    g = jnp.stack(per_layer)
    L, K, N = g.shape
    return jnp.transpose(g.reshape(L, K, N_CHIPS, N // N_CHIPS), (0, 2, 1, 3))


def kernel(x, g_mix, w_in, g_q, w_uq, g_kv, w_ukv, a_re, a_im, b_re, b_im, c_re, c_im, d_skip, log_dt, w_glu, b_glu, g_out_mla, g_out_ssm, g_out_dil, w_o, g_ffn, w_gate, w_up, w_down, g_final, loss_target, m_g_mix, m_w_in, m_g_q, m_w_uq, m_g_kv, m_w_ukv, m_a_re, m_a_im, m_b_re, m_b_im, m_c_re, m_c_im, m_d_skip, m_log_dt, m_w_glu, m_b_glu, m_g_out_mla, m_g_out_ssm, m_g_out_dil, m_w_o, m_g_ffn, m_w_gate, m_w_up, m_w_down, m_g_final, v_g_mix, v_w_in, v_g_q, v_w_uq, v_g_kv, v_w_ukv, v_a_re, v_a_im, v_b_re, v_b_im, v_c_re, v_c_im, v_d_skip, v_log_dt, v_w_glu, v_b_glu, v_g_out_mla, v_g_out_ssm, v_g_out_dil, v_w_o, v_g_ffn, v_w_gate, v_w_up, v_w_down, v_g_final):
    args = locals()
    w = {k: args[k] for k in ORDER}
    mom = {k: args["m_" + k] for k in ORDER}
    var = {k: args["v_" + k] for k in ORDER}
    dm = Dims(x, g_q, g_kv, g_out_mla, g_out_ssm, g_out_dil, w_gate.shape[-1] * N_CHIPS)
    L = g_mix.shape[0]

    full = gather_weights({k: w[k] for k in SHARDED})
    full["w_in"] = _regroup_in(dm, _whole_cols(full["w_in"]))
    full["w_uq"] = _split_heads(_whole_cols(full["w_uq"]), dm.H, MLA_NOPE)
    full["w_ukv"] = _split_heads(_whole_cols(full["w_ukv"]), dm.H, MLA_NOPE)
    full["w_glu"] = _whole_cols(full["w_glu"])

    loss_part, dx, g_final_part, grads, gb = local_step(dm, x.reshape(dm.M, dm.D), loss_target.reshape(dm.M, dm.D), full, {k: w[k] for k in SMALL})

    gb["w_in"] = _shard_cols([_ungroup_in(dm, grads[l]["w_in"]) for l in range(L)])
    gb["w_uq"] = _shard_cols([_merge_heads(grads[l]["w_uq"], dm.H, MLA_NOPE) for l in range(L)])
    gb["w_ukv"] = _shard_cols([_merge_heads(grads[l]["w_ukv"], dm.H, MLA_NOPE) for l in range(L)])
    gb["w_glu"] = _shard_cols([grads[l]["w_glu"] for l in range(L)])
    halves = {k: reduce_scatter(k, gb[k]) for k in SHARDED}
    gsum = {}
    small_names = [k for k in SMALL if k != "g_final"]
    pieces = [jnp.stack([grads[l][k] for l in range(L)]).reshape(-1) for k in small_names] + [g_final_part.reshape(-1), loss_part.reshape(-1)]
    sizes = [int(p.shape[0]) for p in pieces]
    total = sum(sizes)
    rows = -(-total // (LANES * 16)) * 16
    pack = lambda ps: jnp.concatenate(ps + [jnp.zeros((rows * LANES - total,), F32)]).reshape(rows, LANES)
    red = all_reduce_small(pack(pieces))
    flat = red.reshape(-1)
    offs = np.concatenate([[0], np.cumsum(sizes)]).astype(int)
    names = small_names + ["g_final"]
    for i, k in enumerate(names):
        gsum[k] = flat[offs[i]:offs[i + 1]].reshape(w[k].shape)
    loss = flat[offs[len(names)]]

    delta, new_m, new_v = {}, {}, {}
    for k in SHARDED:
        shp = w[k].shape
        two_d = lambda t: t.reshape(shp[0] * shp[1], shp[2])
        g_, d_, m_, v_ = adamw_halves(f"adam_{k}", two_d(w[k]), two_d(mom[k]), two_d(var[k]), *halves[k])
        gsum[k], delta[k], new_m[k], new_v[k] = g_.reshape(shp), d_.reshape(shp), m_.reshape(shp), v_.reshape(shp)
    sm_sizes = sizes[:len(names)]
    sm_total = sum(sm_sizes)
    packs = lambda d: jnp.concatenate([d[k].reshape(-1) for k in names] + [jnp.zeros((rows * LANES - sm_total,), F32)]).reshape(rows, LANES)
    gs = jnp.concatenate([flat[:sm_total], jnp.zeros((rows * LANES - sm_total,), F32)]).reshape(rows, LANES)
    d_, m_, v_ = adamw("adam_small", packs(w), gs, packs(mom), packs(var))
    for i, k in enumerate(names):
        sl = slice(offs[i], offs[i + 1])
        delta[k], new_m[k], new_v[k] = (t.reshape(-1)[sl].reshape(w[k].shape) for t in (d_, m_, v_))

    return (loss, dx.reshape(x.shape), *[gsum[k] for k in ORDER], *[delta[k] for k in ORDER],
            *[new_m[k] for k in ORDER], *[new_v[k] for k in ORDER])
```

```python
import functools
import math

import numpy as np
import jax
import jax.numpy as jnp
from jax import lax
from jax.experimental import pallas as pl
from jax.experimental.pallas import tpu as pltpu

F32 = jnp.float32
BF16 = jnp.bfloat16
MESH = pl.DeviceIdType.MESH

NORM_EPS = 1e-6
MLA_NOPE, MLA_ROPE, MLA_V = 128, 64, 128
SSM_GROUP, SSM_STATE = 16, 64
DIL_HEAD = 64
DIL_PATTERNS = ((128, 1), (512, 4), (2048, 16))
ROPE_THETA = 10000.0
ADAM_LR, ADAM_B1, ADAM_B2, ADAM_EPS, ADAM_WD, ADAM_STEP = 0.001, 0.9, 0.999, 1e-08, 0.01, 10
N_CHIPS = 4

LANES = 128
SUBLANES_BF16 = 16
VMEM_LIMIT = 56 * 1024 * 1024
ROW_BUDGET = 20 * 1024 * 1024
MM_BUDGET = 40 * 1024 * 1024
NEG = -1e30


def _cparams(sem=None):
    return pltpu.CompilerParams(dimension_semantics=sem, vmem_limit_bytes=VMEM_LIMIT)


def _pick(n, cap, q, off=0):
    best = None
    for d in range(q, min(n, cap) + 1, q):
        if n % d == 0 and off % d == 0:
            best = d
    if best is None or (best * 4 <= min(cap, n) and n <= 3072 and off % n == 0):
        assert off % n == 0, (n, off)
        return n
    return best


_DOT_DIMS = {"nn": (((1,), (0,)), ((), ())), "nt": (((1,), (1,)), ((), ())), "tn": (((0,), (0,)), ((), ()))}


def _divs(n, q, within=None, off=0):
    return [d for d in range(q, n + 1, q) if n % d == 0 and off % d == 0 and (within is None or within % d == 0)] or [n]


def _mm_tiles(M, N, K, tms, tns, tks, ab, bb, ob):
    best = None
    for tk in tks:
        nk = K // tk
        for tn in tns:
            for tm in tms:
                if 2 * (tm * tk * ab + tk * tn * bb + tm * tn * ob) + tm * tn * 4 * (2 if nk > 1 else 1) > MM_BUDGET:
                    continue
                steps = (M // tm) * (N // tn) * nk
                hbm = M * K * ab * (1 if nk == 1 else N // tn) + K * N * bb * (M // tm) + M * N * ob
                cost = steps * 0.35e-6 + hbm / 3.0e12 + (nk - 1) * M * N * 12 / 4.0e12
                if best is None or cost < best[0]:
                    best = (cost, tm, tn, tk)
    assert best is not None, (M, N, K)
    return best[1:]


def matmul(name, a, b, mode, *, M, N, K, a_off=(0, 0), b_off=(0, 0), b_lead=None, w=None, add=None, out_dtype=F32, into=None):
    tn_mode = mode == "tn"
    a_ro, a_co = (a_off[1], a_off[0]) if tn_mode else a_off
    b_no, b_ko = b_off if mode == "nt" else (b_off[1], b_off[0])
    n_within = k_within = m_within = None
    if w is not None:
        kind, shard = w
        rows_within, cols_within = (K // 2 if mode == "nn" else N // 2, shard) if kind == "cols" else (shard, (N if mode == "nn" else K) // 2)
        k_within, n_within = (rows_within, cols_within) if mode == "nn" else (cols_within, rows_within)
    if into is not None:
        m_within, n_within = (M // 2, into[1]) if into[0] == "cols" else (into[1], N // 2)
    tms = [d for d in _divs(M, 128 if tn_mode else SUBLANES_BF16, m_within, a_ro) if d <= 1408]
    tns = [d for d in _divs(N, LANES, n_within, b_no) if d <= 2048]
    tks = _divs(K, SUBLANES_BF16 if tn_mode else LANES, k_within, math.gcd(a_co, b_ko))
    ob = jnp.dtype(out_dtype).itemsize + (add.dtype.itemsize if add is not None else 0)
    tm, tn, tk = _mm_tiles(M, N, K, tms, tns, tks, a.dtype.itemsize, b.dtype.itemsize, ob)
    nk = K // tk
    dn = _DOT_DIMS[mode]

    if tn_mode:
        a_spec = pl.BlockSpec((tk, tm), lambda i, j, k: (k + a_co // tk, i + a_ro // tm))
    else:
        a_spec = pl.BlockSpec((tm, tk), lambda i, j, k: (i + a_ro // tm, k + a_co // tk))
    b_blk = (tn, tk) if mode == "nt" else (tk, tn)
    if w is not None:
        tr_, tc_ = (tk, tn) if mode == "nn" else (tn, tk)
        rper, cper = rows_within // tr_, cols_within // tc_

        def wmap(rb, cb):
            if kind == "cols":
                return (cb // cper, rb // rper, rb % rper, cb % cper)
            return (rb // rper, cb // cper, rb % rper, cb % cper)

        imap = (lambda i, j, k: wmap(k, j)) if mode == "nn" else (lambda i, j, k: wmap(j, k))
        b_spec = pl.BlockSpec((None, None) + b_blk, imap)
    else:
        if mode == "nt":
            imap = lambda i, j, k: (j + b_no // tn, k + b_ko // tk)
        else:
            imap = lambda i, j, k: (k + b_ko // tk, j + b_no // tn)
        if b_lead is None:
            b_spec = pl.BlockSpec(b_blk, imap)
        else:
            b_spec = pl.BlockSpec((None,) + b_blk, lambda i, j, k: (b_lead,) + imap(i, j, k))
    o_plain = pl.BlockSpec((tm, tn), lambda i, j, k: (i, j))
    if into is None:
        o_spec, out_shape = o_plain, jax.ShapeDtypeStruct((M, N), out_dtype)
    else:
        rper, cper = m_within // tm, n_within // tn
        if into[0] == "cols":
            o_spec = pl.BlockSpec((None, None, tm, tn), lambda i, j, k: (i // rper, j // cper, i % rper, j % cper))
        else:
            o_spec = pl.BlockSpec((None, None, tm, tn), lambda i, j, k: (j // cper, i // rper, i % rper, j % cper))
        out_shape = jax.ShapeDtypeStruct((2, N_CHIPS, m_within, n_within), out_dtype)
    has_add = add is not None
    n_in = 2 + has_add

    def body(*refs):
        a_ref, b_ref = refs[0], refs[1]
        add_ref = refs[2] if has_add else None
        o_ref = refs[n_in]
        part = lax.dot_general(a_ref[...].astype(BF16), b_ref[...].astype(BF16), dn, preferred_element_type=F32)

        def finish(r):
            if has_add:
                r = r + add_ref[...].astype(F32)
            o_ref[...] = r.astype(o_ref.dtype)

        if nk == 1:
            finish(part)
        else:
            acc_ref = refs[-1]
            k = pl.program_id(2)

            @pl.when(k == 0)
            def _():
                acc_ref[...] = part

            @pl.when((k > 0) & (k < nk - 1))
            def _():
                acc_ref[...] += part

            @pl.when(k == nk - 1)
            def _():
                finish(acc_ref[...] + part)

    in_specs = [a_spec, b_spec] + ([o_plain] if has_add else [])
    args = (a, b) + ((add,) if has_add else ())
    return pl.pallas_call(
        body, name=name, out_shape=out_shape, grid=(M // tm, N // tn, nk), in_specs=in_specs, out_specs=o_spec,
        scratch_shapes=[pltpu.VMEM((tm, tn), F32)] if nk > 1 else [],
        compiler_params=_cparams(("parallel", "parallel", "arbitrary")),
    )(*args)


def selection_matrices(src_of_new, n_shard):
    src = jnp.asarray(np.asarray(src_of_new, np.int32))
    ref = jnp.arange(N_CHIPS, dtype=jnp.int32)[:, None] * n_shard + jnp.arange(n_shard, dtype=jnp.int32)[None, :]
    pm = (ref[:, :, None] == src[None, None, :]).astype(BF16)
    pmt = (src[None, :, None] == ref[:, None, :]).astype(BF16)
    return pm, pmt


def regroup_cols(name, g, pm):
    _, _, kh, nn = g.shape
    n_new = pm.shape[-1]
    tm = _pick(kh, 512, SUBLANES_BF16)
    tn = _pick(n_new, 1024, LANES)
    hb = kh // tm

    def body(g_ref, pm_ref, o_ref, acc_ref):
        j = pl.program_id(2)
        part = _dot(g_ref[...], pm_ref[...], "nn")

        @pl.when(j == 0)
        def _():
            acc_ref[...] = part

        @pl.when(j > 0)
        def _():
            acc_ref[...] += part

        @pl.when(j == N_CHIPS - 1)
        def _():
            o_ref[...] = acc_ref[...].astype(o_ref.dtype)

    return pl.pallas_call(
        body, name=name, out_shape=jax.ShapeDtypeStruct((2 * kh, n_new), BF16), grid=(2 * hb, n_new // tn, N_CHIPS),
        in_specs=[pl.BlockSpec((None, None, tm, nn), lambda i, c, j: (j, i // hb, i % hb, 0)),
                  pl.BlockSpec((None, nn, tn), lambda i, c, j: (j, 0, c))],
        out_specs=pl.BlockSpec((tm, tn), lambda i, c, j: (i, c)), scratch_shapes=[pltpu.VMEM((tm, tn), F32)],
        compiler_params=_cparams(("parallel", "parallel", "arbitrary")),
    )(g, pm)


def ungroup_cols(name, dw, pmt):
    K, n_new = dw.shape
    nn = pmt.shape[-1]
    kh = K // 2
    tm = _pick(kh, 512, SUBLANES_BF16)
    hb = kh // tm

    def body(dw_ref, pmt_ref, o_ref):
        o_ref[...] = _dot(dw_ref[...], pmt_ref[...], "nn").astype(o_ref.dtype)

    return pl.pallas_call(
        body, name=name, out_shape=jax.ShapeDtypeStruct((2, N_CHIPS, kh, nn), BF16), grid=(2 * hb, N_CHIPS),
        in_specs=[pl.BlockSpec((tm, n_new), lambda i, j: (i, 0)), pl.BlockSpec((None, n_new, nn), lambda i, j: (j, 0, 0))],
        out_specs=pl.BlockSpec((None, None, tm, nn), lambda i, j: (i // hb, j, i % hb, 0)),
        compiler_params=_cparams(("parallel", "parallel")),
    )(dw, pmt)


def rowwise(name, fn, rows, vecs, outs, sums=(), *, M):
    rows = [tuple(r) + (0,) * (4 - len(r)) for r in rows]
    nr, nv, no, ns = len(rows), len(vecs), len(outs), len(sums)
    per_row = sum(w * a.dtype.itemsize for a, w, _, _ in rows) + sum(w * jnp.dtype(d).itemsize for w, d in outs)
    tr = _pick(M, max(8, min(512, ROW_BUDGET // (2 * per_row))), 16 if M % 16 == 0 else 8)

    def body(*refs):
        i = pl.program_id(0)
        res = fn(*[r[...] for r in refs[:nr + nv]])
        o_refs = refs[nr + nv:nr + nv + no]
        s_refs = refs[nr + nv + no:]
        for ref, val in zip(o_refs, res[:no]):
            ref[...] = val.astype(ref.dtype)
        if ns:
            @pl.when(i == 0)
            def _():
                for ref in s_refs:
                    ref[...] = jnp.zeros(ref.shape, F32)

            for ref, val in zip(s_refs, res[no:]):
                ref[...] += val

    in_specs = [pl.BlockSpec((tr, w), functools.partial(lambda i, cb, rb: (i + rb, cb), cb=off // w, rb=roff // tr)) for _, w, off, roff in rows]
    for _, w, off, roff in rows:
        assert off % w == 0 and roff % tr == 0
    in_specs += [pl.BlockSpec(v.shape, functools.partial(lambda i, nd: (0,) * nd, nd=v.ndim)) for v in vecs]
    out_specs = [pl.BlockSpec((tr, w), lambda i: (i, 0)) for w, _ in outs]
    out_specs += [pl.BlockSpec((1, w), lambda i: (0, 0)) for w in sums]
    out_shape = [jax.ShapeDtypeStruct((M, w), d) for w, d in outs] + [jax.ShapeDtypeStruct((1, w), F32) for w in sums]
    return pl.pallas_call(
        body, name=name, out_shape=out_shape, grid=(M // tr,), in_specs=in_specs, out_specs=out_specs,
        compiler_params=_cparams(("arbitrary",) if ns else ("parallel",)),
    )(*[r[0] for r in rows], *vecs)


def _rms(x, g):
    xf = x.astype(F32)
    return xf * lax.rsqrt(jnp.mean(xf * xf, axis=-1, keepdims=True) + NORM_EPS) * g


def _gelu(y):
    return 0.5 * y * (1.0 + jnp.tanh(math.sqrt(2.0 / math.pi) * (y + 0.044715 * (y * y * y))))


def _colsum(v):
    return jnp.sum(v, axis=0, keepdims=True)


def rms_fwd(name, x, width, off, g, *, M):
    return rowwise(name, lambda xb, gb: (_rms(xb, gb),), [(x, width, off)], [g], [(width, BF16)], M=M)[0]


def rms_bwd(name, x, width, off, g, dy, resid=None, *, M, out_dtype=F32):
    def fn(xb, dyb, *rest):
        gb = rest[-1]
        _, vjp = jax.vjp(_rms, xb.astype(F32), gb)
        dx, dg = vjp(dyb.astype(F32))
        if resid is not None:
            dx = dx + rest[0]
        return dx, dg

    rows = [(x, width, off), (dy, width, 0)] + ([(resid, width, 0)] if resid is not None else [])
    return rowwise(name, fn, rows, [g], [(width, out_dtype)], [width], M=M)


def lane_concat(name, parts, *, M, pad_to=None):
    width = sum(p.shape[1] for p in parts)
    pad = 0 if pad_to is None else pad_to - width

    def fn(*blocks):
        cols = [b.astype(BF16) for b in blocks]
        if pad:
            cols.append(jnp.zeros((blocks[0].shape[0], pad), BF16))
        return (jnp.concatenate(cols, axis=1),)

    return rowwise(name, fn, [(p, p.shape[1], 0) for p in parts], [], [(width + pad, BF16)], M=M)[0]


def _swap_halves(x, half):
    w = x.shape[-1]
    lane = lax.broadcasted_iota(jnp.int32, x.shape, x.ndim - 1)
    first = (lane % (2 * half)) < half
    return jnp.where(first, pltpu.roll(x, w - half, x.ndim - 1), pltpu.roll(x, half, x.ndim - 1))


def _attn_weight(kind, delta):
    if kind == "causal":
        return (delta >= 0).astype(F32)
    w = jnp.zeros(delta.shape, F32)
    for window, dil in DIL_PATTERNS:
        ok = (delta >= 0) & (delta <= window)
        if dil > 1:
            ok = ok & ((delta & (dil - 1)) == 0)
        w = w + ok.astype(F32)
    return w


def _dot(a, b, mode):
    return lax.dot_general(a, b, _DOT_DIMS[mode], preferred_element_type=F32)


def attention_fwd(name, qa, qa_off, ka, ka_off, v, v_off, *, da, dv, pairs, scale, kind, M, qb=None, qb_off=0, kb=None):
    tq = min(256, M)
    tk = min(512, M)
    has_b = qb is not None
    dr = MLA_ROPE

    def body(*refs):
        if has_b:
            qa_ref, ka_ref, v_ref, qb_ref, kb_ref, o_ref, lse_ref = refs
        else:
            qa_ref, ka_ref, v_ref, o_ref, lse_ref = refs
        i = pl.program_id(1)
        t0 = i * tq
        nkb = (t0 + tq + tk - 1) // tk
        o_parts, lse_parts = [], []
        for hh in range(2):
            q1 = qa_ref[:, hh * da:(hh + 1) * da].astype(BF16)
            q2 = qb_ref[:, hh * dr:(hh + 1) * dr].astype(BF16) if has_b else None

            def step(kbi, carry, hh=hh, q1=q1, q2=q2):
                m, l, acc = carry
                ks = pl.multiple_of(kbi * tk, tk)
                k1 = ka_ref[pl.ds(ks, tk), hh * da:(hh + 1) * da].astype(BF16)
                s = _dot(q1, k1, "nt")
                if has_b:
                    s = s + _dot(q2, kb_ref[pl.ds(ks, tk), 0:dr].astype(BF16), "nt")
                s = s * scale
                delta = (t0 + lax.broadcasted_iota(jnp.int32, (tq, tk), 0)) - (ks + lax.broadcasted_iota(jnp.int32, (tq, tk), 1))
                w = _attn_weight(kind, delta)
                s = jnp.where(w > 0, s, NEG)
                m_new = jnp.maximum(m, jnp.max(s, axis=1, keepdims=True))
                alpha = jnp.exp(m - m_new)
                p = w * jnp.exp(s - m_new)
                l = alpha * l + jnp.sum(p, axis=1, keepdims=True)
                vv = v_ref[pl.ds(ks, tk), hh * dv:(hh + 1) * dv].astype(BF16)
                acc = alpha * acc + _dot(p.astype(BF16), vv, "nn")
                return m_new, l, acc

            m, l, acc = lax.fori_loop(0, nkb, step, (jnp.full((tq, 1), NEG, F32), jnp.zeros((tq, 1), F32), jnp.zeros((tq, dv), F32)))
            o_parts.append(acc / l)
            lse_parts.append(m + jnp.log(l))
        o_ref[...] = jnp.concatenate(o_parts, axis=1)
        lane = lax.broadcasted_iota(jnp.int32, (tq, LANES), 1)
        lse_ref[...] = jnp.where(lane == 0, lse_parts[0], jnp.where(lane == 1, lse_parts[1], 0.0))

    assert qa_off % (2 * da) == 0 and ka_off % (2 * da) == 0 and v_off % (2 * dv) == 0
    in_specs = [
        pl.BlockSpec((tq, 2 * da), lambda hp, i: (i, qa_off // (2 * da) + hp)),
        pl.BlockSpec((M, 2 * da), lambda hp, i: (0, ka_off // (2 * da) + hp)),
        pl.BlockSpec((M, 2 * dv), lambda hp, i: (0, v_off // (2 * dv) + hp)),
    ]
    args = [qa, ka, v]
    if has_b:
        assert qb_off % LANES == 0
        in_specs += [pl.BlockSpec((tq, LANES), lambda hp, i: (i, qb_off // LANES + hp)),
                     pl.BlockSpec((M, LANES), lambda hp, i: (0, 0))]
        args += [qb, kb]
    out_specs = [pl.BlockSpec((tq, 2 * dv), lambda hp, i: (i, hp)),
                 pl.BlockSpec((None, tq, LANES), lambda hp, i: (hp, i, 0))]
    out_shape = [jax.ShapeDtypeStruct((M, pairs * 2 * dv), F32), jax.ShapeDtypeStruct((pairs, M, LANES), F32)]
    return pl.pallas_call(
        body, name=name, out_shape=out_shape, grid=(pairs, M // tq), in_specs=in_specs, out_specs=out_specs,
        compiler_params=_cparams(("parallel", "arbitrary")),
    )(*args)


def attention_bwd(name, qa, qa_off, ka, ka_off, v, v_off, o, do, lse, *, da, dv, pairs, scale, kind, M,
                  qb=None, qb_off=0, kb=None):
    tq = min(256, M)
    tk = min(256, M)
    has_b = qb is not None
    dr = MLA_ROPE

    def body(*refs):
        if has_b:
            qa_ref, ka_ref, v_ref, o_ref, do_ref, lse_ref, qb_ref, kb_ref, dqa_ref, dka_ref, dv_ref, dqb_ref, dkb_ref = refs
        else:
            qa_ref, ka_ref, v_ref, o_ref, do_ref, lse_ref, dqa_ref, dka_ref, dv_ref = refs
        hp = pl.program_id(0)
        i = pl.program_id(1)
        t0 = i * tq
        nkb = (t0 + tq + tk - 1) // tk

        @pl.when(i == 0)
        def _():
            dka_ref[...] = jnp.zeros(dka_ref.shape, F32)
            dv_ref[...] = jnp.zeros(dv_ref.shape, F32)

        if has_b:
            @pl.when((i == 0) & (hp == 0))
            def _():
                dkb_ref[...] = jnp.zeros(dkb_ref.shape, F32)

        dq1_parts, dq2_parts = [], []
        for hh in range(2):
            q1 = qa_ref[:, hh * da:(hh + 1) * da].astype(BF16)
            q2 = qb_ref[:, hh * dr:(hh + 1) * dr].astype(BF16) if has_b else None
            do_h = do_ref[:, hh * dv:(hh + 1) * dv]
            o_h = o_ref[:, hh * dv:(hh + 1) * dv]
            rowdot = jnp.sum(do_h * o_h, axis=1, keepdims=True)
            do_bf = do_h.astype(BF16)
            lse_h = lse_ref[:, hh:hh + 1]

            def step(kbi, carry, hh=hh, q1=q1, q2=q2, do_bf=do_bf, rowdot=rowdot, lse_h=lse_h):
                dq1, dq2 = carry
                ks = pl.multiple_of(kbi * tk, tk)
                k1 = ka_ref[pl.ds(ks, tk), hh * da:(hh + 1) * da].astype(BF16)
                s = _dot(q1, k1, "nt")
                if has_b:
                    k2 = kb_ref[pl.ds(ks, tk), 0:dr].astype(BF16)
                    s = s + _dot(q2, k2, "nt")
                s = s * scale
                delta = (t0 + lax.broadcasted_iota(jnp.int32, (tq, tk), 0)) - (ks + lax.broadcasted_iota(jnp.int32, (tq, tk), 1))
                w = _attn_weight(kind, delta)
                p = w * jnp.exp(jnp.where(w > 0, s, NEG) - lse_h)
                vv = v_ref[pl.ds(ks, tk), hh * dv:(hh + 1) * dv].astype(BF16)
                dp = _dot(do_bf, vv, "nt")
                ds = (p * (dp - rowdot) * scale).astype(BF16)
                dq1 = dq1 + _dot(ds, k1, "nn")
                dka_ref[pl.ds(ks, tk), hh * da:(hh + 1) * da] += _dot(ds, q1, "tn")
                dv_ref[pl.ds(ks, tk), hh * dv:(hh + 1) * dv] += _dot(p.astype(BF16), do_bf, "tn")
                if has_b:
                    dq2 = dq2 + _dot(ds, k2, "nn")
                    dkb_ref[pl.ds(ks, tk), 0:dr] += _dot(ds, q2, "tn")
                return dq1, dq2

            dq1, dq2 = lax.fori_loop(0, nkb, step, (jnp.zeros((tq, da), F32), jnp.zeros((tq, dr), F32)))
            dq1_parts.append(dq1)
            dq2_parts.append(dq2)
        dqa_ref[...] = jnp.concatenate(dq1_parts, axis=1).astype(dqa_ref.dtype)
        if has_b:
            dqb_ref[...] = jnp.concatenate(dq2_parts, axis=1).astype(dqb_ref.dtype)

    in_specs = [
        pl.BlockSpec((tq, 2 * da), lambda hp, i: (i, qa_off // (2 * da) + hp)),
        pl.BlockSpec((M, 2 * da), lambda hp, i: (0, ka_off // (2 * da) + hp)),
        pl.BlockSpec((M, 2 * dv), lambda hp, i: (0, v_off // (2 * dv) + hp)),
        pl.BlockSpec((tq, 2 * dv), lambda hp, i: (i, hp)),
        pl.BlockSpec((tq, 2 * dv), lambda hp, i: (i, hp)),
        pl.BlockSpec((None, tq, LANES), lambda hp, i: (hp, i, 0)),
    ]
    args = [qa, ka, v, o, do, lse]
    out_specs = [pl.BlockSpec((tq, 2 * da), lambda hp, i: (i, hp)),
                 pl.BlockSpec((M, 2 * da), lambda hp, i: (0, hp)),
                 pl.BlockSpec((M, 2 * dv), lambda hp, i: (0, hp))]
    out_shape = [jax.ShapeDtypeStruct((M, pairs * 2 * da), BF16),
                 jax.ShapeDtypeStruct((M, pairs * 2 * da), F32),
                 jax.ShapeDtypeStruct((M, pairs * 2 * dv), F32)]
    if has_b:
        in_specs += [pl.BlockSpec((tq, LANES), lambda hp, i: (i, qb_off // LANES + hp)),
                     pl.BlockSpec((M, LANES), lambda hp, i: (0, 0))]
        args += [qb, kb]
        out_specs += [pl.BlockSpec((tq, LANES), lambda hp, i: (i, hp)), pl.BlockSpec((M, LANES), lambda hp, i: (0, 0))]
        out_shape += [jax.ShapeDtypeStruct((M, pairs * LANES), F32), jax.ShapeDtypeStruct((M, LANES), F32)]
    return pl.pallas_call(
        body, name=name, out_shape=out_shape, grid=(pairs, M // tq), in_specs=in_specs, out_specs=out_specs,
        compiler_params=_cparams(("arbitrary", "arbitrary")),
    )(*args)


def ssm_scan(name, xcat, acat, *, M, reverse=False, hcat=None):
    C2 = xcat.shape[1]
    cb = LANES
    tb = min(128, M)
    nblk = M // tb
    with_da = hcat is not None

    def body(*refs):
        if with_da:
            x_ref, a_ref, h_ref, o_ref, da_ref, p_ref = refs
        else:
            x_ref, a_ref, o_ref, p_ref = refs
        ar, ai = a_ref[:, :cb], a_ref[:, cb:]
        row = lax.broadcasted_iota(jnp.int32, (tb, cb), 0)

        def logscan(xr, xi):
            pr, pi = ar, ai
            d = 1
            while d < tb:
                shift = tb - d if reverse else d
                keep = (row < tb - d) if reverse else (row >= d)
                sr = jnp.where(keep, pltpu.roll(xr, shift, 0), 0.0)
                si = jnp.where(keep, pltpu.roll(xi, shift, 0), 0.0)
                xr, xi = xr + pr * sr - pi * si, xi + pr * si + pi * sr
                pr, pi = pr * pr - pi * pi, 2.0 * pr * pi
                d *= 2
            return xr, xi

        seed = row == (tb - 1 if reverse else 0)
        p0r, p0i = logscan(jnp.where(seed, ar, 0.0), jnp.where(seed, ai, 0.0))
        p_ref[:, :cb] = p0r
        p_ref[:, cb:] = p0i
        sub = lax.broadcasted_iota(jnp.int32, (8, cb), 0)
        edge = 0 if reverse else tb - 8
        pick = sub == (0 if reverse else 7)

        def blk(b, carry):
            cr, ci = carry
            bb = (nblk - 1 - b) if reverse else b
            t0 = pl.multiple_of(bb * tb, tb)
            hr, hi = logscan(x_ref[pl.ds(t0, tb), :cb], x_ref[pl.ds(t0, tb), cb:])
            pr, pi = p_ref[:, :cb], p_ref[:, cb:]
            o_ref[pl.ds(t0, tb), :cb] = hr + pr * cr - pi * ci
            o_ref[pl.ds(t0, tb), cb:] = hi + pr * ci + pi * cr
            te = pl.multiple_of(t0 + edge, 8)
            ncr = jnp.sum(jnp.where(pick, o_ref[pl.ds(te, 8), :cb], 0.0), axis=0, keepdims=True)
            nci = jnp.sum(jnp.where(pick, o_ref[pl.ds(te, 8), cb:], 0.0), axis=0, keepdims=True)
            return ncr, nci

        lax.fori_loop(0, nblk, blk, (jnp.zeros((1, cb), F32), jnp.zeros((1, cb), F32)))
        if with_da:
            first = lax.broadcasted_iota(jnp.int32, (M, cb), 0) >= 1
            hpr = jnp.where(first, pltpu.roll(h_ref[:, :cb], 1, 0), 0.0)
            hpi = jnp.where(first, pltpu.roll(h_ref[:, cb:], 1, 0), 0.0)
            lr, li = o_ref[:, :cb], o_ref[:, cb:]
            da_ref[:, :cb] = _colsum(lr * hpr + li * hpi)
            da_ref[:, cb:] = _colsum(li * hpr - lr * hpi)

    blk_spec = pl.BlockSpec((M, 2 * cb), lambda j: (0, j))
    vec_spec = pl.BlockSpec((1, 2 * cb), lambda j: (0, j))
    in_specs = [blk_spec, vec_spec] + ([blk_spec] if with_da else [])
    out_specs = [blk_spec] + ([vec_spec] if with_da else [])
    out_shape = [jax.ShapeDtypeStruct((M, C2), F32)] + ([jax.ShapeDtypeStruct((1, C2), F32)] if with_da else [])
    args = [xcat, acat] + ([hcat] if with_da else [])
    res = pl.pallas_call(
        body, name=name, out_shape=out_shape, grid=(C2 // (2 * cb),), in_specs=in_specs, out_specs=out_specs,
        scratch_shapes=[pltpu.VMEM((tb, 2 * cb), F32)], compiler_params=_cparams(("parallel",)),
    )(*args)
    return res if with_da else res[0]


def _ssm_param_fn(a_re, a_im, ldt, b_re, b_im):
    lr, li = jnp.minimum(a_re, -1e-4), a_im
    dt = jnp.exp(ldt)
    e, ang = jnp.exp(lr * dt), li * dt
    ar, ai = e * jnp.cos(ang), e * jnp.sin(ang)
    den = lr * lr + li * li
    nr, ni = ar - 1.0, ai
    cr, ci = (nr * lr + ni * li) / den, (ni * lr - nr * li) / den
    return ar, ai, cr * b_re - ci * b_im, cr * b_im + ci * b_re


def _whole(shape):
    return pl.BlockSpec(shape, functools.partial(lambda nd: (0,) * nd, nd=len(shape)))


def ssm_param_fwd(name, a_re, a_im, ldt, b_re, b_im):
    def body(*refs):
        res = _ssm_param_fn(*[r[...] for r in refs[:5]])
        for ref, val in zip(refs[5:], res):
            ref[...] = val

    ins = [a_re, a_im, ldt, b_re, b_im]
    outs = [a_re, a_re, b_re, b_re]
    return pl.pallas_call(
        body, name=name, out_shape=[jax.ShapeDtypeStruct(t.shape, F32) for t in outs],
        in_specs=[_whole(t.shape) for t in ins], out_specs=[_whole(t.shape) for t in outs], compiler_params=_cparams(),
    )(*ins)


def ssm_param_bwd(name, a_re, a_im, ldt, b_re, b_im, d_ar, d_ai, d_bbr, d_bbi):
    def body(*refs):
        _, vjp = jax.vjp(_ssm_param_fn, *[r[...] for r in refs[:5]])
        res = vjp(tuple(r[...] for r in refs[5:9]))
        for ref, val in zip(refs[9:], res):
            ref[...] = val

    ins = [a_re, a_im, ldt, b_re, b_im, d_ar, d_ai, d_bbr, d_bbi]
    outs = [a_re, a_im, ldt, b_re, b_im]
    return pl.pallas_call(
        body, name=name, out_shape=[jax.ShapeDtypeStruct(t.shape, F32) for t in outs],
        in_specs=[_whole(t.shape) for t in ins], out_specs=[_whole(t.shape) for t in outs], compiler_params=_cparams(),
    )(*ins)


ANY = pl.BlockSpec(memory_space=pl.ANY)


def _place():
    x, y, c = lax.axis_index("x"), lax.axis_index("y"), lax.axis_index("c")
    chips = [(1 - x, y), (x, 1 - y), (1 - x, 1 - y)]
    return x, y, c, chips


def cast_into_slot(name, w, kind, l=None):
    K, nn = w.shape[-2:]
    hr, hc = (K // 2, nn) if kind == "cols" else (K, nn // 2)
    tr = _pick(hr, max(16, min(512, ROW_BUDGET // (2 * hc * 6))), SUBLANES_BF16)
    nb = hr // tr

    def body(w_ref, o_ref):
        o_ref[...] = w_ref[...].astype(BF16)

    lead = () if l is None else (l,)
    if kind == "cols":
        in_spec = pl.BlockSpec((None,) * len(lead) + (tr, hc), lambda h, i: lead + (h * nb + i, 0))
    else:
        in_spec = pl.BlockSpec((None,) * len(lead) + (tr, hc), lambda h, i: lead + (i, h))
    return pl.pallas_call(
        body, name=name, out_shape=jax.ShapeDtypeStruct((N_CHIPS, 2, hr, hc), BF16), grid=(2, nb), in_specs=[in_spec],
        out_specs=pl.BlockSpec((None, None, tr, hc), lambda h, i: (2 * lax.axis_index("x") + lax.axis_index("y"), h, i, 0)),
        compiler_params=_cparams(("parallel", "parallel")),
    )(w)


HBM_SPEC = pl.BlockSpec(memory_space=pltpu.HBM)
SEM_SPEC = pl.BlockSpec(memory_space=pltpu.SEMAPHORE)
SPLIT_PARAMS = pltpu.CompilerParams(has_side_effects=pltpu.SideEffectType.DATAFLOW_SIDE_EFFECTING)


def _in_hbm(t):
    return pltpu.with_memory_space_constraint(t, pltpu.HBM)


def split_start(name, plan, n, bufs, fresh, carrier):
    nb, nf = len(bufs), len(fresh)

    def body(*refs):
        outs = refs[nb + 1:]
        for i, (s, d, dev) in enumerate(plan(list(outs[2:2 + nb + nf]))):
            pltpu.make_async_remote_copy(src_ref=s, dst_ref=d, send_sem=outs[0].at[i], recv_sem=outs[1].at[i],
                                         device_id=dev, device_id_type=MESH).start()

    hbm = lambda t: pltpu.HBM(t.shape, t.dtype)
    res = pl.pallas_call(
        body, name=name,
        out_shape=(pltpu.SemaphoreType.DMA((n,)), pltpu.SemaphoreType.DMA((n,)), *[hbm(t) for t in bufs], *[hbm(t) for t in fresh], hbm(carrier)),
        in_specs=[HBM_SPEC] * (nb + 1), out_specs=(SEM_SPEC, SEM_SPEC) + (HBM_SPEC,) * (nb + nf + 1),
        input_output_aliases={**{i: 2 + i for i in range(nb)}, nb: 2 + nb + nf}, compiler_params=SPLIT_PARAMS,
    )(*[_in_hbm(t) for t in bufs], _in_hbm(carrier))
    return (res[0], res[1]), list(res[2:2 + nb]), list(res[2 + nb:2 + nb + nf]), res[2 + nb + nf]


def split_wait(name, plan, sems, bufs, carrier):
    nb = len(bufs)

    def body(*refs):
        for i, (s, d, dev) in enumerate(plan(list(refs[:nb]))):
            cp = pltpu.make_async_remote_copy(src_ref=s, dst_ref=d, send_sem=refs[nb].at[i], recv_sem=refs[nb + 1].at[i],
                                              device_id=dev, device_id_type=MESH)
            cp.wait_send()
            cp.wait_recv()

    hbm = lambda t: pltpu.HBM(t.shape, t.dtype)
    res = pl.pallas_call(
        body, name=name, out_shape=(*[hbm(t) for t in bufs], hbm(carrier)),
        in_specs=[HBM_SPEC] * nb + [SEM_SPEC, SEM_SPEC, HBM_SPEC], out_specs=(HBM_SPEC,) * (nb + 1),
        input_output_aliases={**{i: i for i in range(nb)}, nb + 2: nb}, compiler_params=SPLIT_PARAMS,
    )(*bufs, sems[0], sems[1], carrier)
    return list(res[:nb]), res[nb]


def _me_sib_chips():
    x, y, c, chips = _place()
    return 2 * x + y, c, (x, y, 1 - c), chips


def plan_gather_ici(refs):
    me, c, _, chips = _me_sib_chips()
    return [(r.at[me, c], r.at[me, c], (chip[0], chip[1], c)) for r in refs for chip in chips]


def plan_gather_pass(refs):
    _, c, sib, chips = _me_sib_chips()
    return [(r.at[2 * chip[0] + chip[1], c], r.at[2 * chip[0] + chip[1], c], sib) for r in refs for chip in chips]


def plan_pair(n_arrays):
    def plan(refs):
        _, c, sib, _ = _me_sib_chips()
        return [(refs[a].at[1 - c], refs[n_arrays + a], sib) for a in range(n_arrays)]
    return plan


def plan_chips(n_arrays):
    def plan(refs):
        _, c, _, chips = _me_sib_chips()
        return [(refs[a].at[2 * chip[0] + chip[1]], refs[n_arrays + a].at[k], (chip[0], chip[1], c))
                for a in range(n_arrays) for k, chip in enumerate(chips)]
    return plan


def plan_share(n_arrays):
    def plan(refs):
        _, _, sib, _ = _me_sib_chips()
        return [(refs[a], refs[n_arrays + a], sib) for a in range(n_arrays)]
    return plan


def swap_with_sibling(name, src, pick_other_half):
    shape = src.shape[1:] if pick_other_half else src.shape

    def body(src_ref, out_ref, ssem, rsem):
        x, y, c, _ = _place()
        cp = pltpu.make_async_remote_copy(src_ref=src_ref.at[1 - c] if pick_other_half else src_ref, dst_ref=out_ref,
                                          send_sem=ssem, recv_sem=rsem, device_id=(x, y, 1 - c), device_id_type=MESH)
        cp.start()
        cp.wait()

    return pl.pallas_call(
        body, name=name, out_shape=jax.ShapeDtypeStruct(shape, src.dtype), in_specs=[ANY], out_specs=ANY,
        scratch_shapes=[pltpu.SemaphoreType.DMA(()), pltpu.SemaphoreType.DMA(())],
    )(src)


def exchange_chips(name, src, per_chip):
    shape = src.shape[1:] if per_chip else src.shape

    def body(src_ref, out_ref, send_sems, recv_sems):
        x, y, c, chips = _place()
        cps = []
        for k, chip in enumerate(chips):
            s = src_ref.at[2 * chip[0] + chip[1]] if per_chip else src_ref
            cps.append(pltpu.make_async_remote_copy(src_ref=s, dst_ref=out_ref.at[k], send_sem=send_sems.at[k], recv_sem=recv_sems.at[k],
                                                    device_id=(chip[0], chip[1], c), device_id_type=MESH))
        for cp in cps:
            cp.start()
        for cp in cps:
            cp.wait()

    return pl.pallas_call(
        body, name=name, out_shape=jax.ShapeDtypeStruct((3,) + shape, src.dtype), in_specs=[ANY], out_specs=ANY,
        scratch_shapes=[pltpu.SemaphoreType.DMA((3,)), pltpu.SemaphoreType.DMA((3,))],
    )(src)


def pair_sum(name, p, got):
    _, _, rh, cw = p.shape
    tr = _pick(rh, max(16, min(512, ROW_BUDGET // (2 * cw * 10))), SUBLANES_BF16)

    def body(p_ref, got_ref, s_ref, own_ref):
        j = pl.program_id(1)
        tot = p_ref[...].astype(F32) + got_ref[...].astype(F32)
        s_ref[...] = tot.astype(BF16)

        @pl.when(j == 2 * lax.axis_index("x") + lax.axis_index("y"))
        def _():
            own_ref[...] = tot

    return pl.pallas_call(
        body, name=name, grid=(rh // tr, N_CHIPS),
        in_specs=[pl.BlockSpec((None, None, tr, cw), lambda i, j: (lax.axis_index("c"), j, i, 0)),
                  pl.BlockSpec((None, tr, cw), lambda i, j: (j, i, 0))],
        out_specs=[pl.BlockSpec((None, tr, cw), lambda i, j: (j, i, 0)),
                   pl.BlockSpec((tr, cw), lambda i, j: (i, 0))],
        out_shape=[jax.ShapeDtypeStruct((N_CHIPS, rh, cw), BF16), jax.ShapeDtypeStruct((rh, cw), F32)],
        compiler_params=_cparams(("arbitrary", "arbitrary")),
    )(p, got)


def chips_sum(name, own, parts):
    rh, cw = own.shape
    parts = parts.reshape(3 * rh, cw)
    return rowwise(name, lambda o, a, b, c: (((o + a.astype(F32)) + b.astype(F32)) + c.astype(F32),),
                   [(own, cw, 0), (parts, cw, 0, 0), (parts, cw, 0, rh), (parts, cw, 0, 2 * rh)], [], [(cw, F32)], M=rh)[0]


def all_reduce_small(buf):
    r = buf.shape[0]
    got = swap_with_sibling("ar_pair", buf, False)
    chip = rowwise("ar_pairsum", lambda a, b: (a + b,), [(buf, LANES, 0), (got, LANES, 0)], [], [(LANES, F32)], M=r)[0]
    parts = exchange_chips("ar_chips", chip, False).reshape(3 * r, LANES)
    return rowwise("ar_sum", lambda o, fx, fy, fxy: ((o + fy) + (fx + fxy),),
                   [(chip, LANES, 0), (parts, LANES, 0, 0), (parts, LANES, 0, r), (parts, LANES, 0, 2 * r)], [], [(LANES, F32)], M=r)[0]


def _adam_fn(w, g, m, v):
    m = ADAM_B1 * m + (1.0 - ADAM_B1) * g
    v = ADAM_B2 * v + (1.0 - ADAM_B2) * (g * g)
    m_hat = m / (1.0 - ADAM_B1 ** ADAM_STEP)
    v_hat = v / (1.0 - ADAM_B2 ** ADAM_STEP)
    return -ADAM_LR * (m_hat / (jnp.sqrt(v_hat) + ADAM_EPS) + ADAM_WD * w), m, v


def adamw(name, w, g, m, v):
    r, cw = w.shape
    return rowwise(name, _adam_fn, [(t, cw, 0) for t in (w, g, m, v)], [], [(cw, F32)] * 3, M=r)


def adamw_layers(name, w, m, v, mines, theirs, kind):
    L, K, nn = w.shape
    hr, hc = (K // 2, nn) if kind == "cols" else (K, nn // 2)
    tr = _pick(hr, max(8, min(256, ROW_BUDGET // (2 * hc * 4 * (7 + 2 * L)))), 8)
    nb = hr // tr

    def body(*refs):
        w_ref, m_ref, v_ref = refs[:3]
        outs = refs[3 + 2 * L:]
        l, mine_here = pl.program_id(0), pl.program_id(1) == lax.axis_index("c")
        g = jnp.zeros((tr, hc), F32)
        for ll in range(L):
            g = jnp.where(l == ll, jnp.where(mine_here, refs[3 + ll][...], refs[3 + L + ll][...]), g)
        outs[0][...] = g
        outs[1][...], outs[2][...], outs[3][...] = _adam_fn(w_ref[...], g, m_ref[...], v_ref[...])

    if kind == "cols":
        full = pl.BlockSpec((None, tr, hc), lambda l, h, i: (l, h * nb + i, 0))
    else:
        full = pl.BlockSpec((None, tr, hc), lambda l, h, i: (l, i, h))
    halves = [pl.BlockSpec((tr, hc), functools.partial(lambda l, h, i, ll: (jnp.where(l == ll, i, 0), 0), ll=ll)) for ll in range(L)]
    return pl.pallas_call(
        body, name=name, grid=(L, 2, nb), in_specs=[full] * 3 + halves * 2, out_specs=[full] * 4,
        out_shape=[jax.ShapeDtypeStruct((L, K, nn), F32)] * 4, compiler_params=_cparams(("parallel", "parallel", "parallel")),
    )(w, m, v, *mines, *theirs)


class Dims:
    def __init__(self, x, g_q, g_kv, g_out_mla, g_out_ssm, g_out_dil, ff):
        self.M, self.D = x.shape[-2], x.shape[-1]
        self.QL, self.KVL = g_q.shape[-1], g_kv.shape[-1]
        self.MW, self.SW, self.DW = g_out_mla.shape[-1], g_out_ssm.shape[-1], g_out_dil.shape[-1]
        self.H = self.MW // MLA_V
        self.FF = ff
        self.G = self.SW // SSM_GROUP
        self.C = self.G * SSM_STATE
        self.o_cq, self.o_u = 0, self.QL
        self.o_qd = self.o_u + self.SW
        self.o_kd = self.o_qd + self.DW
        self.o_vd = self.o_kd + self.DW
        self.o_ckv = self.o_vd + self.DW
        self.o_kr = self.o_ckv + self.KVL
        self.PW = -(-(self.o_kr + MLA_ROPE) // (4 * LANES)) * (4 * LANES)
        assert self.o_u % self.SW == 0 and self.o_qd % LANES == 0 and self.o_ckv % self.KVL == 0 and self.o_kr % LANES == 0
        assert self.H % 2 == 0 and self.DW % LANES == 0 and self.C % LANES == 0
        self.QW = self.H * (MLA_NOPE + MLA_ROPE)
        self.KVW = self.H * (MLA_NOPE + MLA_V)
        sizes = [self.QL, self.KVL, MLA_ROPE, self.SW, self.DW, self.DW, self.DW]
        starts = np.concatenate([[0], np.cumsum(sizes)[:-1]])
        self.ref_cols = {n: (int(s), int(z)) for n, s, z in zip(["cq", "ckv", "kr", "u", "qd", "kd", "vd"], starts, sizes)}
        self.INW = int(sum(sizes))
        self.new_order = ["cq", "u", "qd", "kd", "vd", "ckv", "kr"]
        src = np.concatenate([np.arange(self.ref_cols[n][0], self.ref_cols[n][0] + self.ref_cols[n][1]) for n in self.new_order])
        self.src_in = np.concatenate([src, -np.ones(self.PW - self.INW, np.int64)])
        self.src_q = self._heads_split(self.H, MLA_NOPE, MLA_ROPE)
        self.src_kv = self._heads_split(self.H, MLA_NOPE, MLA_V)

    @staticmethod
    def _heads_split(h, d1, d2):
        first = (np.arange(h)[:, None] * (d1 + d2) + np.arange(d1)[None, :]).reshape(-1)
        second = (np.arange(h)[:, None] * (d1 + d2) + d1 + np.arange(d2)[None, :]).reshape(-1)
        return np.concatenate([first, second])


def _regroup_in(dm, w):
    parts = [w[..., dm.ref_cols[n][0]:dm.ref_cols[n][0] + dm.ref_cols[n][1]] for n in dm.new_order]
    pad = dm.PW - dm.INW
    return jnp.concatenate(parts + [jnp.zeros(w.shape[:-1] + (pad,), w.dtype)], axis=-1)


def _ungroup_in(dm, w):
    off, pieces = 0, {}
    for n in dm.new_order:
        pieces[n] = w[..., off:off + dm.ref_cols[n][1]]
        off += dm.ref_cols[n][1]
    return jnp.concatenate([pieces[n] for n in ["cq", "ckv", "kr", "u", "qd", "kd", "vd"]], axis=-1)


def _split_heads(w, h, d1):
    t = w.reshape(w.shape[:-1] + (h, -1))
    return jnp.concatenate([t[..., :d1].reshape(w.shape[:-1] + (-1,)), t[..., d1:].reshape(w.shape[:-1] + (-1,))], axis=-1)


def _merge_heads(w, h, d1):
    a = w[..., :h * d1].reshape(w.shape[:-1] + (h, d1))
    b = w[..., h * d1:].reshape(w.shape[:-1] + (h, -1))
    return jnp.concatenate([a, b], axis=-1).reshape(w.shape[:-1] + (-1,))


def _cat_cols(re, im):
    r, c = re.shape
    return jnp.stack([re.reshape(r, c // LANES, LANES), im.reshape(r, c // LANES, LANES)], axis=2).reshape(r, 2 * c)


def _uncat_cols(cat):
    r, c2 = cat.shape
    t = cat.reshape(r, c2 // (2 * LANES), 2, LANES)
    return t[:, :, 0].reshape(r, c2 // 2), t[:, :, 1].reshape(r, c2 // 2)


def _block_diag(t, g):
    _, a, b = t.shape
    eye = jnp.eye(g, dtype=bool)[:, None, :, None]
    return jnp.where(eye, t[:, :, None, :], 0).reshape(g * a, g * b)


def _diag_blocks(m, g):
    a, b = m.shape[0] // g, m.shape[1] // g
    eye = jnp.eye(g, dtype=m.dtype)[:, None, :, None]
    return jnp.sum(m.reshape(g, a, g, b) * eye, axis=2)


def _rope_tables(dm):
    half = MLA_ROPE // 2
    inv_freq = ROPE_THETA ** (-jnp.arange(half, dtype=F32) / half)
    ang = jnp.arange(dm.M, dtype=F32)[:, None] * inv_freq[None, :]
    cos = jnp.concatenate([jnp.cos(ang), jnp.cos(ang)], axis=1)
    sin = jnp.concatenate([-jnp.sin(ang), jnp.sin(ang)], axis=1)
    return jnp.tile(cos, (1, dm.H)), jnp.tile(sin, (1, dm.H)), jnp.tile(cos, (1, LANES // MLA_ROPE)), jnp.tile(sin, (1, LANES // MLA_ROPE))


def _rope(x, cos, sin):
    return x * cos + _swap_halves(x, MLA_ROPE // 2) * sin


def _rope_t(d, cos, sin):
    return d * cos + _swap_halves(d * sin, MLA_ROPE // 2)


def _ssm_layer_params(dm, a_re, a_im, log_dt, b_re, b_im):
    flat = lambda t: t.reshape(1, dm.C)
    ldt = jnp.repeat(log_dt, SSM_STATE).reshape(1, dm.C)
    bt = lambda t: jnp.transpose(t, (2, 0, 1)).reshape(SSM_GROUP, dm.C)
    return flat(a_re), flat(a_im), ldt, bt(b_re), bt(b_im)


def layer_forward(dm, l, x, lw, sp, tabs, hook):
    M, D = dm.M, dm.D
    n = lambda s: f"{s}_l{l}"
    sv = {"x_in": x}
    h1 = rms_fwd(n("rms_mix"), x, D, 0, lw["g_mix"], M=M)
    proj = matmul(n("in_proj"), h1, lw["w_in"], "nn", M=M, N=dm.PW, K=D)
    sv.update(h1=h1, proj=proj)
    cqn = rms_fwd(n("rms_q"), proj, dm.QL, dm.o_cq, lw["g_q"], M=M)
    q = matmul(n("q_up"), cqn, lw["w_uq"], "nn", M=M, N=dm.QW, K=dm.QL)
    ckvn = rms_fwd(n("rms_kv"), proj, dm.KVL, dm.o_ckv, lw["g_kv"], M=M)
    kv = matmul(n("kv_up"), ckvn, lw["w_ukv"], "nn", M=M, N=dm.KVW, K=dm.KVL, out_dtype=BF16)
    cosq, sinq, cosk, sink = tabs
    nw = dm.H * MLA_NOPE

    def rope_fn(qb, kb, cq, sq, ck, sk):
        return jnp.concatenate([qb[:, :nw], _rope(qb[:, nw:], cq, sq)], axis=1), _rope(kb, ck, sk)

    pw = dm.H * MLA_ROPE
    q_bf, kpe = rowwise(n("rope"), rope_fn, [(q, dm.QW, 0), (proj, LANES, dm.o_kr), (cosq, pw, 0), (sinq, pw, 0), (cosk, LANES, 0), (sink, LANES, 0)],
                        [], [(dm.QW, BF16), (LANES, BF16)], M=M)
    mla_scale = (MLA_NOPE + MLA_ROPE) ** -0.5
    o_mla, lse_mla = attention_fwd(n("mla_fwd"), q_bf, 0, kv, 0, kv, nw, da=MLA_NOPE, dv=MLA_V, pairs=dm.H // 2, scale=mla_scale,
                                   kind="causal", M=M, qb=q_bf, qb_off=nw, kb=kpe)
    sv.update(cqn=cqn, ckvn=ckvn, q_bf=q_bf, kv=kv, kpe=kpe, o_mla=o_mla, lse_mla=lse_mla)
    bu = matmul(n("ssm_bu"), proj, sp["bcat"], "nn", M=M, N=2 * dm.C, K=dm.SW, a_off=(0, dm.o_u))
    hcat = ssm_scan(n("ssm_scan"), bu, sp["acat"], M=M)
    ylin = matmul(n("ssm_y"), hcat, sp["ccat"], "nn", M=M, N=dm.SW, K=2 * dm.C)
    yg = rowwise(n("ssm_gelu"), lambda y, u, d: (_gelu(y + d * u),), [(ylin, dm.SW, 0), (proj, dm.SW, dm.o_u)], [lw["d_skip"]],
                 [(dm.SW, BF16)], M=M)[0]
    z = matmul(n("ssm_glu"), yg, lw["w_glu"], "nn", w=("cols", 2 * dm.SW // N_CHIPS), M=M, N=2 * dm.SW, K=dm.SW)
    sw = dm.SW

    def glu_fn(zb, b):
        zz = zb + b
        return (zz[:, :sw] * jax.nn.sigmoid(zz[:, sw:]),)

    o_ssm = rowwise(n("ssm_gate"), glu_fn, [(z, 2 * sw, 0)], [lw["b_glu"]], [(sw, F32)], M=M)[0]
    sv.update(hcat=hcat, ylin=ylin, yg=yg, z=z, o_ssm=o_ssm)
    o_dil, lse_dil = attention_fwd(n("dil_fwd"), proj, dm.o_qd, proj, dm.o_kd, proj, dm.o_vd, da=DIL_HEAD, dv=DIL_HEAD, pairs=dm.DW // LANES,
                                   scale=DIL_HEAD ** -0.5, kind="dilated", M=M)
    sv.update(o_dil=o_dil, lse_dil=lse_dil)
    yn = rowwise(n("out_norm"), lambda a, b, c, ga, gb, gc: (jnp.concatenate([_rms(a, ga), _rms(b, gb), _rms(c, gc)], axis=1),),
                 [(o_mla, dm.MW, 0), (o_ssm, dm.SW, 0), (o_dil, dm.DW, 0)], [lw["g_out_mla"], lw["g_out_ssm"], lw["g_out_dil"]],
                 [(D, BF16)], M=M)[0]
    yn = hook(yn)
    x_mid = matmul(n("out_proj"), yn, lw["w_o"], "nn", w=("rows", D // N_CHIPS), M=M, N=D, K=D, add=x)
    h2 = rms_fwd(n("rms_ffn"), x_mid, D, 0, lw["g_ffn"], M=M)
    ffs = dm.FF // N_CHIPS
    gate = matmul(n("ffn_gate"), h2, lw["w_gate"], "nn", w=("cols",ffs), M=M, N=dm.FF, K=D)
    up = matmul(n("ffn_up"), h2, lw["w_up"], "nn", w=("cols",ffs), M=M, N=dm.FF, K=D)
    act = rowwise(n("ffn_act"), lambda g, u: (g * jax.nn.sigmoid(g) * u,), [(gate, dm.FF, 0), (up, dm.FF, 0)], [], [(dm.FF, BF16)], M=M)[0]
    x_out = matmul(n("ffn_down"), act, lw["w_down"], "nn", w=("rows",ffs), M=M, N=D, K=dm.FF, add=x_mid)
    sv.update(yn=yn, x_mid=x_mid, h2=h2, gate=gate, up=up, act=act)
    return x_out, sv


def layer_backward(dm, l, dx, lw, sp, tabs, sv, hook_a, hook_b):
    M, D = dm.M, dm.D
    n = lambda s: f"{s}_l{l}"
    g = {}
    ffs = dm.FF // N_CHIPS
    dact = matmul(n("ffn_down_dx"), dx, lw["w_down"], "nt", w=("rows", ffs), M=M, N=dm.FF, K=D)
    g["w_down"] = matmul(n("ffn_down_dw"), sv["act"], dx, "tn", M=dm.FF, N=D, K=M, out_dtype=BF16, into=("rows", ffs))

    def act_bwd(gb, ub, db):
        _, vjp = jax.vjp(lambda a, b: a * jax.nn.sigmoid(a) * b, gb, ub)
        return vjp(db)

    dgate, dup = rowwise(n("ffn_act_bwd"), act_bwd, [(sv["gate"], dm.FF, 0), (sv["up"], dm.FF, 0), (dact, dm.FF, 0)], [],
                         [(dm.FF, BF16), (dm.FF, BF16)], M=M)
    dh2 = matmul(n("ffn_gate_dx"), dgate, lw["w_gate"], "nt", w=("cols",ffs), M=M, N=D, K=dm.FF)
    dh2 = matmul(n("ffn_up_dx"), dup, lw["w_up"], "nt", w=("cols",ffs), M=M, N=D, K=dm.FF, add=dh2)
    g["w_gate"] = matmul(n("ffn_gate_dw"), sv["h2"], dgate, "tn", M=D, N=dm.FF, K=M, out_dtype=BF16, into=("cols", ffs))
    g["w_up"] = matmul(n("ffn_up_dw"), sv["h2"], dup, "tn", M=D, N=dm.FF, K=M, out_dtype=BF16, into=("cols", ffs))
    dx_mid, g["g_ffn"] = rms_bwd(n("rms_ffn_bwd"), sv["x_mid"], D, 0, lw["g_ffn"], dh2, dx, M=M)
    dx_mid = hook_a(dx_mid)
    dyn = matmul(n("out_proj_dx"), dx_mid, lw["w_o"], "nt", w=("rows", D // N_CHIPS), M=M, N=D, K=D)
    g["w_o"] = matmul(n("out_proj_dw"), sv["yn"], dx_mid, "tn", M=D, N=D, K=M, out_dtype=BF16, into=("rows", D // N_CHIPS))
    mw, sw, dw = dm.MW, dm.SW, dm.DW

    def out_norm_bwd(a, b, c, dy, ga, gb, gc):
        res, sums = [], []
        for t, gg, lo, hi in ((a, ga, 0, mw), (b, gb, mw, mw + sw), (c, gc, mw + sw, mw + sw + dw)):
            _, vjp = jax.vjp(_rms, t, gg)
            dt, dg = vjp(dy[:, lo:hi])
            res.append(dt)
            sums.append(dg)
        return res + sums

    do_mla, do_ssm, do_dil, g["g_out_mla"], g["g_out_ssm"], g["g_out_dil"] = rowwise(
        n("out_norm_bwd"), out_norm_bwd, [(sv["o_mla"], mw, 0), (sv["o_ssm"], sw, 0), (sv["o_dil"], dw, 0), (dyn, D, 0)],
        [lw["g_out_mla"], lw["g_out_ssm"], lw["g_out_dil"]], [(mw, F32), (sw, F32), (dw, F32)], [mw, sw, dw], M=M)
    proj = sv["proj"]
    dqd, dkd, dvd = attention_bwd(n("dil_bwd"), proj, dm.o_qd, proj, dm.o_kd, proj, dm.o_vd, sv["o_dil"], do_dil, sv["lse_dil"],
                                  da=DIL_HEAD, dv=DIL_HEAD, pairs=dw // LANES, scale=DIL_HEAD ** -0.5, kind="dilated", M=M)
    def glu_bwd(zb, db, b):
        _, vjp = jax.vjp(lambda zz, bb: (zz + bb)[:, :sw] * jax.nn.sigmoid((zz + bb)[:, sw:]), zb, b)
        return vjp(db)

    dz, g["b_glu"] = rowwise(n("ssm_gate_bwd"), glu_bwd, [(sv["z"], 2 * sw, 0), (do_ssm, sw, 0)], [lw["b_glu"]], [(2 * sw, BF16)], [2 * sw], M=M)
    dyg = matmul(n("ssm_glu_dx"), dz, lw["w_glu"], "nt", w=("cols", 2 * sw // N_CHIPS), M=M, N=sw, K=2 * sw)
    g["w_glu"] = matmul(n("ssm_glu_dw"), sv["yg"], dz, "tn", M=sw, N=2 * sw, K=M, out_dtype=BF16, into=("cols", 2 * sw // N_CHIPS))

    def gelu_bwd(y, u, dy, d):
        _, vjp = jax.vjp(lambda yy, uu, dd: _gelu(yy + dd * uu), y, u, d)
        return vjp(dy)

    dylin, du1, g["d_skip"] = rowwise(n("ssm_gelu_bwd"), gelu_bwd, [(sv["ylin"], sw, 0), (proj, sw, dm.o_u), (dyg, sw, 0)], [lw["d_skip"]],
                                      [(sw, BF16), (sw, F32)], [sw], M=M)
    seed = matmul(n("ssm_y_dx"), dylin, sp["ccat"], "nt", M=M, N=2 * dm.C, K=sw)
    d_ccat = matmul(n("ssm_y_dw"), sv["hcat"], dylin, "tn", M=2 * dm.C, N=sw, K=M)
    lam, d_acat = ssm_scan(n("ssm_scan_bwd"), seed, sp["acat_conj"], M=M, reverse=True, hcat=sv["hcat"])
    du = matmul(n("ssm_bu_dx"), lam, sp["bcat"], "nt", M=M, N=sw, K=2 * dm.C, add=du1, out_dtype=BF16)
    d_bcat = matmul(n("ssm_bu_dw"), proj, lam, "tn", M=sw, N=2 * dm.C, K=M, a_off=(0, dm.o_u))
    g["ssm_raw"] = (d_acat, d_bcat, d_ccat)
    nw = dm.H * MLA_NOPE
    dqn, dkn, dv_, dqp, dkp = attention_bwd(n("mla_bwd"), sv["q_bf"], 0, sv["kv"], 0, sv["kv"], nw, sv["o_mla"], do_mla, sv["lse_mla"],
                                            da=MLA_NOPE, dv=MLA_V, pairs=dm.H // 2, scale=(MLA_NOPE + MLA_ROPE) ** -0.5, kind="causal", M=M,
                                            qb=sv["q_bf"], qb_off=nw, kb=sv["kpe"])
    cosq, sinq, cosk, sink = tabs
    pw = dm.H * MLA_ROPE
    dqp_u, dkr = rowwise(n("rope_bwd"), lambda a, b, cq, sq, ck, sk: (_rope_t(a, cq, sq), _rope_t(b, ck, sk)),
                         [(dqp, pw, 0), (dkp, LANES, 0), (cosq, pw, 0), (sinq, pw, 0), (cosk, LANES, 0), (sink, LANES, 0)], [],
                         [(pw, BF16), (LANES, BF16)], M=M)
    dq = lane_concat(n("dq_cat"), [dqn, dqp_u], M=M)
    dkv = lane_concat(n("dkv_cat"), [dkn, dv_], M=M)
    dcqn = matmul(n("q_up_dx"), dq, lw["w_uq"], "nt", M=M, N=dm.QL, K=dm.QW)
    g["w_uq"] = matmul(n("q_up_dw"), sv["cqn"], dq, "tn", M=dm.QL, N=dm.QW, K=M, out_dtype=BF16)
    dckvn = matmul(n("kv_up_dx"), dkv, lw["w_ukv"], "nt", M=M, N=dm.KVL, K=dm.KVW)
    g["w_ukv"] = matmul(n("kv_up_dw"), sv["ckvn"], dkv, "tn", M=dm.KVL, N=dm.KVW, K=M, out_dtype=BF16)
    dcq, g["g_q"] = rms_bwd(n("rms_q_bwd"), proj, dm.QL, dm.o_cq, lw["g_q"], dcqn, M=M, out_dtype=BF16)
    dckv, g["g_kv"] = rms_bwd(n("rms_kv_bwd"), proj, dm.KVL, dm.o_ckv, lw["g_kv"], dckvn, M=M, out_dtype=BF16)
    dproj = hook_b(lane_concat(n("dproj_cat"), [dcq, du, dqd, dkd, dvd, dckv, dkr], M=M, pad_to=dm.PW))
    dh1 = matmul(n("in_proj_dx"), dproj, lw["w_in"], "nt", M=M, N=D, K=dm.PW)
    g["w_in"] = matmul(n("in_proj_dw"), sv["h1"], dproj, "tn", M=D, N=dm.PW, K=M, out_dtype=BF16)
    dx_in, g["g_mix"] = rms_bwd(n("rms_mix_bwd"), sv["x_in"], D, 0, lw["g_mix"], dh1, dx_mid, M=M)
    return dx_in, g


def layer_params(dm, small, l):
    lw = {k: small[k][l].reshape(1, -1) for k in ("g_mix", "g_q", "g_kv", "b_glu", "g_out_mla", "g_out_ssm", "g_out_dil", "g_ffn", "d_skip")}
    raw = _ssm_layer_params(dm, small["a_re"][l], small["a_im"][l], small["log_dt"][l], small["b_re"][l], small["b_im"][l])
    ar, ai, bbr, bbi = ssm_param_fwd(f"ssm_param_l{l}", *raw)
    g_ = dm.G
    bd = lambda t: _block_diag(jnp.transpose(t.reshape(SSM_GROUP, g_, SSM_STATE), (1, 0, 2)), g_)
    cd = lambda t: _block_diag(jnp.transpose(t, (0, 2, 1)), g_)
    cre, cim = cd(small["c_re"][l]), cd(small["c_im"][l])
    sp = {"acat": _cat_cols(ar, ai), "acat_conj": _cat_cols(ar, -ai),
          "bcat": _cat_cols(bd(bbr), bd(bbi)).astype(BF16),
          "ccat": _cat_cols(cre.T, -cim.T).T.astype(BF16)}
    return lw, sp, raw


def ssm_param_grads(dm, l, g, raw):
    d_acat, d_bcat, d_ccat = g.pop("ssm_raw")
    d_ar, d_ai = _uncat_cols(d_acat)
    dbr, dbi = _uncat_cols(d_bcat)
    g_ = dm.G
    to_rows = lambda t: jnp.transpose(_diag_blocks(t, g_), (1, 0, 2)).reshape(SSM_GROUP, dm.C)
    da_re, da_im, dldt, db_re, db_im = ssm_param_bwd(f"ssm_param_bwd_l{l}", *raw, d_ar, d_ai, to_rows(dbr), to_rows(dbi))
    dcr, dci = _uncat_cols(d_ccat.T)
    g["a_re"], g["a_im"] = da_re.reshape(g_, SSM_STATE), da_im.reshape(g_, SSM_STATE)
    g["log_dt"] = jnp.sum(dldt.reshape(g_, SSM_STATE), axis=1)
    from_rows = lambda t: jnp.transpose(t.reshape(SSM_GROUP, g_, SSM_STATE), (1, 2, 0))
    g["b_re"], g["b_im"] = from_rows(db_re), from_rows(db_im)
    g["c_re"] = jnp.transpose(_diag_blocks(dcr.T, g_), (0, 2, 1))
    g["c_im"] = -jnp.transpose(_diag_blocks(dci.T, g_), (0, 2, 1))
    g["d_skip"] = g["d_skip"].reshape(g_, SSM_GROUP)


def loss_and_grad(dm, h, target, g_final):
    D = dm.D

    def loss_fn(xb, tb, gb):
        y, vjp = jax.vjp(_rms, xb, gb)
        err = y - tb
        dxb, dg = vjp(err * (1.0 / D))
        part = 0.5 * jnp.sum(jnp.mean(err * err, axis=-1, keepdims=True), axis=0, keepdims=True)
        lane = lax.broadcasted_iota(jnp.int32, (1, LANES), 1)
        return dxb, dg, jnp.where(lane == 0, part, 0.0)

    return rowwise("loss", loss_fn, [(h, D, 0), (target, D, 0)], [g_final.reshape(1, D)], [(D, F32)], [D, LANES], M=dm.M)


KIND = {"w_in": "cols", "w_uq": "cols", "w_ukv": "cols", "w_glu": "cols", "w_o": "rows", "w_gate": "cols", "w_up": "cols", "w_down": "rows"}
SHARDED = list(KIND)
SMALL = ["g_mix", "g_q", "g_kv", "a_re", "a_im", "b_re", "b_im", "c_re", "c_im", "d_skip", "log_dt", "b_glu",
         "g_out_mla", "g_out_ssm", "g_out_dil", "g_ffn", "g_final"]
ORDER = ["g_mix", "w_in", "g_q", "w_uq", "g_kv", "w_ukv", "a_re", "a_im", "b_re", "b_im", "c_re", "c_im", "d_skip", "log_dt",
         "w_glu", "b_glu", "g_out_mla", "g_out_ssm", "g_out_dil", "w_o", "g_ffn", "w_gate", "w_up", "w_down", "g_final"]


def kernel(x, g_mix, w_in, g_q, w_uq, g_kv, w_ukv, a_re, a_im, b_re, b_im, c_re, c_im, d_skip, log_dt, w_glu, b_glu, g_out_mla, g_out_ssm, g_out_dil, w_o, g_ffn, w_gate, w_up, w_down, g_final, loss_target, m_g_mix, m_w_in, m_g_q, m_w_uq, m_g_kv, m_w_ukv, m_a_re, m_a_im, m_b_re, m_b_im, m_c_re, m_c_im, m_d_skip, m_log_dt, m_w_glu, m_b_glu, m_g_out_mla, m_g_out_ssm, m_g_out_dil, m_w_o, m_g_ffn, m_w_gate, m_w_up, m_w_down, m_g_final, v_g_mix, v_w_in, v_g_q, v_w_uq, v_g_kv, v_w_ukv, v_a_re, v_a_im, v_b_re, v_b_im, v_c_re, v_c_im, v_d_skip, v_log_dt, v_w_glu, v_b_glu, v_g_out_mla, v_g_out_ssm, v_g_out_dil, v_w_o, v_g_ffn, v_w_gate, v_w_up, v_w_down, v_g_final):
    args = locals()
    w = {k: args[k] for k in ORDER}
    mom = {k: args["m_" + k] for k in ORDER}
    var = {k: args["v_" + k] for k in ORDER}
    dm = Dims(x, g_q, g_kv, g_out_mla, g_out_ssm, g_out_dil, w_gate.shape[-1] * N_CHIPS)
    L = g_mix.shape[0]

    small = {k: w[k] for k in SMALL}
    na = len(SHARDED)
    tabs = _rope_tables(dm)
    sel = {"w_in": selection_matrices(dm.src_in, w_in.shape[-1]), "w_uq": selection_matrices(dm.src_q, w_uq.shape[-1]),
           "w_ukv": selection_matrices(dm.src_kv, w_ukv.shape[-1])}

    G = [[cast_into_slot(f"cast_{k}_l{l}", w[k], KIND[k], l) for k in SHARDED] for l in range(L)]
    sems_ici, sems_pass = {}, {}

    def ici_start(l, car):
        sems_ici[l], G[l], _, car = split_start(f"ag_ici_start_l{l}", plan_gather_ici, 3 * na, G[l], [], car)
        return car

    def ici_wait(l, car):
        G[l], car = split_wait(f"ag_ici_wait_l{l}", plan_gather_ici, sems_ici[l], G[l], car)
        return car

    def pass_start(l, car):
        sems_pass[l], G[l], _, car = split_start(f"ag_pass_start_l{l}", plan_gather_pass, 3 * na, G[l], [], car)
        return car

    def pass_wait(l, car):
        G[l], car = split_wait(f"ag_pass_wait_l{l}", plan_gather_pass, sems_pass[l], G[l], car)
        return car

    def layer_weights(l):
        lw = dict(zip(SHARDED, G[l]))
        for k in sel:
            lw[k] = regroup_cols(f"regroup_{k}_l{l}", lw[k], sel[k][0])
        return lw

    car = tabs[0]
    for l in range(min(2, L)):
        car = ici_start(l, car)
    tabs = (pass_wait(0, pass_start(0, ici_wait(0, car))),) + tabs[1:]
    h = x.reshape(dm.M, dm.D)
    lws, sps, raws, saved = [], [], [], []
    for l in range(L):
        lw, sp, raw = layer_params(dm, small, l)
        lw.update(layer_weights(l))

        def mid(car, l=l):
            if l + 1 < L:
                car = pass_start(l + 1, ici_wait(l + 1, car))
            if l + 2 < L:
                car = ici_start(l + 2, car)
            return car

        h, sv = layer_forward(dm, l, h, lw, sp, tabs, mid)
        if l + 1 < L:
            h = pass_wait(l + 1, h)
        lws.append(lw)
        sps.append(sp)
        raws.append(raw)
        saved.append(sv)
    dx, g_final_part, loss_part = loss_and_grad(dm, h, loss_target.reshape(dm.M, dm.D), w["g_final"])

    def reduce_begin(l, parts, car):
        fresh = [jax.ShapeDtypeStruct(p.shape[1:], BF16) for p in parts]
        sems, parts, gots, car = split_start(f"rs_pair_start_l{l}", plan_pair(na), na, parts, fresh, car)
        return {"l": l, "sems": sems, "parts": parts, "gots": gots}, car

    def reduce_chips(st, car):
        l = st["l"]
        bufs, car = split_wait(f"rs_pair_wait_l{l}", plan_pair(na), st["sems"], st["parts"] + st["gots"], car)
        sums = [pair_sum(f"rs_pairsum_{a}_l{l}", bufs[i], bufs[na + i]) for i, a in enumerate(SHARDED)]
        fresh = [jax.ShapeDtypeStruct((3,) + s.shape[1:], BF16) for s, _ in sums]
        st["sems"], st["s"], st["arrived"], car = split_start(f"rs_chips_start_l{l}", plan_chips(na), 3 * na, [s for s, _ in sums], fresh, car)
        st["own"] = [o for _, o in sums]
        return car

    def reduce_share(st, car):
        l = st["l"]
        bufs, car = split_wait(f"rs_chips_wait_l{l}", plan_chips(na), st["sems"], st["s"] + st["arrived"], car)
        mine = [chips_sum(f"rs_sum_{a}_l{l}", st["own"][i], bufs[na + i]) for i, a in enumerate(SHARDED)]
        fresh = [jax.ShapeDtypeStruct(m_.shape, F32) for m_ in mine]
        st["sems"], st["mine"], st["theirs"], car = split_start(f"rs_share_start_l{l}", plan_share(na), na, mine, fresh, car)
        return car

    def reduce_end(st, car):
        bufs, car = split_wait(f"rs_share_wait_l{st['l']}", plan_share(na), st["sems"], st["mine"] + st["theirs"], car)
        reduced[st["l"]] = (bufs[:na], bufs[na:])
        return car

    reduced, grads, flying = [None] * L, [None] * L, None
    keep = lambda car: car
    for l in reversed(range(L)):
        st = flying
        dx, g = layer_backward(dm, l, dx, lws[l], sps[l], tabs, saved[l],
                               keep if st is None else functools.partial(reduce_chips, st),
                               keep if st is None else functools.partial(reduce_share, st))
        ssm_param_grads(dm, l, g, raws[l])
        if st is not None:
            dx = reduce_end(st, dx)
        for k in sel:
            g[k] = ungroup_cols(f"ungroup_{k}_l{l}", g[k], sel[k][1])
        flying, dx = reduce_begin(l, [g.pop(k) for k in SHARDED], dx)
        grads[l] = g
    dx = reduce_end(flying, reduce_share(flying, reduce_chips(flying, dx)))

    gsum = {}
    small_names = [k for k in SMALL if k != "g_final"]
    pieces = [jnp.stack([grads[l][k] for l in range(L)]).reshape(-1) for k in small_names] + [g_final_part.reshape(-1), loss_part.reshape(-1)]
    sizes = [int(p.shape[0]) for p in pieces]
    total = sum(sizes)
    rows = -(-total // (LANES * 16)) * 16
    pack = lambda ps: jnp.concatenate(ps + [jnp.zeros((rows * LANES - total,), F32)]).reshape(rows, LANES)
    red = all_reduce_small(pack(pieces))
    flat = red.reshape(-1)
    offs = np.concatenate([[0], np.cumsum(sizes)]).astype(int)
    names = small_names + ["g_final"]
    for i, k in enumerate(names):
        gsum[k] = flat[offs[i]:offs[i + 1]].reshape(w[k].shape)
    loss = flat[offs[len(names)]]

    delta, new_m, new_v = {}, {}, {}
    for i, k in enumerate(SHARDED):
        gsum[k], delta[k], new_m[k], new_v[k] = adamw_layers(f"adam_{k}", w[k], mom[k], var[k], [reduced[l][0][i] for l in range(L)],
                                                             [reduced[l][1][i] for l in range(L)], KIND[k])
    sm_sizes = sizes[:len(names)]
    sm_total = sum(sm_sizes)
    packs = lambda d: jnp.concatenate([d[k].reshape(-1) for k in names] + [jnp.zeros((rows * LANES - sm_total,), F32)]).reshape(rows, LANES)
    gs = jnp.concatenate([flat[:sm_total], jnp.zeros((rows * LANES - sm_total,), F32)]).reshape(rows, LANES)
    d_, m_, v_ = adamw("adam_small", packs(w), gs, packs(mom), packs(var))
    for i, k in enumerate(names):
        sl = slice(offs[i], offs[i + 1])
        delta[k], new_m[k], new_v[k] = (t.reshape(-1)[sl].reshape(w[k].shape) for t in (d_, m_, v_))

    return (loss, dx.reshape(x.shape), *[gsum[k] for k in ORDER], *[delta[k] for k in ORDER],
            *[new_m[k] for k in ORDER], *[new_v[k] for k in ORDER])
```

```python
import functools
import math

import numpy as np
import jax
import jax.numpy as jnp
from jax import lax
from jax.experimental import pallas as pl
from jax.experimental.pallas import tpu as pltpu

F32 = jnp.float32
BF16 = jnp.bfloat16
MESH = pl.DeviceIdType.MESH

NORM_EPS = 1e-6
MLA_NOPE, MLA_ROPE, MLA_V = 128, 64, 128
SSM_GROUP, SSM_STATE = 16, 64
DIL_HEAD = 64
DIL_PATTERNS = ((128, 1), (512, 4), (2048, 16))
ROPE_THETA = 10000.0
ADAM_LR, ADAM_B1, ADAM_B2, ADAM_EPS, ADAM_WD, ADAM_STEP = 0.001, 0.9, 0.999, 1e-08, 0.01, 10
N_CHIPS = 4

LANES = 128
SUBLANES_BF16 = 16
VMEM_LIMIT = 56 * 1024 * 1024
ROW_BUDGET = 20 * 1024 * 1024
MM_BUDGET = 40 * 1024 * 1024
NEG = -1e30


def _cparams(sem=None):
    return pltpu.CompilerParams(dimension_semantics=sem, vmem_limit_bytes=VMEM_LIMIT)


def _pick(n, cap, q, off=0):
    best = None
    for d in range(q, min(n, cap) + 1, q):
        if n % d == 0 and off % d == 0:
            best = d
    if best is None or (best * 4 <= min(cap, n) and n <= 3072 and off % n == 0):
        assert off % n == 0, (n, off)
        return n
    return best


_DOT_DIMS = {"nn": (((1,), (0,)), ((), ())), "nt": (((1,), (1,)), ((), ())), "tn": (((0,), (0,)), ((), ()))}


def _divs(n, q, within=None, off=0):
    return [d for d in range(q, n + 1, q) if n % d == 0 and off % d == 0 and (within is None or within % d == 0)] or [n]


def _mm_tiles(M, N, K, tms, tns, tks, ab, bb, ob):
    best = None
    for tk in tks:
        nk = K // tk
        for tn in tns:
            for tm in tms:
                if 2 * (tm * tk * ab + tk * tn * bb + tm * tn * ob) + tm * tn * 4 * (2 if nk > 1 else 1) > MM_BUDGET:
                    continue
                steps = (M // tm) * (N // tn) * nk
                hbm = M * K * ab * (1 if nk == 1 else N // tn) + K * N * bb * (M // tm) + M * N * ob
                cost = steps * 0.35e-6 + hbm / 3.0e12 + (nk - 1) * M * N * 12 / 4.0e12
                if best is None or cost < best[0]:
                    best = (cost, tm, tn, tk)
    assert best is not None, (M, N, K)
    return best[1:]


def matmul(name, a, b, mode, *, M, N, K, a_off=(0, 0), b_off=(0, 0), b_lead=None, w=None, add=None, out_dtype=F32, into=None):
    tn_mode = mode == "tn"
    a_ro, a_co = (a_off[1], a_off[0]) if tn_mode else a_off
    b_no, b_ko = b_off if mode == "nt" else (b_off[1], b_off[0])
    n_within = k_within = m_within = None
    if w is not None:
        kind, shard = w
        if kind == "cols":
            b = b.reshape(N_CHIPS, b.shape[1] * b.shape[2], b.shape[3])
        rows_within, cols_within = (K if mode == "nn" else N, shard) if kind == "cols" else (shard, (N if mode == "nn" else K) // 2)
        k_within, n_within = (rows_within, cols_within) if mode == "nn" else (cols_within, rows_within)
    if into is not None:
        m_within, n_within = (M // 2, into[1]) if into[0] == "cols" else (into[1], N // 2)
    tms = [d for d in _divs(M, 128 if tn_mode else SUBLANES_BF16, m_within, a_ro) if d <= 1408]
    tns = [d for d in _divs(N, LANES, n_within, b_no) if d <= 2048]
    tks = _divs(K, SUBLANES_BF16 if tn_mode else LANES, k_within, math.gcd(a_co, b_ko))
    ob = jnp.dtype(out_dtype).itemsize + (add.dtype.itemsize if add is not None else 0)
    tm, tn, tk = _mm_tiles(M, N, K, tms, tns, tks, a.dtype.itemsize, b.dtype.itemsize, ob)
    nk = K // tk
    dn = _DOT_DIMS[mode]

    if tn_mode:
        a_spec = pl.BlockSpec((tk, tm), lambda i, j, k: (k + a_co // tk, i + a_ro // tm))
    else:
        a_spec = pl.BlockSpec((tm, tk), lambda i, j, k: (i + a_ro // tm, k + a_co // tk))
    b_blk = (tn, tk) if mode == "nt" else (tk, tn)
    if w is not None:
        tr_, tc_ = (tk, tn) if mode == "nn" else (tn, tk)
        rper, cper = rows_within // tr_, cols_within // tc_

        def wmap(rb, cb):
            if kind == "cols":
                return (cb // cper, rb, cb % cper)
            return (rb // rper, cb // cper, rb % rper, cb % cper)

        imap = (lambda i, j, k: wmap(k, j)) if mode == "nn" else (lambda i, j, k: wmap(j, k))
        b_spec = pl.BlockSpec((None,) * (b.ndim - 2) + b_blk, imap)
    else:
        if mode == "nt":
            imap = lambda i, j, k: (j + b_no // tn, k + b_ko // tk)
        else:
            imap = lambda i, j, k: (k + b_ko // tk, j + b_no // tn)
        if b_lead is None:
            b_spec = pl.BlockSpec(b_blk, imap)
        else:
            b_spec = pl.BlockSpec((None,) + b_blk, lambda i, j, k: (b_lead,) + imap(i, j, k))
    o_plain = pl.BlockSpec((tm, tn), lambda i, j, k: (i, j))
    if into is None:
        o_spec, out_shape = o_plain, jax.ShapeDtypeStruct((M, N), out_dtype)
    else:
        rper, cper = m_within // tm, n_within // tn
        if into[0] == "cols":
            o_spec = pl.BlockSpec((None, None, tm, tn), lambda i, j, k: (i // rper, j // cper, i % rper, j % cper))
        else:
            o_spec = pl.BlockSpec((None, None, tm, tn), lambda i, j, k: (j // cper, i // rper, i % rper, j % cper))
        out_shape = jax.ShapeDtypeStruct((2, N_CHIPS, m_within, n_within), out_dtype)
    has_add = add is not None
    n_in = 2 + has_add

    def body(*refs):
        a_ref, b_ref = refs[0], refs[1]
        add_ref = refs[2] if has_add else None
        o_ref = refs[n_in]
        part = lax.dot_general(a_ref[...].astype(BF16), b_ref[...].astype(BF16), dn, preferred_element_type=F32)

        def finish(r):
            if has_add:
                r = r + add_ref[...].astype(F32)
            o_ref[...] = r.astype(o_ref.dtype)

        if nk == 1:
            finish(part)
        else:
            acc_ref = refs[-1]
            k = pl.program_id(2)

            @pl.when(k == 0)
            def _():
                acc_ref[...] = part

            @pl.when((k > 0) & (k < nk - 1))
            def _():
                acc_ref[...] += part

            @pl.when(k == nk - 1)
            def _():
                finish(acc_ref[...] + part)

    in_specs = [a_spec, b_spec] + ([o_plain] if has_add else [])
    args = (a, b) + ((add,) if has_add else ())
    return pl.pallas_call(
        body, name=name, out_shape=out_shape, grid=(M // tm, N // tn, nk), in_specs=in_specs, out_specs=o_spec,
        scratch_shapes=[pltpu.VMEM((tm, tn), F32)] if nk > 1 else [],
        compiler_params=_cparams(("parallel", "parallel", "arbitrary")),
    )(*args)


def selection_matrices(src_of_new, n_shard):
    src_np = np.asarray(src_of_new, np.int64)
    src = jnp.asarray(src_np.astype(np.int32))
    ref = jnp.arange(N_CHIPS, dtype=jnp.int32)[:, None] * n_shard + jnp.arange(n_shard, dtype=jnp.int32)[None, :]
    pm = (ref[:, :, None] == src[None, None, :]).astype(BF16)
    pmt = (src[None, :, None] == ref[:, None, :]).astype(BF16)
    tc = _pick(len(src_np), 512, LANES)
    feeds = [sorted({int(s) // n_shard for s in src_np[cb * tc:(cb + 1) * tc] if s >= 0}) for cb in range(len(src_np) // tc)]
    return pm, pmt, tc, feeds


def regroup_cols(name, g, sel):
    pm, _, tc, feeds = sel
    nn = g.shape[-1]
    g = g.reshape(N_CHIPS, -1, nn)
    K, n_new = g.shape[1], pm.shape[-1]
    tm = _pick(K, 512, SUBLANES_BF16)

    def body(g_ref, pm_ref, o_ref):
        for cb, chips in enumerate(feeds):
            @pl.when(pl.program_id(0) == cb)
            def _(chips=chips):
                acc = jnp.zeros((tm, tc), F32)
                for j in chips:
                    acc = acc + _dot(g_ref[j], pm_ref[j], "nn")
                o_ref[...] = acc.astype(o_ref.dtype)

    return pl.pallas_call(
        body, name=name, out_shape=jax.ShapeDtypeStruct((K, n_new), BF16), grid=(n_new // tc, K // tm),
        in_specs=[pl.BlockSpec((N_CHIPS, tm, nn), lambda c, i: (0, i, 0)), pl.BlockSpec((N_CHIPS, nn, tc), lambda c, i: (0, 0, c))],
        out_specs=pl.BlockSpec((tm, tc), lambda c, i: (i, c)), compiler_params=_cparams(("parallel", "parallel")),
    )(g, pm)


def ungroup_cols(name, dw, sel):
    _, pmt, tc, feeds = sel
    K, n_new = dw.shape
    nn = pmt.shape[-1]
    kh = K // 2
    tm = _pick(kh, 512, SUBLANES_BF16)
    hb = kh // tm
    fed_by = [[cb for cb, chips in enumerate(feeds) if j in chips] for j in range(N_CHIPS)]

    def body(dw_ref, pmt_ref, o_ref):
        for j, blocks in enumerate(fed_by):
            @pl.when(pl.program_id(0) == j)
            def _(blocks=blocks):
                acc = jnp.zeros((tm, nn), F32)
                for cb in blocks:
                    acc = acc + _dot(dw_ref[:, cb * tc:(cb + 1) * tc], pmt_ref[cb * tc:(cb + 1) * tc, :], "nn")
                o_ref[...] = acc.astype(o_ref.dtype)

    return pl.pallas_call(
        body, name=name, out_shape=jax.ShapeDtypeStruct((2, N_CHIPS, kh, nn), BF16), grid=(N_CHIPS, 2 * hb),
        in_specs=[pl.BlockSpec((tm, n_new), lambda j, i: (i, 0)), pl.BlockSpec((None, n_new, nn), lambda j, i: (j, 0, 0))],
        out_specs=pl.BlockSpec((None, None, tm, nn), lambda j, i: (i // hb, j, i % hb, 0)),
        compiler_params=_cparams(("parallel", "parallel")),
    )(dw, pmt)


def rowwise(name, fn, rows, vecs, outs, sums=(), *, M):
    rows = [tuple(r) + (0,) * (4 - len(r)) for r in rows]
    nr, nv, no, ns = len(rows), len(vecs), len(outs), len(sums)
    per_row = sum(w * a.dtype.itemsize for a, w, _, _ in rows) + sum(w * jnp.dtype(d).itemsize for w, d in outs)
    tr = _pick(M, max(8, min(512, ROW_BUDGET // (2 * per_row))), 16 if M % 16 == 0 else 8)

    def body(*refs):
        i = pl.program_id(0)
        res = fn(*[r[...] for r in refs[:nr + nv]])
        o_refs = refs[nr + nv:nr + nv + no]
        s_refs = refs[nr + nv + no:]
        for ref, val in zip(o_refs, res[:no]):
            ref[...] = val.astype(ref.dtype)
        if ns:
            @pl.when(i == 0)
            def _():
                for ref in s_refs:
                    ref[...] = jnp.zeros(ref.shape, F32)

            for ref, val in zip(s_refs, res[no:]):
                ref[...] += val

    in_specs = [pl.BlockSpec((tr, w), functools.partial(lambda i, cb, rb: (i + rb, cb), cb=off // w, rb=roff // tr)) for _, w, off, roff in rows]
    for _, w, off, roff in rows:
        assert off % w == 0 and roff % tr == 0
    in_specs += [pl.BlockSpec(v.shape, functools.partial(lambda i, nd: (0,) * nd, nd=v.ndim)) for v in vecs]
    out_specs = [pl.BlockSpec((tr, w), lambda i: (i, 0)) for w, _ in outs]
    out_specs += [pl.BlockSpec((1, w), lambda i: (0, 0)) for w in sums]
    out_shape = [jax.ShapeDtypeStruct((M, w), d) for w, d in outs] + [jax.ShapeDtypeStruct((1, w), F32) for w in sums]
    return pl.pallas_call(
        body, name=name, out_shape=out_shape, grid=(M // tr,), in_specs=in_specs, out_specs=out_specs,
        compiler_params=_cparams(("arbitrary",) if ns else ("parallel",)),
    )(*[r[0] for r in rows], *vecs)


def _rms(x, g):
    xf = x.astype(F32)
    return xf * lax.rsqrt(jnp.mean(xf * xf, axis=-1, keepdims=True) + NORM_EPS) * g


def _gelu(y):
    return 0.5 * y * (1.0 + jnp.tanh(math.sqrt(2.0 / math.pi) * (y + 0.044715 * (y * y * y))))


def _colsum(v):
    return jnp.sum(v, axis=0, keepdims=True)


def rms_fwd(name, x, width, off, g, *, M):
    return rowwise(name, lambda xb, gb: (_rms(xb, gb),), [(x, width, off)], [g], [(width, BF16)], M=M)[0]


def rms_bwd(name, x, width, off, g, dy, resid=None, *, M, out_dtype=F32):
    def fn(xb, dyb, *rest):
        gb = rest[-1]
        _, vjp = jax.vjp(_rms, xb.astype(F32), gb)
        dx, dg = vjp(dyb.astype(F32))
        if resid is not None:
            dx = dx + rest[0]
        return dx, dg

    rows = [(x, width, off), (dy, width, 0)] + ([(resid, width, 0)] if resid is not None else [])
    return rowwise(name, fn, rows, [g], [(width, out_dtype)], [width], M=M)


def lane_concat(name, parts, *, M, pad_to=None):
    width = sum(p.shape[1] for p in parts)
    pad = 0 if pad_to is None else pad_to - width

    def fn(*blocks):
        cols = [b.astype(BF16) for b in blocks]
        if pad:
            cols.append(jnp.zeros((blocks[0].shape[0], pad), BF16))
        return (jnp.concatenate(cols, axis=1),)

    return rowwise(name, fn, [(p, p.shape[1], 0) for p in parts], [], [(width + pad, BF16)], M=M)[0]


def _swap_halves(x, half):
    w = x.shape[-1]
    lane = lax.broadcasted_iota(jnp.int32, x.shape, x.ndim - 1)
    first = (lane % (2 * half)) < half
    return jnp.where(first, pltpu.roll(x, w - half, x.ndim - 1), pltpu.roll(x, half, x.ndim - 1))


def dilated_bias(M):
    delta = jnp.arange(M, dtype=jnp.int32)[:, None] - jnp.arange(M, dtype=jnp.int32)[None, :]
    w = jnp.zeros(delta.shape, F32)
    for window, dil in DIL_PATTERNS:
        ok = (delta >= 0) & (delta <= window)
        if dil > 1:
            ok = ok & ((delta & (dil - 1)) == 0)
        w = w + ok.astype(F32)
    return jnp.where(w > 0, jnp.log(jnp.maximum(w, 1.0)), NEG)


def _dot(a, b, mode):
    return lax.dot_general(a, b, _DOT_DIMS[mode], preferred_element_type=F32)


def attention_fwd(name, qa, qa_off, ka, ka_off, v, v_off, *, da, dv, pairs, scale, M, qb=None, qb_off=0, kb=None, bias=None):
    tq = min(256, M)
    tk = min(512, M)
    has_b = qb is not None
    has_bias = bias is not None
    dr = MLA_ROPE

    def body(*refs):
        refs = list(refs)
        bias_ref = refs.pop(3) if has_bias else None
        if has_b:
            qa_ref, ka_ref, v_ref, qb_ref, kb_ref, o_ref, lse_ref = refs
        else:
            qa_ref, ka_ref, v_ref, o_ref, lse_ref = refs
        i = pl.program_id(1)
        t0 = i * tq
        nkb = (t0 + tq + tk - 1) // tk
        n_full = nkb if has_bias else t0 // tk
        o_parts, lse_parts = [], []
        for hh in range(2):
            q1 = qa_ref[:, hh * da:(hh + 1) * da].astype(BF16)
            q2 = qb_ref[:, hh * dr:(hh + 1) * dr].astype(BF16) if has_b else None

            def step(kbi, carry, masked, hh=hh, q1=q1, q2=q2):
                m, l, acc = carry
                ks = pl.multiple_of(kbi * tk, tk)
                k1 = ka_ref[pl.ds(ks, tk), hh * da:(hh + 1) * da].astype(BF16)
                s = _dot(q1, k1, "nt")
                if has_b:
                    s = s + _dot(q2, kb_ref[pl.ds(ks, tk), 0:dr].astype(BF16), "nt")
                s = s * scale
                if has_bias:
                    s = s + bias_ref[:, pl.ds(ks, tk)]
                elif masked:
                    delta = (t0 + lax.broadcasted_iota(jnp.int32, (tq, tk), 0)) - (ks + lax.broadcasted_iota(jnp.int32, (tq, tk), 1))
                    s = jnp.where(delta >= 0, s, NEG)
                m_new = jnp.maximum(m, jnp.max(s, axis=1, keepdims=True))
                alpha = jnp.exp(m - m_new)
                p = jnp.exp(s - m_new)
                l = alpha * l + jnp.sum(p, axis=1, keepdims=True)
                vv = v_ref[pl.ds(ks, tk), hh * dv:(hh + 1) * dv].astype(BF16)
                acc = alpha * acc + _dot(p.astype(BF16), vv, "nn")
                return m_new, l, acc

            carry = (jnp.full((tq, 1), NEG, F32), jnp.zeros((tq, 1), F32), jnp.zeros((tq, dv), F32))
            carry = lax.fori_loop(0, n_full, functools.partial(step, masked=False), carry)
            m, l, acc = lax.fori_loop(n_full, nkb, functools.partial(step, masked=True), carry)
            o_parts.append(acc / l)
            lse_parts.append(m + jnp.log(l))
        o_ref[...] = jnp.concatenate(o_parts, axis=1)
        lane = lax.broadcasted_iota(jnp.int32, (tq, LANES), 1)
        lse_ref[...] = jnp.where(lane == 0, lse_parts[0], jnp.where(lane == 1, lse_parts[1], 0.0))

    assert qa_off % (2 * da) == 0 and ka_off % (2 * da) == 0 and v_off % (2 * dv) == 0
    in_specs = [
        pl.BlockSpec((tq, 2 * da), lambda hp, i: (i, qa_off // (2 * da) + hp)),
        pl.BlockSpec((M, 2 * da), lambda hp, i: (0, ka_off // (2 * da) + hp)),
        pl.BlockSpec((M, 2 * dv), lambda hp, i: (0, v_off // (2 * dv) + hp)),
    ]
    args = [qa, ka, v]
    if has_bias:
        in_specs.append(pl.BlockSpec((tq, M), lambda hp, i: (i, 0)))
        args.append(bias)
    if has_b:
        assert qb_off % LANES == 0
        in_specs += [pl.BlockSpec((tq, LANES), lambda hp, i: (i, qb_off // LANES + hp)),
                     pl.BlockSpec((M, LANES), lambda hp, i: (0, 0))]
        args += [qb, kb]
    out_specs = [pl.BlockSpec((tq, 2 * dv), lambda hp, i: (i, hp)),
                 pl.BlockSpec((None, tq, LANES), lambda hp, i: (hp, i, 0))]
    out_shape = [jax.ShapeDtypeStruct((M, pairs * 2 * dv), F32), jax.ShapeDtypeStruct((pairs, M, LANES), F32)]
    return pl.pallas_call(
        body, name=name, out_shape=out_shape, grid=(pairs, M // tq), in_specs=in_specs, out_specs=out_specs,
        compiler_params=_cparams(("parallel", "arbitrary")),
    )(*args)


def attention_bwd(name, qa, qa_off, ka, ka_off, v, v_off, o, do, lse, *, da, dv, pairs, scale, M,
                  qb=None, qb_off=0, kb=None, bias=None):
    tq = min(256, M)
    tk = min(256, M)
    has_b = qb is not None
    has_bias = bias is not None
    dr = MLA_ROPE

    def body(*refs):
        refs = list(refs)
        bias_ref = refs.pop(6) if has_bias else None
        if has_b:
            qa_ref, ka_ref, v_ref, o_ref, do_ref, lse_ref, qb_ref, kb_ref, dqa_ref, dka_ref, dv_ref, dqb_ref, dkb_ref = refs
        else:
            qa_ref, ka_ref, v_ref, o_ref, do_ref, lse_ref, dqa_ref, dka_ref, dv_ref = refs
        hp = pl.program_id(0)
        i = pl.program_id(1)
        t0 = i * tq
        nkb = (t0 + tq + tk - 1) // tk
        n_full = nkb if has_bias else t0 // tk

        @pl.when(i == 0)
        def _():
            dka_ref[...] = jnp.zeros(dka_ref.shape, F32)
            dv_ref[...] = jnp.zeros(dv_ref.shape, F32)

        if has_b:
            @pl.when((i == 0) & (hp == 0))
            def _():
                dkb_ref[...] = jnp.zeros(dkb_ref.shape, F32)

        dq1_parts, dq2_parts = [], []
        for hh in range(2):
            q1 = qa_ref[:, hh * da:(hh + 1) * da].astype(BF16)
            q2 = qb_ref[:, hh * dr:(hh + 1) * dr].astype(BF16) if has_b else None
            do_h = do_ref[:, hh * dv:(hh + 1) * dv]
            o_h = o_ref[:, hh * dv:(hh + 1) * dv]
            rowdot = jnp.sum(do_h * o_h, axis=1, keepdims=True)
            do_bf = do_h.astype(BF16)
            lse_h = lse_ref[:, hh:hh + 1]

            def step(kbi, carry, masked, hh=hh, q1=q1, q2=q2, do_bf=do_bf, rowdot=rowdot, lse_h=lse_h):
                dq1, dq2 = carry
                ks = pl.multiple_of(kbi * tk, tk)
                k1 = ka_ref[pl.ds(ks, tk), hh * da:(hh + 1) * da].astype(BF16)
                s = _dot(q1, k1, "nt")
                if has_b:
                    k2 = kb_ref[pl.ds(ks, tk), 0:dr].astype(BF16)
                    s = s + _dot(q2, k2, "nt")
                s = s * scale
                if has_bias:
                    s = s + bias_ref[:, pl.ds(ks, tk)]
                elif masked:
                    delta = (t0 + lax.broadcasted_iota(jnp.int32, (tq, tk), 0)) - (ks + lax.broadcasted_iota(jnp.int32, (tq, tk), 1))
                    s = jnp.where(delta >= 0, s, NEG)
                p = jnp.exp(s - lse_h)
                vv = v_ref[pl.ds(ks, tk), hh * dv:(hh + 1) * dv].astype(BF16)
                dp = _dot(do_bf, vv, "nt")
                ds = (p * (dp - rowdot) * scale).astype(BF16)
                dq1 = dq1 + _dot(ds, k1, "nn")
                dka_ref[pl.ds(ks, tk), hh * da:(hh + 1) * da] += _dot(ds, q1, "tn")
                dv_ref[pl.ds(ks, tk), hh * dv:(hh + 1) * dv] += _dot(p.astype(BF16), do_bf, "tn")
                if has_b:
                    dq2 = dq2 + _dot(ds, k2, "nn")
                    dkb_ref[pl.ds(ks, tk), 0:dr] += _dot(ds, q2, "tn")
                return dq1, dq2

            carry = lax.fori_loop(0, n_full, functools.partial(step, masked=False), (jnp.zeros((tq, da), F32), jnp.zeros((tq, dr), F32)))
            dq1, dq2 = lax.fori_loop(n_full, nkb, functools.partial(step, masked=True), carry)
            dq1_parts.append(dq1)
            dq2_parts.append(dq2)
        dqa_ref[...] = jnp.concatenate(dq1_parts, axis=1).astype(dqa_ref.dtype)
        if has_b:
            dqb_ref[...] = jnp.concatenate(dq2_parts, axis=1).astype(dqb_ref.dtype)

    in_specs = [
        pl.BlockSpec((tq, 2 * da), lambda hp, i: (i, qa_off // (2 * da) + hp)),
        pl.BlockSpec((M, 2 * da), lambda hp, i: (0, ka_off // (2 * da) + hp)),
        pl.BlockSpec((M, 2 * dv), lambda hp, i: (0, v_off // (2 * dv) + hp)),
        pl.BlockSpec((tq, 2 * dv), lambda hp, i: (i, hp)),
        pl.BlockSpec((tq, 2 * dv), lambda hp, i: (i, hp)),
        pl.BlockSpec((None, tq, LANES), lambda hp, i: (hp, i, 0)),
    ]
    args = [qa, ka, v, o, do, lse]
    if has_bias:
        in_specs.append(pl.BlockSpec((tq, M), lambda hp, i: (i, 0)))
        args.append(bias)
    out_specs = [pl.BlockSpec((tq, 2 * da), lambda hp, i: (i, hp)),
                 pl.BlockSpec((M, 2 * da), lambda hp, i: (0, hp)),
                 pl.BlockSpec((M, 2 * dv), lambda hp, i: (0, hp))]
    out_shape = [jax.ShapeDtypeStruct((M, pairs * 2 * da), BF16),
                 jax.ShapeDtypeStruct((M, pairs * 2 * da), F32),
                 jax.ShapeDtypeStruct((M, pairs * 2 * dv), F32)]
    if has_b:
        in_specs += [pl.BlockSpec((tq, LANES), lambda hp, i: (i, qb_off // LANES + hp)),
                     pl.BlockSpec((M, LANES), lambda hp, i: (0, 0))]
        args += [qb, kb]
        out_specs += [pl.BlockSpec((tq, LANES), lambda hp, i: (i, hp)), pl.BlockSpec((M, LANES), lambda hp, i: (0, 0))]
        out_shape += [jax.ShapeDtypeStruct((M, pairs * LANES), F32), jax.ShapeDtypeStruct((M, LANES), F32)]
    return pl.pallas_call(
        body, name=name, out_shape=out_shape, grid=(pairs, M // tq), in_specs=in_specs, out_specs=out_specs,
        compiler_params=_cparams(("arbitrary", "arbitrary")),
    )(*args)


def ssm_scan(name, xcat, acat, *, M, reverse=False, hcat=None):
    C2 = xcat.shape[1]
    cb = LANES
    tb = min(128, M)
    nblk = M // tb
    with_da = hcat is not None

    def body(*refs):
        if with_da:
            x_ref, a_ref, h_ref, o_ref, da_ref, p_ref = refs
        else:
            x_ref, a_ref, o_ref, p_ref = refs
        ar, ai = a_ref[:, :cb], a_ref[:, cb:]
        row = lax.broadcasted_iota(jnp.int32, (tb, cb), 0)

        def logscan(xr, xi):
            pr, pi = ar, ai
            d = 1
            while d < tb:
                shift = tb - d if reverse else d
                keep = (row < tb - d) if reverse else (row >= d)
                sr = jnp.where(keep, pltpu.roll(xr, shift, 0), 0.0)
                si = jnp.where(keep, pltpu.roll(xi, shift, 0), 0.0)
                xr, xi = xr + pr * sr - pi * si, xi + pr * si + pi * sr
                pr, pi = pr * pr - pi * pi, 2.0 * pr * pi
                d *= 2
            return xr, xi

        seed = row == (tb - 1 if reverse else 0)
        p0r, p0i = logscan(jnp.where(seed, ar, 0.0), jnp.where(seed, ai, 0.0))
        p_ref[:, :cb] = p0r
        p_ref[:, cb:] = p0i
        sub = lax.broadcasted_iota(jnp.int32, (8, cb), 0)
        edge = 0 if reverse else tb - 8
        pick = sub == (0 if reverse else 7)

        def blk(b, carry):
            cr, ci = carry
            bb = (nblk - 1 - b) if reverse else b
            t0 = pl.multiple_of(bb * tb, tb)
            hr, hi = logscan(x_ref[pl.ds(t0, tb), :cb], x_ref[pl.ds(t0, tb), cb:])
            pr, pi = p_ref[:, :cb], p_ref[:, cb:]
            o_ref[pl.ds(t0, tb), :cb] = hr + pr * cr - pi * ci
            o_ref[pl.ds(t0, tb), cb:] = hi + pr * ci + pi * cr
            te = pl.multiple_of(t0 + edge, 8)
            ncr = jnp.sum(jnp.where(pick, o_ref[pl.ds(te, 8), :cb], 0.0), axis=0, keepdims=True)
            nci = jnp.sum(jnp.where(pick, o_ref[pl.ds(te, 8), cb:], 0.0), axis=0, keepdims=True)
            return ncr, nci

        lax.fori_loop(0, nblk, blk, (jnp.zeros((1, cb), F32), jnp.zeros((1, cb), F32)))
        if with_da:
            first = lax.broadcasted_iota(jnp.int32, (M, cb), 0) >= 1
            hpr = jnp.where(first, pltpu.roll(h_ref[:, :cb], 1, 0), 0.0)
            hpi = jnp.where(first, pltpu.roll(h_ref[:, cb:], 1, 0), 0.0)
            lr, li = o_ref[:, :cb], o_ref[:, cb:]
            da_ref[:, :cb] = _colsum(lr * hpr + li * hpi)
            da_ref[:, cb:] = _colsum(li * hpr - lr * hpi)

    blk_spec = pl.BlockSpec((M, 2 * cb), lambda j: (0, j))
    vec_spec = pl.BlockSpec((1, 2 * cb), lambda j: (0, j))
    in_specs = [blk_spec, vec_spec] + ([blk_spec] if with_da else [])
    out_specs = [blk_spec] + ([vec_spec] if with_da else [])
    out_shape = [jax.ShapeDtypeStruct((M, C2), F32)] + ([jax.ShapeDtypeStruct((1, C2), F32)] if with_da else [])
    args = [xcat, acat] + ([hcat] if with_da else [])
    res = pl.pallas_call(
        body, name=name, out_shape=out_shape, grid=(C2 // (2 * cb),), in_specs=in_specs, out_specs=out_specs,
        scratch_shapes=[pltpu.VMEM((tb, 2 * cb), F32)], compiler_params=_cparams(("parallel",)),
    )(*args)
    return res if with_da else res[0]


def _ssm_param_fn(a_re, a_im, ldt, b_re, b_im):
    lr, li = jnp.minimum(a_re, -1e-4), a_im
    dt = jnp.exp(ldt)
    e, ang = jnp.exp(lr * dt), li * dt
    ar, ai = e * jnp.cos(ang), e * jnp.sin(ang)
    den = lr * lr + li * li
    nr, ni = ar - 1.0, ai
    cr, ci = (nr * lr + ni * li) / den, (ni * lr - nr * li) / den
    return ar, ai, cr * b_re - ci * b_im, cr * b_im + ci * b_re


def _whole(shape):
    return pl.BlockSpec(shape, functools.partial(lambda nd: (0,) * nd, nd=len(shape)))


def ssm_param_fwd(name, a_re, a_im, ldt, b_re, b_im):
    def body(*refs):
        res = _ssm_param_fn(*[r[...] for r in refs[:5]])
        for ref, val in zip(refs[5:], res):
            ref[...] = val

    ins = [a_re, a_im, ldt, b_re, b_im]
    outs = [a_re, a_re, b_re, b_re]
    return pl.pallas_call(
        body, name=name, out_shape=[jax.ShapeDtypeStruct(t.shape, F32) for t in outs],
        in_specs=[_whole(t.shape) for t in ins], out_specs=[_whole(t.shape) for t in outs], compiler_params=_cparams(),
    )(*ins)


def ssm_param_bwd(name, a_re, a_im, ldt, b_re, b_im, d_ar, d_ai, d_bbr, d_bbi):
    def body(*refs):
        _, vjp = jax.vjp(_ssm_param_fn, *[r[...] for r in refs[:5]])
        res = vjp(tuple(r[...] for r in refs[5:9]))
        for ref, val in zip(refs[9:], res):
            ref[...] = val

    ins = [a_re, a_im, ldt, b_re, b_im, d_ar, d_ai, d_bbr, d_bbi]
    outs = [a_re, a_im, ldt, b_re, b_im]
    return pl.pallas_call(
        body, name=name, out_shape=[jax.ShapeDtypeStruct(t.shape, F32) for t in outs],
        in_specs=[_whole(t.shape) for t in ins], out_specs=[_whole(t.shape) for t in outs], compiler_params=_cparams(),
    )(*ins)


ANY = pl.BlockSpec(memory_space=pl.ANY)


def _place():
    x, y, c = lax.axis_index("x"), lax.axis_index("y"), lax.axis_index("c")
    chips = [(1 - x, y), (x, 1 - y), (1 - x, 1 - y)]
    return x, y, c, chips


def cast_into_slot(name, w, kind, l=None):
    K, nn = w.shape[-2:]
    hr, hc = (K // 2, nn) if kind == "cols" else (K, nn // 2)
    tr = _pick(hr, max(16, min(512, ROW_BUDGET // (2 * hc * 6))), SUBLANES_BF16)
    nb = hr // tr

    def body(w_ref, o_ref):
        o_ref[...] = w_ref[...].astype(BF16)

    lead = () if l is None else (l,)
    if kind == "cols":
        in_spec = pl.BlockSpec((None,) * len(lead) + (tr, hc), lambda h, i: lead + (h * nb + i, 0))
    else:
        in_spec = pl.BlockSpec((None,) * len(lead) + (tr, hc), lambda h, i: lead + (i, h))
    return pl.pallas_call(
        body, name=name, out_shape=jax.ShapeDtypeStruct((N_CHIPS, 2, hr, hc), BF16), grid=(2, nb), in_specs=[in_spec],
        out_specs=pl.BlockSpec((None, None, tr, hc), lambda h, i: (2 * lax.axis_index("x") + lax.axis_index("y"), h, i, 0)),
        compiler_params=_cparams(("parallel", "parallel")),
    )(w)


HBM_SPEC = pl.BlockSpec(memory_space=pltpu.HBM)
SEM_SPEC = pl.BlockSpec(memory_space=pltpu.SEMAPHORE)
SPLIT_PARAMS = pltpu.CompilerParams(has_side_effects=pltpu.SideEffectType.DATAFLOW_SIDE_EFFECTING)


def _in_hbm(t):
    return pltpu.with_memory_space_constraint(t, pltpu.HBM)


def split_start(name, plan, n, bufs, fresh, carrier):
    nb, nf = len(bufs), len(fresh)

    def body(*refs):
        outs = refs[nb + 1:]
        for i, (s, d, dev) in enumerate(plan(list(outs[2:2 + nb + nf]))):
            pltpu.make_async_remote_copy(src_ref=s, dst_ref=d, send_sem=outs[0].at[i], recv_sem=outs[1].at[i],
                                         device_id=dev, device_id_type=MESH).start()

    hbm = lambda t: pltpu.HBM(t.shape, t.dtype)
    res = pl.pallas_call(
        body, name=name,
        out_shape=(pltpu.SemaphoreType.DMA((n,)), pltpu.SemaphoreType.DMA((n,)), *[hbm(t) for t in bufs], *[hbm(t) for t in fresh], hbm(carrier)),
        in_specs=[HBM_SPEC] * (nb + 1), out_specs=(SEM_SPEC, SEM_SPEC) + (HBM_SPEC,) * (nb + nf + 1),
        input_output_aliases={**{i: 2 + i for i in range(nb)}, nb: 2 + nb + nf}, compiler_params=SPLIT_PARAMS,
    )(*[_in_hbm(t) for t in bufs], _in_hbm(carrier))
    return (res[0], res[1]), list(res[2:2 + nb]), list(res[2 + nb:2 + nb + nf]), res[2 + nb + nf]


def split_wait(name, plan, sems, bufs, carrier):
    nb = len(bufs)

    def body(*refs):
        for i, (s, d, dev) in enumerate(plan(list(refs[:nb]))):
            cp = pltpu.make_async_remote_copy(src_ref=s, dst_ref=d, send_sem=refs[nb].at[i], recv_sem=refs[nb + 1].at[i],
                                              device_id=dev, device_id_type=MESH)
            cp.wait_send()
            cp.wait_recv()

    hbm = lambda t: pltpu.HBM(t.shape, t.dtype)
    res = pl.pallas_call(
        body, name=name, out_shape=(*[hbm(t) for t in bufs], hbm(carrier)),
        in_specs=[HBM_SPEC] * nb + [SEM_SPEC, SEM_SPEC, HBM_SPEC], out_specs=(HBM_SPEC,) * (nb + 1),
        input_output_aliases={**{i: i for i in range(nb)}, nb + 2: nb}, compiler_params=SPLIT_PARAMS,
    )(*bufs, sems[0], sems[1], carrier)
    return list(res[:nb]), res[nb]


def _me_sib_chips():
    x, y, c, chips = _place()
    return 2 * x + y, c, (x, y, 1 - c), chips


def plan_gather_ici(refs):
    me, c, _, chips = _me_sib_chips()
    return [(r.at[me, c], r.at[me, c], (chip[0], chip[1], c)) for r in refs for chip in chips]


def plan_gather_pass(refs):
    _, c, sib, chips = _me_sib_chips()
    return [(r.at[2 * chip[0] + chip[1], c], r.at[2 * chip[0] + chip[1], c], sib) for r in refs for chip in chips]


def plan_pair(n_arrays):
    def plan(refs):
        _, c, sib, _ = _me_sib_chips()
        return [(refs[a].at[1 - c], refs[n_arrays + a], sib) for a in range(n_arrays)]
    return plan


def plan_chips(n_arrays):
    def plan(refs):
        _, c, _, chips = _me_sib_chips()
        return [(refs[a].at[2 * chip[0] + chip[1]], refs[n_arrays + a].at[k], (chip[0], chip[1], c))
                for a in range(n_arrays) for k, chip in enumerate(chips)]
    return plan


def plan_share(n_arrays):
    def plan(refs):
        _, _, sib, _ = _me_sib_chips()
        return [(refs[a], refs[n_arrays + a], sib) for a in range(n_arrays)]
    return plan


def swap_with_sibling(name, src, pick_other_half):
    shape = src.shape[1:] if pick_other_half else src.shape

    def body(src_ref, out_ref, ssem, rsem):
        x, y, c, _ = _place()
        cp = pltpu.make_async_remote_copy(src_ref=src_ref.at[1 - c] if pick_other_half else src_ref, dst_ref=out_ref,
                                          send_sem=ssem, recv_sem=rsem, device_id=(x, y, 1 - c), device_id_type=MESH)
        cp.start()
        cp.wait()

    return pl.pallas_call(
        body, name=name, out_shape=jax.ShapeDtypeStruct(shape, src.dtype), in_specs=[ANY], out_specs=ANY,
        scratch_shapes=[pltpu.SemaphoreType.DMA(()), pltpu.SemaphoreType.DMA(())],
    )(src)


def exchange_chips(name, src, per_chip):
    shape = src.shape[1:] if per_chip else src.shape

    def body(src_ref, out_ref, send_sems, recv_sems):
        x, y, c, chips = _place()
        cps = []
        for k, chip in enumerate(chips):
            s = src_ref.at[2 * chip[0] + chip[1]] if per_chip else src_ref
            cps.append(pltpu.make_async_remote_copy(src_ref=s, dst_ref=out_ref.at[k], send_sem=send_sems.at[k], recv_sem=recv_sems.at[k],
                                                    device_id=(chip[0], chip[1], c), device_id_type=MESH))
        for cp in cps:
            cp.start()
        for cp in cps:
            cp.wait()

    return pl.pallas_call(
        body, name=name, out_shape=jax.ShapeDtypeStruct((3,) + shape, src.dtype), in_specs=[ANY], out_specs=ANY,
        scratch_shapes=[pltpu.SemaphoreType.DMA((3,)), pltpu.SemaphoreType.DMA((3,))],
    )(src)


def pair_sum(name, p, got):
    _, _, rh, cw = p.shape
    tr = _pick(rh, max(16, min(512, ROW_BUDGET // (2 * cw * 10))), SUBLANES_BF16)

    def body(p_ref, got_ref, s_ref, own_ref):
        j = pl.program_id(1)
        tot = p_ref[...].astype(F32) + got_ref[...].astype(F32)
        s_ref[...] = tot.astype(BF16)

        @pl.when(j == 2 * lax.axis_index("x") + lax.axis_index("y"))
        def _():
            own_ref[...] = tot

    return pl.pallas_call(
        body, name=name, grid=(rh // tr, N_CHIPS),
        in_specs=[pl.BlockSpec((None, None, tr, cw), lambda i, j: (lax.axis_index("c"), j, i, 0)),
                  pl.BlockSpec((None, tr, cw), lambda i, j: (j, i, 0))],
        out_specs=[pl.BlockSpec((None, tr, cw), lambda i, j: (j, i, 0)),
                   pl.BlockSpec((tr, cw), lambda i, j: (i, 0))],
        out_shape=[jax.ShapeDtypeStruct((N_CHIPS, rh, cw), BF16), jax.ShapeDtypeStruct((rh, cw), F32)],
        compiler_params=_cparams(("arbitrary", "arbitrary")),
    )(p, got)


def chips_sum(name, own, parts):
    rh, cw = own.shape
    parts = parts.reshape(3 * rh, cw)
    return rowwise(name, lambda o, a, b, c: (((o + a.astype(F32)) + b.astype(F32)) + c.astype(F32),),
                   [(own, cw, 0), (parts, cw, 0, 0), (parts, cw, 0, rh), (parts, cw, 0, 2 * rh)], [], [(cw, F32)], M=rh)[0]


def all_reduce_small(buf):
    r = buf.shape[0]
    got = swap_with_sibling("ar_pair", buf, False)
    chip = rowwise("ar_pairsum", lambda a, b: (a + b,), [(buf, LANES, 0), (got, LANES, 0)], [], [(LANES, F32)], M=r)[0]
    parts = exchange_chips("ar_chips", chip, False).reshape(3 * r, LANES)
    return rowwise("ar_sum", lambda o, fx, fy, fxy: ((o + fy) + (fx + fxy),),
                   [(chip, LANES, 0), (parts, LANES, 0, 0), (parts, LANES, 0, r), (parts, LANES, 0, 2 * r)], [], [(LANES, F32)], M=r)[0]


def _adam_fn(w, g, m, v):
    m = ADAM_B1 * m + (1.0 - ADAM_B1) * g
    v = ADAM_B2 * v + (1.0 - ADAM_B2) * (g * g)
    m_hat = m / (1.0 - ADAM_B1 ** ADAM_STEP)
    v_hat = v / (1.0 - ADAM_B2 ** ADAM_STEP)
    return -ADAM_LR * (m_hat / (jnp.sqrt(v_hat) + ADAM_EPS) + ADAM_WD * w), m, v


def adamw(name, w, g, m, v):
    r, cw = w.shape
    return rowwise(name, _adam_fn, [(t, cw, 0) for t in (w, g, m, v)], [], [(cw, F32)] * 3, M=r)


def adamw_layers(name, w, m, v, mines, theirs, kind):
    L, K, nn = w.shape
    hr, hc = (K // 2, nn) if kind == "cols" else (K, nn // 2)
    tr = _pick(hr, max(8, min(256, ROW_BUDGET // (2 * hc * 4 * (7 + 2 * L)))), 8)
    nb = hr // tr

    def body(*refs):
        w_ref, m_ref, v_ref = refs[:3]
        outs = refs[3 + 2 * L:]
        l, mine_here = pl.program_id(0), pl.program_id(1) == lax.axis_index("c")
        g = jnp.zeros((tr, hc), F32)
        for ll in range(L):
            g = jnp.where(l == ll, jnp.where(mine_here, refs[3 + ll][...], refs[3 + L + ll][...]), g)
        outs[0][...] = g
        outs[1][...], outs[2][...], outs[3][...] = _adam_fn(w_ref[...], g, m_ref[...], v_ref[...])

    if kind == "cols":
        full = pl.BlockSpec((None, tr, hc), lambda l, h, i: (l, h * nb + i, 0))
    else:
        full = pl.BlockSpec((None, tr, hc), lambda l, h, i: (l, i, h))
    def half_spec(ll, mine):
        def imap(l, h, i):
            here = (l == ll) & ((h == lax.axis_index("c")) == mine)
            return (jnp.where(here, i, 0), 0)
        return pl.BlockSpec((tr, hc), imap)

    halves = [half_spec(ll, True) for ll in range(L)] + [half_spec(ll, False) for ll in range(L)]
    return pl.pallas_call(
        body, name=name, grid=(L, 2, nb), in_specs=[full] * 3 + halves, out_specs=[full] * 4,
        out_shape=[jax.ShapeDtypeStruct((L, K, nn), F32)] * 4, compiler_params=_cparams(("parallel", "parallel", "parallel")),
    )(w, m, v, *mines, *theirs)


class Dims:
    def __init__(self, x, g_q, g_kv, g_out_mla, g_out_ssm, g_out_dil, ff):
        self.M, self.D = x.shape[-2], x.shape[-1]
        self.QL, self.KVL = g_q.shape[-1], g_kv.shape[-1]
        self.MW, self.SW, self.DW = g_out_mla.shape[-1], g_out_ssm.shape[-1], g_out_dil.shape[-1]
        self.H = self.MW // MLA_V
        self.FF = ff
        self.G = self.SW // SSM_GROUP
        self.C = self.G * SSM_STATE
        self.o_cq, self.o_u = 0, self.QL
        self.o_qd = self.o_u + self.SW
        self.o_kd = self.o_qd + self.DW
        self.o_vd = self.o_kd + self.DW
        self.o_ckv = self.o_vd + self.DW
        self.o_kr = self.o_ckv + self.KVL
        self.PW = -(-(self.o_kr + MLA_ROPE) // (4 * LANES)) * (4 * LANES)
        assert self.o_u % self.SW == 0 and self.o_qd % LANES == 0 and self.o_ckv % self.KVL == 0 and self.o_kr % LANES == 0
        assert self.H % 2 == 0 and self.DW % LANES == 0 and self.C % LANES == 0
        self.QW = self.H * (MLA_NOPE + MLA_ROPE)
        self.KVW = self.H * (MLA_NOPE + MLA_V)
        sizes = [self.QL, self.KVL, MLA_ROPE, self.SW, self.DW, self.DW, self.DW]
        starts = np.concatenate([[0], np.cumsum(sizes)[:-1]])
        self.ref_cols = {n: (int(s), int(z)) for n, s, z in zip(["cq", "ckv", "kr", "u", "qd", "kd", "vd"], starts, sizes)}
        self.INW = int(sum(sizes))
        self.new_order = ["cq", "u", "qd", "kd", "vd", "ckv", "kr"]
        src = np.concatenate([np.arange(self.ref_cols[n][0], self.ref_cols[n][0] + self.ref_cols[n][1]) for n in self.new_order])
        self.src_in = np.concatenate([src, -np.ones(self.PW - self.INW, np.int64)])
        self.src_q = self._heads_split(self.H, MLA_NOPE, MLA_ROPE)
        self.src_kv = self._heads_split(self.H, MLA_NOPE, MLA_V)

    @staticmethod
    def _heads_split(h, d1, d2):
        first = (np.arange(h)[:, None] * (d1 + d2) + np.arange(d1)[None, :]).reshape(-1)
        second = (np.arange(h)[:, None] * (d1 + d2) + d1 + np.arange(d2)[None, :]).reshape(-1)
        return np.concatenate([first, second])


def _regroup_in(dm, w):
    parts = [w[..., dm.ref_cols[n][0]:dm.ref_cols[n][0] + dm.ref_cols[n][1]] for n in dm.new_order]
    pad = dm.PW - dm.INW
    return jnp.concatenate(parts + [jnp.zeros(w.shape[:-1] + (pad,), w.dtype)], axis=-1)


def _ungroup_in(dm, w):
    off, pieces = 0, {}
    for n in dm.new_order:
        pieces[n] = w[..., off:off + dm.ref_cols[n][1]]
        off += dm.ref_cols[n][1]
    return jnp.concatenate([pieces[n] for n in ["cq", "ckv", "kr", "u", "qd", "kd", "vd"]], axis=-1)


def _split_heads(w, h, d1):
    t = w.reshape(w.shape[:-1] + (h, -1))
    return jnp.concatenate([t[..., :d1].reshape(w.shape[:-1] + (-1,)), t[..., d1:].reshape(w.shape[:-1] + (-1,))], axis=-1)


def _merge_heads(w, h, d1):
    a = w[..., :h * d1].reshape(w.shape[:-1] + (h, d1))
    b = w[..., h * d1:].reshape(w.shape[:-1] + (h, -1))
    return jnp.concatenate([a, b], axis=-1).reshape(w.shape[:-1] + (-1,))


def _cat_cols(re, im):
    r, c = re.shape
    return jnp.stack([re.reshape(r, c // LANES, LANES), im.reshape(r, c // LANES, LANES)], axis=2).reshape(r, 2 * c)


def _uncat_cols(cat):
    r, c2 = cat.shape
    t = cat.reshape(r, c2 // (2 * LANES), 2, LANES)
    return t[:, :, 0].reshape(r, c2 // 2), t[:, :, 1].reshape(r, c2 // 2)


def _block_diag(t, g):
    _, a, b = t.shape
    eye = jnp.eye(g, dtype=bool)[:, None, :, None]
    return jnp.where(eye, t[:, :, None, :], 0).reshape(g * a, g * b)


def _diag_blocks(m, g):
    a, b = m.shape[0] // g, m.shape[1] // g
    eye = jnp.eye(g, dtype=m.dtype)[:, None, :, None]
    return jnp.sum(m.reshape(g, a, g, b) * eye, axis=2)


def _rope_tables(dm):
    half = MLA_ROPE // 2
    inv_freq = ROPE_THETA ** (-jnp.arange(half, dtype=F32) / half)
    ang = jnp.arange(dm.M, dtype=F32)[:, None] * inv_freq[None, :]
    cos = jnp.concatenate([jnp.cos(ang), jnp.cos(ang)], axis=1)
    sin = jnp.concatenate([-jnp.sin(ang), jnp.sin(ang)], axis=1)
    return jnp.tile(cos, (1, dm.H)), jnp.tile(sin, (1, dm.H)), jnp.tile(cos, (1, LANES // MLA_ROPE)), jnp.tile(sin, (1, LANES // MLA_ROPE))


def _rope(x, cos, sin):
    return x * cos + _swap_halves(x, MLA_ROPE // 2) * sin


def _rope_t(d, cos, sin):
    return d * cos + _swap_halves(d * sin, MLA_ROPE // 2)


def _ssm_layer_params(dm, a_re, a_im, log_dt, b_re, b_im):
    flat = lambda t: t.reshape(1, dm.C)
    ldt = jnp.repeat(log_dt, SSM_STATE).reshape(1, dm.C)
    bt = lambda t: jnp.transpose(t, (2, 0, 1)).reshape(SSM_GROUP, dm.C)
    return flat(a_re), flat(a_im), ldt, bt(b_re), bt(b_im)


def layer_forward(dm, l, x, lw, sp, tabs, hook):
    M, D = dm.M, dm.D
    n = lambda s: f"{s}_l{l}"
    sv = {"x_in": x}
    h1 = rms_fwd(n("rms_mix"), x, D, 0, lw["g_mix"], M=M)
    proj = matmul(n("in_proj"), h1, lw["w_in"], "nn", M=M, N=dm.PW, K=D)
    sv.update(h1=h1, proj=proj)
    cqn = rms_fwd(n("rms_q"), proj, dm.QL, dm.o_cq, lw["g_q"], M=M)
    q = matmul(n("q_up"), cqn, lw["w_uq"], "nn", M=M, N=dm.QW, K=dm.QL)
    ckvn = rms_fwd(n("rms_kv"), proj, dm.KVL, dm.o_ckv, lw["g_kv"], M=M)
    kv = matmul(n("kv_up"), ckvn, lw["w_ukv"], "nn", M=M, N=dm.KVW, K=dm.KVL, out_dtype=BF16)
    cosq, sinq, cosk, sink = tabs[:4]
    nw = dm.H * MLA_NOPE

    def rope_fn(qb, kb, cq, sq, ck, sk):
        return jnp.concatenate([qb[:, :nw], _rope(qb[:, nw:], cq, sq)], axis=1), _rope(kb, ck, sk)

    pw = dm.H * MLA_ROPE
    q_bf, kpe = rowwise(n("rope"), rope_fn, [(q, dm.QW, 0), (proj, LANES, dm.o_kr), (cosq, pw, 0), (sinq, pw, 0), (cosk, LANES, 0), (sink, LANES, 0)],
                        [], [(dm.QW, BF16), (LANES, BF16)], M=M)
    mla_scale = (MLA_NOPE + MLA_ROPE) ** -0.5
    o_mla, lse_mla = attention_fwd(n("mla_fwd"), q_bf, 0, kv, 0, kv, nw, da=MLA_NOPE, dv=MLA_V, pairs=dm.H // 2, scale=mla_scale,
                                   M=M, qb=q_bf, qb_off=nw, kb=kpe)
    sv.update(cqn=cqn, ckvn=ckvn, q_bf=q_bf, kv=kv, kpe=kpe, o_mla=o_mla, lse_mla=lse_mla)
    bu = matmul(n("ssm_bu"), proj, sp["bcat"], "nn", M=M, N=2 * dm.C, K=dm.SW, a_off=(0, dm.o_u))
    hcat = ssm_scan(n("ssm_scan"), bu, sp["acat"], M=M)
    ylin = matmul(n("ssm_y"), hcat, sp["ccat"], "nn", M=M, N=dm.SW, K=2 * dm.C)
    yg = rowwise(n("ssm_gelu"), lambda y, u, d: (_gelu(y + d * u),), [(ylin, dm.SW, 0), (proj, dm.SW, dm.o_u)], [lw["d_skip"]],
                 [(dm.SW, BF16)], M=M)[0]
    z = matmul(n("ssm_glu"), yg, lw["w_glu"], "nn", w=("cols", 2 * dm.SW // N_CHIPS), M=M, N=2 * dm.SW, K=dm.SW)
    sw = dm.SW

    def glu_fn(zb, b):
        zz = zb + b
        return (zz[:, :sw] * jax.nn.sigmoid(zz[:, sw:]),)

    o_ssm = rowwise(n("ssm_gate"), glu_fn, [(z, 2 * sw, 0)], [lw["b_glu"]], [(sw, F32)], M=M)[0]
    sv.update(hcat=hcat, ylin=ylin, yg=yg, z=z, o_ssm=o_ssm)
    o_dil, lse_dil = attention_fwd(n("dil_fwd"), proj, dm.o_qd, proj, dm.o_kd, proj, dm.o_vd, da=DIL_HEAD, dv=DIL_HEAD, pairs=dm.DW // LANES,
                                   scale=DIL_HEAD ** -0.5, M=M, bias=tabs[4])
    sv.update(o_dil=o_dil, lse_dil=lse_dil)
    yn = rowwise(n("out_norm"), lambda a, b, c, ga, gb, gc: (jnp.concatenate([_rms(a, ga), _rms(b, gb), _rms(c, gc)], axis=1),),
                 [(o_mla, dm.MW, 0), (o_ssm, dm.SW, 0), (o_dil, dm.DW, 0)], [lw["g_out_mla"], lw["g_out_ssm"], lw["g_out_dil"]],
                 [(D, BF16)], M=M)[0]
    yn = hook(yn)
    x_mid = matmul(n("out_proj"), yn, lw["w_o"], "nn", w=("rows", D // N_CHIPS), M=M, N=D, K=D, add=x)
    h2 = rms_fwd(n("rms_ffn"), x_mid, D, 0, lw["g_ffn"], M=M)
    ffs = dm.FF // N_CHIPS
    gate = matmul(n("ffn_gate"), h2, lw["w_gate"], "nn", w=("cols",ffs), M=M, N=dm.FF, K=D)
    up = matmul(n("ffn_up"), h2, lw["w_up"], "nn", w=("cols",ffs), M=M, N=dm.FF, K=D)
    act = rowwise(n("ffn_act"), lambda g, u: (g * jax.nn.sigmoid(g) * u,), [(gate, dm.FF, 0), (up, dm.FF, 0)], [], [(dm.FF, BF16)], M=M)[0]
    x_out = matmul(n("ffn_down"), act, lw["w_down"], "nn", w=("rows",ffs), M=M, N=D, K=dm.FF, add=x_mid)
    sv.update(yn=yn, x_mid=x_mid, h2=h2, gate=gate, up=up, act=act)
    return x_out, sv


def layer_backward(dm, l, dx, lw, sp, tabs, sv, hook_a, hook_b):
    M, D = dm.M, dm.D
    n = lambda s: f"{s}_l{l}"
    g = {}
    ffs = dm.FF // N_CHIPS
    dact = matmul(n("ffn_down_dx"), dx, lw["w_down"], "nt", w=("rows", ffs), M=M, N=dm.FF, K=D)
    g["w_down"] = matmul(n("ffn_down_dw"), sv["act"], dx, "tn", M=dm.FF, N=D, K=M, out_dtype=BF16, into=("rows", ffs))

    def act_bwd(gb, ub, db):
        _, vjp = jax.vjp(lambda a, b: a * jax.nn.sigmoid(a) * b, gb, ub)
        return vjp(db)

    dgate, dup = rowwise(n("ffn_act_bwd"), act_bwd, [(sv["gate"], dm.FF, 0), (sv["up"], dm.FF, 0), (dact, dm.FF, 0)], [],
                         [(dm.FF, BF16), (dm.FF, BF16)], M=M)
    dh2 = matmul(n("ffn_gate_dx"), dgate, lw["w_gate"], "nt", w=("cols",ffs), M=M, N=D, K=dm.FF)
    dh2 = matmul(n("ffn_up_dx"), dup, lw["w_up"], "nt", w=("cols",ffs), M=M, N=D, K=dm.FF, add=dh2)
    g["w_gate"] = matmul(n("ffn_gate_dw"), sv["h2"], dgate, "tn", M=D, N=dm.FF, K=M, out_dtype=BF16, into=("cols", ffs))
    g["w_up"] = matmul(n("ffn_up_dw"), sv["h2"], dup, "tn", M=D, N=dm.FF, K=M, out_dtype=BF16, into=("cols", ffs))
    dx_mid, g["g_ffn"] = rms_bwd(n("rms_ffn_bwd"), sv["x_mid"], D, 0, lw["g_ffn"], dh2, dx, M=M)
    dx_mid = hook_a(dx_mid)
    dyn = matmul(n("out_proj_dx"), dx_mid, lw["w_o"], "nt", w=("rows", D // N_CHIPS), M=M, N=D, K=D)
    g["w_o"] = matmul(n("out_proj_dw"), sv["yn"], dx_mid, "tn", M=D, N=D, K=M, out_dtype=BF16, into=("rows", D // N_CHIPS))
    mw, sw, dw = dm.MW, dm.SW, dm.DW

    def out_norm_bwd(a, b, c, dy, ga, gb, gc):
        res, sums = [], []
        for t, gg, lo, hi in ((a, ga, 0, mw), (b, gb, mw, mw + sw), (c, gc, mw + sw, mw + sw + dw)):
            _, vjp = jax.vjp(_rms, t, gg)
            dt, dg = vjp(dy[:, lo:hi])
            res.append(dt)
            sums.append(dg)
        return res + sums

    do_mla, do_ssm, do_dil, g["g_out_mla"], g["g_out_ssm"], g["g_out_dil"] = rowwise(
        n("out_norm_bwd"), out_norm_bwd, [(sv["o_mla"], mw, 0), (sv["o_ssm"], sw, 0), (sv["o_dil"], dw, 0), (dyn, D, 0)],
        [lw["g_out_mla"], lw["g_out_ssm"], lw["g_out_dil"]], [(mw, F32), (sw, F32), (dw, F32)], [mw, sw, dw], M=M)
    proj = sv["proj"]
    dqd, dkd, dvd = attention_bwd(n("dil_bwd"), proj, dm.o_qd, proj, dm.o_kd, proj, dm.o_vd, sv["o_dil"], do_dil, sv["lse_dil"],
                                  da=DIL_HEAD, dv=DIL_HEAD, pairs=dw // LANES, scale=DIL_HEAD ** -0.5, M=M, bias=tabs[4])
    def glu_bwd(zb, db, b):
        _, vjp = jax.vjp(lambda zz, bb: (zz + bb)[:, :sw] * jax.nn.sigmoid((zz + bb)[:, sw:]), zb, b)
        return vjp(db)

    dz, g["b_glu"] = rowwise(n("ssm_gate_bwd"), glu_bwd, [(sv["z"], 2 * sw, 0), (do_ssm, sw, 0)], [lw["b_glu"]], [(2 * sw, BF16)], [2 * sw], M=M)
    dyg = matmul(n("ssm_glu_dx"), dz, lw["w_glu"], "nt", w=("cols", 2 * sw // N_CHIPS), M=M, N=sw, K=2 * sw)
    g["w_glu"] = matmul(n("ssm_glu_dw"), sv["yg"], dz, "tn", M=sw, N=2 * sw, K=M, out_dtype=BF16, into=("cols", 2 * sw // N_CHIPS))

    def gelu_bwd(y, u, dy, d):
        _, vjp = jax.vjp(lambda yy, uu, dd: _gelu(yy + dd * uu), y, u, d)
        return vjp(dy)

    dylin, du1, g["d_skip"] = rowwise(n("ssm_gelu_bwd"), gelu_bwd, [(sv["ylin"], sw, 0), (proj, sw, dm.o_u), (dyg, sw, 0)], [lw["d_skip"]],
                                      [(sw, BF16), (sw, F32)], [sw], M=M)
    seed = matmul(n("ssm_y_dx"), dylin, sp["ccat"], "nt", M=M, N=2 * dm.C, K=sw)
    d_ccat = matmul(n("ssm_y_dw"), sv["hcat"], dylin, "tn", M=2 * dm.C, N=sw, K=M)
    lam, d_acat = ssm_scan(n("ssm_scan_bwd"), seed, sp["acat_conj"], M=M, reverse=True, hcat=sv["hcat"])
    du = matmul(n("ssm_bu_dx"), lam, sp["bcat"], "nt", M=M, N=sw, K=2 * dm.C, add=du1, out_dtype=BF16)
    d_bcat = matmul(n("ssm_bu_dw"), proj, lam, "tn", M=sw, N=2 * dm.C, K=M, a_off=(0, dm.o_u))
    g["ssm_raw"] = (d_acat, d_bcat, d_ccat)
    nw = dm.H * MLA_NOPE
    dqn, dkn, dv_, dqp, dkp = attention_bwd(n("mla_bwd"), sv["q_bf"], 0, sv["kv"], 0, sv["kv"], nw, sv["o_mla"], do_mla, sv["lse_mla"],
                                            da=MLA_NOPE, dv=MLA_V, pairs=dm.H // 2, scale=(MLA_NOPE + MLA_ROPE) ** -0.5, M=M,
                                            qb=sv["q_bf"], qb_off=nw, kb=sv["kpe"])
    cosq, sinq, cosk, sink = tabs[:4]
    pw = dm.H * MLA_ROPE
    dqp_u, dkr = rowwise(n("rope_bwd"), lambda a, b, cq, sq, ck, sk: (_rope_t(a, cq, sq), _rope_t(b, ck, sk)),
                         [(dqp, pw, 0), (dkp, LANES, 0), (cosq, pw, 0), (sinq, pw, 0), (cosk, LANES, 0), (sink, LANES, 0)], [],
                         [(pw, BF16), (LANES, BF16)], M=M)
    dq = lane_concat(n("dq_cat"), [dqn, dqp_u], M=M)
    dkv = lane_concat(n("dkv_cat"), [dkn, dv_], M=M)
    dcqn = matmul(n("q_up_dx"), dq, lw["w_uq"], "nt", M=M, N=dm.QL, K=dm.QW)
    g["w_uq"] = matmul(n("q_up_dw"), sv["cqn"], dq, "tn", M=dm.QL, N=dm.QW, K=M, out_dtype=BF16)
    dckvn = matmul(n("kv_up_dx"), dkv, lw["w_ukv"], "nt", M=M, N=dm.KVL, K=dm.KVW)
    g["w_ukv"] = matmul(n("kv_up_dw"), sv["ckvn"], dkv, "tn", M=dm.KVL, N=dm.KVW, K=M, out_dtype=BF16)
    dcq, g["g_q"] = rms_bwd(n("rms_q_bwd"), proj, dm.QL, dm.o_cq, lw["g_q"], dcqn, M=M, out_dtype=BF16)
    dckv, g["g_kv"] = rms_bwd(n("rms_kv_bwd"), proj, dm.KVL, dm.o_ckv, lw["g_kv"], dckvn, M=M, out_dtype=BF16)
    dproj = hook_b(lane_concat(n("dproj_cat"), [dcq, du, dqd, dkd, dvd, dckv, dkr], M=M, pad_to=dm.PW))
    dh1 = matmul(n("in_proj_dx"), dproj, lw["w_in"], "nt", M=M, N=D, K=dm.PW)
    g["w_in"] = matmul(n("in_proj_dw"), sv["h1"], dproj, "tn", M=D, N=dm.PW, K=M, out_dtype=BF16)
    dx_in, g["g_mix"] = rms_bwd(n("rms_mix_bwd"), sv["x_in"], D, 0, lw["g_mix"], dh1, dx_mid, M=M)
    return dx_in, g


def layer_params(dm, small, l):
    lw = {k: small[k][l].reshape(1, -1) for k in ("g_mix", "g_q", "g_kv", "b_glu", "g_out_mla", "g_out_ssm", "g_out_dil", "g_ffn", "d_skip")}
    raw = _ssm_layer_params(dm, small["a_re"][l], small["a_im"][l], small["log_dt"][l], small["b_re"][l], small["b_im"][l])
    ar, ai, bbr, bbi = ssm_param_fwd(f"ssm_param_l{l}", *raw)
    g_ = dm.G
    bd = lambda t: _block_diag(jnp.transpose(t.reshape(SSM_GROUP, g_, SSM_STATE), (1, 0, 2)), g_)
    cd = lambda t: _block_diag(jnp.transpose(t, (0, 2, 1)), g_)
    cre, cim = cd(small["c_re"][l]), cd(small["c_im"][l])
    sp = {"acat": _cat_cols(ar, ai), "acat_conj": _cat_cols(ar, -ai),
          "bcat": _cat_cols(bd(bbr), bd(bbi)).astype(BF16),
          "ccat": _cat_cols(cre.T, -cim.T).T.astype(BF16)}
    return lw, sp, raw


def ssm_param_grads(dm, l, g, raw):
    d_acat, d_bcat, d_ccat = g.pop("ssm_raw")
    d_ar, d_ai = _uncat_cols(d_acat)
    dbr, dbi = _uncat_cols(d_bcat)
    g_ = dm.G
    to_rows = lambda t: jnp.transpose(_diag_blocks(t, g_), (1, 0, 2)).reshape(SSM_GROUP, dm.C)
    da_re, da_im, dldt, db_re, db_im = ssm_param_bwd(f"ssm_param_bwd_l{l}", *raw, d_ar, d_ai, to_rows(dbr), to_rows(dbi))
    dcr, dci = _uncat_cols(d_ccat.T)
    g["a_re"], g["a_im"] = da_re.reshape(g_, SSM_STATE), da_im.reshape(g_, SSM_STATE)
    g["log_dt"] = jnp.sum(dldt.reshape(g_, SSM_STATE), axis=1)
    from_rows = lambda t: jnp.transpose(t.reshape(SSM_GROUP, g_, SSM_STATE), (1, 2, 0))
    g["b_re"], g["b_im"] = from_rows(db_re), from_rows(db_im)
    g["c_re"] = jnp.transpose(_diag_blocks(dcr.T, g_), (0, 2, 1))
    g["c_im"] = -jnp.transpose(_diag_blocks(dci.T, g_), (0, 2, 1))
    g["d_skip"] = g["d_skip"].reshape(g_, SSM_GROUP)


def loss_and_grad(dm, h, target, g_final):
    D = dm.D

    def loss_fn(xb, tb, gb):
        y, vjp = jax.vjp(_rms, xb, gb)
        err = y - tb
        dxb, dg = vjp(err * (1.0 / D))
        part = 0.5 * jnp.sum(jnp.mean(err * err, axis=-1, keepdims=True), axis=0, keepdims=True)
        lane = lax.broadcasted_iota(jnp.int32, (1, LANES), 1)
        return dxb, dg, jnp.where(lane == 0, part, 0.0)

    return rowwise("loss", loss_fn, [(h, D, 0), (target, D, 0)], [g_final.reshape(1, D)], [(D, F32)], [D, LANES], M=dm.M)


KIND = {"w_in": "cols", "w_uq": "cols", "w_ukv": "cols", "w_glu": "cols", "w_o": "rows", "w_gate": "cols", "w_up": "cols", "w_down": "rows"}
SHARDED = list(KIND)
SMALL = ["g_mix", "g_q", "g_kv", "a_re", "a_im", "b_re", "b_im", "c_re", "c_im", "d_skip", "log_dt", "b_glu",
         "g_out_mla", "g_out_ssm", "g_out_dil", "g_ffn", "g_final"]
ORDER = ["g_mix", "w_in", "g_q", "w_uq", "g_kv", "w_ukv", "a_re", "a_im", "b_re", "b_im", "c_re", "c_im", "d_skip", "log_dt",
         "w_glu", "b_glu", "g_out_mla", "g_out_ssm", "g_out_dil", "w_o", "g_ffn", "w_gate", "w_up", "w_down", "g_final"]


def kernel(x, g_mix, w_in, g_q, w_uq, g_kv, w_ukv, a_re, a_im, b_re, b_im, c_re, c_im, d_skip, log_dt, w_glu, b_glu, g_out_mla, g_out_ssm, g_out_dil, w_o, g_ffn, w_gate, w_up, w_down, g_final, loss_target, m_g_mix, m_w_in, m_g_q, m_w_uq, m_g_kv, m_w_ukv, m_a_re, m_a_im, m_b_re, m_b_im, m_c_re, m_c_im, m_d_skip, m_log_dt, m_w_glu, m_b_glu, m_g_out_mla, m_g_out_ssm, m_g_out_dil, m_w_o, m_g_ffn, m_w_gate, m_w_up, m_w_down, m_g_final, v_g_mix, v_w_in, v_g_q, v_w_uq, v_g_kv, v_w_ukv, v_a_re, v_a_im, v_b_re, v_b_im, v_c_re, v_c_im, v_d_skip, v_log_dt, v_w_glu, v_b_glu, v_g_out_mla, v_g_out_ssm, v_g_out_dil, v_w_o, v_g_ffn, v_w_gate, v_w_up, v_w_down, v_g_final):
    args = locals()
    w = {k: args[k] for k in ORDER}
    mom = {k: args["m_" + k] for k in ORDER}
    var = {k: args["v_" + k] for k in ORDER}
    dm = Dims(x, g_q, g_kv, g_out_mla, g_out_ssm, g_out_dil, w_gate.shape[-1] * N_CHIPS)
    L = g_mix.shape[0]

    small = {k: w[k] for k in SMALL}
    na = len(SHARDED)
    tabs = _rope_tables(dm) + (dilated_bias(dm.M),)
    sel ={"w_in": selection_matrices(dm.src_in, w_in.shape[-1]), "w_uq": selection_matrices(dm.src_q, w_uq.shape[-1]),
           "w_ukv": selection_matrices(dm.src_kv, w_ukv.shape[-1])}

    G = [[cast_into_slot(f"cast_{k}_l{l}", w[k], KIND[k], l) for k in SHARDED] for l in range(L)]
    sems_ici, sems_pass = {}, {}

    def ici_start(l, car):
        sems_ici[l], G[l], _, car = split_start(f"ag_ici_start_l{l}", plan_gather_ici, 3 * na, G[l], [], car)
        return car

    def ici_wait(l, car):
        G[l], car = split_wait(f"ag_ici_wait_l{l}", plan_gather_ici, sems_ici[l], G[l], car)
        return car

    def pass_start(l, car):
        sems_pass[l], G[l], _, car = split_start(f"ag_pass_start_l{l}", plan_gather_pass, 3 * na, G[l], [], car)
        return car

    def pass_wait(l, car):
        G[l], car = split_wait(f"ag_pass_wait_l{l}", plan_gather_pass, sems_pass[l], G[l], car)
        return car

    def layer_weights(l):
        lw = dict(zip(SHARDED, G[l]))
        for k in sel:
            lw[k] = regroup_cols(f"regroup_{k}_l{l}", lw[k], sel[k])
        return lw

    car = tabs[0]
    for l in range(min(2, L)):
        car = ici_start(l, car)
    tabs = (pass_wait(0, pass_start(0, ici_wait(0, car))),) + tabs[1:]
    h = x.reshape(dm.M, dm.D)
    lws, sps, raws, saved = [], [], [], []
    for l in range(L):
        lw, sp, raw = layer_params(dm, small, l)
        lw.update(layer_weights(l))

        def mid(car, l=l):
            if l + 1 < L:
                car = pass_start(l + 1, ici_wait(l + 1, car))
            if l + 2 < L:
                car = ici_start(l + 2, car)
            return car

        h, sv = layer_forward(dm, l, h, lw, sp, tabs, mid)
        if l + 1 < L:
            h = pass_wait(l + 1, h)
        lws.append(lw)
        sps.append(sp)
        raws.append(raw)
        saved.append(sv)
    dx, g_final_part, loss_part = loss_and_grad(dm, h, loss_target.reshape(dm.M, dm.D), w["g_final"])

    def reduce_begin(l, parts, car):
        fresh = [jax.ShapeDtypeStruct(p.shape[1:], BF16) for p in parts]
        sems, parts, gots, car = split_start(f"rs_pair_start_l{l}", plan_pair(na), na, parts, fresh, car)
        return {"l": l, "sems": sems, "parts": parts, "gots": gots}, car

    def reduce_chips(st, car):
        l = st["l"]
        bufs, car = split_wait(f"rs_pair_wait_l{l}", plan_pair(na), st["sems"], st["parts"] + st["gots"], car)
        sums = [pair_sum(f"rs_pairsum_{a}_l{l}", bufs[i], bufs[na + i]) for i, a in enumerate(SHARDED)]
        fresh = [jax.ShapeDtypeStruct((3,) + s.shape[1:], BF16) for s, _ in sums]
        st["sems"], st["s"], st["arrived"], car = split_start(f"rs_chips_start_l{l}", plan_chips(na), 3 * na, [s for s, _ in sums], fresh, car)
        st["own"] = [o for _, o in sums]
        return car

    def reduce_share(st, car):
        l = st["l"]
        bufs, car = split_wait(f"rs_chips_wait_l{l}", plan_chips(na), st["sems"], st["s"] + st["arrived"], car)
        mine = [chips_sum(f"rs_sum_{a}_l{l}", st["own"][i], bufs[na + i]) for i, a in enumerate(SHARDED)]
        fresh = [jax.ShapeDtypeStruct(m_.shape, F32) for m_ in mine]
        st["sems"], st["mine"], st["theirs"], car = split_start(f"rs_share_start_l{l}", plan_share(na), na, mine, fresh, car)
        return car

    def reduce_end(st, car):
        bufs, car = split_wait(f"rs_share_wait_l{st['l']}", plan_share(na), st["sems"], st["mine"] + st["theirs"], car)
        reduced[st["l"]] = (bufs[:na], bufs[na:])
        return car

    reduced, grads, flying = [None] * L, [None] * L, None
    keep = lambda car: car
    for l in reversed(range(L)):
        st = flying
        dx, g = layer_backward(dm, l, dx, lws[l], sps[l], tabs, saved[l],
                               keep if st is None else functools.partial(reduce_chips, st),
                               keep if st is None else functools.partial(reduce_share, st))
        ssm_param_grads(dm, l, g, raws[l])
        if st is not None:
            dx = reduce_end(st, dx)
        for k in sel:
            g[k] = ungroup_cols(f"ungroup_{k}_l{l}", g[k], sel[k])
        flying, dx = reduce_begin(l, [g.pop(k) for k in SHARDED], dx)
        grads[l] = g
    dx = reduce_end(flying, reduce_share(flying, reduce_chips(flying, dx)))

    gsum = {}
    small_names = [k for k in SMALL if k != "g_final"]
    pieces = [jnp.stack([grads[l][k] for l in range(L)]).reshape(-1) for k in small_names] + [g_final_part.reshape(-1), loss_part.reshape(-1)]
    sizes = [int(p.shape[0]) for p in pieces]
    total = sum(sizes)
    rows = -(-total // (LANES * 16)) * 16
    pack = lambda ps: jnp.concatenate(ps + [jnp.zeros((rows * LANES - total,), F32)]).reshape(rows, LANES)
    red = all_reduce_small(pack(pieces))
    flat = red.reshape(-1)
    offs = np.concatenate([[0], np.cumsum(sizes)]).astype(int)
    names = small_names + ["g_final"]
    for i, k in enumerate(names):
        gsum[k] = flat[offs[i]:offs[i + 1]].reshape(w[k].shape)
    loss = flat[offs[len(names)]]

    delta, new_m, new_v = {}, {}, {}
    for i, k in enumerate(SHARDED):
        gsum[k], delta[k], new_m[k], new_v[k] = adamw_layers(f"adam_{k}", w[k], mom[k], var[k], [reduced[l][0][i] for l in range(L)],
                                                             [reduced[l][1][i] for l in range(L)], KIND[k])
    sm_sizes = sizes[:len(names)]
    sm_total = sum(sm_sizes)
    packs = lambda d: jnp.concatenate([d[k].reshape(-1) for k in names] + [jnp.zeros((rows * LANES - sm_total,), F32)]).reshape(rows, LANES)
    gs = jnp.concatenate([flat[:sm_total], jnp.zeros((rows * LANES - sm_total,), F32)]).reshape(rows, LANES)
    d_, m_, v_ = adamw("adam_small", packs(w), gs, packs(mom), packs(var))
    for i, k in enumerate(names):
        sl = slice(offs[i], offs[i + 1])
        delta[k], new_m[k], new_v[k] = (t.reshape(-1)[sl].reshape(w[k].shape) for t in (d_, m_, v_))

    return (loss, dx.reshape(x.shape), *[gsum[k] for k in ORDER], *[delta[k] for k in ORDER],
            *[new_m[k] for k in ORDER], *[new_v[k] for k in ORDER])
```

```python
import functools
import math

import numpy as np
import jax
import jax.numpy as jnp
from jax import lax
from jax.experimental import pallas as pl
from jax.experimental.pallas import tpu as pltpu

F32 = jnp.float32
BF16 = jnp.bfloat16
MESH = pl.DeviceIdType.MESH

NORM_EPS = 1e-6
MLA_NOPE, MLA_ROPE, MLA_V = 128, 64, 128
SSM_GROUP, SSM_STATE = 16, 64
DIL_HEAD = 64
DIL_PATTERNS = ((128, 1), (512, 4), (2048, 16))
ROPE_THETA = 10000.0
ADAM_LR, ADAM_B1, ADAM_B2, ADAM_EPS, ADAM_WD, ADAM_STEP = 0.001, 0.9, 0.999, 1e-08, 0.01, 10
N_CHIPS = 4

LANES = 128
SUBLANES_BF16 = 16
VMEM_LIMIT = 56 * 1024 * 1024
ROW_BUDGET = 20 * 1024 * 1024
MM_BUDGET = 40 * 1024 * 1024
NEG = -1e30


def _cparams(sem=None):
    return pltpu.CompilerParams(dimension_semantics=sem, vmem_limit_bytes=VMEM_LIMIT)


def _pick(n, cap, q, off=0):
    best = None
    for d in range(q, min(n, cap) + 1, q):
        if n % d == 0 and off % d == 0:
            best = d
    if best is None or (best * 4 <= min(cap, n) and n <= 3072 and off % n == 0):
        assert off % n == 0, (n, off)
        return n
    return best


_DOT_DIMS = {"nn": (((1,), (0,)), ((), ())), "nt": (((1,), (1,)), ((), ())), "tn": (((0,), (0,)), ((), ()))}


def _divs(n, q, within=None, off=0):
    return [d for d in range(q, n + 1, q) if n % d == 0 and off % d == 0 and (within is None or within % d == 0)] or [n]


def _mm_tiles(M, N, K, tms, tns, tks, ab, bb, ob):
    best = None
    for tk in tks:
        nk = K // tk
        for tn in tns:
            for tm in tms:
                if 2 * (tm * tk * ab + tk * tn * bb + tm * tn * ob) + tm * tn * 4 * (2 if nk > 1 else 1) > MM_BUDGET:
                    continue
                steps = (M // tm) * (N // tn) * nk
                hbm = M * K * ab * (1 if nk == 1 else N // tn) + K * N * bb * (M // tm) + M * N * ob
                cost = steps * 0.35e-6 + hbm / 3.0e12 + (nk - 1) * M * N * 12 / 4.0e12
                if best is None or cost < best[0]:
                    best = (cost, tm, tn, tk)
    assert best is not None, (M, N, K)
    return best[1:]


def matmul(name, a, b, mode, *, M, N, K, a_off=(0, 0), b_off=(0, 0), b_lead=None, w=None, add=None, out_dtype=F32, into=None):
    tn_mode = mode == "tn"
    a_ro, a_co = (a_off[1], a_off[0]) if tn_mode else a_off
    b_no, b_ko = b_off if mode == "nt" else (b_off[1], b_off[0])
    n_within = k_within = m_within = None
    if w is not None:
        kind, shard = w
        if kind == "cols":
            b = b.reshape(N_CHIPS, b.shape[1] * b.shape[2], b.shape[3])
        rows_within, cols_within = (K if mode == "nn" else N, shard) if kind == "cols" else (shard, (N if mode == "nn" else K) // 2)
        k_within, n_within = (rows_within, cols_within) if mode == "nn" else (cols_within, rows_within)
    if into is not None:
        m_within, n_within = (M // 2, into[1]) if into[0] == "cols" else (into[1], N // 2)
    tms = [d for d in _divs(M, 128 if tn_mode else SUBLANES_BF16, m_within, a_ro) if d <= 1408]
    tns = [d for d in _divs(N, LANES, n_within, b_no) if d <= 2048]
    tks = _divs(K, SUBLANES_BF16 if tn_mode else LANES, k_within, math.gcd(a_co, b_ko))
    ob = jnp.dtype(out_dtype).itemsize + (add.dtype.itemsize if add is not None else 0)
    tm, tn, tk = _mm_tiles(M, N, K, tms, tns, tks, a.dtype.itemsize, b.dtype.itemsize, ob)
    nk = K // tk
    dn = _DOT_DIMS[mode]

    if tn_mode:
        a_spec = pl.BlockSpec((tk, tm), lambda i, j, k: (k + a_co // tk, i + a_ro // tm))
    else:
        a_spec = pl.BlockSpec((tm, tk), lambda i, j, k: (i + a_ro // tm, k + a_co // tk))
    b_blk = (tn, tk) if mode == "nt" else (tk, tn)
    if w is not None:
        tr_, tc_ = (tk, tn) if mode == "nn" else (tn, tk)
        rper, cper = rows_within // tr_, cols_within // tc_

        def wmap(rb, cb):
            if kind == "cols":
                return (cb // cper, rb, cb % cper)
            return (rb // rper, cb // cper, rb % rper, cb % cper)

        imap = (lambda i, j, k: wmap(k, j)) if mode == "nn" else (lambda i, j, k: wmap(j, k))
        b_spec = pl.BlockSpec((None,) * (b.ndim - 2) + b_blk, imap)
    else:
        if mode == "nt":
            imap = lambda i, j, k: (j + b_no // tn, k + b_ko // tk)
        else:
            imap = lambda i, j, k: (k + b_ko // tk, j + b_no // tn)
        if b_lead is None:
            b_spec = pl.BlockSpec(b_blk, imap)
        else:
            b_spec = pl.BlockSpec((None,) + b_blk, lambda i, j, k: (b_lead,) + imap(i, j, k))
    o_plain = pl.BlockSpec((tm, tn), lambda i, j, k: (i, j))
    if into is None:
        o_spec, out_shape = o_plain, jax.ShapeDtypeStruct((M, N), out_dtype)
    else:
        rper, cper = m_within // tm, n_within // tn
        if into[0] == "cols":
            o_spec = pl.BlockSpec((None, None, tm, tn), lambda i, j, k: (i // rper, j // cper, i % rper, j % cper))
        else:
            o_spec = pl.BlockSpec((None, None, tm, tn), lambda i, j, k: (j // cper, i // rper, i % rper, j % cper))
        out_shape = jax.ShapeDtypeStruct((2, N_CHIPS, m_within, n_within), out_dtype)
    has_add = add is not None
    n_in = 2 + has_add

    def body(*refs):
        a_ref, b_ref = refs[0], refs[1]
        add_ref = refs[2] if has_add else None
        o_ref = refs[n_in]
        part = lax.dot_general(a_ref[...].astype(BF16), b_ref[...].astype(BF16), dn, preferred_element_type=F32)

        def finish(r):
            if has_add:
                r = r + add_ref[...].astype(F32)
            o_ref[...] = r.astype(o_ref.dtype)

        if nk == 1:
            finish(part)
        else:
            acc_ref = refs[-1]
            k = pl.program_id(2)

            @pl.when(k == 0)
            def _():
                acc_ref[...] = part

            @pl.when((k > 0) & (k < nk - 1))
            def _():
                acc_ref[...] += part

            @pl.when(k == nk - 1)
            def _():
                finish(acc_ref[...] + part)

    in_specs = [a_spec, b_spec] + ([o_plain] if has_add else [])
    args = (a, b) + ((add,) if has_add else ())
    return pl.pallas_call(
        body, name=name, out_shape=out_shape, grid=(M // tm, N // tn, nk), in_specs=in_specs, out_specs=o_spec,
        scratch_shapes=[pltpu.VMEM((tm, tn), F32)] if nk > 1 else [],
        compiler_params=_cparams(("parallel", "parallel", "arbitrary")),
    )(*args)


def selection_matrices(src_of_new, n_shard):
    src_np = np.asarray(src_of_new, np.int64)
    src = jnp.asarray(src_np.astype(np.int32))
    ref = jnp.arange(N_CHIPS, dtype=jnp.int32)[:, None] * n_shard + jnp.arange(n_shard, dtype=jnp.int32)[None, :]
    pm = (ref[:, :, None] == src[None, None, :]).astype(BF16)
    pmt = (src[None, :, None] == ref[:, None, :]).astype(BF16)
    tc = _pick(len(src_np), 512, LANES)
    feeds = [sorted({int(s) // n_shard for s in src_np[cb * tc:(cb + 1) * tc] if s >= 0}) for cb in range(len(src_np) // tc)]
    return pm, pmt, tc, feeds


def regroup_cols(name, g, sel):
    pm, _, tc, feeds = sel
    nn = g.shape[-1]
    g = g.reshape(N_CHIPS, -1, nn)
    K, n_new = g.shape[1], pm.shape[-1]
    tm = _pick(K, 512, SUBLANES_BF16)

    def body(g_ref, pm_ref, o_ref):
        for cb, chips in enumerate(feeds):
            @pl.when(pl.program_id(0) == cb)
            def _(chips=chips):
                acc = jnp.zeros((tm, tc), F32)
                for j in chips:
                    acc = acc + _dot(g_ref[j], pm_ref[j], "nn")
                o_ref[...] = acc.astype(o_ref.dtype)

    return pl.pallas_call(
        body, name=name, out_shape=jax.ShapeDtypeStruct((K, n_new), BF16), grid=(n_new // tc, K // tm),
        in_specs=[pl.BlockSpec((N_CHIPS, tm, nn), lambda c, i: (0, i, 0)), pl.BlockSpec((N_CHIPS, nn, tc), lambda c, i: (0, 0, c))],
        out_specs=pl.BlockSpec((tm, tc), lambda c, i: (i, c)), compiler_params=_cparams(("parallel", "parallel")),
    )(g, pm)


def ungroup_cols(name, dw, sel):
    _, pmt, tc, feeds = sel
    K, n_new = dw.shape
    nn = pmt.shape[-1]
    kh = K // 2
    tm = _pick(kh, 512, SUBLANES_BF16)
    hb = kh // tm
    fed_by = [[cb for cb, chips in enumerate(feeds) if j in chips] for j in range(N_CHIPS)]

    def body(dw_ref, pmt_ref, o_ref):
        for j, blocks in enumerate(fed_by):
            @pl.when(pl.program_id(0) == j)
            def _(blocks=blocks):
                acc = jnp.zeros((tm, nn), F32)
                for cb in blocks:
                    acc = acc + _dot(dw_ref[:, cb * tc:(cb + 1) * tc], pmt_ref[cb * tc:(cb + 1) * tc, :], "nn")
                o_ref[...] = acc.astype(o_ref.dtype)

    return pl.pallas_call(
        body, name=name, out_shape=jax.ShapeDtypeStruct((2, N_CHIPS, kh, nn), BF16), grid=(N_CHIPS, 2 * hb),
        in_specs=[pl.BlockSpec((tm, n_new), lambda j, i: (i, 0)), pl.BlockSpec((None, n_new, nn), lambda j, i: (j, 0, 0))],
        out_specs=pl.BlockSpec((None, None, tm, nn), lambda j, i: (i // hb, j, i % hb, 0)),
        compiler_params=_cparams(("parallel", "parallel")),
    )(dw, pmt)


def rowwise(name, fn, rows, vecs, outs, sums=(), *, M):
    rows = [tuple(r) + (0,) * (4 - len(r)) for r in rows]
    nr, nv, no, ns = len(rows), len(vecs), len(outs), len(sums)
    per_row = sum(w * a.dtype.itemsize for a, w, _, _ in rows) + sum(w * jnp.dtype(d).itemsize for w, d in outs)
    tr = _pick(M, max(8, min(512, ROW_BUDGET // (2 * per_row))), 16 if M % 16 == 0 else 8)

    def body(*refs):
        i = pl.program_id(0)
        res = fn(*[r[...] for r in refs[:nr + nv]])
        o_refs = refs[nr + nv:nr + nv + no]
        s_refs = refs[nr + nv + no:]
        for ref, val in zip(o_refs, res[:no]):
            ref[...] = val.astype(ref.dtype)
        if ns:
            @pl.when(i == 0)
            def _():
                for ref in s_refs:
                    ref[...] = jnp.zeros(ref.shape, F32)

            for ref, val in zip(s_refs, res[no:]):
                ref[...] += val

    in_specs = [pl.BlockSpec((tr, w), functools.partial(lambda i, cb, rb: (i + rb, cb), cb=off // w, rb=roff // tr)) for _, w, off, roff in rows]
    for _, w, off, roff in rows:
        assert off % w == 0 and roff % tr == 0
    in_specs += [pl.BlockSpec(v.shape, functools.partial(lambda i, nd: (0,) * nd, nd=v.ndim)) for v in vecs]
    out_specs = [pl.BlockSpec((tr, w), lambda i: (i, 0)) for w, _ in outs]
    out_specs += [pl.BlockSpec((1, w), lambda i: (0, 0)) for w in sums]
    out_shape = [jax.ShapeDtypeStruct((M, w), d) for w, d in outs] + [jax.ShapeDtypeStruct((1, w), F32) for w in sums]
    return pl.pallas_call(
        body, name=name, out_shape=out_shape, grid=(M // tr,), in_specs=in_specs, out_specs=out_specs,
        compiler_params=_cparams(("arbitrary",) if ns else ("parallel",)),
    )(*[r[0] for r in rows], *vecs)


def _rms(x, g):
    xf = x.astype(F32)
    return xf * lax.rsqrt(jnp.mean(xf * xf, axis=-1, keepdims=True) + NORM_EPS) * g


def _gelu(y):
    return 0.5 * y * (1.0 + jnp.tanh(math.sqrt(2.0 / math.pi) * (y + 0.044715 * (y * y * y))))


def _colsum(v):
    return jnp.sum(v, axis=0, keepdims=True)


def rms_fwd(name, x, width, off, g, *, M):
    return rowwise(name, lambda xb, gb: (_rms(xb, gb),), [(x, width, off)], [g], [(width, BF16)], M=M)[0]


def rms_bwd(name, x, width, off, g, dy, resid=None, *, M, out_dtype=F32):
    def fn(xb, dyb, *rest):
        gb = rest[-1]
        _, vjp = jax.vjp(_rms, xb.astype(F32), gb)
        dx, dg = vjp(dyb.astype(F32))
        if resid is not None:
            dx = dx + rest[0]
        return dx, dg

    rows = [(x, width, off), (dy, width, 0)] + ([(resid, width, 0)] if resid is not None else [])
    return rowwise(name, fn, rows, [g], [(width, out_dtype)], [width], M=M)


def lane_concat(name, parts, *, M, pad_to=None):
    width = sum(p.shape[1] for p in parts)
    pad = 0 if pad_to is None else pad_to - width

    def fn(*blocks):
        cols = [b.astype(BF16) for b in blocks]
        if pad:
            cols.append(jnp.zeros((blocks[0].shape[0], pad), BF16))
        return (jnp.concatenate(cols, axis=1),)

    return rowwise(name, fn, [(p, p.shape[1], 0) for p in parts], [], [(width + pad, BF16)], M=M)[0]


def _swap_halves(x, half):
    w = x.shape[-1]
    lane = lax.broadcasted_iota(jnp.int32, x.shape, x.ndim - 1)
    first = (lane % (2 * half)) < half
    return jnp.where(first, pltpu.roll(x, w - half, x.ndim - 1), pltpu.roll(x, half, x.ndim - 1))


def dilated_bias(M):
    delta = jnp.arange(M, dtype=jnp.int32)[:, None] - jnp.arange(M, dtype=jnp.int32)[None, :]
    w = jnp.zeros(delta.shape, F32)
    for window, dil in DIL_PATTERNS:
        ok = (delta >= 0) & (delta <= window)
        if dil > 1:
            ok = ok & ((delta & (dil - 1)) == 0)
        w = w + ok.astype(F32)
    return jnp.where(w > 0, jnp.log(jnp.maximum(w, 1.0)), NEG)


def _dot(a, b, mode):
    return lax.dot_general(a, b, _DOT_DIMS[mode], preferred_element_type=F32)


def attention_fwd(name, qa, qa_off, ka, ka_off, v, v_off, *, da, dv, pairs, scale, M, qb=None, qb_off=0, kb=None, bias=None):
    tq = min(256, M)
    tk = min(512, M)
    has_b = qb is not None
    has_bias = bias is not None
    dr = MLA_ROPE

    def body(*refs):
        refs = list(refs)
        bias_ref = refs.pop(3) if has_bias else None
        if has_b:
            qa_ref, ka_ref, v_ref, qb_ref, kb_ref, o_ref, lse_ref = refs
        else:
            qa_ref, ka_ref, v_ref, o_ref, lse_ref = refs
        i = pl.program_id(1)
        t0 = i * tq
        nkb = (t0 + tq + tk - 1) // tk
        n_full = nkb if has_bias else t0 // tk
        o_parts, lse_parts = [], []
        for hh in range(2):
            q1 = qa_ref[:, hh * da:(hh + 1) * da].astype(BF16)
            q2 = qb_ref[:, hh * dr:(hh + 1) * dr].astype(BF16) if has_b else None

            def step(kbi, carry, masked, hh=hh, q1=q1, q2=q2):
                m, l, acc = carry
                ks = pl.multiple_of(kbi * tk, tk)
                k1 = ka_ref[pl.ds(ks, tk), hh * da:(hh + 1) * da].astype(BF16)
                s = _dot(q1, k1, "nt")
                if has_b:
                    s = s + _dot(q2, kb_ref[pl.ds(ks, tk), 0:dr].astype(BF16), "nt")
                s = s * scale
                if has_bias:
                    s = s + bias_ref[:, pl.ds(ks, tk)]
                elif masked:
                    delta = (t0 + lax.broadcasted_iota(jnp.int32, (tq, tk), 0)) - (ks + lax.broadcasted_iota(jnp.int32, (tq, tk), 1))
                    s = jnp.where(delta >= 0, s, NEG)
                m_new = jnp.maximum(m, jnp.max(s, axis=1, keepdims=True))
                alpha = jnp.exp(m - m_new)
                p = jnp.exp(s - m_new)
                l = alpha * l + jnp.sum(p, axis=1, keepdims=True)
                vv = v_ref[pl.ds(ks, tk), hh * dv:(hh + 1) * dv].astype(BF16)
                acc = alpha * acc + _dot(p.astype(BF16), vv, "nn")
                return m_new, l, acc

            carry = (jnp.full((tq, 1), NEG, F32), jnp.zeros((tq, 1), F32), jnp.zeros((tq, dv), F32))
            carry = lax.fori_loop(0, n_full, functools.partial(step, masked=False), carry)
            m, l, acc = lax.fori_loop(n_full, nkb, functools.partial(step, masked=True), carry)
            o_parts.append(acc / l)
            lse_parts.append(m + jnp.log(l))
        o_ref[...] = jnp.concatenate(o_parts, axis=1)
        lane = lax.broadcasted_iota(jnp.int32, (tq, LANES), 1)
        lse_ref[...] = jnp.where(lane == 0, lse_parts[0], jnp.where(lane == 1, lse_parts[1], 0.0))

    assert qa_off % (2 * da) == 0 and ka_off % (2 * da) == 0 and v_off % (2 * dv) == 0
    in_specs = [
        pl.BlockSpec((tq, 2 * da), lambda hp, i: (i, qa_off // (2 * da) + hp)),
        pl.BlockSpec((M, 2 * da), lambda hp, i: (0, ka_off // (2 * da) + hp)),
        pl.BlockSpec((M, 2 * dv), lambda hp, i: (0, v_off // (2 * dv) + hp)),
    ]
    args = [qa, ka, v]
    if has_bias:
        in_specs.append(pl.BlockSpec((tq, M), lambda hp, i: (i, 0)))
        args.append(bias)
    if has_b:
        assert qb_off % LANES == 0
        in_specs += [pl.BlockSpec((tq, LANES), lambda hp, i: (i, qb_off // LANES + hp)),
                     pl.BlockSpec((M, LANES), lambda hp, i: (0, 0))]
        args += [qb, kb]
    out_specs = [pl.BlockSpec((tq, 2 * dv), lambda hp, i: (i, hp)),
                 pl.BlockSpec((None, tq, LANES), lambda hp, i: (hp, i, 0))]
    out_shape = [jax.ShapeDtypeStruct((M, pairs * 2 * dv), F32), jax.ShapeDtypeStruct((pairs, M, LANES), F32)]
    return pl.pallas_call(
        body, name=name, out_shape=out_shape, grid=(pairs, M // tq), in_specs=in_specs, out_specs=out_specs,
        compiler_params=_cparams(("parallel", "arbitrary")),
    )(*args)


def attention_bwd(name, qa, qa_off, ka, ka_off, v, v_off, o, do, lse, *, da, dv, pairs, scale, M,
                  qb=None, qb_off=0, kb=None, bias=None):
    tq = min(256, M)
    tk = min(256, M)
    has_b = qb is not None
    has_bias = bias is not None
    dr = MLA_ROPE

    def body(*refs):
        refs = list(refs)
        bias_ref = refs.pop(6) if has_bias else None
        if has_b:
            qa_ref, ka_ref, v_ref, o_ref, do_ref, lse_ref, qb_ref, kb_ref, dqa_ref, dka_ref, dv_ref, dqb_ref, dkb_ref = refs
        else:
            qa_ref, ka_ref, v_ref, o_ref, do_ref, lse_ref, dqa_ref, dka_ref, dv_ref = refs
        hp = pl.program_id(0)
        i = pl.program_id(1)
        t0 = i * tq
        nkb = (t0 + tq + tk - 1) // tk
        n_full = nkb if has_bias else t0 // tk

        @pl.when(i == 0)
        def _():
            dka_ref[...] = jnp.zeros(dka_ref.shape, F32)
            dv_ref[...] = jnp.zeros(dv_ref.shape, F32)

        if has_b:
            @pl.when((i == 0) & (hp == 0))
            def _():
                dkb_ref[...] = jnp.zeros(dkb_ref.shape, F32)

        dq1_parts, dq2_parts = [], []
        for hh in range(2):
            q1 = qa_ref[:, hh * da:(hh + 1) * da].astype(BF16)
            q2 = qb_ref[:, hh * dr:(hh + 1) * dr].astype(BF16) if has_b else None
            do_h = do_ref[:, hh * dv:(hh + 1) * dv]
            o_h = o_ref[:, hh * dv:(hh + 1) * dv]
            rowdot = jnp.sum(do_h * o_h, axis=1, keepdims=True)
            do_bf = do_h.astype(BF16)
            lse_h = lse_ref[:, hh:hh + 1]

            def step(kbi, carry, masked, hh=hh, q1=q1, q2=q2, do_bf=do_bf, rowdot=rowdot, lse_h=lse_h):
                dq1, dq2 = carry
                ks = pl.multiple_of(kbi * tk, tk)
                k1 = ka_ref[pl.ds(ks, tk), hh * da:(hh + 1) * da].astype(BF16)
                s = _dot(q1, k1, "nt")
                if has_b:
                    k2 = kb_ref[pl.ds(ks, tk), 0:dr].astype(BF16)
                    s = s + _dot(q2, k2, "nt")
                s = s * scale
                if has_bias:
                    s = s + bias_ref[:, pl.ds(ks, tk)]
                elif masked:
                    delta = (t0 + lax.broadcasted_iota(jnp.int32, (tq, tk), 0)) - (ks + lax.broadcasted_iota(jnp.int32, (tq, tk), 1))
                    s = jnp.where(delta >= 0, s, NEG)
                p = jnp.exp(s - lse_h)
                vv = v_ref[pl.ds(ks, tk), hh * dv:(hh + 1) * dv].astype(BF16)
                dp = _dot(do_bf, vv, "nt")
                ds = (p * (dp - rowdot) * scale).astype(BF16)
                dq1 = dq1 + _dot(ds, k1, "nn")
                dka_ref[pl.ds(ks, tk), hh * da:(hh + 1) * da] += _dot(ds, q1, "tn")
                dv_ref[pl.ds(ks, tk), hh * dv:(hh + 1) * dv] += _dot(p.astype(BF16), do_bf, "tn")
                if has_b:
                    dq2 = dq2 + _dot(ds, k2, "nn")
                    dkb_ref[pl.ds(ks, tk), 0:dr] += _dot(ds, q2, "tn")
                return dq1, dq2

            carry = lax.fori_loop(0, n_full, functools.partial(step, masked=False), (jnp.zeros((tq, da), F32), jnp.zeros((tq, dr), F32)))
            dq1, dq2 = lax.fori_loop(n_full, nkb, functools.partial(step, masked=True), carry)
            dq1_parts.append(dq1)
            dq2_parts.append(dq2)
        dqa_ref[...] = jnp.concatenate(dq1_parts, axis=1).astype(dqa_ref.dtype)
        if has_b:
            dqb_ref[...] = jnp.concatenate(dq2_parts, axis=1).astype(dqb_ref.dtype)

    in_specs = [
        pl.BlockSpec((tq, 2 * da), lambda hp, i: (i, qa_off // (2 * da) + hp)),
        pl.BlockSpec((M, 2 * da), lambda hp, i: (0, ka_off // (2 * da) + hp)),
        pl.BlockSpec((M, 2 * dv), lambda hp, i: (0, v_off // (2 * dv) + hp)),
        pl.BlockSpec((tq, 2 * dv), lambda hp, i: (i, hp)),
        pl.BlockSpec((tq, 2 * dv), lambda hp, i: (i, hp)),
        pl.BlockSpec((None, tq, LANES), lambda hp, i: (hp, i, 0)),
    ]
    args = [qa, ka, v, o, do, lse]
    if has_bias:
        in_specs.append(pl.BlockSpec((tq, M), lambda hp, i: (i, 0)))
        args.append(bias)
    out_specs = [pl.BlockSpec((tq, 2 * da), lambda hp, i: (i, hp)),
                 pl.BlockSpec((M, 2 * da), lambda hp, i: (0, hp)),
                 pl.BlockSpec((M, 2 * dv), lambda hp, i: (0, hp))]
    out_shape = [jax.ShapeDtypeStruct((M, pairs * 2 * da), BF16),
                 jax.ShapeDtypeStruct((M, pairs * 2 * da), F32),
                 jax.ShapeDtypeStruct((M, pairs * 2 * dv), F32)]
    if has_b:
        in_specs += [pl.BlockSpec((tq, LANES), lambda hp, i: (i, qb_off // LANES + hp)),
                     pl.BlockSpec((M, LANES), lambda hp, i: (0, 0))]
        args += [qb, kb]
        out_specs += [pl.BlockSpec((tq, LANES), lambda hp, i: (i, hp)), pl.BlockSpec((M, LANES), lambda hp, i: (0, 0))]
        out_shape += [jax.ShapeDtypeStruct((M, pairs * LANES), F32), jax.ShapeDtypeStruct((M, LANES), F32)]
    return pl.pallas_call(
        body, name=name, out_shape=out_shape, grid=(pairs, M // tq), in_specs=in_specs, out_specs=out_specs,
        compiler_params=_cparams(("arbitrary", "arbitrary")),
    )(*args)


def ssm_scan(name, xcat, acat, *, M, reverse=False, hcat=None):
    C2 = xcat.shape[1]
    cb = LANES
    tb = min(128, M)
    nblk = M // tb
    with_da = hcat is not None

    def body(*refs):
        if with_da:
            x_ref, a_ref, h_ref, o_ref, da_ref, p_ref = refs
        else:
            x_ref, a_ref, o_ref, p_ref = refs
        ar, ai = a_ref[:, :cb], a_ref[:, cb:]
        row = lax.broadcasted_iota(jnp.int32, (tb, cb), 0)

        def logscan(xr, xi):
            pr, pi = ar, ai
            d = 1
            while d < tb:
                shift = tb - d if reverse else d
                keep = (row < tb - d) if reverse else (row >= d)
                sr = jnp.where(keep, pltpu.roll(xr, shift, 0), 0.0)
                si = jnp.where(keep, pltpu.roll(xi, shift, 0), 0.0)
                xr, xi = xr + pr * sr - pi * si, xi + pr * si + pi * sr
                pr, pi = pr * pr - pi * pi, 2.0 * pr * pi
                d *= 2
            return xr, xi

        seed = row == (tb - 1 if reverse else 0)
        p0r, p0i = logscan(jnp.where(seed, ar, 0.0), jnp.where(seed, ai, 0.0))
        p_ref[:, :cb] = p0r
        p_ref[:, cb:] = p0i
        sub = lax.broadcasted_iota(jnp.int32, (8, cb), 0)
        edge = 0 if reverse else tb - 8
        pick = sub == (0 if reverse else 7)

        def blk(b, carry):
            cr, ci = carry
            bb = (nblk - 1 - b) if reverse else b
            t0 = pl.multiple_of(bb * tb, tb)
            hr, hi = logscan(x_ref[pl.ds(t0, tb), :cb], x_ref[pl.ds(t0, tb), cb:])
            pr, pi = p_ref[:, :cb], p_ref[:, cb:]
            o_ref[pl.ds(t0, tb), :cb] = hr + pr * cr - pi * ci
            o_ref[pl.ds(t0, tb), cb:] = hi + pr * ci + pi * cr
            te = pl.multiple_of(t0 + edge, 8)
            ncr = jnp.sum(jnp.where(pick, o_ref[pl.ds(te, 8), :cb], 0.0), axis=0, keepdims=True)
            nci = jnp.sum(jnp.where(pick, o_ref[pl.ds(te, 8), cb:], 0.0), axis=0, keepdims=True)
            return ncr, nci

        lax.fori_loop(0, nblk, blk, (jnp.zeros((1, cb), F32), jnp.zeros((1, cb), F32)))
        if with_da:
            first = lax.broadcasted_iota(jnp.int32, (M, cb), 0) >= 1
            hpr = jnp.where(first, pltpu.roll(h_ref[:, :cb], 1, 0), 0.0)
            hpi = jnp.where(first, pltpu.roll(h_ref[:, cb:], 1, 0), 0.0)
            lr, li = o_ref[:, :cb], o_ref[:, cb:]
            da_ref[:, :cb] = _colsum(lr * hpr + li * hpi)
            da_ref[:, cb:] = _colsum(li * hpr - lr * hpi)

    blk_spec = pl.BlockSpec((M, 2 * cb), lambda j: (0, j))
    vec_spec = pl.BlockSpec((1, 2 * cb), lambda j: (0, j))
    in_specs = [blk_spec, vec_spec] + ([blk_spec] if with_da else [])
    out_specs = [blk_spec] + ([vec_spec] if with_da else [])
    out_shape = [jax.ShapeDtypeStruct((M, C2), F32)] + ([jax.ShapeDtypeStruct((1, C2), F32)] if with_da else [])
    args = [xcat, acat] + ([hcat] if with_da else [])
    res = pl.pallas_call(
        body, name=name, out_shape=out_shape, grid=(C2 // (2 * cb),), in_specs=in_specs, out_specs=out_specs,
        scratch_shapes=[pltpu.VMEM((tb, 2 * cb), F32)], compiler_params=_cparams(("parallel",)),
    )(*args)
    return res if with_da else res[0]


def _ssm_param_fn(a_re, a_im, ldt, b_re, b_im):
    lr, li = jnp.minimum(a_re, -1e-4), a_im
    dt = jnp.exp(ldt)
    e, ang = jnp.exp(lr * dt), li * dt
    ar, ai = e * jnp.cos(ang), e * jnp.sin(ang)
    den = lr * lr + li * li
    nr, ni = ar - 1.0, ai
    cr, ci = (nr * lr + ni * li) / den, (ni * lr - nr * li) / den
    return ar, ai, cr * b_re - ci * b_im, cr * b_im + ci * b_re


def _whole(shape):
    return pl.BlockSpec(shape, functools.partial(lambda nd: (0,) * nd, nd=len(shape)))


def ssm_param_fwd(name, a_re, a_im, ldt, b_re, b_im):
    def body(*refs):
        res = _ssm_param_fn(*[r[...] for r in refs[:5]])
        for ref, val in zip(refs[5:], res):
            ref[...] = val

    ins = [a_re, a_im, ldt, b_re, b_im]
    outs = [a_re, a_re, b_re, b_re]
    return pl.pallas_call(
        body, name=name, out_shape=[jax.ShapeDtypeStruct(t.shape, F32) for t in outs],
        in_specs=[_whole(t.shape) for t in ins], out_specs=[_whole(t.shape) for t in outs], compiler_params=_cparams(),
    )(*ins)


def ssm_param_bwd(name, a_re, a_im, ldt, b_re, b_im, d_ar, d_ai, d_bbr, d_bbi):
    def body(*refs):
        _, vjp = jax.vjp(_ssm_param_fn, *[r[...] for r in refs[:5]])
        res = vjp(tuple(r[...] for r in refs[5:9]))
        for ref, val in zip(refs[9:], res):
            ref[...] = val

    ins = [a_re, a_im, ldt, b_re, b_im, d_ar, d_ai, d_bbr, d_bbi]
    outs = [a_re, a_im, ldt, b_re, b_im]
    return pl.pallas_call(
        body, name=name, out_shape=[jax.ShapeDtypeStruct(t.shape, F32) for t in outs],
        in_specs=[_whole(t.shape) for t in ins], out_specs=[_whole(t.shape) for t in outs], compiler_params=_cparams(),
    )(*ins)


ANY = pl.BlockSpec(memory_space=pl.ANY)


def _place():
    x, y, c = lax.axis_index("x"), lax.axis_index("y"), lax.axis_index("c")
    chips = [(1 - x, y), (x, 1 - y), (1 - x, 1 - y)]
    return x, y, c, chips


def cast_into_slot(name, w, kind, l=None):
    K, nn = w.shape[-2:]
    hr, hc = (K // 2, nn) if kind == "cols" else (K, nn // 2)
    tr = _pick(hr, max(16, min(512, ROW_BUDGET // (2 * hc * 6))), SUBLANES_BF16)
    nb = hr // tr

    def body(w_ref, o_ref):
        o_ref[...] = w_ref[...].astype(BF16)

    lead = () if l is None else (l,)
    if kind == "cols":
        in_spec = pl.BlockSpec((None,) * len(lead) + (tr, hc), lambda h, i: lead + (h * nb + i, 0))
    else:
        in_spec = pl.BlockSpec((None,) * len(lead) + (tr, hc), lambda h, i: lead + (i, h))
    return pl.pallas_call(
        body, name=name, out_shape=jax.ShapeDtypeStruct((N_CHIPS, 2, hr, hc), BF16), grid=(2, nb), in_specs=[in_spec],
        out_specs=pl.BlockSpec((None, None, tr, hc), lambda h, i: (2 * lax.axis_index("x") + lax.axis_index("y"), h, i, 0)),
        compiler_params=_cparams(("parallel", "parallel")),
    )(w)


HBM_SPEC = pl.BlockSpec(memory_space=pltpu.HBM)
SEM_SPEC = pl.BlockSpec(memory_space=pltpu.SEMAPHORE)
SPLIT_PARAMS = pltpu.CompilerParams(has_side_effects=pltpu.SideEffectType.DATAFLOW_SIDE_EFFECTING)


def _in_hbm(t):
    return pltpu.with_memory_space_constraint(t, pltpu.HBM)


def split_start(name, plan, n, bufs, fresh, carrier):
    nb, nf = len(bufs), len(fresh)

    def body(*refs):
        outs = refs[nb + 1:]
        for i, (s, d, dev) in enumerate(plan(list(outs[2:2 + nb + nf]))):
            pltpu.make_async_remote_copy(src_ref=s, dst_ref=d, send_sem=outs[0].at[i], recv_sem=outs[1].at[i],
                                         device_id=dev, device_id_type=MESH).start()

    hbm = lambda t: pltpu.HBM(t.shape, t.dtype)
    res = pl.pallas_call(
        body, name=name,
        out_shape=(pltpu.SemaphoreType.DMA((n,)), pltpu.SemaphoreType.DMA((n,)), *[hbm(t) for t in bufs], *[hbm(t) for t in fresh], hbm(carrier)),
        in_specs=[HBM_SPEC] * (nb + 1), out_specs=(SEM_SPEC, SEM_SPEC) + (HBM_SPEC,) * (nb + nf + 1),
        input_output_aliases={**{i: 2 + i for i in range(nb)}, nb: 2 + nb + nf}, compiler_params=SPLIT_PARAMS,
    )(*[_in_hbm(t) for t in bufs], _in_hbm(carrier))
    return (res[0], res[1]), list(res[2:2 + nb]), list(res[2 + nb:2 + nb + nf]), res[2 + nb + nf]


def split_wait(name, plan, sems, bufs, carrier):
    nb = len(bufs)

    def body(*refs):
        for i, (s, d, dev) in enumerate(plan(list(refs[:nb]))):
            cp = pltpu.make_async_remote_copy(src_ref=s, dst_ref=d, send_sem=refs[nb].at[i], recv_sem=refs[nb + 1].at[i],
                                              device_id=dev, device_id_type=MESH)
            cp.wait_send()
            cp.wait_recv()

    hbm = lambda t: pltpu.HBM(t.shape, t.dtype)
    res = pl.pallas_call(
        body, name=name, out_shape=(*[hbm(t) for t in bufs], hbm(carrier)),
        in_specs=[HBM_SPEC] * nb + [SEM_SPEC, SEM_SPEC, HBM_SPEC], out_specs=(HBM_SPEC,) * (nb + 1),
        input_output_aliases={**{i: i for i in range(nb)}, nb + 2: nb}, compiler_params=SPLIT_PARAMS,
    )(*bufs, sems[0], sems[1], carrier)
    return list(res[:nb]), res[nb]


def _me_sib_chips():
    x, y, c, chips = _place()
    return 2 * x + y, c, (x, y, 1 - c), chips


def plan_gather_ici(refs):
    me, c, _, chips = _me_sib_chips()
    return [(r.at[me, c], r.at[me, c], (chip[0], chip[1], c)) for r in refs for chip in chips]


def plan_gather_pass(refs):
    _, c, sib, chips = _me_sib_chips()
    return [(r.at[2 * chip[0] + chip[1], c], r.at[2 * chip[0] + chip[1], c], sib) for r in refs for chip in chips]


def plan_pair(n_arrays):
    def plan(refs):
        _, c, sib, _ = _me_sib_chips()
        return [(refs[a].at[1 - c], refs[n_arrays + a], sib) for a in range(n_arrays)]
    return plan


def plan_chips(n_arrays):
    def plan(refs):
        _, c, _, chips = _me_sib_chips()
        return [(refs[a].at[2 * chip[0] + chip[1]], refs[n_arrays + a].at[k], (chip[0], chip[1], c))
                for a in range(n_arrays) for k, chip in enumerate(chips)]
    return plan


def plan_share(n_arrays):
    def plan(refs):
        _, _, sib, _ = _me_sib_chips()
        return [(refs[a], refs[n_arrays + a], sib) for a in range(n_arrays)]
    return plan


def swap_with_sibling(name, src, pick_other_half):
    shape = src.shape[1:] if pick_other_half else src.shape

    def body(src_ref, out_ref, ssem, rsem):
        x, y, c, _ = _place()
        cp = pltpu.make_async_remote_copy(src_ref=src_ref.at[1 - c] if pick_other_half else src_ref, dst_ref=out_ref,
                                          send_sem=ssem, recv_sem=rsem, device_id=(x, y, 1 - c), device_id_type=MESH)
        cp.start()
        cp.wait()

    return pl.pallas_call(
        body, name=name, out_shape=jax.ShapeDtypeStruct(shape, src.dtype), in_specs=[ANY], out_specs=ANY,
        scratch_shapes=[pltpu.SemaphoreType.DMA(()), pltpu.SemaphoreType.DMA(())],
    )(src)


def exchange_chips(name, src, per_chip):
    shape = src.shape[1:] if per_chip else src.shape

    def body(src_ref, out_ref, send_sems, recv_sems):
        x, y, c, chips = _place()
        cps = []
        for k, chip in enumerate(chips):
            s = src_ref.at[2 * chip[0] + chip[1]] if per_chip else src_ref
            cps.append(pltpu.make_async_remote_copy(src_ref=s, dst_ref=out_ref.at[k], send_sem=send_sems.at[k], recv_sem=recv_sems.at[k],
                                                    device_id=(chip[0], chip[1], c), device_id_type=MESH))
        for cp in cps:
            cp.start()
        for cp in cps:
            cp.wait()

    return pl.pallas_call(
        body, name=name, out_shape=jax.ShapeDtypeStruct((3,) + shape, src.dtype), in_specs=[ANY], out_specs=ANY,
        scratch_shapes=[pltpu.SemaphoreType.DMA((3,)), pltpu.SemaphoreType.DMA((3,))],
    )(src)


def pair_sum(name, p, got):
    _, _, rh, cw = p.shape
    tr = _pick(rh, max(16, min(512, ROW_BUDGET // (2 * cw * 10))), SUBLANES_BF16)

    def body(p_ref, got_ref, s_ref, own_ref):
        j = pl.program_id(1)
        tot = p_ref[...].astype(F32) + got_ref[...].astype(F32)
        s_ref[...] = tot.astype(BF16)

        @pl.when(j == 2 * lax.axis_index("x") + lax.axis_index("y"))
        def _():
            own_ref[...] = tot

    return pl.pallas_call(
        body, name=name, grid=(rh // tr, N_CHIPS),
        in_specs=[pl.BlockSpec((None, None, tr, cw), lambda i, j: (lax.axis_index("c"), j, i, 0)),
                  pl.BlockSpec((None, tr, cw), lambda i, j: (j, i, 0))],
        out_specs=[pl.BlockSpec((None, tr, cw), lambda i, j: (j, i, 0)),
                   pl.BlockSpec((tr, cw), lambda i, j: (i, 0))],
        out_shape=[jax.ShapeDtypeStruct((N_CHIPS, rh, cw), BF16), jax.ShapeDtypeStruct((rh, cw), F32)],
        compiler_params=_cparams(("arbitrary", "arbitrary")),
    )(p, got)


def chips_sum(name, own, parts):
    rh, cw = own.shape
    parts = parts.reshape(3 * rh, cw)
    return rowwise(name, lambda o, a, b, c: (((o + a.astype(F32)) + b.astype(F32)) + c.astype(F32),),
                   [(own, cw, 0), (parts, cw, 0, 0), (parts, cw, 0, rh), (parts, cw, 0, 2 * rh)], [], [(cw, F32)], M=rh)[0]


def all_reduce_small(buf):
    r = buf.shape[0]
    got = swap_with_sibling("ar_pair", buf, False)
    chip = rowwise("ar_pairsum", lambda a, b: (a + b,), [(buf, LANES, 0), (got, LANES, 0)], [], [(LANES, F32)], M=r)[0]
    parts = exchange_chips("ar_chips", chip, False).reshape(3 * r, LANES)
    return rowwise("ar_sum", lambda o, fx, fy, fxy: ((o + fy) + (fx + fxy),),
                   [(chip, LANES, 0), (parts, LANES, 0, 0), (parts, LANES, 0, r), (parts, LANES, 0, 2 * r)], [], [(LANES, F32)], M=r)[0]


def _adam_fn(w, g, m, v):
    m = ADAM_B1 * m + (1.0 - ADAM_B1) * g
    v = ADAM_B2 * v + (1.0 - ADAM_B2) * (g * g)
    m_hat = m / (1.0 - ADAM_B1 ** ADAM_STEP)
    v_hat = v / (1.0 - ADAM_B2 ** ADAM_STEP)
    return -ADAM_LR * (m_hat / (jnp.sqrt(v_hat) + ADAM_EPS) + ADAM_WD * w), m, v


def adamw(name, w, g, m, v):
    r, cw = w.shape
    return rowwise(name, _adam_fn, [(t, cw, 0) for t in (w, g, m, v)], [], [(cw, F32)] * 3, M=r)


def adamw_layers(name, w, m, v, mines, theirs, kind):
    L, K, nn = w.shape
    hr, hc = (K // 2, nn) if kind == "cols" else (K, nn // 2)
    tr = _pick(hr, max(8, min(256, ROW_BUDGET // (2 * hc * 4 * (7 + 2 * L)))), 8)
    nb = hr // tr

    def body(*refs):
        w_ref, m_ref, v_ref = refs[:3]
        outs = refs[3 + 2 * L:]
        l, mine_here = pl.program_id(0), pl.program_id(1) == lax.axis_index("c")
        g = jnp.zeros((tr, hc), F32)
        for ll in range(L):
            g = jnp.where(l == ll, jnp.where(mine_here, refs[3 + ll][...], refs[3 + L + ll][...]), g)
        outs[0][...] = g
        outs[1][...], outs[2][...], outs[3][...] = _adam_fn(w_ref[...], g, m_ref[...], v_ref[...])

    if kind == "cols":
        full = pl.BlockSpec((None, tr, hc), lambda l, h, i: (l, h * nb + i, 0))
    else:
        full = pl.BlockSpec((None, tr, hc), lambda l, h, i: (l, i, h))
    def half_spec(ll, mine):
        def imap(l, h, i):
            here = (l == ll) & ((h == lax.axis_index("c")) == mine)
            return (jnp.where(here, i, 0), 0)
        return pl.BlockSpec((tr, hc), imap)

    halves = [half_spec(ll, True) for ll in range(L)] + [half_spec(ll, False) for ll in range(L)]
    return pl.pallas_call(
        body, name=name, grid=(L, 2, nb), in_specs=[full] * 3 + halves, out_specs=[full] * 4,
        out_shape=[jax.ShapeDtypeStruct((L, K, nn), F32)] * 4, compiler_params=_cparams(("parallel", "parallel", "parallel")),
    )(w, m, v, *mines, *theirs)


class Dims:
    def __init__(self, x, g_q, g_kv, g_out_mla, g_out_ssm, g_out_dil, ff):
        self.M, self.D = x.shape[-2], x.shape[-1]
        self.QL, self.KVL = g_q.shape[-1], g_kv.shape[-1]
        self.MW, self.SW, self.DW = g_out_mla.shape[-1], g_out_ssm.shape[-1], g_out_dil.shape[-1]
        self.H = self.MW // MLA_V
        self.FF = ff
        self.G = self.SW // SSM_GROUP
        self.C = self.G * SSM_STATE
        self.o_cq, self.o_u = 0, self.QL
        self.o_qd = self.o_u + self.SW
        self.o_kd = self.o_qd + self.DW
        self.o_vd = self.o_kd + self.DW
        self.o_ckv = self.o_vd + self.DW
        self.o_kr = self.o_ckv + self.KVL
        self.PW = -(-(self.o_kr + MLA_ROPE) // (4 * LANES)) * (4 * LANES)
        assert self.o_u % self.SW == 0 and self.o_qd % LANES == 0 and self.o_ckv % self.KVL == 0 and self.o_kr % LANES == 0
        assert self.H % 2 == 0 and self.DW % LANES == 0 and self.C % LANES == 0
        self.QW = self.H * (MLA_NOPE + MLA_ROPE)
        self.KVW = self.H * (MLA_NOPE + MLA_V)
        sizes = [self.QL, self.KVL, MLA_ROPE, self.SW, self.DW, self.DW, self.DW]
        starts = np.concatenate([[0], np.cumsum(sizes)[:-1]])
        self.ref_cols = {n: (int(s), int(z)) for n, s, z in zip(["cq", "ckv", "kr", "u", "qd", "kd", "vd"], starts, sizes)}
        self.INW = int(sum(sizes))
        self.new_order = ["cq", "u", "qd", "kd", "vd", "ckv", "kr"]
        src = np.concatenate([np.arange(self.ref_cols[n][0], self.ref_cols[n][0] + self.ref_cols[n][1]) for n in self.new_order])
        self.src_in = np.concatenate([src, -np.ones(self.PW - self.INW, np.int64)])
        self.src_q = self._heads_split(self.H, MLA_NOPE, MLA_ROPE)
        self.src_kv = self._heads_split(self.H, MLA_NOPE, MLA_V)

    @staticmethod
    def _heads_split(h, d1, d2):
        first = (np.arange(h)[:, None] * (d1 + d2) + np.arange(d1)[None, :]).reshape(-1)
        second = (np.arange(h)[:, None] * (d1 + d2) + d1 + np.arange(d2)[None, :]).reshape(-1)
        return np.concatenate([first, second])


def _regroup_in(dm, w):
    parts = [w[..., dm.ref_cols[n][0]:dm.ref_cols[n][0] + dm.ref_cols[n][1]] for n in dm.new_order]
    pad = dm.PW - dm.INW
    return jnp.concatenate(parts + [jnp.zeros(w.shape[:-1] + (pad,), w.dtype)], axis=-1)


def _ungroup_in(dm, w):
    off, pieces = 0, {}
    for n in dm.new_order:
        pieces[n] = w[..., off:off + dm.ref_cols[n][1]]
        off += dm.ref_cols[n][1]
    return jnp.concatenate([pieces[n] for n in ["cq", "ckv", "kr", "u", "qd", "kd", "vd"]], axis=-1)


def _split_heads(w, h, d1):
    t = w.reshape(w.shape[:-1] + (h, -1))
    return jnp.concatenate([t[..., :d1].reshape(w.shape[:-1] + (-1,)), t[..., d1:].reshape(w.shape[:-1] + (-1,))], axis=-1)


def _merge_heads(w, h, d1):
    a = w[..., :h * d1].reshape(w.shape[:-1] + (h, d1))
    b = w[..., h * d1:].reshape(w.shape[:-1] + (h, -1))
    return jnp.concatenate([a, b], axis=-1).reshape(w.shape[:-1] + (-1,))


def _cat_cols(re, im):
    r, c = re.shape
    return jnp.stack([re.reshape(r, c // LANES, LANES), im.reshape(r, c // LANES, LANES)], axis=2).reshape(r, 2 * c)


def _uncat_cols(cat):
    r, c2 = cat.shape
    t = cat.reshape(r, c2 // (2 * LANES), 2, LANES)
    return t[:, :, 0].reshape(r, c2 // 2), t[:, :, 1].reshape(r, c2 // 2)


def _block_diag(t, g):
    _, a, b = t.shape
    eye = jnp.eye(g, dtype=bool)[:, None, :, None]
    return jnp.where(eye, t[:, :, None, :], 0).reshape(g * a, g * b)


def _diag_blocks(m, g):
    a, b = m.shape[0] // g, m.shape[1] // g
    eye = jnp.eye(g, dtype=m.dtype)[:, None, :, None]
    return jnp.sum(m.reshape(g, a, g, b) * eye, axis=2)


def _rope_tables(dm):
    half = MLA_ROPE // 2
    inv_freq = ROPE_THETA ** (-jnp.arange(half, dtype=F32) / half)
    ang = jnp.arange(dm.M, dtype=F32)[:, None] * inv_freq[None, :]
    cos = jnp.concatenate([jnp.cos(ang), jnp.cos(ang)], axis=1)
    sin = jnp.concatenate([-jnp.sin(ang), jnp.sin(ang)], axis=1)
    return jnp.tile(cos, (1, dm.H)), jnp.tile(sin, (1, dm.H)), jnp.tile(cos, (1, LANES // MLA_ROPE)), jnp.tile(sin, (1, LANES // MLA_ROPE))


def _rope(x, cos, sin):
    return x * cos + _swap_halves(x, MLA_ROPE // 2) * sin


def _rope_t(d, cos, sin):
    return d * cos + _swap_halves(d * sin, MLA_ROPE // 2)


def _ssm_layer_params(dm, a_re, a_im, log_dt, b_re, b_im):
    flat = lambda t: t.reshape(1, dm.C)
    ldt = jnp.repeat(log_dt, SSM_STATE).reshape(1, dm.C)
    bt = lambda t: jnp.transpose(t, (2, 0, 1)).reshape(SSM_GROUP, dm.C)
    return flat(a_re), flat(a_im), ldt, bt(b_re), bt(b_im)


def layer_forward(dm, l, x, lw, sp, tabs, hook_q, hook_mid):
    M, D = dm.M, dm.D
    n = lambda s: f"{s}_l{l}"
    sv = {"x_in": x}
    h1 = rms_fwd(n("rms_mix"), x, D, 0, lw["g_mix"], M=M)
    proj = matmul(n("in_proj"), h1, lw["w_in"], "nn", M=M, N=dm.PW, K=D)
    sv.update(h1=h1, proj=proj)
    cqn = rms_fwd(n("rms_q"), proj, dm.QL, dm.o_cq, lw["g_q"], M=M)
    q = matmul(n("q_up"), cqn, lw["w_uq"], "nn", M=M, N=dm.QW, K=dm.QL)
    ckvn = rms_fwd(n("rms_kv"), proj, dm.KVL, dm.o_ckv, lw["g_kv"], M=M)
    kv = matmul(n("kv_up"), ckvn, lw["w_ukv"], "nn", M=M, N=dm.KVW, K=dm.KVL, out_dtype=BF16)
    cosq, sinq, cosk, sink = tabs[:4]
    nw = dm.H * MLA_NOPE

    def rope_fn(qb, kb, cq, sq, ck, sk):
        return jnp.concatenate([qb[:, :nw], _rope(qb[:, nw:], cq, sq)], axis=1), _rope(kb, ck, sk)

    pw = dm.H * MLA_ROPE
    q_bf, kpe = rowwise(n("rope"), rope_fn, [(q, dm.QW, 0), (proj, LANES, dm.o_kr), (cosq, pw, 0), (sinq, pw, 0), (cosk, LANES, 0), (sink, LANES, 0)],
                        [], [(dm.QW, BF16), (LANES, BF16)], M=M)
    mla_scale = (MLA_NOPE + MLA_ROPE) ** -0.5
    o_mla, lse_mla = attention_fwd(n("mla_fwd"), q_bf, 0, kv, 0, kv, nw, da=MLA_NOPE, dv=MLA_V, pairs=dm.H // 2, scale=mla_scale,
                                   M=M, qb=q_bf, qb_off=nw, kb=kpe)
    o_mla = hook_q(o_mla)
    sv.update(cqn=cqn, ckvn=ckvn, q_bf=q_bf, kv=kv, kpe=kpe, o_mla=o_mla, lse_mla=lse_mla)
    bu = matmul(n("ssm_bu"), proj, sp["bcat"], "nn", M=M, N=2 * dm.C, K=dm.SW, a_off=(0, dm.o_u))
    hcat = ssm_scan(n("ssm_scan"), bu, sp["acat"], M=M)
    ylin = matmul(n("ssm_y"), hcat, sp["ccat"], "nn", M=M, N=dm.SW, K=2 * dm.C)
    yg = rowwise(n("ssm_gelu"), lambda y, u, d: (_gelu(y + d * u),), [(ylin, dm.SW, 0), (proj, dm.SW, dm.o_u)], [lw["d_skip"]],
                 [(dm.SW, BF16)], M=M)[0]
    z = matmul(n("ssm_glu"), yg, lw["w_glu"], "nn", w=("cols", 2 * dm.SW // N_CHIPS), M=M, N=2 * dm.SW, K=dm.SW)
    sw = dm.SW

    def glu_fn(zb, b):
        zz = zb + b
        return (zz[:, :sw] * jax.nn.sigmoid(zz[:, sw:]),)

    o_ssm = rowwise(n("ssm_gate"), glu_fn, [(z, 2 * sw, 0)], [lw["b_glu"]], [(sw, F32)], M=M)[0]
    sv.update(hcat=hcat, ylin=ylin, yg=yg, z=z, o_ssm=o_ssm)
    o_dil, lse_dil = attention_fwd(n("dil_fwd"), proj, dm.o_qd, proj, dm.o_kd, proj, dm.o_vd, da=DIL_HEAD, dv=DIL_HEAD, pairs=dm.DW // LANES,
                                   scale=DIL_HEAD ** -0.5, M=M, bias=tabs[4])
    sv.update(o_dil=o_dil, lse_dil=lse_dil)
    yn = rowwise(n("out_norm"), lambda a, b, c, ga, gb, gc: (jnp.concatenate([_rms(a, ga), _rms(b, gb), _rms(c, gc)], axis=1),),
                 [(o_mla, dm.MW, 0), (o_ssm, dm.SW, 0), (o_dil, dm.DW, 0)], [lw["g_out_mla"], lw["g_out_ssm"], lw["g_out_dil"]],
                 [(D, BF16)], M=M)[0]
    yn = hook_mid(yn, lw)
    x_mid = matmul(n("out_proj"), yn, lw["w_o"], "nn", w=("rows", D // N_CHIPS), M=M, N=D, K=D, add=x)
    h2 = rms_fwd(n("rms_ffn"), x_mid, D, 0, lw["g_ffn"], M=M)
    ffs = dm.FF // N_CHIPS
    gate = matmul(n("ffn_gate"), h2, lw["w_gate"], "nn", w=("cols",ffs), M=M, N=dm.FF, K=D)
    up = matmul(n("ffn_up"), h2, lw["w_up"], "nn", w=("cols",ffs), M=M, N=dm.FF, K=D)
    act = rowwise(n("ffn_act"), lambda g, u: (g * jax.nn.sigmoid(g) * u,), [(gate, dm.FF, 0), (up, dm.FF, 0)], [], [(dm.FF, BF16)], M=M)[0]
    x_out = matmul(n("ffn_down"), act, lw["w_down"], "nn", w=("rows",ffs), M=M, N=D, K=dm.FF, add=x_mid)
    sv.update(yn=yn, x_mid=x_mid, h2=h2, gate=gate, up=up, act=act)
    return x_out, sv


def layer_backward(dm, l, dx, lw, sp, tabs, sv, hook_a, hook_m, hook_b):
    M, D = dm.M, dm.D
    n = lambda s: f"{s}_l{l}"
    g = {}
    ffs = dm.FF // N_CHIPS
    dact = matmul(n("ffn_down_dx"), dx, lw["w_down"], "nt", w=("rows", ffs), M=M, N=dm.FF, K=D)
    g["w_down"] = matmul(n("ffn_down_dw"), sv["act"], dx, "tn", M=dm.FF, N=D, K=M, out_dtype=BF16, into=("rows", ffs))

    def act_bwd(gb, ub, db):
        _, vjp = jax.vjp(lambda a, b: a * jax.nn.sigmoid(a) * b, gb, ub)
        return vjp(db)

    dgate, dup = rowwise(n("ffn_act_bwd"), act_bwd, [(sv["gate"], dm.FF, 0), (sv["up"], dm.FF, 0), (dact, dm.FF, 0)], [],
                         [(dm.FF, BF16), (dm.FF, BF16)], M=M)
    dh2 = matmul(n("ffn_gate_dx"), dgate, lw["w_gate"], "nt", w=("cols",ffs), M=M, N=D, K=dm.FF)
    dh2 = matmul(n("ffn_up_dx"), dup, lw["w_up"], "nt", w=("cols",ffs), M=M, N=D, K=dm.FF, add=dh2)
    g["w_gate"] = matmul(n("ffn_gate_dw"), sv["h2"], dgate, "tn", M=D, N=dm.FF, K=M, out_dtype=BF16, into=("cols", ffs))
    g["w_up"] = matmul(n("ffn_up_dw"), sv["h2"], dup, "tn", M=D, N=dm.FF, K=M, out_dtype=BF16, into=("cols", ffs))
    dx_mid, g["g_ffn"] = rms_bwd(n("rms_ffn_bwd"), sv["x_mid"], D, 0, lw["g_ffn"], dh2, dx, M=M)
    dx_mid = hook_a(dx_mid, g)
    dyn = matmul(n("out_proj_dx"), dx_mid, lw["w_o"], "nt", w=("rows", D // N_CHIPS), M=M, N=D, K=D)
    g["w_o"] = matmul(n("out_proj_dw"), sv["yn"], dx_mid, "tn", M=D, N=D, K=M, out_dtype=BF16, into=("rows", D // N_CHIPS))
    mw, sw, dw = dm.MW, dm.SW, dm.DW

    def out_norm_bwd(a, b, c, dy, ga, gb, gc):
        res, sums = [], []
        for t, gg, lo, hi in ((a, ga, 0, mw), (b, gb, mw, mw + sw), (c, gc, mw + sw, mw + sw + dw)):
            _, vjp = jax.vjp(_rms, t, gg)
            dt, dg = vjp(dy[:, lo:hi])
            res.append(dt)
            sums.append(dg)
        return res + sums

    do_mla, do_ssm, do_dil, g["g_out_mla"], g["g_out_ssm"], g["g_out_dil"] = rowwise(
        n("out_norm_bwd"), out_norm_bwd, [(sv["o_mla"], mw, 0), (sv["o_ssm"], sw, 0), (sv["o_dil"], dw, 0), (dyn, D, 0)],
        [lw["g_out_mla"], lw["g_out_ssm"], lw["g_out_dil"]], [(mw, F32), (sw, F32), (dw, F32)], [mw, sw, dw], M=M)
    proj = sv["proj"]
    dqd, dkd, dvd = attention_bwd(n("dil_bwd"), proj, dm.o_qd, proj, dm.o_kd, proj, dm.o_vd, sv["o_dil"], do_dil, sv["lse_dil"],
                                  da=DIL_HEAD, dv=DIL_HEAD, pairs=dw // LANES, scale=DIL_HEAD ** -0.5, M=M, bias=tabs[4])
    do_ssm = hook_m(do_ssm, g)
    def glu_bwd(zb, db, b):
        _, vjp = jax.vjp(lambda zz, bb: (zz + bb)[:, :sw] * jax.nn.sigmoid((zz + bb)[:, sw:]), zb, b)
        return vjp(db)

    dz, g["b_glu"] = rowwise(n("ssm_gate_bwd"), glu_bwd, [(sv["z"], 2 * sw, 0), (do_ssm, sw, 0)], [lw["b_glu"]], [(2 * sw, BF16)], [2 * sw], M=M)
    dyg = matmul(n("ssm_glu_dx"), dz, lw["w_glu"], "nt", w=("cols", 2 * sw // N_CHIPS), M=M, N=sw, K=2 * sw)
    g["w_glu"] = matmul(n("ssm_glu_dw"), sv["yg"], dz, "tn", M=sw, N=2 * sw, K=M, out_dtype=BF16, into=("cols", 2 * sw // N_CHIPS))

    def gelu_bwd(y, u, dy, d):
        _, vjp = jax.vjp(lambda yy, uu, dd: _gelu(yy + dd * uu), y, u, d)
        return vjp(dy)

    dylin, du1, g["d_skip"] = rowwise(n("ssm_gelu_bwd"), gelu_bwd, [(sv["ylin"], sw, 0), (proj, sw, dm.o_u), (dyg, sw, 0)], [lw["d_skip"]],
                                      [(sw, BF16), (sw, F32)], [sw], M=M)
    seed = matmul(n("ssm_y_dx"), dylin, sp["ccat"], "nt", M=M, N=2 * dm.C, K=sw)
    d_ccat = matmul(n("ssm_y_dw"), sv["hcat"], dylin, "tn", M=2 * dm.C, N=sw, K=M)
    lam, d_acat = ssm_scan(n("ssm_scan_bwd"), seed, sp["acat_conj"], M=M, reverse=True, hcat=sv["hcat"])
    du = matmul(n("ssm_bu_dx"), lam, sp["bcat"], "nt", M=M, N=sw, K=2 * dm.C, add=du1, out_dtype=BF16)
    d_bcat = matmul(n("ssm_bu_dw"), proj, lam, "tn", M=sw, N=2 * dm.C, K=M, a_off=(0, dm.o_u))
    g["ssm_raw"] = (d_acat, d_bcat, d_ccat)
    nw = dm.H * MLA_NOPE
    dqn, dkn, dv_, dqp, dkp = attention_bwd(n("mla_bwd"), sv["q_bf"], 0, sv["kv"], 0, sv["kv"], nw, sv["o_mla"], do_mla, sv["lse_mla"],
                                            da=MLA_NOPE, dv=MLA_V, pairs=dm.H // 2, scale=(MLA_NOPE + MLA_ROPE) ** -0.5, M=M,
                                            qb=sv["q_bf"], qb_off=nw, kb=sv["kpe"])
    cosq, sinq, cosk, sink = tabs[:4]
    pw = dm.H * MLA_ROPE
    dqp_u, dkr = rowwise(n("rope_bwd"), lambda a, b, cq, sq, ck, sk: (_rope_t(a, cq, sq), _rope_t(b, ck, sk)),
                         [(dqp, pw, 0), (dkp, LANES, 0), (cosq, pw, 0), (sinq, pw, 0), (cosk, LANES, 0), (sink, LANES, 0)], [],
                         [(pw, BF16), (LANES, BF16)], M=M)
    dq = lane_concat(n("dq_cat"), [dqn, dqp_u], M=M)
    dkv = lane_concat(n("dkv_cat"), [dkn, dv_], M=M)
    dcqn = matmul(n("q_up_dx"), dq, lw["w_uq"], "nt", M=M, N=dm.QL, K=dm.QW)
    g["w_uq"] = matmul(n("q_up_dw"), sv["cqn"], dq, "tn", M=dm.QL, N=dm.QW, K=M, out_dtype=BF16)
    dckvn = matmul(n("kv_up_dx"), dkv, lw["w_ukv"], "nt", M=M, N=dm.KVL, K=dm.KVW)
    g["w_ukv"] = matmul(n("kv_up_dw"), sv["ckvn"], dkv, "tn", M=dm.KVL, N=dm.KVW, K=M, out_dtype=BF16)
    dcq, g["g_q"] = rms_bwd(n("rms_q_bwd"), proj, dm.QL, dm.o_cq, lw["g_q"], dcqn, M=M, out_dtype=BF16)
    dckv, g["g_kv"] = rms_bwd(n("rms_kv_bwd"), proj, dm.KVL, dm.o_ckv, lw["g_kv"], dckvn, M=M, out_dtype=BF16)
    dproj = hook_b(lane_concat(n("dproj_cat"), [dcq, du, dqd, dkd, dvd, dckv, dkr], M=M, pad_to=dm.PW), g)
    dh1 = matmul(n("in_proj_dx"), dproj, lw["w_in"], "nt", M=M, N=D, K=dm.PW)
    g["w_in"] = matmul(n("in_proj_dw"), sv["h1"], dproj, "tn", M=D, N=dm.PW, K=M, out_dtype=BF16)
    dx_in, g["g_mix"] = rms_bwd(n("rms_mix_bwd"), sv["x_in"], D, 0, lw["g_mix"], dh1, dx_mid, M=M)
    return dx_in, g


def layer_params(dm, small, l):
    lw = {k: small[k][l].reshape(1, -1) for k in ("g_mix", "g_q", "g_kv", "b_glu", "g_out_mla", "g_out_ssm", "g_out_dil", "g_ffn", "d_skip")}
    raw = _ssm_layer_params(dm, small["a_re"][l], small["a_im"][l], small["log_dt"][l], small["b_re"][l], small["b_im"][l])
    ar, ai, bbr, bbi = ssm_param_fwd(f"ssm_param_l{l}", *raw)
    g_ = dm.G
    bd = lambda t: _block_diag(jnp.transpose(t.reshape(SSM_GROUP, g_, SSM_STATE), (1, 0, 2)), g_)
    cd = lambda t: _block_diag(jnp.transpose(t, (0, 2, 1)), g_)
    cre, cim = cd(small["c_re"][l]), cd(small["c_im"][l])
    sp = {"acat": _cat_cols(ar, ai), "acat_conj": _cat_cols(ar, -ai),
          "bcat": _cat_cols(bd(bbr), bd(bbi)).astype(BF16),
          "ccat": _cat_cols(cre.T, -cim.T).T.astype(BF16)}
    return lw, sp, raw


def ssm_param_grads(dm, l, g, raw):
    d_acat, d_bcat, d_ccat = g.pop("ssm_raw")
    d_ar, d_ai = _uncat_cols(d_acat)
    dbr, dbi = _uncat_cols(d_bcat)
    g_ = dm.G
    to_rows = lambda t: jnp.transpose(_diag_blocks(t, g_), (1, 0, 2)).reshape(SSM_GROUP, dm.C)
    da_re, da_im, dldt, db_re, db_im = ssm_param_bwd(f"ssm_param_bwd_l{l}", *raw, d_ar, d_ai, to_rows(dbr), to_rows(dbi))
    dcr, dci = _uncat_cols(d_ccat.T)
    g["a_re"], g["a_im"] = da_re.reshape(g_, SSM_STATE), da_im.reshape(g_, SSM_STATE)
    g["log_dt"] = jnp.sum(dldt.reshape(g_, SSM_STATE), axis=1)
    from_rows = lambda t: jnp.transpose(t.reshape(SSM_GROUP, g_, SSM_STATE), (1, 2, 0))
    g["b_re"], g["b_im"] = from_rows(db_re), from_rows(db_im)
    g["c_re"] = jnp.transpose(_diag_blocks(dcr.T, g_), (0, 2, 1))
    g["c_im"] = -jnp.transpose(_diag_blocks(dci.T, g_), (0, 2, 1))
    g["d_skip"] = g["d_skip"].reshape(g_, SSM_GROUP)


def loss_and_grad(dm, h, target, g_final):
    D = dm.D

    def loss_fn(xb, tb, gb):
        y, vjp = jax.vjp(_rms, xb, gb)
        err = y - tb
        dxb, dg = vjp(err * (1.0 / D))
        part = 0.5 * jnp.sum(jnp.mean(err * err, axis=-1, keepdims=True), axis=0, keepdims=True)
        lane = lax.broadcasted_iota(jnp.int32, (1, LANES), 1)
        return dxb, dg, jnp.where(lane == 0, part, 0.0)

    return rowwise("loss", loss_fn, [(h, D, 0), (target, D, 0)], [g_final.reshape(1, D)], [(D, F32)], [D, LANES], M=dm.M)


KIND = {"w_in": "cols", "w_uq": "cols", "w_ukv": "cols", "w_glu": "cols", "w_o": "rows", "w_gate": "cols", "w_up": "cols", "w_down": "rows"}
SHARDED = list(KIND)
GATHER_GROUPS = {"mixer": ["w_in", "w_uq", "w_ukv", "w_glu"], "rest": ["w_o", "w_gate", "w_up", "w_down"]}
REDUCE_GROUPS = {"ffn": ["w_gate", "w_up", "w_down"], "others": ["w_o", "w_in", "w_uq", "w_ukv", "w_glu"]}
SMALL = ["g_mix", "g_q", "g_kv", "a_re", "a_im", "b_re", "b_im", "c_re", "c_im", "d_skip", "log_dt", "b_glu",
         "g_out_mla", "g_out_ssm", "g_out_dil", "g_ffn", "g_final"]
ORDER = ["g_mix", "w_in", "g_q", "w_uq", "g_kv", "w_ukv", "a_re", "a_im", "b_re", "b_im", "c_re", "c_im", "d_skip", "log_dt",
         "w_glu", "b_glu", "g_out_mla", "g_out_ssm", "g_out_dil", "w_o", "g_ffn", "w_gate", "w_up", "w_down", "g_final"]


def kernel(x, g_mix, w_in, g_q, w_uq, g_kv, w_ukv, a_re, a_im, b_re, b_im, c_re, c_im, d_skip, log_dt, w_glu, b_glu, g_out_mla, g_out_ssm, g_out_dil, w_o, g_ffn, w_gate, w_up, w_down, g_final, loss_target, m_g_mix, m_w_in, m_g_q, m_w_uq, m_g_kv, m_w_ukv, m_a_re, m_a_im, m_b_re, m_b_im, m_c_re, m_c_im, m_d_skip, m_log_dt, m_w_glu, m_b_glu, m_g_out_mla, m_g_out_ssm, m_g_out_dil, m_w_o, m_g_ffn, m_w_gate, m_w_up, m_w_down, m_g_final, v_g_mix, v_w_in, v_g_q, v_w_uq, v_g_kv, v_w_ukv, v_a_re, v_a_im, v_b_re, v_b_im, v_c_re, v_c_im, v_d_skip, v_log_dt, v_w_glu, v_b_glu, v_g_out_mla, v_g_out_ssm, v_g_out_dil, v_w_o, v_g_ffn, v_w_gate, v_w_up, v_w_down, v_g_final):
    args = locals()
    w = {k: args[k] for k in ORDER}
    mom = {k: args["m_" + k] for k in ORDER}
    var = {k: args["v_" + k] for k in ORDER}
    dm = Dims(x, g_q, g_kv, g_out_mla, g_out_ssm, g_out_dil, w_gate.shape[-1] * N_CHIPS)
    L = g_mix.shape[0]

    small = {k: w[k] for k in SMALL}
    na = len(SHARDED)
    tabs = _rope_tables(dm) + (dilated_bias(dm.M),)
    sel ={"w_in": selection_matrices(dm.src_in, w_in.shape[-1]), "w_uq": selection_matrices(dm.src_q, w_uq.shape[-1]),
           "w_ukv": selection_matrices(dm.src_kv, w_ukv.shape[-1])}

    G = [{k: cast_into_slot(f"cast_{k}_l{l}", w[k], KIND[k], l) for k in SHARDED} for l in range(L)]
    sems = {}

    def gather_stage(stage, plan):
        def start(l, grp, car):
            names = GATHER_GROUPS[grp]
            sems[stage, l, grp], bufs, _, car = split_start(f"ag_{stage}_start_{grp}_l{l}", plan, 3 * len(names), [G[l][k] for k in names], [], car)
            G[l].update(zip(names, bufs))
            return car

        def wait(l, grp, car):
            names = GATHER_GROUPS[grp]
            bufs, car = split_wait(f"ag_{stage}_wait_{grp}_l{l}", plan, sems[stage, l, grp], [G[l][k] for k in names], car)
            G[l].update(zip(names, bufs))
            return car

        return start, wait

    ici_start, ici_wait = gather_stage("ici", plan_gather_ici)
    pass_start, pass_wait = gather_stage("pass", plan_gather_pass)

    car = tabs[0]
    for l, grp in ((0, "mixer"), (0, "rest"), (1, "mixer")):
        if l < L:
            car = ici_start(l, grp, car)
    tabs = (pass_wait(0, "mixer", pass_start(0, "mixer", ici_wait(0, "mixer", car))),) + tabs[1:]
    h = x.reshape(dm.M, dm.D)
    lws, sps, raws, saved = [], [], [], []
    for l in range(L):
        lw, sp, raw = layer_params(dm, small, l)
        for k in GATHER_GROUPS["mixer"]:
            lw[k] = regroup_cols(f"regroup_{k}_l{l}", G[l][k], sel[k]) if k in sel else G[l][k]

        def at_q(car, l=l):
            car = pass_start(l, "rest", ici_wait(l, "rest", car))
            if l + 1 < L:
                car = pass_start(l + 1, "mixer", ici_wait(l + 1, "mixer", car))
            return car

        def at_mid(car, lw, l=l):
            car = pass_wait(l, "rest", car)
            lw.update({k: G[l][k] for k in GATHER_GROUPS["rest"]})
            if l + 1 < L:
                car = ici_start(l + 1, "rest", pass_wait(l + 1, "mixer", car))
            if l + 2 < L:
                car = ici_start(l + 2, "mixer", car)
            return car

        h, sv = layer_forward(dm, l, h, lw, sp, tabs, at_q, at_mid)
        lws.append(lw)
        sps.append(sp)
        raws.append(raw)
        saved.append(sv)
    dx, g_final_part, loss_part = loss_and_grad(dm, h, loss_target.reshape(dm.M, dm.D), w["g_final"])

    def reduce_begin(l, grp, g, car):
        names = REDUCE_GROUPS[grp]
        parts = [g.pop(k) for k in names]
        fresh = [jax.ShapeDtypeStruct(p.shape[1:], BF16) for p in parts]
        sm, parts, gots, car = split_start(f"rs_pair_start_{grp}_l{l}", plan_pair(len(names)), len(names), parts, fresh, car)
        return {"l": l, "grp": grp, "names": names, "sems": sm, "parts": parts, "gots": gots}, car

    def reduce_chips(st, car):
        names, tag, n = st["names"], f"{st['grp']}_l{st['l']}", len(st["names"])
        bufs, car = split_wait(f"rs_pair_wait_{tag}", plan_pair(n), st["sems"], st["parts"] + st["gots"], car)
        sums = [pair_sum(f"rs_pairsum_{a}_l{st['l']}", bufs[i], bufs[n + i]) for i, a in enumerate(names)]
        fresh = [jax.ShapeDtypeStruct((3,) + s.shape[1:], BF16) for s, _ in sums]
        st["sems"], st["s"], st["arrived"], car = split_start(f"rs_chips_start_{tag}", plan_chips(n), 3 * n, [s for s, _ in sums], fresh, car)
        st["own"] = [o for _, o in sums]
        return car

    def reduce_share(st, car):
        names, tag, n = st["names"], f"{st['grp']}_l{st['l']}", len(st["names"])
        bufs, car = split_wait(f"rs_chips_wait_{tag}", plan_chips(n), st["sems"], st["s"] + st["arrived"], car)
        mine = [chips_sum(f"rs_sum_{a}_l{st['l']}", st["own"][i], bufs[n + i]) for i, a in enumerate(names)]
        fresh = [jax.ShapeDtypeStruct(m_.shape, F32) for m_ in mine]
        st["sems"], st["mine"], st["theirs"], car = split_start(f"rs_share_start_{tag}", plan_share(n), n, mine, fresh, car)
        return car

    def reduce_end(st, car):
        n = len(st["names"])
        bufs, car = split_wait(f"rs_share_wait_{st['grp']}_l{st['l']}", plan_share(n), st["sems"], st["mine"] + st["theirs"], car)
        for i, k in enumerate(st["names"]):
            reduced[st["l"]][k] = (bufs[i], bufs[n + i])
        return car

    reduced, grads, older = [{} for _ in range(L)], [None] * L, None
    for l in reversed(range(L)):
        mine = {}

        def at_a(car, g, l=l, mine=mine, older=older):
            mine["st"], car = reduce_begin(l, "ffn", g, car)
            return car if older is None else reduce_chips(older, car)

        def at_m(car, g, mine=mine, older=older):
            car = reduce_chips(mine["st"], car)
            return car if older is None else reduce_share(older, car)

        def at_b(car, g, mine=mine, older=older):
            car = reduce_share(mine["st"], car)
            return car if older is None else reduce_end(older, car)

        dx, g = layer_backward(dm, l, dx, lws[l], sps[l], tabs, saved[l], at_a, at_m, at_b)
        ssm_param_grads(dm, l, g, raws[l])
        dx = reduce_end(mine["st"], dx)
        for k in sel:
            g[k] = ungroup_cols(f"ungroup_{k}_l{l}", g[k], sel[k])
        older, dx = reduce_begin(l, "others", g, dx)
        grads[l] = g
    dx = reduce_end(older, reduce_share(older, reduce_chips(older, dx)))

    gsum = {}
    small_names = [k for k in SMALL if k != "g_final"]
    pieces = [jnp.stack([grads[l][k] for l in range(L)]).reshape(-1) for k in small_names] + [g_final_part.reshape(-1), loss_part.reshape(-1)]
    sizes = [int(p.shape[0]) for p in pieces]
    total = sum(sizes)
    rows = -(-total // (LANES * 16)) * 16
    pack = lambda ps: jnp.concatenate(ps + [jnp.zeros((rows * LANES - total,), F32)]).reshape(rows, LANES)
    red = all_reduce_small(pack(pieces))
    flat = red.reshape(-1)
    offs = np.concatenate([[0], np.cumsum(sizes)]).astype(int)
    names = small_names + ["g_final"]
    for i, k in enumerate(names):
        gsum[k] = flat[offs[i]:offs[i + 1]].reshape(w[k].shape)
    loss = flat[offs[len(names)]]

    delta, new_m, new_v = {}, {}, {}
    for k in SHARDED:
        gsum[k], delta[k], new_m[k], new_v[k] = adamw_layers(f"adam_{k}", w[k], mom[k], var[k], [reduced[l][k][0] for l in range(L)],
                                                             [reduced[l][k][1] for l in range(L)], KIND[k])
    sm_sizes = sizes[:len(names)]
    sm_total = sum(sm_sizes)
    packs = lambda d: jnp.concatenate([d[k].reshape(-1) for k in names] + [jnp.zeros((rows * LANES - sm_total,), F32)]).reshape(rows, LANES)
    gs = jnp.concatenate([flat[:sm_total], jnp.zeros((rows * LANES - sm_total,), F32)]).reshape(rows, LANES)
    d_, m_, v_ = adamw("adam_small", packs(w), gs, packs(mom), packs(var))
    for i, k in enumerate(names):
        sl = slice(offs[i], offs[i + 1])
        delta[k], new_m[k], new_v[k] = (t.reshape(-1)[sl].reshape(w[k].shape) for t in (d_, m_, v_))

    return (loss, dx.reshape(x.shape), *[gsum[k] for k in ORDER], *[delta[k] for k in ORDER],
            *[new_m[k] for k in ORDER], *[new_v[k] for k in ORDER])
```

```python
import functools
import math

import numpy as np
import jax
import jax.numpy as jnp
from jax import lax
from jax.experimental import pallas as pl
from jax.experimental.pallas import tpu as pltpu

F32 = jnp.float32
BF16 = jnp.bfloat16
MESH = pl.DeviceIdType.MESH

NORM_EPS = 1e-6
MLA_NOPE, MLA_ROPE, MLA_V = 128, 64, 128
SSM_GROUP, SSM_STATE = 16, 64
DIL_HEAD = 64
DIL_PATTERNS = ((128, 1), (512, 4), (2048, 16))
ROPE_THETA = 10000.0
ADAM_LR, ADAM_B1, ADAM_B2, ADAM_EPS, ADAM_WD, ADAM_STEP = 0.001, 0.9, 0.999, 1e-08, 0.01, 10
N_CHIPS = 4

LANES = 128
SUBLANES_BF16 = 16
VMEM_LIMIT = 56 * 1024 * 1024
ROW_BUDGET = 20 * 1024 * 1024
MM_BUDGET = 40 * 1024 * 1024
NEG = -1e30


def _cparams(sem=None):
    return pltpu.CompilerParams(dimension_semantics=sem, vmem_limit_bytes=VMEM_LIMIT)


def _pick(n, cap, q, off=0):
    best = None
    for d in range(q, min(n, cap) + 1, q):
        if n % d == 0 and off % d == 0:
            best = d
    if best is None or (best * 4 <= min(cap, n) and n <= 3072 and off % n == 0):
        assert off % n == 0, (n, off)
        return n
    return best


_DOT_DIMS = {"nn": (((1,), (0,)), ((), ())), "nt": (((1,), (1,)), ((), ())), "tn": (((0,), (0,)), ((), ()))}


def _divs(n, q, within=None, off=0):
    return [d for d in range(q, n + 1, q) if n % d == 0 and off % d == 0 and (within is None or within % d == 0)] or [n]


def _mm_tiles(M, N, K, tms, tns, tks, ab, bb, ob):
    best = None
    for tk in tks:
        nk = K // tk
        for tn in tns:
            for tm in tms:
                if 2 * (tm * tk * ab + tk * tn * bb + tm * tn * ob) + tm * tn * 4 * (2 if nk > 1 else 1) > MM_BUDGET:
                    continue
                steps = (M // tm) * (N // tn) * nk
                hbm = M * K * ab * (1 if nk == 1 else N // tn) + K * N * bb * (M // tm) + M * N * ob
                cost = steps * 0.35e-6 + hbm / 3.0e12 + (nk - 1) * M * N * 12 / 4.0e12
                if best is None or cost < best[0]:
                    best = (cost, tm, tn, tk)
    assert best is not None, (M, N, K)
    return best[1:]


def matmul(name, a, b, mode, *, M, N, K, a_off=(0, 0), b_off=(0, 0), b_lead=None, w=None, add=None, out_dtype=F32, into=None):
    tn_mode = mode == "tn"
    a_ro, a_co = (a_off[1], a_off[0]) if tn_mode else a_off
    b_no, b_ko = b_off if mode == "nt" else (b_off[1], b_off[0])
    n_within = k_within = m_within = None
    if w is not None:
        kind, shard = w
        if kind == "cols":
            b = b.reshape(N_CHIPS, b.shape[1] * b.shape[2], b.shape[3])
        rows_within, cols_within = (K if mode == "nn" else N, shard) if kind == "cols" else (shard, (N if mode == "nn" else K) // 2)
        k_within, n_within = (rows_within, cols_within) if mode == "nn" else (cols_within, rows_within)
    if into is not None:
        m_within, n_within = (M // 2, into[1]) if into[0] == "cols" else (into[1], N // 2)
    tms = [d for d in _divs(M, 128 if tn_mode else SUBLANES_BF16, m_within, a_ro) if d <= 1408]
    tns = [d for d in _divs(N, LANES, n_within, b_no) if d <= 2048]
    tks = _divs(K, SUBLANES_BF16 if tn_mode else LANES, k_within, math.gcd(a_co, b_ko))
    ob = jnp.dtype(out_dtype).itemsize + (add.dtype.itemsize if add is not None else 0)
    tm, tn, tk = _mm_tiles(M, N, K, tms, tns, tks, a.dtype.itemsize, b.dtype.itemsize, ob)
    nk = K // tk
    dn = _DOT_DIMS[mode]

    if tn_mode:
        a_spec = pl.BlockSpec((tk, tm), lambda i, j, k: (k + a_co // tk, i + a_ro // tm))
    else:
        a_spec = pl.BlockSpec((tm, tk), lambda i, j, k: (i + a_ro // tm, k + a_co // tk))
    b_blk = (tn, tk) if mode == "nt" else (tk, tn)
    if w is not None:
        tr_, tc_ = (tk, tn) if mode == "nn" else (tn, tk)
        rper, cper = rows_within // tr_, cols_within // tc_

        def wmap(rb, cb):
            if kind == "cols":
                return (cb // cper, rb, cb % cper)
            return (rb // rper, cb // cper, rb % rper, cb % cper)

        imap = (lambda i, j, k: wmap(k, j)) if mode == "nn" else (lambda i, j, k: wmap(j, k))
        b_spec = pl.BlockSpec((None,) * (b.ndim - 2) + b_blk, imap)
    else:
        if mode == "nt":
            imap = lambda i, j, k: (j + b_no // tn, k + b_ko // tk)
        else:
            imap = lambda i, j, k: (k + b_ko // tk, j + b_no // tn)
        if b_lead is None:
            b_spec = pl.BlockSpec(b_blk, imap)
        else:
            b_spec = pl.BlockSpec((None,) + b_blk, lambda i, j, k: (b_lead,) + imap(i, j, k))
    o_plain = pl.BlockSpec((tm, tn), lambda i, j, k: (i, j))
    if into is None:
        o_spec, out_shape = o_plain, jax.ShapeDtypeStruct((M, N), out_dtype)
    else:
        rper, cper = m_within // tm, n_within // tn
        if into[0] == "cols":
            o_spec = pl.BlockSpec((None, None, tm, tn), lambda i, j, k: (i // rper, j // cper, i % rper, j % cper))
        else:
            o_spec = pl.BlockSpec((None, None, tm, tn), lambda i, j, k: (j // cper, i // rper, i % rper, j % cper))
        out_shape = jax.ShapeDtypeStruct((2, N_CHIPS, m_within, n_within), out_dtype)
    has_add = add is not None
    n_in = 2 + has_add

    def body(*refs):
        a_ref, b_ref = refs[0], refs[1]
        add_ref = refs[2] if has_add else None
        o_ref = refs[n_in]
        part = lax.dot_general(a_ref[...].astype(BF16), b_ref[...].astype(BF16), dn, preferred_element_type=F32)

        def finish(r):
            if has_add:
                r = r + add_ref[...].astype(F32)
            o_ref[...] = r.astype(o_ref.dtype)

        if nk == 1:
            finish(part)
        else:
            acc_ref = refs[-1]
            k = pl.program_id(2)

            @pl.when(k == 0)
            def _():
                acc_ref[...] = part

            @pl.when((k > 0) & (k < nk - 1))
            def _():
                acc_ref[...] += part

            @pl.when(k == nk - 1)
            def _():
                finish(acc_ref[...] + part)

    in_specs = [a_spec, b_spec] + ([o_plain] if has_add else [])
    args = (a, b) + ((add,) if has_add else ())
    return pl.pallas_call(
        body, name=name, out_shape=out_shape, grid=(M // tm, N // tn, nk), in_specs=in_specs, out_specs=o_spec,
        scratch_shapes=[pltpu.VMEM((tm, tn), F32)] if nk > 1 else [],
        compiler_params=_cparams(("parallel", "parallel", "arbitrary")),
    )(*args)


def selection_matrices(src_of_new, n_shard):
    src_np = np.asarray(src_of_new, np.int64)
    src = jnp.asarray(src_np.astype(np.int32))
    ref = jnp.arange(N_CHIPS, dtype=jnp.int32)[:, None] * n_shard + jnp.arange(n_shard, dtype=jnp.int32)[None, :]
    pm = (ref[:, :, None] == src[None, None, :]).astype(BF16)
    pmt = (src[None, :, None] == ref[:, None, :]).astype(BF16)
    tc = _pick(len(src_np), 512, LANES)
    feeds = [sorted({int(s) // n_shard for s in src_np[cb * tc:(cb + 1) * tc] if s >= 0}) for cb in range(len(src_np) // tc)]
    return pm, pmt, tc, feeds


def regroup_cols(name, g, sel):
    pm, _, tc, feeds = sel
    nn = g.shape[-1]
    g = g.reshape(N_CHIPS, -1, nn)
    K, n_new = g.shape[1], pm.shape[-1]
    tm = _pick(K, 512, SUBLANES_BF16)

    def body(g_ref, pm_ref, o_ref):
        for cb, chips in enumerate(feeds):
            @pl.when(pl.program_id(0) == cb)
            def _(chips=chips):
                acc = jnp.zeros((tm, tc), F32)
                for j in chips:
                    acc = acc + _dot(g_ref[j], pm_ref[j], "nn")
                o_ref[...] = acc.astype(o_ref.dtype)

    return pl.pallas_call(
        body, name=name, out_shape=jax.ShapeDtypeStruct((K, n_new), BF16), grid=(n_new // tc, K // tm),
        in_specs=[pl.BlockSpec((N_CHIPS, tm, nn), lambda c, i: (0, i, 0)), pl.BlockSpec((N_CHIPS, nn, tc), lambda c, i: (0, 0, c))],
        out_specs=pl.BlockSpec((tm, tc), lambda c, i: (i, c)), compiler_params=_cparams(("parallel", "parallel")),
    )(g, pm)


def ungroup_cols(name, dw, sel):
    _, pmt, tc, feeds = sel
    K, n_new = dw.shape
    nn = pmt.shape[-1]
    kh = K // 2
    tm = _pick(kh, 512, SUBLANES_BF16)
    hb = kh // tm
    fed_by = [[cb for cb, chips in enumerate(feeds) if j in chips] for j in range(N_CHIPS)]

    def body(dw_ref, pmt_ref, o_ref):
        for j, blocks in enumerate(fed_by):
            @pl.when(pl.program_id(0) == j)
            def _(blocks=blocks):
                acc = jnp.zeros((tm, nn), F32)
                for cb in blocks:
                    acc = acc + _dot(dw_ref[:, cb * tc:(cb + 1) * tc], pmt_ref[cb * tc:(cb + 1) * tc, :], "nn")
                o_ref[...] = acc.astype(o_ref.dtype)

    return pl.pallas_call(
        body, name=name, out_shape=jax.ShapeDtypeStruct((2, N_CHIPS, kh, nn), BF16), grid=(N_CHIPS, 2 * hb),
        in_specs=[pl.BlockSpec((tm, n_new), lambda j, i: (i, 0)), pl.BlockSpec((None, n_new, nn), lambda j, i: (j, 0, 0))],
        out_specs=pl.BlockSpec((None, None, tm, nn), lambda j, i: (i // hb, j, i % hb, 0)),
        compiler_params=_cparams(("parallel", "parallel")),
    )(dw, pmt)


def rowwise(name, fn, rows, vecs, outs, sums=(), *, M):
    rows = [tuple(r) + (0,) * (4 - len(r)) for r in rows]
    nr, nv, no, ns = len(rows), len(vecs), len(outs), len(sums)
    per_row = sum(w * a.dtype.itemsize for a, w, _, _ in rows) + sum(w * jnp.dtype(d).itemsize for w, d in outs)
    tr = _pick(M, max(8, min(512, ROW_BUDGET // (2 * per_row))), 16 if M % 16 == 0 else 8)

    def body(*refs):
        i = pl.program_id(0)
        res = fn(*[r[...] for r in refs[:nr + nv]])
        o_refs = refs[nr + nv:nr + nv + no]
        s_refs = refs[nr + nv + no:]
        for ref, val in zip(o_refs, res[:no]):
            ref[...] = val.astype(ref.dtype)
        if ns:
            @pl.when(i == 0)
            def _():
                for ref in s_refs:
                    ref[...] = jnp.zeros(ref.shape, F32)

            for ref, val in zip(s_refs, res[no:]):
                ref[...] += val

    in_specs = [pl.BlockSpec((tr, w), functools.partial(lambda i, cb, rb: (i + rb, cb), cb=off // w, rb=roff // tr)) for _, w, off, roff in rows]
    for _, w, off, roff in rows:
        assert off % w == 0 and roff % tr == 0
    in_specs += [pl.BlockSpec(v.shape, functools.partial(lambda i, nd: (0,) * nd, nd=v.ndim)) for v in vecs]
    out_specs = [pl.BlockSpec((tr, w), lambda i: (i, 0)) for w, _ in outs]
    out_specs += [pl.BlockSpec((1, w), lambda i: (0, 0)) for w in sums]
    out_shape = [jax.ShapeDtypeStruct((M, w), d) for w, d in outs] + [jax.ShapeDtypeStruct((1, w), F32) for w in sums]
    return pl.pallas_call(
        body, name=name, out_shape=out_shape, grid=(M // tr,), in_specs=in_specs, out_specs=out_specs,
        compiler_params=_cparams(("arbitrary",) if ns else ("parallel",)),
    )(*[r[0] for r in rows], *vecs)


def _rms(x, g):
    xf = x.astype(F32)
    return xf * lax.rsqrt(jnp.mean(xf * xf, axis=-1, keepdims=True) + NORM_EPS) * g


def _gelu(y):
    return 0.5 * y * (1.0 + jnp.tanh(math.sqrt(2.0 / math.pi) * (y + 0.044715 * (y * y * y))))


def _colsum(v):
    return jnp.sum(v, axis=0, keepdims=True)


def rms_fwd(name, x, width, off, g, *, M):
    return rowwise(name, lambda xb, gb: (_rms(xb, gb),), [(x, width, off)], [g], [(width, BF16)], M=M)[0]


def rms_bwd(name, x, width, off, g, dy, resid=None, *, M, out_dtype=F32):
    def fn(xb, dyb, *rest):
        gb = rest[-1]
        _, vjp = jax.vjp(_rms, xb.astype(F32), gb)
        dx, dg = vjp(dyb.astype(F32))
        if resid is not None:
            dx = dx + rest[0]
        return dx, dg

    rows = [(x, width, off), (dy, width, 0)] + ([(resid, width, 0)] if resid is not None else [])
    return rowwise(name, fn, rows, [g], [(width, out_dtype)], [width], M=M)


def lane_concat(name, parts, *, M, pad_to=None):
    width = sum(p.shape[1] for p in parts)
    pad = 0 if pad_to is None else pad_to - width

    def fn(*blocks):
        cols = [b.astype(BF16) for b in blocks]
        if pad:
            cols.append(jnp.zeros((blocks[0].shape[0], pad), BF16))
        return (jnp.concatenate(cols, axis=1),)

    return rowwise(name, fn, [(p, p.shape[1], 0) for p in parts], [], [(width + pad, BF16)], M=M)[0]


def _swap_halves(x, half):
    w = x.shape[-1]
    lane = lax.broadcasted_iota(jnp.int32, x.shape, x.ndim - 1)
    first = (lane % (2 * half)) < half
    return jnp.where(first, pltpu.roll(x, w - half, x.ndim - 1), pltpu.roll(x, half, x.ndim - 1))


def dilated_bias(M):
    delta = jnp.arange(M, dtype=jnp.int32)[:, None] - jnp.arange(M, dtype=jnp.int32)[None, :]
    w = jnp.zeros(delta.shape, F32)
    for window, dil in DIL_PATTERNS:
        ok = (delta >= 0) & (delta <= window)
        if dil > 1:
            ok = ok & ((delta & (dil - 1)) == 0)
        w = w + ok.astype(F32)
    return jnp.where(w > 0, jnp.log(jnp.maximum(w, 1.0)), NEG)


def _dot(a, b, mode):
    return lax.dot_general(a, b, _DOT_DIMS[mode], preferred_element_type=F32)


def attention_fwd(name, qa, qa_off, ka, ka_off, v, v_off, *, da, dv, pairs, scale, M, qb=None, qb_off=0, kb=None, bias=None):
    tq = min(256, M)
    tk = min(512, M)
    has_b = qb is not None
    has_bias = bias is not None
    dr = MLA_ROPE

    def body(*refs):
        refs = list(refs)
        bias_ref = refs.pop(3) if has_bias else None
        if has_b:
            qa_ref, ka_ref, v_ref, qb_ref, kb_ref, o_ref, lse_ref = refs
        else:
            qa_ref, ka_ref, v_ref, o_ref, lse_ref = refs
        i = pl.program_id(1)
        t0 = i * tq
        nkb = (t0 + tq + tk - 1) // tk
        n_full = nkb if has_bias else t0 // tk
        q1s = [qa_ref[:, hh * da:(hh + 1) * da].astype(BF16) for hh in range(2)]
        q2s = [qb_ref[:, hh * dr:(hh + 1) * dr].astype(BF16) if has_b else None for hh in range(2)]

        def step(kbi, carry, masked):
            ks = pl.multiple_of(kbi * tk, tk)
            k2 = kb_ref[pl.ds(ks, tk), 0:dr].astype(BF16) if has_b else None
            if has_bias:
                extra = bias_ref[:, pl.ds(ks, tk)]
            elif masked:
                delta = (t0 + lax.broadcasted_iota(jnp.int32, (tq, tk), 0)) - (ks + lax.broadcasted_iota(jnp.int32, (tq, tk), 1))
            new = []
            for hh, (m, l, acc) in enumerate(carry):
                k1 = ka_ref[pl.ds(ks, tk), hh * da:(hh + 1) * da].astype(BF16)
                s = _dot(q1s[hh], k1, "nt")
                if has_b:
                    s = s + _dot(q2s[hh], k2, "nt")
                s = s * scale
                if has_bias:
                    s = s + extra
                elif masked:
                    s = jnp.where(delta >= 0, s, NEG)
                m_new = jnp.maximum(m, jnp.max(s, axis=1, keepdims=True))
                alpha = jnp.exp(m - m_new)
                p = jnp.exp(s - m_new)
                l = alpha * l + jnp.sum(p, axis=1, keepdims=True)
                vv = v_ref[pl.ds(ks, tk), hh * dv:(hh + 1) * dv].astype(BF16)
                acc = alpha * acc + _dot(p.astype(BF16), vv, "nn")
                new.append((m_new, l, acc))
            return tuple(new)

        carry = tuple((jnp.full((tq, 1), NEG, F32), jnp.zeros((tq, 1), F32), jnp.zeros((tq, dv), F32)) for _ in range(2))
        carry = lax.fori_loop(0, n_full, functools.partial(step, masked=False), carry)
        carry = lax.fori_loop(n_full, nkb, functools.partial(step, masked=True), carry)
        o_parts = [acc / l for _, l, acc in carry]
        lse_parts = [m + jnp.log(l) for m, l, _ in carry]
        o_ref[...] = jnp.concatenate(o_parts, axis=1)
        lane = lax.broadcasted_iota(jnp.int32, (tq, LANES), 1)
        lse_ref[...] = jnp.where(lane == 0, lse_parts[0], jnp.where(lane == 1, lse_parts[1], 0.0))

    assert qa_off % (2 * da) == 0 and ka_off % (2 * da) == 0 and v_off % (2 * dv) == 0
    in_specs = [
        pl.BlockSpec((tq, 2 * da), lambda hp, i: (i, qa_off // (2 * da) + hp)),
        pl.BlockSpec((M, 2 * da), lambda hp, i: (0, ka_off // (2 * da) + hp)),
        pl.BlockSpec((M, 2 * dv), lambda hp, i: (0, v_off // (2 * dv) + hp)),
    ]
    args = [qa, ka, v]
    if has_bias:
        in_specs.append(pl.BlockSpec((tq, M), lambda hp, i: (i, 0)))
        args.append(bias)
    if has_b:
        assert qb_off % LANES == 0
        in_specs += [pl.BlockSpec((tq, LANES), lambda hp, i: (i, qb_off // LANES + hp)),
                     pl.BlockSpec((M, LANES), lambda hp, i: (0, 0))]
        args += [qb, kb]
    out_specs = [pl.BlockSpec((tq, 2 * dv), lambda hp, i: (i, hp)),
                 pl.BlockSpec((None, tq, LANES), lambda hp, i: (hp, i, 0))]
    out_shape = [jax.ShapeDtypeStruct((M, pairs * 2 * dv), F32), jax.ShapeDtypeStruct((pairs, M, LANES), F32)]
    return pl.pallas_call(
        body, name=name, out_shape=out_shape, grid=(pairs, M // tq), in_specs=in_specs, out_specs=out_specs,
        compiler_params=_cparams(("parallel", "arbitrary")),
    )(*args)


def attention_bwd(name, qa, qa_off, ka, ka_off, v, v_off, o, do, lse, *, da, dv, pairs, scale, M,
                  qb=None, qb_off=0, kb=None, bias=None):
    tq = min(256, M)
    tk = min(256, M)
    has_b = qb is not None
    has_bias = bias is not None
    dr = MLA_ROPE

    def body(*refs):
        refs = list(refs)
        bias_ref = refs.pop(6) if has_bias else None
        if has_b:
            qa_ref, ka_ref, v_ref, o_ref, do_ref, lse_ref, qb_ref, kb_ref, dqa_ref, dka_ref, dv_ref, dqb_ref, dkb_ref = refs
        else:
            qa_ref, ka_ref, v_ref, o_ref, do_ref, lse_ref, dqa_ref, dka_ref, dv_ref = refs
        hp = pl.program_id(0)
        i = pl.program_id(1)
        t0 = i * tq
        nkb = (t0 + tq + tk - 1) // tk
        n_full = nkb if has_bias else t0 // tk

        @pl.when(i == 0)
        def _():
            dka_ref[...] = jnp.zeros(dka_ref.shape, F32)
            dv_ref[...] = jnp.zeros(dv_ref.shape, F32)

        if has_b:
            @pl.when((i == 0) & (hp == 0))
            def _():
                dkb_ref[...] = jnp.zeros(dkb_ref.shape, F32)

        q1s = [qa_ref[:, hh * da:(hh + 1) * da].astype(BF16) for hh in range(2)]
        q2s = [qb_ref[:, hh * dr:(hh + 1) * dr].astype(BF16) if has_b else None for hh in range(2)]
        do_bfs = [do_ref[:, hh * dv:(hh + 1) * dv].astype(BF16) for hh in range(2)]
        rowdots = [jnp.sum(do_ref[:, hh * dv:(hh + 1) * dv] * o_ref[:, hh * dv:(hh + 1) * dv], axis=1, keepdims=True) for hh in range(2)]
        lses = [lse_ref[:, hh:hh + 1] for hh in range(2)]

        def step(kbi, carry, masked):
            ks = pl.multiple_of(kbi * tk, tk)
            k2 = kb_ref[pl.ds(ks, tk), 0:dr].astype(BF16) if has_b else None
            if has_bias:
                extra = bias_ref[:, pl.ds(ks, tk)]
            elif masked:
                delta = (t0 + lax.broadcasted_iota(jnp.int32, (tq, tk), 0)) - (ks + lax.broadcasted_iota(jnp.int32, (tq, tk), 1))
            new, dkb_part = [], None
            for hh, (dq1, dq2) in enumerate(carry):
                k1 = ka_ref[pl.ds(ks, tk), hh * da:(hh + 1) * da].astype(BF16)
                s = _dot(q1s[hh], k1, "nt")
                if has_b:
                    s = s + _dot(q2s[hh], k2, "nt")
                s = s * scale
                if has_bias:
                    s = s + extra
                elif masked:
                    s = jnp.where(delta >= 0, s, NEG)
                p = jnp.exp(s - lses[hh])
                vv = v_ref[pl.ds(ks, tk), hh * dv:(hh + 1) * dv].astype(BF16)
                dp = _dot(do_bfs[hh], vv, "nt")
                ds = (p * (dp - rowdots[hh]) * scale).astype(BF16)
                dq1 = dq1 + _dot(ds, k1, "nn")
                dka_ref[pl.ds(ks, tk), hh * da:(hh + 1) * da] += _dot(ds, q1s[hh], "tn")
                dv_ref[pl.ds(ks, tk), hh * dv:(hh + 1) * dv] += _dot(p.astype(BF16), do_bfs[hh], "tn")
                if has_b:
                    dq2 = dq2 + _dot(ds, k2, "nn")
                    part = _dot(ds, q2s[hh], "tn")
                    dkb_part = part if dkb_part is None else dkb_part + part
                new.append((dq1, dq2))
            if has_b:
                dkb_ref[pl.ds(ks, tk), 0:dr] += dkb_part
            return tuple(new)

        carry = tuple((jnp.zeros((tq, da), F32), jnp.zeros((tq, dr), F32)) for _ in range(2))
        carry = lax.fori_loop(0, n_full, functools.partial(step, masked=False), carry)
        carry = lax.fori_loop(n_full, nkb, functools.partial(step, masked=True), carry)
        dqa_ref[...] = jnp.concatenate([c[0] for c in carry], axis=1).astype(dqa_ref.dtype)
        if has_b:
            dqb_ref[...] = jnp.concatenate([c[1] for c in carry], axis=1).astype(dqb_ref.dtype)

    in_specs = [
        pl.BlockSpec((tq, 2 * da), lambda hp, i: (i, qa_off // (2 * da) + hp)),
        pl.BlockSpec((M, 2 * da), lambda hp, i: (0, ka_off // (2 * da) + hp)),
        pl.BlockSpec((M, 2 * dv), lambda hp, i: (0, v_off // (2 * dv) + hp)),
        pl.BlockSpec((tq, 2 * dv), lambda hp, i: (i, hp)),
        pl.BlockSpec((tq, 2 * dv), lambda hp, i: (i, hp)),
        pl.BlockSpec((None, tq, LANES), lambda hp, i: (hp, i, 0)),
    ]
    args = [qa, ka, v, o, do, lse]
    if has_bias:
        in_specs.append(pl.BlockSpec((tq, M), lambda hp, i: (i, 0)))
        args.append(bias)
    out_specs = [pl.BlockSpec((tq, 2 * da), lambda hp, i: (i, hp)),
                 pl.BlockSpec((M, 2 * da), lambda hp, i: (0, hp)),
                 pl.BlockSpec((M, 2 * dv), lambda hp, i: (0, hp))]
    out_shape = [jax.ShapeDtypeStruct((M, pairs * 2 * da), BF16),
                 jax.ShapeDtypeStruct((M, pairs * 2 * da), F32),
                 jax.ShapeDtypeStruct((M, pairs * 2 * dv), F32)]
    if has_b:
        in_specs += [pl.BlockSpec((tq, LANES), lambda hp, i: (i, qb_off // LANES + hp)),
                     pl.BlockSpec((M, LANES), lambda hp, i: (0, 0))]
        args += [qb, kb]
        out_specs += [pl.BlockSpec((tq, LANES), lambda hp, i: (i, hp)), pl.BlockSpec((M, LANES), lambda hp, i: (0, 0))]
        out_shape += [jax.ShapeDtypeStruct((M, pairs * LANES), F32), jax.ShapeDtypeStruct((M, LANES), F32)]
    return pl.pallas_call(
        body, name=name, out_shape=out_shape, grid=(pairs, M // tq), in_specs=in_specs, out_specs=out_specs,
        compiler_params=_cparams(("arbitrary", "arbitrary")),
    )(*args)


def ssm_scan(name, xcat, acat, *, M, reverse=False, hcat=None):
    C2 = xcat.shape[1]
    cb = LANES
    tb = min(128, M)
    nblk = M // tb
    with_da = hcat is not None

    def body(*refs):
        if with_da:
            x_ref, a_ref, h_ref, o_ref, da_ref, p_ref = refs
        else:
            x_ref, a_ref, o_ref, p_ref = refs
        ar, ai = a_ref[:, :cb], a_ref[:, cb:]
        row = lax.broadcasted_iota(jnp.int32, (tb, cb), 0)

        def logscan(xr, xi):
            pr, pi = ar, ai
            d = 1
            while d < tb:
                shift = tb - d if reverse else d
                keep = (row < tb - d) if reverse else (row >= d)
                sr = jnp.where(keep, pltpu.roll(xr, shift, 0), 0.0)
                si = jnp.where(keep, pltpu.roll(xi, shift, 0), 0.0)
                xr, xi = xr + pr * sr - pi * si, xi + pr * si + pi * sr
                pr, pi = pr * pr - pi * pi, 2.0 * pr * pi
                d *= 2
            return xr, xi

        seed = row == (tb - 1 if reverse else 0)
        p0r, p0i = logscan(jnp.where(seed, ar, 0.0), jnp.where(seed, ai, 0.0))
        p_ref[:, :cb] = p0r
        p_ref[:, cb:] = p0i
        sub = lax.broadcasted_iota(jnp.int32, (8, cb), 0)
        edge = 0 if reverse else tb - 8
        pick = sub == (0 if reverse else 7)

        def blk(b, carry):
            cr, ci = carry
            bb = (nblk - 1 - b) if reverse else b
            t0 = pl.multiple_of(bb * tb, tb)
            hr, hi = logscan(x_ref[pl.ds(t0, tb), :cb], x_ref[pl.ds(t0, tb), cb:])
            pr, pi = p_ref[:, :cb], p_ref[:, cb:]
            o_ref[pl.ds(t0, tb), :cb] = hr + pr * cr - pi * ci
            o_ref[pl.ds(t0, tb), cb:] = hi + pr * ci + pi * cr
            te = pl.multiple_of(t0 + edge, 8)
            ncr = jnp.sum(jnp.where(pick, o_ref[pl.ds(te, 8), :cb], 0.0), axis=0, keepdims=True)
            nci = jnp.sum(jnp.where(pick, o_ref[pl.ds(te, 8), cb:], 0.0), axis=0, keepdims=True)
            return ncr, nci

        lax.fori_loop(0, nblk, blk, (jnp.zeros((1, cb), F32), jnp.zeros((1, cb), F32)))
        if with_da:
            first = lax.broadcasted_iota(jnp.int32, (M, cb), 0) >= 1
            hpr = jnp.where(first, pltpu.roll(h_ref[:, :cb], 1, 0), 0.0)
            hpi = jnp.where(first, pltpu.roll(h_ref[:, cb:], 1, 0), 0.0)
            lr, li = o_ref[:, :cb], o_ref[:, cb:]
            da_ref[:, :cb] = _colsum(lr * hpr + li * hpi)
            da_ref[:, cb:] = _colsum(li * hpr - lr * hpi)

    blk_spec = pl.BlockSpec((M, 2 * cb), lambda j: (0, j))
    vec_spec = pl.BlockSpec((1, 2 * cb), lambda j: (0, j))
    in_specs = [blk_spec, vec_spec] + ([blk_spec] if with_da else [])
    out_specs = [blk_spec] + ([vec_spec] if with_da else [])
    out_shape = [jax.ShapeDtypeStruct((M, C2), F32)] + ([jax.ShapeDtypeStruct((1, C2), F32)] if with_da else [])
    args = [xcat, acat] + ([hcat] if with_da else [])
    res = pl.pallas_call(
        body, name=name, out_shape=out_shape, grid=(C2 // (2 * cb),), in_specs=in_specs, out_specs=out_specs,
        scratch_shapes=[pltpu.VMEM((tb, 2 * cb), F32)], compiler_params=_cparams(("parallel",)),
    )(*args)
    return res if with_da else res[0]


def _ssm_param_fn(a_re, a_im, ldt, b_re, b_im):
    lr, li = jnp.minimum(a_re, -1e-4), a_im
    dt = jnp.exp(ldt)
    e, ang = jnp.exp(lr * dt), li * dt
    ar, ai = e * jnp.cos(ang), e * jnp.sin(ang)
    den = lr * lr + li * li
    nr, ni = ar - 1.0, ai
    cr, ci = (nr * lr + ni * li) / den, (ni * lr - nr * li) / den
    return ar, ai, cr * b_re - ci * b_im, cr * b_im + ci * b_re


def _whole(shape):
    return pl.BlockSpec(shape, functools.partial(lambda nd: (0,) * nd, nd=len(shape)))


def ssm_param_fwd(name, a_re, a_im, ldt, b_re, b_im):
    def body(*refs):
        res = _ssm_param_fn(*[r[...] for r in refs[:5]])
        for ref, val in zip(refs[5:], res):
            ref[...] = val

    ins = [a_re, a_im, ldt, b_re, b_im]
    outs = [a_re, a_re, b_re, b_re]
    return pl.pallas_call(
        body, name=name, out_shape=[jax.ShapeDtypeStruct(t.shape, F32) for t in outs],
        in_specs=[_whole(t.shape) for t in ins], out_specs=[_whole(t.shape) for t in outs], compiler_params=_cparams(),
    )(*ins)


def ssm_param_bwd(name, a_re, a_im, ldt, b_re, b_im, d_ar, d_ai, d_bbr, d_bbi):
    def body(*refs):
        _, vjp = jax.vjp(_ssm_param_fn, *[r[...] for r in refs[:5]])
        res = vjp(tuple(r[...] for r in refs[5:9]))
        for ref, val in zip(refs[9:], res):
            ref[...] = val

    ins = [a_re, a_im, ldt, b_re, b_im, d_ar, d_ai, d_bbr, d_bbi]
    outs = [a_re, a_im, ldt, b_re, b_im]
    return pl.pallas_call(
        body, name=name, out_shape=[jax.ShapeDtypeStruct(t.shape, F32) for t in outs],
        in_specs=[_whole(t.shape) for t in ins], out_specs=[_whole(t.shape) for t in outs], compiler_params=_cparams(),
    )(*ins)


ANY = pl.BlockSpec(memory_space=pl.ANY)


def _place():
    x, y, c = lax.axis_index("x"), lax.axis_index("y"), lax.axis_index("c")
    chips = [(1 - x, y), (x, 1 - y), (1 - x, 1 - y)]
    return x, y, c, chips


def cast_into_slot(name, w, kind, l=None):
    K, nn = w.shape[-2:]
    hr, hc = (K // 2, nn) if kind == "cols" else (K, nn // 2)
    tr = _pick(hr, max(16, min(512, ROW_BUDGET // (2 * hc * 6))), SUBLANES_BF16)
    nb = hr // tr

    def body(w_ref, o_ref):
        o_ref[...] = w_ref[...].astype(BF16)

    lead = () if l is None else (l,)
    if kind == "cols":
        in_spec = pl.BlockSpec((None,) * len(lead) + (tr, hc), lambda h, i: lead + (h * nb + i, 0))
    else:
        in_spec = pl.BlockSpec((None,) * len(lead) + (tr, hc), lambda h, i: lead + (i, h))
    return pl.pallas_call(
        body, name=name, out_shape=jax.ShapeDtypeStruct((N_CHIPS, 2, hr, hc), BF16), grid=(2, nb), in_specs=[in_spec],
        out_specs=pl.BlockSpec((None, None, tr, hc), lambda h, i: (2 * lax.axis_index("x") + lax.axis_index("y"), h, i, 0)),
        compiler_params=_cparams(("parallel", "parallel")),
    )(w)


HBM_SPEC = pl.BlockSpec(memory_space=pltpu.HBM)
SEM_SPEC = pl.BlockSpec(memory_space=pltpu.SEMAPHORE)
SPLIT_PARAMS = pltpu.CompilerParams(has_side_effects=pltpu.SideEffectType.DATAFLOW_SIDE_EFFECTING)


def _in_hbm(t):
    return pltpu.with_memory_space_constraint(t, pltpu.HBM)


def split_start(name, plan, n, bufs, fresh, carrier):
    nb, nf = len(bufs), len(fresh)

    def body(*refs):
        outs = refs[nb + 1:]
        for i, (s, d, dev) in enumerate(plan(list(outs[2:2 + nb + nf]))):
            pltpu.make_async_remote_copy(src_ref=s, dst_ref=d, send_sem=outs[0].at[i], recv_sem=outs[1].at[i],
                                         device_id=dev, device_id_type=MESH).start()

    hbm = lambda t: pltpu.HBM(t.shape, t.dtype)
    res = pl.pallas_call(
        body, name=name,
        out_shape=(pltpu.SemaphoreType.DMA((n,)), pltpu.SemaphoreType.DMA((n,)), *[hbm(t) for t in bufs], *[hbm(t) for t in fresh], hbm(carrier)),
        in_specs=[HBM_SPEC] * (nb + 1), out_specs=(SEM_SPEC, SEM_SPEC) + (HBM_SPEC,) * (nb + nf + 1),
        input_output_aliases={**{i: 2 + i for i in range(nb)}, nb: 2 + nb + nf}, compiler_params=SPLIT_PARAMS,
    )(*[_in_hbm(t) for t in bufs], _in_hbm(carrier))
    return (res[0], res[1]), list(res[2:2 + nb]), list(res[2 + nb:2 + nb + nf]), res[2 + nb + nf]


def split_wait(name, plan, sems, bufs, carrier):
    nb = len(bufs)

    def body(*refs):
        for i, (s, d, dev) in enumerate(plan(list(refs[:nb]))):
            cp = pltpu.make_async_remote_copy(src_ref=s, dst_ref=d, send_sem=refs[nb].at[i], recv_sem=refs[nb + 1].at[i],
                                              device_id=dev, device_id_type=MESH)
            cp.wait_send()
            cp.wait_recv()

    hbm = lambda t: pltpu.HBM(t.shape, t.dtype)
    res = pl.pallas_call(
        body, name=name, out_shape=(*[hbm(t) for t in bufs], hbm(carrier)),
        in_specs=[HBM_SPEC] * nb + [SEM_SPEC, SEM_SPEC, HBM_SPEC], out_specs=(HBM_SPEC,) * (nb + 1),
        input_output_aliases={**{i: i for i in range(nb)}, nb + 2: nb}, compiler_params=SPLIT_PARAMS,
    )(*bufs, sems[0], sems[1], carrier)
    return list(res[:nb]), res[nb]


def _me_sib_chips():
    x, y, c, chips = _place()
    return 2 * x + y, c, (x, y, 1 - c), chips


def plan_gather_ici(refs):
    me, c, _, chips = _me_sib_chips()
    return [(r.at[me, c], r.at[me, c], (chip[0], chip[1], c)) for r in refs for chip in chips]


def plan_gather_pass(refs):
    _, c, sib, chips = _me_sib_chips()
    return [(r.at[2 * chip[0] + chip[1], c], r.at[2 * chip[0] + chip[1], c], sib) for r in refs for chip in chips]


def plan_pair(n_arrays):
    def plan(refs):
        _, c, sib, _ = _me_sib_chips()
        return [(refs[a].at[1 - c], refs[n_arrays + a], sib) for a in range(n_arrays)]
    return plan


def plan_chips(n_arrays):
    def plan(refs):
        _, c, _, chips = _me_sib_chips()
        return [(refs[a].at[2 * chip[0] + chip[1]], refs[n_arrays + a].at[k], (chip[0], chip[1], c))
                for a in range(n_arrays) for k, chip in enumerate(chips)]
    return plan


def plan_share(n_arrays):
    def plan(refs):
        _, _, sib, _ = _me_sib_chips()
        return [(refs[a], refs[n_arrays + a], sib) for a in range(n_arrays)]
    return plan


def swap_with_sibling(name, src, pick_other_half):
    shape = src.shape[1:] if pick_other_half else src.shape

    def body(src_ref, out_ref, ssem, rsem):
        x, y, c, _ = _place()
        cp = pltpu.make_async_remote_copy(src_ref=src_ref.at[1 - c] if pick_other_half else src_ref, dst_ref=out_ref,
                                          send_sem=ssem, recv_sem=rsem, device_id=(x, y, 1 - c), device_id_type=MESH)
        cp.start()
        cp.wait()

    return pl.pallas_call(
        body, name=name, out_shape=jax.ShapeDtypeStruct(shape, src.dtype), in_specs=[ANY], out_specs=ANY,
        scratch_shapes=[pltpu.SemaphoreType.DMA(()), pltpu.SemaphoreType.DMA(())],
    )(src)


def exchange_chips(name, src, per_chip):
    shape = src.shape[1:] if per_chip else src.shape

    def body(src_ref, out_ref, send_sems, recv_sems):
        x, y, c, chips = _place()
        cps = []
        for k, chip in enumerate(chips):
            s = src_ref.at[2 * chip[0] + chip[1]] if per_chip else src_ref
            cps.append(pltpu.make_async_remote_copy(src_ref=s, dst_ref=out_ref.at[k], send_sem=send_sems.at[k], recv_sem=recv_sems.at[k],
                                                    device_id=(chip[0], chip[1], c), device_id_type=MESH))
        for cp in cps:
            cp.start()
        for cp in cps:
            cp.wait()

    return pl.pallas_call(
        body, name=name, out_shape=jax.ShapeDtypeStruct((3,) + shape, src.dtype), in_specs=[ANY], out_specs=ANY,
        scratch_shapes=[pltpu.SemaphoreType.DMA((3,)), pltpu.SemaphoreType.DMA((3,))],
    )(src)


def pair_sum(name, p, got):
    _, _, rh, cw = p.shape
    tr = _pick(rh, max(16, min(512, ROW_BUDGET // (2 * cw * 10))), SUBLANES_BF16)

    def body(p_ref, got_ref, s_ref, own_ref):
        j = pl.program_id(1)
        tot = p_ref[...].astype(F32) + got_ref[...].astype(F32)
        s_ref[...] = tot.astype(BF16)

        @pl.when(j == 2 * lax.axis_index("x") + lax.axis_index("y"))
        def _():
            own_ref[...] = tot

    return pl.pallas_call(
        body, name=name, grid=(rh // tr, N_CHIPS),
        in_specs=[pl.BlockSpec((None, None, tr, cw), lambda i, j: (lax.axis_index("c"), j, i, 0)),
                  pl.BlockSpec((None, tr, cw), lambda i, j: (j, i, 0))],
        out_specs=[pl.BlockSpec((None, tr, cw), lambda i, j: (j, i, 0)),
                   pl.BlockSpec((tr, cw), lambda i, j: (i, 0))],
        out_shape=[jax.ShapeDtypeStruct((N_CHIPS, rh, cw), BF16), jax.ShapeDtypeStruct((rh, cw), F32)],
        compiler_params=_cparams(("arbitrary", "arbitrary")),
    )(p, got)


def chips_sum(name, own, parts):
    rh, cw = own.shape
    parts = parts.reshape(3 * rh, cw)
    return rowwise(name, lambda o, a, b, c: (((o + a.astype(F32)) + b.astype(F32)) + c.astype(F32),),
                   [(own, cw, 0), (parts, cw, 0, 0), (parts, cw, 0, rh), (parts, cw, 0, 2 * rh)], [], [(cw, F32)], M=rh)[0]


def all_reduce_small(buf):
    r = buf.shape[0]
    got = swap_with_sibling("ar_pair", buf, False)
    chip = rowwise("ar_pairsum", lambda a, b: (a + b,), [(buf, LANES, 0), (got, LANES, 0)], [], [(LANES, F32)], M=r)[0]
    parts = exchange_chips("ar_chips", chip, False).reshape(3 * r, LANES)
    return rowwise("ar_sum", lambda o, fx, fy, fxy: ((o + fy) + (fx + fxy),),
                   [(chip, LANES, 0), (parts, LANES, 0, 0), (parts, LANES, 0, r), (parts, LANES, 0, 2 * r)], [], [(LANES, F32)], M=r)[0]


def _adam_fn(w, g, m, v):
    m = ADAM_B1 * m + (1.0 - ADAM_B1) * g
    v = ADAM_B2 * v + (1.0 - ADAM_B2) * (g * g)
    m_hat = m / (1.0 - ADAM_B1 ** ADAM_STEP)
    v_hat = v / (1.0 - ADAM_B2 ** ADAM_STEP)
    return -ADAM_LR * (m_hat / (jnp.sqrt(v_hat) + ADAM_EPS) + ADAM_WD * w), m, v


def adamw(name, w, g, m, v):
    r, cw = w.shape
    return rowwise(name, _adam_fn, [(t, cw, 0) for t in (w, g, m, v)], [], [(cw, F32)] * 3, M=r)


def adamw_layers(name, w, m, v, mines, theirs, kind):
    L, K, nn = w.shape
    hr, hc = (K // 2, nn) if kind == "cols" else (K, nn // 2)
    tr = _pick(hr, max(8, min(256, ROW_BUDGET // (2 * hc * 4 * (7 + 2 * L)))), 8)
    nb = hr // tr

    def body(*refs):
        w_ref, m_ref, v_ref = refs[:3]
        outs = refs[3 + 2 * L:]
        l, mine_here = pl.program_id(0), pl.program_id(1) == lax.axis_index("c")
        g = jnp.zeros((tr, hc), F32)
        for ll in range(L):
            g = jnp.where(l == ll, jnp.where(mine_here, refs[3 + ll][...], refs[3 + L + ll][...]), g)
        outs[0][...] = g
        outs[1][...], outs[2][...], outs[3][...] = _adam_fn(w_ref[...], g, m_ref[...], v_ref[...])

    if kind == "cols":
        full = pl.BlockSpec((None, tr, hc), lambda l, h, i: (l, h * nb + i, 0))
    else:
        full = pl.BlockSpec((None, tr, hc), lambda l, h, i: (l, i, h))
    def half_spec(ll, mine):
        def imap(l, h, i):
            here = (l == ll) & ((h == lax.axis_index("c")) == mine)
            return (jnp.where(here, i, 0), 0)
        return pl.BlockSpec((tr, hc), imap)

    halves = [half_spec(ll, True) for ll in range(L)] + [half_spec(ll, False) for ll in range(L)]
    return pl.pallas_call(
        body, name=name, grid=(L, 2, nb), in_specs=[full] * 3 + halves, out_specs=[full] * 4,
        out_shape=[jax.ShapeDtypeStruct((L, K, nn), F32)] * 4, compiler_params=_cparams(("parallel", "parallel", "parallel")),
    )(w, m, v, *mines, *theirs)


class Dims:
    def __init__(self, x, g_q, g_kv, g_out_mla, g_out_ssm, g_out_dil, ff):
        self.M, self.D = x.shape[-2], x.shape[-1]
        self.QL, self.KVL = g_q.shape[-1], g_kv.shape[-1]
        self.MW, self.SW, self.DW = g_out_mla.shape[-1], g_out_ssm.shape[-1], g_out_dil.shape[-1]
        self.H = self.MW // MLA_V
        self.FF = ff
        self.G = self.SW // SSM_GROUP
        self.C = self.G * SSM_STATE
        self.o_cq, self.o_u = 0, self.QL
        self.o_qd = self.o_u + self.SW
        self.o_kd = self.o_qd + self.DW
        self.o_vd = self.o_kd + self.DW
        self.o_ckv = self.o_vd + self.DW
        self.o_kr = self.o_ckv + self.KVL
        self.PW = -(-(self.o_kr + MLA_ROPE) // (4 * LANES)) * (4 * LANES)
        assert self.o_u % self.SW == 0 and self.o_qd % LANES == 0 and self.o_ckv % self.KVL == 0 and self.o_kr % LANES == 0
        assert self.H % 2 == 0 and self.DW % LANES == 0 and self.C % LANES == 0
        self.QW = self.H * (MLA_NOPE + MLA_ROPE)
        self.KVW = self.H * (MLA_NOPE + MLA_V)
        sizes = [self.QL, self.KVL, MLA_ROPE, self.SW, self.DW, self.DW, self.DW]
        starts = np.concatenate([[0], np.cumsum(sizes)[:-1]])
        self.ref_cols = {n: (int(s), int(z)) for n, s, z in zip(["cq", "ckv", "kr", "u", "qd", "kd", "vd"], starts, sizes)}
        self.INW = int(sum(sizes))
        self.new_order = ["cq", "u", "qd", "kd", "vd", "ckv", "kr"]
        src = np.concatenate([np.arange(self.ref_cols[n][0], self.ref_cols[n][0] + self.ref_cols[n][1]) for n in self.new_order])
        self.src_in = np.concatenate([src, -np.ones(self.PW - self.INW, np.int64)])
        self.src_q = self._heads_split(self.H, MLA_NOPE, MLA_ROPE)
        self.src_kv = self._heads_split(self.H, MLA_NOPE, MLA_V)

    @staticmethod
    def _heads_split(h, d1, d2):
        first = (np.arange(h)[:, None] * (d1 + d2) + np.arange(d1)[None, :]).reshape(-1)
        second = (np.arange(h)[:, None] * (d1 + d2) + d1 + np.arange(d2)[None, :]).reshape(-1)
        return np.concatenate([first, second])


def _regroup_in(dm, w):
    parts = [w[..., dm.ref_cols[n][0]:dm.ref_cols[n][0] + dm.ref_cols[n][1]] for n in dm.new_order]
    pad = dm.PW - dm.INW
    return jnp.concatenate(parts + [jnp.zeros(w.shape[:-1] + (pad,), w.dtype)], axis=-1)


def _ungroup_in(dm, w):
    off, pieces = 0, {}
    for n in dm.new_order:
        pieces[n] = w[..., off:off + dm.ref_cols[n][1]]
        off += dm.ref_cols[n][1]
    return jnp.concatenate([pieces[n] for n in ["cq", "ckv", "kr", "u", "qd", "kd", "vd"]], axis=-1)


def _split_heads(w, h, d1):
    t = w.reshape(w.shape[:-1] + (h, -1))
    return jnp.concatenate([t[..., :d1].reshape(w.shape[:-1] + (-1,)), t[..., d1:].reshape(w.shape[:-1] + (-1,))], axis=-1)


def _merge_heads(w, h, d1):
    a = w[..., :h * d1].reshape(w.shape[:-1] + (h, d1))
    b = w[..., h * d1:].reshape(w.shape[:-1] + (h, -1))
    return jnp.concatenate([a, b], axis=-1).reshape(w.shape[:-1] + (-1,))


def _cat_cols(re, im):
    r, c = re.shape
    return jnp.stack([re.reshape(r, c // LANES, LANES), im.reshape(r, c // LANES, LANES)], axis=2).reshape(r, 2 * c)


def _uncat_cols(cat):
    r, c2 = cat.shape
    t = cat.reshape(r, c2 // (2 * LANES), 2, LANES)
    return t[:, :, 0].reshape(r, c2 // 2), t[:, :, 1].reshape(r, c2 // 2)


def _block_diag(t, g):
    _, a, b = t.shape
    eye = jnp.eye(g, dtype=bool)[:, None, :, None]
    return jnp.where(eye, t[:, :, None, :], 0).reshape(g * a, g * b)


def _diag_blocks(m, g):
    a, b = m.shape[0] // g, m.shape[1] // g
    eye = jnp.eye(g, dtype=m.dtype)[:, None, :, None]
    return jnp.sum(m.reshape(g, a, g, b) * eye, axis=2)


def _rope_tables(dm):
    half = MLA_ROPE // 2
    inv_freq = ROPE_THETA ** (-jnp.arange(half, dtype=F32) / half)
    ang = jnp.arange(dm.M, dtype=F32)[:, None] * inv_freq[None, :]
    cos = jnp.concatenate([jnp.cos(ang), jnp.cos(ang)], axis=1)
    sin = jnp.concatenate([-jnp.sin(ang), jnp.sin(ang)], axis=1)
    return jnp.tile(cos, (1, dm.H)), jnp.tile(sin, (1, dm.H)), jnp.tile(cos, (1, LANES // MLA_ROPE)), jnp.tile(sin, (1, LANES // MLA_ROPE))


def _rope(x, cos, sin):
    return x * cos + _swap_halves(x, MLA_ROPE // 2) * sin


def _rope_t(d, cos, sin):
    return d * cos + _swap_halves(d * sin, MLA_ROPE // 2)


def _ssm_layer_params(dm, a_re, a_im, log_dt, b_re, b_im):
    flat = lambda t: t.reshape(1, dm.C)
    ldt = jnp.repeat(log_dt, SSM_STATE).reshape(1, dm.C)
    bt = lambda t: jnp.transpose(t, (2, 0, 1)).reshape(SSM_GROUP, dm.C)
    return flat(a_re), flat(a_im), ldt, bt(b_re), bt(b_im)


def layer_forward(dm, l, x, lw, sp, tabs, hook_q, hook_mid):
    M, D = dm.M, dm.D
    n = lambda s: f"{s}_l{l}"
    sv = {"x_in": x}
    h1 = rms_fwd(n("rms_mix"), x, D, 0, lw["g_mix"], M=M)
    proj = matmul(n("in_proj"), h1, lw["w_in"], "nn", M=M, N=dm.PW, K=D)
    sv.update(h1=h1, proj=proj)
    cqn = rms_fwd(n("rms_q"), proj, dm.QL, dm.o_cq, lw["g_q"], M=M)
    q = matmul(n("q_up"), cqn, lw["w_uq"], "nn", M=M, N=dm.QW, K=dm.QL)
    ckvn = rms_fwd(n("rms_kv"), proj, dm.KVL, dm.o_ckv, lw["g_kv"], M=M)
    kv = matmul(n("kv_up"), ckvn, lw["w_ukv"], "nn", M=M, N=dm.KVW, K=dm.KVL, out_dtype=BF16)
    cosq, sinq, cosk, sink = tabs[:4]
    nw = dm.H * MLA_NOPE

    def rope_fn(qb, kb, cq, sq, ck, sk):
        return jnp.concatenate([qb[:, :nw], _rope(qb[:, nw:], cq, sq)], axis=1), _rope(kb, ck, sk)

    pw = dm.H * MLA_ROPE
    q_bf, kpe = rowwise(n("rope"), rope_fn, [(q, dm.QW, 0), (proj, LANES, dm.o_kr), (cosq, pw, 0), (sinq, pw, 0), (cosk, LANES, 0), (sink, LANES, 0)],
                        [], [(dm.QW, BF16), (LANES, BF16)], M=M)
    mla_scale = (MLA_NOPE + MLA_ROPE) ** -0.5
    o_mla, lse_mla = attention_fwd(n("mla_fwd"), q_bf, 0, kv, 0, kv, nw, da=MLA_NOPE, dv=MLA_V, pairs=dm.H // 2, scale=mla_scale,
                                   M=M, qb=q_bf, qb_off=nw, kb=kpe)
    o_mla = hook_q(o_mla)
    sv.update(cqn=cqn, ckvn=ckvn, q_bf=q_bf, kv=kv, kpe=kpe, o_mla=o_mla, lse_mla=lse_mla)
    bu = matmul(n("ssm_bu"), proj, sp["bcat"], "nn", M=M, N=2 * dm.C, K=dm.SW, a_off=(0, dm.o_u))
    hcat = ssm_scan(n("ssm_scan"), bu, sp["acat"], M=M)
    ylin = matmul(n("ssm_y"), hcat, sp["ccat"], "nn", M=M, N=dm.SW, K=2 * dm.C)
    yg = rowwise(n("ssm_gelu"), lambda y, u, d: (_gelu(y + d * u),), [(ylin, dm.SW, 0), (proj, dm.SW, dm.o_u)], [lw["d_skip"]],
                 [(dm.SW, BF16)], M=M)[0]
    z = matmul(n("ssm_glu"), yg, lw["w_glu"], "nn", w=("cols", 2 * dm.SW // N_CHIPS), M=M, N=2 * dm.SW, K=dm.SW)
    sw = dm.SW

    def glu_fn(zb, b):
        zz = zb + b
        return (zz[:, :sw] * jax.nn.sigmoid(zz[:, sw:]),)

    o_ssm = rowwise(n("ssm_gate"), glu_fn, [(z, 2 * sw, 0)], [lw["b_glu"]], [(sw, F32)], M=M)[0]
    sv.update(hcat=hcat, ylin=ylin, yg=yg, z=z, o_ssm=o_ssm)
    o_dil, lse_dil = attention_fwd(n("dil_fwd"), proj, dm.o_qd, proj, dm.o_kd, proj, dm.o_vd, da=DIL_HEAD, dv=DIL_HEAD, pairs=dm.DW // LANES,
                                   scale=DIL_HEAD ** -0.5, M=M, bias=tabs[4])
    sv.update(o_dil=o_dil, lse_dil=lse_dil)
    yn = rowwise(n("out_norm"), lambda a, b, c, ga, gb, gc: (jnp.concatenate([_rms(a, ga), _rms(b, gb), _rms(c, gc)], axis=1),),
                 [(o_mla, dm.MW, 0), (o_ssm, dm.SW, 0), (o_dil, dm.DW, 0)], [lw["g_out_mla"], lw["g_out_ssm"], lw["g_out_dil"]],
                 [(D, BF16)], M=M)[0]
    yn = hook_mid(yn, lw)
    x_mid = matmul(n("out_proj"), yn, lw["w_o"], "nn", w=("rows", D // N_CHIPS), M=M, N=D, K=D, add=x)
    h2 = rms_fwd(n("rms_ffn"), x_mid, D, 0, lw["g_ffn"], M=M)
    ffs = dm.FF // N_CHIPS
    gate = matmul(n("ffn_gate"), h2, lw["w_gate"], "nn", w=("cols",ffs), M=M, N=dm.FF, K=D)
    up = matmul(n("ffn_up"), h2, lw["w_up"], "nn", w=("cols",ffs), M=M, N=dm.FF, K=D)
    act = rowwise(n("ffn_act"), lambda g, u: (g * jax.nn.sigmoid(g) * u,), [(gate, dm.FF, 0), (up, dm.FF, 0)], [], [(dm.FF, BF16)], M=M)[0]
    x_out = matmul(n("ffn_down"), act, lw["w_down"], "nn", w=("rows",ffs), M=M, N=D, K=dm.FF, add=x_mid)
    sv.update(yn=yn, x_mid=x_mid, h2=h2, gate=gate, up=up, act=act)
    return x_out, sv


def layer_backward(dm, l, dx, lw, sp, tabs, sv, hook_a, hook_m, hook_b):
    M, D = dm.M, dm.D
    n = lambda s: f"{s}_l{l}"
    g = {}
    ffs = dm.FF // N_CHIPS
    dact = matmul(n("ffn_down_dx"), dx, lw["w_down"], "nt", w=("rows", ffs), M=M, N=dm.FF, K=D)
    g["w_down"] = matmul(n("ffn_down_dw"), sv["act"], dx, "tn", M=dm.FF, N=D, K=M, out_dtype=BF16, into=("rows", ffs))

    def act_bwd(gb, ub, db):
        _, vjp = jax.vjp(lambda a, b: a * jax.nn.sigmoid(a) * b, gb, ub)
        return vjp(db)

    dgate, dup = rowwise(n("ffn_act_bwd"), act_bwd, [(sv["gate"], dm.FF, 0), (sv["up"], dm.FF, 0), (dact, dm.FF, 0)], [],
                         [(dm.FF, BF16), (dm.FF, BF16)], M=M)
    dh2 = matmul(n("ffn_gate_dx"), dgate, lw["w_gate"], "nt", w=("cols",ffs), M=M, N=D, K=dm.FF)
    dh2 = matmul(n("ffn_up_dx"), dup, lw["w_up"], "nt", w=("cols",ffs), M=M, N=D, K=dm.FF, add=dh2)
    g["w_gate"] = matmul(n("ffn_gate_dw"), sv["h2"], dgate, "tn", M=D, N=dm.FF, K=M, out_dtype=BF16, into=("cols", ffs))
    g["w_up"] = matmul(n("ffn_up_dw"), sv["h2"], dup, "tn", M=D, N=dm.FF, K=M, out_dtype=BF16, into=("cols", ffs))
    dx_mid, g["g_ffn"] = rms_bwd(n("rms_ffn_bwd"), sv["x_mid"], D, 0, lw["g_ffn"], dh2, dx, M=M)
    dx_mid = hook_a(dx_mid, g)
    dyn = matmul(n("out_proj_dx"), dx_mid, lw["w_o"], "nt", w=("rows", D // N_CHIPS), M=M, N=D, K=D)
    g["w_o"] = matmul(n("out_proj_dw"), sv["yn"], dx_mid, "tn", M=D, N=D, K=M, out_dtype=BF16, into=("rows", D // N_CHIPS))
    mw, sw, dw = dm.MW, dm.SW, dm.DW

    def out_norm_bwd(a, b, c, dy, ga, gb, gc):
        res, sums = [], []
        for t, gg, lo, hi in ((a, ga, 0, mw), (b, gb, mw, mw + sw), (c, gc, mw + sw, mw + sw + dw)):
            _, vjp = jax.vjp(_rms, t, gg)
            dt, dg = vjp(dy[:, lo:hi])
            res.append(dt)
            sums.append(dg)
        return res + sums

    do_mla, do_ssm, do_dil, g["g_out_mla"], g["g_out_ssm"], g["g_out_dil"] = rowwise(
        n("out_norm_bwd"), out_norm_bwd, [(sv["o_mla"], mw, 0), (sv["o_ssm"], sw, 0), (sv["o_dil"], dw, 0), (dyn, D, 0)],
        [lw["g_out_mla"], lw["g_out_ssm"], lw["g_out_dil"]], [(mw, F32), (sw, F32), (dw, F32)], [mw, sw, dw], M=M)
    proj = sv["proj"]
    dqd, dkd, dvd = attention_bwd(n("dil_bwd"), proj, dm.o_qd, proj, dm.o_kd, proj, dm.o_vd, sv["o_dil"], do_dil, sv["lse_dil"],
                                  da=DIL_HEAD, dv=DIL_HEAD, pairs=dw // LANES, scale=DIL_HEAD ** -0.5, M=M, bias=tabs[4])
    do_ssm = hook_m(do_ssm, g)
    def glu_bwd(zb, db, b):
        _, vjp = jax.vjp(lambda zz, bb: (zz + bb)[:, :sw] * jax.nn.sigmoid((zz + bb)[:, sw:]), zb, b)
        return vjp(db)

    dz, g["b_glu"] = rowwise(n("ssm_gate_bwd"), glu_bwd, [(sv["z"], 2 * sw, 0), (do_ssm, sw, 0)], [lw["b_glu"]], [(2 * sw, BF16)], [2 * sw], M=M)
    dyg = matmul(n("ssm_glu_dx"), dz, lw["w_glu"], "nt", w=("cols", 2 * sw // N_CHIPS), M=M, N=sw, K=2 * sw)
    g["w_glu"] = matmul(n("ssm_glu_dw"), sv["yg"], dz, "tn", M=sw, N=2 * sw, K=M, out_dtype=BF16, into=("cols", 2 * sw // N_CHIPS))

    def gelu_bwd(y, u, dy, d):
        _, vjp = jax.vjp(lambda yy, uu, dd: _gelu(yy + dd * uu), y, u, d)
        return vjp(dy)

    dylin, du1, g["d_skip"] = rowwise(n("ssm_gelu_bwd"), gelu_bwd, [(sv["ylin"], sw, 0), (proj, sw, dm.o_u), (dyg, sw, 0)], [lw["d_skip"]],
                                      [(sw, BF16), (sw, F32)], [sw], M=M)
    seed = matmul(n("ssm_y_dx"), dylin, sp["ccat"], "nt", M=M, N=2 * dm.C, K=sw)
    d_ccat = matmul(n("ssm_y_dw"), sv["hcat"], dylin, "tn", M=2 * dm.C, N=sw, K=M)
    lam, d_acat = ssm_scan(n("ssm_scan_bwd"), seed, sp["acat_conj"], M=M, reverse=True, hcat=sv["hcat"])
    du = matmul(n("ssm_bu_dx"), lam, sp["bcat"], "nt", M=M, N=sw, K=2 * dm.C, add=du1, out_dtype=BF16)
    d_bcat = matmul(n("ssm_bu_dw"), proj, lam, "tn", M=sw, N=2 * dm.C, K=M, a_off=(0, dm.o_u))
    g["ssm_raw"] = (d_acat, d_bcat, d_ccat)
    nw = dm.H * MLA_NOPE
    dqn, dkn, dv_, dqp, dkp = attention_bwd(n("mla_bwd"), sv["q_bf"], 0, sv["kv"], 0, sv["kv"], nw, sv["o_mla"], do_mla, sv["lse_mla"],
                                            da=MLA_NOPE, dv=MLA_V, pairs=dm.H // 2, scale=(MLA_NOPE + MLA_ROPE) ** -0.5, M=M,
                                            qb=sv["q_bf"], qb_off=nw, kb=sv["kpe"])
    cosq, sinq, cosk, sink = tabs[:4]
    pw = dm.H * MLA_ROPE
    dqp_u, dkr = rowwise(n("rope_bwd"), lambda a, b, cq, sq, ck, sk: (_rope_t(a, cq, sq), _rope_t(b, ck, sk)),
                         [(dqp, pw, 0), (dkp, LANES, 0), (cosq, pw, 0), (sinq, pw, 0), (cosk, LANES, 0), (sink, LANES, 0)], [],
                         [(pw, BF16), (LANES, BF16)], M=M)
    dq = lane_concat(n("dq_cat"), [dqn, dqp_u], M=M)
    dkv = lane_concat(n("dkv_cat"), [dkn, dv_], M=M)
    dcqn = matmul(n("q_up_dx"), dq, lw["w_uq"], "nt", M=M, N=dm.QL, K=dm.QW)
    g["w_uq"] = matmul(n("q_up_dw"), sv["cqn"], dq, "tn", M=dm.QL, N=dm.QW, K=M, out_dtype=BF16)
    dckvn = matmul(n("kv_up_dx"), dkv, lw["w_ukv"], "nt", M=M, N=dm.KVL, K=dm.KVW)
    g["w_ukv"] = matmul(n("kv_up_dw"), sv["ckvn"], dkv, "tn", M=dm.KVL, N=dm.KVW, K=M, out_dtype=BF16)
    dcq, g["g_q"] = rms_bwd(n("rms_q_bwd"), proj, dm.QL, dm.o_cq, lw["g_q"], dcqn, M=M, out_dtype=BF16)
    dckv, g["g_kv"] = rms_bwd(n("rms_kv_bwd"), proj, dm.KVL, dm.o_ckv, lw["g_kv"], dckvn, M=M, out_dtype=BF16)
    dproj = hook_b(lane_concat(n("dproj_cat"), [dcq, du, dqd, dkd, dvd, dckv, dkr], M=M, pad_to=dm.PW), g)
    dh1 = matmul(n("in_proj_dx"), dproj, lw["w_in"], "nt", M=M, N=D, K=dm.PW)
    g["w_in"] = matmul(n("in_proj_dw"), sv["h1"], dproj, "tn", M=D, N=dm.PW, K=M, out_dtype=BF16)
    dx_in, g["g_mix"] = rms_bwd(n("rms_mix_bwd"), sv["x_in"], D, 0, lw["g_mix"], dh1, dx_mid, M=M)
    return dx_in, g


def layer_params(dm, small, l):
    lw = {k: small[k][l].reshape(1, -1) for k in ("g_mix", "g_q", "g_kv", "b_glu", "g_out_mla", "g_out_ssm", "g_out_dil", "g_ffn", "d_skip")}
    raw = _ssm_layer_params(dm, small["a_re"][l], small["a_im"][l], small["log_dt"][l], small["b_re"][l], small["b_im"][l])
    ar, ai, bbr, bbi = ssm_param_fwd(f"ssm_param_l{l}", *raw)
    g_ = dm.G
    bd = lambda t: _block_diag(jnp.transpose(t.reshape(SSM_GROUP, g_, SSM_STATE), (1, 0, 2)), g_)
    cd = lambda t: _block_diag(jnp.transpose(t, (0, 2, 1)), g_)
    cre, cim = cd(small["c_re"][l]), cd(small["c_im"][l])
    sp = {"acat": _cat_cols(ar, ai), "acat_conj": _cat_cols(ar, -ai),
          "bcat": _cat_cols(bd(bbr), bd(bbi)).astype(BF16),
          "ccat": _cat_cols(cre.T, -cim.T).T.astype(BF16)}
    return lw, sp, raw


def ssm_param_grads(dm, l, g, raw):
    d_acat, d_bcat, d_ccat = g.pop("ssm_raw")
    d_ar, d_ai = _uncat_cols(d_acat)
    dbr, dbi = _uncat_cols(d_bcat)
    g_ = dm.G
    to_rows = lambda t: jnp.transpose(_diag_blocks(t, g_), (1, 0, 2)).reshape(SSM_GROUP, dm.C)
    da_re, da_im, dldt, db_re, db_im = ssm_param_bwd(f"ssm_param_bwd_l{l}", *raw, d_ar, d_ai, to_rows(dbr), to_rows(dbi))
    dcr, dci = _uncat_cols(d_ccat.T)
    g["a_re"], g["a_im"] = da_re.reshape(g_, SSM_STATE), da_im.reshape(g_, SSM_STATE)
    g["log_dt"] = jnp.sum(dldt.reshape(g_, SSM_STATE), axis=1)
    from_rows = lambda t: jnp.transpose(t.reshape(SSM_GROUP, g_, SSM_STATE), (1, 2, 0))
    g["b_re"], g["b_im"] = from_rows(db_re), from_rows(db_im)
    g["c_re"] = jnp.transpose(_diag_blocks(dcr.T, g_), (0, 2, 1))
    g["c_im"] = -jnp.transpose(_diag_blocks(dci.T, g_), (0, 2, 1))
    g["d_skip"] = g["d_skip"].reshape(g_, SSM_GROUP)


def loss_and_grad(dm, h, target, g_final):
    D = dm.D

    def loss_fn(xb, tb, gb):
        y, vjp = jax.vjp(_rms, xb, gb)
        err = y - tb
        dxb, dg = vjp(err * (1.0 / D))
        part = 0.5 * jnp.sum(jnp.mean(err * err, axis=-1, keepdims=True), axis=0, keepdims=True)
        lane = lax.broadcasted_iota(jnp.int32, (1, LANES), 1)
        return dxb, dg, jnp.where(lane == 0, part, 0.0)

    return rowwise("loss", loss_fn, [(h, D, 0), (target, D, 0)], [g_final.reshape(1, D)], [(D, F32)], [D, LANES], M=dm.M)


KIND = {"w_in": "cols", "w_uq": "cols", "w_ukv": "cols", "w_glu": "cols", "w_o": "rows", "w_gate": "cols", "w_up": "cols", "w_down": "rows"}
SHARDED = list(KIND)
GATHER_GROUPS = {"mixer": ["w_in", "w_uq", "w_ukv", "w_glu"], "rest": ["w_o", "w_gate", "w_up", "w_down"]}
REDUCE_GROUPS = {"ffn": ["w_gate", "w_up", "w_down"], "others": ["w_o", "w_in", "w_uq", "w_ukv", "w_glu"]}
SMALL = ["g_mix", "g_q", "g_kv", "a_re", "a_im", "b_re", "b_im", "c_re", "c_im", "d_skip", "log_dt", "b_glu",
         "g_out_mla", "g_out_ssm", "g_out_dil", "g_ffn", "g_final"]
ORDER = ["g_mix", "w_in", "g_q", "w_uq", "g_kv", "w_ukv", "a_re", "a_im", "b_re", "b_im", "c_re", "c_im", "d_skip", "log_dt",
         "w_glu", "b_glu", "g_out_mla", "g_out_ssm", "g_out_dil", "w_o", "g_ffn", "w_gate", "w_up", "w_down", "g_final"]


def kernel(x, g_mix, w_in, g_q, w_uq, g_kv, w_ukv, a_re, a_im, b_re, b_im, c_re, c_im, d_skip, log_dt, w_glu, b_glu, g_out_mla, g_out_ssm, g_out_dil, w_o, g_ffn, w_gate, w_up, w_down, g_final, loss_target, m_g_mix, m_w_in, m_g_q, m_w_uq, m_g_kv, m_w_ukv, m_a_re, m_a_im, m_b_re, m_b_im, m_c_re, m_c_im, m_d_skip, m_log_dt, m_w_glu, m_b_glu, m_g_out_mla, m_g_out_ssm, m_g_out_dil, m_w_o, m_g_ffn, m_w_gate, m_w_up, m_w_down, m_g_final, v_g_mix, v_w_in, v_g_q, v_w_uq, v_g_kv, v_w_ukv, v_a_re, v_a_im, v_b_re, v_b_im, v_c_re, v_c_im, v_d_skip, v_log_dt, v_w_glu, v_b_glu, v_g_out_mla, v_g_out_ssm, v_g_out_dil, v_w_o, v_g_ffn, v_w_gate, v_w_up, v_w_down, v_g_final):
    args = locals()
    w = {k: args[k] for k in ORDER}
    mom = {k: args["m_" + k] for k in ORDER}
    var = {k: args["v_" + k] for k in ORDER}
    dm = Dims(x, g_q, g_kv, g_out_mla, g_out_ssm, g_out_dil, w_gate.shape[-1] * N_CHIPS)
    L = g_mix.shape[0]

    small = {k: w[k] for k in SMALL}
    na = len(SHARDED)
    tabs = _rope_tables(dm) + (dilated_bias(dm.M),)
    sel ={"w_in": selection_matrices(dm.src_in, w_in.shape[-1]), "w_uq": selection_matrices(dm.src_q, w_uq.shape[-1]),
           "w_ukv": selection_matrices(dm.src_kv, w_ukv.shape[-1])}

    G = [{k: cast_into_slot(f"cast_{k}_l{l}", w[k], KIND[k], l) for k in SHARDED} for l in range(L)]
    sems = {}

    def gather_stage(stage, plan):
        def start(l, grp, car):
            names = GATHER_GROUPS[grp]
            sems[stage, l, grp], bufs, _, car = split_start(f"ag_{stage}_start_{grp}_l{l}", plan, 3 * len(names), [G[l][k] for k in names], [], car)
            G[l].update(zip(names, bufs))
            return car

        def wait(l, grp, car):
            names = GATHER_GROUPS[grp]
            bufs, car = split_wait(f"ag_{stage}_wait_{grp}_l{l}", plan, sems[stage, l, grp], [G[l][k] for k in names], car)
            G[l].update(zip(names, bufs))
            return car

        return start, wait

    ici_start, ici_wait = gather_stage("ici", plan_gather_ici)
    pass_start, pass_wait = gather_stage("pass", plan_gather_pass)

    car = tabs[0]
    for l, grp in ((0, "mixer"), (0, "rest"), (1, "mixer")):
        if l < L:
            car = ici_start(l, grp, car)
    tabs = (pass_wait(0, "mixer", pass_start(0, "mixer", ici_wait(0, "mixer", car))),) + tabs[1:]
    h = x.reshape(dm.M, dm.D)
    lws, sps, raws, saved = [], [], [], []
    for l in range(L):
        lw, sp, raw = layer_params(dm, small, l)
        for k in GATHER_GROUPS["mixer"]:
            lw[k] = regroup_cols(f"regroup_{k}_l{l}", G[l][k], sel[k]) if k in sel else G[l][k]

        def at_q(car, l=l):
            car = pass_start(l, "rest", ici_wait(l, "rest", car))
            if l + 1 < L:
                car = pass_start(l + 1, "mixer", ici_wait(l + 1, "mixer", car))
            return car

        def at_mid(car, lw, l=l):
            car = pass_wait(l, "rest", car)
            lw.update({k: G[l][k] for k in GATHER_GROUPS["rest"]})
            if l + 1 < L:
                car = ici_start(l + 1, "rest", pass_wait(l + 1, "mixer", car))
            if l + 2 < L:
                car = ici_start(l + 2, "mixer", car)
            return car

        h, sv = layer_forward(dm, l, h, lw, sp, tabs, at_q, at_mid)
        lws.append(lw)
        sps.append(sp)
        raws.append(raw)
        saved.append(sv)
    dx, g_final_part, loss_part = loss_and_grad(dm, h, loss_target.reshape(dm.M, dm.D), w["g_final"])

    def reduce_begin(l, grp, g, car):
        names = REDUCE_GROUPS[grp]
        parts = [g.pop(k) for k in names]
        fresh = [jax.ShapeDtypeStruct(p.shape[1:], BF16) for p in parts]
        sm, parts, gots, car = split_start(f"rs_pair_start_{grp}_l{l}", plan_pair(len(names)), len(names), parts, fresh, car)
        return {"l": l, "grp": grp, "names": names, "sems": sm, "parts": parts, "gots": gots}, car

    def reduce_chips(st, car):
        names, tag, n = st["names"], f"{st['grp']}_l{st['l']}", len(st["names"])
        bufs, car = split_wait(f"rs_pair_wait_{tag}", plan_pair(n), st["sems"], st["parts"] + st["gots"], car)
        sums = [pair_sum(f"rs_pairsum_{a}_l{st['l']}", bufs[i], bufs[n + i]) for i, a in enumerate(names)]
        fresh = [jax.ShapeDtypeStruct((3,) + s.shape[1:], BF16) for s, _ in sums]
        st["sems"], st["s"], st["arrived"], car = split_start(f"rs_chips_start_{tag}", plan_chips(n), 3 * n, [s for s, _ in sums], fresh, car)
        st["own"] = [o for _, o in sums]
        return car

    def reduce_share(st, car):
        names, tag, n = st["names"], f"{st['grp']}_l{st['l']}", len(st["names"])
        bufs, car = split_wait(f"rs_chips_wait_{tag}", plan_chips(n), st["sems"], st["s"] + st["arrived"], car)
        mine = [chips_sum(f"rs_sum_{a}_l{st['l']}", st["own"][i], bufs[n + i]) for i, a in enumerate(names)]
        fresh = [jax.ShapeDtypeStruct(m_.shape, F32) for m_ in mine]
        st["sems"], st["mine"], st["theirs"], car = split_start(f"rs_share_start_{tag}", plan_share(n), n, mine, fresh, car)
        return car

    def reduce_end(st, car):
        n = len(st["names"])
        bufs, car = split_wait(f"rs_share_wait_{st['grp']}_l{st['l']}", plan_share(n), st["sems"], st["mine"] + st["theirs"], car)
        for i, k in enumerate(st["names"]):
            reduced[st["l"]][k] = (bufs[i], bufs[n + i])
        return car

    reduced, grads, prev_ffn, prev_oth = [{} for _ in range(L)], [None] * L, None, None
    for l in reversed(range(L)):
        mine = {}

        def at_a(car, g, l=l, mine=mine, pf=prev_ffn, po=prev_oth):
            mine["st"], car = reduce_begin(l, "ffn", g, car)
            if pf is not None:
                car = reduce_chips(po, reduce_share(pf, car))
            return car

        def at_m(car, g, mine=mine, pf=prev_ffn):
            car = reduce_chips(mine["st"], car)
            return car if pf is None else reduce_end(pf, car)

        def at_b(car, g, po=prev_oth):
            return car if po is None else reduce_share(po, car)

        dx, g = layer_backward(dm, l, dx, lws[l], sps[l], tabs, saved[l], at_a, at_m, at_b)
        ssm_param_grads(dm, l, g, raws[l])
        if prev_oth is not None:
            dx = reduce_end(prev_oth, dx)
        for k in sel:
            g[k] = ungroup_cols(f"ungroup_{k}_l{l}", g[k], sel[k])
        prev_ffn = mine["st"]
        prev_oth, dx = reduce_begin(l, "others", g, dx)
        grads[l] = g
    dx = reduce_end(prev_ffn, reduce_share(prev_ffn, reduce_chips(prev_oth, dx)))
    dx = reduce_end(prev_oth, reduce_share(prev_oth, dx))

    gsum = {}
    small_names = [k for k in SMALL if k != "g_final"]
    pieces = [jnp.stack([grads[l][k] for l in range(L)]).reshape(-1) for k in small_names] + [g_final_part.reshape(-1), loss_part.reshape(-1)]
    sizes = [int(p.shape[0]) for p in pieces]
    total = sum(sizes)
    rows = -(-total // (LANES * 16)) * 16
    pack = lambda ps: jnp.concatenate(ps + [jnp.zeros((rows * LANES - total,), F32)]).reshape(rows, LANES)
    red = all_reduce_small(pack(pieces))
    flat = red.reshape(-1)
    offs = np.concatenate([[0], np.cumsum(sizes)]).astype(int)
    names = small_names + ["g_final"]
    for i, k in enumerate(names):
        gsum[k] = flat[offs[i]:offs[i + 1]].reshape(w[k].shape)
    loss = flat[offs[len(names)]]

    delta, new_m, new_v = {}, {}, {}
    for k in SHARDED:
        gsum[k], delta[k], new_m[k], new_v[k] = adamw_layers(f"adam_{k}", w[k], mom[k], var[k], [reduced[l][k][0] for l in range(L)],
                                                             [reduced[l][k][1] for l in range(L)], KIND[k])
    sm_sizes = sizes[:len(names)]
    sm_total = sum(sm_sizes)
    packs = lambda d: jnp.concatenate([d[k].reshape(-1) for k in names] + [jnp.zeros((rows * LANES - sm_total,), F32)]).reshape(rows, LANES)
    gs = jnp.concatenate([flat[:sm_total], jnp.zeros((rows * LANES - sm_total,), F32)]).reshape(rows, LANES)
    d_, m_, v_ = adamw("adam_small", packs(w), gs, packs(mom), packs(var))
    for i, k in enumerate(names):
        sl = slice(offs[i], offs[i + 1])
        delta[k], new_m[k], new_v[k] = (t.reshape(-1)[sl].reshape(w[k].shape) for t in (d_, m_, v_))

    return (loss, dx.reshape(x.shape), *[gsum[k] for k in ORDER], *[delta[k] for k in ORDER],
            *[new_m[k] for k in ORDER], *[new_v[k] for k in ORDER])
```

```python
import functools
import math

import numpy as np
import jax
import jax.numpy as jnp
from jax import lax
from jax.experimental import pallas as pl
from jax.experimental.pallas import tpu as pltpu

F32 = jnp.float32
BF16 = jnp.bfloat16
MESH = pl.DeviceIdType.MESH

NORM_EPS = 1e-6
MLA_NOPE, MLA_ROPE, MLA_V = 128, 64, 128
SSM_GROUP, SSM_STATE = 16, 64
DIL_HEAD = 64
DIL_PATTERNS = ((128, 1), (512, 4), (2048, 16))
ROPE_THETA = 10000.0
ADAM_LR, ADAM_B1, ADAM_B2, ADAM_EPS, ADAM_WD, ADAM_STEP = 0.001, 0.9, 0.999, 1e-08, 0.01, 10
N_CHIPS = 4

LANES = 128
SUBLANES_BF16 = 16
VMEM_LIMIT = 56 * 1024 * 1024
ROW_BUDGET = 20 * 1024 * 1024
MM_BUDGET = 40 * 1024 * 1024
NEG = -1e30


def _cparams(sem=None):
    return pltpu.CompilerParams(dimension_semantics=sem, vmem_limit_bytes=VMEM_LIMIT)


def _pick(n, cap, q, off=0):
    best = None
    for d in range(q, min(n, cap) + 1, q):
        if n % d == 0 and off % d == 0:
            best = d
    if best is None or (best * 4 <= min(cap, n) and n <= 3072 and off % n == 0):
        assert off % n == 0, (n, off)
        return n
    return best


_DOT_DIMS = {"nn": (((1,), (0,)), ((), ())), "nt": (((1,), (1,)), ((), ())), "tn": (((0,), (0,)), ((), ()))}


def _divs(n, q, within=None, off=0):
    return [d for d in range(q, n + 1, q) if n % d == 0 and off % d == 0 and (within is None or within % d == 0)] or [n]


def _mm_tiles(M, N, K, tms, tns, tks, ab, bb, ob):
    best = None
    for tk in tks:
        nk = K // tk
        for tn in tns:
            for tm in tms:
                if 2 * (tm * tk * ab + tk * tn * bb + tm * tn * ob) + tm * tn * 4 * (2 if nk > 1 else 1) > MM_BUDGET:
                    continue
                steps = (M // tm) * (N // tn) * nk
                hbm = M * K * ab * (1 if nk == 1 else N // tn) + K * N * bb * (M // tm) + M * N * ob
                cost = steps * 0.35e-6 + hbm / 3.0e12 + (nk - 1) * M * N * 12 / 4.0e12
                if best is None or cost < best[0]:
                    best = (cost, tm, tn, tk)
    assert best is not None, (M, N, K)
    return best[1:]


def matmul(name, a, b, mode, *, M, N, K, a_off=(0, 0), b_off=(0, 0), b_lead=None, w=None, add=None, out_dtype=F32, into=None):
    tn_mode = mode == "tn"
    a_ro, a_co = (a_off[1], a_off[0]) if tn_mode else a_off
    b_no, b_ko = b_off if mode == "nt" else (b_off[1], b_off[0])
    n_within = k_within = m_within = None
    if w is not None:
        kind, shard = w
        if kind == "cols":
            b = b.reshape(N_CHIPS, b.shape[1] * b.shape[2], b.shape[3])
        rows_within, cols_within = (K if mode == "nn" else N, shard) if kind == "cols" else (shard, (N if mode == "nn" else K) // 2)
        k_within, n_within = (rows_within, cols_within) if mode == "nn" else (cols_within, rows_within)
    if into is not None:
        m_within, n_within = (M // 2, into[1]) if into[0] == "cols" else (into[1], N // 2)
    tms = [d for d in _divs(M, 128 if tn_mode else SUBLANES_BF16, m_within, a_ro) if d <= 1408]
    tns = [d for d in _divs(N, LANES, n_within, b_no) if d <= 2048]
    tks = _divs(K, SUBLANES_BF16 if tn_mode else LANES, k_within, math.gcd(a_co, b_ko))
    ob = jnp.dtype(out_dtype).itemsize + (add.dtype.itemsize if add is not None else 0)
    tm, tn, tk = _mm_tiles(M, N, K, tms, tns, tks, a.dtype.itemsize, b.dtype.itemsize, ob)
    nk = K // tk
    dn = _DOT_DIMS[mode]

    if tn_mode:
        a_spec = pl.BlockSpec((tk, tm), lambda i, j, k: (k + a_co // tk, i + a_ro // tm))
    else:
        a_spec = pl.BlockSpec((tm, tk), lambda i, j, k: (i + a_ro // tm, k + a_co // tk))
    b_blk = (tn, tk) if mode == "nt" else (tk, tn)
    if w is not None:
        tr_, tc_ = (tk, tn) if mode == "nn" else (tn, tk)
        rper, cper = rows_within // tr_, cols_within // tc_

        def wmap(rb, cb):
            if kind == "cols":
                return (cb // cper, rb, cb % cper)
            return (rb // rper, cb // cper, rb % rper, cb % cper)

        imap = (lambda i, j, k: wmap(k, j)) if mode == "nn" else (lambda i, j, k: wmap(j, k))
        b_spec = pl.BlockSpec((None,) * (b.ndim - 2) + b_blk, imap)
    else:
        if mode == "nt":
            imap = lambda i, j, k: (j + b_no // tn, k + b_ko // tk)
        else:
            imap = lambda i, j, k: (k + b_ko // tk, j + b_no // tn)
        if b_lead is None:
            b_spec = pl.BlockSpec(b_blk, imap)
        else:
            b_spec = pl.BlockSpec((None,) + b_blk, lambda i, j, k: (b_lead,) + imap(i, j, k))
    o_plain = pl.BlockSpec((tm, tn), lambda i, j, k: (i, j))
    if into is None:
        o_spec, out_shape = o_plain, jax.ShapeDtypeStruct((M, N), out_dtype)
    else:
        rper, cper = m_within // tm, n_within // tn
        if into[0] == "cols":
            o_spec = pl.BlockSpec((None, None, tm, tn), lambda i, j, k: (i // rper, j // cper, i % rper, j % cper))
        else:
            o_spec = pl.BlockSpec((None, None, tm, tn), lambda i, j, k: (j // cper, i // rper, i % rper, j % cper))
        out_shape = jax.ShapeDtypeStruct((2, N_CHIPS, m_within, n_within), out_dtype)
    has_add = add is not None
    n_in = 2 + has_add

    def body(*refs):
        a_ref, b_ref = refs[0], refs[1]
        add_ref = refs[2] if has_add else None
        o_ref = refs[n_in]
        part = lax.dot_general(a_ref[...].astype(BF16), b_ref[...].astype(BF16), dn, preferred_element_type=F32)

        def finish(r):
            if has_add:
                r = r + add_ref[...].astype(F32)
            o_ref[...] = r.astype(o_ref.dtype)

        if nk == 1:
            finish(part)
        else:
            acc_ref = refs[-1]
            k = pl.program_id(2)

            @pl.when(k == 0)
            def _():
                acc_ref[...] = part

            @pl.when((k > 0) & (k < nk - 1))
            def _():
                acc_ref[...] += part

            @pl.when(k == nk - 1)
            def _():
                finish(acc_ref[...] + part)

    in_specs = [a_spec, b_spec] + ([o_plain] if has_add else [])
    args = (a, b) + ((add,) if has_add else ())
    return pl.pallas_call(
        body, name=name, out_shape=out_shape, grid=(M // tm, N // tn, nk), in_specs=in_specs, out_specs=o_spec,
        scratch_shapes=[pltpu.VMEM((tm, tn), F32)] if nk > 1 else [],
        compiler_params=_cparams(("parallel", "parallel", "arbitrary")),
    )(*args)


def selection_matrices(src_of_new, n_shard):
    src_np = np.asarray(src_of_new, np.int64)
    src = jnp.asarray(src_np.astype(np.int32))
    ref = jnp.arange(N_CHIPS, dtype=jnp.int32)[:, None] * n_shard + jnp.arange(n_shard, dtype=jnp.int32)[None, :]
    pm = (ref[:, :, None] == src[None, None, :]).astype(BF16)
    pmt = (src[None, :, None] == ref[:, None, :]).astype(BF16)
    tc = _pick(len(src_np), 512, LANES)
    feeds = [sorted({int(s) // n_shard for s in src_np[cb * tc:(cb + 1) * tc] if s >= 0}) for cb in range(len(src_np) // tc)]
    return pm, pmt, tc, feeds


def regroup_cols(name, g, sel, transposed=False):
    pm, _, tc, feeds = sel
    n_new = pm.shape[-1]
    if transposed:
        nn, kh = g.shape[2:]
        K = 2 * kh
        tm = _pick(kh, 512, LANES)
        hb = kh // tm
        g_spec = pl.BlockSpec((N_CHIPS, None, nn, tm), lambda c, i: (0, i // hb, 0, i % hb))
    else:
        nn = g.shape[-1]
        g = g.reshape(N_CHIPS, -1, nn)
        K = g.shape[1]
        tm = _pick(K, 512, SUBLANES_BF16)
        g_spec = pl.BlockSpec((N_CHIPS, tm, nn), lambda c, i: (0, i, 0))

    def body(g_ref, pm_ref, o_ref):
        for cb, chips in enumerate(feeds):
            @pl.when(pl.program_id(0) == cb)
            def _(chips=chips):
                acc = jnp.zeros((tm, tc), F32)
                for j in chips:
                    acc = acc + _dot(g_ref[j], pm_ref[j], "tn" if transposed else "nn")
                o_ref[...] = acc.astype(o_ref.dtype)

    return pl.pallas_call(
        body, name=name, out_shape=jax.ShapeDtypeStruct((K, n_new), BF16), grid=(n_new // tc, K // tm),
        in_specs=[g_spec, pl.BlockSpec((N_CHIPS, nn, tc), lambda c, i: (0, 0, c))],
        out_specs=pl.BlockSpec((tm, tc), lambda c, i: (i, c)), compiler_params=_cparams(("parallel", "parallel")),
    )(g, pm)


def ungroup_cols(name, dw, sel, transposed=False):
    pm, pmt, tc, feeds = sel
    K, n_new = dw.shape
    nn = pmt.shape[-1]
    kh = K // 2
    tm = _pick(kh, 512, LANES if transposed else SUBLANES_BF16)
    hb = kh // tm
    fed_by = [[cb for cb, chips in enumerate(feeds) if j in chips] for j in range(N_CHIPS)]

    def body(dw_ref, sel_ref, o_ref):
        for j, blocks in enumerate(fed_by):
            @pl.when(pl.program_id(0) == j)
            def _(blocks=blocks):
                acc = jnp.zeros((nn, tm) if transposed else (tm, nn), F32)
                for cb in blocks:
                    cols = slice(cb * tc, (cb + 1) * tc)
                    if transposed:
                        acc = acc + _dot(sel_ref[:, cols], dw_ref[:, cols], "nt")
                    else:
                        acc = acc + _dot(dw_ref[:, cols], sel_ref[cols, :], "nn")
                o_ref[...] = acc.astype(o_ref.dtype)

    if transposed:
        sel_arr, sel_spec = pm, pl.BlockSpec((None, nn, n_new), lambda j, i: (j, 0, 0))
        out_shape = jax.ShapeDtypeStruct((2, N_CHIPS, nn, kh), BF16)
        out_spec = pl.BlockSpec((None, None, nn, tm), lambda j, i: (i // hb, j, 0, i % hb))
    else:
        sel_arr, sel_spec = pmt, pl.BlockSpec((None, n_new, nn), lambda j, i: (j, 0, 0))
        out_shape = jax.ShapeDtypeStruct((2, N_CHIPS, kh, nn), BF16)
        out_spec = pl.BlockSpec((None, None, tm, nn), lambda j, i: (i // hb, j, i % hb, 0))
    return pl.pallas_call(
        body, name=name, out_shape=out_shape, grid=(N_CHIPS, 2 * hb),
        in_specs=[pl.BlockSpec((tm, n_new), lambda j, i: (i, 0)), sel_spec], out_specs=out_spec,
        compiler_params=_cparams(("parallel", "parallel")),
    )(dw, sel_arr)


def rowwise(name, fn, rows, vecs, outs, sums=(), *, M):
    rows = [tuple(r) + (0,) * (4 - len(r)) for r in rows]
    nr, nv, no, ns = len(rows), len(vecs), len(outs), len(sums)
    per_row = sum(w * a.dtype.itemsize for a, w, _, _ in rows) + sum(w * jnp.dtype(d).itemsize for w, d in outs)
    tr = _pick(M, max(8, min(512, ROW_BUDGET // (2 * per_row))), 16 if M % 16 == 0 else 8)

    def body(*refs):
        i = pl.program_id(0)
        res = fn(*[r[...] for r in refs[:nr + nv]])
        o_refs = refs[nr + nv:nr + nv + no]
        s_refs = refs[nr + nv + no:]
        for ref, val in zip(o_refs, res[:no]):
            ref[...] = val.astype(ref.dtype)
        if ns:
            @pl.when(i == 0)
            def _():
                for ref in s_refs:
                    ref[...] = jnp.zeros(ref.shape, F32)

            for ref, val in zip(s_refs, res[no:]):
                ref[...] += val

    in_specs = [pl.BlockSpec((tr, w), functools.partial(lambda i, cb, rb: (i + rb, cb), cb=off // w, rb=roff // tr)) for _, w, off, roff in rows]
    for _, w, off, roff in rows:
        assert off % w == 0 and roff % tr == 0
    in_specs += [pl.BlockSpec(v.shape, functools.partial(lambda i, nd: (0,) * nd, nd=v.ndim)) for v in vecs]
    out_specs = [pl.BlockSpec((tr, w), lambda i: (i, 0)) for w, _ in outs]
    out_specs += [pl.BlockSpec((1, w), lambda i: (0, 0)) for w in sums]
    out_shape = [jax.ShapeDtypeStruct((M, w), d) for w, d in outs] + [jax.ShapeDtypeStruct((1, w), F32) for w in sums]
    return pl.pallas_call(
        body, name=name, out_shape=out_shape, grid=(M // tr,), in_specs=in_specs, out_specs=out_specs,
        compiler_params=_cparams(("arbitrary",) if ns else ("parallel",)),
    )(*[r[0] for r in rows], *vecs)


def _rms(x, g):
    xf = x.astype(F32)
    return xf * lax.rsqrt(jnp.mean(xf * xf, axis=-1, keepdims=True) + NORM_EPS) * g


def _gelu(y):
    return 0.5 * y * (1.0 + jnp.tanh(math.sqrt(2.0 / math.pi) * (y + 0.044715 * (y * y * y))))


def _colsum(v):
    return jnp.sum(v, axis=0, keepdims=True)


def rms_fwd(name, x, width, off, g, *, M):
    return rowwise(name, lambda xb, gb: (_rms(xb, gb),), [(x, width, off)], [g], [(width, BF16)], M=M)[0]


def rms_bwd(name, x, width, off, g, dy, resid=None, *, M, out_dtype=F32):
    def fn(xb, dyb, *rest):
        gb = rest[-1]
        _, vjp = jax.vjp(_rms, xb.astype(F32), gb)
        dx, dg = vjp(dyb.astype(F32))
        if resid is not None:
            dx = dx + rest[0]
        return dx, dg

    rows = [(x, width, off), (dy, width, 0)] + ([(resid, width, 0)] if resid is not None else [])
    return rowwise(name, fn, rows, [g], [(width, out_dtype)], [width], M=M)


def lane_concat(name, parts, *, M, pad_to=None):
    width = sum(p.shape[1] for p in parts)
    pad = 0 if pad_to is None else pad_to - width

    def fn(*blocks):
        cols = [b.astype(BF16) for b in blocks]
        if pad:
            cols.append(jnp.zeros((blocks[0].shape[0], pad), BF16))
        return (jnp.concatenate(cols, axis=1),)

    return rowwise(name, fn, [(p, p.shape[1], 0) for p in parts], [], [(width + pad, BF16)], M=M)[0]


def _swap_halves(x, half):
    w = x.shape[-1]
    lane = lax.broadcasted_iota(jnp.int32, x.shape, x.ndim - 1)
    first = (lane % (2 * half)) < half
    return jnp.where(first, pltpu.roll(x, w - half, x.ndim - 1), pltpu.roll(x, half, x.ndim - 1))


def dilated_bias(M):
    delta = jnp.arange(M, dtype=jnp.int32)[:, None] - jnp.arange(M, dtype=jnp.int32)[None, :]
    w = jnp.zeros(delta.shape, F32)
    for window, dil in DIL_PATTERNS:
        ok = (delta >= 0) & (delta <= window)
        if dil > 1:
            ok = ok & ((delta & (dil - 1)) == 0)
        w = w + ok.astype(F32)
    return jnp.where(w > 0, jnp.log(jnp.maximum(w, 1.0)), NEG)


def _dot(a, b, mode):
    return lax.dot_general(a, b, _DOT_DIMS[mode], preferred_element_type=F32)


def attention_fwd(name, qa, qa_off, ka, ka_off, v, v_off, *, da, dv, pairs, scale, M, qb=None, qb_off=0, kb=None, bias=None):
    tq = min(256, M)
    tk = min(512, M)
    has_b = qb is not None
    has_bias = bias is not None
    dr = MLA_ROPE

    def body(*refs):
        refs = list(refs)
        bias_ref = refs.pop(3) if has_bias else None
        if has_b:
            qa_ref, ka_ref, v_ref, qb_ref, kb_ref, o_ref, lse_ref = refs
        else:
            qa_ref, ka_ref, v_ref, o_ref, lse_ref = refs
        i = pl.program_id(1)
        t0 = i * tq
        nkb = (t0 + tq + tk - 1) // tk
        n_full = nkb if has_bias else t0 // tk
        q1s = [qa_ref[:, hh * da:(hh + 1) * da].astype(BF16) for hh in range(2)]
        q2s = [qb_ref[:, hh * dr:(hh + 1) * dr].astype(BF16) if has_b else None for hh in range(2)]

        def step(kbi, carry, masked):
            ks = pl.multiple_of(kbi * tk, tk)
            k2 = kb_ref[pl.ds(ks, tk), 0:dr].astype(BF16) if has_b else None
            if has_bias:
                extra = bias_ref[:, pl.ds(ks, tk)]
            elif masked:
                delta = (t0 + lax.broadcasted_iota(jnp.int32, (tq, tk), 0)) - (ks + lax.broadcasted_iota(jnp.int32, (tq, tk), 1))
            new = []
            for hh, (m, l, acc) in enumerate(carry):
                k1 = ka_ref[pl.ds(ks, tk), hh * da:(hh + 1) * da].astype(BF16)
                s = _dot(q1s[hh], k1, "nt")
                if has_b:
                    s = s + _dot(q2s[hh], k2, "nt")
                s = s * scale
                if has_bias:
                    s = s + extra
                elif masked:
                    s = jnp.where(delta >= 0, s, NEG)
                m_new = jnp.maximum(m, jnp.max(s, axis=1, keepdims=True))
                alpha = jnp.exp(m - m_new)
                p = jnp.exp(s - m_new)
                l = alpha * l + jnp.sum(p, axis=1, keepdims=True)
                vv = v_ref[pl.ds(ks, tk), hh * dv:(hh + 1) * dv].astype(BF16)
                acc = alpha * acc + _dot(p.astype(BF16), vv, "nn")
                new.append((m_new, l, acc))
            return tuple(new)

        carry = tuple((jnp.full((tq, 1), NEG, F32), jnp.zeros((tq, 1), F32), jnp.zeros((tq, dv), F32)) for _ in range(2))
        carry = lax.fori_loop(0, n_full, functools.partial(step, masked=False), carry)
        carry = lax.fori_loop(n_full, nkb, functools.partial(step, masked=True), carry)
        o_parts = [acc / l for _, l, acc in carry]
        lse_parts = [m + jnp.log(l) for m, l, _ in carry]
        o_ref[...] = jnp.concatenate(o_parts, axis=1)
        lane = lax.broadcasted_iota(jnp.int32, (tq, LANES), 1)
        lse_ref[...] = jnp.where(lane == 0, lse_parts[0], jnp.where(lane == 1, lse_parts[1], 0.0))

    assert qa_off % (2 * da) == 0 and ka_off % (2 * da) == 0 and v_off % (2 * dv) == 0
    in_specs = [
        pl.BlockSpec((tq, 2 * da), lambda hp, i: (i, qa_off // (2 * da) + hp)),
        pl.BlockSpec((M, 2 * da), lambda hp, i: (0, ka_off // (2 * da) + hp)),
        pl.BlockSpec((M, 2 * dv), lambda hp, i: (0, v_off // (2 * dv) + hp)),
    ]
    args = [qa, ka, v]
    if has_bias:
        in_specs.append(pl.BlockSpec((tq, M), lambda hp, i: (i, 0)))
        args.append(bias)
    if has_b:
        assert qb_off % LANES == 0
        in_specs += [pl.BlockSpec((tq, LANES), lambda hp, i: (i, qb_off // LANES + hp)),
                     pl.BlockSpec((M, LANES), lambda hp, i: (0, 0))]
        args += [qb, kb]
    out_specs = [pl.BlockSpec((tq, 2 * dv), lambda hp, i: (i, hp)),
                 pl.BlockSpec((None, tq, LANES), lambda hp, i: (hp, i, 0))]
    out_shape = [jax.ShapeDtypeStruct((M, pairs * 2 * dv), F32), jax.ShapeDtypeStruct((pairs, M, LANES), F32)]
    return pl.pallas_call(
        body, name=name, out_shape=out_shape, grid=(pairs, M // tq), in_specs=in_specs, out_specs=out_specs,
        compiler_params=_cparams(("parallel", "arbitrary")),
    )(*args)


def attention_bwd(name, qa, qa_off, ka, ka_off, v, v_off, o, do, lse, *, da, dv, pairs, scale, M,
                  qb=None, qb_off=0, kb=None, bias=None):
    tq = min(256, M)
    tk = min(256, M)
    has_b = qb is not None
    has_bias = bias is not None
    dr = MLA_ROPE

    def body(*refs):
        refs = list(refs)
        bias_ref = refs.pop(6) if has_bias else None
        if has_b:
            qa_ref, ka_ref, v_ref, o_ref, do_ref, lse_ref, qb_ref, kb_ref, dqa_ref, dka_ref, dv_ref, dqb_ref, dkb_ref = refs
        else:
            qa_ref, ka_ref, v_ref, o_ref, do_ref, lse_ref, dqa_ref, dka_ref, dv_ref = refs
        hp = pl.program_id(0)
        i = pl.program_id(1)
        t0 = i * tq
        nkb = (t0 + tq + tk - 1) // tk
        n_full = nkb if has_bias else t0 // tk

        @pl.when(i == 0)
        def _():
            dka_ref[...] = jnp.zeros(dka_ref.shape, F32)
            dv_ref[...] = jnp.zeros(dv_ref.shape, F32)

        if has_b:
            @pl.when((i == 0) & (hp == 0))
            def _():
                dkb_ref[...] = jnp.zeros(dkb_ref.shape, F32)

        q1s = [qa_ref[:, hh * da:(hh + 1) * da].astype(BF16) for hh in range(2)]
        q2s = [qb_ref[:, hh * dr:(hh + 1) * dr].astype(BF16) if has_b else None for hh in range(2)]
        do_bfs = [do_ref[:, hh * dv:(hh + 1) * dv].astype(BF16) for hh in range(2)]
        rowdots = [jnp.sum(do_ref[:, hh * dv:(hh + 1) * dv] * o_ref[:, hh * dv:(hh + 1) * dv], axis=1, keepdims=True) for hh in range(2)]
        lses = [lse_ref[:, hh:hh + 1] for hh in range(2)]

        def step(kbi, carry, masked):
            ks = pl.multiple_of(kbi * tk, tk)
            k2 = kb_ref[pl.ds(ks, tk), 0:dr].astype(BF16) if has_b else None
            if has_bias:
                extra = bias_ref[:, pl.ds(ks, tk)]
            elif masked:
                delta = (t0 + lax.broadcasted_iota(jnp.int32, (tq, tk), 0)) - (ks + lax.broadcasted_iota(jnp.int32, (tq, tk), 1))
            new, dkb_part = [], None
            for hh, (dq1, dq2) in enumerate(carry):
                k1 = ka_ref[pl.ds(ks, tk), hh * da:(hh + 1) * da].astype(BF16)
                s = _dot(q1s[hh], k1, "nt")
                if has_b:
                    s = s + _dot(q2s[hh], k2, "nt")
                s = s * scale
                if has_bias:
                    s = s + extra
                elif masked:
                    s = jnp.where(delta >= 0, s, NEG)
                p = jnp.exp(s - lses[hh])
                vv = v_ref[pl.ds(ks, tk), hh * dv:(hh + 1) * dv].astype(BF16)
                dp = _dot(do_bfs[hh], vv, "nt")
                ds = (p * (dp - rowdots[hh]) * scale).astype(BF16)
                dq1 = dq1 + _dot(ds, k1, "nn")
                dka_ref[pl.ds(ks, tk), hh * da:(hh + 1) * da] += _dot(ds, q1s[hh], "tn")
                dv_ref[pl.ds(ks, tk), hh * dv:(hh + 1) * dv] += _dot(p.astype(BF16), do_bfs[hh], "tn")
                if has_b:
                    dq2 = dq2 + _dot(ds, k2, "nn")
                    part = _dot(ds, q2s[hh], "tn")
                    dkb_part = part if dkb_part is None else dkb_part + part
                new.append((dq1, dq2))
            if has_b:
                dkb_ref[pl.ds(ks, tk), 0:dr] += dkb_part
            return tuple(new)

        carry = tuple((jnp.zeros((tq, da), F32), jnp.zeros((tq, dr), F32)) for _ in range(2))
        carry = lax.fori_loop(0, n_full, functools.partial(step, masked=False), carry)
        carry = lax.fori_loop(n_full, nkb, functools.partial(step, masked=True), carry)
        dqa_ref[...] = jnp.concatenate([c[0] for c in carry], axis=1).astype(dqa_ref.dtype)
        if has_b:
            dqb_ref[...] = jnp.concatenate([c[1] for c in carry], axis=1).astype(dqb_ref.dtype)

    in_specs = [
        pl.BlockSpec((tq, 2 * da), lambda hp, i: (i, qa_off // (2 * da) + hp)),
        pl.BlockSpec((M, 2 * da), lambda hp, i: (0, ka_off // (2 * da) + hp)),
        pl.BlockSpec((M, 2 * dv), lambda hp, i: (0, v_off // (2 * dv) + hp)),
        pl.BlockSpec((tq, 2 * dv), lambda hp, i: (i, hp)),
        pl.BlockSpec((tq, 2 * dv), lambda hp, i: (i, hp)),
        pl.BlockSpec((None, tq, LANES), lambda hp, i: (hp, i, 0)),
    ]
    args = [qa, ka, v, o, do, lse]
    if has_bias:
        in_specs.append(pl.BlockSpec((tq, M), lambda hp, i: (i, 0)))
        args.append(bias)
    out_specs = [pl.BlockSpec((tq, 2 * da), lambda hp, i: (i, hp)),
                 pl.BlockSpec((M, 2 * da), lambda hp, i: (0, hp)),
                 pl.BlockSpec((M, 2 * dv), lambda hp, i: (0, hp))]
    out_shape = [jax.ShapeDtypeStruct((M, pairs * 2 * da), BF16),
                 jax.ShapeDtypeStruct((M, pairs * 2 * da), F32),
                 jax.ShapeDtypeStruct((M, pairs * 2 * dv), F32)]
    if has_b:
        in_specs += [pl.BlockSpec((tq, LANES), lambda hp, i: (i, qb_off // LANES + hp)),
                     pl.BlockSpec((M, LANES), lambda hp, i: (0, 0))]
        args += [qb, kb]
        out_specs += [pl.BlockSpec((tq, LANES), lambda hp, i: (i, hp)), pl.BlockSpec((M, LANES), lambda hp, i: (0, 0))]
        out_shape += [jax.ShapeDtypeStruct((M, pairs * LANES), F32), jax.ShapeDtypeStruct((M, LANES), F32)]
    return pl.pallas_call(
        body, name=name, out_shape=out_shape, grid=(pairs, M // tq), in_specs=in_specs, out_specs=out_specs,
        compiler_params=_cparams(("arbitrary", "arbitrary")),
    )(*args)


def ssm_scan(name, xcat, acat, *, M, reverse=False, hcat=None):
    C2 = xcat.shape[1]
    cb = LANES
    tb = min(128, M)
    nblk = M // tb
    with_da = hcat is not None

    def body(*refs):
        if with_da:
            x_ref, a_ref, h_ref, o_ref, da_ref, p_ref = refs
        else:
            x_ref, a_ref, o_ref, p_ref = refs
        ar, ai = a_ref[:, :cb], a_ref[:, cb:]
        row = lax.broadcasted_iota(jnp.int32, (tb, cb), 0)

        def logscan(xr, xi):
            pr, pi = ar, ai
            d = 1
            while d < tb:
                shift = tb - d if reverse else d
                keep = (row < tb - d) if reverse else (row >= d)
                sr = jnp.where(keep, pltpu.roll(xr, shift, 0), 0.0)
                si = jnp.where(keep, pltpu.roll(xi, shift, 0), 0.0)
                xr, xi = xr + pr * sr - pi * si, xi + pr * si + pi * sr
                pr, pi = pr * pr - pi * pi, 2.0 * pr * pi
                d *= 2
            return xr, xi

        seed = row == (tb - 1 if reverse else 0)
        p0r, p0i = logscan(jnp.where(seed, ar, 0.0), jnp.where(seed, ai, 0.0))
        p_ref[:, :cb] = p0r
        p_ref[:, cb:] = p0i
        sub = lax.broadcasted_iota(jnp.int32, (8, cb), 0)
        edge = 0 if reverse else tb - 8
        pick = sub == (0 if reverse else 7)

        def blk(b, carry):
            cr, ci = carry
            bb = (nblk - 1 - b) if reverse else b
            t0 = pl.multiple_of(bb * tb, tb)
            hr, hi = logscan(x_ref[pl.ds(t0, tb), :cb], x_ref[pl.ds(t0, tb), cb:])
            pr, pi = p_ref[:, :cb], p_ref[:, cb:]
            o_ref[pl.ds(t0, tb), :cb] = hr + pr * cr - pi * ci
            o_ref[pl.ds(t0, tb), cb:] = hi + pr * ci + pi * cr
            te = pl.multiple_of(t0 + edge, 8)
            ncr = jnp.sum(jnp.where(pick, o_ref[pl.ds(te, 8), :cb], 0.0), axis=0, keepdims=True)
            nci = jnp.sum(jnp.where(pick, o_ref[pl.ds(te, 8), cb:], 0.0), axis=0, keepdims=True)
            return ncr, nci

        lax.fori_loop(0, nblk, blk, (jnp.zeros((1, cb), F32), jnp.zeros((1, cb), F32)))
        if with_da:
            first = lax.broadcasted_iota(jnp.int32, (M, cb), 0) >= 1
            hpr = jnp.where(first, pltpu.roll(h_ref[:, :cb], 1, 0), 0.0)
            hpi = jnp.where(first, pltpu.roll(h_ref[:, cb:], 1, 0), 0.0)
            lr, li = o_ref[:, :cb], o_ref[:, cb:]
            da_ref[:, :cb] = _colsum(lr * hpr + li * hpi)
            da_ref[:, cb:] = _colsum(li * hpr - lr * hpi)

    blk_spec = pl.BlockSpec((M, 2 * cb), lambda j: (0, j))
    vec_spec = pl.BlockSpec((1, 2 * cb), lambda j: (0, j))
    in_specs = [blk_spec, vec_spec] + ([blk_spec] if with_da else [])
    out_specs = [blk_spec] + ([vec_spec] if with_da else [])
    out_shape = [jax.ShapeDtypeStruct((M, C2), F32)] + ([jax.ShapeDtypeStruct((1, C2), F32)] if with_da else [])
    args = [xcat, acat] + ([hcat] if with_da else [])
    res = pl.pallas_call(
        body, name=name, out_shape=out_shape, grid=(C2 // (2 * cb),), in_specs=in_specs, out_specs=out_specs,
        scratch_shapes=[pltpu.VMEM((tb, 2 * cb), F32)], compiler_params=_cparams(("parallel",)),
    )(*args)
    return res if with_da else res[0]


def _ssm_param_fn(a_re, a_im, ldt, b_re, b_im):
    lr, li = jnp.minimum(a_re, -1e-4), a_im
    dt = jnp.exp(ldt)
    e, ang = jnp.exp(lr * dt), li * dt
    ar, ai = e * jnp.cos(ang), e * jnp.sin(ang)
    den = lr * lr + li * li
    nr, ni = ar - 1.0, ai
    cr, ci = (nr * lr + ni * li) / den, (ni * lr - nr * li) / den
    return ar, ai, cr * b_re - ci * b_im, cr * b_im + ci * b_re


def _whole(shape):
    return pl.BlockSpec(shape, functools.partial(lambda nd: (0,) * nd, nd=len(shape)))


def ssm_param_fwd(name, a_re, a_im, ldt, b_re, b_im):
    def body(*refs):
        res = _ssm_param_fn(*[r[...] for r in refs[:5]])
        for ref, val in zip(refs[5:], res):
            ref[...] = val

    ins = [a_re, a_im, ldt, b_re, b_im]
    outs = [a_re, a_re, b_re, b_re]
    return pl.pallas_call(
        body, name=name, out_shape=[jax.ShapeDtypeStruct(t.shape, F32) for t in outs],
        in_specs=[_whole(t.shape) for t in ins], out_specs=[_whole(t.shape) for t in outs], compiler_params=_cparams(),
    )(*ins)


def ssm_param_bwd(name, a_re, a_im, ldt, b_re, b_im, d_ar, d_ai, d_bbr, d_bbi):
    def body(*refs):
        _, vjp = jax.vjp(_ssm_param_fn, *[r[...] for r in refs[:5]])
        res = vjp(tuple(r[...] for r in refs[5:9]))
        for ref, val in zip(refs[9:], res):
            ref[...] = val

    ins = [a_re, a_im, ldt, b_re, b_im, d_ar, d_ai, d_bbr, d_bbi]
    outs = [a_re, a_im, ldt, b_re, b_im]
    return pl.pallas_call(
        body, name=name, out_shape=[jax.ShapeDtypeStruct(t.shape, F32) for t in outs],
        in_specs=[_whole(t.shape) for t in ins], out_specs=[_whole(t.shape) for t in outs], compiler_params=_cparams(),
    )(*ins)


ANY = pl.BlockSpec(memory_space=pl.ANY)


def _place():
    x, y, c = lax.axis_index("x"), lax.axis_index("y"), lax.axis_index("c")
    chips = [(1 - x, y), (x, 1 - y), (1 - x, 1 - y)]
    return x, y, c, chips


def cast_into_slot(name, w, kind, l=None):
    K, nn = w.shape[-2:]
    hr, hc = (K // 2, nn) if kind == "cols" else (K, nn // 2)
    tr = _pick(hr, max(16, min(512, ROW_BUDGET // (2 * hc * 6))), SUBLANES_BF16)
    nb = hr // tr

    def body(w_ref, o_ref):
        o_ref[...] = w_ref[...].astype(BF16)

    lead = () if l is None else (l,)
    if kind == "cols":
        in_spec = pl.BlockSpec((None,) * len(lead) + (tr, hc), lambda h, i: lead + (h * nb + i, 0))
    else:
        in_spec = pl.BlockSpec((None,) * len(lead) + (tr, hc), lambda h, i: lead + (i, h))
    return pl.pallas_call(
        body, name=name, out_shape=jax.ShapeDtypeStruct((N_CHIPS, 2, hr, hc), BF16), grid=(2, nb), in_specs=[in_spec],
        out_specs=pl.BlockSpec((None, None, tr, hc), lambda h, i: (2 * lax.axis_index("x") + lax.axis_index("y"), h, i, 0)),
        compiler_params=_cparams(("parallel", "parallel")),
    )(w)


HBM_SPEC = pl.BlockSpec(memory_space=pltpu.HBM)
SEM_SPEC = pl.BlockSpec(memory_space=pltpu.SEMAPHORE)
SPLIT_PARAMS = pltpu.CompilerParams(has_side_effects=pltpu.SideEffectType.DATAFLOW_SIDE_EFFECTING)


def _in_hbm(t):
    return pltpu.with_memory_space_constraint(t, pltpu.HBM)


def split_start(name, plan, n, bufs, fresh, carrier):
    nb, nf = len(bufs), len(fresh)

    def body(*refs):
        outs = refs[nb + 1:]
        for i, (s, d, dev) in enumerate(plan(list(outs[2:2 + nb + nf]))):
            pltpu.make_async_remote_copy(src_ref=s, dst_ref=d, send_sem=outs[0].at[i], recv_sem=outs[1].at[i],
                                         device_id=dev, device_id_type=MESH).start()

    hbm = lambda t: pltpu.HBM(t.shape, t.dtype)
    res = pl.pallas_call(
        body, name=name,
        out_shape=(pltpu.SemaphoreType.DMA((n,)), pltpu.SemaphoreType.DMA((n,)), *[hbm(t) for t in bufs], *[hbm(t) for t in fresh], hbm(carrier)),
        in_specs=[HBM_SPEC] * (nb + 1), out_specs=(SEM_SPEC, SEM_SPEC) + (HBM_SPEC,) * (nb + nf + 1),
        input_output_aliases={**{i: 2 + i for i in range(nb)}, nb: 2 + nb + nf}, compiler_params=SPLIT_PARAMS,
    )(*[_in_hbm(t) for t in bufs], _in_hbm(carrier))
    return (res[0], res[1]), list(res[2:2 + nb]), list(res[2 + nb:2 + nb + nf]), res[2 + nb + nf]


def split_wait(name, plan, sems, bufs, carrier):
    nb = len(bufs)

    def body(*refs):
        for i, (s, d, dev) in enumerate(plan(list(refs[:nb]))):
            cp = pltpu.make_async_remote_copy(src_ref=s, dst_ref=d, send_sem=refs[nb].at[i], recv_sem=refs[nb + 1].at[i],
                                              device_id=dev, device_id_type=MESH)
            cp.wait_send()
            cp.wait_recv()

    hbm = lambda t: pltpu.HBM(t.shape, t.dtype)
    res = pl.pallas_call(
        body, name=name, out_shape=(*[hbm(t) for t in bufs], hbm(carrier)),
        in_specs=[HBM_SPEC] * nb + [SEM_SPEC, SEM_SPEC, HBM_SPEC], out_specs=(HBM_SPEC,) * (nb + 1),
        input_output_aliases={**{i: i for i in range(nb)}, nb + 2: nb}, compiler_params=SPLIT_PARAMS,
    )(*bufs, sems[0], sems[1], carrier)
    return list(res[:nb]), res[nb]


def _me_sib_chips():
    x, y, c, chips = _place()
    return 2 * x + y, c, (x, y, 1 - c), chips


def plan_gather_ici(refs):
    me, c, _, chips = _me_sib_chips()
    return [(r.at[me, c], r.at[me, c], (chip[0], chip[1], c)) for r in refs for chip in chips]


def plan_gather_pass(refs):
    _, c, sib, chips = _me_sib_chips()
    return [(r.at[2 * chip[0] + chip[1], c], r.at[2 * chip[0] + chip[1], c], sib) for r in refs for chip in chips]


def plan_pair(n_arrays):
    def plan(refs):
        _, c, sib, _ = _me_sib_chips()
        return [(refs[a].at[1 - c], refs[n_arrays + a], sib) for a in range(n_arrays)]
    return plan


def plan_chips(n_arrays):
    def plan(refs):
        _, c, _, chips = _me_sib_chips()
        return [(refs[a].at[2 * chip[0] + chip[1]], refs[n_arrays + a].at[k], (chip[0], chip[1], c))
                for a in range(n_arrays) for k, chip in enumerate(chips)]
    return plan


def plan_share(n_arrays):
    def plan(refs):
        _, _, sib, _ = _me_sib_chips()
        return [(refs[a], refs[n_arrays + a], sib) for a in range(n_arrays)]
    return plan


def swap_with_sibling(name, src, pick_other_half):
    shape = src.shape[1:] if pick_other_half else src.shape

    def body(src_ref, out_ref, ssem, rsem):
        x, y, c, _ = _place()
        cp = pltpu.make_async_remote_copy(src_ref=src_ref.at[1 - c] if pick_other_half else src_ref, dst_ref=out_ref,
                                          send_sem=ssem, recv_sem=rsem, device_id=(x, y, 1 - c), device_id_type=MESH)
        cp.start()
        cp.wait()

    return pl.pallas_call(
        body, name=name, out_shape=jax.ShapeDtypeStruct(shape, src.dtype), in_specs=[ANY], out_specs=ANY,
        scratch_shapes=[pltpu.SemaphoreType.DMA(()), pltpu.SemaphoreType.DMA(())],
    )(src)


def exchange_chips(name, src, per_chip):
    shape = src.shape[1:] if per_chip else src.shape

    def body(src_ref, out_ref, send_sems, recv_sems):
        x, y, c, chips = _place()
        cps = []
        for k, chip in enumerate(chips):
            s = src_ref.at[2 * chip[0] + chip[1]] if per_chip else src_ref
            cps.append(pltpu.make_async_remote_copy(src_ref=s, dst_ref=out_ref.at[k], send_sem=send_sems.at[k], recv_sem=recv_sems.at[k],
                                                    device_id=(chip[0], chip[1], c), device_id_type=MESH))
        for cp in cps:
            cp.start()
        for cp in cps:
            cp.wait()

    return pl.pallas_call(
        body, name=name, out_shape=jax.ShapeDtypeStruct((3,) + shape, src.dtype), in_specs=[ANY], out_specs=ANY,
        scratch_shapes=[pltpu.SemaphoreType.DMA((3,)), pltpu.SemaphoreType.DMA((3,))],
    )(src)


def pair_sum(name, p, got):
    _, _, rh, cw = p.shape
    tr = _pick(rh, max(16, min(512, ROW_BUDGET // (2 * cw * 10))), SUBLANES_BF16)

    def body(p_ref, got_ref, s_ref, own_ref):
        j = pl.program_id(1)
        tot = p_ref[...].astype(F32) + got_ref[...].astype(F32)
        s_ref[...] = tot.astype(BF16)

        @pl.when(j == 2 * lax.axis_index("x") + lax.axis_index("y"))
        def _():
            own_ref[...] = tot

    return pl.pallas_call(
        body, name=name, grid=(rh // tr, N_CHIPS),
        in_specs=[pl.BlockSpec((None, None, tr, cw), lambda i, j: (lax.axis_index("c"), j, i, 0)),
                  pl.BlockSpec((None, tr, cw), lambda i, j: (j, i, 0))],
        out_specs=[pl.BlockSpec((None, tr, cw), lambda i, j: (j, i, 0)),
                   pl.BlockSpec((tr, cw), lambda i, j: (i, 0))],
        out_shape=[jax.ShapeDtypeStruct((N_CHIPS, rh, cw), BF16), jax.ShapeDtypeStruct((rh, cw), F32)],
        compiler_params=_cparams(("arbitrary", "arbitrary")),
    )(p, got)


def chips_sum(name, own, parts):
    rh, cw = own.shape
    parts = parts.reshape(3 * rh, cw)
    return rowwise(name, lambda o, a, b, c: (((o + a.astype(F32)) + b.astype(F32)) + c.astype(F32),),
                   [(own, cw, 0), (parts, cw, 0, 0), (parts, cw, 0, rh), (parts, cw, 0, 2 * rh)], [], [(cw, F32)], M=rh)[0]


def all_reduce_small(buf):
    r = buf.shape[0]
    got = swap_with_sibling("ar_pair", buf, False)
    chip = rowwise("ar_pairsum", lambda a, b: (a + b,), [(buf, LANES, 0), (got, LANES, 0)], [], [(LANES, F32)], M=r)[0]
    parts = exchange_chips("ar_chips", chip, False).reshape(3 * r, LANES)
    return rowwise("ar_sum", lambda o, fx, fy, fxy: ((o + fy) + (fx + fxy),),
                   [(chip, LANES, 0), (parts, LANES, 0, 0), (parts, LANES, 0, r), (parts, LANES, 0, 2 * r)], [], [(LANES, F32)], M=r)[0]


def _adam_fn(w, g, m, v):
    m = ADAM_B1 * m + (1.0 - ADAM_B1) * g
    v = ADAM_B2 * v + (1.0 - ADAM_B2) * (g * g)
    m_hat = m / (1.0 - ADAM_B1 ** ADAM_STEP)
    v_hat = v / (1.0 - ADAM_B2 ** ADAM_STEP)
    return -ADAM_LR * (m_hat / (jnp.sqrt(v_hat) + ADAM_EPS) + ADAM_WD * w), m, v


def adamw(name, w, g, m, v):
    r, cw = w.shape
    return rowwise(name, _adam_fn, [(t, cw, 0) for t in (w, g, m, v)], [], [(cw, F32)] * 3, M=r)


def adamw_layers(name, w, m, v, mines, theirs, kind):
    L, K, nn = w.shape
    hr, hc = (K // 2, nn) if kind == "cols" else (K, nn // 2)
    tr = _pick(hr, max(8, min(256, MM_BUDGET // (2 * hc * 4 * (7 + 2 * L)))), 8)
    nb = hr // tr

    def body(*refs):
        w_ref, m_ref, v_ref = refs[:3]
        outs = refs[3 + 2 * L:]
        l, mine_here = pl.program_id(0), pl.program_id(1) == lax.axis_index("c")
        g = jnp.zeros((tr, hc), F32)
        for ll in range(L):
            g = jnp.where(l == ll, jnp.where(mine_here, refs[3 + ll][...], refs[3 + L + ll][...]), g)
        outs[0][...] = g
        outs[1][...], outs[2][...], outs[3][...] = _adam_fn(w_ref[...], g, m_ref[...], v_ref[...])

    if kind == "cols":
        full = pl.BlockSpec((None, tr, hc), lambda l, h, i: (l, h * nb + i, 0))
    else:
        full = pl.BlockSpec((None, tr, hc), lambda l, h, i: (l, i, h))
    def half_spec(ll, mine):
        def imap(l, h, i):
            here = (l == ll) & ((h == lax.axis_index("c")) == mine)
            return (jnp.where(here, i, 0), 0)
        return pl.BlockSpec((tr, hc), imap)

    halves = [half_spec(ll, True) for ll in range(L)] + [half_spec(ll, False) for ll in range(L)]
    return pl.pallas_call(
        body, name=name, grid=(L, 2, nb), in_specs=[full] * 3 + halves, out_specs=[full] * 4,
        out_shape=[jax.ShapeDtypeStruct((L, K, nn), F32)] * 4, compiler_params=_cparams(("parallel", "parallel", "parallel")),
    )(w, m, v, *mines, *theirs)


class Dims:
    def __init__(self, x, g_q, g_kv, g_out_mla, g_out_ssm, g_out_dil, ff):
        self.M, self.D = x.shape[-2], x.shape[-1]
        self.QL, self.KVL = g_q.shape[-1], g_kv.shape[-1]
        self.MW, self.SW, self.DW = g_out_mla.shape[-1], g_out_ssm.shape[-1], g_out_dil.shape[-1]
        self.H = self.MW // MLA_V
        self.FF = ff
        self.G = self.SW // SSM_GROUP
        self.C = self.G * SSM_STATE
        self.o_cq, self.o_u = 0, self.QL
        self.o_qd = self.o_u + self.SW
        self.o_kd = self.o_qd + self.DW
        self.o_vd = self.o_kd + self.DW
        self.o_ckv = self.o_vd + self.DW
        self.o_kr = self.o_ckv + self.KVL
        self.PW = -(-(self.o_kr + MLA_ROPE) // (4 * LANES)) * (4 * LANES)
        assert self.o_u % self.SW == 0 and self.o_qd % LANES == 0 and self.o_ckv % self.KVL == 0 and self.o_kr % LANES == 0
        assert self.H % 2 == 0 and self.DW % LANES == 0 and self.C % LANES == 0
        self.QW = self.H * (MLA_NOPE + MLA_ROPE)
        self.KVW = self.H * (MLA_NOPE + MLA_V)
        sizes = [self.QL, self.KVL, MLA_ROPE, self.SW, self.DW, self.DW, self.DW]
        starts = np.concatenate([[0], np.cumsum(sizes)[:-1]])
        self.ref_cols = {n: (int(s), int(z)) for n, s, z in zip(["cq", "ckv", "kr", "u", "qd", "kd", "vd"], starts, sizes)}
        self.INW = int(sum(sizes))
        self.new_order = ["cq", "u", "qd", "kd", "vd", "ckv", "kr"]
        src = np.concatenate([np.arange(self.ref_cols[n][0], self.ref_cols[n][0] + self.ref_cols[n][1]) for n in self.new_order])
        self.src_in = np.concatenate([src, -np.ones(self.PW - self.INW, np.int64)])
        self.src_q = self._heads_split(self.H, MLA_NOPE, MLA_ROPE)
        self.src_kv = self._heads_split(self.H, MLA_NOPE, MLA_V)

    @staticmethod
    def _heads_split(h, d1, d2):
        first = (np.arange(h)[:, None] * (d1 + d2) + np.arange(d1)[None, :]).reshape(-1)
        second = (np.arange(h)[:, None] * (d1 + d2) + d1 + np.arange(d2)[None, :]).reshape(-1)
        return np.concatenate([first, second])


def _regroup_in(dm, w):
    parts = [w[..., dm.ref_cols[n][0]:dm.ref_cols[n][0] + dm.ref_cols[n][1]] for n in dm.new_order]
    pad = dm.PW - dm.INW
    return jnp.concatenate(parts + [jnp.zeros(w.shape[:-1] + (pad,), w.dtype)], axis=-1)


def _ungroup_in(dm, w):
    off, pieces = 0, {}
    for n in dm.new_order:
        pieces[n] = w[..., off:off + dm.ref_cols[n][1]]
        off += dm.ref_cols[n][1]
    return jnp.concatenate([pieces[n] for n in ["cq", "ckv", "kr", "u", "qd", "kd", "vd"]], axis=-1)


def _split_heads(w, h, d1):
    t = w.reshape(w.shape[:-1] + (h, -1))
    return jnp.concatenate([t[..., :d1].reshape(w.shape[:-1] + (-1,)), t[..., d1:].reshape(w.shape[:-1] + (-1,))], axis=-1)


def _merge_heads(w, h, d1):
    a = w[..., :h * d1].reshape(w.shape[:-1] + (h, d1))
    b = w[..., h * d1:].reshape(w.shape[:-1] + (h, -1))
    return jnp.concatenate([a, b], axis=-1).reshape(w.shape[:-1] + (-1,))


def _cat_cols(re, im):
    r, c = re.shape
    return jnp.stack([re.reshape(r, c // LANES, LANES), im.reshape(r, c // LANES, LANES)], axis=2).reshape(r, 2 * c)


def _uncat_cols(cat):
    r, c2 = cat.shape
    t = cat.reshape(r, c2 // (2 * LANES), 2, LANES)
    return t[:, :, 0].reshape(r, c2 // 2), t[:, :, 1].reshape(r, c2 // 2)


def _block_diag(t, g):
    _, a, b = t.shape
    eye = jnp.eye(g, dtype=bool)[:, None, :, None]
    return jnp.where(eye, t[:, :, None, :], 0).reshape(g * a, g * b)


def _diag_blocks(m, g):
    a, b = m.shape[0] // g, m.shape[1] // g
    eye = jnp.eye(g, dtype=m.dtype)[:, None, :, None]
    return jnp.sum(m.reshape(g, a, g, b) * eye, axis=2)


def _rope_tables(dm):
    half = MLA_ROPE // 2
    inv_freq = ROPE_THETA ** (-jnp.arange(half, dtype=F32) / half)
    ang = jnp.arange(dm.M, dtype=F32)[:, None] * inv_freq[None, :]
    cos = jnp.concatenate([jnp.cos(ang), jnp.cos(ang)], axis=1)
    sin = jnp.concatenate([-jnp.sin(ang), jnp.sin(ang)], axis=1)
    return jnp.tile(cos, (1, dm.H)), jnp.tile(sin, (1, dm.H)), jnp.tile(cos, (1, LANES // MLA_ROPE)), jnp.tile(sin, (1, LANES // MLA_ROPE))


def _rope(x, cos, sin):
    return x * cos + _swap_halves(x, MLA_ROPE // 2) * sin


def _rope_t(d, cos, sin):
    return d * cos + _swap_halves(d * sin, MLA_ROPE // 2)


def _ssm_layer_params(dm, a_re, a_im, log_dt, b_re, b_im):
    flat = lambda t: t.reshape(1, dm.C)
    ldt = jnp.repeat(log_dt, SSM_STATE).reshape(1, dm.C)
    bt = lambda t: jnp.transpose(t, (2, 0, 1)).reshape(SSM_GROUP, dm.C)
    return flat(a_re), flat(a_im), ldt, bt(b_re), bt(b_im)


def layer_forward(dm, l, x, lw, sp, tabs, hook_q, hook_mid):
    M, D = dm.M, dm.D
    n = lambda s: f"{s}_l{l}"
    sv = {"x_in": x}
    h1 = rms_fwd(n("rms_mix"), x, D, 0, lw["g_mix"], M=M)
    proj = matmul(n("in_proj"), h1, lw["w_in"], "nn", M=M, N=dm.PW, K=D)
    sv.update(h1=h1, proj=proj)
    cqn = rms_fwd(n("rms_q"), proj, dm.QL, dm.o_cq, lw["g_q"], M=M)
    q = matmul(n("q_up"), cqn, lw["w_uq"], "nn", M=M, N=dm.QW, K=dm.QL)
    ckvn = rms_fwd(n("rms_kv"), proj, dm.KVL, dm.o_ckv, lw["g_kv"], M=M)
    kv = matmul(n("kv_up"), ckvn, lw["w_ukv"], "nn", M=M, N=dm.KVW, K=dm.KVL, out_dtype=BF16)
    cosq, sinq, cosk, sink = tabs[:4]
    nw = dm.H * MLA_NOPE

    def rope_fn(qb, kb, cq, sq, ck, sk):
        return jnp.concatenate([qb[:, :nw], _rope(qb[:, nw:], cq, sq)], axis=1), _rope(kb, ck, sk)

    pw = dm.H * MLA_ROPE
    q_bf, kpe = rowwise(n("rope"), rope_fn, [(q, dm.QW, 0), (proj, LANES, dm.o_kr), (cosq, pw, 0), (sinq, pw, 0), (cosk, LANES, 0), (sink, LANES, 0)],
                        [], [(dm.QW, BF16), (LANES, BF16)], M=M)
    mla_scale = (MLA_NOPE + MLA_ROPE) ** -0.5
    o_mla, lse_mla = attention_fwd(n("mla_fwd"), q_bf, 0, kv, 0, kv, nw, da=MLA_NOPE, dv=MLA_V, pairs=dm.H // 2, scale=mla_scale,
                                   M=M, qb=q_bf, qb_off=nw, kb=kpe)
    o_mla = hook_q(o_mla)
    sv.update(cqn=cqn, ckvn=ckvn, q_bf=q_bf, kv=kv, kpe=kpe, o_mla=o_mla, lse_mla=lse_mla)
    bu = matmul(n("ssm_bu"), proj, sp["bcat"], "nn", M=M, N=2 * dm.C, K=dm.SW, a_off=(0, dm.o_u))
    hcat = ssm_scan(n("ssm_scan"), bu, sp["acat"], M=M)
    ylin = matmul(n("ssm_y"), hcat, sp["ccat"], "nn", M=M, N=dm.SW, K=2 * dm.C)
    yg = rowwise(n("ssm_gelu"), lambda y, u, d: (_gelu(y + d * u),), [(ylin, dm.SW, 0), (proj, dm.SW, dm.o_u)], [lw["d_skip"]],
                 [(dm.SW, BF16)], M=M)[0]
    z = matmul(n("ssm_glu"), yg, lw["w_glu"], "nn", w=("cols", 2 * dm.SW // N_CHIPS), M=M, N=2 * dm.SW, K=dm.SW)
    sw = dm.SW

    def glu_fn(zb, b):
        zz = zb + b
        return (zz[:, :sw] * jax.nn.sigmoid(zz[:, sw:]),)

    o_ssm = rowwise(n("ssm_gate"), glu_fn, [(z, 2 * sw, 0)], [lw["b_glu"]], [(sw, F32)], M=M)[0]
    sv.update(hcat=hcat, ylin=ylin, yg=yg, z=z, o_ssm=o_ssm)
    o_dil, lse_dil = attention_fwd(n("dil_fwd"), proj, dm.o_qd, proj, dm.o_kd, proj, dm.o_vd, da=DIL_HEAD, dv=DIL_HEAD, pairs=dm.DW // LANES,
                                   scale=DIL_HEAD ** -0.5, M=M, bias=tabs[4])
    sv.update(o_dil=o_dil, lse_dil=lse_dil)
    yn = rowwise(n("out_norm"), lambda a, b, c, ga, gb, gc: (jnp.concatenate([_rms(a, ga), _rms(b, gb), _rms(c, gc)], axis=1),),
                 [(o_mla, dm.MW, 0), (o_ssm, dm.SW, 0), (o_dil, dm.DW, 0)], [lw["g_out_mla"], lw["g_out_ssm"], lw["g_out_dil"]],
                 [(D, BF16)], M=M)[0]
    yn = hook_mid(yn, lw)
    x_mid = matmul(n("out_proj"), yn, lw["w_o"], "nn", w=("rows", D // N_CHIPS), M=M, N=D, K=D, add=x)
    h2 = rms_fwd(n("rms_ffn"), x_mid, D, 0, lw["g_ffn"], M=M)
    ffs = dm.FF // N_CHIPS
    gate = matmul(n("ffn_gate"), h2, lw["w_gate"], "nn", w=("cols", ffs), M=M, N=dm.FF, K=D, out_dtype=BF16)
    up = matmul(n("ffn_up"), h2, lw["w_up"], "nn", w=("cols", ffs), M=M, N=dm.FF, K=D, out_dtype=BF16)

    def act_fn(gb, ub):
        gf = gb.astype(F32)
        return (gf * jax.nn.sigmoid(gf) * ub.astype(F32),)

    act = rowwise(n("ffn_act"), act_fn, [(gate, dm.FF, 0), (up, dm.FF, 0)], [], [(dm.FF, BF16)], M=M)[0]
    x_out = matmul(n("ffn_down"), act, lw["w_down"], "nn", w=("rows",ffs), M=M, N=D, K=dm.FF, add=x_mid)
    sv.update(yn=yn, x_mid=x_mid, h2=h2, gate=gate, up=up, act=act)
    return x_out, sv


def layer_backward(dm, l, dx, lw, sp, tabs, sv, hook_a, hook_m, hook_b):
    M, D = dm.M, dm.D
    n = lambda s: f"{s}_l{l}"
    g = {}
    ffs = dm.FF // N_CHIPS
    dact = matmul(n("ffn_down_dx"), dx, lw["w_down"], "nt", w=("rows", ffs), M=M, N=dm.FF, K=D, out_dtype=BF16)
    g["w_down"] = matmul(n("ffn_down_dw"), sv["act"], dx, "tn", M=dm.FF, N=D, K=M, out_dtype=BF16, into=("rows", ffs))

    def act_bwd(gb, ub, db):
        _, vjp = jax.vjp(lambda a, b: a * jax.nn.sigmoid(a) * b, gb.astype(F32), ub.astype(F32))
        return vjp(db.astype(F32))

    dgate, dup = rowwise(n("ffn_act_bwd"), act_bwd, [(sv["gate"], dm.FF, 0), (sv["up"], dm.FF, 0), (dact, dm.FF, 0)], [],
                         [(dm.FF, BF16), (dm.FF, BF16)], M=M)
    dh2 = matmul(n("ffn_gate_dx"), dgate, lw["w_gate"], "nt", w=("cols",ffs), M=M, N=D, K=dm.FF)
    dh2 = matmul(n("ffn_up_dx"), dup, lw["w_up"], "nt", w=("cols",ffs), M=M, N=D, K=dm.FF, add=dh2)
    g["w_gate"] = matmul(n("ffn_gate_dw"), sv["h2"], dgate, "tn", M=D, N=dm.FF, K=M, out_dtype=BF16, into=("cols", ffs))
    g["w_up"] = matmul(n("ffn_up_dw"), sv["h2"], dup, "tn", M=D, N=dm.FF, K=M, out_dtype=BF16, into=("cols", ffs))
    dx_mid, g["g_ffn"] = rms_bwd(n("rms_ffn_bwd"), sv["x_mid"], D, 0, lw["g_ffn"], dh2, dx, M=M)
    dx_mid = hook_a(dx_mid, g)
    dyn = matmul(n("out_proj_dx"), dx_mid, lw["w_o"], "nt", w=("rows", D // N_CHIPS), M=M, N=D, K=D)
    g["w_o"] = matmul(n("out_proj_dw"), sv["yn"], dx_mid, "tn", M=D, N=D, K=M, out_dtype=BF16, into=("rows", D // N_CHIPS))
    mw, sw, dw = dm.MW, dm.SW, dm.DW

    def out_norm_bwd(a, b, c, dy, ga, gb, gc):
        res, sums = [], []
        for t, gg, lo, hi in ((a, ga, 0, mw), (b, gb, mw, mw + sw), (c, gc, mw + sw, mw + sw + dw)):
            _, vjp = jax.vjp(_rms, t, gg)
            dt, dg = vjp(dy[:, lo:hi])
            res.append(dt)
            sums.append(dg)
        return res + sums

    do_mla, do_ssm, do_dil, g["g_out_mla"], g["g_out_ssm"], g["g_out_dil"] = rowwise(
        n("out_norm_bwd"), out_norm_bwd, [(sv["o_mla"], mw, 0), (sv["o_ssm"], sw, 0), (sv["o_dil"], dw, 0), (dyn, D, 0)],
        [lw["g_out_mla"], lw["g_out_ssm"], lw["g_out_dil"]], [(mw, F32), (sw, F32), (dw, F32)], [mw, sw, dw], M=M)
    proj = sv["proj"]
    dqd, dkd, dvd = attention_bwd(n("dil_bwd"), proj, dm.o_qd, proj, dm.o_kd, proj, dm.o_vd, sv["o_dil"], do_dil, sv["lse_dil"],
                                  da=DIL_HEAD, dv=DIL_HEAD, pairs=dw // LANES, scale=DIL_HEAD ** -0.5, M=M, bias=tabs[4])
    do_ssm = hook_m(do_ssm, g)
    def glu_bwd(zb, db, b):
        _, vjp = jax.vjp(lambda zz, bb: (zz + bb)[:, :sw] * jax.nn.sigmoid((zz + bb)[:, sw:]), zb, b)
        return vjp(db)

    dz, g["b_glu"] = rowwise(n("ssm_gate_bwd"), glu_bwd, [(sv["z"], 2 * sw, 0), (do_ssm, sw, 0)], [lw["b_glu"]], [(2 * sw, BF16)], [2 * sw], M=M)
    dyg = matmul(n("ssm_glu_dx"), dz, lw["w_glu"], "nt", w=("cols", 2 * sw // N_CHIPS), M=M, N=sw, K=2 * sw)
    g["w_glu"] = matmul(n("ssm_glu_dw"), sv["yg"], dz, "tn", M=sw, N=2 * sw, K=M, out_dtype=BF16, into=("cols", 2 * sw // N_CHIPS))

    def gelu_bwd(y, u, dy, d):
        _, vjp = jax.vjp(lambda yy, uu, dd: _gelu(yy + dd * uu), y, u, d)
        return vjp(dy)

    dylin, du1, g["d_skip"] = rowwise(n("ssm_gelu_bwd"), gelu_bwd, [(sv["ylin"], sw, 0), (proj, sw, dm.o_u), (dyg, sw, 0)], [lw["d_skip"]],
                                      [(sw, BF16), (sw, F32)], [sw], M=M)
    seed = matmul(n("ssm_y_dx"), dylin, sp["ccat"], "nt", M=M, N=2 * dm.C, K=sw)
    d_ccat = matmul(n("ssm_y_dw"), sv["hcat"], dylin, "tn", M=2 * dm.C, N=sw, K=M)
    lam, d_acat = ssm_scan(n("ssm_scan_bwd"), seed, sp["acat_conj"], M=M, reverse=True, hcat=sv["hcat"])
    du = matmul(n("ssm_bu_dx"), lam, sp["bcat"], "nt", M=M, N=sw, K=2 * dm.C, add=du1, out_dtype=BF16)
    d_bcat = matmul(n("ssm_bu_dw"), proj, lam, "tn", M=sw, N=2 * dm.C, K=M, a_off=(0, dm.o_u))
    g["ssm_raw"] = (d_acat, d_bcat, d_ccat)
    nw = dm.H * MLA_NOPE
    dqn, dkn, dv_, dqp, dkp = attention_bwd(n("mla_bwd"), sv["q_bf"], 0, sv["kv"], 0, sv["kv"], nw, sv["o_mla"], do_mla, sv["lse_mla"],
                                            da=MLA_NOPE, dv=MLA_V, pairs=dm.H // 2, scale=(MLA_NOPE + MLA_ROPE) ** -0.5, M=M,
                                            qb=sv["q_bf"], qb_off=nw, kb=sv["kpe"])
    cosq, sinq, cosk, sink = tabs[:4]
    pw = dm.H * MLA_ROPE
    dqp_u, dkr = rowwise(n("rope_bwd"), lambda a, b, cq, sq, ck, sk: (_rope_t(a, cq, sq), _rope_t(b, ck, sk)),
                         [(dqp, pw, 0), (dkp, LANES, 0), (cosq, pw, 0), (sinq, pw, 0), (cosk, LANES, 0), (sink, LANES, 0)], [],
                         [(pw, BF16), (LANES, BF16)], M=M)
    dq = lane_concat(n("dq_cat"), [dqn, dqp_u], M=M)
    dkv = lane_concat(n("dkv_cat"), [dkn, dv_], M=M)
    dcqn = matmul(n("q_up_dx"), dq, lw["w_uq"], "nt", M=M, N=dm.QL, K=dm.QW)
    g["w_uq"] = matmul(n("q_up_dw"), sv["cqn"], dq, "tn", M=dm.QL, N=dm.QW, K=M, out_dtype=BF16)
    dckvn = matmul(n("kv_up_dx"), dkv, lw["w_ukv"], "nt", M=M, N=dm.KVL, K=dm.KVW)
    g["w_ukv"] = matmul(n("kv_up_dw"), sv["ckvn"], dkv, "tn", M=dm.KVL, N=dm.KVW, K=M, out_dtype=BF16)
    dcq, g["g_q"] = rms_bwd(n("rms_q_bwd"), proj, dm.QL, dm.o_cq, lw["g_q"], dcqn, M=M, out_dtype=BF16)
    dckv, g["g_kv"] = rms_bwd(n("rms_kv_bwd"), proj, dm.KVL, dm.o_ckv, lw["g_kv"], dckvn, M=M, out_dtype=BF16)
    dproj = hook_b(lane_concat(n("dproj_cat"), [dcq, du, dqd, dkd, dvd, dckv, dkr], M=M, pad_to=dm.PW), g)
    dh1 = matmul(n("in_proj_dx"), dproj, lw["w_in"], "nt", M=M, N=D, K=dm.PW)
    g["w_in"] = matmul(n("in_proj_dw"), sv["h1"], dproj, "tn", M=D, N=dm.PW, K=M, out_dtype=BF16)
    dx_in, g["g_mix"] = rms_bwd(n("rms_mix_bwd"), sv["x_in"], D, 0, lw["g_mix"], dh1, dx_mid, M=M)
    return dx_in, g


def layer_params(dm, small, l):
    lw = {k: small[k][l].reshape(1, -1) for k in ("g_mix", "g_q", "g_kv", "b_glu", "g_out_mla", "g_out_ssm", "g_out_dil", "g_ffn", "d_skip")}
    raw = _ssm_layer_params(dm, small["a_re"][l], small["a_im"][l], small["log_dt"][l], small["b_re"][l], small["b_im"][l])
    ar, ai, bbr, bbi = ssm_param_fwd(f"ssm_param_l{l}", *raw)
    g_ = dm.G
    bd = lambda t: _block_diag(jnp.transpose(t.reshape(SSM_GROUP, g_, SSM_STATE), (1, 0, 2)), g_)
    cd = lambda t: _block_diag(jnp.transpose(t, (0, 2, 1)), g_)
    cre, cim = cd(small["c_re"][l]), cd(small["c_im"][l])
    sp = {"acat": _cat_cols(ar, ai), "acat_conj": _cat_cols(ar, -ai),
          "bcat": _cat_cols(bd(bbr), bd(bbi)).astype(BF16),
          "ccat": _cat_cols(cre.T, -cim.T).T.astype(BF16)}
    return lw, sp, raw


def ssm_param_grads(dm, l, g, raw):
    d_acat, d_bcat, d_ccat = g.pop("ssm_raw")
    d_ar, d_ai = _uncat_cols(d_acat)
    dbr, dbi = _uncat_cols(d_bcat)
    g_ = dm.G
    to_rows = lambda t: jnp.transpose(_diag_blocks(t, g_), (1, 0, 2)).reshape(SSM_GROUP, dm.C)
    da_re, da_im, dldt, db_re, db_im = ssm_param_bwd(f"ssm_param_bwd_l{l}", *raw, d_ar, d_ai, to_rows(dbr), to_rows(dbi))
    dcr, dci = _uncat_cols(d_ccat.T)
    g["a_re"], g["a_im"] = da_re.reshape(g_, SSM_STATE), da_im.reshape(g_, SSM_STATE)
    g["log_dt"] = jnp.sum(dldt.reshape(g_, SSM_STATE), axis=1)
    from_rows = lambda t: jnp.transpose(t.reshape(SSM_GROUP, g_, SSM_STATE), (1, 2, 0))
    g["b_re"], g["b_im"] = from_rows(db_re), from_rows(db_im)
    g["c_re"] = jnp.transpose(_diag_blocks(dcr.T, g_), (0, 2, 1))
    g["c_im"] = -jnp.transpose(_diag_blocks(dci.T, g_), (0, 2, 1))
    g["d_skip"] = g["d_skip"].reshape(g_, SSM_GROUP)


def loss_and_grad(dm, h, target, g_final):
    D = dm.D

    def loss_fn(xb, tb, gb):
        y, vjp = jax.vjp(_rms, xb, gb)
        err = y - tb
        dxb, dg = vjp(err * (1.0 / D))
        part = 0.5 * jnp.sum(jnp.mean(err * err, axis=-1, keepdims=True), axis=0, keepdims=True)
        lane = lax.broadcasted_iota(jnp.int32, (1, LANES), 1)
        return dxb, dg, jnp.where(lane == 0, part, 0.0)

    return rowwise("loss", loss_fn, [(h, D, 0), (target, D, 0)], [g_final.reshape(1, D)], [(D, F32)], [D, LANES], M=dm.M)


KIND = {"w_in": "cols", "w_uq": "cols", "w_ukv": "cols", "w_glu": "cols", "w_o": "rows", "w_gate": "cols", "w_up": "cols", "w_down": "rows"}
SHARDED = list(KIND)
TRANSPOSED = ("w_in",)
GATHER_GROUPS = {"mixer": ["w_in", "w_uq", "w_ukv", "w_glu"], "rest": ["w_o", "w_gate", "w_up", "w_down"]}
REDUCE_GROUPS = {"ffn": ["w_gate", "w_up", "w_down"], "others": ["w_o", "w_in", "w_uq", "w_ukv", "w_glu"]}
SMALL = ["g_mix", "g_q", "g_kv", "a_re", "a_im", "b_re", "b_im", "c_re", "c_im", "d_skip", "log_dt", "b_glu",
         "g_out_mla", "g_out_ssm", "g_out_dil", "g_ffn", "g_final"]
ORDER = ["g_mix", "w_in", "g_q", "w_uq", "g_kv", "w_ukv", "a_re", "a_im", "b_re", "b_im", "c_re", "c_im", "d_skip", "log_dt",
         "w_glu", "b_glu", "g_out_mla", "g_out_ssm", "g_out_dil", "w_o", "g_ffn", "w_gate", "w_up", "w_down", "g_final"]


def kernel(x, g_mix, w_in, g_q, w_uq, g_kv, w_ukv, a_re, a_im, b_re, b_im, c_re, c_im, d_skip, log_dt, w_glu, b_glu, g_out_mla, g_out_ssm, g_out_dil, w_o, g_ffn, w_gate, w_up, w_down, g_final, loss_target, m_g_mix, m_w_in, m_g_q, m_w_uq, m_g_kv, m_w_ukv, m_a_re, m_a_im, m_b_re, m_b_im, m_c_re, m_c_im, m_d_skip, m_log_dt, m_w_glu, m_b_glu, m_g_out_mla, m_g_out_ssm, m_g_out_dil, m_w_o, m_g_ffn, m_w_gate, m_w_up, m_w_down, m_g_final, v_g_mix, v_w_in, v_g_q, v_w_uq, v_g_kv, v_w_ukv, v_a_re, v_a_im, v_b_re, v_b_im, v_c_re, v_c_im, v_d_skip, v_log_dt, v_w_glu, v_b_glu, v_g_out_mla, v_g_out_ssm, v_g_out_dil, v_w_o, v_g_ffn, v_w_gate, v_w_up, v_w_down, v_g_final):
    args = locals()
    w = {k: args[k] for k in ORDER}
    mom = {k: args["m_" + k] for k in ORDER}
    var = {k: args["v_" + k] for k in ORDER}
    dm = Dims(x, g_q, g_kv, g_out_mla, g_out_ssm, g_out_dil, w_gate.shape[-1] * N_CHIPS)
    L = g_mix.shape[0]

    small = {k: w[k] for k in SMALL}
    na = len(SHARDED)
    tabs = _rope_tables(dm) + (dilated_bias(dm.M),)
    sel ={"w_in": selection_matrices(dm.src_in, w_in.shape[-1]), "w_uq": selection_matrices(dm.src_q, w_uq.shape[-1]),
           "w_ukv": selection_matrices(dm.src_kv, w_ukv.shape[-1])}

    stored = lambda t, k: jnp.swapaxes(t, 1, 2) if k in TRANSPOSED else t
    store_kind = {k: "rows" if k in TRANSPOSED else KIND[k] for k in SHARDED}
    G = [{k: cast_into_slot(f"cast_{k}_l{l}", stored(w[k], k), store_kind[k], l) for k in SHARDED} for l in range(L)]
    sems = {}

    def gather_stage(stage, plan):
        def start(l, grp, car):
            names = GATHER_GROUPS[grp]
            sems[stage, l, grp], bufs, _, car = split_start(f"ag_{stage}_start_{grp}_l{l}", plan, 3 * len(names), [G[l][k] for k in names], [], car)
            G[l].update(zip(names, bufs))
            return car

        def wait(l, grp, car):
            names = GATHER_GROUPS[grp]
            bufs, car = split_wait(f"ag_{stage}_wait_{grp}_l{l}", plan, sems[stage, l, grp], [G[l][k] for k in names], car)
            G[l].update(zip(names, bufs))
            return car

        return start, wait

    ici_start, ici_wait = gather_stage("ici", plan_gather_ici)
    pass_start, pass_wait = gather_stage("pass", plan_gather_pass)

    car = tabs[0]
    for l, grp in ((0, "mixer"), (0, "rest"), (1, "mixer")):
        if l < L:
            car = ici_start(l, grp, car)
    tabs = (pass_wait(0, "mixer", pass_start(0, "mixer", ici_wait(0, "mixer", car))),) + tabs[1:]
    h = x.reshape(dm.M, dm.D)
    lws, sps, raws, saved = [], [], [], []
    for l in range(L):
        lw, sp, raw = layer_params(dm, small, l)
        for k in GATHER_GROUPS["mixer"]:
            lw[k] = regroup_cols(f"regroup_{k}_l{l}", G[l][k], sel[k], k in TRANSPOSED) if k in sel else G[l][k]

        def at_q(car, l=l):
            car = pass_start(l, "rest", ici_wait(l, "rest", car))
            if l + 1 < L:
                car = pass_start(l + 1, "mixer", ici_wait(l + 1, "mixer", car))
            return car

        def at_mid(car, lw, l=l):
            car = pass_wait(l, "rest", car)
            lw.update({k: G[l][k] for k in GATHER_GROUPS["rest"]})
            if l + 1 < L:
                car = ici_start(l + 1, "rest", pass_wait(l + 1, "mixer", car))
            if l + 2 < L:
                car = ici_start(l + 2, "mixer", car)
            return car

        h, sv = layer_forward(dm, l, h, lw, sp, tabs, at_q, at_mid)
        lws.append(lw)
        sps.append(sp)
        raws.append(raw)
        saved.append(sv)
    dx, g_final_part, loss_part = loss_and_grad(dm, h, loss_target.reshape(dm.M, dm.D), w["g_final"])

    def reduce_begin(l, grp, g, car):
        names = REDUCE_GROUPS[grp]
        parts = [g.pop(k) for k in names]
        fresh = [jax.ShapeDtypeStruct(p.shape[1:], BF16) for p in parts]
        sm, parts, gots, car = split_start(f"rs_pair_start_{grp}_l{l}", plan_pair(len(names)), len(names), parts, fresh, car)
        return {"l": l, "grp": grp, "names": names, "sems": sm, "parts": parts, "gots": gots}, car

    def reduce_chips(st, car):
        names, tag, n = st["names"], f"{st['grp']}_l{st['l']}", len(st["names"])
        bufs, car = split_wait(f"rs_pair_wait_{tag}", plan_pair(n), st["sems"], st["parts"] + st["gots"], car)
        sums = [pair_sum(f"rs_pairsum_{a}_l{st['l']}", bufs[i], bufs[n + i]) for i, a in enumerate(names)]
        fresh = [jax.ShapeDtypeStruct((3,) + s.shape[1:], BF16) for s, _ in sums]
        st["sems"], st["s"], st["arrived"], car = split_start(f"rs_chips_start_{tag}", plan_chips(n), 3 * n, [s for s, _ in sums], fresh, car)
        st["own"] = [o for _, o in sums]
        return car

    def reduce_share(st, car):
        names, tag, n = st["names"], f"{st['grp']}_l{st['l']}", len(st["names"])
        bufs, car = split_wait(f"rs_chips_wait_{tag}", plan_chips(n), st["sems"], st["s"] + st["arrived"], car)
        mine = [chips_sum(f"rs_sum_{a}_l{st['l']}", st["own"][i], bufs[n + i]) for i, a in enumerate(names)]
        fresh = [jax.ShapeDtypeStruct(m_.shape, F32) for m_ in mine]
        st["sems"], st["mine"], st["theirs"], car = split_start(f"rs_share_start_{tag}", plan_share(n), n, mine, fresh, car)
        return car

    def reduce_end(st, car):
        n = len(st["names"])
        bufs, car = split_wait(f"rs_share_wait_{st['grp']}_l{st['l']}", plan_share(n), st["sems"], st["mine"] + st["theirs"], car)
        for i, k in enumerate(st["names"]):
            reduced[st["l"]][k] = (bufs[i], bufs[n + i])
        return car

    reduced, grads, prev_ffn, prev_oth = [{} for _ in range(L)], [None] * L, None, None
    for l in reversed(range(L)):
        mine = {}

        def at_a(car, g, l=l, mine=mine, pf=prev_ffn, po=prev_oth):
            mine["st"], car = reduce_begin(l, "ffn", g, car)
            if pf is not None:
                car = reduce_chips(po, reduce_share(pf, car))
            return car

        def at_m(car, g, mine=mine, pf=prev_ffn):
            car = reduce_chips(mine["st"], car)
            return car if pf is None else reduce_end(pf, car)

        def at_b(car, g, po=prev_oth):
            return car if po is None else reduce_share(po, car)

        dx, g = layer_backward(dm, l, dx, lws[l], sps[l], tabs, saved[l], at_a, at_m, at_b)
        ssm_param_grads(dm, l, g, raws[l])
        car = dx if l else loss_part
        if prev_oth is not None:
            car = reduce_end(prev_oth, car)
        for k in sel:
            g[k] = ungroup_cols(f"ungroup_{k}_l{l}", g[k], sel[k], k in TRANSPOSED)
        prev_ffn = mine["st"]
        prev_oth, car = reduce_begin(l, "others", g, car)
        if l:
            dx = car
        grads[l] = g
    car = reduce_end(prev_ffn, reduce_share(prev_ffn, reduce_chips(prev_oth, car)))
    loss_part = reduce_end(prev_oth, reduce_share(prev_oth, car))

    gsum = {}
    small_names = [k for k in SMALL if k != "g_final"]
    pieces = [jnp.stack([grads[l][k] for l in range(L)]).reshape(-1) for k in small_names] + [g_final_part.reshape(-1), loss_part.reshape(-1)]
    sizes = [int(p.shape[0]) for p in pieces]
    total = sum(sizes)
    rows = -(-total // (LANES * 16)) * 16
    pack = lambda ps: jnp.concatenate(ps + [jnp.zeros((rows * LANES - total,), F32)]).reshape(rows, LANES)
    red = all_reduce_small(pack(pieces))
    flat = red.reshape(-1)
    offs = np.concatenate([[0], np.cumsum(sizes)]).astype(int)
    names = small_names + ["g_final"]
    for i, k in enumerate(names):
        gsum[k] = flat[offs[i]:offs[i + 1]].reshape(w[k].shape)
    loss = flat[offs[len(names)]]

    delta, new_m, new_v = {}, {}, {}
    for k in SHARDED:
        res = adamw_layers(f"adam_{k}", stored(w[k], k), stored(mom[k], k), stored(var[k], k), [reduced[l][k][0] for l in range(L)],
                           [reduced[l][k][1] for l in range(L)], store_kind[k])
        gsum[k], delta[k], new_m[k], new_v[k] = (stored(t, k) for t in res)
    sm_sizes = sizes[:len(names)]
    sm_total = sum(sm_sizes)
    packs = lambda d: jnp.concatenate([d[k].reshape(-1) for k in names] + [jnp.zeros((rows * LANES - sm_total,), F32)]).reshape(rows, LANES)
    gs = jnp.concatenate([flat[:sm_total], jnp.zeros((rows * LANES - sm_total,), F32)]).reshape(rows, LANES)
    d_, m_, v_ = adamw("adam_small", packs(w), gs, packs(mom), packs(var))
    for i, k in enumerate(names):
        sl = slice(offs[i], offs[i + 1])
        delta[k], new_m[k], new_v[k] = (t.reshape(-1)[sl].reshape(w[k].shape) for t in (d_, m_, v_))

    return (loss, dx.reshape(x.shape), *[gsum[k] for k in ORDER], *[delta[k] for k in ORDER],
            *[new_m[k] for k in ORDER], *[new_v[k] for k in ORDER])
```

```python
import functools
import math

import numpy as np
import jax
import jax.numpy as jnp
from jax import lax
from jax.experimental import pallas as pl
from jax.experimental.pallas import tpu as pltpu

F32 = jnp.float32
BF16 = jnp.bfloat16
MESH = pl.DeviceIdType.MESH

NORM_EPS = 1e-6
MLA_NOPE, MLA_ROPE, MLA_V = 128, 64, 128
SSM_GROUP, SSM_STATE = 16, 64
DIL_HEAD = 64
DIL_PATTERNS = ((128, 1), (512, 4), (2048, 16))
ROPE_THETA = 10000.0
ADAM_LR, ADAM_B1, ADAM_B2, ADAM_EPS, ADAM_WD, ADAM_STEP = 0.001, 0.9, 0.999, 1e-08, 0.01, 10
N_CHIPS = 4

LANES = 128
SUBLANES_BF16 = 16
VMEM_LIMIT = 56 * 1024 * 1024
ROW_BUDGET = 20 * 1024 * 1024
MM_BUDGET = 40 * 1024 * 1024
NEG = -1e30


def _cparams(sem=None):
    return pltpu.CompilerParams(dimension_semantics=sem, vmem_limit_bytes=VMEM_LIMIT)


def _pick(n, cap, q, off=0):
    best = None
    for d in range(q, min(n, cap) + 1, q):
        if n % d == 0 and off % d == 0:
            best = d
    if best is None or (best * 4 <= min(cap, n) and n <= 3072 and off % n == 0):
        assert off % n == 0, (n, off)
        return n
    return best


_DOT_DIMS = {"nn": (((1,), (0,)), ((), ())), "nt": (((1,), (1,)), ((), ())), "tn": (((0,), (0,)), ((), ()))}


def _divs(n, q, within=None, off=0):
    return [d for d in range(q, n + 1, q) if n % d == 0 and off % d == 0 and (within is None or within % d == 0)] or [n]


def _mm_tiles(M, N, K, tms, tns, tks, ab, bb, ob):
    best = None
    for tk in tks:
        nk = K // tk
        for tn in tns:
            for tm in tms:
                if 2 * (tm * tk * ab + tk * tn * bb + tm * tn * ob) + tm * tn * 4 * (2 if nk > 1 else 1) > MM_BUDGET:
                    continue
                steps = (M // tm) * (N // tn) * nk
                hbm = M * K * ab * (1 if nk == 1 else N // tn) + K * N * bb * (M // tm) + M * N * ob
                cost = steps * 0.35e-6 + hbm / 3.0e12 + (nk - 1) * M * N * 12 / 4.0e12
                if best is None or cost < best[0]:
                    best = (cost, tm, tn, tk)
    assert best is not None, (M, N, K)
    return best[1:]


def matmul(name, a, b, mode, *, M, N, K, a_off=(0, 0), b_off=(0, 0), b_lead=None, w=None, add=None, out_dtype=F32, into=None):
    tn_mode = mode == "tn"
    a_ro, a_co = (a_off[1], a_off[0]) if tn_mode else a_off
    b_no, b_ko = b_off if mode == "nt" else (b_off[1], b_off[0])
    n_within = k_within = m_within = None
    if w is not None:
        kind, shard = w
        if kind == "cols":
            b = b.reshape(N_CHIPS, b.shape[1] * b.shape[2], b.shape[3])
        rows_within, cols_within = (K if mode == "nn" else N, shard) if kind == "cols" else (shard, (N if mode == "nn" else K) // 2)
        k_within, n_within = (rows_within, cols_within) if mode == "nn" else (cols_within, rows_within)
    if into is not None:
        m_within, n_within = (M // 2, into[1]) if into[0] == "cols" else (into[1], N // 2)
    tms = [d for d in _divs(M, 128 if tn_mode else SUBLANES_BF16, m_within, a_ro) if d <= 1408]
    tns = [d for d in _divs(N, LANES, n_within, b_no) if d <= 2048]
    tks = _divs(K, SUBLANES_BF16 if tn_mode else LANES, k_within, math.gcd(a_co, b_ko))
    ob = jnp.dtype(out_dtype).itemsize + (add.dtype.itemsize if add is not None else 0)
    tm, tn, tk = _mm_tiles(M, N, K, tms, tns, tks, a.dtype.itemsize, b.dtype.itemsize, ob)
    nk = K // tk
    dn = _DOT_DIMS[mode]

    if tn_mode:
        a_spec = pl.BlockSpec((tk, tm), lambda i, j, k: (k + a_co // tk, i + a_ro // tm))
    else:
        a_spec = pl.BlockSpec((tm, tk), lambda i, j, k: (i + a_ro // tm, k + a_co // tk))
    b_blk = (tn, tk) if mode == "nt" else (tk, tn)
    if w is not None:
        tr_, tc_ = (tk, tn) if mode == "nn" else (tn, tk)
        rper, cper = rows_within // tr_, cols_within // tc_

        def wmap(rb, cb):
            if kind == "cols":
                return (cb // cper, rb, cb % cper)
            return (rb // rper, cb // cper, rb % rper, cb % cper)

        imap = (lambda i, j, k: wmap(k, j)) if mode == "nn" else (lambda i, j, k: wmap(j, k))
        b_spec = pl.BlockSpec((None,) * (b.ndim - 2) + b_blk, imap)
    else:
        if mode == "nt":
            imap = lambda i, j, k: (j + b_no // tn, k + b_ko // tk)
        else:
            imap = lambda i, j, k: (k + b_ko // tk, j + b_no // tn)
        if b_lead is None:
            b_spec = pl.BlockSpec(b_blk, imap)
        else:
            b_spec = pl.BlockSpec((None,) + b_blk, lambda i, j, k: (b_lead,) + imap(i, j, k))
    o_plain = pl.BlockSpec((tm, tn), lambda i, j, k: (i, j))
    if into is None:
        o_spec, out_shape = o_plain, jax.ShapeDtypeStruct((M, N), out_dtype)
    else:
        rper, cper = m_within // tm, n_within // tn
        if into[0] == "cols":
            o_spec = pl.BlockSpec((None, None, tm, tn), lambda i, j, k: (i // rper, j // cper, i % rper, j % cper))
        else:
            o_spec = pl.BlockSpec((None, None, tm, tn), lambda i, j, k: (j // cper, i // rper, i % rper, j % cper))
        out_shape = jax.ShapeDtypeStruct((2, N_CHIPS, m_within, n_within), out_dtype)
    has_add = add is not None
    n_in = 2 + has_add

    def body(*refs):
        a_ref, b_ref = refs[0], refs[1]
        add_ref = refs[2] if has_add else None
        o_ref = refs[n_in]
        part = lax.dot_general(a_ref[...].astype(BF16), b_ref[...].astype(BF16), dn, preferred_element_type=F32)

        def finish(r):
            if has_add:
                r = r + add_ref[...].astype(F32)
            o_ref[...] = r.astype(o_ref.dtype)

        if nk == 1:
            finish(part)
        else:
            acc_ref = refs[-1]
            k = pl.program_id(2)

            @pl.when(k == 0)
            def _():
                acc_ref[...] = part

            @pl.when((k > 0) & (k < nk - 1))
            def _():
                acc_ref[...] += part

            @pl.when(k == nk - 1)
            def _():
                finish(acc_ref[...] + part)

    in_specs = [a_spec, b_spec] + ([o_plain] if has_add else [])
    args = (a, b) + ((add,) if has_add else ())
    return pl.pallas_call(
        body, name=name, out_shape=out_shape, grid=(M // tm, N // tn, nk), in_specs=in_specs, out_specs=o_spec,
        scratch_shapes=[pltpu.VMEM((tm, tn), F32)] if nk > 1 else [],
        compiler_params=_cparams(("parallel", "parallel", "arbitrary")),
    )(*args)


def selection_matrices(src_of_new, n_shard):
    src_np = np.asarray(src_of_new, np.int64)
    src = jnp.asarray(src_np.astype(np.int32))
    ref = jnp.arange(N_CHIPS, dtype=jnp.int32)[:, None] * n_shard + jnp.arange(n_shard, dtype=jnp.int32)[None, :]
    pm = (ref[:, :, None] == src[None, None, :]).astype(BF16)
    pmt = (src[None, :, None] == ref[:, None, :]).astype(BF16)
    tc = _pick(len(src_np), 512, LANES)
    feeds = [sorted({int(s) // n_shard for s in src_np[cb * tc:(cb + 1) * tc] if s >= 0}) for cb in range(len(src_np) // tc)]
    return pm, pmt, tc, feeds


def regroup_cols(name, g, sel, transposed=False):
    pm, _, tc, feeds = sel
    n_new = pm.shape[-1]
    if transposed:
        nn, kh = g.shape[2:]
        K = 2 * kh
        tm = _pick(kh, 512, LANES)
        hb = kh // tm
        g_spec = pl.BlockSpec((N_CHIPS, None, nn, tm), lambda c, i: (0, i // hb, 0, i % hb))
    else:
        nn = g.shape[-1]
        g = g.reshape(N_CHIPS, -1, nn)
        K = g.shape[1]
        tm = _pick(K, 512, SUBLANES_BF16)
        g_spec = pl.BlockSpec((N_CHIPS, tm, nn), lambda c, i: (0, i, 0))

    def body(g_ref, pm_ref, o_ref):
        for cb, chips in enumerate(feeds):
            @pl.when(pl.program_id(0) == cb)
            def _(chips=chips):
                acc = jnp.zeros((tm, tc), F32)
                for j in chips:
                    acc = acc + _dot(g_ref[j], pm_ref[j], "tn" if transposed else "nn")
                o_ref[...] = acc.astype(o_ref.dtype)

    return pl.pallas_call(
        body, name=name, out_shape=jax.ShapeDtypeStruct((K, n_new), BF16), grid=(n_new // tc, K // tm),
        in_specs=[g_spec, pl.BlockSpec((N_CHIPS, nn, tc), lambda c, i: (0, 0, c))],
        out_specs=pl.BlockSpec((tm, tc), lambda c, i: (i, c)), compiler_params=_cparams(("parallel", "parallel")),
    )(g, pm)


def ungroup_cols(name, dw, sel, transposed=False):
    pm, pmt, tc, feeds = sel
    K, n_new = dw.shape
    nn = pmt.shape[-1]
    kh = K // 2
    tm = _pick(kh, 512, LANES if transposed else SUBLANES_BF16)
    hb = kh // tm
    fed_by = [[cb for cb, chips in enumerate(feeds) if j in chips] for j in range(N_CHIPS)]

    def body(dw_ref, sel_ref, o_ref):
        for j, blocks in enumerate(fed_by):
            @pl.when(pl.program_id(0) == j)
            def _(blocks=blocks):
                acc = jnp.zeros((nn, tm) if transposed else (tm, nn), F32)
                for cb in blocks:
                    cols = slice(cb * tc, (cb + 1) * tc)
                    if transposed:
                        acc = acc + _dot(sel_ref[:, cols], dw_ref[:, cols], "nt")
                    else:
                        acc = acc + _dot(dw_ref[:, cols], sel_ref[cols, :], "nn")
                o_ref[...] = acc.astype(o_ref.dtype)

    if transposed:
        sel_arr, sel_spec = pm, pl.BlockSpec((None, nn, n_new), lambda j, i: (j, 0, 0))
        out_shape = jax.ShapeDtypeStruct((2, N_CHIPS, nn, kh), BF16)
        out_spec = pl.BlockSpec((None, None, nn, tm), lambda j, i: (i // hb, j, 0, i % hb))
    else:
        sel_arr, sel_spec = pmt, pl.BlockSpec((None, n_new, nn), lambda j, i: (j, 0, 0))
        out_shape = jax.ShapeDtypeStruct((2, N_CHIPS, kh, nn), BF16)
        out_spec = pl.BlockSpec((None, None, tm, nn), lambda j, i: (i // hb, j, i % hb, 0))
    return pl.pallas_call(
        body, name=name, out_shape=out_shape, grid=(N_CHIPS, 2 * hb),
        in_specs=[pl.BlockSpec((tm, n_new), lambda j, i: (i, 0)), sel_spec], out_specs=out_spec,
        compiler_params=_cparams(("parallel", "parallel")),
    )(dw, sel_arr)


def rowwise(name, fn, rows, vecs, outs, sums=(), *, M):
    rows = [tuple(r) + (0,) * (4 - len(r)) for r in rows]
    nr, nv, no, ns = len(rows), len(vecs), len(outs), len(sums)
    per_row = sum(w * a.dtype.itemsize for a, w, _, _ in rows) + sum(w * jnp.dtype(d).itemsize for w, d in outs)
    tr = _pick(M, max(8, min(512, ROW_BUDGET // (2 * per_row))), 16 if M % 16 == 0 else 8)

    def body(*refs):
        i = pl.program_id(0)
        res = fn(*[r[...] for r in refs[:nr + nv]])
        o_refs = refs[nr + nv:nr + nv + no]
        s_refs = refs[nr + nv + no:]
        for ref, val in zip(o_refs, res[:no]):
            ref[...] = val.astype(ref.dtype)
        if ns:
            @pl.when(i == 0)
            def _():
                for ref in s_refs:
                    ref[...] = jnp.zeros(ref.shape, F32)

            for ref, val in zip(s_refs, res[no:]):
                ref[...] += val

    in_specs = [pl.BlockSpec((tr, w), functools.partial(lambda i, cb, rb: (i + rb, cb), cb=off // w, rb=roff // tr)) for _, w, off, roff in rows]
    for _, w, off, roff in rows:
        assert off % w == 0 and roff % tr == 0
    in_specs += [pl.BlockSpec(v.shape, functools.partial(lambda i, nd: (0,) * nd, nd=v.ndim)) for v in vecs]
    out_specs = [pl.BlockSpec((tr, w), lambda i: (i, 0)) for w, _ in outs]
    out_specs += [pl.BlockSpec((1, w), lambda i: (0, 0)) for w in sums]
    out_shape = [jax.ShapeDtypeStruct((M, w), d) for w, d in outs] + [jax.ShapeDtypeStruct((1, w), F32) for w in sums]
    return pl.pallas_call(
        body, name=name, out_shape=out_shape, grid=(M // tr,), in_specs=in_specs, out_specs=out_specs,
        compiler_params=_cparams(("arbitrary",) if ns else ("parallel",)),
    )(*[r[0] for r in rows], *vecs)


def _rms(x, g):
    xf = x.astype(F32)
    return xf * lax.rsqrt(jnp.mean(xf * xf, axis=-1, keepdims=True) + NORM_EPS) * g


def _gelu(y):
    return 0.5 * y * (1.0 + jnp.tanh(math.sqrt(2.0 / math.pi) * (y + 0.044715 * (y * y * y))))


def _colsum(v):
    return jnp.sum(v, axis=0, keepdims=True)


def rms_fwd(name, x, width, off, g, *, M):
    return rowwise(name, lambda xb, gb: (_rms(xb, gb),), [(x, width, off)], [g], [(width, BF16)], M=M)[0]


def rms_bwd(name, x, width, off, g, dy, resid=None, *, M, out_dtype=F32):
    def fn(xb, dyb, *rest):
        gb = rest[-1]
        _, vjp = jax.vjp(_rms, xb.astype(F32), gb)
        dx, dg = vjp(dyb.astype(F32))
        if resid is not None:
            dx = dx + rest[0]
        return dx, dg

    rows = [(x, width, off), (dy, width, 0)] + ([(resid, width, 0)] if resid is not None else [])
    return rowwise(name, fn, rows, [g], [(width, out_dtype)], [width], M=M)


def lane_concat(name, parts, *, M, pad_to=None):
    width = sum(p.shape[1] for p in parts)
    pad = 0 if pad_to is None else pad_to - width

    def fn(*blocks):
        cols = [b.astype(BF16) for b in blocks]
        if pad:
            cols.append(jnp.zeros((blocks[0].shape[0], pad), BF16))
        return (jnp.concatenate(cols, axis=1),)

    return rowwise(name, fn, [(p, p.shape[1], 0) for p in parts], [], [(width + pad, BF16)], M=M)[0]


def _swap_halves(x, half):
    w = x.shape[-1]
    lane = lax.broadcasted_iota(jnp.int32, x.shape, x.ndim - 1)
    first = (lane % (2 * half)) < half
    return jnp.where(first, pltpu.roll(x, w - half, x.ndim - 1), pltpu.roll(x, half, x.ndim - 1))


def dilated_bias(M):
    delta = jnp.arange(M, dtype=jnp.int32)[:, None] - jnp.arange(M, dtype=jnp.int32)[None, :]
    w = jnp.zeros(delta.shape, F32)
    for window, dil in DIL_PATTERNS:
        ok = (delta >= 0) & (delta <= window)
        if dil > 1:
            ok = ok & ((delta & (dil - 1)) == 0)
        w = w + ok.astype(F32)
    return jnp.where(w > 0, jnp.log(jnp.maximum(w, 1.0)), NEG)


def _dot(a, b, mode):
    return lax.dot_general(a, b, _DOT_DIMS[mode], preferred_element_type=F32)


def attention_fwd(name, qa, qa_off, ka, ka_off, v, v_off, *, da, dv, pairs, scale, M, qb=None, qb_off=0, kb=None, bias=None):
    tq = min(256, M)
    tk = min(512, M)
    has_b = qb is not None
    has_bias = bias is not None
    dr = MLA_ROPE

    def body(*refs):
        refs = list(refs)
        bias_ref = refs.pop(3) if has_bias else None
        if has_b:
            qa_ref, ka_ref, v_ref, qb_ref, kb_ref, o_ref, lse_ref = refs
        else:
            qa_ref, ka_ref, v_ref, o_ref, lse_ref = refs
        i = pl.program_id(1)
        t0 = i * tq
        nkb = (t0 + tq + tk - 1) // tk
        n_full = nkb if has_bias else t0 // tk
        q1s = [qa_ref[:, hh * da:(hh + 1) * da].astype(BF16) for hh in range(2)]
        q2s = [qb_ref[:, hh * dr:(hh + 1) * dr].astype(BF16) if has_b else None for hh in range(2)]

        def step(kbi, carry, masked):
            ks = pl.multiple_of(kbi * tk, tk)
            k2 = kb_ref[pl.ds(ks, tk), 0:dr].astype(BF16) if has_b else None
            if has_bias:
                extra = bias_ref[:, pl.ds(ks, tk)]
            elif masked:
                delta = (t0 + lax.broadcasted_iota(jnp.int32, (tq, tk), 0)) - (ks + lax.broadcasted_iota(jnp.int32, (tq, tk), 1))
            new = []
            for hh, (m, l, acc) in enumerate(carry):
                k1 = ka_ref[pl.ds(ks, tk), hh * da:(hh + 1) * da].astype(BF16)
                s = _dot(q1s[hh], k1, "nt")
                if has_b:
                    s = s + _dot(q2s[hh], k2, "nt")
                s = s * scale
                if has_bias:
                    s = s + extra
                elif masked:
                    s = jnp.where(delta >= 0, s, NEG)
                m_new = jnp.maximum(m, jnp.max(s, axis=1, keepdims=True))
                alpha = jnp.exp(m - m_new)
                p = jnp.exp(s - m_new)
                l = alpha * l + jnp.sum(p, axis=1, keepdims=True)
                vv = v_ref[pl.ds(ks, tk), hh * dv:(hh + 1) * dv].astype(BF16)
                acc = alpha * acc + _dot(p.astype(BF16), vv, "nn")
                new.append((m_new, l, acc))
            return tuple(new)

        carry = tuple((jnp.full((tq, 1), NEG, F32), jnp.zeros((tq, 1), F32), jnp.zeros((tq, dv), F32)) for _ in range(2))
        carry = lax.fori_loop(0, n_full, functools.partial(step, masked=False), carry)
        carry = lax.fori_loop(n_full, nkb, functools.partial(step, masked=True), carry)
        o_parts = [acc / l for _, l, acc in carry]
        lse_parts = [m + jnp.log(l) for m, l, _ in carry]
        o_ref[...] = jnp.concatenate(o_parts, axis=1)
        lane = lax.broadcasted_iota(jnp.int32, (tq, LANES), 1)
        lse_ref[...] = jnp.where(lane == 0, lse_parts[0], jnp.where(lane == 1, lse_parts[1], 0.0))

    assert qa_off % (2 * da) == 0 and ka_off % (2 * da) == 0 and v_off % (2 * dv) == 0
    in_specs = [
        pl.BlockSpec((tq, 2 * da), lambda hp, i: (i, qa_off // (2 * da) + hp)),
        pl.BlockSpec((M, 2 * da), lambda hp, i: (0, ka_off // (2 * da) + hp)),
        pl.BlockSpec((M, 2 * dv), lambda hp, i: (0, v_off // (2 * dv) + hp)),
    ]
    args = [qa, ka, v]
    if has_bias:
        in_specs.append(pl.BlockSpec((tq, M), lambda hp, i: (i, 0)))
        args.append(bias)
    if has_b:
        assert qb_off % LANES == 0
        in_specs += [pl.BlockSpec((tq, LANES), lambda hp, i: (i, qb_off // LANES + hp)),
                     pl.BlockSpec((M, LANES), lambda hp, i: (0, 0))]
        args += [qb, kb]
    out_specs = [pl.BlockSpec((tq, 2 * dv), lambda hp, i: (i, hp)),
                 pl.BlockSpec((None, tq, LANES), lambda hp, i: (hp, i, 0))]
    out_shape = [jax.ShapeDtypeStruct((M, pairs * 2 * dv), F32), jax.ShapeDtypeStruct((pairs, M, LANES), F32)]
    return pl.pallas_call(
        body, name=name, out_shape=out_shape, grid=(pairs, M // tq), in_specs=in_specs, out_specs=out_specs,
        compiler_params=_cparams(("parallel", "arbitrary")),
    )(*args)


def attention_bwd(name, qa, qa_off, ka, ka_off, v, v_off, o, do, lse, *, da, dv, pairs, scale, M,
                  qb=None, qb_off=0, kb=None, bias=None):
    tq = min(256, M)
    tk = min(256, M)
    has_b = qb is not None
    has_bias = bias is not None
    dr = MLA_ROPE

    def body(*refs):
        refs = list(refs)
        bias_ref = refs.pop(6) if has_bias else None
        if has_b:
            qa_ref, ka_ref, v_ref, o_ref, do_ref, lse_ref, qb_ref, kb_ref, dqa_ref, dka_ref, dv_ref, dqb_ref, dkb_ref = refs
        else:
            qa_ref, ka_ref, v_ref, o_ref, do_ref, lse_ref, dqa_ref, dka_ref, dv_ref = refs
        hp = pl.program_id(0)
        i = pl.program_id(1)
        t0 = i * tq
        nkb = (t0 + tq + tk - 1) // tk
        n_full = nkb if has_bias else t0 // tk

        @pl.when(i == 0)
        def _():
            dka_ref[...] = jnp.zeros(dka_ref.shape, F32)
            dv_ref[...] = jnp.zeros(dv_ref.shape, F32)

        if has_b:
            @pl.when((i == 0) & (hp == 0))
            def _():
                dkb_ref[...] = jnp.zeros(dkb_ref.shape, F32)

        q1s = [qa_ref[:, hh * da:(hh + 1) * da].astype(BF16) for hh in range(2)]
        q2s = [qb_ref[:, hh * dr:(hh + 1) * dr].astype(BF16) if has_b else None for hh in range(2)]
        do_bfs = [do_ref[:, hh * dv:(hh + 1) * dv].astype(BF16) for hh in range(2)]
        rowdots = [jnp.sum(do_ref[:, hh * dv:(hh + 1) * dv] * o_ref[:, hh * dv:(hh + 1) * dv], axis=1, keepdims=True) for hh in range(2)]
        lses = [lse_ref[:, hh:hh + 1] for hh in range(2)]

        def step(kbi, carry, masked):
            ks = pl.multiple_of(kbi * tk, tk)
            k2 = kb_ref[pl.ds(ks, tk), 0:dr].astype(BF16) if has_b else None
            if has_bias:
                extra = bias_ref[:, pl.ds(ks, tk)]
            elif masked:
                delta = (t0 + lax.broadcasted_iota(jnp.int32, (tq, tk), 0)) - (ks + lax.broadcasted_iota(jnp.int32, (tq, tk), 1))
            new, dkb_part = [], None
            for hh, (dq1, dq2) in enumerate(carry):
                k1 = ka_ref[pl.ds(ks, tk), hh * da:(hh + 1) * da].astype(BF16)
                s = _dot(q1s[hh], k1, "nt")
                if has_b:
                    s = s + _dot(q2s[hh], k2, "nt")
                s = s * scale
                if has_bias:
                    s = s + extra
                elif masked:
                    s = jnp.where(delta >= 0, s, NEG)
                p = jnp.exp(s - lses[hh])
                vv = v_ref[pl.ds(ks, tk), hh * dv:(hh + 1) * dv].astype(BF16)
                dp = _dot(do_bfs[hh], vv, "nt")
                ds = (p * (dp - rowdots[hh]) * scale).astype(BF16)
                dq1 = dq1 + _dot(ds, k1, "nn")
                dka_ref[pl.ds(ks, tk), hh * da:(hh + 1) * da] += _dot(ds, q1s[hh], "tn")
                dv_ref[pl.ds(ks, tk), hh * dv:(hh + 1) * dv] += _dot(p.astype(BF16), do_bfs[hh], "tn")
                if has_b:
                    dq2 = dq2 + _dot(ds, k2, "nn")
                    part = _dot(ds, q2s[hh], "tn")
                    dkb_part = part if dkb_part is None else dkb_part + part
                new.append((dq1, dq2))
            if has_b:
                dkb_ref[pl.ds(ks, tk), 0:dr] += dkb_part
            return tuple(new)

        carry = tuple((jnp.zeros((tq, da), F32), jnp.zeros((tq, dr), F32)) for _ in range(2))
        carry = lax.fori_loop(0, n_full, functools.partial(step, masked=False), carry)
        carry = lax.fori_loop(n_full, nkb, functools.partial(step, masked=True), carry)
        dqa_ref[...] = jnp.concatenate([c[0] for c in carry], axis=1).astype(dqa_ref.dtype)
        if has_b:
            dqb_ref[...] = jnp.concatenate([c[1] for c in carry], axis=1).astype(dqb_ref.dtype)

    in_specs = [
        pl.BlockSpec((tq, 2 * da), lambda hp, i: (i, qa_off // (2 * da) + hp)),
        pl.BlockSpec((M, 2 * da), lambda hp, i: (0, ka_off // (2 * da) + hp)),
        pl.BlockSpec((M, 2 * dv), lambda hp, i: (0, v_off // (2 * dv) + hp)),
        pl.BlockSpec((tq, 2 * dv), lambda hp, i: (i, hp)),
        pl.BlockSpec((tq, 2 * dv), lambda hp, i: (i, hp)),
        pl.BlockSpec((None, tq, LANES), lambda hp, i: (hp, i, 0)),
    ]
    args = [qa, ka, v, o, do, lse]
    if has_bias:
        in_specs.append(pl.BlockSpec((tq, M), lambda hp, i: (i, 0)))
        args.append(bias)
    out_specs = [pl.BlockSpec((tq, 2 * da), lambda hp, i: (i, hp)),
                 pl.BlockSpec((M, 2 * da), lambda hp, i: (0, hp)),
                 pl.BlockSpec((M, 2 * dv), lambda hp, i: (0, hp))]
    out_shape = [jax.ShapeDtypeStruct((M, pairs * 2 * da), BF16),
                 jax.ShapeDtypeStruct((M, pairs * 2 * da), F32),
                 jax.ShapeDtypeStruct((M, pairs * 2 * dv), F32)]
    if has_b:
        in_specs += [pl.BlockSpec((tq, LANES), lambda hp, i: (i, qb_off // LANES + hp)),
                     pl.BlockSpec((M, LANES), lambda hp, i: (0, 0))]
        args += [qb, kb]
        out_specs += [pl.BlockSpec((tq, LANES), lambda hp, i: (i, hp)), pl.BlockSpec((M, LANES), lambda hp, i: (0, 0))]
        out_shape += [jax.ShapeDtypeStruct((M, pairs * LANES), F32), jax.ShapeDtypeStruct((M, LANES), F32)]
    return pl.pallas_call(
        body, name=name, out_shape=out_shape, grid=(pairs, M // tq), in_specs=in_specs, out_specs=out_specs,
        compiler_params=_cparams(("arbitrary", "arbitrary")),
    )(*args)


def ssm_scan(name, xcat, acat, *, M, reverse=False, hcat=None):
    C2 = xcat.shape[1]
    cb = LANES
    tb = min(128, M)
    nblk = M // tb
    with_da = hcat is not None
    nsub = 2 if C2 % (4 * cb) == 0 else 1
    wide = nsub * 2 * cb

    def body(*refs):
        if with_da:
            x_ref, a_ref, h_ref, o_ref, da_ref, p_ref = refs
        else:
            x_ref, a_ref, o_ref, p_ref = refs
        re = [slice(s * 2 * cb, s * 2 * cb + cb) for s in range(nsub)]
        im = [slice(s * 2 * cb + cb, (s + 1) * 2 * cb) for s in range(nsub)]
        ars, ais = [a_ref[:, c] for c in re], [a_ref[:, c] for c in im]
        row = lax.broadcasted_iota(jnp.int32, (tb, cb), 0)

        def logscan(xs):
            ps = list(zip(ars, ais))
            d = 1
            while d < tb:
                shift = tb - d if reverse else d
                keep = (row < tb - d) if reverse else (row >= d)
                nxt = []
                for (xr, xi), (pr, pi) in zip(xs, ps):
                    sr = jnp.where(keep, pltpu.roll(xr, shift, 0), 0.0)
                    si = jnp.where(keep, pltpu.roll(xi, shift, 0), 0.0)
                    nxt.append((xr + pr * sr - pi * si, xi + pr * si + pi * sr))
                xs = nxt
                ps = [(pr * pr - pi * pi, 2.0 * pr * pi) for pr, pi in ps]
                d *= 2
            return xs

        seed = row == (tb - 1 if reverse else 0)
        for s, (p0r, p0i) in enumerate(logscan([(jnp.where(seed, ar, 0.0), jnp.where(seed, ai, 0.0)) for ar, ai in zip(ars, ais)])):
            p_ref[:, re[s]] = p0r
            p_ref[:, im[s]] = p0i
        sub = lax.broadcasted_iota(jnp.int32, (8, cb), 0)
        edge = 0 if reverse else tb - 8
        pick = sub == (0 if reverse else 7)

        def blk(b, carry):
            bb = (nblk - 1 - b) if reverse else b
            t0 = pl.multiple_of(bb * tb, tb)
            te = pl.multiple_of(t0 + edge, 8)
            hs = logscan([(x_ref[pl.ds(t0, tb), re[s]], x_ref[pl.ds(t0, tb), im[s]]) for s in range(nsub)])
            new = []
            for s, ((hr, hi), (cr, ci)) in enumerate(zip(hs, carry)):
                pr, pi = p_ref[:, re[s]], p_ref[:, im[s]]
                o_ref[pl.ds(t0, tb), re[s]] = hr + pr * cr - pi * ci
                o_ref[pl.ds(t0, tb), im[s]] = hi + pr * ci + pi * cr
                new.append((jnp.sum(jnp.where(pick, o_ref[pl.ds(te, 8), re[s]], 0.0), axis=0, keepdims=True),
                            jnp.sum(jnp.where(pick, o_ref[pl.ds(te, 8), im[s]], 0.0), axis=0, keepdims=True)))
            return tuple(new)

        lax.fori_loop(0, nblk, blk, tuple((jnp.zeros((1, cb), F32), jnp.zeros((1, cb), F32)) for _ in range(nsub)))
        if with_da:
            first = lax.broadcasted_iota(jnp.int32, (M, cb), 0) >= 1
            for s in range(nsub):
                hpr = jnp.where(first, pltpu.roll(h_ref[:, re[s]], 1, 0), 0.0)
                hpi = jnp.where(first, pltpu.roll(h_ref[:, im[s]], 1, 0), 0.0)
                lr, li = o_ref[:, re[s]], o_ref[:, im[s]]
                da_ref[:, re[s]] = _colsum(lr * hpr + li * hpi)
                da_ref[:, im[s]] = _colsum(li * hpr - lr * hpi)

    blk_spec = pl.BlockSpec((M, wide), lambda j: (0, j))
    vec_spec = pl.BlockSpec((1, wide), lambda j: (0, j))
    in_specs = [blk_spec, vec_spec] + ([blk_spec] if with_da else [])
    out_specs = [blk_spec] + ([vec_spec] if with_da else [])
    out_shape = [jax.ShapeDtypeStruct((M, C2), F32)] + ([jax.ShapeDtypeStruct((1, C2), F32)] if with_da else [])
    args = [xcat, acat] + ([hcat] if with_da else [])
    res = pl.pallas_call(
        body, name=name, out_shape=out_shape, grid=(C2 // wide,), in_specs=in_specs, out_specs=out_specs,
        scratch_shapes=[pltpu.VMEM((tb, wide), F32)], compiler_params=_cparams(("parallel",)),
    )(*args)
    return res if with_da else res[0]


def ssm_diag_rows(name, m):
    assert LANES == 2 * SSM_STATE
    gp, c2 = m.shape
    groups = gp // SSM_GROUP
    st = SSM_STATE

    def body(m_ref, re_ref, im_ref):
        x = m_ref[...]
        odd = pl.program_id(0) % 2

        @pl.when(odd == 0)
        def _():
            re_ref[:, 0:st] = x[:, 0:st]
            im_ref[:, 0:st] = x[:, LANES:LANES + st]

        @pl.when(odd == 1)
        def _():
            re_ref[:, st:LANES] = x[:, st:LANES]
            im_ref[:, st:LANES] = x[:, LANES + st:2 * LANES]

    out = jax.ShapeDtypeStruct((SSM_GROUP, c2 // 2), F32)
    o_spec = pl.BlockSpec((SSM_GROUP, LANES), lambda g: (0, g // 2))
    return pl.pallas_call(
        body, name=name, out_shape=[out, out], grid=(groups,), in_specs=[pl.BlockSpec((SSM_GROUP, 2 * LANES), lambda g: (g, g // 2))],
        out_specs=[o_spec, o_spec], compiler_params=_cparams(("arbitrary",)),
    )(m)


def _ssm_param_fn(a_re, a_im, ldt, b_re, b_im):
    lr, li = jnp.minimum(a_re, -1e-4), a_im
    dt = jnp.exp(ldt)
    e, ang = jnp.exp(lr * dt), li * dt
    ar, ai = e * jnp.cos(ang), e * jnp.sin(ang)
    den = lr * lr + li * li
    nr, ni = ar - 1.0, ai
    cr, ci = (nr * lr + ni * li) / den, (ni * lr - nr * li) / den
    return ar, ai, cr * b_re - ci * b_im, cr * b_im + ci * b_re


def _whole(shape):
    return pl.BlockSpec(shape, functools.partial(lambda nd: (0,) * nd, nd=len(shape)))


def ssm_param_fwd(name, a_re, a_im, ldt, b_re, b_im):
    def body(*refs):
        res = _ssm_param_fn(*[r[...] for r in refs[:5]])
        for ref, val in zip(refs[5:], res):
            ref[...] = val

    ins = [a_re, a_im, ldt, b_re, b_im]
    outs = [a_re, a_re, b_re, b_re]
    return pl.pallas_call(
        body, name=name, out_shape=[jax.ShapeDtypeStruct(t.shape, F32) for t in outs],
        in_specs=[_whole(t.shape) for t in ins], out_specs=[_whole(t.shape) for t in outs], compiler_params=_cparams(),
    )(*ins)


def ssm_param_bwd(name, a_re, a_im, ldt, b_re, b_im, d_ar, d_ai, d_bbr, d_bbi):
    def body(*refs):
        _, vjp = jax.vjp(_ssm_param_fn, *[r[...] for r in refs[:5]])
        res = vjp(tuple(r[...] for r in refs[5:9]))
        for ref, val in zip(refs[9:], res):
            ref[...] = val

    ins = [a_re, a_im, ldt, b_re, b_im, d_ar, d_ai, d_bbr, d_bbi]
    outs = [a_re, a_im, ldt, b_re, b_im]
    return pl.pallas_call(
        body, name=name, out_shape=[jax.ShapeDtypeStruct(t.shape, F32) for t in outs],
        in_specs=[_whole(t.shape) for t in ins], out_specs=[_whole(t.shape) for t in outs], compiler_params=_cparams(),
    )(*ins)


ANY = pl.BlockSpec(memory_space=pl.ANY)


def _place():
    x, y, c = lax.axis_index("x"), lax.axis_index("y"), lax.axis_index("c")
    chips = [(1 - x, y), (x, 1 - y), (1 - x, 1 - y)]
    return x, y, c, chips


def cast_into_slot(name, w, kind, l=None):
    K, nn = w.shape[-2:]
    hr, hc = (K // 2, nn) if kind == "cols" else (K, nn // 2)
    tr = _pick(hr, max(16, min(512, ROW_BUDGET // (2 * hc * 6))), SUBLANES_BF16)
    nb = hr // tr

    def body(w_ref, o_ref):
        o_ref[...] = w_ref[...].astype(BF16)

    lead = () if l is None else (l,)
    if kind == "cols":
        in_spec = pl.BlockSpec((None,) * len(lead) + (tr, hc), lambda h, i: lead + (h * nb + i, 0))
    else:
        in_spec = pl.BlockSpec((None,) * len(lead) + (tr, hc), lambda h, i: lead + (i, h))
    return pl.pallas_call(
        body, name=name, out_shape=jax.ShapeDtypeStruct((N_CHIPS, 2, hr, hc), BF16), grid=(2, nb), in_specs=[in_spec],
        out_specs=pl.BlockSpec((None, None, tr, hc), lambda h, i: (2 * lax.axis_index("x") + lax.axis_index("y"), h, i, 0)),
        compiler_params=_cparams(("parallel", "parallel")),
    )(w)


HBM_SPEC = pl.BlockSpec(memory_space=pltpu.HBM)
SEM_SPEC = pl.BlockSpec(memory_space=pltpu.SEMAPHORE)
SPLIT_PARAMS = pltpu.CompilerParams(has_side_effects=pltpu.SideEffectType.DATAFLOW_SIDE_EFFECTING)


def _in_hbm(t):
    return pltpu.with_memory_space_constraint(t, pltpu.HBM)


def split_start(name, plan, n, bufs, fresh, carrier):
    nb, nf = len(bufs), len(fresh)

    def body(*refs):
        outs = refs[nb + 1:]
        for i, (s, d, dev) in enumerate(plan(list(outs[2:2 + nb + nf]))):
            pltpu.make_async_remote_copy(src_ref=s, dst_ref=d, send_sem=outs[0].at[i], recv_sem=outs[1].at[i],
                                         device_id=dev, device_id_type=MESH).start()

    hbm = lambda t: pltpu.HBM(t.shape, t.dtype)
    res = pl.pallas_call(
        body, name=name,
        out_shape=(pltpu.SemaphoreType.DMA((n,)), pltpu.SemaphoreType.DMA((n,)), *[hbm(t) for t in bufs], *[hbm(t) for t in fresh], hbm(carrier)),
        in_specs=[HBM_SPEC] * (nb + 1), out_specs=(SEM_SPEC, SEM_SPEC) + (HBM_SPEC,) * (nb + nf + 1),
        input_output_aliases={**{i: 2 + i for i in range(nb)}, nb: 2 + nb + nf}, compiler_params=SPLIT_PARAMS,
    )(*[_in_hbm(t) for t in bufs], _in_hbm(carrier))
    return (res[0], res[1]), list(res[2:2 + nb]), list(res[2 + nb:2 + nb + nf]), res[2 + nb + nf]


def split_wait(name, plan, sems, bufs, carrier):
    nb = len(bufs)

    def body(*refs):
        for i, (s, d, dev) in enumerate(plan(list(refs[:nb]))):
            cp = pltpu.make_async_remote_copy(src_ref=s, dst_ref=d, send_sem=refs[nb].at[i], recv_sem=refs[nb + 1].at[i],
                                              device_id=dev, device_id_type=MESH)
            cp.wait_send()
            cp.wait_recv()

    hbm = lambda t: pltpu.HBM(t.shape, t.dtype)
    res = pl.pallas_call(
        body, name=name, out_shape=(*[hbm(t) for t in bufs], hbm(carrier)),
        in_specs=[HBM_SPEC] * nb + [SEM_SPEC, SEM_SPEC, HBM_SPEC], out_specs=(HBM_SPEC,) * (nb + 1),
        input_output_aliases={**{i: i for i in range(nb)}, nb + 2: nb}, compiler_params=SPLIT_PARAMS,
    )(*bufs, sems[0], sems[1], carrier)
    return list(res[:nb]), res[nb]


def _me_sib_chips():
    x, y, c, chips = _place()
    return 2 * x + y, c, (x, y, 1 - c), chips


def plan_gather_ici(refs):
    me, c, _, chips = _me_sib_chips()
    return [(r.at[me, c], r.at[me, c], (chip[0], chip[1], c)) for r in refs for chip in chips]


def plan_gather_pass(refs):
    _, c, sib, chips = _me_sib_chips()
    return [(r.at[2 * chip[0] + chip[1], c], r.at[2 * chip[0] + chip[1], c], sib) for r in refs for chip in chips]


def plan_pair(n_arrays):
    def plan(refs):
        _, c, sib, _ = _me_sib_chips()
        return [(refs[a].at[1 - c], refs[n_arrays + a], sib) for a in range(n_arrays)]
    return plan


def plan_chips(n_arrays):
    def plan(refs):
        _, c, _, chips = _me_sib_chips()
        return [(refs[a].at[2 * chip[0] + chip[1]], refs[n_arrays + a].at[k], (chip[0], chip[1], c))
                for a in range(n_arrays) for k, chip in enumerate(chips)]
    return plan


def plan_share(n_arrays):
    def plan(refs):
        _, _, sib, _ = _me_sib_chips()
        return [(refs[a], refs[n_arrays + a], sib) for a in range(n_arrays)]
    return plan


def swap_with_sibling(name, src, pick_other_half):
    shape = src.shape[1:] if pick_other_half else src.shape

    def body(src_ref, out_ref, ssem, rsem):
        x, y, c, _ = _place()
        cp = pltpu.make_async_remote_copy(src_ref=src_ref.at[1 - c] if pick_other_half else src_ref, dst_ref=out_ref,
                                          send_sem=ssem, recv_sem=rsem, device_id=(x, y, 1 - c), device_id_type=MESH)
        cp.start()
        cp.wait()

    return pl.pallas_call(
        body, name=name, out_shape=jax.ShapeDtypeStruct(shape, src.dtype), in_specs=[ANY], out_specs=ANY,
        scratch_shapes=[pltpu.SemaphoreType.DMA(()), pltpu.SemaphoreType.DMA(())],
    )(src)


def exchange_chips(name, src, per_chip):
    shape = src.shape[1:] if per_chip else src.shape

    def body(src_ref, out_ref, send_sems, recv_sems):
        x, y, c, chips = _place()
        cps = []
        for k, chip in enumerate(chips):
            s = src_ref.at[2 * chip[0] + chip[1]] if per_chip else src_ref
            cps.append(pltpu.make_async_remote_copy(src_ref=s, dst_ref=out_ref.at[k], send_sem=send_sems.at[k], recv_sem=recv_sems.at[k],
                                                    device_id=(chip[0], chip[1], c), device_id_type=MESH))
        for cp in cps:
            cp.start()
        for cp in cps:
            cp.wait()

    return pl.pallas_call(
        body, name=name, out_shape=jax.ShapeDtypeStruct((3,) + shape, src.dtype), in_specs=[ANY], out_specs=ANY,
        scratch_shapes=[pltpu.SemaphoreType.DMA((3,)), pltpu.SemaphoreType.DMA((3,))],
    )(src)


def pair_sum(name, p, got):
    _, _, rh, cw = p.shape
    tr = _pick(rh, max(16, min(512, ROW_BUDGET // (2 * cw * 10))), SUBLANES_BF16)

    def body(p_ref, got_ref, s_ref, own_ref):
        j = pl.program_id(1)
        tot = p_ref[...].astype(F32) + got_ref[...].astype(F32)
        s_ref[...] = tot.astype(BF16)

        @pl.when(j == 2 * lax.axis_index("x") + lax.axis_index("y"))
        def _():
            own_ref[...] = tot

    return pl.pallas_call(
        body, name=name, grid=(rh // tr, N_CHIPS),
        in_specs=[pl.BlockSpec((None, None, tr, cw), lambda i, j: (lax.axis_index("c"), j, i, 0)),
                  pl.BlockSpec((None, tr, cw), lambda i, j: (j, i, 0))],
        out_specs=[pl.BlockSpec((None, tr, cw), lambda i, j: (j, i, 0)),
                   pl.BlockSpec((tr, cw), lambda i, j: (i, 0))],
        out_shape=[jax.ShapeDtypeStruct((N_CHIPS, rh, cw), BF16), jax.ShapeDtypeStruct((rh, cw), F32)],
        compiler_params=_cparams(("arbitrary", "arbitrary")),
    )(p, got)


def chips_sum(name, own, parts):
    rh, cw = own.shape
    parts = parts.reshape(3 * rh, cw)
    return rowwise(name, lambda o, a, b, c: (((o + a.astype(F32)) + b.astype(F32)) + c.astype(F32),),
                   [(own, cw, 0), (parts, cw, 0, 0), (parts, cw, 0, rh), (parts, cw, 0, 2 * rh)], [], [(cw, F32)], M=rh)[0]


def all_reduce_small(buf):
    r = buf.shape[0]
    got = swap_with_sibling("ar_pair", buf, False)
    chip = rowwise("ar_pairsum", lambda a, b: (a + b,), [(buf, LANES, 0), (got, LANES, 0)], [], [(LANES, F32)], M=r)[0]
    parts = exchange_chips("ar_chips", chip, False).reshape(3 * r, LANES)
    return rowwise("ar_sum", lambda o, fx, fy, fxy: ((o + fy) + (fx + fxy),),
                   [(chip, LANES, 0), (parts, LANES, 0, 0), (parts, LANES, 0, r), (parts, LANES, 0, 2 * r)], [], [(LANES, F32)], M=r)[0]


def _adam_fn(w, g, m, v):
    m = ADAM_B1 * m + (1.0 - ADAM_B1) * g
    v = ADAM_B2 * v + (1.0 - ADAM_B2) * (g * g)
    m_hat = m / (1.0 - ADAM_B1 ** ADAM_STEP)
    v_hat = v / (1.0 - ADAM_B2 ** ADAM_STEP)
    return -ADAM_LR * (m_hat / (jnp.sqrt(v_hat) + ADAM_EPS) + ADAM_WD * w), m, v


def adamw(name, w, g, m, v):
    r, cw = w.shape
    return rowwise(name, _adam_fn, [(t, cw, 0) for t in (w, g, m, v)], [], [(cw, F32)] * 3, M=r)


def adamw_layers(name, w, m, v, mines, theirs, kind):
    L, K, nn = w.shape
    hr, hc = (K // 2, nn) if kind == "cols" else (K, nn // 2)
    tr = _pick(hr, max(8, min(256, MM_BUDGET // (2 * hc * 4 * (7 + 2 * L)))), 8)
    nb = hr // tr

    def body(*refs):
        w_ref, m_ref, v_ref = refs[:3]
        outs = refs[3 + 2 * L:]
        l, mine_here = pl.program_id(0), pl.program_id(1) == lax.axis_index("c")
        g = jnp.zeros((tr, hc), F32)
        for ll in range(L):
            g = jnp.where(l == ll, jnp.where(mine_here, refs[3 + ll][...], refs[3 + L + ll][...]), g)
        outs[0][...] = g
        outs[1][...], outs[2][...], outs[3][...] = _adam_fn(w_ref[...], g, m_ref[...], v_ref[...])

    if kind == "cols":
        full = pl.BlockSpec((None, tr, hc), lambda l, h, i: (l, h * nb + i, 0))
    else:
        full = pl.BlockSpec((None, tr, hc), lambda l, h, i: (l, i, h))
    def half_spec(ll, mine):
        def imap(l, h, i):
            here = (l == ll) & ((h == lax.axis_index("c")) == mine)
            return (jnp.where(here, i, 0), 0)
        return pl.BlockSpec((tr, hc), imap)

    halves = [half_spec(ll, True) for ll in range(L)] + [half_spec(ll, False) for ll in range(L)]
    return pl.pallas_call(
        body, name=name, grid=(L, 2, nb), in_specs=[full] * 3 + halves, out_specs=[full] * 4,
        out_shape=[jax.ShapeDtypeStruct((L, K, nn), F32)] * 4, compiler_params=_cparams(("parallel", "parallel", "parallel")),
    )(w, m, v, *mines, *theirs)


class Dims:
    def __init__(self, x, g_q, g_kv, g_out_mla, g_out_ssm, g_out_dil, ff):
        self.M, self.D = x.shape[-2], x.shape[-1]
        self.QL, self.KVL = g_q.shape[-1], g_kv.shape[-1]
        self.MW, self.SW, self.DW = g_out_mla.shape[-1], g_out_ssm.shape[-1], g_out_dil.shape[-1]
        self.H = self.MW // MLA_V
        self.FF = ff
        self.G = self.SW // SSM_GROUP
        self.C = self.G * SSM_STATE
        self.o_cq, self.o_u = 0, self.QL
        self.o_qd = self.o_u + self.SW
        self.o_kd = self.o_qd + self.DW
        self.o_vd = self.o_kd + self.DW
        self.o_ckv = self.o_vd + self.DW
        self.o_kr = self.o_ckv + self.KVL
        self.PW = -(-(self.o_kr + MLA_ROPE) // (4 * LANES)) * (4 * LANES)
        assert self.o_u % self.SW == 0 and self.o_qd % LANES == 0 and self.o_ckv % self.KVL == 0 and self.o_kr % LANES == 0
        assert self.H % 2 == 0 and self.DW % LANES == 0 and self.C % LANES == 0
        self.QW = self.H * (MLA_NOPE + MLA_ROPE)
        self.KVW = self.H * (MLA_NOPE + MLA_V)
        sizes = [self.QL, self.KVL, MLA_ROPE, self.SW, self.DW, self.DW, self.DW]
        starts = np.concatenate([[0], np.cumsum(sizes)[:-1]])
        self.ref_cols = {n: (int(s), int(z)) for n, s, z in zip(["cq", "ckv", "kr", "u", "qd", "kd", "vd"], starts, sizes)}
        self.INW = int(sum(sizes))
        self.new_order = ["cq", "u", "qd", "kd", "vd", "ckv", "kr"]
        src = np.concatenate([np.arange(self.ref_cols[n][0], self.ref_cols[n][0] + self.ref_cols[n][1]) for n in self.new_order])
        self.src_in = np.concatenate([src, -np.ones(self.PW - self.INW, np.int64)])
        self.src_q = self._heads_split(self.H, MLA_NOPE, MLA_ROPE)
        self.src_kv = self._heads_split(self.H, MLA_NOPE, MLA_V)

    @staticmethod
    def _heads_split(h, d1, d2):
        first = (np.arange(h)[:, None] * (d1 + d2) + np.arange(d1)[None, :]).reshape(-1)
        second = (np.arange(h)[:, None] * (d1 + d2) + d1 + np.arange(d2)[None, :]).reshape(-1)
        return np.concatenate([first, second])


def _regroup_in(dm, w):
    parts = [w[..., dm.ref_cols[n][0]:dm.ref_cols[n][0] + dm.ref_cols[n][1]] for n in dm.new_order]
    pad = dm.PW - dm.INW
    return jnp.concatenate(parts + [jnp.zeros(w.shape[:-1] + (pad,), w.dtype)], axis=-1)


def _ungroup_in(dm, w):
    off, pieces = 0, {}
    for n in dm.new_order:
        pieces[n] = w[..., off:off + dm.ref_cols[n][1]]
        off += dm.ref_cols[n][1]
    return jnp.concatenate([pieces[n] for n in ["cq", "ckv", "kr", "u", "qd", "kd", "vd"]], axis=-1)


def _split_heads(w, h, d1):
    t = w.reshape(w.shape[:-1] + (h, -1))
    return jnp.concatenate([t[..., :d1].reshape(w.shape[:-1] + (-1,)), t[..., d1:].reshape(w.shape[:-1] + (-1,))], axis=-1)


def _merge_heads(w, h, d1):
    a = w[..., :h * d1].reshape(w.shape[:-1] + (h, d1))
    b = w[..., h * d1:].reshape(w.shape[:-1] + (h, -1))
    return jnp.concatenate([a, b], axis=-1).reshape(w.shape[:-1] + (-1,))


def _cat_cols(re, im):
    r, c = re.shape
    return jnp.stack([re.reshape(r, c // LANES, LANES), im.reshape(r, c // LANES, LANES)], axis=2).reshape(r, 2 * c)


def _uncat_cols(cat):
    r, c2 = cat.shape
    t = cat.reshape(r, c2 // (2 * LANES), 2, LANES)
    return t[:, :, 0].reshape(r, c2 // 2), t[:, :, 1].reshape(r, c2 // 2)


def _block_diag(t, g):
    _, a, b = t.shape
    eye = jnp.eye(g, dtype=bool)[:, None, :, None]
    return jnp.where(eye, t[:, :, None, :], 0).reshape(g * a, g * b)


def _diag_blocks(m, g):
    a, b = m.shape[0] // g, m.shape[1] // g
    eye = jnp.eye(g, dtype=m.dtype)[:, None, :, None]
    return jnp.sum(m.reshape(g, a, g, b) * eye, axis=2)


def _rope_tables(dm):
    half = MLA_ROPE // 2
    inv_freq = ROPE_THETA ** (-jnp.arange(half, dtype=F32) / half)
    ang = jnp.arange(dm.M, dtype=F32)[:, None] * inv_freq[None, :]
    cos = jnp.concatenate([jnp.cos(ang), jnp.cos(ang)], axis=1)
    sin = jnp.concatenate([-jnp.sin(ang), jnp.sin(ang)], axis=1)
    return jnp.tile(cos, (1, dm.H)), jnp.tile(sin, (1, dm.H)), jnp.tile(cos, (1, LANES // MLA_ROPE)), jnp.tile(sin, (1, LANES // MLA_ROPE))


def _rope(x, cos, sin):
    return x * cos + _swap_halves(x, MLA_ROPE // 2) * sin


def _rope_t(d, cos, sin):
    return d * cos + _swap_halves(d * sin, MLA_ROPE // 2)


def _ssm_layer_params(dm, a_re, a_im, log_dt, b_re, b_im):
    flat = lambda t: t.reshape(1, dm.C)
    ldt = jnp.repeat(log_dt, SSM_STATE).reshape(1, dm.C)
    bt = lambda t: jnp.transpose(t, (2, 0, 1)).reshape(SSM_GROUP, dm.C)
    return flat(a_re), flat(a_im), ldt, bt(b_re), bt(b_im)


def layer_forward(dm, l, x, lw, sp, tabs, hook_q, hook_mid):
    M, D = dm.M, dm.D
    n = lambda s: f"{s}_l{l}"
    sv = {"x_in": x}
    h1 = rms_fwd(n("rms_mix"), x, D, 0, lw["g_mix"], M=M)
    proj = matmul(n("in_proj"), h1, lw["w_in"], "nn", M=M, N=dm.PW, K=D)
    sv.update(h1=h1, proj=proj)
    cqn = rms_fwd(n("rms_q"), proj, dm.QL, dm.o_cq, lw["g_q"], M=M)
    q = matmul(n("q_up"), cqn, lw["w_uq"], "nn", M=M, N=dm.QW, K=dm.QL)
    ckvn = rms_fwd(n("rms_kv"), proj, dm.KVL, dm.o_ckv, lw["g_kv"], M=M)
    kv = matmul(n("kv_up"), ckvn, lw["w_ukv"], "nn", M=M, N=dm.KVW, K=dm.KVL, out_dtype=BF16)
    cosq, sinq, cosk, sink = tabs[:4]
    nw = dm.H * MLA_NOPE

    def rope_fn(qb, kb, cq, sq, ck, sk):
        return jnp.concatenate([qb[:, :nw], _rope(qb[:, nw:], cq, sq)], axis=1), _rope(kb, ck, sk)

    pw = dm.H * MLA_ROPE
    q_bf, kpe = rowwise(n("rope"), rope_fn, [(q, dm.QW, 0), (proj, LANES, dm.o_kr), (cosq, pw, 0), (sinq, pw, 0), (cosk, LANES, 0), (sink, LANES, 0)],
                        [], [(dm.QW, BF16), (LANES, BF16)], M=M)
    mla_scale = (MLA_NOPE + MLA_ROPE) ** -0.5
    o_mla, lse_mla = attention_fwd(n("mla_fwd"), q_bf, 0, kv, 0, kv, nw, da=MLA_NOPE, dv=MLA_V, pairs=dm.H // 2, scale=mla_scale,
                                   M=M, qb=q_bf, qb_off=nw, kb=kpe)
    sv.update(cqn=cqn, ckvn=ckvn, q_bf=q_bf, kv=kv, kpe=kpe, o_mla=o_mla, lse_mla=lse_mla)
    bu = matmul(n("ssm_bu"), proj, sp["bcat"], "nn", M=M, N=2 * dm.C, K=dm.SW, a_off=(0, dm.o_u))
    hcat = ssm_scan(n("ssm_scan"), bu, sp["acat"], M=M)
    ylin = matmul(n("ssm_y"), hcat, sp["ccat"], "nn", M=M, N=dm.SW, K=2 * dm.C)
    yg = rowwise(n("ssm_gelu"), lambda y, u, d: (_gelu(y + d * u),), [(ylin, dm.SW, 0), (proj, dm.SW, dm.o_u)], [lw["d_skip"]],
                 [(dm.SW, BF16)], M=M)[0]
    z = matmul(n("ssm_glu"), yg, lw["w_glu"], "nn", w=("cols", 2 * dm.SW // N_CHIPS), M=M, N=2 * dm.SW, K=dm.SW)
    sw = dm.SW

    def glu_fn(zb, b):
        zz = zb + b
        return (zz[:, :sw] * jax.nn.sigmoid(zz[:, sw:]),)

    o_ssm = hook_q(rowwise(n("ssm_gate"), glu_fn, [(z, 2 * sw, 0)], [lw["b_glu"]], [(sw, F32)], M=M)[0])
    sv.update(hcat=hcat, ylin=ylin, yg=yg, z=z, o_ssm=o_ssm)
    o_dil, lse_dil = attention_fwd(n("dil_fwd"), proj, dm.o_qd, proj, dm.o_kd, proj, dm.o_vd, da=DIL_HEAD, dv=DIL_HEAD, pairs=dm.DW // LANES,
                                   scale=DIL_HEAD ** -0.5, M=M, bias=tabs[4])
    sv.update(o_dil=o_dil, lse_dil=lse_dil)
    yn = rowwise(n("out_norm"), lambda a, b, c, ga, gb, gc: (jnp.concatenate([_rms(a, ga), _rms(b, gb), _rms(c, gc)], axis=1),),
                 [(o_mla, dm.MW, 0), (o_ssm, dm.SW, 0), (o_dil, dm.DW, 0)], [lw["g_out_mla"], lw["g_out_ssm"], lw["g_out_dil"]],
                 [(D, BF16)], M=M)[0]
    yn = hook_mid(yn, lw)
    x_mid = matmul(n("out_proj"), yn, lw["w_o"], "nn", w=("rows", D // N_CHIPS), M=M, N=D, K=D, add=x)
    h2 = rms_fwd(n("rms_ffn"), x_mid, D, 0, lw["g_ffn"], M=M)
    ffs = dm.FF // N_CHIPS
    gate = matmul(n("ffn_gate"), h2, lw["w_gate"], "nn", w=("cols", ffs), M=M, N=dm.FF, K=D, out_dtype=BF16)
    up = matmul(n("ffn_up"), h2, lw["w_up"], "nn", w=("cols", ffs), M=M, N=dm.FF, K=D, out_dtype=BF16)

    def act_fn(gb, ub):
        gf = gb.astype(F32)
        return (gf * jax.nn.sigmoid(gf) * ub.astype(F32),)

    act = rowwise(n("ffn_act"), act_fn, [(gate, dm.FF, 0), (up, dm.FF, 0)], [], [(dm.FF, BF16)], M=M)[0]
    x_out = matmul(n("ffn_down"), act, lw["w_down"], "nn", w=("rows",ffs), M=M, N=D, K=dm.FF, add=x_mid)
    sv.update(yn=yn, x_mid=x_mid, h2=h2, gate=gate, up=up, act=act)
    return x_out, sv


def layer_backward(dm, l, dx, lw, sp, tabs, sv, hook_a, hook_m, hook_b):
    M, D = dm.M, dm.D
    n = lambda s: f"{s}_l{l}"
    g = {}
    ffs = dm.FF // N_CHIPS
    dact = matmul(n("ffn_down_dx"), dx, lw["w_down"], "nt", w=("rows", ffs), M=M, N=dm.FF, K=D, out_dtype=BF16)
    g["w_down"] = matmul(n("ffn_down_dw"), sv["act"], dx, "tn", M=dm.FF, N=D, K=M, out_dtype=BF16, into=("rows", ffs))

    def act_bwd(gb, ub, db):
        _, vjp = jax.vjp(lambda a, b: a * jax.nn.sigmoid(a) * b, gb.astype(F32), ub.astype(F32))
        return vjp(db.astype(F32))

    dgate, dup = rowwise(n("ffn_act_bwd"), act_bwd, [(sv["gate"], dm.FF, 0), (sv["up"], dm.FF, 0), (dact, dm.FF, 0)], [],
                         [(dm.FF, BF16), (dm.FF, BF16)], M=M)
    dh2 = matmul(n("ffn_gate_dx"), dgate, lw["w_gate"], "nt", w=("cols",ffs), M=M, N=D, K=dm.FF)
    dh2 = matmul(n("ffn_up_dx"), dup, lw["w_up"], "nt", w=("cols",ffs), M=M, N=D, K=dm.FF, add=dh2)
    g["w_gate"] = matmul(n("ffn_gate_dw"), sv["h2"], dgate, "tn", M=D, N=dm.FF, K=M, out_dtype=BF16, into=("cols", ffs))
    g["w_up"] = matmul(n("ffn_up_dw"), sv["h2"], dup, "tn", M=D, N=dm.FF, K=M, out_dtype=BF16, into=("cols", ffs))
    dx_mid, g["g_ffn"] = rms_bwd(n("rms_ffn_bwd"), sv["x_mid"], D, 0, lw["g_ffn"], dh2, dx, M=M)
    dx_mid = hook_a(dx_mid, g)
    dyn = matmul(n("out_proj_dx"), dx_mid, lw["w_o"], "nt", w=("rows", D // N_CHIPS), M=M, N=D, K=D)
    g["w_o"] = matmul(n("out_proj_dw"), sv["yn"], dx_mid, "tn", M=D, N=D, K=M, out_dtype=BF16, into=("rows", D // N_CHIPS))
    mw, sw, dw = dm.MW, dm.SW, dm.DW

    def out_norm_bwd(a, b, c, dy, ga, gb, gc):
        res, sums = [], []
        for t, gg, lo, hi in ((a, ga, 0, mw), (b, gb, mw, mw + sw), (c, gc, mw + sw, mw + sw + dw)):
            _, vjp = jax.vjp(_rms, t, gg)
            dt, dg = vjp(dy[:, lo:hi])
            res.append(dt)
            sums.append(dg)
        return res + sums

    do_mla, do_ssm, do_dil, g["g_out_mla"], g["g_out_ssm"], g["g_out_dil"] = rowwise(
        n("out_norm_bwd"), out_norm_bwd, [(sv["o_mla"], mw, 0), (sv["o_ssm"], sw, 0), (sv["o_dil"], dw, 0), (dyn, D, 0)],
        [lw["g_out_mla"], lw["g_out_ssm"], lw["g_out_dil"]], [(mw, F32), (sw, F32), (dw, F32)], [mw, sw, dw], M=M)
    proj = sv["proj"]
    dqd, dkd, dvd = attention_bwd(n("dil_bwd"), proj, dm.o_qd, proj, dm.o_kd, proj, dm.o_vd, sv["o_dil"], do_dil, sv["lse_dil"],
                                  da=DIL_HEAD, dv=DIL_HEAD, pairs=dw // LANES, scale=DIL_HEAD ** -0.5, M=M, bias=tabs[4])
    do_ssm = hook_m(do_ssm, g)
    def glu_bwd(zb, db, b):
        _, vjp = jax.vjp(lambda zz, bb: (zz + bb)[:, :sw] * jax.nn.sigmoid((zz + bb)[:, sw:]), zb, b)
        return vjp(db)

    dz, g["b_glu"] = rowwise(n("ssm_gate_bwd"), glu_bwd, [(sv["z"], 2 * sw, 0), (do_ssm, sw, 0)], [lw["b_glu"]], [(2 * sw, BF16)], [2 * sw], M=M)
    dyg = matmul(n("ssm_glu_dx"), dz, lw["w_glu"], "nt", w=("cols", 2 * sw // N_CHIPS), M=M, N=sw, K=2 * sw)
    g["w_glu"] = matmul(n("ssm_glu_dw"), sv["yg"], dz, "tn", M=sw, N=2 * sw, K=M, out_dtype=BF16, into=("cols", 2 * sw // N_CHIPS))

    def gelu_bwd(y, u, dy, d):
        _, vjp = jax.vjp(lambda yy, uu, dd: _gelu(yy + dd * uu), y, u, d)
        return vjp(dy)

    dylin, du1, g["d_skip"] = rowwise(n("ssm_gelu_bwd"), gelu_bwd, [(sv["ylin"], sw, 0), (proj, sw, dm.o_u), (dyg, sw, 0)], [lw["d_skip"]],
                                      [(sw, BF16), (sw, F32)], [sw], M=M)
    seed = matmul(n("ssm_y_dx"), dylin, sp["ccat"], "nt", M=M, N=2 * dm.C, K=sw)
    d_ccat = matmul(n("ssm_y_dw"), dylin, sv["hcat"], "tn", M=sw, N=2 * dm.C, K=M)
    lam, d_acat = ssm_scan(n("ssm_scan_bwd"), seed, sp["acat_conj"], M=M, reverse=True, hcat=sv["hcat"])
    du = matmul(n("ssm_bu_dx"), lam, sp["bcat"], "nt", M=M, N=sw, K=2 * dm.C, add=du1, out_dtype=BF16)
    d_bcat = matmul(n("ssm_bu_dw"), proj, lam, "tn", M=sw, N=2 * dm.C, K=M, a_off=(0, dm.o_u))
    g["ssm_raw"] = (d_acat, d_bcat, d_ccat)
    nw = dm.H * MLA_NOPE
    dqn, dkn, dv_, dqp, dkp = attention_bwd(n("mla_bwd"), sv["q_bf"], 0, sv["kv"], 0, sv["kv"], nw, sv["o_mla"], do_mla, sv["lse_mla"],
                                            da=MLA_NOPE, dv=MLA_V, pairs=dm.H // 2, scale=(MLA_NOPE + MLA_ROPE) ** -0.5, M=M,
                                            qb=sv["q_bf"], qb_off=nw, kb=sv["kpe"])
    cosq, sinq, cosk, sink = tabs[:4]
    pw = dm.H * MLA_ROPE
    dqp_u, dkr = rowwise(n("rope_bwd"), lambda a, b, cq, sq, ck, sk: (_rope_t(a, cq, sq), _rope_t(b, ck, sk)),
                         [(dqp, pw, 0), (dkp, LANES, 0), (cosq, pw, 0), (sinq, pw, 0), (cosk, LANES, 0), (sink, LANES, 0)], [],
                         [(pw, BF16), (LANES, BF16)], M=M)
    dq = lane_concat(n("dq_cat"), [dqn, dqp_u], M=M)
    dkv = lane_concat(n("dkv_cat"), [dkn, dv_], M=M)
    dcqn = matmul(n("q_up_dx"), dq, lw["w_uq"], "nt", M=M, N=dm.QL, K=dm.QW)
    g["w_uq"] = matmul(n("q_up_dw"), sv["cqn"], dq, "tn", M=dm.QL, N=dm.QW, K=M, out_dtype=BF16)
    dckvn = matmul(n("kv_up_dx"), dkv, lw["w_ukv"], "nt", M=M, N=dm.KVL, K=dm.KVW)
    g["w_ukv"] = matmul(n("kv_up_dw"), sv["ckvn"], dkv, "tn", M=dm.KVL, N=dm.KVW, K=M, out_dtype=BF16)
    dcq, g["g_q"] = rms_bwd(n("rms_q_bwd"), proj, dm.QL, dm.o_cq, lw["g_q"], dcqn, M=M, out_dtype=BF16)
    dckv, g["g_kv"] = rms_bwd(n("rms_kv_bwd"), proj, dm.KVL, dm.o_ckv, lw["g_kv"], dckvn, M=M, out_dtype=BF16)
    dproj = hook_b(lane_concat(n("dproj_cat"), [dcq, du, dqd, dkd, dvd, dckv, dkr], M=M, pad_to=dm.PW), g)
    dh1 = matmul(n("in_proj_dx"), dproj, lw["w_in"], "nt", M=M, N=D, K=dm.PW)
    g["w_in"] = matmul(n("in_proj_dw"), sv["h1"], dproj, "tn", M=D, N=dm.PW, K=M, out_dtype=BF16)
    dx_in, g["g_mix"] = rms_bwd(n("rms_mix_bwd"), sv["x_in"], D, 0, lw["g_mix"], dh1, dx_mid, M=M)
    return dx_in, g


def layer_params(dm, small, l):
    lw = {k: small[k][l].reshape(1, -1) for k in ("g_mix", "g_q", "g_kv", "b_glu", "g_out_mla", "g_out_ssm", "g_out_dil", "g_ffn", "d_skip")}
    raw = _ssm_layer_params(dm, small["a_re"][l], small["a_im"][l], small["log_dt"][l], small["b_re"][l], small["b_im"][l])
    ar, ai, bbr, bbi = ssm_param_fwd(f"ssm_param_l{l}", *raw)
    g_ = dm.G
    bd = lambda t: _block_diag(jnp.transpose(t.reshape(SSM_GROUP, g_, SSM_STATE), (1, 0, 2)), g_)
    cd = lambda t: _block_diag(jnp.transpose(t, (0, 2, 1)), g_)
    cre, cim = cd(small["c_re"][l]), cd(small["c_im"][l])
    sp = {"acat": _cat_cols(ar, ai), "acat_conj": _cat_cols(ar, -ai),
          "bcat": _cat_cols(bd(bbr), bd(bbi)).astype(BF16),
          "ccat": _cat_cols(cre.T, -cim.T).T.astype(BF16)}
    return lw, sp, raw


def ssm_param_grads(dm, l, g, raw):
    d_acat, d_bcat, d_ccat_t = g.pop("ssm_raw")
    d_ar, d_ai = _uncat_cols(d_acat)
    dbr, dbi = ssm_diag_rows(f"ssm_db_l{l}", d_bcat)
    dcr, dci = ssm_diag_rows(f"ssm_dc_l{l}", d_ccat_t)
    g_ = dm.G
    da_re, da_im, dldt, db_re, db_im = ssm_param_bwd(f"ssm_param_bwd_l{l}", *raw, d_ar, d_ai, dbr, dbi)
    g["a_re"], g["a_im"] = da_re.reshape(g_, SSM_STATE), da_im.reshape(g_, SSM_STATE)
    g["log_dt"] = jnp.sum(dldt.reshape(g_, SSM_STATE), axis=1)
    from_rows = lambda t: jnp.transpose(t.reshape(SSM_GROUP, g_, SSM_STATE), (1, 2, 0))
    g["b_re"], g["b_im"] = from_rows(db_re), from_rows(db_im)
    g["c_re"] = jnp.transpose(dcr.reshape(SSM_GROUP, g_, SSM_STATE), (1, 0, 2))
    g["c_im"] = -jnp.transpose(dci.reshape(SSM_GROUP, g_, SSM_STATE), (1, 0, 2))
    g["d_skip"] = g["d_skip"].reshape(g_, SSM_GROUP)


def loss_and_grad(dm, h, target, g_final):
    D = dm.D

    def loss_fn(xb, tb, gb):
        y, vjp = jax.vjp(_rms, xb, gb)
        err = y - tb
        dxb, dg = vjp(err * (1.0 / D))
        part = 0.5 * jnp.sum(jnp.mean(err * err, axis=-1, keepdims=True), axis=0, keepdims=True)
        lane = lax.broadcasted_iota(jnp.int32, (1, LANES), 1)
        return dxb, dg, jnp.where(lane == 0, part, 0.0)

    return rowwise("loss", loss_fn, [(h, D, 0), (target, D, 0)], [g_final.reshape(1, D)], [(D, F32)], [D, LANES], M=dm.M)


KIND = {"w_in": "cols", "w_uq": "cols", "w_ukv": "cols", "w_glu": "cols", "w_o": "rows", "w_gate": "cols", "w_up": "cols", "w_down": "rows"}
SHARDED = list(KIND)
TRANSPOSED = ("w_in",)
GATHER_GROUPS = {"mixer": ["w_in", "w_uq", "w_ukv", "w_glu"], "rest": ["w_o", "w_gate", "w_up", "w_down"]}
REDUCE_GROUPS = {"ffn": ["w_gate", "w_up", "w_down"], "others": ["w_o", "w_in", "w_uq", "w_ukv", "w_glu"]}
SMALL = ["g_mix", "g_q", "g_kv", "a_re", "a_im", "b_re", "b_im", "c_re", "c_im", "d_skip", "log_dt", "b_glu",
         "g_out_mla", "g_out_ssm", "g_out_dil", "g_ffn", "g_final"]
ORDER = ["g_mix", "w_in", "g_q", "w_uq", "g_kv", "w_ukv", "a_re", "a_im", "b_re", "b_im", "c_re", "c_im", "d_skip", "log_dt",
         "w_glu", "b_glu", "g_out_mla", "g_out_ssm", "g_out_dil", "w_o", "g_ffn", "w_gate", "w_up", "w_down", "g_final"]


def kernel(x, g_mix, w_in, g_q, w_uq, g_kv, w_ukv, a_re, a_im, b_re, b_im, c_re, c_im, d_skip, log_dt, w_glu, b_glu, g_out_mla, g_out_ssm, g_out_dil, w_o, g_ffn, w_gate, w_up, w_down, g_final, loss_target, m_g_mix, m_w_in, m_g_q, m_w_uq, m_g_kv, m_w_ukv, m_a_re, m_a_im, m_b_re, m_b_im, m_c_re, m_c_im, m_d_skip, m_log_dt, m_w_glu, m_b_glu, m_g_out_mla, m_g_out_ssm, m_g_out_dil, m_w_o, m_g_ffn, m_w_gate, m_w_up, m_w_down, m_g_final, v_g_mix, v_w_in, v_g_q, v_w_uq, v_g_kv, v_w_ukv, v_a_re, v_a_im, v_b_re, v_b_im, v_c_re, v_c_im, v_d_skip, v_log_dt, v_w_glu, v_b_glu, v_g_out_mla, v_g_out_ssm, v_g_out_dil, v_w_o, v_g_ffn, v_w_gate, v_w_up, v_w_down, v_g_final):
    args = locals()
    w = {k: args[k] for k in ORDER}
    mom = {k: args["m_" + k] for k in ORDER}
    var = {k: args["v_" + k] for k in ORDER}
    dm = Dims(x, g_q, g_kv, g_out_mla, g_out_ssm, g_out_dil, w_gate.shape[-1] * N_CHIPS)
    L = g_mix.shape[0]

    small = {k: w[k] for k in SMALL}
    na = len(SHARDED)
    tabs = _rope_tables(dm) + (dilated_bias(dm.M),)
    sel ={"w_in": selection_matrices(dm.src_in, w_in.shape[-1]), "w_uq": selection_matrices(dm.src_q, w_uq.shape[-1]),
           "w_ukv": selection_matrices(dm.src_kv, w_ukv.shape[-1])}

    stored = lambda t, k: jnp.swapaxes(t, 1, 2) if k in TRANSPOSED else t
    store_kind = {k: "rows" if k in TRANSPOSED else KIND[k] for k in SHARDED}
    G = [{k: cast_into_slot(f"cast_{k}_l{l}", stored(w[k], k), store_kind[k], l) for k in SHARDED} for l in range(L)]
    sems = {}

    def gather_stage(stage, plan):
        def start(l, grp, car):
            names = GATHER_GROUPS[grp]
            sems[stage, l, grp], bufs, _, car = split_start(f"ag_{stage}_start_{grp}_l{l}", plan, 3 * len(names), [G[l][k] for k in names], [], car)
            G[l].update(zip(names, bufs))
            return car

        def wait(l, grp, car):
            names = GATHER_GROUPS[grp]
            bufs, car = split_wait(f"ag_{stage}_wait_{grp}_l{l}", plan, sems[stage, l, grp], [G[l][k] for k in names], car)
            G[l].update(zip(names, bufs))
            return car

        return start, wait

    ici_start, ici_wait = gather_stage("ici", plan_gather_ici)
    pass_start, pass_wait = gather_stage("pass", plan_gather_pass)

    car = tabs[0]
    for l, grp in ((0, "mixer"), (0, "rest"), (1, "mixer")):
        if l < L:
            car = ici_start(l, grp, car)
    tabs = (pass_wait(0, "mixer", pass_start(0, "mixer", ici_wait(0, "mixer", car))),) + tabs[1:]
    h = x.reshape(dm.M, dm.D)
    lws, sps, raws, saved = [], [], [], []
    for l in range(L):
        lw, sp, raw = layer_params(dm, small, l)
        for k in GATHER_GROUPS["mixer"]:
            lw[k] = regroup_cols(f"regroup_{k}_l{l}", G[l][k], sel[k], k in TRANSPOSED) if k in sel else G[l][k]

        def at_q(car, l=l):
            car = pass_start(l, "rest", ici_wait(l, "rest", car))
            if l + 1 < L:
                car = pass_start(l + 1, "mixer", ici_wait(l + 1, "mixer", car))
            return car

        def at_mid(car, lw, l=l):
            car = pass_wait(l, "rest", car)
            lw.update({k: G[l][k] for k in GATHER_GROUPS["rest"]})
            if l + 1 < L:
                car = ici_start(l + 1, "rest", pass_wait(l + 1, "mixer", car))
            if l + 2 < L:
                car = ici_start(l + 2, "mixer", car)
            return car

        h, sv = layer_forward(dm, l, h, lw, sp, tabs, at_q, at_mid)
        lws.append(lw)
        sps.append(sp)
        raws.append(raw)
        saved.append(sv)
    dx, g_final_part, loss_part = loss_and_grad(dm, h, loss_target.reshape(dm.M, dm.D), w["g_final"])

    def reduce_begin(l, grp, g, car):
        names = REDUCE_GROUPS[grp]
        parts = [g.pop(k) for k in names]
        fresh = [jax.ShapeDtypeStruct(p.shape[1:], BF16) for p in parts]
        sm, parts, gots, car = split_start(f"rs_pair_start_{grp}_l{l}", plan_pair(len(names)), len(names), parts, fresh, car)
        return {"l": l, "grp": grp, "names": names, "sems": sm, "parts": parts, "gots": gots}, car

    def reduce_chips(st, car):
        names, tag, n = st["names"], f"{st['grp']}_l{st['l']}", len(st["names"])
        bufs, car = split_wait(f"rs_pair_wait_{tag}", plan_pair(n), st["sems"], st["parts"] + st["gots"], car)
        sums = [pair_sum(f"rs_pairsum_{a}_l{st['l']}", bufs[i], bufs[n + i]) for i, a in enumerate(names)]
        fresh = [jax.ShapeDtypeStruct((3,) + s.shape[1:], BF16) for s, _ in sums]
        st["sems"], st["s"], st["arrived"], car = split_start(f"rs_chips_start_{tag}", plan_chips(n), 3 * n, [s for s, _ in sums], fresh, car)
        st["own"] = [o for _, o in sums]
        return car

    def reduce_share(st, car):
        names, tag, n = st["names"], f"{st['grp']}_l{st['l']}", len(st["names"])
        bufs, car = split_wait(f"rs_chips_wait_{tag}", plan_chips(n), st["sems"], st["s"] + st["arrived"], car)
        mine = [chips_sum(f"rs_sum_{a}_l{st['l']}", st["own"][i], bufs[n + i]) for i, a in enumerate(names)]
        fresh = [jax.ShapeDtypeStruct(m_.shape, F32) for m_ in mine]
        st["sems"], st["mine"], st["theirs"], car = split_start(f"rs_share_start_{tag}", plan_share(n), n, mine, fresh, car)
        return car

    def reduce_end(st, car):
        n = len(st["names"])
        bufs, car = split_wait(f"rs_share_wait_{st['grp']}_l{st['l']}", plan_share(n), st["sems"], st["mine"] + st["theirs"], car)
        for i, k in enumerate(st["names"]):
            reduced[st["l"]][k] = (bufs[i], bufs[n + i])
        return car

    reduced, grads, prev_ffn, prev_oth = [{} for _ in range(L)], [None] * L, None, None
    for l in reversed(range(L)):
        mine = {}

        def at_a(car, g, l=l, mine=mine, pf=prev_ffn, po=prev_oth):
            mine["st"], car = reduce_begin(l, "ffn", g, car)
            if pf is not None:
                car = reduce_chips(po, reduce_share(pf, car))
            return car

        def at_m(car, g, mine=mine, pf=prev_ffn):
            car = reduce_chips(mine["st"], car)
            return car if pf is None else reduce_end(pf, car)

        def at_b(car, g, po=prev_oth):
            return car if po is None else reduce_share(po, car)

        dx, g = layer_backward(dm, l, dx, lws[l], sps[l], tabs, saved[l], at_a, at_m, at_b)
        ssm_param_grads(dm, l, g, raws[l])
        car = dx if l else loss_part
        if prev_oth is not None:
            car = reduce_end(prev_oth, car)
        for k in sel:
            g[k] = ungroup_cols(f"ungroup_{k}_l{l}", g[k], sel[k], k in TRANSPOSED)
        prev_ffn = mine["st"]
        prev_oth, car = reduce_begin(l, "others", g, car)
        if l:
            dx = car
        grads[l] = g
    car = reduce_end(prev_ffn, reduce_share(prev_ffn, reduce_chips(prev_oth, car)))
    loss_part = reduce_end(prev_oth, reduce_share(prev_oth, car))

    gsum = {}
    small_names = [k for k in SMALL if k != "g_final"]
    pieces = [jnp.stack([grads[l][k] for l in range(L)]).reshape(-1) for k in small_names] + [g_final_part.reshape(-1), loss_part.reshape(-1)]
    sizes = [int(p.shape[0]) for p in pieces]
    total = sum(sizes)
    rows = -(-total // (LANES * 16)) * 16
    pack = lambda ps: jnp.concatenate(ps + [jnp.zeros((rows * LANES - total,), F32)]).reshape(rows, LANES)
    red = all_reduce_small(pack(pieces))
    flat = red.reshape(-1)
    offs = np.concatenate([[0], np.cumsum(sizes)]).astype(int)
    names = small_names + ["g_final"]
    for i, k in enumerate(names):
        gsum[k] = flat[offs[i]:offs[i + 1]].reshape(w[k].shape)
    loss = flat[offs[len(names)]]

    delta, new_m, new_v = {}, {}, {}
    for k in SHARDED:
        res = adamw_layers(f"adam_{k}", stored(w[k], k), stored(mom[k], k), stored(var[k], k), [reduced[l][k][0] for l in range(L)],
                           [reduced[l][k][1] for l in range(L)], store_kind[k])
        gsum[k], delta[k], new_m[k], new_v[k] = (stored(t, k) for t in res)
    sm_sizes = sizes[:len(names)]
    sm_total = sum(sm_sizes)
    packs = lambda d: jnp.concatenate([d[k].reshape(-1) for k in names] + [jnp.zeros((rows * LANES - sm_total,), F32)]).reshape(rows, LANES)
    gs = jnp.concatenate([flat[:sm_total], jnp.zeros((rows * LANES - sm_total,), F32)]).reshape(rows, LANES)
    d_, m_, v_ = adamw("adam_small", packs(w), gs, packs(mom), packs(var))
    for i, k in enumerate(names):
        sl = slice(offs[i], offs[i + 1])
        delta[k], new_m[k], new_v[k] = (t.reshape(-1)[sl].reshape(w[k].shape) for t in (d_, m_, v_))

    return (loss, dx.reshape(x.shape), *[gsum[k] for k in ORDER], *[delta[k] for k in ORDER],
            *[new_m[k] for k in ORDER], *[new_v[k] for k in ORDER])
```

```python
import functools
import math

import numpy as np
import jax
import jax.numpy as jnp
from jax import lax
from jax.experimental import pallas as pl
from jax.experimental.pallas import tpu as pltpu

F32 = jnp.float32
BF16 = jnp.bfloat16
MESH = pl.DeviceIdType.MESH

NORM_EPS = 1e-6
MLA_NOPE, MLA_ROPE, MLA_V = 128, 64, 128
SSM_GROUP, SSM_STATE = 16, 64
DIL_HEAD = 64
DIL_PATTERNS = ((128, 1), (512, 4), (2048, 16))
ROPE_THETA = 10000.0
ADAM_LR, ADAM_B1, ADAM_B2, ADAM_EPS, ADAM_WD, ADAM_STEP = 0.001, 0.9, 0.999, 1e-08, 0.01, 10
N_CHIPS = 4

LANES = 128
SUBLANES_BF16 = 16
VMEM_LIMIT = 56 * 1024 * 1024
ROW_BUDGET = 20 * 1024 * 1024
MM_BUDGET = 40 * 1024 * 1024
NEG = -1e30


def _cparams(sem=None):
    return pltpu.CompilerParams(dimension_semantics=sem, vmem_limit_bytes=VMEM_LIMIT)


def _pick(n, cap, q, off=0):
    best = None
    for d in range(q, min(n, cap) + 1, q):
        if n % d == 0 and off % d == 0:
            best = d
    if best is None or (best * 4 <= min(cap, n) and n <= 3072 and off % n == 0):
        assert off % n == 0, (n, off)
        return n
    return best


_DOT_DIMS = {"nn": (((1,), (0,)), ((), ())), "nt": (((1,), (1,)), ((), ())), "tn": (((0,), (0,)), ((), ()))}


def _divs(n, q, within=None, off=0):
    return [d for d in range(q, n + 1, q) if n % d == 0 and off % d == 0 and (within is None or within % d == 0)] or [n]


def _mm_tiles(M, N, K, tms, tns, tks, ab, bb, ob):
    best = None
    for tk in tks:
        nk = K // tk
        for tn in tns:
            for tm in tms:
                if 2 * (tm * tk * ab + tk * tn * bb + tm * tn * ob) + tm * tn * 4 * (2 if nk > 1 else 1) > MM_BUDGET:
                    continue
                steps = (M // tm) * (N // tn) * nk
                hbm = M * K * ab * (1 if nk == 1 else N // tn) + K * N * bb * (M // tm) + M * N * ob
                cost = steps * 0.35e-6 + hbm / 3.0e12 + (nk - 1) * M * N * 12 / 4.0e12
                if best is None or cost < best[0]:
                    best = (cost, tm, tn, tk)
    assert best is not None, (M, N, K)
    return best[1:]


def matmul(name, a, b, mode, *, M, N, K, a_off=(0, 0), b_off=(0, 0), b_lead=None, w=None, add=None, out_dtype=F32, into=None):
    tn_mode = mode == "tn"
    a_ro, a_co = (a_off[1], a_off[0]) if tn_mode else a_off
    b_no, b_ko = b_off if mode == "nt" else (b_off[1], b_off[0])
    n_within = k_within = m_within = None
    if w is not None:
        kind, shard = w
        if kind == "cols":
            b = b.reshape(N_CHIPS, b.shape[1] * b.shape[2], b.shape[3])
        rows_within, cols_within = (K if mode == "nn" else N, shard) if kind == "cols" else (shard, (N if mode == "nn" else K) // 2)
        k_within, n_within = (rows_within, cols_within) if mode == "nn" else (cols_within, rows_within)
    if into is not None:
        m_within, n_within = (M // 2, into[1]) if into[0] == "cols" else (into[1], N // 2)
    tms = [d for d in _divs(M, 128 if tn_mode else SUBLANES_BF16, m_within, a_ro) if d <= 1408]
    tns = [d for d in _divs(N, LANES, n_within, b_no) if d <= 2048]
    tks = _divs(K, SUBLANES_BF16 if tn_mode else LANES, k_within, math.gcd(a_co, b_ko))
    ob = jnp.dtype(out_dtype).itemsize + (add.dtype.itemsize if add is not None else 0)
    tm, tn, tk = _mm_tiles(M, N, K, tms, tns, tks, a.dtype.itemsize, b.dtype.itemsize, ob)
    nk = K // tk
    dn = _DOT_DIMS[mode]

    if tn_mode:
        a_spec = pl.BlockSpec((tk, tm), lambda i, j, k: (k + a_co // tk, i + a_ro // tm))
    else:
        a_spec = pl.BlockSpec((tm, tk), lambda i, j, k: (i + a_ro // tm, k + a_co // tk))
    b_blk = (tn, tk) if mode == "nt" else (tk, tn)
    if w is not None:
        tr_, tc_ = (tk, tn) if mode == "nn" else (tn, tk)
        rper, cper = rows_within // tr_, cols_within // tc_

        def wmap(rb, cb):
            if kind == "cols":
                return (cb // cper, rb, cb % cper)
            return (rb // rper, cb // cper, rb % rper, cb % cper)

        imap = (lambda i, j, k: wmap(k, j)) if mode == "nn" else (lambda i, j, k: wmap(j, k))
        b_spec = pl.BlockSpec((None,) * (b.ndim - 2) + b_blk, imap)
    else:
        if mode == "nt":
            imap = lambda i, j, k: (j + b_no // tn, k + b_ko // tk)
        else:
            imap = lambda i, j, k: (k + b_ko // tk, j + b_no // tn)
        if b_lead is None:
            b_spec = pl.BlockSpec(b_blk, imap)
        else:
            b_spec = pl.BlockSpec((None,) + b_blk, lambda i, j, k: (b_lead,) + imap(i, j, k))
    o_plain = pl.BlockSpec((tm, tn), lambda i, j, k: (i, j))
    if into is None:
        o_spec, out_shape = o_plain, jax.ShapeDtypeStruct((M, N), out_dtype)
    else:
        rper, cper = m_within // tm, n_within // tn
        if into[0] == "cols":
            o_spec = pl.BlockSpec((None, None, tm, tn), lambda i, j, k: (i // rper, j // cper, i % rper, j % cper))
        else:
            o_spec = pl.BlockSpec((None, None, tm, tn), lambda i, j, k: (j // cper, i // rper, i % rper, j % cper))
        out_shape = jax.ShapeDtypeStruct((2, N_CHIPS, m_within, n_within), out_dtype)
    has_add = add is not None
    n_in = 2 + has_add

    def body(*refs):
        a_ref, b_ref = refs[0], refs[1]
        add_ref = refs[2] if has_add else None
        o_ref = refs[n_in]
        part = lax.dot_general(a_ref[...].astype(BF16), b_ref[...].astype(BF16), dn, preferred_element_type=F32)

        def finish(r):
            if has_add:
                r = r + add_ref[...].astype(F32)
            o_ref[...] = r.astype(o_ref.dtype)

        if nk == 1:
            finish(part)
        else:
            acc_ref = refs[-1]
            k = pl.program_id(2)

            @pl.when(k == 0)
            def _():
                acc_ref[...] = part

            @pl.when((k > 0) & (k < nk - 1))
            def _():
                acc_ref[...] += part

            @pl.when(k == nk - 1)
            def _():
                finish(acc_ref[...] + part)

    in_specs = [a_spec, b_spec] + ([o_plain] if has_add else [])
    args = (a, b) + ((add,) if has_add else ())
    return pl.pallas_call(
        body, name=name, out_shape=out_shape, grid=(M // tm, N // tn, nk), in_specs=in_specs, out_specs=o_spec,
        scratch_shapes=[pltpu.VMEM((tm, tn), F32)] if nk > 1 else [],
        compiler_params=_cparams(("parallel", "parallel", "arbitrary")),
    )(*args)


def selection_matrices(src_of_new, n_shard):
    src_np = np.asarray(src_of_new, np.int64)
    src = jnp.asarray(src_np.astype(np.int32))
    ref = jnp.arange(N_CHIPS, dtype=jnp.int32)[:, None] * n_shard + jnp.arange(n_shard, dtype=jnp.int32)[None, :]
    pm = (ref[:, :, None] == src[None, None, :]).astype(BF16)
    pmt = (src[None, :, None] == ref[:, None, :]).astype(BF16)
    tc = _pick(len(src_np), 512, LANES)
    feeds = [sorted({int(s) // n_shard for s in src_np[cb * tc:(cb + 1) * tc] if s >= 0}) for cb in range(len(src_np) // tc)]
    return pm, pmt, tc, feeds


def regroup_cols(name, g, sel, transposed=False):
    pm, _, tc, feeds = sel
    n_new = pm.shape[-1]
    if transposed:
        nn, kh = g.shape[2:]
        K = 2 * kh
        tm = _pick(kh, 512, LANES)
        hb = kh // tm
        g_spec = pl.BlockSpec((N_CHIPS, None, nn, tm), lambda c, i: (0, i // hb, 0, i % hb))
    else:
        nn = g.shape[-1]
        g = g.reshape(N_CHIPS, -1, nn)
        K = g.shape[1]
        tm = _pick(K, 512, SUBLANES_BF16)
        g_spec = pl.BlockSpec((N_CHIPS, tm, nn), lambda c, i: (0, i, 0))

    def body(g_ref, pm_ref, o_ref):
        for cb, chips in enumerate(feeds):
            @pl.when(pl.program_id(0) == cb)
            def _(chips=chips):
                acc = jnp.zeros((tm, tc), F32)
                for j in chips:
                    acc = acc + _dot(g_ref[j], pm_ref[j], "tn" if transposed else "nn")
                o_ref[...] = acc.astype(o_ref.dtype)

    return pl.pallas_call(
        body, name=name, out_shape=jax.ShapeDtypeStruct((K, n_new), BF16), grid=(n_new // tc, K // tm),
        in_specs=[g_spec, pl.BlockSpec((N_CHIPS, nn, tc), lambda c, i: (0, 0, c))],
        out_specs=pl.BlockSpec((tm, tc), lambda c, i: (i, c)), compiler_params=_cparams(("parallel", "parallel")),
    )(g, pm)


def ungroup_cols(name, dw, sel, transposed=False):
    pm, pmt, tc, feeds = sel
    K, n_new = dw.shape
    nn = pmt.shape[-1]
    kh = K // 2
    tm = _pick(kh, 512, LANES if transposed else SUBLANES_BF16)
    hb = kh // tm
    fed_by = [[cb for cb, chips in enumerate(feeds) if j in chips] for j in range(N_CHIPS)]

    def body(dw_ref, sel_ref, o_ref):
        for j, blocks in enumerate(fed_by):
            @pl.when(pl.program_id(0) == j)
            def _(blocks=blocks):
                acc = jnp.zeros((nn, tm) if transposed else (tm, nn), F32)
                for cb in blocks:
                    cols = slice(cb * tc, (cb + 1) * tc)
                    if transposed:
                        acc = acc + _dot(sel_ref[:, cols], dw_ref[:, cols], "nt")
                    else:
                        acc = acc + _dot(dw_ref[:, cols], sel_ref[cols, :], "nn")
                o_ref[...] = acc.astype(o_ref.dtype)

    if transposed:
        sel_arr, sel_spec = pm, pl.BlockSpec((None, nn, n_new), lambda j, i: (j, 0, 0))
        out_shape = jax.ShapeDtypeStruct((2, N_CHIPS, nn, kh), BF16)
        out_spec = pl.BlockSpec((None, None, nn, tm), lambda j, i: (i // hb, j, 0, i % hb))
    else:
        sel_arr, sel_spec = pmt, pl.BlockSpec((None, n_new, nn), lambda j, i: (j, 0, 0))
        out_shape = jax.ShapeDtypeStruct((2, N_CHIPS, kh, nn), BF16)
        out_spec = pl.BlockSpec((None, None, tm, nn), lambda j, i: (i // hb, j, i % hb, 0))
    return pl.pallas_call(
        body, name=name, out_shape=out_shape, grid=(N_CHIPS, 2 * hb),
        in_specs=[pl.BlockSpec((tm, n_new), lambda j, i: (i, 0)), sel_spec], out_specs=out_spec,
        compiler_params=_cparams(("parallel", "parallel")),
    )(dw, sel_arr)


def rowwise(name, fn, rows, vecs, outs, sums=(), *, M):
    rows = [tuple(r) + (0,) * (4 - len(r)) for r in rows]
    nr, nv, no, ns = len(rows), len(vecs), len(outs), len(sums)
    per_row = sum(w * a.dtype.itemsize for a, w, _, _ in rows) + sum(w * jnp.dtype(d).itemsize for w, d in outs)
    tr = _pick(M, max(8, min(512, ROW_BUDGET // (2 * per_row))), 16 if M % 16 == 0 else 8)

    def body(*refs):
        i = pl.program_id(0)
        res = fn(*[r[...] for r in refs[:nr + nv]])
        o_refs = refs[nr + nv:nr + nv + no]
        s_refs = refs[nr + nv + no:]
        for ref, val in zip(o_refs, res[:no]):
            ref[...] = val.astype(ref.dtype)
        if ns:
            @pl.when(i == 0)
            def _():
                for ref in s_refs:
                    ref[...] = jnp.zeros(ref.shape, F32)

            for ref, val in zip(s_refs, res[no:]):
                ref[...] += val

    in_specs = [pl.BlockSpec((tr, w), functools.partial(lambda i, cb, rb: (i + rb, cb), cb=off // w, rb=roff // tr)) for _, w, off, roff in rows]
    for _, w, off, roff in rows:
        assert off % w == 0 and roff % tr == 0
    in_specs += [pl.BlockSpec(v.shape, functools.partial(lambda i, nd: (0,) * nd, nd=v.ndim)) for v in vecs]
    out_specs = [pl.BlockSpec((tr, w), lambda i: (i, 0)) for w, _ in outs]
    out_specs += [pl.BlockSpec((1, w), lambda i: (0, 0)) for w in sums]
    out_shape = [jax.ShapeDtypeStruct((M, w), d) for w, d in outs] + [jax.ShapeDtypeStruct((1, w), F32) for w in sums]
    return pl.pallas_call(
        body, name=name, out_shape=out_shape, grid=(M // tr,), in_specs=in_specs, out_specs=out_specs,
        compiler_params=_cparams(("arbitrary",) if ns else ("parallel",)),
    )(*[r[0] for r in rows], *vecs)


def _rms(x, g):
    xf = x.astype(F32)
    return xf * lax.rsqrt(jnp.mean(xf * xf, axis=-1, keepdims=True) + NORM_EPS) * g


def _gelu(y):
    return 0.5 * y * (1.0 + jnp.tanh(math.sqrt(2.0 / math.pi) * (y + 0.044715 * (y * y * y))))


def _colsum(v):
    return jnp.sum(v, axis=0, keepdims=True)


def rms_fwd(name, x, width, off, g, *, M):
    return rowwise(name, lambda xb, gb: (_rms(xb, gb),), [(x, width, off)], [g], [(width, BF16)], M=M)[0]


def rms_bwd(name, x, width, off, g, dy, resid=None, *, M, out_dtype=F32):
    def fn(xb, dyb, *rest):
        gb = rest[-1]
        _, vjp = jax.vjp(_rms, xb.astype(F32), gb)
        dx, dg = vjp(dyb.astype(F32))
        if resid is not None:
            dx = dx + rest[0]
        return dx, dg

    rows = [(x, width, off), (dy, width, 0)] + ([(resid, width, 0)] if resid is not None else [])
    return rowwise(name, fn, rows, [g], [(width, out_dtype)], [width], M=M)


def lane_concat(name, parts, *, M, pad_to=None):
    width = sum(p.shape[1] for p in parts)
    pad = 0 if pad_to is None else pad_to - width

    def fn(*blocks):
        cols = [b.astype(BF16) for b in blocks]
        if pad:
            cols.append(jnp.zeros((blocks[0].shape[0], pad), BF16))
        return (jnp.concatenate(cols, axis=1),)

    return rowwise(name, fn, [(p, p.shape[1], 0) for p in parts], [], [(width + pad, BF16)], M=M)[0]


def _swap_halves(x, half):
    w = x.shape[-1]
    lane = lax.broadcasted_iota(jnp.int32, x.shape, x.ndim - 1)
    first = (lane % (2 * half)) < half
    return jnp.where(first, pltpu.roll(x, w - half, x.ndim - 1), pltpu.roll(x, half, x.ndim - 1))


def dilated_bias(M):
    delta = jnp.arange(M, dtype=jnp.int32)[:, None] - jnp.arange(M, dtype=jnp.int32)[None, :]
    w = jnp.zeros(delta.shape, F32)
    for window, dil in DIL_PATTERNS:
        ok = (delta >= 0) & (delta <= window)
        if dil > 1:
            ok = ok & ((delta & (dil - 1)) == 0)
        w = w + ok.astype(F32)
    return jnp.where(w > 0, jnp.log(jnp.maximum(w, 1.0)), NEG)


def _dot(a, b, mode):
    return lax.dot_general(a, b, _DOT_DIMS[mode], preferred_element_type=F32)


def attention_fwd(name, qa, qa_off, ka, ka_off, v, v_off, *, da, dv, pairs, scale, M, qb=None, qb_off=0, kb=None, bias=None):
    tq = min(256, M)
    tk = min(1024, M)
    has_b = qb is not None
    has_bias = bias is not None
    dr = MLA_ROPE

    def body(*refs):
        refs = list(refs)
        bias_ref = refs.pop(3) if has_bias else None
        if has_b:
            qa_ref, ka_ref, v_ref, qb_ref, kb_ref, o_ref, lse_ref = refs
        else:
            qa_ref, ka_ref, v_ref, o_ref, lse_ref = refs
        i = pl.program_id(1)
        t0 = i * tq
        nkb = (t0 + tq + tk - 1) // tk
        n_full = nkb if has_bias else t0 // tk
        q1s = [qa_ref[:, hh * da:(hh + 1) * da].astype(BF16) for hh in range(2)]
        q2s = [qb_ref[:, hh * dr:(hh + 1) * dr].astype(BF16) if has_b else None for hh in range(2)]

        def step(kbi, carry, masked):
            ks = pl.multiple_of(kbi * tk, tk)
            k2 = kb_ref[pl.ds(ks, tk), 0:dr].astype(BF16) if has_b else None
            if has_bias:
                extra = bias_ref[:, pl.ds(ks, tk)]
            elif masked:
                delta = (t0 + lax.broadcasted_iota(jnp.int32, (tq, tk), 0)) - (ks + lax.broadcasted_iota(jnp.int32, (tq, tk), 1))
            new = []
            for hh, (m, l, acc) in enumerate(carry):
                k1 = ka_ref[pl.ds(ks, tk), hh * da:(hh + 1) * da].astype(BF16)
                s = _dot(q1s[hh], k1, "nt")
                if has_b:
                    s = s + _dot(q2s[hh], k2, "nt")
                s = s * scale
                if has_bias:
                    s = s + extra
                elif masked:
                    s = jnp.where(delta >= 0, s, NEG)
                m_new = jnp.maximum(m, jnp.max(s, axis=1, keepdims=True))
                alpha = jnp.exp(m - m_new)
                p = jnp.exp(s - m_new)
                l = alpha * l + jnp.sum(p, axis=1, keepdims=True)
                vv = v_ref[pl.ds(ks, tk), hh * dv:(hh + 1) * dv].astype(BF16)
                acc = alpha * acc + _dot(p.astype(BF16), vv, "nn")
                new.append((m_new, l, acc))
            return tuple(new)

        carry = tuple((jnp.full((tq, 1), NEG, F32), jnp.zeros((tq, 1), F32), jnp.zeros((tq, dv), F32)) for _ in range(2))
        carry = lax.fori_loop(0, n_full, functools.partial(step, masked=False), carry)
        carry = lax.fori_loop(n_full, nkb, functools.partial(step, masked=True), carry)
        o_parts = [acc / l for _, l, acc in carry]
        lse_parts = [m + jnp.log(l) for m, l, _ in carry]
        o_ref[...] = jnp.concatenate(o_parts, axis=1)
        lane = lax.broadcasted_iota(jnp.int32, (tq, LANES), 1)
        lse_ref[...] = jnp.where(lane == 0, lse_parts[0], jnp.where(lane == 1, lse_parts[1], 0.0))

    assert qa_off % (2 * da) == 0 and ka_off % (2 * da) == 0 and v_off % (2 * dv) == 0
    in_specs = [
        pl.BlockSpec((tq, 2 * da), lambda hp, i: (i, qa_off // (2 * da) + hp)),
        pl.BlockSpec((M, 2 * da), lambda hp, i: (0, ka_off // (2 * da) + hp)),
        pl.BlockSpec((M, 2 * dv), lambda hp, i: (0, v_off // (2 * dv) + hp)),
    ]
    args = [qa, ka, v]
    if has_bias:
        in_specs.append(pl.BlockSpec((tq, M), lambda hp, i: (i, 0)))
        args.append(bias)
    if has_b:
        assert qb_off % LANES == 0
        in_specs += [pl.BlockSpec((tq, LANES), lambda hp, i: (i, qb_off // LANES + hp)),
                     pl.BlockSpec((M, LANES), lambda hp, i: (0, 0))]
        args += [qb, kb]
    out_specs = [pl.BlockSpec((tq, 2 * dv), lambda hp, i: (i, hp)),
                 pl.BlockSpec((None, tq, LANES), lambda hp, i: (hp, i, 0))]
    out_shape = [jax.ShapeDtypeStruct((M, pairs * 2 * dv), F32), jax.ShapeDtypeStruct((pairs, M, LANES), F32)]
    return pl.pallas_call(
        body, name=name, out_shape=out_shape, grid=(pairs, M // tq), in_specs=in_specs, out_specs=out_specs,
        compiler_params=_cparams(("parallel", "arbitrary")),
    )(*args)


def attention_bwd(name, qa, qa_off, ka, ka_off, v, v_off, o, do, lse, *, da, dv, pairs, scale, M,
                  qb=None, qb_off=0, kb=None, bias=None):
    tq = min(256, M)
    tk = min(512, M)
    has_b = qb is not None
    has_bias = bias is not None
    dr = MLA_ROPE

    def body(*refs):
        refs = list(refs)
        bias_ref = refs.pop(6) if has_bias else None
        if has_b:
            qa_ref, ka_ref, v_ref, o_ref, do_ref, lse_ref, qb_ref, kb_ref, dqa_ref, dka_ref, dv_ref, dqb_ref, dkb_ref = refs
        else:
            qa_ref, ka_ref, v_ref, o_ref, do_ref, lse_ref, dqa_ref, dka_ref, dv_ref = refs
        hp = pl.program_id(0)
        i = pl.program_id(1)
        t0 = i * tq
        nkb = (t0 + tq + tk - 1) // tk
        n_full = nkb if has_bias else t0 // tk

        @pl.when(i == 0)
        def _():
            dka_ref[...] = jnp.zeros(dka_ref.shape, F32)
            dv_ref[...] = jnp.zeros(dv_ref.shape, F32)

        if has_b:
            @pl.when((i == 0) & (hp == 0))
            def _():
                dkb_ref[...] = jnp.zeros(dkb_ref.shape, F32)

        q1s = [qa_ref[:, hh * da:(hh + 1) * da].astype(BF16) for hh in range(2)]
        q2s = [qb_ref[:, hh * dr:(hh + 1) * dr].astype(BF16) if has_b else None for hh in range(2)]
        do_bfs = [do_ref[:, hh * dv:(hh + 1) * dv].astype(BF16) for hh in range(2)]
        rowdots = [jnp.sum(do_ref[:, hh * dv:(hh + 1) * dv] * o_ref[:, hh * dv:(hh + 1) * dv], axis=1, keepdims=True) for hh in range(2)]
        lses = [lse_ref[:, hh:hh + 1] for hh in range(2)]

        def step(kbi, carry, masked):
            ks = pl.multiple_of(kbi * tk, tk)
            k2 = kb_ref[pl.ds(ks, tk), 0:dr].astype(BF16) if has_b else None
            if has_bias:
                extra = bias_ref[:, pl.ds(ks, tk)]
            elif masked:
                delta = (t0 + lax.broadcasted_iota(jnp.int32, (tq, tk), 0)) - (ks + lax.broadcasted_iota(jnp.int32, (tq, tk), 1))
            new, dkb_part = [], None
            for hh, (dq1, dq2) in enumerate(carry):
                k1 = ka_ref[pl.ds(ks, tk), hh * da:(hh + 1) * da].astype(BF16)
                s = _dot(q1s[hh], k1, "nt")
                if has_b:
                    s = s + _dot(q2s[hh], k2, "nt")
                s = s * scale
                if has_bias:
                    s = s + extra
                elif masked:
                    s = jnp.where(delta >= 0, s, NEG)
                p = jnp.exp(s - lses[hh])
                vv = v_ref[pl.ds(ks, tk), hh * dv:(hh + 1) * dv].astype(BF16)
                dp = _dot(do_bfs[hh], vv, "nt")
                ds = (p * (dp - rowdots[hh]) * scale).astype(BF16)
                dq1 = dq1 + _dot(ds, k1, "nn")
                dka_ref[pl.ds(ks, tk), hh * da:(hh + 1) * da] += _dot(ds, q1s[hh], "tn")
                dv_ref[pl.ds(ks, tk), hh * dv:(hh + 1) * dv] += _dot(p.astype(BF16), do_bfs[hh], "tn")
                if has_b:
                    dq2 = dq2 + _dot(ds, k2, "nn")
                    part = _dot(ds, q2s[hh], "tn")
                    dkb_part = part if dkb_part is None else dkb_part + part
                new.append((dq1, dq2))
            if has_b:
                dkb_ref[pl.ds(ks, tk), 0:dr] += dkb_part
            return tuple(new)

        carry = tuple((jnp.zeros((tq, da), F32), jnp.zeros((tq, dr), F32)) for _ in range(2))
        carry = lax.fori_loop(0, n_full, functools.partial(step, masked=False), carry)
        carry = lax.fori_loop(n_full, nkb, functools.partial(step, masked=True), carry)
        dqa_ref[...] = jnp.concatenate([c[0] for c in carry], axis=1).astype(dqa_ref.dtype)
        if has_b:
            dqb_ref[...] = jnp.concatenate([c[1] for c in carry], axis=1).astype(dqb_ref.dtype)

    in_specs = [
        pl.BlockSpec((tq, 2 * da), lambda hp, i: (i, qa_off // (2 * da) + hp)),
        pl.BlockSpec((M, 2 * da), lambda hp, i: (0, ka_off // (2 * da) + hp)),
        pl.BlockSpec((M, 2 * dv), lambda hp, i: (0, v_off // (2 * dv) + hp)),
        pl.BlockSpec((tq, 2 * dv), lambda hp, i: (i, hp)),
        pl.BlockSpec((tq, 2 * dv), lambda hp, i: (i, hp)),
        pl.BlockSpec((None, tq, LANES), lambda hp, i: (hp, i, 0)),
    ]
    args = [qa, ka, v, o, do, lse]
    if has_bias:
        in_specs.append(pl.BlockSpec((tq, M), lambda hp, i: (i, 0)))
        args.append(bias)
    out_specs = [pl.BlockSpec((tq, 2 * da), lambda hp, i: (i, hp)),
                 pl.BlockSpec((M, 2 * da), lambda hp, i: (0, hp)),
                 pl.BlockSpec((M, 2 * dv), lambda hp, i: (0, hp))]
    out_shape = [jax.ShapeDtypeStruct((M, pairs * 2 * da), BF16),
                 jax.ShapeDtypeStruct((M, pairs * 2 * da), F32),
                 jax.ShapeDtypeStruct((M, pairs * 2 * dv), F32)]
    if has_b:
        in_specs += [pl.BlockSpec((tq, LANES), lambda hp, i: (i, qb_off // LANES + hp)),
                     pl.BlockSpec((M, LANES), lambda hp, i: (0, 0))]
        args += [qb, kb]
        out_specs += [pl.BlockSpec((tq, LANES), lambda hp, i: (i, hp)), pl.BlockSpec((M, LANES), lambda hp, i: (0, 0))]
        out_shape += [jax.ShapeDtypeStruct((M, pairs * LANES), F32), jax.ShapeDtypeStruct((M, LANES), F32)]
    return pl.pallas_call(
        body, name=name, out_shape=out_shape, grid=(pairs, M // tq), in_specs=in_specs, out_specs=out_specs,
        compiler_params=_cparams(("arbitrary", "arbitrary")),
    )(*args)


def ssm_scan(name, xcat, acat, *, M, reverse=False, hcat=None):
    C2 = xcat.shape[1]
    cb = LANES
    tb = min(128, M)
    nblk = M // tb
    with_da = hcat is not None
    nsub = 1
    wide = nsub * 2 * cb

    def body(*refs):
        if with_da:
            x_ref, a_ref, h_ref, o_ref, da_ref, p_ref = refs
        else:
            x_ref, a_ref, o_ref, p_ref = refs
        re = [slice(s * 2 * cb, s * 2 * cb + cb) for s in range(nsub)]
        im = [slice(s * 2 * cb + cb, (s + 1) * 2 * cb) for s in range(nsub)]
        ars, ais = [a_ref[:, c] for c in re], [a_ref[:, c] for c in im]
        row = lax.broadcasted_iota(jnp.int32, (tb, cb), 0)

        def logscan(xs):
            ps = list(zip(ars, ais))
            d = 1
            while d < tb:
                shift = tb - d if reverse else d
                keep = (row < tb - d) if reverse else (row >= d)
                nxt = []
                for (xr, xi), (pr, pi) in zip(xs, ps):
                    sr = jnp.where(keep, pltpu.roll(xr, shift, 0), 0.0)
                    si = jnp.where(keep, pltpu.roll(xi, shift, 0), 0.0)
                    nxt.append((xr + pr * sr - pi * si, xi + pr * si + pi * sr))
                xs = nxt
                ps = [(pr * pr - pi * pi, 2.0 * pr * pi) for pr, pi in ps]
                d *= 2
            return xs

        seed = row == (tb - 1 if reverse else 0)
        for s, (p0r, p0i) in enumerate(logscan([(jnp.where(seed, ar, 0.0), jnp.where(seed, ai, 0.0)) for ar, ai in zip(ars, ais)])):
            p_ref[:, re[s]] = p0r
            p_ref[:, im[s]] = p0i
        sub = lax.broadcasted_iota(jnp.int32, (8, cb), 0)
        edge = 0 if reverse else tb - 8
        pick = sub == (0 if reverse else 7)

        def blk(b, carry):
            bb = (nblk - 1 - b) if reverse else b
            t0 = pl.multiple_of(bb * tb, tb)
            te = pl.multiple_of(t0 + edge, 8)
            hs = logscan([(x_ref[pl.ds(t0, tb), re[s]], x_ref[pl.ds(t0, tb), im[s]]) for s in range(nsub)])
            new = []
            for s, ((hr, hi), (cr, ci)) in enumerate(zip(hs, carry)):
                pr, pi = p_ref[:, re[s]], p_ref[:, im[s]]
                o_ref[pl.ds(t0, tb), re[s]] = hr + pr * cr - pi * ci
                o_ref[pl.ds(t0, tb), im[s]] = hi + pr * ci + pi * cr
                new.append((jnp.sum(jnp.where(pick, o_ref[pl.ds(te, 8), re[s]], 0.0), axis=0, keepdims=True),
                            jnp.sum(jnp.where(pick, o_ref[pl.ds(te, 8), im[s]], 0.0), axis=0, keepdims=True)))
            return tuple(new)

        lax.fori_loop(0, nblk, blk, tuple((jnp.zeros((1, cb), F32), jnp.zeros((1, cb), F32)) for _ in range(nsub)))
        if with_da:
            first = lax.broadcasted_iota(jnp.int32, (M, cb), 0) >= 1
            for s in range(nsub):
                hpr = jnp.where(first, pltpu.roll(h_ref[:, re[s]], 1, 0), 0.0)
                hpi = jnp.where(first, pltpu.roll(h_ref[:, im[s]], 1, 0), 0.0)
                lr, li = o_ref[:, re[s]], o_ref[:, im[s]]
                da_ref[:, re[s]] = _colsum(lr * hpr + li * hpi)
                da_ref[:, im[s]] = _colsum(li * hpr - lr * hpi)

    blk_spec = pl.BlockSpec((M, wide), lambda j: (0, j))
    vec_spec = pl.BlockSpec((1, wide), lambda j: (0, j))
    in_specs = [blk_spec, vec_spec] + ([blk_spec] if with_da else [])
    out_specs = [blk_spec] + ([vec_spec] if with_da else [])
    out_shape = [jax.ShapeDtypeStruct((M, C2), F32)] + ([jax.ShapeDtypeStruct((1, C2), F32)] if with_da else [])
    args = [xcat, acat] + ([hcat] if with_da else [])
    res = pl.pallas_call(
        body, name=name, out_shape=out_shape, grid=(C2 // wide,), in_specs=in_specs, out_specs=out_specs,
        scratch_shapes=[pltpu.VMEM((tb, wide), F32)], compiler_params=_cparams(("parallel",)),
    )(*args)
    return res if with_da else res[0]


def ssm_diag_rows(name, m):
    assert LANES == 2 * SSM_STATE
    gp, c2 = m.shape
    groups = gp // SSM_GROUP
    st = SSM_STATE

    def body(m_ref, re_ref, im_ref):
        x = m_ref[...]
        odd = pl.program_id(0) % 2

        @pl.when(odd == 0)
        def _():
            re_ref[:, 0:st] = x[:, 0:st]
            im_ref[:, 0:st] = x[:, LANES:LANES + st]

        @pl.when(odd == 1)
        def _():
            re_ref[:, st:LANES] = x[:, st:LANES]
            im_ref[:, st:LANES] = x[:, LANES + st:2 * LANES]

    out = jax.ShapeDtypeStruct((SSM_GROUP, c2 // 2), F32)
    o_spec = pl.BlockSpec((SSM_GROUP, LANES), lambda g: (0, g // 2))
    return pl.pallas_call(
        body, name=name, out_shape=[out, out], grid=(groups,), in_specs=[pl.BlockSpec((SSM_GROUP, 2 * LANES), lambda g: (g, g // 2))],
        out_specs=[o_spec, o_spec], compiler_params=_cparams(("arbitrary",)),
    )(m)


def _ssm_param_fn(a_re, a_im, ldt, b_re, b_im):
    lr, li = jnp.minimum(a_re, -1e-4), a_im
    dt = jnp.exp(ldt)
    e, ang = jnp.exp(lr * dt), li * dt
    ar, ai = e * jnp.cos(ang), e * jnp.sin(ang)
    den = lr * lr + li * li
    nr, ni = ar - 1.0, ai
    cr, ci = (nr * lr + ni * li) / den, (ni * lr - nr * li) / den
    return ar, ai, cr * b_re - ci * b_im, cr * b_im + ci * b_re


def _whole(shape):
    return pl.BlockSpec(shape, functools.partial(lambda nd: (0,) * nd, nd=len(shape)))


def ssm_param_fwd(name, a_re, a_im, ldt, b_re, b_im):
    def body(*refs):
        res = _ssm_param_fn(*[r[...] for r in refs[:5]])
        for ref, val in zip(refs[5:], res):
            ref[...] = val

    ins = [a_re, a_im, ldt, b_re, b_im]
    outs = [a_re, a_re, b_re, b_re]
    return pl.pallas_call(
        body, name=name, out_shape=[jax.ShapeDtypeStruct(t.shape, F32) for t in outs],
        in_specs=[_whole(t.shape) for t in ins], out_specs=[_whole(t.shape) for t in outs], compiler_params=_cparams(),
    )(*ins)


def ssm_param_bwd(name, a_re, a_im, ldt, b_re, b_im, d_ar, d_ai, d_bbr, d_bbi):
    def body(*refs):
        _, vjp = jax.vjp(_ssm_param_fn, *[r[...] for r in refs[:5]])
        res = vjp(tuple(r[...] for r in refs[5:9]))
        for ref, val in zip(refs[9:], res):
            ref[...] = val

    ins = [a_re, a_im, ldt, b_re, b_im, d_ar, d_ai, d_bbr, d_bbi]
    outs = [a_re, a_im, ldt, b_re, b_im]
    return pl.pallas_call(
        body, name=name, out_shape=[jax.ShapeDtypeStruct(t.shape, F32) for t in outs],
        in_specs=[_whole(t.shape) for t in ins], out_specs=[_whole(t.shape) for t in outs], compiler_params=_cparams(),
    )(*ins)


ANY = pl.BlockSpec(memory_space=pl.ANY)


def _place():
    x, y, c = lax.axis_index("x"), lax.axis_index("y"), lax.axis_index("c")
    chips = [(1 - x, y), (x, 1 - y), (1 - x, 1 - y)]
    return x, y, c, chips


def cast_into_slot(name, w, kind, l=None):
    K, nn = w.shape[-2:]
    hr, hc = (K // 2, nn) if kind == "cols" else (K, nn // 2)
    tr = _pick(hr, max(16, min(512, ROW_BUDGET // (2 * hc * 6))), SUBLANES_BF16)
    nb = hr // tr

    def body(w_ref, o_ref):
        o_ref[...] = w_ref[...].astype(BF16)

    lead = () if l is None else (l,)
    if kind == "cols":
        in_spec = pl.BlockSpec((None,) * len(lead) + (tr, hc), lambda h, i: lead + (h * nb + i, 0))
    else:
        in_spec = pl.BlockSpec((None,) * len(lead) + (tr, hc), lambda h, i: lead + (i, h))
    return pl.pallas_call(
        body, name=name, out_shape=jax.ShapeDtypeStruct((N_CHIPS, 2, hr, hc), BF16), grid=(2, nb), in_specs=[in_spec],
        out_specs=pl.BlockSpec((None, None, tr, hc), lambda h, i: (2 * lax.axis_index("x") + lax.axis_index("y"), h, i, 0)),
        compiler_params=_cparams(("parallel", "parallel")),
    )(w)


HBM_SPEC = pl.BlockSpec(memory_space=pltpu.HBM)
SEM_SPEC = pl.BlockSpec(memory_space=pltpu.SEMAPHORE)
SPLIT_PARAMS = pltpu.CompilerParams(has_side_effects=pltpu.SideEffectType.DATAFLOW_SIDE_EFFECTING)


def _in_hbm(t):
    return pltpu.with_memory_space_constraint(t, pltpu.HBM)


def split_start(name, plan, n, bufs, fresh, carrier):
    nb, nf = len(bufs), len(fresh)

    def body(*refs):
        outs = refs[nb + 1:]
        for i, (s, d, dev) in enumerate(plan(list(outs[2:2 + nb + nf]))):
            pltpu.make_async_remote_copy(src_ref=s, dst_ref=d, send_sem=outs[0].at[i], recv_sem=outs[1].at[i],
                                         device_id=dev, device_id_type=MESH).start()

    hbm = lambda t: pltpu.HBM(t.shape, t.dtype)
    res = pl.pallas_call(
        body, name=name,
        out_shape=(pltpu.SemaphoreType.DMA((n,)), pltpu.SemaphoreType.DMA((n,)), *[hbm(t) for t in bufs], *[hbm(t) for t in fresh], hbm(carrier)),
        in_specs=[HBM_SPEC] * (nb + 1), out_specs=(SEM_SPEC, SEM_SPEC) + (HBM_SPEC,) * (nb + nf + 1),
        input_output_aliases={**{i: 2 + i for i in range(nb)}, nb: 2 + nb + nf}, compiler_params=SPLIT_PARAMS,
    )(*[_in_hbm(t) for t in bufs], _in_hbm(carrier))
    return (res[0], res[1]), list(res[2:2 + nb]), list(res[2 + nb:2 + nb + nf]), res[2 + nb + nf]


def split_wait(name, plan, sems, bufs, carrier):
    nb = len(bufs)

    def body(*refs):
        for i, (s, d, dev) in enumerate(plan(list(refs[:nb]))):
            cp = pltpu.make_async_remote_copy(src_ref=s, dst_ref=d, send_sem=refs[nb].at[i], recv_sem=refs[nb + 1].at[i],
                                              device_id=dev, device_id_type=MESH)
            cp.wait_send()
            cp.wait_recv()

    hbm = lambda t: pltpu.HBM(t.shape, t.dtype)
    res = pl.pallas_call(
        body, name=name, out_shape=(*[hbm(t) for t in bufs], hbm(carrier)),
        in_specs=[HBM_SPEC] * nb + [SEM_SPEC, SEM_SPEC, HBM_SPEC], out_specs=(HBM_SPEC,) * (nb + 1),
        input_output_aliases={**{i: i for i in range(nb)}, nb + 2: nb}, compiler_params=SPLIT_PARAMS,
    )(*bufs, sems[0], sems[1], carrier)
    return list(res[:nb]), res[nb]


def _me_sib_chips():
    x, y, c, chips = _place()
    return 2 * x + y, c, (x, y, 1 - c), chips


def plan_gather_ici(refs):
    me, c, _, chips = _me_sib_chips()
    return [(r.at[me, c], r.at[me, c], (chip[0], chip[1], c)) for r in refs for chip in chips]


def plan_gather_pass(refs):
    _, c, sib, chips = _me_sib_chips()
    return [(r.at[2 * chip[0] + chip[1], c], r.at[2 * chip[0] + chip[1], c], sib) for r in refs for chip in chips]


def plan_pair(n_arrays):
    def plan(refs):
        _, c, sib, _ = _me_sib_chips()
        return [(refs[a].at[1 - c], refs[n_arrays + a], sib) for a in range(n_arrays)]
    return plan


def plan_chips(n_arrays):
    def plan(refs):
        _, c, _, chips = _me_sib_chips()
        return [(refs[a].at[2 * chip[0] + chip[1]], refs[n_arrays + a].at[k], (chip[0], chip[1], c))
                for a in range(n_arrays) for k, chip in enumerate(chips)]
    return plan


def plan_share(n_arrays):
    def plan(refs):
        _, _, sib, _ = _me_sib_chips()
        return [(refs[a], refs[n_arrays + a], sib) for a in range(n_arrays)]
    return plan


def swap_with_sibling(name, src, pick_other_half):
    shape = src.shape[1:] if pick_other_half else src.shape

    def body(src_ref, out_ref, ssem, rsem):
        x, y, c, _ = _place()
        cp = pltpu.make_async_remote_copy(src_ref=src_ref.at[1 - c] if pick_other_half else src_ref, dst_ref=out_ref,
                                          send_sem=ssem, recv_sem=rsem, device_id=(x, y, 1 - c), device_id_type=MESH)
        cp.start()
        cp.wait()

    return pl.pallas_call(
        body, name=name, out_shape=jax.ShapeDtypeStruct(shape, src.dtype), in_specs=[ANY], out_specs=ANY,
        scratch_shapes=[pltpu.SemaphoreType.DMA(()), pltpu.SemaphoreType.DMA(())],
    )(src)


def exchange_chips(name, src, per_chip):
    shape = src.shape[1:] if per_chip else src.shape

    def body(src_ref, out_ref, send_sems, recv_sems):
        x, y, c, chips = _place()
        cps = []
        for k, chip in enumerate(chips):
            s = src_ref.at[2 * chip[0] + chip[1]] if per_chip else src_ref
            cps.append(pltpu.make_async_remote_copy(src_ref=s, dst_ref=out_ref.at[k], send_sem=send_sems.at[k], recv_sem=recv_sems.at[k],
                                                    device_id=(chip[0], chip[1], c), device_id_type=MESH))
        for cp in cps:
            cp.start()
        for cp in cps:
            cp.wait()

    return pl.pallas_call(
        body, name=name, out_shape=jax.ShapeDtypeStruct((3,) + shape, src.dtype), in_specs=[ANY], out_specs=ANY,
        scratch_shapes=[pltpu.SemaphoreType.DMA((3,)), pltpu.SemaphoreType.DMA((3,))],
    )(src)


def pair_sum(name, p, got):
    _, _, rh, cw = p.shape
    tr = _pick(rh, max(16, min(512, ROW_BUDGET // (2 * cw * 10))), SUBLANES_BF16)

    def body(p_ref, got_ref, s_ref, own_ref):
        j = pl.program_id(1)
        tot = p_ref[...].astype(F32) + got_ref[...].astype(F32)
        s_ref[...] = tot.astype(BF16)

        @pl.when(j == 2 * lax.axis_index("x") + lax.axis_index("y"))
        def _():
            own_ref[...] = tot

    return pl.pallas_call(
        body, name=name, grid=(rh // tr, N_CHIPS),
        in_specs=[pl.BlockSpec((None, None, tr, cw), lambda i, j: (lax.axis_index("c"), j, i, 0)),
                  pl.BlockSpec((None, tr, cw), lambda i, j: (j, i, 0))],
        out_specs=[pl.BlockSpec((None, tr, cw), lambda i, j: (j, i, 0)),
                   pl.BlockSpec((tr, cw), lambda i, j: (i, 0))],
        out_shape=[jax.ShapeDtypeStruct((N_CHIPS, rh, cw), BF16), jax.ShapeDtypeStruct((rh, cw), F32)],
        compiler_params=_cparams(("arbitrary", "arbitrary")),
    )(p, got)


def chips_sum(name, own, parts):
    rh, cw = own.shape
    parts = parts.reshape(3 * rh, cw)
    return rowwise(name, lambda o, a, b, c: (((o + a.astype(F32)) + b.astype(F32)) + c.astype(F32),),
                   [(own, cw, 0), (parts, cw, 0, 0), (parts, cw, 0, rh), (parts, cw, 0, 2 * rh)], [], [(cw, F32)], M=rh)[0]


def all_reduce_small(buf):
    r = buf.shape[0]
    got = swap_with_sibling("ar_pair", buf, False)
    chip = rowwise("ar_pairsum", lambda a, b: (a + b,), [(buf, LANES, 0), (got, LANES, 0)], [], [(LANES, F32)], M=r)[0]
    parts = exchange_chips("ar_chips", chip, False).reshape(3 * r, LANES)
    return rowwise("ar_sum", lambda o, fx, fy, fxy: ((o + fy) + (fx + fxy),),
                   [(chip, LANES, 0), (parts, LANES, 0, 0), (parts, LANES, 0, r), (parts, LANES, 0, 2 * r)], [], [(LANES, F32)], M=r)[0]


def _adam_fn(w, g, m, v):
    m = ADAM_B1 * m + (1.0 - ADAM_B1) * g
    v = ADAM_B2 * v + (1.0 - ADAM_B2) * (g * g)
    m_hat = m / (1.0 - ADAM_B1 ** ADAM_STEP)
    v_hat = v / (1.0 - ADAM_B2 ** ADAM_STEP)
    return -ADAM_LR * (m_hat / (jnp.sqrt(v_hat) + ADAM_EPS) + ADAM_WD * w), m, v


def adamw(name, w, g, m, v):
    r, cw = w.shape
    return rowwise(name, _adam_fn, [(t, cw, 0) for t in (w, g, m, v)], [], [(cw, F32)] * 3, M=r)


def adamw_layers(name, w, m, v, mines, theirs, kind):
    L, K, nn = w.shape
    hr, hc = (K // 2, nn) if kind == "cols" else (K, nn // 2)
    tr = _pick(hr, max(8, min(256, MM_BUDGET // (2 * hc * 4 * (7 + 2 * L)))), 8)
    nb = hr // tr

    def body(*refs):
        w_ref, m_ref, v_ref = refs[:3]
        outs = refs[3 + 2 * L:]
        l, mine_here = pl.program_id(0), pl.program_id(1) == lax.axis_index("c")
        g = jnp.zeros((tr, hc), F32)
        for ll in range(L):
            g = jnp.where(l == ll, jnp.where(mine_here, refs[3 + ll][...], refs[3 + L + ll][...]), g)
        outs[0][...] = g
        outs[1][...], outs[2][...], outs[3][...] = _adam_fn(w_ref[...], g, m_ref[...], v_ref[...])

    if kind == "cols":
        full = pl.BlockSpec((None, tr, hc), lambda l, h, i: (l, h * nb + i, 0))
    else:
        full = pl.BlockSpec((None, tr, hc), lambda l, h, i: (l, i, h))
    def half_spec(ll, mine):
        def imap(l, h, i):
            here = (l == ll) & ((h == lax.axis_index("c")) == mine)
            return (jnp.where(here, i, 0), 0)
        return pl.BlockSpec((tr, hc), imap)

    halves = [half_spec(ll, True) for ll in range(L)] + [half_spec(ll, False) for ll in range(L)]
    return pl.pallas_call(
        body, name=name, grid=(L, 2, nb), in_specs=[full] * 3 + halves, out_specs=[full] * 4,
        out_shape=[jax.ShapeDtypeStruct((L, K, nn), F32)] * 4, compiler_params=_cparams(("parallel", "parallel", "parallel")),
    )(w, m, v, *mines, *theirs)


class Dims:
    def __init__(self, x, g_q, g_kv, g_out_mla, g_out_ssm, g_out_dil, ff):
        self.M, self.D = x.shape[-2], x.shape[-1]
        self.QL, self.KVL = g_q.shape[-1], g_kv.shape[-1]
        self.MW, self.SW, self.DW = g_out_mla.shape[-1], g_out_ssm.shape[-1], g_out_dil.shape[-1]
        self.H = self.MW // MLA_V
        self.FF = ff
        self.G = self.SW // SSM_GROUP
        self.C = self.G * SSM_STATE
        self.o_cq, self.o_u = 0, self.QL
        self.o_qd = self.o_u + self.SW
        self.o_kd = self.o_qd + self.DW
        self.o_vd = self.o_kd + self.DW
        self.o_ckv = self.o_vd + self.DW
        self.o_kr = self.o_ckv + self.KVL
        self.PW = -(-(self.o_kr + MLA_ROPE) // (4 * LANES)) * (4 * LANES)
        assert self.o_u % self.SW == 0 and self.o_qd % LANES == 0 and self.o_ckv % self.KVL == 0 and self.o_kr % LANES == 0
        assert self.H % 2 == 0 and self.DW % LANES == 0 and self.C % LANES == 0
        self.QW = self.H * (MLA_NOPE + MLA_ROPE)
        self.KVW = self.H * (MLA_NOPE + MLA_V)
        sizes = [self.QL, self.KVL, MLA_ROPE, self.SW, self.DW, self.DW, self.DW]
        starts = np.concatenate([[0], np.cumsum(sizes)[:-1]])
        self.ref_cols = {n: (int(s), int(z)) for n, s, z in zip(["cq", "ckv", "kr", "u", "qd", "kd", "vd"], starts, sizes)}
        self.INW = int(sum(sizes))
        self.new_order = ["cq", "u", "qd", "kd", "vd", "ckv", "kr"]
        src = np.concatenate([np.arange(self.ref_cols[n][0], self.ref_cols[n][0] + self.ref_cols[n][1]) for n in self.new_order])
        self.src_in = np.concatenate([src, -np.ones(self.PW - self.INW, np.int64)])
        self.src_q = self._heads_split(self.H, MLA_NOPE, MLA_ROPE)
        self.src_kv = self._heads_split(self.H, MLA_NOPE, MLA_V)

    @staticmethod
    def _heads_split(h, d1, d2):
        first = (np.arange(h)[:, None] * (d1 + d2) + np.arange(d1)[None, :]).reshape(-1)
        second = (np.arange(h)[:, None] * (d1 + d2) + d1 + np.arange(d2)[None, :]).reshape(-1)
        return np.concatenate([first, second])


def _regroup_in(dm, w):
    parts = [w[..., dm.ref_cols[n][0]:dm.ref_cols[n][0] + dm.ref_cols[n][1]] for n in dm.new_order]
    pad = dm.PW - dm.INW
    return jnp.concatenate(parts + [jnp.zeros(w.shape[:-1] + (pad,), w.dtype)], axis=-1)


def _ungroup_in(dm, w):
    off, pieces = 0, {}
    for n in dm.new_order:
        pieces[n] = w[..., off:off + dm.ref_cols[n][1]]
        off += dm.ref_cols[n][1]
    return jnp.concatenate([pieces[n] for n in ["cq", "ckv", "kr", "u", "qd", "kd", "vd"]], axis=-1)


def _split_heads(w, h, d1):
    t = w.reshape(w.shape[:-1] + (h, -1))
    return jnp.concatenate([t[..., :d1].reshape(w.shape[:-1] + (-1,)), t[..., d1:].reshape(w.shape[:-1] + (-1,))], axis=-1)


def _merge_heads(w, h, d1):
    a = w[..., :h * d1].reshape(w.shape[:-1] + (h, d1))
    b = w[..., h * d1:].reshape(w.shape[:-1] + (h, -1))
    return jnp.concatenate([a, b], axis=-1).reshape(w.shape[:-1] + (-1,))


def _cat_cols(re, im):
    r, c = re.shape
    return jnp.stack([re.reshape(r, c // LANES, LANES), im.reshape(r, c // LANES, LANES)], axis=2).reshape(r, 2 * c)


def _uncat_cols(cat):
    r, c2 = cat.shape
    t = cat.reshape(r, c2 // (2 * LANES), 2, LANES)
    return t[:, :, 0].reshape(r, c2 // 2), t[:, :, 1].reshape(r, c2 // 2)


def _block_diag(t, g):
    _, a, b = t.shape
    eye = jnp.eye(g, dtype=bool)[:, None, :, None]
    return jnp.where(eye, t[:, :, None, :], 0).reshape(g * a, g * b)


def _diag_blocks(m, g):
    a, b = m.shape[0] // g, m.shape[1] // g
    eye = jnp.eye(g, dtype=m.dtype)[:, None, :, None]
    return jnp.sum(m.reshape(g, a, g, b) * eye, axis=2)


def _rope_tables(dm):
    half = MLA_ROPE // 2
    inv_freq = ROPE_THETA ** (-jnp.arange(half, dtype=F32) / half)
    ang = jnp.arange(dm.M, dtype=F32)[:, None] * inv_freq[None, :]
    cos = jnp.concatenate([jnp.cos(ang), jnp.cos(ang)], axis=1)
    sin = jnp.concatenate([-jnp.sin(ang), jnp.sin(ang)], axis=1)
    return jnp.tile(cos, (1, dm.H)), jnp.tile(sin, (1, dm.H)), jnp.tile(cos, (1, LANES // MLA_ROPE)), jnp.tile(sin, (1, LANES // MLA_ROPE))


def _rope(x, cos, sin):
    return x * cos + _swap_halves(x, MLA_ROPE // 2) * sin


def _rope_t(d, cos, sin):
    return d * cos + _swap_halves(d * sin, MLA_ROPE // 2)


def _ssm_layer_params(dm, a_re, a_im, log_dt, b_re, b_im):
    flat = lambda t: t.reshape(1, dm.C)
    ldt = jnp.repeat(log_dt, SSM_STATE).reshape(1, dm.C)
    bt = lambda t: jnp.transpose(t, (2, 0, 1)).reshape(SSM_GROUP, dm.C)
    return flat(a_re), flat(a_im), ldt, bt(b_re), bt(b_im)


def layer_forward(dm, l, x, lw, sp, tabs, hook_q, hook_mid):
    M, D = dm.M, dm.D
    n = lambda s: f"{s}_l{l}"
    sv = {"x_in": x}
    h1 = rms_fwd(n("rms_mix"), x, D, 0, lw["g_mix"], M=M)
    proj = matmul(n("in_proj"), h1, lw["w_in"], "nn", M=M, N=dm.PW, K=D)
    sv.update(h1=h1, proj=proj)
    cqn = rms_fwd(n("rms_q"), proj, dm.QL, dm.o_cq, lw["g_q"], M=M)
    q = matmul(n("q_up"), cqn, lw["w_uq"], "nn", M=M, N=dm.QW, K=dm.QL)
    ckvn = rms_fwd(n("rms_kv"), proj, dm.KVL, dm.o_ckv, lw["g_kv"], M=M)
    kv = matmul(n("kv_up"), ckvn, lw["w_ukv"], "nn", M=M, N=dm.KVW, K=dm.KVL, out_dtype=BF16)
    cosq, sinq, cosk, sink = tabs[:4]
    nw = dm.H * MLA_NOPE

    def rope_fn(qb, kb, cq, sq, ck, sk):
        return jnp.concatenate([qb[:, :nw], _rope(qb[:, nw:], cq, sq)], axis=1), _rope(kb, ck, sk)

    pw = dm.H * MLA_ROPE
    q_bf, kpe = rowwise(n("rope"), rope_fn, [(q, dm.QW, 0), (proj, LANES, dm.o_kr), (cosq, pw, 0), (sinq, pw, 0), (cosk, LANES, 0), (sink, LANES, 0)],
                        [], [(dm.QW, BF16), (LANES, BF16)], M=M)
    mla_scale = (MLA_NOPE + MLA_ROPE) ** -0.5
    o_mla, lse_mla = attention_fwd(n("mla_fwd"), q_bf, 0, kv, 0, kv, nw, da=MLA_NOPE, dv=MLA_V, pairs=dm.H // 2, scale=mla_scale,
                                   M=M, qb=q_bf, qb_off=nw, kb=kpe)
    sv.update(cqn=cqn, ckvn=ckvn, q_bf=q_bf, kv=kv, kpe=kpe, o_mla=o_mla, lse_mla=lse_mla)
    bu = matmul(n("ssm_bu"), proj, sp["bcat"], "nn", M=M, N=2 * dm.C, K=dm.SW, a_off=(0, dm.o_u))
    hcat = ssm_scan(n("ssm_scan"), bu, sp["acat"], M=M)
    ylin = matmul(n("ssm_y"), hcat, sp["ccat"], "nn", M=M, N=dm.SW, K=2 * dm.C)
    yg = rowwise(n("ssm_gelu"), lambda y, u, d: (_gelu(y + d * u),), [(ylin, dm.SW, 0), (proj, dm.SW, dm.o_u)], [lw["d_skip"]],
                 [(dm.SW, BF16)], M=M)[0]
    z = matmul(n("ssm_glu"), yg, lw["w_glu"], "nn", w=("cols", 2 * dm.SW // N_CHIPS), M=M, N=2 * dm.SW, K=dm.SW)
    sw = dm.SW

    def glu_fn(zb, b):
        zz = zb + b
        return (zz[:, :sw] * jax.nn.sigmoid(zz[:, sw:]),)

    o_ssm = hook_q(rowwise(n("ssm_gate"), glu_fn, [(z, 2 * sw, 0)], [lw["b_glu"]], [(sw, F32)], M=M)[0])
    sv.update(hcat=hcat, ylin=ylin, yg=yg, z=z, o_ssm=o_ssm)
    o_dil, lse_dil = attention_fwd(n("dil_fwd"), proj, dm.o_qd, proj, dm.o_kd, proj, dm.o_vd, da=DIL_HEAD, dv=DIL_HEAD, pairs=dm.DW // LANES,
                                   scale=DIL_HEAD ** -0.5, M=M, bias=tabs[4])
    sv.update(o_dil=o_dil, lse_dil=lse_dil)
    yn = rowwise(n("out_norm"), lambda a, b, c, ga, gb, gc: (jnp.concatenate([_rms(a, ga), _rms(b, gb), _rms(c, gc)], axis=1),),
                 [(o_mla, dm.MW, 0), (o_ssm, dm.SW, 0), (o_dil, dm.DW, 0)], [lw["g_out_mla"], lw["g_out_ssm"], lw["g_out_dil"]],
                 [(D, BF16)], M=M)[0]
    yn = hook_mid(yn, lw)
    x_mid = matmul(n("out_proj"), yn, lw["w_o"], "nn", w=("rows", D // N_CHIPS), M=M, N=D, K=D, add=x)
    h2 = rms_fwd(n("rms_ffn"), x_mid, D, 0, lw["g_ffn"], M=M)
    ffs = dm.FF // N_CHIPS
    gate = matmul(n("ffn_gate"), h2, lw["w_gate"], "nn", w=("cols", ffs), M=M, N=dm.FF, K=D, out_dtype=BF16)
    up = matmul(n("ffn_up"), h2, lw["w_up"], "nn", w=("cols", ffs), M=M, N=dm.FF, K=D, out_dtype=BF16)

    def act_fn(gb, ub):
        gf = gb.astype(F32)
        return (gf * jax.nn.sigmoid(gf) * ub.astype(F32),)

    act = rowwise(n("ffn_act"), act_fn, [(gate, dm.FF, 0), (up, dm.FF, 0)], [], [(dm.FF, BF16)], M=M)[0]
    x_out = matmul(n("ffn_down"), act, lw["w_down"], "nn", w=("rows",ffs), M=M, N=D, K=dm.FF, add=x_mid)
    sv.update(yn=yn, x_mid=x_mid, h2=h2, gate=gate, up=up, act=act)
    return x_out, sv


def layer_backward(dm, l, dx, lw, sp, tabs, sv, hook_a, hook_m, hook_b):
    M, D = dm.M, dm.D
    n = lambda s: f"{s}_l{l}"
    g = {}
    ffs = dm.FF // N_CHIPS
    dact = matmul(n("ffn_down_dx"), dx, lw["w_down"], "nt", w=("rows", ffs), M=M, N=dm.FF, K=D, out_dtype=BF16)
    g["w_down"] = matmul(n("ffn_down_dw"), sv["act"], dx, "tn", M=dm.FF, N=D, K=M, out_dtype=BF16, into=("rows", ffs))

    def act_bwd(gb, ub, db):
        _, vjp = jax.vjp(lambda a, b: a * jax.nn.sigmoid(a) * b, gb.astype(F32), ub.astype(F32))
        return vjp(db.astype(F32))

    dgate, dup = rowwise(n("ffn_act_bwd"), act_bwd, [(sv["gate"], dm.FF, 0), (sv["up"], dm.FF, 0), (dact, dm.FF, 0)], [],
                         [(dm.FF, BF16), (dm.FF, BF16)], M=M)
    dh2 = matmul(n("ffn_gate_dx"), dgate, lw["w_gate"], "nt", w=("cols",ffs), M=M, N=D, K=dm.FF)
    dh2 = matmul(n("ffn_up_dx"), dup, lw["w_up"], "nt", w=("cols",ffs), M=M, N=D, K=dm.FF, add=dh2)
    g["w_gate"] = matmul(n("ffn_gate_dw"), sv["h2"], dgate, "tn", M=D, N=dm.FF, K=M, out_dtype=BF16, into=("cols", ffs))
    g["w_up"] = matmul(n("ffn_up_dw"), sv["h2"], dup, "tn", M=D, N=dm.FF, K=M, out_dtype=BF16, into=("cols", ffs))
    dx_mid, g["g_ffn"] = rms_bwd(n("rms_ffn_bwd"), sv["x_mid"], D, 0, lw["g_ffn"], dh2, dx, M=M)
    dx_mid = hook_a(dx_mid, g)
    dyn = matmul(n("out_proj_dx"), dx_mid, lw["w_o"], "nt", w=("rows", D // N_CHIPS), M=M, N=D, K=D)
    g["w_o"] = matmul(n("out_proj_dw"), sv["yn"], dx_mid, "tn", M=D, N=D, K=M, out_dtype=BF16, into=("rows", D // N_CHIPS))
    mw, sw, dw = dm.MW, dm.SW, dm.DW

    def out_norm_bwd(a, b, c, dy, ga, gb, gc):
        res, sums = [], []
        for t, gg, lo, hi in ((a, ga, 0, mw), (b, gb, mw, mw + sw), (c, gc, mw + sw, mw + sw + dw)):
            _, vjp = jax.vjp(_rms, t, gg)
            dt, dg = vjp(dy[:, lo:hi])
            res.append(dt)
            sums.append(dg)
        return res + sums

    do_mla, do_ssm, do_dil, g["g_out_mla"], g["g_out_ssm"], g["g_out_dil"] = rowwise(
        n("out_norm_bwd"), out_norm_bwd, [(sv["o_mla"], mw, 0), (sv["o_ssm"], sw, 0), (sv["o_dil"], dw, 0), (dyn, D, 0)],
        [lw["g_out_mla"], lw["g_out_ssm"], lw["g_out_dil"]], [(mw, F32), (sw, F32), (dw, F32)], [mw, sw, dw], M=M)
    proj = sv["proj"]
    dqd, dkd, dvd = attention_bwd(n("dil_bwd"), proj, dm.o_qd, proj, dm.o_kd, proj, dm.o_vd, sv["o_dil"], do_dil, sv["lse_dil"],
                                  da=DIL_HEAD, dv=DIL_HEAD, pairs=dw // LANES, scale=DIL_HEAD ** -0.5, M=M, bias=tabs[4])
    do_ssm = hook_m(do_ssm, g)
    def glu_bwd(zb, db, b):
        _, vjp = jax.vjp(lambda zz, bb: (zz + bb)[:, :sw] * jax.nn.sigmoid((zz + bb)[:, sw:]), zb, b)
        return vjp(db)

    dz, g["b_glu"] = rowwise(n("ssm_gate_bwd"), glu_bwd, [(sv["z"], 2 * sw, 0), (do_ssm, sw, 0)], [lw["b_glu"]], [(2 * sw, BF16)], [2 * sw], M=M)
    dyg = matmul(n("ssm_glu_dx"), dz, lw["w_glu"], "nt", w=("cols", 2 * sw // N_CHIPS), M=M, N=sw, K=2 * sw)
    g["w_glu"] = matmul(n("ssm_glu_dw"), sv["yg"], dz, "tn", M=sw, N=2 * sw, K=M, out_dtype=BF16, into=("cols", 2 * sw // N_CHIPS))

    def gelu_bwd(y, u, dy, d):
        _, vjp = jax.vjp(lambda yy, uu, dd: _gelu(yy + dd * uu), y, u, d)
        return vjp(dy)

    dylin, du1, g["d_skip"] = rowwise(n("ssm_gelu_bwd"), gelu_bwd, [(sv["ylin"], sw, 0), (proj, sw, dm.o_u), (dyg, sw, 0)], [lw["d_skip"]],
                                      [(sw, BF16), (sw, F32)], [sw], M=M)
    seed = matmul(n("ssm_y_dx"), dylin, sp["ccat"], "nt", M=M, N=2 * dm.C, K=sw)
    d_ccat = matmul(n("ssm_y_dw"), dylin, sv["hcat"], "tn", M=sw, N=2 * dm.C, K=M)
    lam, d_acat = ssm_scan(n("ssm_scan_bwd"), seed, sp["acat_conj"], M=M, reverse=True, hcat=sv["hcat"])
    du = matmul(n("ssm_bu_dx"), lam, sp["bcat"], "nt", M=M, N=sw, K=2 * dm.C, add=du1, out_dtype=BF16)
    d_bcat = matmul(n("ssm_bu_dw"), proj, lam, "tn", M=sw, N=2 * dm.C, K=M, a_off=(0, dm.o_u))
    g["ssm_raw"] = (d_acat, d_bcat, d_ccat)
    nw = dm.H * MLA_NOPE
    dqn, dkn, dv_, dqp, dkp = attention_bwd(n("mla_bwd"), sv["q_bf"], 0, sv["kv"], 0, sv["kv"], nw, sv["o_mla"], do_mla, sv["lse_mla"],
                                            da=MLA_NOPE, dv=MLA_V, pairs=dm.H // 2, scale=(MLA_NOPE + MLA_ROPE) ** -0.5, M=M,
                                            qb=sv["q_bf"], qb_off=nw, kb=sv["kpe"])
    cosq, sinq, cosk, sink = tabs[:4]
    pw = dm.H * MLA_ROPE
    dqp_u, dkr = rowwise(n("rope_bwd"), lambda a, b, cq, sq, ck, sk: (_rope_t(a, cq, sq), _rope_t(b, ck, sk)),
                         [(dqp, pw, 0), (dkp, LANES, 0), (cosq, pw, 0), (sinq, pw, 0), (cosk, LANES, 0), (sink, LANES, 0)], [],
                         [(pw, BF16), (LANES, BF16)], M=M)
    dq = lane_concat(n("dq_cat"), [dqn, dqp_u], M=M)
    dkv = lane_concat(n("dkv_cat"), [dkn, dv_], M=M)
    dcqn = matmul(n("q_up_dx"), dq, lw["w_uq"], "nt", M=M, N=dm.QL, K=dm.QW)
    g["w_uq"] = matmul(n("q_up_dw"), sv["cqn"], dq, "tn", M=dm.QL, N=dm.QW, K=M, out_dtype=BF16)
    dckvn = matmul(n("kv_up_dx"), dkv, lw["w_ukv"], "nt", M=M, N=dm.KVL, K=dm.KVW)
    g["w_ukv"] = matmul(n("kv_up_dw"), sv["ckvn"], dkv, "tn", M=dm.KVL, N=dm.KVW, K=M, out_dtype=BF16)
    dcq, g["g_q"] = rms_bwd(n("rms_q_bwd"), proj, dm.QL, dm.o_cq, lw["g_q"], dcqn, M=M, out_dtype=BF16)
    dckv, g["g_kv"] = rms_bwd(n("rms_kv_bwd"), proj, dm.KVL, dm.o_ckv, lw["g_kv"], dckvn, M=M, out_dtype=BF16)
    dproj = hook_b(lane_concat(n("dproj_cat"), [dcq, du, dqd, dkd, dvd, dckv, dkr], M=M, pad_to=dm.PW), g)
    dh1 = matmul(n("in_proj_dx"), dproj, lw["w_in"], "nt", M=M, N=D, K=dm.PW)
    g["w_in"] = matmul(n("in_proj_dw"), sv["h1"], dproj, "tn", M=D, N=dm.PW, K=M, out_dtype=BF16)
    dx_in, g["g_mix"] = rms_bwd(n("rms_mix_bwd"), sv["x_in"], D, 0, lw["g_mix"], dh1, dx_mid, M=M)
    return dx_in, g


def layer_params(dm, small, l):
    lw = {k: small[k][l].reshape(1, -1) for k in ("g_mix", "g_q", "g_kv", "b_glu", "g_out_mla", "g_out_ssm", "g_out_dil", "g_ffn", "d_skip")}
    raw = _ssm_layer_params(dm, small["a_re"][l], small["a_im"][l], small["log_dt"][l], small["b_re"][l], small["b_im"][l])
    ar, ai, bbr, bbi = ssm_param_fwd(f"ssm_param_l{l}", *raw)
    g_ = dm.G
    bd = lambda t: _block_diag(jnp.transpose(t.reshape(SSM_GROUP, g_, SSM_STATE), (1, 0, 2)), g_)
    cd = lambda t: _block_diag(jnp.transpose(t, (0, 2, 1)), g_)
    cre, cim = cd(small["c_re"][l]), cd(small["c_im"][l])
    sp = {"acat": _cat_cols(ar, ai), "acat_conj": _cat_cols(ar, -ai),
          "bcat": _cat_cols(bd(bbr), bd(bbi)).astype(BF16),
          "ccat": _cat_cols(cre.T, -cim.T).T.astype(BF16)}
    return lw, sp, raw


def ssm_param_grads(dm, l, g, raw):
    d_acat, d_bcat, d_ccat_t = g.pop("ssm_raw")
    d_ar, d_ai = _uncat_cols(d_acat)
    dbr, dbi = ssm_diag_rows(f"ssm_db_l{l}", d_bcat)
    dcr, dci = ssm_diag_rows(f"ssm_dc_l{l}", d_ccat_t)
    g_ = dm.G
    da_re, da_im, dldt, db_re, db_im = ssm_param_bwd(f"ssm_param_bwd_l{l}", *raw, d_ar, d_ai, dbr, dbi)
    g["a_re"], g["a_im"] = da_re.reshape(g_, SSM_STATE), da_im.reshape(g_, SSM_STATE)
    g["log_dt"] = jnp.sum(dldt.reshape(g_, SSM_STATE), axis=1)
    from_rows = lambda t: jnp.transpose(t.reshape(SSM_GROUP, g_, SSM_STATE), (1, 2, 0))
    g["b_re"], g["b_im"] = from_rows(db_re), from_rows(db_im)
    g["c_re"] = jnp.transpose(dcr.reshape(SSM_GROUP, g_, SSM_STATE), (1, 0, 2))
    g["c_im"] = -jnp.transpose(dci.reshape(SSM_GROUP, g_, SSM_STATE), (1, 0, 2))
    g["d_skip"] = g["d_skip"].reshape(g_, SSM_GROUP)


def loss_and_grad(dm, h, target, g_final):
    D = dm.D

    def loss_fn(xb, tb, gb):
        y, vjp = jax.vjp(_rms, xb, gb)
        err = y - tb
        dxb, dg = vjp(err * (1.0 / D))
        part = 0.5 * jnp.sum(jnp.mean(err * err, axis=-1, keepdims=True), axis=0, keepdims=True)
        lane = lax.broadcasted_iota(jnp.int32, (1, LANES), 1)
        return dxb, dg, jnp.where(lane == 0, part, 0.0)

    return rowwise("loss", loss_fn, [(h, D, 0), (target, D, 0)], [g_final.reshape(1, D)], [(D, F32)], [D, LANES], M=dm.M)


KIND = {"w_in": "cols", "w_uq": "cols", "w_ukv": "cols", "w_glu": "cols", "w_o": "rows", "w_gate": "cols", "w_up": "cols", "w_down": "rows"}
SHARDED = list(KIND)
TRANSPOSED = ("w_in",)
GATHER_GROUPS = {"mixer": ["w_in", "w_uq", "w_ukv", "w_glu"], "rest": ["w_o", "w_gate", "w_up", "w_down"]}
REDUCE_GROUPS = {"ffn": ["w_gate", "w_up", "w_down"], "others": ["w_o", "w_in", "w_uq", "w_ukv", "w_glu"]}
SMALL = ["g_mix", "g_q", "g_kv", "a_re", "a_im", "b_re", "b_im", "c_re", "c_im", "d_skip", "log_dt", "b_glu",
         "g_out_mla", "g_out_ssm", "g_out_dil", "g_ffn", "g_final"]
ORDER = ["g_mix", "w_in", "g_q", "w_uq", "g_kv", "w_ukv", "a_re", "a_im", "b_re", "b_im", "c_re", "c_im", "d_skip", "log_dt",
         "w_glu", "b_glu", "g_out_mla", "g_out_ssm", "g_out_dil", "w_o", "g_ffn", "w_gate", "w_up", "w_down", "g_final"]


def kernel(x, g_mix, w_in, g_q, w_uq, g_kv, w_ukv, a_re, a_im, b_re, b_im, c_re, c_im, d_skip, log_dt, w_glu, b_glu, g_out_mla, g_out_ssm, g_out_dil, w_o, g_ffn, w_gate, w_up, w_down, g_final, loss_target, m_g_mix, m_w_in, m_g_q, m_w_uq, m_g_kv, m_w_ukv, m_a_re, m_a_im, m_b_re, m_b_im, m_c_re, m_c_im, m_d_skip, m_log_dt, m_w_glu, m_b_glu, m_g_out_mla, m_g_out_ssm, m_g_out_dil, m_w_o, m_g_ffn, m_w_gate, m_w_up, m_w_down, m_g_final, v_g_mix, v_w_in, v_g_q, v_w_uq, v_g_kv, v_w_ukv, v_a_re, v_a_im, v_b_re, v_b_im, v_c_re, v_c_im, v_d_skip, v_log_dt, v_w_glu, v_b_glu, v_g_out_mla, v_g_out_ssm, v_g_out_dil, v_w_o, v_g_ffn, v_w_gate, v_w_up, v_w_down, v_g_final):
    args = locals()
    w = {k: args[k] for k in ORDER}
    mom = {k: args["m_" + k] for k in ORDER}
    var = {k: args["v_" + k] for k in ORDER}
    dm = Dims(x, g_q, g_kv, g_out_mla, g_out_ssm, g_out_dil, w_gate.shape[-1] * N_CHIPS)
    L = g_mix.shape[0]

    small = {k: w[k] for k in SMALL}
    na = len(SHARDED)
    tabs = _rope_tables(dm) + (dilated_bias(dm.M),)
    sel ={"w_in": selection_matrices(dm.src_in, w_in.shape[-1]), "w_uq": selection_matrices(dm.src_q, w_uq.shape[-1]),
           "w_ukv": selection_matrices(dm.src_kv, w_ukv.shape[-1])}

    stored = lambda t, k: jnp.swapaxes(t, 1, 2) if k in TRANSPOSED else t
    store_kind = {k: "rows" if k in TRANSPOSED else KIND[k] for k in SHARDED}
    G = [{k: cast_into_slot(f"cast_{k}_l{l}", stored(w[k], k), store_kind[k], l) for k in SHARDED} for l in range(L)]
    sems = {}

    def gather_stage(stage, plan):
        def start(l, grp, car):
            names = GATHER_GROUPS[grp]
            sems[stage, l, grp], bufs, _, car = split_start(f"ag_{stage}_start_{grp}_l{l}", plan, 3 * len(names), [G[l][k] for k in names], [], car)
            G[l].update(zip(names, bufs))
            return car

        def wait(l, grp, car):
            names = GATHER_GROUPS[grp]
            bufs, car = split_wait(f"ag_{stage}_wait_{grp}_l{l}", plan, sems[stage, l, grp], [G[l][k] for k in names], car)
            G[l].update(zip(names, bufs))
            return car

        return start, wait

    ici_start, ici_wait = gather_stage("ici", plan_gather_ici)
    pass_start, pass_wait = gather_stage("pass", plan_gather_pass)

    car = tabs[0]
    for l, grp in ((0, "mixer"), (0, "rest"), (1, "mixer")):
        if l < L:
            car = ici_start(l, grp, car)
    tabs = (pass_wait(0, "mixer", pass_start(0, "mixer", ici_wait(0, "mixer", car))),) + tabs[1:]
    h = x.reshape(dm.M, dm.D)
    lws, sps, raws, saved = [], [], [], []
    for l in range(L):
        lw, sp, raw = layer_params(dm, small, l)
        for k in GATHER_GROUPS["mixer"]:
            lw[k] = regroup_cols(f"regroup_{k}_l{l}", G[l][k], sel[k], k in TRANSPOSED) if k in sel else G[l][k]

        def at_q(car, l=l):
            return pass_start(l, "rest", ici_wait(l, "rest", car))

        def at_mid(car, lw, l=l):
            car = pass_wait(l, "rest", car)
            lw.update({k: G[l][k] for k in GATHER_GROUPS["rest"]})
            if l + 1 < L:
                car = pass_wait(l + 1, "mixer", pass_start(l + 1, "mixer", ici_wait(l + 1, "mixer", car)))
                car = ici_start(l + 1, "rest", car)
            if l + 2 < L:
                car = ici_start(l + 2, "mixer", car)
            return car

        h, sv = layer_forward(dm, l, h, lw, sp, tabs, at_q, at_mid)
        lws.append(lw)
        sps.append(sp)
        raws.append(raw)
        saved.append(sv)
    dx, g_final_part, loss_part = loss_and_grad(dm, h, loss_target.reshape(dm.M, dm.D), w["g_final"])

    def reduce_begin(l, grp, g, car):
        names = REDUCE_GROUPS[grp]
        parts = [g.pop(k) for k in names]
        fresh = [jax.ShapeDtypeStruct(p.shape[1:], BF16) for p in parts]
        sm, parts, gots, car = split_start(f"rs_pair_start_{grp}_l{l}", plan_pair(len(names)), len(names), parts, fresh, car)
        return {"l": l, "grp": grp, "names": names, "sems": sm, "parts": parts, "gots": gots}, car

    def reduce_chips(st, car):
        names, tag, n = st["names"], f"{st['grp']}_l{st['l']}", len(st["names"])
        bufs, car = split_wait(f"rs_pair_wait_{tag}", plan_pair(n), st["sems"], st["parts"] + st["gots"], car)
        sums = [pair_sum(f"rs_pairsum_{a}_l{st['l']}", bufs[i], bufs[n + i]) for i, a in enumerate(names)]
        fresh = [jax.ShapeDtypeStruct((3,) + s.shape[1:], BF16) for s, _ in sums]
        st["sems"], st["s"], st["arrived"], car = split_start(f"rs_chips_start_{tag}", plan_chips(n), 3 * n, [s for s, _ in sums], fresh, car)
        st["own"] = [o for _, o in sums]
        return car

    def reduce_share(st, car):
        names, tag, n = st["names"], f"{st['grp']}_l{st['l']}", len(st["names"])
        bufs, car = split_wait(f"rs_chips_wait_{tag}", plan_chips(n), st["sems"], st["s"] + st["arrived"], car)
        mine = [chips_sum(f"rs_sum_{a}_l{st['l']}", st["own"][i], bufs[n + i]) for i, a in enumerate(names)]
        fresh = [jax.ShapeDtypeStruct(m_.shape, F32) for m_ in mine]
        st["sems"], st["mine"], st["theirs"], car = split_start(f"rs_share_start_{tag}", plan_share(n), n, mine, fresh, car)
        return car

    def reduce_end(st, car):
        n = len(st["names"])
        bufs, car = split_wait(f"rs_share_wait_{st['grp']}_l{st['l']}", plan_share(n), st["sems"], st["mine"] + st["theirs"], car)
        for i, k in enumerate(st["names"]):
            reduced[st["l"]][k] = (bufs[i], bufs[n + i])
        return car

    reduced, grads, prev_ffn, prev_oth = [{} for _ in range(L)], [None] * L, None, None
    for l in reversed(range(L)):
        mine = {}

        def at_a(car, g, l=l, mine=mine, pf=prev_ffn, po=prev_oth):
            mine["st"], car = reduce_begin(l, "ffn", g, car)
            if pf is not None:
                car = reduce_chips(po, reduce_share(pf, car))
            return car

        def at_m(car, g, mine=mine, pf=prev_ffn):
            car = reduce_chips(mine["st"], car)
            return car if pf is None else reduce_end(pf, car)

        def at_b(car, g, po=prev_oth):
            return car if po is None else reduce_share(po, car)

        dx, g = layer_backward(dm, l, dx, lws[l], sps[l], tabs, saved[l], at_a, at_m, at_b)
        ssm_param_grads(dm, l, g, raws[l])
        car = dx if l else loss_part
        if prev_oth is not None:
            car = reduce_end(prev_oth, car)
        for k in sel:
            g[k] = ungroup_cols(f"ungroup_{k}_l{l}", g[k], sel[k], k in TRANSPOSED)
        prev_ffn = mine["st"]
        prev_oth, car = reduce_begin(l, "others", g, car)
        if l:
            dx = car
        grads[l] = g
    car = reduce_end(prev_ffn, reduce_share(prev_ffn, reduce_chips(prev_oth, car)))
    loss_part = reduce_end(prev_oth, reduce_share(prev_oth, car))

    gsum = {}
    small_names = [k for k in SMALL if k != "g_final"]
    pieces = [jnp.stack([grads[l][k] for l in range(L)]).reshape(-1) for k in small_names] + [g_final_part.reshape(-1), loss_part.reshape(-1)]
    sizes = [int(p.shape[0]) for p in pieces]
    total = sum(sizes)
    rows = -(-total // (LANES * 16)) * 16
    pack = lambda ps: jnp.concatenate(ps + [jnp.zeros((rows * LANES - total,), F32)]).reshape(rows, LANES)
    red = all_reduce_small(pack(pieces))
    flat = red.reshape(-1)
    offs = np.concatenate([[0], np.cumsum(sizes)]).astype(int)
    names = small_names + ["g_final"]
    for i, k in enumerate(names):
        gsum[k] = flat[offs[i]:offs[i + 1]].reshape(w[k].shape)
    loss = flat[offs[len(names)]]

    delta, new_m, new_v = {}, {}, {}
    for k in SHARDED:
        res = adamw_layers(f"adam_{k}", stored(w[k], k), stored(mom[k], k), stored(var[k], k), [reduced[l][k][0] for l in range(L)],
                           [reduced[l][k][1] for l in range(L)], store_kind[k])
        gsum[k], delta[k], new_m[k], new_v[k] = (stored(t, k) for t in res)
    sm_sizes = sizes[:len(names)]
    sm_total = sum(sm_sizes)
    packs = lambda d: jnp.concatenate([d[k].reshape(-1) for k in names] + [jnp.zeros((rows * LANES - sm_total,), F32)]).reshape(rows, LANES)
    gs = jnp.concatenate([flat[:sm_total], jnp.zeros((rows * LANES - sm_total,), F32)]).reshape(rows, LANES)
    d_, m_, v_ = adamw("adam_small", packs(w), gs, packs(mom), packs(var))
    for i, k in enumerate(names):
        sl = slice(offs[i], offs[i + 1])
        delta[k], new_m[k], new_v[k] = (t.reshape(-1)[sl].reshape(w[k].shape) for t in (d_, m_, v_))

    return (loss, dx.reshape(x.shape), *[gsum[k] for k in ORDER], *[delta[k] for k in ORDER],
            *[new_m[k] for k in ORDER], *[new_v[k] for k in ORDER])
```

```python
import functools
import math

import numpy as np
import jax
import jax.numpy as jnp
from jax import lax
from jax.experimental import pallas as pl
from jax.experimental.pallas import tpu as pltpu

F32 = jnp.float32
BF16 = jnp.bfloat16
MESH = pl.DeviceIdType.MESH

NORM_EPS = 1e-6
MLA_NOPE, MLA_ROPE, MLA_V = 128, 64, 128
SSM_GROUP, SSM_STATE = 16, 64
DIL_HEAD = 64
DIL_PATTERNS = ((128, 1), (512, 4), (2048, 16))
ROPE_THETA = 10000.0
ADAM_LR, ADAM_B1, ADAM_B2, ADAM_EPS, ADAM_WD, ADAM_STEP = 0.001, 0.9, 0.999, 1e-08, 0.01, 10
N_CHIPS = 4

LANES = 128
SUBLANES_BF16 = 16
VMEM_LIMIT = 56 * 1024 * 1024
ROW_BUDGET = 20 * 1024 * 1024
MM_BUDGET = 40 * 1024 * 1024
NEG = -1e30


def _cparams(sem=None):
    return pltpu.CompilerParams(dimension_semantics=sem, vmem_limit_bytes=VMEM_LIMIT)


def _pick(n, cap, q, off=0):
    best = None
    for d in range(q, min(n, cap) + 1, q):
        if n % d == 0 and off % d == 0:
            best = d
    if best is None or (best * 4 <= min(cap, n) and n <= 3072 and off % n == 0):
        assert off % n == 0, (n, off)
        return n
    return best


_DOT_DIMS = {"nn": (((1,), (0,)), ((), ())), "nt": (((1,), (1,)), ((), ())), "tn": (((0,), (0,)), ((), ()))}


def _divs(n, q, within=None, off=0):
    return [d for d in range(q, n + 1, q) if n % d == 0 and off % d == 0 and (within is None or within % d == 0)] or [n]


def _mm_tiles(M, N, K, tms, tns, tks, ab, bb, ob):
    best = None
    for tk in tks:
        nk = K // tk
        for tn in tns:
            for tm in tms:
                if 2 * (tm * tk * ab + tk * tn * bb + tm * tn * ob) + tm * tn * 4 * (2 if nk > 1 else 1) > MM_BUDGET:
                    continue
                steps = (M // tm) * (N // tn) * nk
                hbm = M * K * ab * (1 if nk == 1 else N // tn) + K * N * bb * (M // tm) + M * N * ob
                cost = steps * 0.35e-6 + hbm / 3.0e12 + (nk - 1) * M * N * 12 / 4.0e12
                if best is None or cost < best[0]:
                    best = (cost, tm, tn, tk)
    assert best is not None, (M, N, K)
    return best[1:]


def matmul(name, a, b, mode, *, M, N, K, a_off=(0, 0), b_off=(0, 0), b_lead=None, w=None, add=None, out_dtype=F32, into=None):
    tn_mode = mode == "tn"
    a_ro, a_co = (a_off[1], a_off[0]) if tn_mode else a_off
    b_no, b_ko = b_off if mode == "nt" else (b_off[1], b_off[0])
    n_within = k_within = m_within = None
    if w is not None:
        kind, shard = w
        if kind == "cols":
            b = b.reshape(N_CHIPS, b.shape[1] * b.shape[2], b.shape[3])
        rows_within, cols_within = (K if mode == "nn" else N, shard) if kind == "cols" else (shard, (N if mode == "nn" else K) // 2)
        k_within, n_within = (rows_within, cols_within) if mode == "nn" else (cols_within, rows_within)
    if into is not None:
        m_within, n_within = (M // 2, into[1]) if into[0] == "cols" else (into[1], N // 2)
    tms = [d for d in _divs(M, 128 if tn_mode else SUBLANES_BF16, m_within, a_ro) if d <= 1408]
    tns = [d for d in _divs(N, LANES, n_within, b_no) if d <= 2048]
    tks = _divs(K, SUBLANES_BF16 if tn_mode else LANES, k_within, math.gcd(a_co, b_ko))
    ob = jnp.dtype(out_dtype).itemsize + (add.dtype.itemsize if add is not None else 0)
    tm, tn, tk = _mm_tiles(M, N, K, tms, tns, tks, a.dtype.itemsize, b.dtype.itemsize, ob)
    nk = K // tk
    dn = _DOT_DIMS[mode]

    if tn_mode:
        a_spec = pl.BlockSpec((tk, tm), lambda i, j, k: (k + a_co // tk, i + a_ro // tm))
    else:
        a_spec = pl.BlockSpec((tm, tk), lambda i, j, k: (i + a_ro // tm, k + a_co // tk))
    b_blk = (tn, tk) if mode == "nt" else (tk, tn)
    if w is not None:
        tr_, tc_ = (tk, tn) if mode == "nn" else (tn, tk)
        rper, cper = rows_within // tr_, cols_within // tc_

        def wmap(rb, cb):
            if kind == "cols":
                return (cb // cper, rb, cb % cper)
            return (rb // rper, cb // cper, rb % rper, cb % cper)

        imap = (lambda i, j, k: wmap(k, j)) if mode == "nn" else (lambda i, j, k: wmap(j, k))
        b_spec = pl.BlockSpec((None,) * (b.ndim - 2) + b_blk, imap)
    else:
        if mode == "nt":
            imap = lambda i, j, k: (j + b_no // tn, k + b_ko // tk)
        else:
            imap = lambda i, j, k: (k + b_ko // tk, j + b_no // tn)
        if b_lead is None:
            b_spec = pl.BlockSpec(b_blk, imap)
        else:
            b_spec = pl.BlockSpec((None,) + b_blk, lambda i, j, k: (b_lead,) + imap(i, j, k))
    o_plain = pl.BlockSpec((tm, tn), lambda i, j, k: (i, j))
    if into is None:
        o_spec, out_shape = o_plain, jax.ShapeDtypeStruct((M, N), out_dtype)
    else:
        rper, cper = m_within // tm, n_within // tn
        if into[0] == "cols":
            o_spec = pl.BlockSpec((None, None, tm, tn), lambda i, j, k: (i // rper, j // cper, i % rper, j % cper))
        else:
            o_spec = pl.BlockSpec((None, None, tm, tn), lambda i, j, k: (j // cper, i // rper, i % rper, j % cper))
        out_shape = jax.ShapeDtypeStruct((2, N_CHIPS, m_within, n_within), out_dtype)
    has_add = add is not None
    n_in = 2 + has_add

    def body(*refs):
        a_ref, b_ref = refs[0], refs[1]
        add_ref = refs[2] if has_add else None
        o_ref = refs[n_in]
        part = lax.dot_general(a_ref[...].astype(BF16), b_ref[...].astype(BF16), dn, preferred_element_type=F32)

        def finish(r):
            if has_add:
                r = r + add_ref[...].astype(F32)
            o_ref[...] = r.astype(o_ref.dtype)

        if nk == 1:
            finish(part)
        else:
            acc_ref = refs[-1]
            k = pl.program_id(2)

            @pl.when(k == 0)
            def _():
                acc_ref[...] = part

            @pl.when((k > 0) & (k < nk - 1))
            def _():
                acc_ref[...] += part

            @pl.when(k == nk - 1)
            def _():
                finish(acc_ref[...] + part)

    in_specs = [a_spec, b_spec] + ([o_plain] if has_add else [])
    args = (a, b) + ((add,) if has_add else ())
    return pl.pallas_call(
        body, name=name, out_shape=out_shape, grid=(M // tm, N // tn, nk), in_specs=in_specs, out_specs=o_spec,
        scratch_shapes=[pltpu.VMEM((tm, tn), F32)] if nk > 1 else [],
        compiler_params=_cparams(("parallel", "parallel", "arbitrary")),
    )(*args)


def selection_matrices(src_of_new, n_shard):
    src_np = np.asarray(src_of_new, np.int64)
    src = jnp.asarray(src_np.astype(np.int32))
    ref = jnp.arange(N_CHIPS, dtype=jnp.int32)[:, None] * n_shard + jnp.arange(n_shard, dtype=jnp.int32)[None, :]
    pm = (ref[:, :, None] == src[None, None, :]).astype(BF16)
    pmt = (src[None, :, None] == ref[:, None, :]).astype(BF16)
    tc = _pick(len(src_np), 512, LANES)
    feeds = [sorted({int(s) // n_shard for s in src_np[cb * tc:(cb + 1) * tc] if s >= 0}) for cb in range(len(src_np) // tc)]
    return pm, pmt, tc, feeds


def regroup_cols(name, g, sel, transposed=False):
    pm, _, tc, feeds = sel
    n_new = pm.shape[-1]
    if transposed:
        nn, kh = g.shape[2:]
        K = 2 * kh
        tm = _pick(kh, 512, LANES)
        hb = kh // tm
        g_spec = pl.BlockSpec((N_CHIPS, None, nn, tm), lambda c, i: (0, i // hb, 0, i % hb))
    else:
        nn = g.shape[-1]
        g = g.reshape(N_CHIPS, -1, nn)
        K = g.shape[1]
        tm = _pick(K, 512, SUBLANES_BF16)
        g_spec = pl.BlockSpec((N_CHIPS, tm, nn), lambda c, i: (0, i, 0))

    def body(g_ref, pm_ref, o_ref):
        for cb, chips in enumerate(feeds):
            @pl.when(pl.program_id(0) == cb)
            def _(chips=chips):
                acc = jnp.zeros((tm, tc), F32)
                for j in chips:
                    acc = acc + _dot(g_ref[j], pm_ref[j], "tn" if transposed else "nn")
                o_ref[...] = acc.astype(o_ref.dtype)

    return pl.pallas_call(
        body, name=name, out_shape=jax.ShapeDtypeStruct((K, n_new), BF16), grid=(n_new // tc, K // tm),
        in_specs=[g_spec, pl.BlockSpec((N_CHIPS, nn, tc), lambda c, i: (0, 0, c))],
        out_specs=pl.BlockSpec((tm, tc), lambda c, i: (i, c)), compiler_params=_cparams(("parallel", "parallel")),
    )(g, pm)


def ungroup_cols(name, dw, sel, transposed=False):
    pm, pmt, tc, feeds = sel
    K, n_new = dw.shape
    nn = pmt.shape[-1]
    kh = K // 2
    tm = _pick(kh, 512, LANES if transposed else SUBLANES_BF16)
    hb = kh // tm
    fed_by = [[cb for cb, chips in enumerate(feeds) if j in chips] for j in range(N_CHIPS)]

    def body(dw_ref, sel_ref, o_ref):
        for j, blocks in enumerate(fed_by):
            @pl.when(pl.program_id(0) == j)
            def _(blocks=blocks):
                acc = jnp.zeros((nn, tm) if transposed else (tm, nn), F32)
                for cb in blocks:
                    cols = slice(cb * tc, (cb + 1) * tc)
                    if transposed:
                        acc = acc + _dot(sel_ref[:, cols], dw_ref[:, cols], "nt")
                    else:
                        acc = acc + _dot(dw_ref[:, cols], sel_ref[cols, :], "nn")
                o_ref[...] = acc.astype(o_ref.dtype)

    if transposed:
        sel_arr, sel_spec = pm, pl.BlockSpec((None, nn, n_new), lambda j, i: (j, 0, 0))
        out_shape = jax.ShapeDtypeStruct((2, N_CHIPS, nn, kh), BF16)
        out_spec = pl.BlockSpec((None, None, nn, tm), lambda j, i: (i // hb, j, 0, i % hb))
    else:
        sel_arr, sel_spec = pmt, pl.BlockSpec((None, n_new, nn), lambda j, i: (j, 0, 0))
        out_shape = jax.ShapeDtypeStruct((2, N_CHIPS, kh, nn), BF16)
        out_spec = pl.BlockSpec((None, None, tm, nn), lambda j, i: (i // hb, j, i % hb, 0))
    return pl.pallas_call(
        body, name=name, out_shape=out_shape, grid=(N_CHIPS, 2 * hb),
        in_specs=[pl.BlockSpec((tm, n_new), lambda j, i: (i, 0)), sel_spec], out_specs=out_spec,
        compiler_params=_cparams(("parallel", "parallel")),
    )(dw, sel_arr)


def rowwise(name, fn, rows, vecs, outs, sums=(), *, M):
    rows = [tuple(r) + (0,) * (4 - len(r)) for r in rows]
    nr, nv, no, ns = len(rows), len(vecs), len(outs), len(sums)
    per_row = sum(w * a.dtype.itemsize for a, w, _, _ in rows) + sum(w * jnp.dtype(d).itemsize for w, d in outs)
    tr = _pick(M, max(8, min(512, ROW_BUDGET // (2 * per_row))), 16 if M % 16 == 0 else 8)

    def body(*refs):
        i = pl.program_id(0)
        res = fn(*[r[...] for r in refs[:nr + nv]])
        o_refs = refs[nr + nv:nr + nv + no]
        s_refs = refs[nr + nv + no:]
        for ref, val in zip(o_refs, res[:no]):
            ref[...] = val.astype(ref.dtype)
        if ns:
            @pl.when(i == 0)
            def _():
                for ref in s_refs:
                    ref[...] = jnp.zeros(ref.shape, F32)

            for ref, val in zip(s_refs, res[no:]):
                ref[...] += val

    in_specs = [pl.BlockSpec((tr, w), functools.partial(lambda i, cb, rb: (i + rb, cb), cb=off // w, rb=roff // tr)) for _, w, off, roff in rows]
    for _, w, off, roff in rows:
        assert off % w == 0 and roff % tr == 0
    in_specs += [pl.BlockSpec(v.shape, functools.partial(lambda i, nd: (0,) * nd, nd=v.ndim)) for v in vecs]
    out_specs = [pl.BlockSpec((tr, w), lambda i: (i, 0)) for w, _ in outs]
    out_specs += [pl.BlockSpec((1, w), lambda i: (0, 0)) for w in sums]
    out_shape = [jax.ShapeDtypeStruct((M, w), d) for w, d in outs] + [jax.ShapeDtypeStruct((1, w), F32) for w in sums]
    return pl.pallas_call(
        body, name=name, out_shape=out_shape, grid=(M // tr,), in_specs=in_specs, out_specs=out_specs,
        compiler_params=_cparams(("arbitrary",) if ns else ("parallel",)),
    )(*[r[0] for r in rows], *vecs)


def _rms(x, g):
    xf = x.astype(F32)
    return xf * lax.rsqrt(jnp.mean(xf * xf, axis=-1, keepdims=True) + NORM_EPS) * g


def _gelu(y):
    return 0.5 * y * (1.0 + jnp.tanh(math.sqrt(2.0 / math.pi) * (y + 0.044715 * (y * y * y))))


def _colsum(v):
    return jnp.sum(v, axis=0, keepdims=True)


def rms_fwd(name, x, width, off, g, *, M):
    return rowwise(name, lambda xb, gb: (_rms(xb, gb),), [(x, width, off)], [g], [(width, BF16)], M=M)[0]


def rms_bwd(name, x, width, off, g, dy, resid=None, *, M, out_dtype=F32):
    def fn(xb, dyb, *rest):
        gb = rest[-1]
        _, vjp = jax.vjp(_rms, xb.astype(F32), gb)
        dx, dg = vjp(dyb.astype(F32))
        if resid is not None:
            dx = dx + rest[0]
        return dx, dg

    rows = [(x, width, off), (dy, width, 0)] + ([(resid, width, 0)] if resid is not None else [])
    return rowwise(name, fn, rows, [g], [(width, out_dtype)], [width], M=M)


def lane_concat(name, parts, *, M, pad_to=None):
    width = sum(p.shape[1] for p in parts)
    pad = 0 if pad_to is None else pad_to - width

    def fn(*blocks):
        cols = [b.astype(BF16) for b in blocks]
        if pad:
            cols.append(jnp.zeros((blocks[0].shape[0], pad), BF16))
        return (jnp.concatenate(cols, axis=1),)

    return rowwise(name, fn, [(p, p.shape[1], 0) for p in parts], [], [(width + pad, BF16)], M=M)[0]


def _swap_halves(x, half):
    w = x.shape[-1]
    lane = lax.broadcasted_iota(jnp.int32, x.shape, x.ndim - 1)
    first = (lane % (2 * half)) < half
    return jnp.where(first, pltpu.roll(x, w - half, x.ndim - 1), pltpu.roll(x, half, x.ndim - 1))


def dilated_bias(M):
    delta = jnp.arange(M, dtype=jnp.int32)[:, None] - jnp.arange(M, dtype=jnp.int32)[None, :]
    w = jnp.zeros(delta.shape, F32)
    for window, dil in DIL_PATTERNS:
        ok = (delta >= 0) & (delta <= window)
        if dil > 1:
            ok = ok & ((delta & (dil - 1)) == 0)
        w = w + ok.astype(F32)
    return jnp.where(w > 0, jnp.log(jnp.maximum(w, 1.0)), NEG)


def _dot(a, b, mode):
    return lax.dot_general(a, b, _DOT_DIMS[mode], preferred_element_type=F32)


def attention_fwd(name, qa, qa_off, ka, ka_off, v, v_off, *, da, dv, pairs, scale, M, qb=None, qb_off=0, kb=None, bias=None):
    tq = min(256, M)
    tk = min(1024, M)
    has_b = qb is not None
    has_bias = bias is not None
    dr = MLA_ROPE

    def body(*refs):
        refs = list(refs)
        bias_ref = refs.pop(3) if has_bias else None
        if has_b:
            qa_ref, ka_ref, v_ref, qb_ref, kb_ref, o_ref, lse_ref = refs
        else:
            qa_ref, ka_ref, v_ref, o_ref, lse_ref = refs
        i = pl.program_id(1)
        t0 = i * tq
        nkb = (t0 + tq + tk - 1) // tk
        n_full = nkb if has_bias else t0 // tk
        q1s = [qa_ref[:, hh * da:(hh + 1) * da].astype(BF16) for hh in range(2)]
        q2s = [qb_ref[:, hh * dr:(hh + 1) * dr].astype(BF16) if has_b else None for hh in range(2)]

        def step(kbi, carry, masked):
            ks = pl.multiple_of(kbi * tk, tk)
            k2 = kb_ref[pl.ds(ks, tk), 0:dr].astype(BF16) if has_b else None
            if has_bias:
                extra = bias_ref[:, pl.ds(ks, tk)]
            elif masked:
                delta = (t0 + lax.broadcasted_iota(jnp.int32, (tq, tk), 0)) - (ks + lax.broadcasted_iota(jnp.int32, (tq, tk), 1))
            new = []
            for hh, (m, l, acc) in enumerate(carry):
                k1 = ka_ref[pl.ds(ks, tk), hh * da:(hh + 1) * da].astype(BF16)
                s = _dot(q1s[hh], k1, "nt")
                if has_b:
                    s = s + _dot(q2s[hh], k2, "nt")
                s = s * scale
                if has_bias:
                    s = s + extra
                elif masked:
                    s = jnp.where(delta >= 0, s, NEG)
                m_new = jnp.maximum(m, jnp.max(s, axis=1, keepdims=True))
                alpha = jnp.exp(m - m_new)
                p = jnp.exp(s - m_new)
                l = alpha * l + jnp.sum(p, axis=1, keepdims=True)
                vv = v_ref[pl.ds(ks, tk), hh * dv:(hh + 1) * dv].astype(BF16)
                acc = alpha * acc + _dot(p.astype(BF16), vv, "nn")
                new.append((m_new, l, acc))
            return tuple(new)

        carry = tuple((jnp.full((tq, 1), NEG, F32), jnp.zeros((tq, 1), F32), jnp.zeros((tq, dv), F32)) for _ in range(2))
        carry = lax.fori_loop(0, n_full, functools.partial(step, masked=False), carry)
        carry = lax.fori_loop(n_full, nkb, functools.partial(step, masked=True), carry)
        o_parts = [acc / l for _, l, acc in carry]
        lse_parts = [m + jnp.log(l) for m, l, _ in carry]
        o_ref[...] = jnp.concatenate(o_parts, axis=1)
        lane = lax.broadcasted_iota(jnp.int32, (tq, LANES), 1)
        lse_ref[...] = jnp.where(lane == 0, lse_parts[0], jnp.where(lane == 1, lse_parts[1], 0.0))

    assert qa_off % (2 * da) == 0 and ka_off % (2 * da) == 0 and v_off % (2 * dv) == 0
    in_specs = [
        pl.BlockSpec((tq, 2 * da), lambda hp, i: (i, qa_off // (2 * da) + hp)),
        pl.BlockSpec((M, 2 * da), lambda hp, i: (0, ka_off // (2 * da) + hp)),
        pl.BlockSpec((M, 2 * dv), lambda hp, i: (0, v_off // (2 * dv) + hp)),
    ]
    args = [qa, ka, v]
    if has_bias:
        in_specs.append(pl.BlockSpec((tq, M), lambda hp, i: (i, 0)))
        args.append(bias)
    if has_b:
        assert qb_off % LANES == 0
        in_specs += [pl.BlockSpec((tq, LANES), lambda hp, i: (i, qb_off // LANES + hp)),
                     pl.BlockSpec((M, LANES), lambda hp, i: (0, 0))]
        args += [qb, kb]
    out_specs = [pl.BlockSpec((tq, 2 * dv), lambda hp, i: (i, hp)),
                 pl.BlockSpec((None, tq, LANES), lambda hp, i: (hp, i, 0))]
    out_shape = [jax.ShapeDtypeStruct((M, pairs * 2 * dv), F32), jax.ShapeDtypeStruct((pairs, M, LANES), F32)]
    return pl.pallas_call(
        body, name=name, out_shape=out_shape, grid=(pairs, M // tq), in_specs=in_specs, out_specs=out_specs,
        compiler_params=_cparams(("parallel", "arbitrary")),
    )(*args)


def attention_bwd(name, qa, qa_off, ka, ka_off, v, v_off, o, do, lse, *, da, dv, pairs, scale, M,
                  qb=None, qb_off=0, kb=None, bias=None, tk_cap=512):
    tq = min(256, M)
    tk = min(tk_cap, M)
    has_b = qb is not None
    has_bias = bias is not None
    dr = MLA_ROPE

    def body(*refs):
        refs = list(refs)
        bias_ref = refs.pop(6) if has_bias else None
        if has_b:
            qa_ref, ka_ref, v_ref, o_ref, do_ref, lse_ref, qb_ref, kb_ref, dqa_ref, dka_ref, dv_ref, dqb_ref, dkb_ref = refs
        else:
            qa_ref, ka_ref, v_ref, o_ref, do_ref, lse_ref, dqa_ref, dka_ref, dv_ref = refs
        hp = pl.program_id(0)
        i = pl.program_id(1)
        t0 = i * tq
        nkb = (t0 + tq + tk - 1) // tk
        n_full = nkb if has_bias else t0 // tk

        @pl.when(i == 0)
        def _():
            dka_ref[...] = jnp.zeros(dka_ref.shape, F32)
            dv_ref[...] = jnp.zeros(dv_ref.shape, F32)

        if has_b:
            @pl.when((i == 0) & (hp == 0))
            def _():
                dkb_ref[...] = jnp.zeros(dkb_ref.shape, F32)

        q1s = [qa_ref[:, hh * da:(hh + 1) * da].astype(BF16) for hh in range(2)]
        q2s = [qb_ref[:, hh * dr:(hh + 1) * dr].astype(BF16) if has_b else None for hh in range(2)]
        do_bfs = [do_ref[:, hh * dv:(hh + 1) * dv].astype(BF16) for hh in range(2)]
        rowdots = [jnp.sum(do_ref[:, hh * dv:(hh + 1) * dv] * o_ref[:, hh * dv:(hh + 1) * dv], axis=1, keepdims=True) for hh in range(2)]
        lses = [lse_ref[:, hh:hh + 1] for hh in range(2)]

        def step(kbi, carry, masked):
            ks = pl.multiple_of(kbi * tk, tk)
            k2 = kb_ref[pl.ds(ks, tk), 0:dr].astype(BF16) if has_b else None
            if has_bias:
                extra = bias_ref[:, pl.ds(ks, tk)]
            elif masked:
                delta = (t0 + lax.broadcasted_iota(jnp.int32, (tq, tk), 0)) - (ks + lax.broadcasted_iota(jnp.int32, (tq, tk), 1))
            new, dkb_part = [], None
            for hh, (dq1, dq2) in enumerate(carry):
                k1 = ka_ref[pl.ds(ks, tk), hh * da:(hh + 1) * da].astype(BF16)
                s = _dot(q1s[hh], k1, "nt")
                if has_b:
                    s = s + _dot(q2s[hh], k2, "nt")
                s = s * scale
                if has_bias:
                    s = s + extra
                elif masked:
                    s = jnp.where(delta >= 0, s, NEG)
                p = jnp.exp(s - lses[hh])
                vv = v_ref[pl.ds(ks, tk), hh * dv:(hh + 1) * dv].astype(BF16)
                dp = _dot(do_bfs[hh], vv, "nt")
                ds = (p * (dp - rowdots[hh]) * scale).astype(BF16)
                dq1 = dq1 + _dot(ds, k1, "nn")
                dka_ref[pl.ds(ks, tk), hh * da:(hh + 1) * da] += _dot(ds, q1s[hh], "tn")
                dv_ref[pl.ds(ks, tk), hh * dv:(hh + 1) * dv] += _dot(p.astype(BF16), do_bfs[hh], "tn")
                if has_b:
                    dq2 = dq2 + _dot(ds, k2, "nn")
                    part = _dot(ds, q2s[hh], "tn")
                    dkb_part = part if dkb_part is None else dkb_part + part
                new.append((dq1, dq2))
            if has_b:
                dkb_ref[pl.ds(ks, tk), 0:dr] += dkb_part
            return tuple(new)

        carry = tuple((jnp.zeros((tq, da), F32), jnp.zeros((tq, dr), F32)) for _ in range(2))
        carry = lax.fori_loop(0, n_full, functools.partial(step, masked=False), carry)
        carry = lax.fori_loop(n_full, nkb, functools.partial(step, masked=True), carry)
        dqa_ref[...] = jnp.concatenate([c[0] for c in carry], axis=1).astype(dqa_ref.dtype)
        if has_b:
            dqb_ref[...] = jnp.concatenate([c[1] for c in carry], axis=1).astype(dqb_ref.dtype)

    in_specs = [
        pl.BlockSpec((tq, 2 * da), lambda hp, i: (i, qa_off // (2 * da) + hp)),
        pl.BlockSpec((M, 2 * da), lambda hp, i: (0, ka_off // (2 * da) + hp)),
        pl.BlockSpec((M, 2 * dv), lambda hp, i: (0, v_off // (2 * dv) + hp)),
        pl.BlockSpec((tq, 2 * dv), lambda hp, i: (i, hp)),
        pl.BlockSpec((tq, 2 * dv), lambda hp, i: (i, hp)),
        pl.BlockSpec((None, tq, LANES), lambda hp, i: (hp, i, 0)),
    ]
    args = [qa, ka, v, o, do, lse]
    if has_bias:
        in_specs.append(pl.BlockSpec((tq, M), lambda hp, i: (i, 0)))
        args.append(bias)
    out_specs = [pl.BlockSpec((tq, 2 * da), lambda hp, i: (i, hp)),
                 pl.BlockSpec((M, 2 * da), lambda hp, i: (0, hp)),
                 pl.BlockSpec((M, 2 * dv), lambda hp, i: (0, hp))]
    out_shape = [jax.ShapeDtypeStruct((M, pairs * 2 * da), BF16),
                 jax.ShapeDtypeStruct((M, pairs * 2 * da), F32),
                 jax.ShapeDtypeStruct((M, pairs * 2 * dv), F32)]
    if has_b:
        in_specs += [pl.BlockSpec((tq, LANES), lambda hp, i: (i, qb_off // LANES + hp)),
                     pl.BlockSpec((M, LANES), lambda hp, i: (0, 0))]
        args += [qb, kb]
        out_specs += [pl.BlockSpec((tq, LANES), lambda hp, i: (i, hp)), pl.BlockSpec((M, LANES), lambda hp, i: (0, 0))]
        out_shape += [jax.ShapeDtypeStruct((M, pairs * LANES), F32), jax.ShapeDtypeStruct((M, LANES), F32)]
    return pl.pallas_call(
        body, name=name, out_shape=out_shape, grid=(pairs, M // tq), in_specs=in_specs, out_specs=out_specs,
        compiler_params=_cparams(("arbitrary", "arbitrary")),
    )(*args)


def ssm_scan(name, xcat, acat, *, M, reverse=False, hcat=None):
    C2 = xcat.shape[1]
    cb = LANES
    tb = min(128, M)
    nblk = M // tb
    with_da = hcat is not None
    nsub = 1
    wide = nsub * 2 * cb

    def body(*refs):
        if with_da:
            x_ref, a_ref, h_ref, o_ref, da_ref, p_ref = refs
        else:
            x_ref, a_ref, o_ref, p_ref = refs
        re = [slice(s * 2 * cb, s * 2 * cb + cb) for s in range(nsub)]
        im = [slice(s * 2 * cb + cb, (s + 1) * 2 * cb) for s in range(nsub)]
        ars, ais = [a_ref[:, c] for c in re], [a_ref[:, c] for c in im]
        row = lax.broadcasted_iota(jnp.int32, (tb, cb), 0)

        def logscan(xs):
            ps = list(zip(ars, ais))
            d = 1
            while d < tb:
                shift = tb - d if reverse else d
                keep = (row < tb - d) if reverse else (row >= d)
                nxt = []
                for (xr, xi), (pr, pi) in zip(xs, ps):
                    sr = jnp.where(keep, pltpu.roll(xr, shift, 0), 0.0)
                    si = jnp.where(keep, pltpu.roll(xi, shift, 0), 0.0)
                    nxt.append((xr + pr * sr - pi * si, xi + pr * si + pi * sr))
                xs = nxt
                ps = [(pr * pr - pi * pi, 2.0 * pr * pi) for pr, pi in ps]
                d *= 2
            return xs

        seed = row == (tb - 1 if reverse else 0)
        for s, (p0r, p0i) in enumerate(logscan([(jnp.where(seed, ar, 0.0), jnp.where(seed, ai, 0.0)) for ar, ai in zip(ars, ais)])):
            p_ref[:, re[s]] = p0r
            p_ref[:, im[s]] = p0i
        sub = lax.broadcasted_iota(jnp.int32, (8, cb), 0)
        edge = 0 if reverse else tb - 8
        pick = sub == (0 if reverse else 7)

        def blk(b, carry):
            bb = (nblk - 1 - b) if reverse else b
            t0 = pl.multiple_of(bb * tb, tb)
            te = pl.multiple_of(t0 + edge, 8)
            hs = logscan([(x_ref[pl.ds(t0, tb), re[s]], x_ref[pl.ds(t0, tb), im[s]]) for s in range(nsub)])
            new = []
            for s, ((hr, hi), (cr, ci)) in enumerate(zip(hs, carry)):
                pr, pi = p_ref[:, re[s]], p_ref[:, im[s]]
                o_ref[pl.ds(t0, tb), re[s]] = hr + pr * cr - pi * ci
                o_ref[pl.ds(t0, tb), im[s]] = hi + pr * ci + pi * cr
                new.append((jnp.sum(jnp.where(pick, o_ref[pl.ds(te, 8), re[s]], 0.0), axis=0, keepdims=True),
                            jnp.sum(jnp.where(pick, o_ref[pl.ds(te, 8), im[s]], 0.0), axis=0, keepdims=True)))
            return tuple(new)

        lax.fori_loop(0, nblk, blk, tuple((jnp.zeros((1, cb), F32), jnp.zeros((1, cb), F32)) for _ in range(nsub)))
        if with_da:
            first = lax.broadcasted_iota(jnp.int32, (M, cb), 0) >= 1
            for s in range(nsub):
                hpr = jnp.where(first, pltpu.roll(h_ref[:, re[s]], 1, 0), 0.0)
                hpi = jnp.where(first, pltpu.roll(h_ref[:, im[s]], 1, 0), 0.0)
                lr, li = o_ref[:, re[s]], o_ref[:, im[s]]
                da_ref[:, re[s]] = _colsum(lr * hpr + li * hpi)
                da_ref[:, im[s]] = _colsum(li * hpr - lr * hpi)

    blk_spec = pl.BlockSpec((M, wide), lambda j: (0, j))
    vec_spec = pl.BlockSpec((1, wide), lambda j: (0, j))
    in_specs = [blk_spec, vec_spec] + ([blk_spec] if with_da else [])
    out_specs = [blk_spec] + ([vec_spec] if with_da else [])
    out_shape = [jax.ShapeDtypeStruct((M, C2), F32)] + ([jax.ShapeDtypeStruct((1, C2), F32)] if with_da else [])
    args = [xcat, acat] + ([hcat] if with_da else [])
    res = pl.pallas_call(
        body, name=name, out_shape=out_shape, grid=(C2 // wide,), in_specs=in_specs, out_specs=out_specs,
        scratch_shapes=[pltpu.VMEM((tb, wide), F32)], compiler_params=_cparams(("parallel",)),
    )(*args)
    return res if with_da else res[0]


def ssm_diag_rows(name, m):
    assert LANES == 2 * SSM_STATE
    gp, c2 = m.shape
    groups = gp // SSM_GROUP
    st = SSM_STATE

    def body(m_ref, re_ref, im_ref):
        x = m_ref[...]
        odd = pl.program_id(0) % 2

        @pl.when(odd == 0)
        def _():
            re_ref[:, 0:st] = x[:, 0:st]
            im_ref[:, 0:st] = x[:, LANES:LANES + st]

        @pl.when(odd == 1)
        def _():
            re_ref[:, st:LANES] = x[:, st:LANES]
            im_ref[:, st:LANES] = x[:, LANES + st:2 * LANES]

    out = jax.ShapeDtypeStruct((SSM_GROUP, c2 // 2), F32)
    o_spec = pl.BlockSpec((SSM_GROUP, LANES), lambda g: (0, g // 2))
    return pl.pallas_call(
        body, name=name, out_shape=[out, out], grid=(groups,), in_specs=[pl.BlockSpec((SSM_GROUP, 2 * LANES), lambda g: (g, g // 2))],
        out_specs=[o_spec, o_spec], compiler_params=_cparams(("arbitrary",)),
    )(m)


def _ssm_param_fn(a_re, a_im, ldt, b_re, b_im):
    lr, li = jnp.minimum(a_re, -1e-4), a_im
    dt = jnp.exp(ldt)
    e, ang = jnp.exp(lr * dt), li * dt
    ar, ai = e * jnp.cos(ang), e * jnp.sin(ang)
    den = lr * lr + li * li
    nr, ni = ar - 1.0, ai
    cr, ci = (nr * lr + ni * li) / den, (ni * lr - nr * li) / den
    return ar, ai, cr * b_re - ci * b_im, cr * b_im + ci * b_re


def _whole(shape):
    return pl.BlockSpec(shape, functools.partial(lambda nd: (0,) * nd, nd=len(shape)))


def ssm_param_fwd(name, a_re, a_im, ldt, b_re, b_im):
    def body(*refs):
        res = _ssm_param_fn(*[r[...] for r in refs[:5]])
        for ref, val in zip(refs[5:], res):
            ref[...] = val

    ins = [a_re, a_im, ldt, b_re, b_im]
    outs = [a_re, a_re, b_re, b_re]
    return pl.pallas_call(
        body, name=name, out_shape=[jax.ShapeDtypeStruct(t.shape, F32) for t in outs],
        in_specs=[_whole(t.shape) for t in ins], out_specs=[_whole(t.shape) for t in outs], compiler_params=_cparams(),
    )(*ins)


def ssm_param_bwd(name, a_re, a_im, ldt, b_re, b_im, d_ar, d_ai, d_bbr, d_bbi):
    def body(*refs):
        _, vjp = jax.vjp(_ssm_param_fn, *[r[...] for r in refs[:5]])
        res = vjp(tuple(r[...] for r in refs[5:9]))
        for ref, val in zip(refs[9:], res):
            ref[...] = val

    ins = [a_re, a_im, ldt, b_re, b_im, d_ar, d_ai, d_bbr, d_bbi]
    outs = [a_re, a_im, ldt, b_re, b_im]
    return pl.pallas_call(
        body, name=name, out_shape=[jax.ShapeDtypeStruct(t.shape, F32) for t in outs],
        in_specs=[_whole(t.shape) for t in ins], out_specs=[_whole(t.shape) for t in outs], compiler_params=_cparams(),
    )(*ins)


ANY = pl.BlockSpec(memory_space=pl.ANY)


def _place():
    x, y, c = lax.axis_index("x"), lax.axis_index("y"), lax.axis_index("c")
    chips = [(1 - x, y), (x, 1 - y), (1 - x, 1 - y)]
    return x, y, c, chips


def cast_into_slot(name, w, kind, l, after):
    K, nn = w.shape[-2:]
    hr, hc = (K // 2, nn) if kind == "cols" else (K, nn // 2)
    tr = _pick(hr, max(16, min(512, ROW_BUDGET // (2 * hc * 6))), SUBLANES_BF16)
    nb = hr // tr

    def body(w_ref, after_ref, o_ref, after_out):
        o_ref[...] = w_ref[...].astype(BF16)

    if kind == "cols":
        in_spec = pl.BlockSpec((None, tr, hc), lambda h, i: (l, h * nb + i, 0))
    else:
        in_spec = pl.BlockSpec((None, tr, hc), lambda h, i: (l, i, h))
    return pl.pallas_call(
        body, name=name, out_shape=[jax.ShapeDtypeStruct((N_CHIPS, 2, hr, hc), BF16), jax.ShapeDtypeStruct(after.shape, after.dtype)],
        grid=(2, nb), in_specs=[in_spec, ANY],
        out_specs=[pl.BlockSpec((None, None, tr, hc), lambda h, i: (2 * lax.axis_index("x") + lax.axis_index("y"), h, i, 0)), ANY],
        input_output_aliases={1: 1}, compiler_params=_cparams(("arbitrary", "arbitrary")),
    )(w, after)


HBM_SPEC = pl.BlockSpec(memory_space=pltpu.HBM)
SEM_SPEC = pl.BlockSpec(memory_space=pltpu.SEMAPHORE)
SPLIT_PARAMS = pltpu.CompilerParams(has_side_effects=pltpu.SideEffectType.DATAFLOW_SIDE_EFFECTING)


def _in_hbm(t):
    return pltpu.with_memory_space_constraint(t, pltpu.HBM)


def split_start(name, plan, n, bufs, fresh, carrier):
    nb, nf = len(bufs), len(fresh)

    def body(*refs):
        outs = refs[nb + 1:]
        for i, (s, d, dev) in enumerate(plan(list(outs[2:2 + nb + nf]))):
            pltpu.make_async_remote_copy(src_ref=s, dst_ref=d, send_sem=outs[0].at[i], recv_sem=outs[1].at[i],
                                         device_id=dev, device_id_type=MESH).start()

    hbm = lambda t: pltpu.HBM(t.shape, t.dtype)
    res = pl.pallas_call(
        body, name=name,
        out_shape=(pltpu.SemaphoreType.DMA((n,)), pltpu.SemaphoreType.DMA((n,)), *[hbm(t) for t in bufs], *[hbm(t) for t in fresh], hbm(carrier)),
        in_specs=[HBM_SPEC] * (nb + 1), out_specs=(SEM_SPEC, SEM_SPEC) + (HBM_SPEC,) * (nb + nf + 1),
        input_output_aliases={**{i: 2 + i for i in range(nb)}, nb: 2 + nb + nf}, compiler_params=SPLIT_PARAMS,
    )(*[_in_hbm(t) for t in bufs], _in_hbm(carrier))
    return (res[0], res[1]), list(res[2:2 + nb]), list(res[2 + nb:2 + nb + nf]), res[2 + nb + nf]


def split_wait(name, plan, sems, bufs, carrier):
    nb = len(bufs)

    def body(*refs):
        for i, (s, d, dev) in enumerate(plan(list(refs[:nb]))):
            cp = pltpu.make_async_remote_copy(src_ref=s, dst_ref=d, send_sem=refs[nb].at[i], recv_sem=refs[nb + 1].at[i],
                                              device_id=dev, device_id_type=MESH)
            cp.wait_send()
            cp.wait_recv()

    hbm = lambda t: pltpu.HBM(t.shape, t.dtype)
    res = pl.pallas_call(
        body, name=name, out_shape=(*[hbm(t) for t in bufs], hbm(carrier)),
        in_specs=[HBM_SPEC] * nb + [SEM_SPEC, SEM_SPEC, HBM_SPEC], out_specs=(HBM_SPEC,) * (nb + 1),
        input_output_aliases={**{i: i for i in range(nb)}, nb + 2: nb}, compiler_params=SPLIT_PARAMS,
    )(*bufs, sems[0], sems[1], carrier)
    return list(res[:nb]), res[nb]


def _me_sib_chips():
    x, y, c, chips = _place()
    return 2 * x + y, c, (x, y, 1 - c), chips


def plan_gather_ici(refs):
    me, c, _, chips = _me_sib_chips()
    return [(r.at[me, c], r.at[me, c], (chip[0], chip[1], c)) for r in refs for chip in chips]


def plan_gather_pass(refs):
    _, c, sib, chips = _me_sib_chips()
    return [(r.at[2 * chip[0] + chip[1], c], r.at[2 * chip[0] + chip[1], c], sib) for r in refs for chip in chips]


def plan_pair(n_arrays):
    def plan(refs):
        _, c, sib, _ = _me_sib_chips()
        return [(refs[a].at[1 - c], refs[n_arrays + a], sib) for a in range(n_arrays)]
    return plan


def plan_chips(n_arrays):
    def plan(refs):
        _, c, _, chips = _me_sib_chips()
        return [(refs[a].at[2 * chip[0] + chip[1]], refs[n_arrays + a].at[k], (chip[0], chip[1], c))
                for a in range(n_arrays) for k, chip in enumerate(chips)]
    return plan


def plan_share(n_arrays):
    def plan(refs):
        _, _, sib, _ = _me_sib_chips()
        return [(refs[a], refs[n_arrays + a], sib) for a in range(n_arrays)]
    return plan


def swap_with_sibling(name, src, pick_other_half):
    shape = src.shape[1:] if pick_other_half else src.shape

    def body(src_ref, out_ref, ssem, rsem):
        x, y, c, _ = _place()
        cp = pltpu.make_async_remote_copy(src_ref=src_ref.at[1 - c] if pick_other_half else src_ref, dst_ref=out_ref,
                                          send_sem=ssem, recv_sem=rsem, device_id=(x, y, 1 - c), device_id_type=MESH)
        cp.start()
        cp.wait()

    return pl.pallas_call(
        body, name=name, out_shape=jax.ShapeDtypeStruct(shape, src.dtype), in_specs=[ANY], out_specs=ANY,
        scratch_shapes=[pltpu.SemaphoreType.DMA(()), pltpu.SemaphoreType.DMA(())],
    )(src)


def exchange_chips(name, src, per_chip):
    shape = src.shape[1:] if per_chip else src.shape

    def body(src_ref, out_ref, send_sems, recv_sems):
        x, y, c, chips = _place()
        cps = []
        for k, chip in enumerate(chips):
            s = src_ref.at[2 * chip[0] + chip[1]] if per_chip else src_ref
            cps.append(pltpu.make_async_remote_copy(src_ref=s, dst_ref=out_ref.at[k], send_sem=send_sems.at[k], recv_sem=recv_sems.at[k],
                                                    device_id=(chip[0], chip[1], c), device_id_type=MESH))
        for cp in cps:
            cp.start()
        for cp in cps:
            cp.wait()

    return pl.pallas_call(
        body, name=name, out_shape=jax.ShapeDtypeStruct((3,) + shape, src.dtype), in_specs=[ANY], out_specs=ANY,
        scratch_shapes=[pltpu.SemaphoreType.DMA((3,)), pltpu.SemaphoreType.DMA((3,))],
    )(src)


def pair_sum(name, p, got):
    _, _, rh, cw = p.shape
    tr = _pick(rh, max(16, min(512, ROW_BUDGET // (2 * cw * 10))), SUBLANES_BF16)

    def body(p_ref, got_ref, s_ref, own_ref):
        j = pl.program_id(1)
        tot = p_ref[...].astype(F32) + got_ref[...].astype(F32)
        s_ref[...] = tot.astype(BF16)

        @pl.when(j == 2 * lax.axis_index("x") + lax.axis_index("y"))
        def _():
            own_ref[...] = tot

    return pl.pallas_call(
        body, name=name, grid=(rh // tr, N_CHIPS),
        in_specs=[pl.BlockSpec((None, None, tr, cw), lambda i, j: (lax.axis_index("c"), j, i, 0)),
                  pl.BlockSpec((None, tr, cw), lambda i, j: (j, i, 0))],
        out_specs=[pl.BlockSpec((None, tr, cw), lambda i, j: (j, i, 0)),
                   pl.BlockSpec((tr, cw), lambda i, j: (i, 0))],
        out_shape=[jax.ShapeDtypeStruct((N_CHIPS, rh, cw), BF16), jax.ShapeDtypeStruct((rh, cw), F32)],
        compiler_params=_cparams(("arbitrary", "arbitrary")),
    )(p, got)


def chips_sum(name, own, parts):
    rh, cw = own.shape
    parts = parts.reshape(3 * rh, cw)
    return rowwise(name, lambda o, a, b, c: (((o + a.astype(F32)) + b.astype(F32)) + c.astype(F32),),
                   [(own, cw, 0), (parts, cw, 0, 0), (parts, cw, 0, rh), (parts, cw, 0, 2 * rh)], [], [(cw, F32)], M=rh)[0]


def all_reduce_small(buf):
    r = buf.shape[0]
    got = swap_with_sibling("ar_pair", buf, False)
    chip = rowwise("ar_pairsum", lambda a, b: (a + b,), [(buf, LANES, 0), (got, LANES, 0)], [], [(LANES, F32)], M=r)[0]
    parts = exchange_chips("ar_chips", chip, False).reshape(3 * r, LANES)
    return rowwise("ar_sum", lambda o, fx, fy, fxy: ((o + fy) + (fx + fxy),),
                   [(chip, LANES, 0), (parts, LANES, 0, 0), (parts, LANES, 0, r), (parts, LANES, 0, 2 * r)], [], [(LANES, F32)], M=r)[0]


def _adam_fn(w, g, m, v):
    m = ADAM_B1 * m + (1.0 - ADAM_B1) * g
    v = ADAM_B2 * v + (1.0 - ADAM_B2) * (g * g)
    m_hat = m / (1.0 - ADAM_B1 ** ADAM_STEP)
    v_hat = v / (1.0 - ADAM_B2 ** ADAM_STEP)
    return -ADAM_LR * (m_hat / (jnp.sqrt(v_hat) + ADAM_EPS) + ADAM_WD * w), m, v


def adamw(name, w, g, m, v):
    r, cw = w.shape
    return rowwise(name, _adam_fn, [(t, cw, 0) for t in (w, g, m, v)], [], [(cw, F32)] * 3, M=r)


def adamw_layers(name, w, m, v, mines, theirs, kind):
    L, K, nn = w.shape
    hr, hc = (K // 2, nn) if kind == "cols" else (K, nn // 2)
    tr = _pick(hr, max(8, min(256, MM_BUDGET // (2 * hc * 4 * (7 + 2 * L)))), 8)
    nb = hr // tr

    def body(*refs):
        w_ref, m_ref, v_ref = refs[:3]
        outs = refs[3 + 2 * L:]
        l, mine_here = pl.program_id(0), pl.program_id(1) == lax.axis_index("c")
        g = jnp.zeros((tr, hc), F32)
        for ll in range(L):
            g = jnp.where(l == ll, jnp.where(mine_here, refs[3 + ll][...], refs[3 + L + ll][...]), g)
        outs[0][...] = g
        outs[1][...], outs[2][...], outs[3][...] = _adam_fn(w_ref[...], g, m_ref[...], v_ref[...])

    if kind == "cols":
        full = pl.BlockSpec((None, tr, hc), lambda l, h, i: (l, h * nb + i, 0))
    else:
        full = pl.BlockSpec((None, tr, hc), lambda l, h, i: (l, i, h))
    def half_spec(ll, mine):
        def imap(l, h, i):
            here = (l == ll) & ((h == lax.axis_index("c")) == mine)
            return (jnp.where(here, i, 0), 0)
        return pl.BlockSpec((tr, hc), imap)

    halves = [half_spec(ll, True) for ll in range(L)] + [half_spec(ll, False) for ll in range(L)]
    return pl.pallas_call(
        body, name=name, grid=(L, 2, nb), in_specs=[full] * 3 + halves, out_specs=[full] * 4,
        out_shape=[jax.ShapeDtypeStruct((L, K, nn), F32)] * 4, compiler_params=_cparams(("parallel", "parallel", "parallel")),
    )(w, m, v, *mines, *theirs)


class Dims:
    def __init__(self, x, g_q, g_kv, g_out_mla, g_out_ssm, g_out_dil, ff):
        self.M, self.D = x.shape[-2], x.shape[-1]
        self.QL, self.KVL = g_q.shape[-1], g_kv.shape[-1]
        self.MW, self.SW, self.DW = g_out_mla.shape[-1], g_out_ssm.shape[-1], g_out_dil.shape[-1]
        self.H = self.MW // MLA_V
        self.FF = ff
        self.G = self.SW // SSM_GROUP
        self.C = self.G * SSM_STATE
        self.o_cq, self.o_u = 0, self.QL
        self.o_qd = self.o_u + self.SW
        self.o_kd = self.o_qd + self.DW
        self.o_vd = self.o_kd + self.DW
        self.o_ckv = self.o_vd + self.DW
        self.o_kr = self.o_ckv + self.KVL
        self.PW = -(-(self.o_kr + MLA_ROPE) // (4 * LANES)) * (4 * LANES)
        assert self.o_u % self.SW == 0 and self.o_qd % LANES == 0 and self.o_ckv % self.KVL == 0 and self.o_kr % LANES == 0
        assert self.H % 2 == 0 and self.DW % LANES == 0 and self.C % LANES == 0
        self.QW = self.H * (MLA_NOPE + MLA_ROPE)
        self.KVW = self.H * (MLA_NOPE + MLA_V)
        sizes = [self.QL, self.KVL, MLA_ROPE, self.SW, self.DW, self.DW, self.DW]
        starts = np.concatenate([[0], np.cumsum(sizes)[:-1]])
        self.ref_cols = {n: (int(s), int(z)) for n, s, z in zip(["cq", "ckv", "kr", "u", "qd", "kd", "vd"], starts, sizes)}
        self.INW = int(sum(sizes))
        self.new_order = ["cq", "u", "qd", "kd", "vd", "ckv", "kr"]
        src = np.concatenate([np.arange(self.ref_cols[n][0], self.ref_cols[n][0] + self.ref_cols[n][1]) for n in self.new_order])
        self.src_in = np.concatenate([src, -np.ones(self.PW - self.INW, np.int64)])
        self.src_q = self._heads_split(self.H, MLA_NOPE, MLA_ROPE)
        self.src_kv = self._heads_split(self.H, MLA_NOPE, MLA_V)

    @staticmethod
    def _heads_split(h, d1, d2):
        first = (np.arange(h)[:, None] * (d1 + d2) + np.arange(d1)[None, :]).reshape(-1)
        second = (np.arange(h)[:, None] * (d1 + d2) + d1 + np.arange(d2)[None, :]).reshape(-1)
        return np.concatenate([first, second])


def _regroup_in(dm, w):
    parts = [w[..., dm.ref_cols[n][0]:dm.ref_cols[n][0] + dm.ref_cols[n][1]] for n in dm.new_order]
    pad = dm.PW - dm.INW
    return jnp.concatenate(parts + [jnp.zeros(w.shape[:-1] + (pad,), w.dtype)], axis=-1)


def _ungroup_in(dm, w):
    off, pieces = 0, {}
    for n in dm.new_order:
        pieces[n] = w[..., off:off + dm.ref_cols[n][1]]
        off += dm.ref_cols[n][1]
    return jnp.concatenate([pieces[n] for n in ["cq", "ckv", "kr", "u", "qd", "kd", "vd"]], axis=-1)


def _split_heads(w, h, d1):
    t = w.reshape(w.shape[:-1] + (h, -1))
    return jnp.concatenate([t[..., :d1].reshape(w.shape[:-1] + (-1,)), t[..., d1:].reshape(w.shape[:-1] + (-1,))], axis=-1)


def _merge_heads(w, h, d1):
    a = w[..., :h * d1].reshape(w.shape[:-1] + (h, d1))
    b = w[..., h * d1:].reshape(w.shape[:-1] + (h, -1))
    return jnp.concatenate([a, b], axis=-1).reshape(w.shape[:-1] + (-1,))


def _cat_cols(re, im):
    r, c = re.shape
    return jnp.stack([re.reshape(r, c // LANES, LANES), im.reshape(r, c // LANES, LANES)], axis=2).reshape(r, 2 * c)


def _uncat_cols(cat):
    r, c2 = cat.shape
    t = cat.reshape(r, c2 // (2 * LANES), 2, LANES)
    return t[:, :, 0].reshape(r, c2 // 2), t[:, :, 1].reshape(r, c2 // 2)


def _block_diag(t, g):
    _, a, b = t.shape
    eye = jnp.eye(g, dtype=bool)[:, None, :, None]
    return jnp.where(eye, t[:, :, None, :], 0).reshape(g * a, g * b)


def _diag_blocks(m, g):
    a, b = m.shape[0] // g, m.shape[1] // g
    eye = jnp.eye(g, dtype=m.dtype)[:, None, :, None]
    return jnp.sum(m.reshape(g, a, g, b) * eye, axis=2)


def _rope_tables(dm):
    half = MLA_ROPE // 2
    inv_freq = ROPE_THETA ** (-jnp.arange(half, dtype=F32) / half)
    ang = jnp.arange(dm.M, dtype=F32)[:, None] * inv_freq[None, :]
    cos = jnp.concatenate([jnp.cos(ang), jnp.cos(ang)], axis=1)
    sin = jnp.concatenate([-jnp.sin(ang), jnp.sin(ang)], axis=1)
    return jnp.tile(cos, (1, dm.H)), jnp.tile(sin, (1, dm.H)), jnp.tile(cos, (1, LANES // MLA_ROPE)), jnp.tile(sin, (1, LANES // MLA_ROPE))


def _rope(x, cos, sin):
    return x * cos + _swap_halves(x, MLA_ROPE // 2) * sin


def _rope_t(d, cos, sin):
    return d * cos + _swap_halves(d * sin, MLA_ROPE // 2)


def _ssm_layer_params(dm, a_re, a_im, log_dt, b_re, b_im):
    flat = lambda t: t.reshape(1, dm.C)
    ldt = jnp.repeat(log_dt, SSM_STATE).reshape(1, dm.C)
    bt = lambda t: jnp.transpose(t, (2, 0, 1)).reshape(SSM_GROUP, dm.C)
    return flat(a_re), flat(a_im), ldt, bt(b_re), bt(b_im)


def layer_forward(dm, l, x, lw, sp, tabs, hook_q, hook_mid):
    M, D = dm.M, dm.D
    n = lambda s: f"{s}_l{l}"
    sv = {"x_in": x}
    h1 = rms_fwd(n("rms_mix"), x, D, 0, lw["g_mix"], M=M)
    proj = matmul(n("in_proj"), h1, lw["w_in"], "nn", M=M, N=dm.PW, K=D)
    sv.update(h1=h1, proj=proj)
    cqn = rms_fwd(n("rms_q"), proj, dm.QL, dm.o_cq, lw["g_q"], M=M)
    q = matmul(n("q_up"), cqn, lw["w_uq"], "nn", M=M, N=dm.QW, K=dm.QL)
    ckvn = rms_fwd(n("rms_kv"), proj, dm.KVL, dm.o_ckv, lw["g_kv"], M=M)
    kv = matmul(n("kv_up"), ckvn, lw["w_ukv"], "nn", M=M, N=dm.KVW, K=dm.KVL, out_dtype=BF16)
    cosq, sinq, cosk, sink = tabs[:4]
    nw = dm.H * MLA_NOPE

    def rope_fn(qb, kb, cq, sq, ck, sk):
        return jnp.concatenate([qb[:, :nw], _rope(qb[:, nw:], cq, sq)], axis=1), _rope(kb, ck, sk)

    pw = dm.H * MLA_ROPE
    q_bf, kpe = rowwise(n("rope"), rope_fn, [(q, dm.QW, 0), (proj, LANES, dm.o_kr), (cosq, pw, 0), (sinq, pw, 0), (cosk, LANES, 0), (sink, LANES, 0)],
                        [], [(dm.QW, BF16), (LANES, BF16)], M=M)
    mla_scale = (MLA_NOPE + MLA_ROPE) ** -0.5
    o_mla, lse_mla = attention_fwd(n("mla_fwd"), q_bf, 0, kv, 0, kv, nw, da=MLA_NOPE, dv=MLA_V, pairs=dm.H // 2, scale=mla_scale,
                                   M=M, qb=q_bf, qb_off=nw, kb=kpe)
    sv.update(cqn=cqn, ckvn=ckvn, q_bf=q_bf, kv=kv, kpe=kpe, o_mla=o_mla, lse_mla=lse_mla)
    bu = matmul(n("ssm_bu"), proj, sp["bcat"], "nn", M=M, N=2 * dm.C, K=dm.SW, a_off=(0, dm.o_u))
    hcat = ssm_scan(n("ssm_scan"), bu, sp["acat"], M=M)
    ylin = matmul(n("ssm_y"), hcat, sp["ccat"], "nn", M=M, N=dm.SW, K=2 * dm.C)
    yg = rowwise(n("ssm_gelu"), lambda y, u, d: (_gelu(y + d * u),), [(ylin, dm.SW, 0), (proj, dm.SW, dm.o_u)], [lw["d_skip"]],
                 [(dm.SW, BF16)], M=M)[0]
    z = matmul(n("ssm_glu"), yg, lw["w_glu"], "nn", w=("cols", 2 * dm.SW // N_CHIPS), M=M, N=2 * dm.SW, K=dm.SW)
    sw = dm.SW

    def glu_fn(zb, b):
        zz = zb + b
        return (zz[:, :sw] * jax.nn.sigmoid(zz[:, sw:]),)

    o_ssm = hook_q(rowwise(n("ssm_gate"), glu_fn, [(z, 2 * sw, 0)], [lw["b_glu"]], [(sw, F32)], M=M)[0])
    sv.update(hcat=hcat, ylin=ylin, yg=yg, z=z, o_ssm=o_ssm)
    o_dil, lse_dil = attention_fwd(n("dil_fwd"), proj, dm.o_qd, proj, dm.o_kd, proj, dm.o_vd, da=DIL_HEAD, dv=DIL_HEAD, pairs=dm.DW // LANES,
                                   scale=DIL_HEAD ** -0.5, M=M, bias=tabs[4])
    sv.update(o_dil=o_dil, lse_dil=lse_dil)
    yn = rowwise(n("out_norm"), lambda a, b, c, ga, gb, gc: (jnp.concatenate([_rms(a, ga), _rms(b, gb), _rms(c, gc)], axis=1),),
                 [(o_mla, dm.MW, 0), (o_ssm, dm.SW, 0), (o_dil, dm.DW, 0)], [lw["g_out_mla"], lw["g_out_ssm"], lw["g_out_dil"]],
                 [(D, BF16)], M=M)[0]
    yn = hook_mid(yn, lw)
    x_mid = matmul(n("out_proj"), yn, lw["w_o"], "nn", w=("rows", D // N_CHIPS), M=M, N=D, K=D, add=x)
    h2 = rms_fwd(n("rms_ffn"), x_mid, D, 0, lw["g_ffn"], M=M)
    ffs = dm.FF // N_CHIPS
    gate = matmul(n("ffn_gate"), h2, lw["w_gate"], "nn", w=("cols", ffs), M=M, N=dm.FF, K=D, out_dtype=BF16)
    up = matmul(n("ffn_up"), h2, lw["w_up"], "nn", w=("cols", ffs), M=M, N=dm.FF, K=D, out_dtype=BF16)

    def act_fn(gb, ub):
        gf = gb.astype(F32)
        return (gf * jax.nn.sigmoid(gf) * ub.astype(F32),)

    act = rowwise(n("ffn_act"), act_fn, [(gate, dm.FF, 0), (up, dm.FF, 0)], [], [(dm.FF, BF16)], M=M)[0]
    x_out = matmul(n("ffn_down"), act, lw["w_down"], "nn", w=("rows",ffs), M=M, N=D, K=dm.FF, add=x_mid)
    sv.update(yn=yn, x_mid=x_mid, h2=h2, gate=gate, up=up, act=act)
    return x_out, sv


def layer_backward(dm, l, dx, lw, sp, tabs, sv, hook_a, hook_m, hook_b):
    M, D = dm.M, dm.D
    n = lambda s: f"{s}_l{l}"
    g = {}
    ffs = dm.FF // N_CHIPS
    dact = matmul(n("ffn_down_dx"), dx, lw["w_down"], "nt", w=("rows", ffs), M=M, N=dm.FF, K=D, out_dtype=BF16)
    g["w_down"] = matmul(n("ffn_down_dw"), sv["act"], dx, "tn", M=dm.FF, N=D, K=M, out_dtype=BF16, into=("rows", ffs))

    def act_bwd(gb, ub, db):
        _, vjp = jax.vjp(lambda a, b: a * jax.nn.sigmoid(a) * b, gb.astype(F32), ub.astype(F32))
        return vjp(db.astype(F32))

    dgate, dup = rowwise(n("ffn_act_bwd"), act_bwd, [(sv["gate"], dm.FF, 0), (sv["up"], dm.FF, 0), (dact, dm.FF, 0)], [],
                         [(dm.FF, BF16), (dm.FF, BF16)], M=M)
    dh2 = matmul(n("ffn_gate_dx"), dgate, lw["w_gate"], "nt", w=("cols",ffs), M=M, N=D, K=dm.FF)
    dh2 = matmul(n("ffn_up_dx"), dup, lw["w_up"], "nt", w=("cols",ffs), M=M, N=D, K=dm.FF, add=dh2)
    g["w_gate"] = matmul(n("ffn_gate_dw"), sv["h2"], dgate, "tn", M=D, N=dm.FF, K=M, out_dtype=BF16, into=("cols", ffs))
    g["w_up"] = matmul(n("ffn_up_dw"), sv["h2"], dup, "tn", M=D, N=dm.FF, K=M, out_dtype=BF16, into=("cols", ffs))
    dx_mid, g["g_ffn"] = rms_bwd(n("rms_ffn_bwd"), sv["x_mid"], D, 0, lw["g_ffn"], dh2, dx, M=M)
    dx_mid = hook_a(dx_mid, g)
    dyn = matmul(n("out_proj_dx"), dx_mid, lw["w_o"], "nt", w=("rows", D // N_CHIPS), M=M, N=D, K=D)
    g["w_o"] = matmul(n("out_proj_dw"), sv["yn"], dx_mid, "tn", M=D, N=D, K=M, out_dtype=BF16, into=("rows", D // N_CHIPS))
    mw, sw, dw = dm.MW, dm.SW, dm.DW

    def out_norm_bwd(a, b, c, dy, ga, gb, gc):
        res, sums = [], []
        for t, gg, lo, hi in ((a, ga, 0, mw), (b, gb, mw, mw + sw), (c, gc, mw + sw, mw + sw + dw)):
            _, vjp = jax.vjp(_rms, t, gg)
            dt, dg = vjp(dy[:, lo:hi])
            res.append(dt)
            sums.append(dg)
        return res + sums

    do_mla, do_ssm, do_dil, g["g_out_mla"], g["g_out_ssm"], g["g_out_dil"] = rowwise(
        n("out_norm_bwd"), out_norm_bwd, [(sv["o_mla"], mw, 0), (sv["o_ssm"], sw, 0), (sv["o_dil"], dw, 0), (dyn, D, 0)],
        [lw["g_out_mla"], lw["g_out_ssm"], lw["g_out_dil"]], [(mw, F32), (sw, F32), (dw, F32)], [mw, sw, dw], M=M)
    proj = sv["proj"]
    dqd, dkd, dvd = attention_bwd(n("dil_bwd"), proj, dm.o_qd, proj, dm.o_kd, proj, dm.o_vd, sv["o_dil"], do_dil, sv["lse_dil"],
                                  da=DIL_HEAD, dv=DIL_HEAD, pairs=dw // LANES, scale=DIL_HEAD ** -0.5, M=M, bias=tabs[4], tk_cap=1024)
    do_ssm = hook_m(do_ssm, g)
    def glu_bwd(zb, db, b):
        _, vjp = jax.vjp(lambda zz, bb: (zz + bb)[:, :sw] * jax.nn.sigmoid((zz + bb)[:, sw:]), zb, b)
        return vjp(db)

    dz, g["b_glu"] = rowwise(n("ssm_gate_bwd"), glu_bwd, [(sv["z"], 2 * sw, 0), (do_ssm, sw, 0)], [lw["b_glu"]], [(2 * sw, BF16)], [2 * sw], M=M)
    dyg = matmul(n("ssm_glu_dx"), dz, lw["w_glu"], "nt", w=("cols", 2 * sw // N_CHIPS), M=M, N=sw, K=2 * sw)
    g["w_glu"] = matmul(n("ssm_glu_dw"), sv["yg"], dz, "tn", M=sw, N=2 * sw, K=M, out_dtype=BF16, into=("cols", 2 * sw // N_CHIPS))

    def gelu_bwd(y, u, dy, d):
        _, vjp = jax.vjp(lambda yy, uu, dd: _gelu(yy + dd * uu), y, u, d)
        return vjp(dy)

    dylin, du1, g["d_skip"] = rowwise(n("ssm_gelu_bwd"), gelu_bwd, [(sv["ylin"], sw, 0), (proj, sw, dm.o_u), (dyg, sw, 0)], [lw["d_skip"]],
                                      [(sw, BF16), (sw, F32)], [sw], M=M)
    seed = matmul(n("ssm_y_dx"), dylin, sp["ccat"], "nt", M=M, N=2 * dm.C, K=sw)
    d_ccat = matmul(n("ssm_y_dw"), dylin, sv["hcat"], "tn", M=sw, N=2 * dm.C, K=M)
    lam, d_acat = ssm_scan(n("ssm_scan_bwd"), seed, sp["acat_conj"], M=M, reverse=True, hcat=sv["hcat"])
    du = matmul(n("ssm_bu_dx"), lam, sp["bcat"], "nt", M=M, N=sw, K=2 * dm.C, add=du1, out_dtype=BF16)
    d_bcat = matmul(n("ssm_bu_dw"), proj, lam, "tn", M=sw, N=2 * dm.C, K=M, a_off=(0, dm.o_u))
    g["ssm_raw"] = (d_acat, d_bcat, d_ccat)
    nw = dm.H * MLA_NOPE
    dqn, dkn, dv_, dqp, dkp = attention_bwd(n("mla_bwd"), sv["q_bf"], 0, sv["kv"], 0, sv["kv"], nw, sv["o_mla"], do_mla, sv["lse_mla"],
                                            da=MLA_NOPE, dv=MLA_V, pairs=dm.H // 2, scale=(MLA_NOPE + MLA_ROPE) ** -0.5, M=M,
                                            qb=sv["q_bf"], qb_off=nw, kb=sv["kpe"])
    cosq, sinq, cosk, sink = tabs[:4]
    pw = dm.H * MLA_ROPE
    dqp_u, dkr = rowwise(n("rope_bwd"), lambda a, b, cq, sq, ck, sk: (_rope_t(a, cq, sq), _rope_t(b, ck, sk)),
                         [(dqp, pw, 0), (dkp, LANES, 0), (cosq, pw, 0), (sinq, pw, 0), (cosk, LANES, 0), (sink, LANES, 0)], [],
                         [(pw, BF16), (LANES, BF16)], M=M)
    dq = lane_concat(n("dq_cat"), [dqn, dqp_u], M=M)
    dkv = lane_concat(n("dkv_cat"), [dkn, dv_], M=M)
    dcqn = matmul(n("q_up_dx"), dq, lw["w_uq"], "nt", M=M, N=dm.QL, K=dm.QW)
    g["w_uq"] = matmul(n("q_up_dw"), sv["cqn"], dq, "tn", M=dm.QL, N=dm.QW, K=M, out_dtype=BF16)
    dckvn = matmul(n("kv_up_dx"), dkv, lw["w_ukv"], "nt", M=M, N=dm.KVL, K=dm.KVW)
    g["w_ukv"] = matmul(n("kv_up_dw"), sv["ckvn"], dkv, "tn", M=dm.KVL, N=dm.KVW, K=M, out_dtype=BF16)
    dcq, g["g_q"] = rms_bwd(n("rms_q_bwd"), proj, dm.QL, dm.o_cq, lw["g_q"], dcqn, M=M, out_dtype=BF16)
    dckv, g["g_kv"] = rms_bwd(n("rms_kv_bwd"), proj, dm.KVL, dm.o_ckv, lw["g_kv"], dckvn, M=M, out_dtype=BF16)
    dproj = hook_b(lane_concat(n("dproj_cat"), [dcq, du, dqd, dkd, dvd, dckv, dkr], M=M, pad_to=dm.PW), g)
    dh1 = matmul(n("in_proj_dx"), dproj, lw["w_in"], "nt", M=M, N=D, K=dm.PW)
    g["w_in"] = matmul(n("in_proj_dw"), sv["h1"], dproj, "tn", M=D, N=dm.PW, K=M, out_dtype=BF16)
    dx_in, g["g_mix"] = rms_bwd(n("rms_mix_bwd"), sv["x_in"], D, 0, lw["g_mix"], dh1, dx_mid, M=M)
    return dx_in, g


def layer_params(dm, small, l):
    lw = {k: small[k][l].reshape(1, -1) for k in ("g_mix", "g_q", "g_kv", "b_glu", "g_out_mla", "g_out_ssm", "g_out_dil", "g_ffn", "d_skip")}
    raw = _ssm_layer_params(dm, small["a_re"][l], small["a_im"][l], small["log_dt"][l], small["b_re"][l], small["b_im"][l])
    ar, ai, bbr, bbi = ssm_param_fwd(f"ssm_param_l{l}", *raw)
    g_ = dm.G
    bd = lambda t: _block_diag(jnp.transpose(t.reshape(SSM_GROUP, g_, SSM_STATE), (1, 0, 2)), g_)
    cd = lambda t: _block_diag(jnp.transpose(t, (0, 2, 1)), g_)
    cre, cim = cd(small["c_re"][l]), cd(small["c_im"][l])
    sp = {"acat": _cat_cols(ar, ai), "acat_conj": _cat_cols(ar, -ai),
          "bcat": _cat_cols(bd(bbr), bd(bbi)).astype(BF16),
          "ccat": _cat_cols(cre.T, -cim.T).T.astype(BF16)}
    return lw, sp, raw


def ssm_param_grads(dm, l, g, raw):
    d_acat, d_bcat, d_ccat_t = g.pop("ssm_raw")
    d_ar, d_ai = _uncat_cols(d_acat)
    dbr, dbi = ssm_diag_rows(f"ssm_db_l{l}", d_bcat)
    dcr, dci = ssm_diag_rows(f"ssm_dc_l{l}", d_ccat_t)
    g_ = dm.G
    da_re, da_im, dldt, db_re, db_im = ssm_param_bwd(f"ssm_param_bwd_l{l}", *raw, d_ar, d_ai, dbr, dbi)
    g["a_re"], g["a_im"] = da_re.reshape(g_, SSM_STATE), da_im.reshape(g_, SSM_STATE)
    g["log_dt"] = jnp.sum(dldt.reshape(g_, SSM_STATE), axis=1)
    from_rows = lambda t: jnp.transpose(t.reshape(SSM_GROUP, g_, SSM_STATE), (1, 2, 0))
    g["b_re"], g["b_im"] = from_rows(db_re), from_rows(db_im)
    g["c_re"] = jnp.transpose(dcr.reshape(SSM_GROUP, g_, SSM_STATE), (1, 0, 2))
    g["c_im"] = -jnp.transpose(dci.reshape(SSM_GROUP, g_, SSM_STATE), (1, 0, 2))
    g["d_skip"] = g["d_skip"].reshape(g_, SSM_GROUP)


def loss_and_grad(dm, h, target, g_final):
    D = dm.D

    def loss_fn(xb, tb, gb):
        y, vjp = jax.vjp(_rms, xb, gb)
        err = y - tb
        dxb, dg = vjp(err * (1.0 / D))
        part = 0.5 * jnp.sum(jnp.mean(err * err, axis=-1, keepdims=True), axis=0, keepdims=True)
        lane = lax.broadcasted_iota(jnp.int32, (1, LANES), 1)
        return dxb, dg, jnp.where(lane == 0, part, 0.0)

    return rowwise("loss", loss_fn, [(h, D, 0), (target, D, 0)], [g_final.reshape(1, D)], [(D, F32)], [D, LANES], M=dm.M)


KIND = {"w_in": "cols", "w_uq": "cols", "w_ukv": "cols", "w_glu": "cols", "w_o": "rows", "w_gate": "cols", "w_up": "cols", "w_down": "rows"}
SHARDED = list(KIND)
TRANSPOSED = ("w_in",)
GATHER_GROUPS = {"mixer": ["w_in", "w_uq", "w_ukv", "w_glu"], "rest": ["w_o", "w_gate", "w_up", "w_down"]}
REDUCE_GROUPS = {"ffn": ["w_gate", "w_up", "w_down"], "others": ["w_o", "w_in", "w_uq", "w_ukv", "w_glu"]}
SMALL = ["g_mix", "g_q", "g_kv", "a_re", "a_im", "b_re", "b_im", "c_re", "c_im", "d_skip", "log_dt", "b_glu",
         "g_out_mla", "g_out_ssm", "g_out_dil", "g_ffn", "g_final"]
ORDER = ["g_mix", "w_in", "g_q", "w_uq", "g_kv", "w_ukv", "a_re", "a_im", "b_re", "b_im", "c_re", "c_im", "d_skip", "log_dt",
         "w_glu", "b_glu", "g_out_mla", "g_out_ssm", "g_out_dil", "w_o", "g_ffn", "w_gate", "w_up", "w_down", "g_final"]


def kernel(x, g_mix, w_in, g_q, w_uq, g_kv, w_ukv, a_re, a_im, b_re, b_im, c_re, c_im, d_skip, log_dt, w_glu, b_glu, g_out_mla, g_out_ssm, g_out_dil, w_o, g_ffn, w_gate, w_up, w_down, g_final, loss_target, m_g_mix, m_w_in, m_g_q, m_w_uq, m_g_kv, m_w_ukv, m_a_re, m_a_im, m_b_re, m_b_im, m_c_re, m_c_im, m_d_skip, m_log_dt, m_w_glu, m_b_glu, m_g_out_mla, m_g_out_ssm, m_g_out_dil, m_w_o, m_g_ffn, m_w_gate, m_w_up, m_w_down, m_g_final, v_g_mix, v_w_in, v_g_q, v_w_uq, v_g_kv, v_w_ukv, v_a_re, v_a_im, v_b_re, v_b_im, v_c_re, v_c_im, v_d_skip, v_log_dt, v_w_glu, v_b_glu, v_g_out_mla, v_g_out_ssm, v_g_out_dil, v_w_o, v_g_ffn, v_w_gate, v_w_up, v_w_down, v_g_final):
    args = locals()
    w = {k: args[k] for k in ORDER}
    mom = {k: args["m_" + k] for k in ORDER}
    var = {k: args["v_" + k] for k in ORDER}
    dm = Dims(x, g_q, g_kv, g_out_mla, g_out_ssm, g_out_dil, w_gate.shape[-1] * N_CHIPS)
    L = g_mix.shape[0]

    small = {k: w[k] for k in SMALL}
    na = len(SHARDED)
    tabs = _rope_tables(dm) + (dilated_bias(dm.M),)
    sel ={"w_in": selection_matrices(dm.src_in, w_in.shape[-1]), "w_uq": selection_matrices(dm.src_q, w_uq.shape[-1]),
           "w_ukv": selection_matrices(dm.src_kv, w_ukv.shape[-1])}

    stored = lambda t, k: jnp.swapaxes(t, 1, 2) if k in TRANSPOSED else t
    store_kind = {k: "rows" if k in TRANSPOSED else KIND[k] for k in SHARDED}
    G = [{} for _ in range(L)]
    sems = {}

    def cast_group(l, grp, car):
        for k in GATHER_GROUPS[grp]:
            G[l][k], car = cast_into_slot(f"cast_{k}_l{l}", stored(w[k], k), store_kind[k], l, car)
        return car

    def gather_stage(stage, plan):
        def start(l, grp, car):
            names = GATHER_GROUPS[grp]
            sems[stage, l, grp], bufs, _, car = split_start(f"ag_{stage}_start_{grp}_l{l}", plan, 3 * len(names), [G[l][k] for k in names], [], car)
            G[l].update(zip(names, bufs))
            return car

        def wait(l, grp, car):
            names = GATHER_GROUPS[grp]
            bufs, car = split_wait(f"ag_{stage}_wait_{grp}_l{l}", plan, sems[stage, l, grp], [G[l][k] for k in names], car)
            G[l].update(zip(names, bufs))
            return car

        return start, wait

    ici_start, ici_wait = gather_stage("ici", plan_gather_ici)
    pass_start, pass_wait = gather_stage("pass", plan_gather_pass)

    car = tabs[0]
    first = [(l, grp) for l, grp in ((0, "mixer"), (0, "rest"), (1, "mixer")) if l < L]
    for l, grp in first:
        car = ici_start(l, grp, cast_group(l, grp, car))
    for l in range(L):
        for grp in GATHER_GROUPS:
            if (l, grp) not in first:
                car = cast_group(l, grp, car)
    tabs = (pass_wait(0, "mixer", pass_start(0, "mixer", ici_wait(0, "mixer", car))),) + tabs[1:]
    h = x.reshape(dm.M, dm.D)
    lws, sps, raws, saved = [], [], [], []
    for l in range(L):
        lw, sp, raw = layer_params(dm, small, l)
        for k in GATHER_GROUPS["mixer"]:
            lw[k] = regroup_cols(f"regroup_{k}_l{l}", G[l][k], sel[k], k in TRANSPOSED) if k in sel else G[l][k]

        def at_q(car, l=l):
            return pass_start(l, "rest", ici_wait(l, "rest", car))

        def at_mid(car, lw, l=l):
            car = pass_wait(l, "rest", car)
            lw.update({k: G[l][k] for k in GATHER_GROUPS["rest"]})
            if l + 1 < L:
                car = pass_wait(l + 1, "mixer", pass_start(l + 1, "mixer", ici_wait(l + 1, "mixer", car)))
                car = ici_start(l + 1, "rest", car)
            if l + 2 < L:
                car = ici_start(l + 2, "mixer", car)
            return car

        h, sv = layer_forward(dm, l, h, lw, sp, tabs, at_q, at_mid)
        lws.append(lw)
        sps.append(sp)
        raws.append(raw)
        saved.append(sv)
    dx, g_final_part, loss_part = loss_and_grad(dm, h, loss_target.reshape(dm.M, dm.D), w["g_final"])

    def reduce_begin(l, grp, g, car):
        names = REDUCE_GROUPS[grp]
        parts = [g.pop(k) for k in names]
        fresh = [jax.ShapeDtypeStruct(p.shape[1:], BF16) for p in parts]
        sm, parts, gots, car = split_start(f"rs_pair_start_{grp}_l{l}", plan_pair(len(names)), len(names), parts, fresh, car)
        return {"l": l, "grp": grp, "names": names, "sems": sm, "parts": parts, "gots": gots}, car

    def reduce_chips(st, car):
        names, tag, n = st["names"], f"{st['grp']}_l{st['l']}", len(st["names"])
        bufs, car = split_wait(f"rs_pair_wait_{tag}", plan_pair(n), st["sems"], st["parts"] + st["gots"], car)
        sums = [pair_sum(f"rs_pairsum_{a}_l{st['l']}", bufs[i], bufs[n + i]) for i, a in enumerate(names)]
        fresh = [jax.ShapeDtypeStruct((3,) + s.shape[1:], BF16) for s, _ in sums]
        st["sems"], st["s"], st["arrived"], car = split_start(f"rs_chips_start_{tag}", plan_chips(n), 3 * n, [s for s, _ in sums], fresh, car)
        st["own"] = [o for _, o in sums]
        return car

    def reduce_share(st, car):
        names, tag, n = st["names"], f"{st['grp']}_l{st['l']}", len(st["names"])
        bufs, car = split_wait(f"rs_chips_wait_{tag}", plan_chips(n), st["sems"], st["s"] + st["arrived"], car)
        mine = [chips_sum(f"rs_sum_{a}_l{st['l']}", st["own"][i], bufs[n + i]) for i, a in enumerate(names)]
        fresh = [jax.ShapeDtypeStruct(m_.shape, F32) for m_ in mine]
        st["sems"], st["mine"], st["theirs"], car = split_start(f"rs_share_start_{tag}", plan_share(n), n, mine, fresh, car)
        return car

    def reduce_end(st, car):
        n = len(st["names"])
        bufs, car = split_wait(f"rs_share_wait_{st['grp']}_l{st['l']}", plan_share(n), st["sems"], st["mine"] + st["theirs"], car)
        for i, k in enumerate(st["names"]):
            reduced[st["l"]][k] = (bufs[i], bufs[n + i])
        return car

    reduced, grads, prev_ffn, prev_oth = [{} for _ in range(L)], [None] * L, None, None
    for l in reversed(range(L)):
        mine = {}

        def at_a(car, g, l=l, mine=mine, pf=prev_ffn, po=prev_oth):
            mine["st"], car = reduce_begin(l, "ffn", g, car)
            if pf is not None:
                car = reduce_chips(po, reduce_share(pf, car))
            return car

        def at_m(car, g, mine=mine, pf=prev_ffn):
            car = reduce_chips(mine["st"], car)
            return car if pf is None else reduce_end(pf, car)

        def at_b(car, g, po=prev_oth):
            return car if po is None else reduce_share(po, car)

        dx, g = layer_backward(dm, l, dx, lws[l], sps[l], tabs, saved[l], at_a, at_m, at_b)
        ssm_param_grads(dm, l, g, raws[l])
        car = dx if l else loss_part
        if prev_oth is not None:
            car = reduce_end(prev_oth, car)
        for k in sel:
            g[k] = ungroup_cols(f"ungroup_{k}_l{l}", g[k], sel[k], k in TRANSPOSED)
        prev_ffn = mine["st"]
        prev_oth, car = reduce_begin(l, "others", g, car)
        if l:
            dx = car
        grads[l] = g
    car = reduce_end(prev_ffn, reduce_share(prev_ffn, reduce_chips(prev_oth, car)))
    loss_part = reduce_end(prev_oth, reduce_share(prev_oth, car))

    gsum = {}
    small_names = [k for k in SMALL if k != "g_final"]
    pieces = [jnp.stack([grads[l][k] for l in range(L)]).reshape(-1) for k in small_names] + [g_final_part.reshape(-1), loss_part.reshape(-1)]
    sizes = [int(p.shape[0]) for p in pieces]
    total = sum(sizes)
    rows = -(-total // (LANES * 16)) * 16
    pack = lambda ps: jnp.concatenate(ps + [jnp.zeros((rows * LANES - total,), F32)]).reshape(rows, LANES)
    red = all_reduce_small(pack(pieces))
    flat = red.reshape(-1)
    offs = np.concatenate([[0], np.cumsum(sizes)]).astype(int)
    names = small_names + ["g_final"]
    for i, k in enumerate(names):
        gsum[k] = flat[offs[i]:offs[i + 1]].reshape(w[k].shape)
    loss = flat[offs[len(names)]]

    delta, new_m, new_v = {}, {}, {}
    for k in SHARDED:
        res = adamw_layers(f"adam_{k}", stored(w[k], k), stored(mom[k], k), stored(var[k], k), [reduced[l][k][0] for l in range(L)],
                           [reduced[l][k][1] for l in range(L)], store_kind[k])
        gsum[k], delta[k], new_m[k], new_v[k] = (stored(t, k) for t in res)
    sm_sizes = sizes[:len(names)]
    sm_total = sum(sm_sizes)
    packs = lambda d: jnp.concatenate([d[k].reshape(-1) for k in names] + [jnp.zeros((rows * LANES - sm_total,), F32)]).reshape(rows, LANES)
    gs = jnp.concatenate([flat[:sm_total], jnp.zeros((rows * LANES - sm_total,), F32)]).reshape(rows, LANES)
    d_, m_, v_ = adamw("adam_small", packs(w), gs, packs(mom), packs(var))
    for i, k in enumerate(names):
        sl = slice(offs[i], offs[i + 1])
        delta[k], new_m[k], new_v[k] = (t.reshape(-1)[sl].reshape(w[k].shape) for t in (d_, m_, v_))

    return (loss, dx.reshape(x.shape), *[gsum[k] for k in ORDER], *[delta[k] for k in ORDER],
            *[new_m[k] for k in ORDER], *[new_v[k] for k in ORDER])
```

```python
import functools
import math

import numpy as np
import jax
import jax.numpy as jnp
from jax import lax
from jax.experimental import pallas as pl
from jax.experimental.pallas import tpu as pltpu

F32 = jnp.float32
BF16 = jnp.bfloat16
MESH = pl.DeviceIdType.MESH

NORM_EPS = 1e-6
MLA_NOPE, MLA_ROPE, MLA_V = 128, 64, 128
SSM_GROUP, SSM_STATE = 16, 64
DIL_HEAD = 64
DIL_PATTERNS = ((128, 1), (512, 4), (2048, 16))
ROPE_THETA = 10000.0
ADAM_LR, ADAM_B1, ADAM_B2, ADAM_EPS, ADAM_WD, ADAM_STEP = 0.001, 0.9, 0.999, 1e-08, 0.01, 10
N_CHIPS = 4

LANES = 128
SUBLANES_BF16 = 16
VMEM_LIMIT = 56 * 1024 * 1024
ROW_BUDGET = 20 * 1024 * 1024
MM_BUDGET = 40 * 1024 * 1024
NEG = -1e30


def _cparams(sem=None):
    return pltpu.CompilerParams(dimension_semantics=sem, vmem_limit_bytes=VMEM_LIMIT)


def _pick(n, cap, q, off=0):
    best = None
    for d in range(q, min(n, cap) + 1, q):
        if n % d == 0 and off % d == 0:
            best = d
    if best is None or (best * 4 <= min(cap, n) and n <= 3072 and off % n == 0):
        assert off % n == 0, (n, off)
        return n
    return best


_DOT_DIMS = {"nn": (((1,), (0,)), ((), ())), "nt": (((1,), (1,)), ((), ())), "tn": (((0,), (0,)), ((), ()))}


def _divs(n, q, within=None, off=0):
    return [d for d in range(q, n + 1, q) if n % d == 0 and off % d == 0 and (within is None or within % d == 0)] or [n]


def _mm_tiles(M, N, K, tms, tns, tks, ab, bb, ob):
    best = None
    for tk in tks:
        nk = K // tk
        for tn in tns:
            for tm in tms:
                if 2 * (tm * tk * ab + tk * tn * bb + tm * tn * ob) + tm * tn * 4 * (2 if nk > 1 else 1) > MM_BUDGET:
                    continue
                steps = (M // tm) * (N // tn) * nk
                hbm = M * K * ab * (1 if nk == 1 else N // tn) + K * N * bb * (M // tm) + M * N * ob
                cost = steps * 0.35e-6 + hbm / 3.0e12 + (nk - 1) * M * N * 12 / 4.0e12
                if best is None or cost < best[0]:
                    best = (cost, tm, tn, tk)
    assert best is not None, (M, N, K)
    return best[1:]


def matmul(name, a, b, mode, *, M, N, K, a_off=(0, 0), b_off=(0, 0), b_lead=None, w=None, add=None, out_dtype=F32, into=None):
    tn_mode = mode == "tn"
    a_ro, a_co = (a_off[1], a_off[0]) if tn_mode else a_off
    b_no, b_ko = b_off if mode == "nt" else (b_off[1], b_off[0])
    n_within = k_within = m_within = None
    if w is not None:
        kind, shard = w
        if kind == "cols":
            b = b.reshape(N_CHIPS, b.shape[1] * b.shape[2], b.shape[3])
        rows_within, cols_within = (K if mode == "nn" else N, shard) if kind == "cols" else (shard, (N if mode == "nn" else K) // 2)
        k_within, n_within = (rows_within, cols_within) if mode == "nn" else (cols_within, rows_within)
    if into is not None:
        m_within, n_within = (M // 2, into[1]) if into[0] == "cols" else (into[1], N // 2)
    tms = [d for d in _divs(M, 128 if tn_mode else SUBLANES_BF16, m_within, a_ro) if d <= 1408]
    tns = [d for d in _divs(N, LANES, n_within, b_no) if d <= 2048]
    tks = _divs(K, SUBLANES_BF16 if tn_mode else LANES, k_within, math.gcd(a_co, b_ko))
    ob = jnp.dtype(out_dtype).itemsize + (add.dtype.itemsize if add is not None else 0)
    tm, tn, tk = _mm_tiles(M, N, K, tms, tns, tks, a.dtype.itemsize, b.dtype.itemsize, ob)
    nk = K // tk
    dn = _DOT_DIMS[mode]

    if tn_mode:
        a_spec = pl.BlockSpec((tk, tm), lambda i, j, k: (k + a_co // tk, i + a_ro // tm))
    else:
        a_spec = pl.BlockSpec((tm, tk), lambda i, j, k: (i + a_ro // tm, k + a_co // tk))
    b_blk = (tn, tk) if mode == "nt" else (tk, tn)
    if w is not None:
        tr_, tc_ = (tk, tn) if mode == "nn" else (tn, tk)
        rper, cper = rows_within // tr_, cols_within // tc_

        def wmap(rb, cb):
            if kind == "cols":
                return (cb // cper, rb, cb % cper)
            return (rb // rper, cb // cper, rb % rper, cb % cper)

        imap = (lambda i, j, k: wmap(k, j)) if mode == "nn" else (lambda i, j, k: wmap(j, k))
        b_spec = pl.BlockSpec((None,) * (b.ndim - 2) + b_blk, imap)
    else:
        if mode == "nt":
            imap = lambda i, j, k: (j + b_no // tn, k + b_ko // tk)
        else:
            imap = lambda i, j, k: (k + b_ko // tk, j + b_no // tn)
        if b_lead is None:
            b_spec = pl.BlockSpec(b_blk, imap)
        else:
            b_spec = pl.BlockSpec((None,) + b_blk, lambda i, j, k: (b_lead,) + imap(i, j, k))
    o_plain = pl.BlockSpec((tm, tn), lambda i, j, k: (i, j))
    if into is None:
        o_spec, out_shape = o_plain, jax.ShapeDtypeStruct((M, N), out_dtype)
    else:
        rper, cper = m_within // tm, n_within // tn
        if into[0] == "cols":
            o_spec = pl.BlockSpec((None, None, tm, tn), lambda i, j, k: (i // rper, j // cper, i % rper, j % cper))
        else:
            o_spec = pl.BlockSpec((None, None, tm, tn), lambda i, j, k: (j // cper, i // rper, i % rper, j % cper))
        out_shape = jax.ShapeDtypeStruct((2, N_CHIPS, m_within, n_within), out_dtype)
    has_add = add is not None
    n_in = 2 + has_add

    def body(*refs):
        a_ref, b_ref = refs[0], refs[1]
        add_ref = refs[2] if has_add else None
        o_ref = refs[n_in]
        part = lax.dot_general(a_ref[...].astype(BF16), b_ref[...].astype(BF16), dn, preferred_element_type=F32)

        def finish(r):
            if has_add:
                r = r + add_ref[...].astype(F32)
            o_ref[...] = r.astype(o_ref.dtype)

        if nk == 1:
            finish(part)
        else:
            acc_ref = refs[-1]
            k = pl.program_id(2)

            @pl.when(k == 0)
            def _():
                acc_ref[...] = part

            @pl.when((k > 0) & (k < nk - 1))
            def _():
                acc_ref[...] += part

            @pl.when(k == nk - 1)
            def _():
                finish(acc_ref[...] + part)

    in_specs = [a_spec, b_spec] + ([o_plain] if has_add else [])
    args = (a, b) + ((add,) if has_add else ())
    return pl.pallas_call(
        body, name=name, out_shape=out_shape, grid=(M // tm, N // tn, nk), in_specs=in_specs, out_specs=o_spec,
        scratch_shapes=[pltpu.VMEM((tm, tn), F32)] if nk > 1 else [],
        compiler_params=_cparams(("parallel", "parallel", "arbitrary")),
    )(*args)


def selection_matrices(src_of_new, n_shard):
    src_np = np.asarray(src_of_new, np.int64)
    src = jnp.asarray(src_np.astype(np.int32))
    ref = jnp.arange(N_CHIPS, dtype=jnp.int32)[:, None] * n_shard + jnp.arange(n_shard, dtype=jnp.int32)[None, :]
    pm = (ref[:, :, None] == src[None, None, :]).astype(BF16)
    pmt = (src[None, :, None] == ref[:, None, :]).astype(BF16)
    tc = _pick(len(src_np), 512, LANES)
    feeds = [sorted({int(s) // n_shard for s in src_np[cb * tc:(cb + 1) * tc] if s >= 0}) for cb in range(len(src_np) // tc)]
    return pm, pmt, tc, feeds


def regroup_cols(name, g, sel, transposed=False):
    pm, _, tc, feeds = sel
    n_new = pm.shape[-1]
    if transposed:
        nn, kh = g.shape[2:]
        K = 2 * kh
        tm = _pick(kh, 512, LANES)
        hb = kh // tm
        g_spec = pl.BlockSpec((N_CHIPS, None, nn, tm), lambda c, i: (0, i // hb, 0, i % hb))
    else:
        nn = g.shape[-1]
        g = g.reshape(N_CHIPS, -1, nn)
        K = g.shape[1]
        tm = _pick(K, 512, SUBLANES_BF16)
        g_spec = pl.BlockSpec((N_CHIPS, tm, nn), lambda c, i: (0, i, 0))

    def body(g_ref, pm_ref, o_ref):
        for cb, chips in enumerate(feeds):
            @pl.when(pl.program_id(0) == cb)
            def _(chips=chips):
                acc = jnp.zeros((tm, tc), F32)
                for j in chips:
                    acc = acc + _dot(g_ref[j], pm_ref[j], "tn" if transposed else "nn")
                o_ref[...] = acc.astype(o_ref.dtype)

    return pl.pallas_call(
        body, name=name, out_shape=jax.ShapeDtypeStruct((K, n_new), BF16), grid=(n_new // tc, K // tm),
        in_specs=[g_spec, pl.BlockSpec((N_CHIPS, nn, tc), lambda c, i: (0, 0, c))],
        out_specs=pl.BlockSpec((tm, tc), lambda c, i: (i, c)), compiler_params=_cparams(("parallel", "parallel")),
    )(g, pm)


def ungroup_cols(name, dw, sel, transposed=False):
    pm, pmt, tc, feeds = sel
    K, n_new = dw.shape
    nn = pmt.shape[-1]
    kh = K // 2
    tm = _pick(kh, 512, LANES if transposed else SUBLANES_BF16)
    hb = kh // tm
    fed_by = [[cb for cb, chips in enumerate(feeds) if j in chips] for j in range(N_CHIPS)]

    def body(dw_ref, sel_ref, o_ref):
        for j, blocks in enumerate(fed_by):
            @pl.when(pl.program_id(0) == j)
            def _(blocks=blocks):
                acc = jnp.zeros((nn, tm) if transposed else (tm, nn), F32)
                for cb in blocks:
                    cols = slice(cb * tc, (cb + 1) * tc)
                    if transposed:
                        acc = acc + _dot(sel_ref[:, cols], dw_ref[:, cols], "nt")
                    else:
                        acc = acc + _dot(dw_ref[:, cols], sel_ref[cols, :], "nn")
                o_ref[...] = acc.astype(o_ref.dtype)

    if transposed:
        sel_arr, sel_spec = pm, pl.BlockSpec((None, nn, n_new), lambda j, i: (j, 0, 0))
        out_shape = jax.ShapeDtypeStruct((2, N_CHIPS, nn, kh), BF16)
        out_spec = pl.BlockSpec((None, None, nn, tm), lambda j, i: (i // hb, j, 0, i % hb))
    else:
        sel_arr, sel_spec = pmt, pl.BlockSpec((None, n_new, nn), lambda j, i: (j, 0, 0))
        out_shape = jax.ShapeDtypeStruct((2, N_CHIPS, kh, nn), BF16)
        out_spec = pl.BlockSpec((None, None, tm, nn), lambda j, i: (i // hb, j, i % hb, 0))
    return pl.pallas_call(
        body, name=name, out_shape=out_shape, grid=(N_CHIPS, 2 * hb),
        in_specs=[pl.BlockSpec((tm, n_new), lambda j, i: (i, 0)), sel_spec], out_specs=out_spec,
        compiler_params=_cparams(("parallel", "parallel")),
    )(dw, sel_arr)


def rowwise(name, fn, rows, vecs, outs, sums=(), *, M):
    rows = [tuple(r) + (0,) * (4 - len(r)) for r in rows]
    nr, nv, no, ns = len(rows), len(vecs), len(outs), len(sums)
    per_row = sum(w * a.dtype.itemsize for a, w, _, _ in rows) + sum(w * jnp.dtype(d).itemsize for w, d in outs)
    tr = _pick(M, max(8, min(512, ROW_BUDGET // (2 * per_row))), 16 if M % 16 == 0 else 8)

    def body(*refs):
        i = pl.program_id(0)
        res = fn(*[r[...] for r in refs[:nr + nv]])
        o_refs = refs[nr + nv:nr + nv + no]
        s_refs = refs[nr + nv + no:]
        for ref, val in zip(o_refs, res[:no]):
            ref[...] = val.astype(ref.dtype)
        if ns:
            @pl.when(i == 0)
            def _():
                for ref in s_refs:
                    ref[...] = jnp.zeros(ref.shape, F32)

            for ref, val in zip(s_refs, res[no:]):
                ref[...] += val

    in_specs = [pl.BlockSpec((tr, w), functools.partial(lambda i, cb, rb: (i + rb, cb), cb=off // w, rb=roff // tr)) for _, w, off, roff in rows]
    for _, w, off, roff in rows:
        assert off % w == 0 and roff % tr == 0
    in_specs += [pl.BlockSpec(v.shape, functools.partial(lambda i, nd: (0,) * nd, nd=v.ndim)) for v in vecs]
    out_specs = [pl.BlockSpec((tr, w), lambda i: (i, 0)) for w, _ in outs]
    out_specs += [pl.BlockSpec((1, w), lambda i: (0, 0)) for w in sums]
    out_shape = [jax.ShapeDtypeStruct((M, w), d) for w, d in outs] + [jax.ShapeDtypeStruct((1, w), F32) for w in sums]
    return pl.pallas_call(
        body, name=name, out_shape=out_shape, grid=(M // tr,), in_specs=in_specs, out_specs=out_specs,
        compiler_params=_cparams(("arbitrary",) if ns else ("parallel",)),
    )(*[r[0] for r in rows], *vecs)


def _rms(x, g):
    xf = x.astype(F32)
    return xf * lax.rsqrt(jnp.mean(xf * xf, axis=-1, keepdims=True) + NORM_EPS) * g


def _gelu(y):
    return 0.5 * y * (1.0 + jnp.tanh(math.sqrt(2.0 / math.pi) * (y + 0.044715 * (y * y * y))))


def _colsum(v):
    return jnp.sum(v, axis=0, keepdims=True)


def rms_fwd(name, x, width, off, g, *, M):
    return rowwise(name, lambda xb, gb: (_rms(xb, gb),), [(x, width, off)], [g], [(width, BF16)], M=M)[0]


def rms_bwd(name, x, width, off, g, dy, resid=None, *, M, out_dtype=F32):
    def fn(xb, dyb, *rest):
        gb = rest[-1]
        _, vjp = jax.vjp(_rms, xb.astype(F32), gb)
        dx, dg = vjp(dyb.astype(F32))
        if resid is not None:
            dx = dx + rest[0]
        return dx, dg

    rows = [(x, width, off), (dy, width, 0)] + ([(resid, width, 0)] if resid is not None else [])
    return rowwise(name, fn, rows, [g], [(width, out_dtype)], [width], M=M)


def lane_concat(name, parts, *, M, pad_to=None):
    width = sum(p.shape[1] for p in parts)
    pad = 0 if pad_to is None else pad_to - width

    def fn(*blocks):
        cols = [b.astype(BF16) for b in blocks]
        if pad:
            cols.append(jnp.zeros((blocks[0].shape[0], pad), BF16))
        return (jnp.concatenate(cols, axis=1),)

    return rowwise(name, fn, [(p, p.shape[1], 0) for p in parts], [], [(width + pad, BF16)], M=M)[0]


def _swap_halves(x, half):
    w = x.shape[-1]
    lane = lax.broadcasted_iota(jnp.int32, x.shape, x.ndim - 1)
    first = (lane % (2 * half)) < half
    return jnp.where(first, pltpu.roll(x, w - half, x.ndim - 1), pltpu.roll(x, half, x.ndim - 1))


def dilated_bias(M):
    delta = jnp.arange(M, dtype=jnp.int32)[:, None] - jnp.arange(M, dtype=jnp.int32)[None, :]
    w = jnp.zeros(delta.shape, F32)
    for window, dil in DIL_PATTERNS:
        ok = (delta >= 0) & (delta <= window)
        if dil > 1:
            ok = ok & ((delta & (dil - 1)) == 0)
        w = w + ok.astype(F32)
    return jnp.where(w > 0, jnp.log(jnp.maximum(w, 1.0)), NEG)


def _dot(a, b, mode):
    return lax.dot_general(a, b, _DOT_DIMS[mode], preferred_element_type=F32)


def attention_fwd(name, qa, qa_off, ka, ka_off, v, v_off, *, da, dv, pairs, scale, M, qb=None, qb_off=0, kb=None, bias=None):
    tq = min(512, M)
    tk = min(1024, M)
    has_b = qb is not None
    has_bias = bias is not None
    dr = MLA_ROPE

    def body(*refs):
        refs = list(refs)
        bias_ref = refs.pop(3) if has_bias else None
        if has_b:
            qa_ref, ka_ref, v_ref, qb_ref, kb_ref, o_ref, lse_ref = refs
        else:
            qa_ref, ka_ref, v_ref, o_ref, lse_ref = refs
        i = pl.program_id(1)
        t0 = i * tq
        nkb = (t0 + tq + tk - 1) // tk
        n_full = nkb if has_bias else t0 // tk
        q1s = [qa_ref[:, hh * da:(hh + 1) * da].astype(BF16) for hh in range(2)]
        q2s = [qb_ref[:, hh * dr:(hh + 1) * dr].astype(BF16) if has_b else None for hh in range(2)]

        def step(kbi, carry, masked):
            ks = pl.multiple_of(kbi * tk, tk)
            k2 = kb_ref[pl.ds(ks, tk), 0:dr].astype(BF16) if has_b else None
            if has_bias:
                extra = bias_ref[:, pl.ds(ks, tk)]
            elif masked:
                delta = (t0 + lax.broadcasted_iota(jnp.int32, (tq, tk), 0)) - (ks + lax.broadcasted_iota(jnp.int32, (tq, tk), 1))
            new = []
            for hh, (m, l, acc) in enumerate(carry):
                k1 = ka_ref[pl.ds(ks, tk), hh * da:(hh + 1) * da].astype(BF16)
                s = _dot(q1s[hh], k1, "nt")
                if has_b:
                    s = s + _dot(q2s[hh], k2, "nt")
                s = s * scale
                if has_bias:
                    s = s + extra
                elif masked:
                    s = jnp.where(delta >= 0, s, NEG)
                m_new = jnp.maximum(m, jnp.max(s, axis=1, keepdims=True))
                alpha = jnp.exp(m - m_new)
                p = jnp.exp(s - m_new)
                l = alpha * l + jnp.sum(p, axis=1, keepdims=True)
                vv = v_ref[pl.ds(ks, tk), hh * dv:(hh + 1) * dv].astype(BF16)
                acc = alpha * acc + _dot(p.astype(BF16), vv, "nn")
                new.append((m_new, l, acc))
            return tuple(new)

        carry = tuple((jnp.full((tq, 1), NEG, F32), jnp.zeros((tq, 1), F32), jnp.zeros((tq, dv), F32)) for _ in range(2))
        carry = lax.fori_loop(0, n_full, functools.partial(step, masked=False), carry)
        carry = lax.fori_loop(n_full, nkb, functools.partial(step, masked=True), carry)
        o_parts = [acc / l for _, l, acc in carry]
        lse_parts = [m + jnp.log(l) for m, l, _ in carry]
        o_ref[...] = jnp.concatenate(o_parts, axis=1)
        lane = lax.broadcasted_iota(jnp.int32, (tq, LANES), 1)
        lse_ref[...] = jnp.where(lane == 0, lse_parts[0], jnp.where(lane == 1, lse_parts[1], 0.0))

    assert qa_off % (2 * da) == 0 and ka_off % (2 * da) == 0 and v_off % (2 * dv) == 0
    in_specs = [
        pl.BlockSpec((tq, 2 * da), lambda hp, i: (i, qa_off // (2 * da) + hp)),
        pl.BlockSpec((M, 2 * da), lambda hp, i: (0, ka_off // (2 * da) + hp)),
        pl.BlockSpec((M, 2 * dv), lambda hp, i: (0, v_off // (2 * dv) + hp)),
    ]
    args = [qa, ka, v]
    if has_bias:
        in_specs.append(pl.BlockSpec((tq, M), lambda hp, i: (i, 0)))
        args.append(bias)
    if has_b:
        assert qb_off % LANES == 0
        in_specs += [pl.BlockSpec((tq, LANES), lambda hp, i: (i, qb_off // LANES + hp)),
                     pl.BlockSpec((M, LANES), lambda hp, i: (0, 0))]
        args += [qb, kb]
    out_specs = [pl.BlockSpec((tq, 2 * dv), lambda hp, i: (i, hp)),
                 pl.BlockSpec((None, tq, LANES), lambda hp, i: (hp, i, 0))]
    out_shape = [jax.ShapeDtypeStruct((M, pairs * 2 * dv), F32), jax.ShapeDtypeStruct((pairs, M, LANES), F32)]
    return pl.pallas_call(
        body, name=name, out_shape=out_shape, grid=(pairs, M // tq), in_specs=in_specs, out_specs=out_specs,
        compiler_params=_cparams(("parallel", "arbitrary")),
    )(*args)


def attention_bwd(name, qa, qa_off, ka, ka_off, v, v_off, o, do, lse, *, da, dv, pairs, scale, M,
                  qb=None, qb_off=0, kb=None, bias=None, tk_cap=512):
    tq = min(512, M)
    tk = min(tk_cap, M)
    has_b = qb is not None
    has_bias = bias is not None
    dr = MLA_ROPE

    def body(*refs):
        refs = list(refs)
        bias_ref = refs.pop(6) if has_bias else None
        if has_b:
            qa_ref, ka_ref, v_ref, o_ref, do_ref, lse_ref, qb_ref, kb_ref, dqa_ref, dka_ref, dv_ref, dqb_ref, dkb_ref = refs
        else:
            qa_ref, ka_ref, v_ref, o_ref, do_ref, lse_ref, dqa_ref, dka_ref, dv_ref = refs
        hp = pl.program_id(0)
        i = pl.program_id(1)
        t0 = i * tq
        nkb = (t0 + tq + tk - 1) // tk
        n_full = nkb if has_bias else t0 // tk

        @pl.when(i == 0)
        def _():
            dka_ref[...] = jnp.zeros(dka_ref.shape, F32)
            dv_ref[...] = jnp.zeros(dv_ref.shape, F32)

        if has_b:
            @pl.when((i == 0) & (hp == 0))
            def _():
                dkb_ref[...] = jnp.zeros(dkb_ref.shape, F32)

        q1s = [qa_ref[:, hh * da:(hh + 1) * da].astype(BF16) for hh in range(2)]
        q2s = [qb_ref[:, hh * dr:(hh + 1) * dr].astype(BF16) if has_b else None for hh in range(2)]
        do_bfs = [do_ref[:, hh * dv:(hh + 1) * dv].astype(BF16) for hh in range(2)]
        rowdots = [jnp.sum(do_ref[:, hh * dv:(hh + 1) * dv] * o_ref[:, hh * dv:(hh + 1) * dv], axis=1, keepdims=True) for hh in range(2)]
        lses = [lse_ref[:, hh:hh + 1] for hh in range(2)]

        def step(kbi, carry, masked):
            ks = pl.multiple_of(kbi * tk, tk)
            k2 = kb_ref[pl.ds(ks, tk), 0:dr].astype(BF16) if has_b else None
            if has_bias:
                extra = bias_ref[:, pl.ds(ks, tk)]
            elif masked:
                delta = (t0 + lax.broadcasted_iota(jnp.int32, (tq, tk), 0)) - (ks + lax.broadcasted_iota(jnp.int32, (tq, tk), 1))
            new, dkb_part = [], None
            for hh, (dq1, dq2) in enumerate(carry):
                k1 = ka_ref[pl.ds(ks, tk), hh * da:(hh + 1) * da].astype(BF16)
                s = _dot(q1s[hh], k1, "nt")
                if has_b:
                    s = s + _dot(q2s[hh], k2, "nt")
                s = s * scale
                if has_bias:
                    s = s + extra
                elif masked:
                    s = jnp.where(delta >= 0, s, NEG)
                p = jnp.exp(s - lses[hh])
                vv = v_ref[pl.ds(ks, tk), hh * dv:(hh + 1) * dv].astype(BF16)
                dp = _dot(do_bfs[hh], vv, "nt")
                ds = (p * (dp - rowdots[hh]) * scale).astype(BF16)
                dq1 = dq1 + _dot(ds, k1, "nn")
                dka_ref[pl.ds(ks, tk), hh * da:(hh + 1) * da] += _dot(ds, q1s[hh], "tn")
                dv_ref[pl.ds(ks, tk), hh * dv:(hh + 1) * dv] += _dot(p.astype(BF16), do_bfs[hh], "tn")
                if has_b:
                    dq2 = dq2 + _dot(ds, k2, "nn")
                    part = _dot(ds, q2s[hh], "tn")
                    dkb_part = part if dkb_part is None else dkb_part + part
                new.append((dq1, dq2))
            if has_b:
                dkb_ref[pl.ds(ks, tk), 0:dr] += dkb_part
            return tuple(new)

        carry = tuple((jnp.zeros((tq, da), F32), jnp.zeros((tq, dr), F32)) for _ in range(2))
        carry = lax.fori_loop(0, n_full, functools.partial(step, masked=False), carry)
        carry = lax.fori_loop(n_full, nkb, functools.partial(step, masked=True), carry)
        dqa_ref[...] = jnp.concatenate([c[0] for c in carry], axis=1).astype(dqa_ref.dtype)
        if has_b:
            dqb_ref[...] = jnp.concatenate([c[1] for c in carry], axis=1).astype(dqb_ref.dtype)

    in_specs = [
        pl.BlockSpec((tq, 2 * da), lambda hp, i: (i, qa_off // (2 * da) + hp)),
        pl.BlockSpec((M, 2 * da), lambda hp, i: (0, ka_off // (2 * da) + hp)),
        pl.BlockSpec((M, 2 * dv), lambda hp, i: (0, v_off // (2 * dv) + hp)),
        pl.BlockSpec((tq, 2 * dv), lambda hp, i: (i, hp)),
        pl.BlockSpec((tq, 2 * dv), lambda hp, i: (i, hp)),
        pl.BlockSpec((None, tq, LANES), lambda hp, i: (hp, i, 0)),
    ]
    args = [qa, ka, v, o, do, lse]
    if has_bias:
        in_specs.append(pl.BlockSpec((tq, M), lambda hp, i: (i, 0)))
        args.append(bias)
    out_specs = [pl.BlockSpec((tq, 2 * da), lambda hp, i: (i, hp)),
                 pl.BlockSpec((M, 2 * da), lambda hp, i: (0, hp)),
                 pl.BlockSpec((M, 2 * dv), lambda hp, i: (0, hp))]
    out_shape = [jax.ShapeDtypeStruct((M, pairs * 2 * da), BF16),
                 jax.ShapeDtypeStruct((M, pairs * 2 * da), F32),
                 jax.ShapeDtypeStruct((M, pairs * 2 * dv), F32)]
    if has_b:
        in_specs += [pl.BlockSpec((tq, LANES), lambda hp, i: (i, qb_off // LANES + hp)),
                     pl.BlockSpec((M, LANES), lambda hp, i: (0, 0))]
        args += [qb, kb]
        out_specs += [pl.BlockSpec((tq, LANES), lambda hp, i: (i, hp)), pl.BlockSpec((M, LANES), lambda hp, i: (0, 0))]
        out_shape += [jax.ShapeDtypeStruct((M, pairs * LANES), F32), jax.ShapeDtypeStruct((M, LANES), F32)]
    return pl.pallas_call(
        body, name=name, out_shape=out_shape, grid=(pairs, M // tq), in_specs=in_specs, out_specs=out_specs,
        compiler_params=_cparams(("arbitrary", "arbitrary")),
    )(*args)


def ssm_scan(name, xcat, acat, *, M, reverse=False, hcat=None):
    C2 = xcat.shape[1]
    cb = LANES
    tb = min(128, M)
    nblk = M // tb
    with_da = hcat is not None
    nsub = 1
    wide = nsub * 2 * cb

    def body(*refs):
        if with_da:
            x_ref, a_ref, h_ref, o_ref, da_ref, p_ref = refs
        else:
            x_ref, a_ref, o_ref, p_ref = refs
        re = [slice(s * 2 * cb, s * 2 * cb + cb) for s in range(nsub)]
        im = [slice(s * 2 * cb + cb, (s + 1) * 2 * cb) for s in range(nsub)]
        ars, ais = [a_ref[:, c] for c in re], [a_ref[:, c] for c in im]
        row = lax.broadcasted_iota(jnp.int32, (tb, cb), 0)

        def logscan(xs):
            ps = list(zip(ars, ais))
            d = 1
            while d < tb:
                shift = tb - d if reverse else d
                keep = (row < tb - d) if reverse else (row >= d)
                nxt = []
                for (xr, xi), (pr, pi) in zip(xs, ps):
                    sr = jnp.where(keep, pltpu.roll(xr, shift, 0), 0.0)
                    si = jnp.where(keep, pltpu.roll(xi, shift, 0), 0.0)
                    nxt.append((xr + pr * sr - pi * si, xi + pr * si + pi * sr))
                xs = nxt
                ps = [(pr * pr - pi * pi, 2.0 * pr * pi) for pr, pi in ps]
                d *= 2
            return xs

        seed = row == (tb - 1 if reverse else 0)
        for s, (p0r, p0i) in enumerate(logscan([(jnp.where(seed, ar, 0.0), jnp.where(seed, ai, 0.0)) for ar, ai in zip(ars, ais)])):
            p_ref[:, re[s]] = p0r
            p_ref[:, im[s]] = p0i
        sub = lax.broadcasted_iota(jnp.int32, (8, cb), 0)
        edge = 0 if reverse else tb - 8
        pick = sub == (0 if reverse else 7)

        def blk(b, carry):
            bb = (nblk - 1 - b) if reverse else b
            t0 = pl.multiple_of(bb * tb, tb)
            te = pl.multiple_of(t0 + edge, 8)
            hs = logscan([(x_ref[pl.ds(t0, tb), re[s]], x_ref[pl.ds(t0, tb), im[s]]) for s in range(nsub)])
            new = []
            for s, ((hr, hi), (cr, ci)) in enumerate(zip(hs, carry)):
                pr, pi = p_ref[:, re[s]], p_ref[:, im[s]]
                o_ref[pl.ds(t0, tb), re[s]] = hr + pr * cr - pi * ci
                o_ref[pl.ds(t0, tb), im[s]] = hi + pr * ci + pi * cr
                new.append((jnp.sum(jnp.where(pick, o_ref[pl.ds(te, 8), re[s]], 0.0), axis=0, keepdims=True),
                            jnp.sum(jnp.where(pick, o_ref[pl.ds(te, 8), im[s]], 0.0), axis=0, keepdims=True)))
            return tuple(new)

        lax.fori_loop(0, nblk, blk, tuple((jnp.zeros((1, cb), F32), jnp.zeros((1, cb), F32)) for _ in range(nsub)))
        if with_da:
            first = lax.broadcasted_iota(jnp.int32, (M, cb), 0) >= 1
            for s in range(nsub):
                hpr = jnp.where(first, pltpu.roll(h_ref[:, re[s]], 1, 0), 0.0)
                hpi = jnp.where(first, pltpu.roll(h_ref[:, im[s]], 1, 0), 0.0)
                lr, li = o_ref[:, re[s]], o_ref[:, im[s]]
                da_ref[:, re[s]] = _colsum(lr * hpr + li * hpi)
                da_ref[:, im[s]] = _colsum(li * hpr - lr * hpi)

    blk_spec = pl.BlockSpec((M, wide), lambda j: (0, j))
    vec_spec = pl.BlockSpec((1, wide), lambda j: (0, j))
    in_specs = [blk_spec, vec_spec] + ([blk_spec] if with_da else [])
    out_specs = [blk_spec] + ([vec_spec] if with_da else [])
    out_shape = [jax.ShapeDtypeStruct((M, C2), F32)] + ([jax.ShapeDtypeStruct((1, C2), F32)] if with_da else [])
    args = [xcat, acat] + ([hcat] if with_da else [])
    res = pl.pallas_call(
        body, name=name, out_shape=out_shape, grid=(C2 // wide,), in_specs=in_specs, out_specs=out_specs,
        scratch_shapes=[pltpu.VMEM((tb, wide), F32)], compiler_params=_cparams(("parallel",)),
    )(*args)
    return res if with_da else res[0]


def ssm_diag_rows(name, m):
    assert LANES == 2 * SSM_STATE
    gp, c2 = m.shape
    groups = gp // SSM_GROUP
    st = SSM_STATE

    def body(m_ref, re_ref, im_ref):
        x = m_ref[...]
        odd = pl.program_id(0) % 2

        @pl.when(odd == 0)
        def _():
            re_ref[:, 0:st] = x[:, 0:st]
            im_ref[:, 0:st] = x[:, LANES:LANES + st]

        @pl.when(odd == 1)
        def _():
            re_ref[:, st:LANES] = x[:, st:LANES]
            im_ref[:, st:LANES] = x[:, LANES + st:2 * LANES]

    out = jax.ShapeDtypeStruct((SSM_GROUP, c2 // 2), F32)
    o_spec = pl.BlockSpec((SSM_GROUP, LANES), lambda g: (0, g // 2))
    return pl.pallas_call(
        body, name=name, out_shape=[out, out], grid=(groups,), in_specs=[pl.BlockSpec((SSM_GROUP, 2 * LANES), lambda g: (g, g // 2))],
        out_specs=[o_spec, o_spec], compiler_params=_cparams(("arbitrary",)),
    )(m)


def _ssm_param_fn(a_re, a_im, ldt, b_re, b_im):
    lr, li = jnp.minimum(a_re, -1e-4), a_im
    dt = jnp.exp(ldt)
    e, ang = jnp.exp(lr * dt), li * dt
    ar, ai = e * jnp.cos(ang), e * jnp.sin(ang)
    den = lr * lr + li * li
    nr, ni = ar - 1.0, ai
    cr, ci = (nr * lr + ni * li) / den, (ni * lr - nr * li) / den
    return ar, ai, cr * b_re - ci * b_im, cr * b_im + ci * b_re


def _whole(shape):
    return pl.BlockSpec(shape, functools.partial(lambda nd: (0,) * nd, nd=len(shape)))


def ssm_param_fwd(name, a_re, a_im, ldt, b_re, b_im):
    def body(*refs):
        res = _ssm_param_fn(*[r[...] for r in refs[:5]])
        for ref, val in zip(refs[5:], res):
            ref[...] = val

    ins = [a_re, a_im, ldt, b_re, b_im]
    outs = [a_re, a_re, b_re, b_re]
    return pl.pallas_call(
        body, name=name, out_shape=[jax.ShapeDtypeStruct(t.shape, F32) for t in outs],
        in_specs=[_whole(t.shape) for t in ins], out_specs=[_whole(t.shape) for t in outs], compiler_params=_cparams(),
    )(*ins)


def ssm_param_bwd(name, a_re, a_im, ldt, b_re, b_im, d_ar, d_ai, d_bbr, d_bbi):
    def body(*refs):
        _, vjp = jax.vjp(_ssm_param_fn, *[r[...] for r in refs[:5]])
        res = vjp(tuple(r[...] for r in refs[5:9]))
        for ref, val in zip(refs[9:], res):
            ref[...] = val

    ins = [a_re, a_im, ldt, b_re, b_im, d_ar, d_ai, d_bbr, d_bbi]
    outs = [a_re, a_im, ldt, b_re, b_im]
    return pl.pallas_call(
        body, name=name, out_shape=[jax.ShapeDtypeStruct(t.shape, F32) for t in outs],
        in_specs=[_whole(t.shape) for t in ins], out_specs=[_whole(t.shape) for t in outs], compiler_params=_cparams(),
    )(*ins)


ANY = pl.BlockSpec(memory_space=pl.ANY)


def _place():
    x, y, c = lax.axis_index("x"), lax.axis_index("y"), lax.axis_index("c")
    chips = [(1 - x, y), (x, 1 - y), (1 - x, 1 - y)]
    return x, y, c, chips


def cast_into_slot(name, w, kind, l, after):
    K, nn = w.shape[-2:]
    hr, hc = (K // 2, nn) if kind == "cols" else (K, nn // 2)
    tr = _pick(hr, max(16, min(512, ROW_BUDGET // (2 * hc * 6))), SUBLANES_BF16)
    nb = hr // tr

    def body(w_ref, after_ref, o_ref, after_out):
        o_ref[...] = w_ref[...].astype(BF16)

    if kind == "cols":
        in_spec = pl.BlockSpec((None, tr, hc), lambda h, i: (l, h * nb + i, 0))
    else:
        in_spec = pl.BlockSpec((None, tr, hc), lambda h, i: (l, i, h))
    return pl.pallas_call(
        body, name=name, out_shape=[jax.ShapeDtypeStruct((N_CHIPS, 2, hr, hc), BF16), jax.ShapeDtypeStruct(after.shape, after.dtype)],
        grid=(2, nb), in_specs=[in_spec, ANY],
        out_specs=[pl.BlockSpec((None, None, tr, hc), lambda h, i: (2 * lax.axis_index("x") + lax.axis_index("y"), h, i, 0)), ANY],
        input_output_aliases={1: 1}, compiler_params=_cparams(("arbitrary", "arbitrary")),
    )(w, after)


HBM_SPEC = pl.BlockSpec(memory_space=pltpu.HBM)
SEM_SPEC = pl.BlockSpec(memory_space=pltpu.SEMAPHORE)
SPLIT_PARAMS = pltpu.CompilerParams(has_side_effects=pltpu.SideEffectType.DATAFLOW_SIDE_EFFECTING)


def _in_hbm(t):
    return pltpu.with_memory_space_constraint(t, pltpu.HBM)


def split_start(name, plan, n, bufs, fresh, carrier):
    nb, nf = len(bufs), len(fresh)

    def body(*refs):
        outs = refs[nb + 1:]
        for i, (s, d, dev) in enumerate(plan(list(outs[2:2 + nb + nf]))):
            pltpu.make_async_remote_copy(src_ref=s, dst_ref=d, send_sem=outs[0].at[i], recv_sem=outs[1].at[i],
                                         device_id=dev, device_id_type=MESH).start()

    hbm = lambda t: pltpu.HBM(t.shape, t.dtype)
    res = pl.pallas_call(
        body, name=name,
        out_shape=(pltpu.SemaphoreType.DMA((n,)), pltpu.SemaphoreType.DMA((n,)), *[hbm(t) for t in bufs], *[hbm(t) for t in fresh], hbm(carrier)),
        in_specs=[HBM_SPEC] * (nb + 1), out_specs=(SEM_SPEC, SEM_SPEC) + (HBM_SPEC,) * (nb + nf + 1),
        input_output_aliases={**{i: 2 + i for i in range(nb)}, nb: 2 + nb + nf}, compiler_params=SPLIT_PARAMS,
    )(*[_in_hbm(t) for t in bufs], _in_hbm(carrier))
    return (res[0], res[1]), list(res[2:2 + nb]), list(res[2 + nb:2 + nb + nf]), res[2 + nb + nf]


def split_wait(name, plan, sems, bufs, carrier):
    nb = len(bufs)

    def body(*refs):
        for i, (s, d, dev) in enumerate(plan(list(refs[:nb]))):
            cp = pltpu.make_async_remote_copy(src_ref=s, dst_ref=d, send_sem=refs[nb].at[i], recv_sem=refs[nb + 1].at[i],
                                              device_id=dev, device_id_type=MESH)
            cp.wait_send()
            cp.wait_recv()

    hbm = lambda t: pltpu.HBM(t.shape, t.dtype)
    res = pl.pallas_call(
        body, name=name, out_shape=(*[hbm(t) for t in bufs], hbm(carrier)),
        in_specs=[HBM_SPEC] * nb + [SEM_SPEC, SEM_SPEC, HBM_SPEC], out_specs=(HBM_SPEC,) * (nb + 1),
        input_output_aliases={**{i: i for i in range(nb)}, nb + 2: nb}, compiler_params=SPLIT_PARAMS,
    )(*bufs, sems[0], sems[1], carrier)
    return list(res[:nb]), res[nb]


def _me_sib_chips():
    x, y, c, chips = _place()
    return 2 * x + y, c, (x, y, 1 - c), chips


def plan_gather_ici(refs):
    me, c, _, chips = _me_sib_chips()
    return [(r.at[me, c], r.at[me, c], (chip[0], chip[1], c)) for r in refs for chip in chips]


def plan_gather_pass(refs):
    _, c, sib, chips = _me_sib_chips()
    return [(r.at[2 * chip[0] + chip[1], c], r.at[2 * chip[0] + chip[1], c], sib) for r in refs for chip in chips]


def plan_pair(n_arrays):
    def plan(refs):
        _, c, sib, _ = _me_sib_chips()
        return [(refs[a].at[1 - c], refs[n_arrays + a], sib) for a in range(n_arrays)]
    return plan


def plan_chips(n_arrays):
    def plan(refs):
        _, c, _, chips = _me_sib_chips()
        return [(refs[a].at[2 * chip[0] + chip[1]], refs[n_arrays + a].at[k], (chip[0], chip[1], c))
                for a in range(n_arrays) for k, chip in enumerate(chips)]
    return plan


def plan_share(n_arrays):
    def plan(refs):
        _, _, sib, _ = _me_sib_chips()
        return [(refs[a], refs[n_arrays + a], sib) for a in range(n_arrays)]
    return plan


def swap_with_sibling(name, src, pick_other_half):
    shape = src.shape[1:] if pick_other_half else src.shape

    def body(src_ref, out_ref, ssem, rsem):
        x, y, c, _ = _place()
        cp = pltpu.make_async_remote_copy(src_ref=src_ref.at[1 - c] if pick_other_half else src_ref, dst_ref=out_ref,
                                          send_sem=ssem, recv_sem=rsem, device_id=(x, y, 1 - c), device_id_type=MESH)
        cp.start()
        cp.wait()

    return pl.pallas_call(
        body, name=name, out_shape=jax.ShapeDtypeStruct(shape, src.dtype), in_specs=[ANY], out_specs=ANY,
        scratch_shapes=[pltpu.SemaphoreType.DMA(()), pltpu.SemaphoreType.DMA(())],
    )(src)


def exchange_chips(name, src, per_chip):
    shape = src.shape[1:] if per_chip else src.shape

    def body(src_ref, out_ref, send_sems, recv_sems):
        x, y, c, chips = _place()
        cps = []
        for k, chip in enumerate(chips):
            s = src_ref.at[2 * chip[0] + chip[1]] if per_chip else src_ref
            cps.append(pltpu.make_async_remote_copy(src_ref=s, dst_ref=out_ref.at[k], send_sem=send_sems.at[k], recv_sem=recv_sems.at[k],
                                                    device_id=(chip[0], chip[1], c), device_id_type=MESH))
        for cp in cps:
            cp.start()
        for cp in cps:
            cp.wait()

    return pl.pallas_call(
        body, name=name, out_shape=jax.ShapeDtypeStruct((3,) + shape, src.dtype), in_specs=[ANY], out_specs=ANY,
        scratch_shapes=[pltpu.SemaphoreType.DMA((3,)), pltpu.SemaphoreType.DMA((3,))],
    )(src)


def pair_sum(name, p, got):
    _, _, rh, cw = p.shape
    tr = _pick(rh, max(16, min(512, ROW_BUDGET // (2 * cw * 10))), SUBLANES_BF16)

    def body(p_ref, got_ref, s_ref, own_ref):
        j = pl.program_id(1)
        tot = p_ref[...].astype(F32) + got_ref[...].astype(F32)
        s_ref[...] = tot.astype(BF16)

        @pl.when(j == 2 * lax.axis_index("x") + lax.axis_index("y"))
        def _():
            own_ref[...] = tot

    return pl.pallas_call(
        body, name=name, grid=(rh // tr, N_CHIPS),
        in_specs=[pl.BlockSpec((None, None, tr, cw), lambda i, j: (lax.axis_index("c"), j, i, 0)),
                  pl.BlockSpec((None, tr, cw), lambda i, j: (j, i, 0))],
        out_specs=[pl.BlockSpec((None, tr, cw), lambda i, j: (j, i, 0)),
                   pl.BlockSpec((tr, cw), lambda i, j: (i, 0))],
        out_shape=[jax.ShapeDtypeStruct((N_CHIPS, rh, cw), BF16), jax.ShapeDtypeStruct((rh, cw), F32)],
        compiler_params=_cparams(("arbitrary", "arbitrary")),
    )(p, got)


def chips_sum(name, own, parts):
    rh, cw = own.shape
    parts = parts.reshape(3 * rh, cw)
    return rowwise(name, lambda o, a, b, c: (((o + a.astype(F32)) + b.astype(F32)) + c.astype(F32),),
                   [(own, cw, 0), (parts, cw, 0, 0), (parts, cw, 0, rh), (parts, cw, 0, 2 * rh)], [], [(cw, F32)], M=rh)[0]


def all_reduce_small(buf):
    r = buf.shape[0]
    got = swap_with_sibling("ar_pair", buf, False)
    chip = rowwise("ar_pairsum", lambda a, b: (a + b,), [(buf, LANES, 0), (got, LANES, 0)], [], [(LANES, F32)], M=r)[0]
    parts = exchange_chips("ar_chips", chip, False).reshape(3 * r, LANES)
    return rowwise("ar_sum", lambda o, fx, fy, fxy: ((o + fy) + (fx + fxy),),
                   [(chip, LANES, 0), (parts, LANES, 0, 0), (parts, LANES, 0, r), (parts, LANES, 0, 2 * r)], [], [(LANES, F32)], M=r)[0]


def _adam_fn(w, g, m, v):
    m = ADAM_B1 * m + (1.0 - ADAM_B1) * g
    v = ADAM_B2 * v + (1.0 - ADAM_B2) * (g * g)
    m_hat = m / (1.0 - ADAM_B1 ** ADAM_STEP)
    v_hat = v / (1.0 - ADAM_B2 ** ADAM_STEP)
    return -ADAM_LR * (m_hat / (jnp.sqrt(v_hat) + ADAM_EPS) + ADAM_WD * w), m, v


def adamw(name, w, g, m, v):
    r, cw = w.shape
    return rowwise(name, _adam_fn, [(t, cw, 0) for t in (w, g, m, v)], [], [(cw, F32)] * 3, M=r)


def adamw_layers(name, w, m, v, mines, theirs, kind):
    L, K, nn = w.shape
    hr, hc = (K // 2, nn) if kind == "cols" else (K, nn // 2)
    tr = _pick(hr, max(8, min(256, MM_BUDGET // (2 * hc * 4 * (7 + 2 * L)))), 8)
    nb = hr // tr

    def body(*refs):
        w_ref, m_ref, v_ref = refs[:3]
        outs = refs[3 + 2 * L:]
        l, mine_here = pl.program_id(0), pl.program_id(1) == lax.axis_index("c")
        g = jnp.zeros((tr, hc), F32)
        for ll in range(L):
            g = jnp.where(l == ll, jnp.where(mine_here, refs[3 + ll][...], refs[3 + L + ll][...]), g)
        outs[0][...] = g
        outs[1][...], outs[2][...], outs[3][...] = _adam_fn(w_ref[...], g, m_ref[...], v_ref[...])

    if kind == "cols":
        full = pl.BlockSpec((None, tr, hc), lambda l, h, i: (l, h * nb + i, 0))
    else:
        full = pl.BlockSpec((None, tr, hc), lambda l, h, i: (l, i, h))
    def half_spec(ll, mine):
        def imap(l, h, i):
            here = (l == ll) & ((h == lax.axis_index("c")) == mine)
            return (jnp.where(here, i, 0), 0)
        return pl.BlockSpec((tr, hc), imap)

    halves = [half_spec(ll, True) for ll in range(L)] + [half_spec(ll, False) for ll in range(L)]
    return pl.pallas_call(
        body, name=name, grid=(L, 2, nb), in_specs=[full] * 3 + halves, out_specs=[full] * 4,
        out_shape=[jax.ShapeDtypeStruct((L, K, nn), F32)] * 4, compiler_params=_cparams(("parallel", "parallel", "parallel")),
    )(w, m, v, *mines, *theirs)


class Dims:
    def __init__(self, x, g_q, g_kv, g_out_mla, g_out_ssm, g_out_dil, ff):
        self.M, self.D = x.shape[-2], x.shape[-1]
        self.QL, self.KVL = g_q.shape[-1], g_kv.shape[-1]
        self.MW, self.SW, self.DW = g_out_mla.shape[-1], g_out_ssm.shape[-1], g_out_dil.shape[-1]
        self.H = self.MW // MLA_V
        self.FF = ff
        self.G = self.SW // SSM_GROUP
        self.C = self.G * SSM_STATE
        self.o_cq, self.o_u = 0, self.QL
        self.o_qd = self.o_u + self.SW
        self.o_kd = self.o_qd + self.DW
        self.o_vd = self.o_kd + self.DW
        self.o_ckv = self.o_vd + self.DW
        self.o_kr = self.o_ckv + self.KVL
        self.PW = -(-(self.o_kr + MLA_ROPE) // (4 * LANES)) * (4 * LANES)
        assert self.o_u % self.SW == 0 and self.o_qd % LANES == 0 and self.o_ckv % self.KVL == 0 and self.o_kr % LANES == 0
        assert self.H % 2 == 0 and self.DW % LANES == 0 and self.C % LANES == 0
        self.QW = self.H * (MLA_NOPE + MLA_ROPE)
        self.KVW = self.H * (MLA_NOPE + MLA_V)
        sizes = [self.QL, self.KVL, MLA_ROPE, self.SW, self.DW, self.DW, self.DW]
        starts = np.concatenate([[0], np.cumsum(sizes)[:-1]])
        self.ref_cols = {n: (int(s), int(z)) for n, s, z in zip(["cq", "ckv", "kr", "u", "qd", "kd", "vd"], starts, sizes)}
        self.INW = int(sum(sizes))
        self.new_order = ["cq", "u", "qd", "kd", "vd", "ckv", "kr"]
        src = np.concatenate([np.arange(self.ref_cols[n][0], self.ref_cols[n][0] + self.ref_cols[n][1]) for n in self.new_order])
        self.src_in = np.concatenate([src, -np.ones(self.PW - self.INW, np.int64)])
        self.src_q = self._heads_split(self.H, MLA_NOPE, MLA_ROPE)
        self.src_kv = self._heads_split(self.H, MLA_NOPE, MLA_V)

    @staticmethod
    def _heads_split(h, d1, d2):
        first = (np.arange(h)[:, None] * (d1 + d2) + np.arange(d1)[None, :]).reshape(-1)
        second = (np.arange(h)[:, None] * (d1 + d2) + d1 + np.arange(d2)[None, :]).reshape(-1)
        return np.concatenate([first, second])


def _regroup_in(dm, w):
    parts = [w[..., dm.ref_cols[n][0]:dm.ref_cols[n][0] + dm.ref_cols[n][1]] for n in dm.new_order]
    pad = dm.PW - dm.INW
    return jnp.concatenate(parts + [jnp.zeros(w.shape[:-1] + (pad,), w.dtype)], axis=-1)


def _ungroup_in(dm, w):
    off, pieces = 0, {}
    for n in dm.new_order:
        pieces[n] = w[..., off:off + dm.ref_cols[n][1]]
        off += dm.ref_cols[n][1]
    return jnp.concatenate([pieces[n] for n in ["cq", "ckv", "kr", "u", "qd", "kd", "vd"]], axis=-1)


def _split_heads(w, h, d1):
    t = w.reshape(w.shape[:-1] + (h, -1))
    return jnp.concatenate([t[..., :d1].reshape(w.shape[:-1] + (-1,)), t[..., d1:].reshape(w.shape[:-1] + (-1,))], axis=-1)


def _merge_heads(w, h, d1):
    a = w[..., :h * d1].reshape(w.shape[:-1] + (h, d1))
    b = w[..., h * d1:].reshape(w.shape[:-1] + (h, -1))
    return jnp.concatenate([a, b], axis=-1).reshape(w.shape[:-1] + (-1,))


def _cat_cols(re, im):
    r, c = re.shape
    return jnp.stack([re.reshape(r, c // LANES, LANES), im.reshape(r, c // LANES, LANES)], axis=2).reshape(r, 2 * c)


def _uncat_cols(cat):
    r, c2 = cat.shape
    t = cat.reshape(r, c2 // (2 * LANES), 2, LANES)
    return t[:, :, 0].reshape(r, c2 // 2), t[:, :, 1].reshape(r, c2 // 2)


def _block_diag(t, g):
    _, a, b = t.shape
    eye = jnp.eye(g, dtype=bool)[:, None, :, None]
    return jnp.where(eye, t[:, :, None, :], 0).reshape(g * a, g * b)


def _diag_blocks(m, g):
    a, b = m.shape[0] // g, m.shape[1] // g
    eye = jnp.eye(g, dtype=m.dtype)[:, None, :, None]
    return jnp.sum(m.reshape(g, a, g, b) * eye, axis=2)


def _rope_tables(dm):
    half = MLA_ROPE // 2
    inv_freq = ROPE_THETA ** (-jnp.arange(half, dtype=F32) / half)
    ang = jnp.arange(dm.M, dtype=F32)[:, None] * inv_freq[None, :]
    cos = jnp.concatenate([jnp.cos(ang), jnp.cos(ang)], axis=1)
    sin = jnp.concatenate([-jnp.sin(ang), jnp.sin(ang)], axis=1)
    return jnp.tile(cos, (1, dm.H)), jnp.tile(sin, (1, dm.H)), jnp.tile(cos, (1, LANES // MLA_ROPE)), jnp.tile(sin, (1, LANES // MLA_ROPE))


def _rope(x, cos, sin):
    return x * cos + _swap_halves(x, MLA_ROPE // 2) * sin


def _rope_t(d, cos, sin):
    return d * cos + _swap_halves(d * sin, MLA_ROPE // 2)


def _ssm_layer_params(dm, a_re, a_im, log_dt, b_re, b_im):
    flat = lambda t: t.reshape(1, dm.C)
    ldt = jnp.repeat(log_dt, SSM_STATE).reshape(1, dm.C)
    bt = lambda t: jnp.transpose(t, (2, 0, 1)).reshape(SSM_GROUP, dm.C)
    return flat(a_re), flat(a_im), ldt, bt(b_re), bt(b_im)


def layer_forward(dm, l, x, lw, sp, tabs, hook_q, hook_mid):
    M, D = dm.M, dm.D
    n = lambda s: f"{s}_l{l}"
    sv = {"x_in": x}
    h1 = rms_fwd(n("rms_mix"), x, D, 0, lw["g_mix"], M=M)
    proj = matmul(n("in_proj"), h1, lw["w_in"], "nn", M=M, N=dm.PW, K=D)
    sv.update(h1=h1, proj=proj)
    cqn = rms_fwd(n("rms_q"), proj, dm.QL, dm.o_cq, lw["g_q"], M=M)
    q = matmul(n("q_up"), cqn, lw["w_uq"], "nn", M=M, N=dm.QW, K=dm.QL)
    ckvn = rms_fwd(n("rms_kv"), proj, dm.KVL, dm.o_ckv, lw["g_kv"], M=M)
    kv = matmul(n("kv_up"), ckvn, lw["w_ukv"], "nn", M=M, N=dm.KVW, K=dm.KVL, out_dtype=BF16)
    cosq, sinq, cosk, sink = tabs[:4]
    nw = dm.H * MLA_NOPE

    def rope_fn(qb, kb, cq, sq, ck, sk):
        return jnp.concatenate([qb[:, :nw], _rope(qb[:, nw:], cq, sq)], axis=1), _rope(kb, ck, sk)

    pw = dm.H * MLA_ROPE
    q_bf, kpe = rowwise(n("rope"), rope_fn, [(q, dm.QW, 0), (proj, LANES, dm.o_kr), (cosq, pw, 0), (sinq, pw, 0), (cosk, LANES, 0), (sink, LANES, 0)],
                        [], [(dm.QW, BF16), (LANES, BF16)], M=M)
    mla_scale = (MLA_NOPE + MLA_ROPE) ** -0.5
    o_mla, lse_mla = attention_fwd(n("mla_fwd"), q_bf, 0, kv, 0, kv, nw, da=MLA_NOPE, dv=MLA_V, pairs=dm.H // 2, scale=mla_scale,
                                   M=M, qb=q_bf, qb_off=nw, kb=kpe)
    sv.update(cqn=cqn, ckvn=ckvn, q_bf=q_bf, kv=kv, kpe=kpe, o_mla=o_mla, lse_mla=lse_mla)
    bu = matmul(n("ssm_bu"), proj, sp["bcat"], "nn", M=M, N=2 * dm.C, K=dm.SW, a_off=(0, dm.o_u))
    hcat = ssm_scan(n("ssm_scan"), bu, sp["acat"], M=M)
    ylin = matmul(n("ssm_y"), hcat, sp["ccat"], "nn", M=M, N=dm.SW, K=2 * dm.C)
    yg = rowwise(n("ssm_gelu"), lambda y, u, d: (_gelu(y + d * u),), [(ylin, dm.SW, 0), (proj, dm.SW, dm.o_u)], [lw["d_skip"]],
                 [(dm.SW, BF16)], M=M)[0]
    z = matmul(n("ssm_glu"), yg, lw["w_glu"], "nn", w=("cols", 2 * dm.SW // N_CHIPS), M=M, N=2 * dm.SW, K=dm.SW)
    sw = dm.SW

    def glu_fn(zb, b):
        zz = zb + b
        return (zz[:, :sw] * jax.nn.sigmoid(zz[:, sw:]),)

    o_ssm = hook_q(rowwise(n("ssm_gate"), glu_fn, [(z, 2 * sw, 0)], [lw["b_glu"]], [(sw, F32)], M=M)[0])
    sv.update(hcat=hcat, ylin=ylin, yg=yg, z=z, o_ssm=o_ssm)
    o_dil, lse_dil = attention_fwd(n("dil_fwd"), proj, dm.o_qd, proj, dm.o_kd, proj, dm.o_vd, da=DIL_HEAD, dv=DIL_HEAD, pairs=dm.DW // LANES,
                                   scale=DIL_HEAD ** -0.5, M=M, bias=tabs[4])
    sv.update(o_dil=o_dil, lse_dil=lse_dil)
    yn = rowwise(n("out_norm"), lambda a, b, c, ga, gb, gc: (jnp.concatenate([_rms(a, ga), _rms(b, gb), _rms(c, gc)], axis=1),),
                 [(o_mla, dm.MW, 0), (o_ssm, dm.SW, 0), (o_dil, dm.DW, 0)], [lw["g_out_mla"], lw["g_out_ssm"], lw["g_out_dil"]],
                 [(D, BF16)], M=M)[0]
    yn = hook_mid(yn, lw)
    x_mid = matmul(n("out_proj"), yn, lw["w_o"], "nn", w=("rows", D // N_CHIPS), M=M, N=D, K=D, add=x)
    h2 = rms_fwd(n("rms_ffn"), x_mid, D, 0, lw["g_ffn"], M=M)
    ffs = dm.FF // N_CHIPS
    gate = matmul(n("ffn_gate"), h2, lw["w_gate"], "nn", w=("cols", ffs), M=M, N=dm.FF, K=D, out_dtype=BF16)
    up = matmul(n("ffn_up"), h2, lw["w_up"], "nn", w=("cols", ffs), M=M, N=dm.FF, K=D, out_dtype=BF16)

    def act_fn(gb, ub):
        gf = gb.astype(F32)
        return (gf * jax.nn.sigmoid(gf) * ub.astype(F32),)

    act = rowwise(n("ffn_act"), act_fn, [(gate, dm.FF, 0), (up, dm.FF, 0)], [], [(dm.FF, BF16)], M=M)[0]
    x_out = matmul(n("ffn_down"), act, lw["w_down"], "nn", w=("rows",ffs), M=M, N=D, K=dm.FF, add=x_mid)
    sv.update(yn=yn, x_mid=x_mid, h2=h2, gate=gate, up=up, act=act)
    return x_out, sv


def layer_backward(dm, l, dx, lw, sp, tabs, sv, hook_a, hook_m, hook_b):
    M, D = dm.M, dm.D
    n = lambda s: f"{s}_l{l}"
    g = {}
    ffs = dm.FF // N_CHIPS
    dact = matmul(n("ffn_down_dx"), dx, lw["w_down"], "nt", w=("rows", ffs), M=M, N=dm.FF, K=D, out_dtype=BF16)
    g["w_down"] = matmul(n("ffn_down_dw"), sv["act"], dx, "tn", M=dm.FF, N=D, K=M, out_dtype=BF16, into=("rows", ffs))

    def act_bwd(gb, ub, db):
        _, vjp = jax.vjp(lambda a, b: a * jax.nn.sigmoid(a) * b, gb.astype(F32), ub.astype(F32))
        return vjp(db.astype(F32))

    dgate, dup = rowwise(n("ffn_act_bwd"), act_bwd, [(sv["gate"], dm.FF, 0), (sv["up"], dm.FF, 0), (dact, dm.FF, 0)], [],
                         [(dm.FF, BF16), (dm.FF, BF16)], M=M)
    dh2 = matmul(n("ffn_gate_dx"), dgate, lw["w_gate"], "nt", w=("cols",ffs), M=M, N=D, K=dm.FF)
    dh2 = matmul(n("ffn_up_dx"), dup, lw["w_up"], "nt", w=("cols",ffs), M=M, N=D, K=dm.FF, add=dh2)
    g["w_gate"] = matmul(n("ffn_gate_dw"), sv["h2"], dgate, "tn", M=D, N=dm.FF, K=M, out_dtype=BF16, into=("cols", ffs))
    g["w_up"] = matmul(n("ffn_up_dw"), sv["h2"], dup, "tn", M=D, N=dm.FF, K=M, out_dtype=BF16, into=("cols", ffs))
    dx_mid, g["g_ffn"] = rms_bwd(n("rms_ffn_bwd"), sv["x_mid"], D, 0, lw["g_ffn"], dh2, dx, M=M)
    dx_mid = hook_a(dx_mid, g)
    dyn = matmul(n("out_proj_dx"), dx_mid, lw["w_o"], "nt", w=("rows", D // N_CHIPS), M=M, N=D, K=D)
    g["w_o"] = matmul(n("out_proj_dw"), sv["yn"], dx_mid, "tn", M=D, N=D, K=M, out_dtype=BF16, into=("rows", D // N_CHIPS))
    mw, sw, dw = dm.MW, dm.SW, dm.DW

    def out_norm_bwd(a, b, c, dy, ga, gb, gc):
        res, sums = [], []
        for t, gg, lo, hi in ((a, ga, 0, mw), (b, gb, mw, mw + sw), (c, gc, mw + sw, mw + sw + dw)):
            _, vjp = jax.vjp(_rms, t, gg)
            dt, dg = vjp(dy[:, lo:hi])
            res.append(dt)
            sums.append(dg)
        return res + sums

    do_mla, do_ssm, do_dil, g["g_out_mla"], g["g_out_ssm"], g["g_out_dil"] = rowwise(
        n("out_norm_bwd"), out_norm_bwd, [(sv["o_mla"], mw, 0), (sv["o_ssm"], sw, 0), (sv["o_dil"], dw, 0), (dyn, D, 0)],
        [lw["g_out_mla"], lw["g_out_ssm"], lw["g_out_dil"]], [(mw, F32), (sw, F32), (dw, F32)], [mw, sw, dw], M=M)
    proj = sv["proj"]
    dqd, dkd, dvd = attention_bwd(n("dil_bwd"), proj, dm.o_qd, proj, dm.o_kd, proj, dm.o_vd, sv["o_dil"], do_dil, sv["lse_dil"],
                                  da=DIL_HEAD, dv=DIL_HEAD, pairs=dw // LANES, scale=DIL_HEAD ** -0.5, M=M, bias=tabs[4], tk_cap=1024)
    do_ssm = hook_m(do_ssm, g)
    def glu_bwd(zb, db, b):
        _, vjp = jax.vjp(lambda zz, bb: (zz + bb)[:, :sw] * jax.nn.sigmoid((zz + bb)[:, sw:]), zb, b)
        return vjp(db)

    dz, g["b_glu"] = rowwise(n("ssm_gate_bwd"), glu_bwd, [(sv["z"], 2 * sw, 0), (do_ssm, sw, 0)], [lw["b_glu"]], [(2 * sw, BF16)], [2 * sw], M=M)
    dyg = matmul(n("ssm_glu_dx"), dz, lw["w_glu"], "nt", w=("cols", 2 * sw // N_CHIPS), M=M, N=sw, K=2 * sw)
    g["w_glu"] = matmul(n("ssm_glu_dw"), sv["yg"], dz, "tn", M=sw, N=2 * sw, K=M, out_dtype=BF16, into=("cols", 2 * sw // N_CHIPS))

    def gelu_bwd(y, u, dy, d):
        _, vjp = jax.vjp(lambda yy, uu, dd: _gelu(yy + dd * uu), y, u, d)
        return vjp(dy)

    dylin, du1, g["d_skip"] = rowwise(n("ssm_gelu_bwd"), gelu_bwd, [(sv["ylin"], sw, 0), (proj, sw, dm.o_u), (dyg, sw, 0)], [lw["d_skip"]],
                                      [(sw, BF16), (sw, F32)], [sw], M=M)
    seed = matmul(n("ssm_y_dx"), dylin, sp["ccat"], "nt", M=M, N=2 * dm.C, K=sw)
    d_ccat = matmul(n("ssm_y_dw"), dylin, sv["hcat"], "tn", M=sw, N=2 * dm.C, K=M)
    lam, d_acat = ssm_scan(n("ssm_scan_bwd"), seed, sp["acat_conj"], M=M, reverse=True, hcat=sv["hcat"])
    du = matmul(n("ssm_bu_dx"), lam, sp["bcat"], "nt", M=M, N=sw, K=2 * dm.C, add=du1, out_dtype=BF16)
    d_bcat = matmul(n("ssm_bu_dw"), proj, lam, "tn", M=sw, N=2 * dm.C, K=M, a_off=(0, dm.o_u))
    g["ssm_raw"] = (d_acat, d_bcat, d_ccat)
    nw = dm.H * MLA_NOPE
    dqn, dkn, dv_, dqp, dkp = attention_bwd(n("mla_bwd"), sv["q_bf"], 0, sv["kv"], 0, sv["kv"], nw, sv["o_mla"], do_mla, sv["lse_mla"],
                                            da=MLA_NOPE, dv=MLA_V, pairs=dm.H // 2, scale=(MLA_NOPE + MLA_ROPE) ** -0.5, M=M,
                                            qb=sv["q_bf"], qb_off=nw, kb=sv["kpe"])
    cosq, sinq, cosk, sink = tabs[:4]
    pw = dm.H * MLA_ROPE
    dqp_u, dkr = rowwise(n("rope_bwd"), lambda a, b, cq, sq, ck, sk: (_rope_t(a, cq, sq), _rope_t(b, ck, sk)),
                         [(dqp, pw, 0), (dkp, LANES, 0), (cosq, pw, 0), (sinq, pw, 0), (cosk, LANES, 0), (sink, LANES, 0)], [],
                         [(pw, BF16), (LANES, BF16)], M=M)
    dq = lane_concat(n("dq_cat"), [dqn, dqp_u], M=M)
    dkv = lane_concat(n("dkv_cat"), [dkn, dv_], M=M)
    dcqn = matmul(n("q_up_dx"), dq, lw["w_uq"], "nt", M=M, N=dm.QL, K=dm.QW)
    g["w_uq"] = matmul(n("q_up_dw"), sv["cqn"], dq, "tn", M=dm.QL, N=dm.QW, K=M, out_dtype=BF16)
    dckvn = matmul(n("kv_up_dx"), dkv, lw["w_ukv"], "nt", M=M, N=dm.KVL, K=dm.KVW)
    g["w_ukv"] = matmul(n("kv_up_dw"), sv["ckvn"], dkv, "tn", M=dm.KVL, N=dm.KVW, K=M, out_dtype=BF16)
    dcq, g["g_q"] = rms_bwd(n("rms_q_bwd"), proj, dm.QL, dm.o_cq, lw["g_q"], dcqn, M=M, out_dtype=BF16)
    dckv, g["g_kv"] = rms_bwd(n("rms_kv_bwd"), proj, dm.KVL, dm.o_ckv, lw["g_kv"], dckvn, M=M, out_dtype=BF16)
    dproj = hook_b(lane_concat(n("dproj_cat"), [dcq, du, dqd, dkd, dvd, dckv, dkr], M=M, pad_to=dm.PW), g)
    dh1 = matmul(n("in_proj_dx"), dproj, lw["w_in"], "nt", M=M, N=D, K=dm.PW)
    g["w_in"] = matmul(n("in_proj_dw"), sv["h1"], dproj, "tn", M=D, N=dm.PW, K=M, out_dtype=BF16)
    dx_in, g["g_mix"] = rms_bwd(n("rms_mix_bwd"), sv["x_in"], D, 0, lw["g_mix"], dh1, dx_mid, M=M)
    return dx_in, g


def layer_params(dm, small, l):
    lw = {k: small[k][l].reshape(1, -1) for k in ("g_mix", "g_q", "g_kv", "b_glu", "g_out_mla", "g_out_ssm", "g_out_dil", "g_ffn", "d_skip")}
    raw = _ssm_layer_params(dm, small["a_re"][l], small["a_im"][l], small["log_dt"][l], small["b_re"][l], small["b_im"][l])
    ar, ai, bbr, bbi = ssm_param_fwd(f"ssm_param_l{l}", *raw)
    g_ = dm.G
    bd = lambda t: _block_diag(jnp.transpose(t.reshape(SSM_GROUP, g_, SSM_STATE), (1, 0, 2)), g_)
    cd = lambda t: _block_diag(jnp.transpose(t, (0, 2, 1)), g_)
    cre, cim = cd(small["c_re"][l]), cd(small["c_im"][l])
    sp = {"acat": _cat_cols(ar, ai), "acat_conj": _cat_cols(ar, -ai),
          "bcat": _cat_cols(bd(bbr), bd(bbi)).astype(BF16),
          "ccat": _cat_cols(cre.T, -cim.T).T.astype(BF16)}
    return lw, sp, raw


def ssm_param_grads(dm, l, g, raw):
    d_acat, d_bcat, d_ccat_t = g.pop("ssm_raw")
    d_ar, d_ai = _uncat_cols(d_acat)
    dbr, dbi = ssm_diag_rows(f"ssm_db_l{l}", d_bcat)
    dcr, dci = ssm_diag_rows(f"ssm_dc_l{l}", d_ccat_t)
    g_ = dm.G
    da_re, da_im, dldt, db_re, db_im = ssm_param_bwd(f"ssm_param_bwd_l{l}", *raw, d_ar, d_ai, dbr, dbi)
    g["a_re"], g["a_im"] = da_re.reshape(g_, SSM_STATE), da_im.reshape(g_, SSM_STATE)
    g["log_dt"] = jnp.sum(dldt.reshape(g_, SSM_STATE), axis=1)
    from_rows = lambda t: jnp.transpose(t.reshape(SSM_GROUP, g_, SSM_STATE), (1, 2, 0))
    g["b_re"], g["b_im"] = from_rows(db_re), from_rows(db_im)
    g["c_re"] = jnp.transpose(dcr.reshape(SSM_GROUP, g_, SSM_STATE), (1, 0, 2))
    g["c_im"] = -jnp.transpose(dci.reshape(SSM_GROUP, g_, SSM_STATE), (1, 0, 2))
    g["d_skip"] = g["d_skip"].reshape(g_, SSM_GROUP)


def loss_and_grad(dm, h, target, g_final):
    D = dm.D

    def loss_fn(xb, tb, gb):
        y, vjp = jax.vjp(_rms, xb, gb)
        err = y - tb
        dxb, dg = vjp(err * (1.0 / D))
        part = 0.5 * jnp.sum(jnp.mean(err * err, axis=-1, keepdims=True), axis=0, keepdims=True)
        lane = lax.broadcasted_iota(jnp.int32, (1, LANES), 1)
        return dxb, dg, jnp.where(lane == 0, part, 0.0)

    return rowwise("loss", loss_fn, [(h, D, 0), (target, D, 0)], [g_final.reshape(1, D)], [(D, F32)], [D, LANES], M=dm.M)


KIND = {"w_in": "cols", "w_uq": "cols", "w_ukv": "cols", "w_glu": "cols", "w_o": "rows", "w_gate": "cols", "w_up": "cols", "w_down": "rows"}
SHARDED = list(KIND)
TRANSPOSED = ("w_in",)
GATHER_GROUPS = {"mixer": ["w_in", "w_uq", "w_ukv", "w_glu"], "rest": ["w_o", "w_gate", "w_up", "w_down"]}
REDUCE_GROUPS = {"ffn": ["w_gate", "w_up", "w_down"], "others": ["w_o", "w_in", "w_uq", "w_ukv", "w_glu"]}
SMALL = ["g_mix", "g_q", "g_kv", "a_re", "a_im", "b_re", "b_im", "c_re", "c_im", "d_skip", "log_dt", "b_glu",
         "g_out_mla", "g_out_ssm", "g_out_dil", "g_ffn", "g_final"]
ORDER = ["g_mix", "w_in", "g_q", "w_uq", "g_kv", "w_ukv", "a_re", "a_im", "b_re", "b_im", "c_re", "c_im", "d_skip", "log_dt",
         "w_glu", "b_glu", "g_out_mla", "g_out_ssm", "g_out_dil", "w_o", "g_ffn", "w_gate", "w_up", "w_down", "g_final"]


def kernel(x, g_mix, w_in, g_q, w_uq, g_kv, w_ukv, a_re, a_im, b_re, b_im, c_re, c_im, d_skip, log_dt, w_glu, b_glu, g_out_mla, g_out_ssm, g_out_dil, w_o, g_ffn, w_gate, w_up, w_down, g_final, loss_target, m_g_mix, m_w_in, m_g_q, m_w_uq, m_g_kv, m_w_ukv, m_a_re, m_a_im, m_b_re, m_b_im, m_c_re, m_c_im, m_d_skip, m_log_dt, m_w_glu, m_b_glu, m_g_out_mla, m_g_out_ssm, m_g_out_dil, m_w_o, m_g_ffn, m_w_gate, m_w_up, m_w_down, m_g_final, v_g_mix, v_w_in, v_g_q, v_w_uq, v_g_kv, v_w_ukv, v_a_re, v_a_im, v_b_re, v_b_im, v_c_re, v_c_im, v_d_skip, v_log_dt, v_w_glu, v_b_glu, v_g_out_mla, v_g_out_ssm, v_g_out_dil, v_w_o, v_g_ffn, v_w_gate, v_w_up, v_w_down, v_g_final):
    args = locals()
    w = {k: args[k] for k in ORDER}
    mom = {k: args["m_" + k] for k in ORDER}
    var = {k: args["v_" + k] for k in ORDER}
    dm = Dims(x, g_q, g_kv, g_out_mla, g_out_ssm, g_out_dil, w_gate.shape[-1] * N_CHIPS)
    L = g_mix.shape[0]

    small = {k: w[k] for k in SMALL}
    na = len(SHARDED)
    tabs = _rope_tables(dm) + (dilated_bias(dm.M),)
    sel ={"w_in": selection_matrices(dm.src_in, w_in.shape[-1]), "w_uq": selection_matrices(dm.src_q, w_uq.shape[-1]),
           "w_ukv": selection_matrices(dm.src_kv, w_ukv.shape[-1])}

    stored = lambda t, k: jnp.swapaxes(t, 1, 2) if k in TRANSPOSED else t
    store_kind = {k: "rows" if k in TRANSPOSED else KIND[k] for k in SHARDED}
    G = [{} for _ in range(L)]
    sems = {}

    def cast_group(l, grp, car):
        for k in GATHER_GROUPS[grp]:
            G[l][k], car = cast_into_slot(f"cast_{k}_l{l}", stored(w[k], k), store_kind[k], l, car)
        return car

    def gather_stage(stage, plan):
        def start(l, grp, car):
            names = GATHER_GROUPS[grp]
            sems[stage, l, grp], bufs, _, car = split_start(f"ag_{stage}_start_{grp}_l{l}", plan, 3 * len(names), [G[l][k] for k in names], [], car)
            G[l].update(zip(names, bufs))
            return car

        def wait(l, grp, car):
            names = GATHER_GROUPS[grp]
            bufs, car = split_wait(f"ag_{stage}_wait_{grp}_l{l}", plan, sems[stage, l, grp], [G[l][k] for k in names], car)
            G[l].update(zip(names, bufs))
            return car

        return start, wait

    ici_start, ici_wait = gather_stage("ici", plan_gather_ici)
    pass_start, pass_wait = gather_stage("pass", plan_gather_pass)

    car = tabs[0]
    first = [(l, grp) for l, grp in ((0, "mixer"), (0, "rest"), (1, "mixer")) if l < L]
    for l, grp in first:
        car = ici_start(l, grp, cast_group(l, grp, car))
    for l in range(L):
        for grp in GATHER_GROUPS:
            if (l, grp) not in first:
                car = cast_group(l, grp, car)
    tabs = (pass_wait(0, "mixer", pass_start(0, "mixer", ici_wait(0, "mixer", car))),) + tabs[1:]
    h = x.reshape(dm.M, dm.D)
    lws, sps, raws, saved = [], [], [], []
    for l in range(L):
        lw, sp, raw = layer_params(dm, small, l)
        for k in GATHER_GROUPS["mixer"]:
            lw[k] = regroup_cols(f"regroup_{k}_l{l}", G[l][k], sel[k], k in TRANSPOSED) if k in sel else G[l][k]

        def at_q(car, l=l):
            return pass_start(l, "rest", ici_wait(l, "rest", car))

        def at_mid(car, lw, l=l):
            car = pass_wait(l, "rest", car)
            lw.update({k: G[l][k] for k in GATHER_GROUPS["rest"]})
            if l + 1 < L:
                car = pass_wait(l + 1, "mixer", pass_start(l + 1, "mixer", ici_wait(l + 1, "mixer", car)))
                car = ici_start(l + 1, "rest", car)
            if l + 2 < L:
                car = ici_start(l + 2, "mixer", car)
            return car

        h, sv = layer_forward(dm, l, h, lw, sp, tabs, at_q, at_mid)
        lws.append(lw)
        sps.append(sp)
        raws.append(raw)
        saved.append(sv)
    dx, g_final_part, loss_part = loss_and_grad(dm, h, loss_target.reshape(dm.M, dm.D), w["g_final"])

    def reduce_begin(l, grp, g, car):
        names = REDUCE_GROUPS[grp]
        parts = [g.pop(k) for k in names]
        fresh = [jax.ShapeDtypeStruct(p.shape[1:], BF16) for p in parts]
        sm, parts, gots, car = split_start(f"rs_pair_start_{grp}_l{l}", plan_pair(len(names)), len(names), parts, fresh, car)
        return {"l": l, "grp": grp, "names": names, "sems": sm, "parts": parts, "gots": gots}, car

    def reduce_chips(st, car):
        names, tag, n = st["names"], f"{st['grp']}_l{st['l']}", len(st["names"])
        bufs, car = split_wait(f"rs_pair_wait_{tag}", plan_pair(n), st["sems"], st["parts"] + st["gots"], car)
        sums = [pair_sum(f"rs_pairsum_{a}_l{st['l']}", bufs[i], bufs[n + i]) for i, a in enumerate(names)]
        fresh = [jax.ShapeDtypeStruct((3,) + s.shape[1:], BF16) for s, _ in sums]
        st["sems"], st["s"], st["arrived"], car = split_start(f"rs_chips_start_{tag}", plan_chips(n), 3 * n, [s for s, _ in sums], fresh, car)
        st["own"] = [o for _, o in sums]
        return car

    def reduce_share(st, car):
        names, tag, n = st["names"], f"{st['grp']}_l{st['l']}", len(st["names"])
        bufs, car = split_wait(f"rs_chips_wait_{tag}", plan_chips(n), st["sems"], st["s"] + st["arrived"], car)
        mine = [chips_sum(f"rs_sum_{a}_l{st['l']}", st["own"][i], bufs[n + i]) for i, a in enumerate(names)]
        fresh = [jax.ShapeDtypeStruct(m_.shape, F32) for m_ in mine]
        st["sems"], st["mine"], st["theirs"], car = split_start(f"rs_share_start_{tag}", plan_share(n), n, mine, fresh, car)
        return car

    def reduce_end(st, car):
        n = len(st["names"])
        bufs, car = split_wait(f"rs_share_wait_{st['grp']}_l{st['l']}", plan_share(n), st["sems"], st["mine"] + st["theirs"], car)
        for i, k in enumerate(st["names"]):
            reduced[st["l"]][k] = (bufs[i], bufs[n + i])
        return car

    reduced, grads, prev_ffn, prev_oth = [{} for _ in range(L)], [None] * L, None, None
    for l in reversed(range(L)):
        mine = {}

        def at_a(car, g, l=l, mine=mine, pf=prev_ffn, po=prev_oth):
            mine["st"], car = reduce_begin(l, "ffn", g, car)
            if pf is not None:
                car = reduce_chips(po, reduce_share(pf, car))
            return car

        def at_m(car, g, mine=mine, pf=prev_ffn):
            car = reduce_chips(mine["st"], car)
            return car if pf is None else reduce_end(pf, car)

        def at_b(car, g, po=prev_oth):
            return car if po is None else reduce_share(po, car)

        dx, g = layer_backward(dm, l, dx, lws[l], sps[l], tabs, saved[l], at_a, at_m, at_b)
        ssm_param_grads(dm, l, g, raws[l])
        car = dx if l else loss_part
        if prev_oth is not None:
            car = reduce_end(prev_oth, car)
        for k in sel:
            g[k] = ungroup_cols(f"ungroup_{k}_l{l}", g[k], sel[k], k in TRANSPOSED)
        prev_ffn = mine["st"]
        prev_oth, car = reduce_begin(l, "others", g, car)
        if l:
            dx = car
        grads[l] = g
    car = reduce_end(prev_ffn, reduce_share(prev_ffn, reduce_chips(prev_oth, car)))
    loss_part = reduce_end(prev_oth, reduce_share(prev_oth, car))

    gsum = {}
    small_names = [k for k in SMALL if k != "g_final"]
    pieces = [jnp.stack([grads[l][k] for l in range(L)]).reshape(-1) for k in small_names] + [g_final_part.reshape(-1), loss_part.reshape(-1)]
    sizes = [int(p.shape[0]) for p in pieces]
    total = sum(sizes)
    rows = -(-total // (LANES * 16)) * 16
    pack = lambda ps: jnp.concatenate(ps + [jnp.zeros((rows * LANES - total,), F32)]).reshape(rows, LANES)
    red = all_reduce_small(pack(pieces))
    flat = red.reshape(-1)
    offs = np.concatenate([[0], np.cumsum(sizes)]).astype(int)
    names = small_names + ["g_final"]
    for i, k in enumerate(names):
        gsum[k] = flat[offs[i]:offs[i + 1]].reshape(w[k].shape)
    loss = flat[offs[len(names)]]

    delta, new_m, new_v = {}, {}, {}
    for k in SHARDED:
        res = adamw_layers(f"adam_{k}", stored(w[k], k), stored(mom[k], k), stored(var[k], k), [reduced[l][k][0] for l in range(L)],
                           [reduced[l][k][1] for l in range(L)], store_kind[k])
        gsum[k], delta[k], new_m[k], new_v[k] = (stored(t, k) for t in res)
    sm_sizes = sizes[:len(names)]
    sm_total = sum(sm_sizes)
    packs = lambda d: jnp.concatenate([d[k].reshape(-1) for k in names] + [jnp.zeros((rows * LANES - sm_total,), F32)]).reshape(rows, LANES)
    gs = jnp.concatenate([flat[:sm_total], jnp.zeros((rows * LANES - sm_total,), F32)]).reshape(rows, LANES)
    d_, m_, v_ = adamw("adam_small", packs(w), gs, packs(mom), packs(var))
    for i, k in enumerate(names):
        sl = slice(offs[i], offs[i + 1])
        delta[k], new_m[k], new_v[k] = (t.reshape(-1)[sl].reshape(w[k].shape) for t in (d_, m_, v_))

    return (loss, dx.reshape(x.shape), *[gsum[k] for k in ORDER], *[delta[k] for k in ORDER],
            *[new_m[k] for k in ORDER], *[new_v[k] for k in ORDER])
```

```python
import functools
import math

import numpy as np
import jax
import jax.numpy as jnp
from jax import lax
from jax.experimental import pallas as pl
from jax.experimental.pallas import tpu as pltpu

F32 = jnp.float32
BF16 = jnp.bfloat16
MESH = pl.DeviceIdType.MESH

NORM_EPS = 1e-6
MLA_NOPE, MLA_ROPE, MLA_V = 128, 64, 128
SSM_GROUP, SSM_STATE = 16, 64
DIL_HEAD = 64
DIL_PATTERNS = ((128, 1), (512, 4), (2048, 16))
ROPE_THETA = 10000.0
ADAM_LR, ADAM_B1, ADAM_B2, ADAM_EPS, ADAM_WD, ADAM_STEP = 0.001, 0.9, 0.999, 1e-08, 0.01, 10
N_CHIPS = 4

LANES = 128
SUBLANES_BF16 = 16
VMEM_LIMIT = 56 * 1024 * 1024
ROW_BUDGET = 20 * 1024 * 1024
MM_BUDGET = 40 * 1024 * 1024
NEG = -1e30


def _cparams(sem=None):
    return pltpu.CompilerParams(dimension_semantics=sem, vmem_limit_bytes=VMEM_LIMIT)


def _pick(n, cap, q, off=0):
    best = None
    for d in range(q, min(n, cap) + 1, q):
        if n % d == 0 and off % d == 0:
            best = d
    if best is None or (best * 4 <= min(cap, n) and n <= 3072 and off % n == 0):
        assert off % n == 0, (n, off)
        return n
    return best


_DOT_DIMS = {"nn": (((1,), (0,)), ((), ())), "nt": (((1,), (1,)), ((), ())), "tn": (((0,), (0,)), ((), ()))}


def _divs(n, q, within=None, off=0):
    return [d for d in range(q, n + 1, q) if n % d == 0 and off % d == 0 and (within is None or within % d == 0)] or [n]


def _mm_tiles(M, N, K, tms, tns, tks, ab, bb, ob):
    best = None
    for tk in tks:
        nk = K // tk
        for tn in tns:
            for tm in tms:
                if 2 * (tm * tk * ab + tk * tn * bb + tm * tn * ob) + tm * tn * 4 * (2 if nk > 1 else 1) > MM_BUDGET:
                    continue
                steps = (M // tm) * (N // tn) * nk
                hbm = M * K * ab * (1 if nk == 1 else N // tn) + K * N * bb * (M // tm) + M * N * ob
                cost = steps * 0.35e-6 + hbm / 3.0e12 + (nk - 1) * M * N * 12 / 4.0e12
                if best is None or cost < best[0]:
                    best = (cost, tm, tn, tk)
    assert best is not None, (M, N, K)
    return best[1:]


def matmul(name, a, b, mode, *, M, N, K, a_off=(0, 0), b_off=(0, 0), b_lead=None, w=None, add=None, out_dtype=F32, into=None):
    tn_mode = mode == "tn"
    a_ro, a_co = (a_off[1], a_off[0]) if tn_mode else a_off
    b_no, b_ko = b_off if mode == "nt" else (b_off[1], b_off[0])
    n_within = k_within = m_within = None
    if w is not None:
        kind, shard = w
        if kind == "cols":
            b = b.reshape(N_CHIPS, b.shape[1] * b.shape[2], b.shape[3])
        rows_within, cols_within = (K if mode == "nn" else N, shard) if kind == "cols" else (shard, (N if mode == "nn" else K) // 2)
        k_within, n_within = (rows_within, cols_within) if mode == "nn" else (cols_within, rows_within)
    if into is not None:
        m_within, n_within = (M // 2, into[1]) if into[0] == "cols" else (into[1], N // 2)
    tms = [d for d in _divs(M, 128 if tn_mode else SUBLANES_BF16, m_within, a_ro) if d <= 1408]
    tns = [d for d in _divs(N, LANES, n_within, b_no) if d <= 2048]
    tks = _divs(K, SUBLANES_BF16 if tn_mode else LANES, k_within, math.gcd(a_co, b_ko))
    ob = jnp.dtype(out_dtype).itemsize + (add.dtype.itemsize if add is not None else 0)
    tm, tn, tk = _mm_tiles(M, N, K, tms, tns, tks, a.dtype.itemsize, b.dtype.itemsize, ob)
    nk = K // tk
    dn = _DOT_DIMS[mode]

    if tn_mode:
        a_spec = pl.BlockSpec((tk, tm), lambda i, j, k: (k + a_co // tk, i + a_ro // tm))
    else:
        a_spec = pl.BlockSpec((tm, tk), lambda i, j, k: (i + a_ro // tm, k + a_co // tk))
    b_blk = (tn, tk) if mode == "nt" else (tk, tn)
    if w is not None:
        tr_, tc_ = (tk, tn) if mode == "nn" else (tn, tk)
        rper, cper = rows_within // tr_, cols_within // tc_

        def wmap(rb, cb):
            if kind == "cols":
                return (cb // cper, rb, cb % cper)
            return (rb // rper, cb // cper, rb % rper, cb % cper)

        imap = (lambda i, j, k: wmap(k, j)) if mode == "nn" else (lambda i, j, k: wmap(j, k))
        b_spec = pl.BlockSpec((None,) * (b.ndim - 2) + b_blk, imap)
    else:
        if mode == "nt":
            imap = lambda i, j, k: (j + b_no // tn, k + b_ko // tk)
        else:
            imap = lambda i, j, k: (k + b_ko // tk, j + b_no // tn)
        if b_lead is None:
            b_spec = pl.BlockSpec(b_blk, imap)
        else:
            b_spec = pl.BlockSpec((None,) + b_blk, lambda i, j, k: (b_lead,) + imap(i, j, k))
    o_plain = pl.BlockSpec((tm, tn), lambda i, j, k: (i, j))
    if into is None:
        o_spec, out_shape = o_plain, jax.ShapeDtypeStruct((M, N), out_dtype)
    else:
        rper, cper = m_within // tm, n_within // tn
        if into[0] == "cols":
            o_spec = pl.BlockSpec((None, None, tm, tn), lambda i, j, k: (i // rper, j // cper, i % rper, j % cper))
        else:
            o_spec = pl.BlockSpec((None, None, tm, tn), lambda i, j, k: (j // cper, i // rper, i % rper, j % cper))
        out_shape = jax.ShapeDtypeStruct((2, N_CHIPS, m_within, n_within), out_dtype)
    has_add = add is not None
    n_in = 2 + has_add

    def body(*refs):
        a_ref, b_ref = refs[0], refs[1]
        add_ref = refs[2] if has_add else None
        o_ref = refs[n_in]
        part = lax.dot_general(a_ref[...].astype(BF16), b_ref[...].astype(BF16), dn, preferred_element_type=F32)

        def finish(r):
            if has_add:
                r = r + add_ref[...].astype(F32)
            o_ref[...] = r.astype(o_ref.dtype)

        if nk == 1:
            finish(part)
        else:
            acc_ref = refs[-1]
            k = pl.program_id(2)

            @pl.when(k == 0)
            def _():
                acc_ref[...] = part

            @pl.when((k > 0) & (k < nk - 1))
            def _():
                acc_ref[...] += part

            @pl.when(k == nk - 1)
            def _():
                finish(acc_ref[...] + part)

    in_specs = [a_spec, b_spec] + ([o_plain] if has_add else [])
    args = (a, b) + ((add,) if has_add else ())
    return pl.pallas_call(
        body, name=name, out_shape=out_shape, grid=(M // tm, N // tn, nk), in_specs=in_specs, out_specs=o_spec,
        scratch_shapes=[pltpu.VMEM((tm, tn), F32)] if nk > 1 else [],
        compiler_params=_cparams(("parallel", "parallel", "arbitrary")),
    )(*args)


def selection_matrices(src_of_new, n_shard):
    src_np = np.asarray(src_of_new, np.int64)
    src = jnp.asarray(src_np.astype(np.int32))
    ref = jnp.arange(N_CHIPS, dtype=jnp.int32)[:, None] * n_shard + jnp.arange(n_shard, dtype=jnp.int32)[None, :]
    pm = (ref[:, :, None] == src[None, None, :]).astype(BF16)
    pmt = (src[None, :, None] == ref[:, None, :]).astype(BF16)
    tc = _pick(len(src_np), 512, LANES)
    feeds = [sorted({int(s) // n_shard for s in src_np[cb * tc:(cb + 1) * tc] if s >= 0}) for cb in range(len(src_np) // tc)]
    return pm, pmt, tc, feeds


def regroup_cols(name, g, sel, transposed=False):
    pm, _, tc, feeds = sel
    n_new = pm.shape[-1]
    if transposed:
        nn, kh = g.shape[2:]
        K = 2 * kh
        tm = _pick(kh, 512, LANES)
        hb = kh // tm
        g_spec = pl.BlockSpec((N_CHIPS, None, nn, tm), lambda c, i: (0, i // hb, 0, i % hb))
    else:
        nn = g.shape[-1]
        g = g.reshape(N_CHIPS, -1, nn)
        K = g.shape[1]
        tm = _pick(K, 512, SUBLANES_BF16)
        g_spec = pl.BlockSpec((N_CHIPS, tm, nn), lambda c, i: (0, i, 0))

    def body(g_ref, pm_ref, o_ref):
        for cb, chips in enumerate(feeds):
            @pl.when(pl.program_id(0) == cb)
            def _(chips=chips):
                acc = jnp.zeros((tm, tc), F32)
                for j in chips:
                    acc = acc + _dot(g_ref[j], pm_ref[j], "tn" if transposed else "nn")
                o_ref[...] = acc.astype(o_ref.dtype)

    return pl.pallas_call(
        body, name=name, out_shape=jax.ShapeDtypeStruct((K, n_new), BF16), grid=(n_new // tc, K // tm),
        in_specs=[g_spec, pl.BlockSpec((N_CHIPS, nn, tc), lambda c, i: (0, 0, c))],
        out_specs=pl.BlockSpec((tm, tc), lambda c, i: (i, c)), compiler_params=_cparams(("parallel", "parallel")),
    )(g, pm)


def ungroup_cols(name, dw, sel, transposed=False):
    pm, pmt, tc, feeds = sel
    K, n_new = dw.shape
    nn = pmt.shape[-1]
    kh = K // 2
    tm = _pick(kh, 512, LANES if transposed else SUBLANES_BF16)
    hb = kh // tm
    fed_by = [[cb for cb, chips in enumerate(feeds) if j in chips] for j in range(N_CHIPS)]

    def body(dw_ref, sel_ref, o_ref):
        for j, blocks in enumerate(fed_by):
            @pl.when(pl.program_id(0) == j)
            def _(blocks=blocks):
                acc = jnp.zeros((nn, tm) if transposed else (tm, nn), F32)
                for cb in blocks:
                    cols = slice(cb * tc, (cb + 1) * tc)
                    if transposed:
                        acc = acc + _dot(sel_ref[:, cols], dw_ref[:, cols], "nt")
                    else:
                        acc = acc + _dot(dw_ref[:, cols], sel_ref[cols, :], "nn")
                o_ref[...] = acc.astype(o_ref.dtype)

    if transposed:
        sel_arr, sel_spec = pm, pl.BlockSpec((None, nn, n_new), lambda j, i: (j, 0, 0))
        out_shape = jax.ShapeDtypeStruct((2, N_CHIPS, nn, kh), BF16)
        out_spec = pl.BlockSpec((None, None, nn, tm), lambda j, i: (i // hb, j, 0, i % hb))
    else:
        sel_arr, sel_spec = pmt, pl.BlockSpec((None, n_new, nn), lambda j, i: (j, 0, 0))
        out_shape = jax.ShapeDtypeStruct((2, N_CHIPS, kh, nn), BF16)
        out_spec = pl.BlockSpec((None, None, tm, nn), lambda j, i: (i // hb, j, i % hb, 0))
    return pl.pallas_call(
        body, name=name, out_shape=out_shape, grid=(N_CHIPS, 2 * hb),
        in_specs=[pl.BlockSpec((tm, n_new), lambda j, i: (i, 0)), sel_spec], out_specs=out_spec,
        compiler_params=_cparams(("parallel", "parallel")),
    )(dw, sel_arr)


def rowwise(name, fn, rows, vecs, outs, sums=(), *, M):
    rows = [tuple(r) + (0,) * (4 - len(r)) for r in rows]
    nr, nv, no, ns = len(rows), len(vecs), len(outs), len(sums)
    per_row = sum(w * a.dtype.itemsize for a, w, _, _ in rows) + sum(w * jnp.dtype(d).itemsize for w, d in outs)
    tr = _pick(M, max(8, min(512, ROW_BUDGET // (2 * per_row))), 16 if M % 16 == 0 else 8)

    def body(*refs):
        i = pl.program_id(0)
        res = fn(*[r[...] for r in refs[:nr + nv]])
        o_refs = refs[nr + nv:nr + nv + no]
        s_refs = refs[nr + nv + no:]
        for ref, val in zip(o_refs, res[:no]):
            ref[...] = val.astype(ref.dtype)
        if ns:
            @pl.when(i == 0)
            def _():
                for ref in s_refs:
                    ref[...] = jnp.zeros(ref.shape, F32)

            for ref, val in zip(s_refs, res[no:]):
                ref[...] += val

    in_specs = [pl.BlockSpec((tr, w), functools.partial(lambda i, cb, rb: (i + rb, cb), cb=off // w, rb=roff // tr)) for _, w, off, roff in rows]
    for _, w, off, roff in rows:
        assert off % w == 0 and roff % tr == 0
    in_specs += [pl.BlockSpec(v.shape, functools.partial(lambda i, nd: (0,) * nd, nd=v.ndim)) for v in vecs]
    out_specs = [pl.BlockSpec((tr, w), lambda i: (i, 0)) for w, _ in outs]
    out_specs += [pl.BlockSpec((1, w), lambda i: (0, 0)) for w in sums]
    out_shape = [jax.ShapeDtypeStruct((M, w), d) for w, d in outs] + [jax.ShapeDtypeStruct((1, w), F32) for w in sums]
    return pl.pallas_call(
        body, name=name, out_shape=out_shape, grid=(M // tr,), in_specs=in_specs, out_specs=out_specs,
        compiler_params=_cparams(("arbitrary",) if ns else ("parallel",)),
    )(*[r[0] for r in rows], *vecs)


def _rms(x, g):
    xf = x.astype(F32)
    return xf * lax.rsqrt(jnp.mean(xf * xf, axis=-1, keepdims=True) + NORM_EPS) * g


def _gelu(y):
    return 0.5 * y * (1.0 + jnp.tanh(math.sqrt(2.0 / math.pi) * (y + 0.044715 * (y * y * y))))


def _colsum(v):
    return jnp.sum(v, axis=0, keepdims=True)


def rms_fwd(name, x, width, off, g, *, M):
    return rowwise(name, lambda xb, gb: (_rms(xb, gb),), [(x, width, off)], [g], [(width, BF16)], M=M)[0]


def rms_bwd(name, x, width, off, g, dy, resid=None, *, M, out_dtype=F32):
    def fn(xb, dyb, *rest):
        gb = rest[-1]
        _, vjp = jax.vjp(_rms, xb.astype(F32), gb)
        dx, dg = vjp(dyb.astype(F32))
        if resid is not None:
            dx = dx + rest[0]
        return dx, dg

    rows = [(x, width, off), (dy, width, 0)] + ([(resid, width, 0)] if resid is not None else [])
    return rowwise(name, fn, rows, [g], [(width, out_dtype)], [width], M=M)


def lane_concat(name, parts, *, M, pad_to=None):
    width = sum(p.shape[1] for p in parts)
    pad = 0 if pad_to is None else pad_to - width

    def fn(*blocks):
        cols = [b.astype(BF16) for b in blocks]
        if pad:
            cols.append(jnp.zeros((blocks[0].shape[0], pad), BF16))
        return (jnp.concatenate(cols, axis=1),)

    return rowwise(name, fn, [(p, p.shape[1], 0) for p in parts], [], [(width + pad, BF16)], M=M)[0]


def _swap_halves(x, half):
    w = x.shape[-1]
    lane = lax.broadcasted_iota(jnp.int32, x.shape, x.ndim - 1)
    first = (lane % (2 * half)) < half
    return jnp.where(first, pltpu.roll(x, w - half, x.ndim - 1), pltpu.roll(x, half, x.ndim - 1))


def dilated_bias(M):
    delta = jnp.arange(M, dtype=jnp.int32)[:, None] - jnp.arange(M, dtype=jnp.int32)[None, :]
    w = jnp.zeros(delta.shape, F32)
    for window, dil in DIL_PATTERNS:
        ok = (delta >= 0) & (delta <= window)
        if dil > 1:
            ok = ok & ((delta & (dil - 1)) == 0)
        w = w + ok.astype(F32)
    return jnp.where(w > 0, jnp.log(jnp.maximum(w, 1.0)), NEG)


def _dot(a, b, mode):
    return lax.dot_general(a, b, _DOT_DIMS[mode], preferred_element_type=F32)


def attention_fwd(name, qa, qa_off, ka, ka_off, v, v_off, *, da, dv, pairs, scale, M, qb=None, qb_off=0, kb=None, bias=None):
    tq = min(512, M)
    tk = min(1024, M)
    has_b = qb is not None
    has_bias = bias is not None
    dr = MLA_ROPE

    def body(*refs):
        refs = list(refs)
        bias_ref = refs.pop(3) if has_bias else None
        if has_b:
            qa_ref, ka_ref, v_ref, qb_ref, kb_ref, o_ref, lse_ref = refs
        else:
            qa_ref, ka_ref, v_ref, o_ref, lse_ref = refs
        i = pl.program_id(1)
        t0 = i * tq
        nkb = (t0 + tq + tk - 1) // tk
        n_full = nkb if has_bias else t0 // tk
        q1s = [qa_ref[:, hh * da:(hh + 1) * da].astype(BF16) for hh in range(2)]
        q2s = [qb_ref[:, hh * dr:(hh + 1) * dr].astype(BF16) if has_b else None for hh in range(2)]

        def step(kbi, carry, masked):
            ks = pl.multiple_of(kbi * tk, tk)
            k2 = kb_ref[pl.ds(ks, tk), 0:dr].astype(BF16) if has_b else None
            if has_bias:
                extra = bias_ref[:, pl.ds(ks, tk)]
            elif masked:
                delta = (t0 + lax.broadcasted_iota(jnp.int32, (tq, tk), 0)) - (ks + lax.broadcasted_iota(jnp.int32, (tq, tk), 1))
            new = []
            for hh, (m, l, acc) in enumerate(carry):
                k1 = ka_ref[pl.ds(ks, tk), hh * da:(hh + 1) * da].astype(BF16)
                s = _dot(q1s[hh], k1, "nt")
                if has_b:
                    s = s + _dot(q2s[hh], k2, "nt")
                s = s * scale
                if has_bias:
                    s = s + extra
                elif masked:
                    s = jnp.where(delta >= 0, s, NEG)
                m_new = jnp.maximum(m, jnp.max(s, axis=1, keepdims=True))
                alpha = jnp.exp(m - m_new)
                p = jnp.exp(s - m_new)
                l = alpha * l + jnp.sum(p, axis=1, keepdims=True)
                vv = v_ref[pl.ds(ks, tk), hh * dv:(hh + 1) * dv].astype(BF16)
                acc = alpha * acc + _dot(p.astype(BF16), vv, "nn")
                new.append((m_new, l, acc))
            return tuple(new)

        carry = tuple((jnp.full((tq, 1), NEG, F32), jnp.zeros((tq, 1), F32), jnp.zeros((tq, dv), F32)) for _ in range(2))
        carry = lax.fori_loop(0, n_full, functools.partial(step, masked=False), carry)
        carry = lax.fori_loop(n_full, nkb, functools.partial(step, masked=True), carry)
        o_parts = [acc / l for _, l, acc in carry]
        lse_parts = [m + jnp.log(l) for m, l, _ in carry]
        o_ref[...] = jnp.concatenate(o_parts, axis=1)
        lane = lax.broadcasted_iota(jnp.int32, (tq, LANES), 1)
        lse_ref[...] = jnp.where(lane == 0, lse_parts[0], jnp.where(lane == 1, lse_parts[1], 0.0))

    assert qa_off % (2 * da) == 0 and ka_off % (2 * da) == 0 and v_off % (2 * dv) == 0
    in_specs = [
        pl.BlockSpec((tq, 2 * da), lambda hp, i: (i, qa_off // (2 * da) + hp)),
        pl.BlockSpec((M, 2 * da), lambda hp, i: (0, ka_off // (2 * da) + hp)),
        pl.BlockSpec((M, 2 * dv), lambda hp, i: (0, v_off // (2 * dv) + hp)),
    ]
    args = [qa, ka, v]
    if has_bias:
        in_specs.append(pl.BlockSpec((tq, M), lambda hp, i: (i, 0)))
        args.append(bias)
    if has_b:
        assert qb_off % LANES == 0
        in_specs += [pl.BlockSpec((tq, LANES), lambda hp, i: (i, qb_off // LANES + hp)),
                     pl.BlockSpec((M, LANES), lambda hp, i: (0, 0))]
        args += [qb, kb]
    out_specs = [pl.BlockSpec((tq, 2 * dv), lambda hp, i: (i, hp)),
                 pl.BlockSpec((None, tq, LANES), lambda hp, i: (hp, i, 0))]
    out_shape = [jax.ShapeDtypeStruct((M, pairs * 2 * dv), F32), jax.ShapeDtypeStruct((pairs, M, LANES), F32)]
    return pl.pallas_call(
        body, name=name, out_shape=out_shape, grid=(pairs, M // tq), in_specs=in_specs, out_specs=out_specs,
        compiler_params=_cparams(("parallel", "arbitrary")),
    )(*args)


def attention_bwd(name, qa, qa_off, ka, ka_off, v, v_off, o, do, lse, *, da, dv, pairs, scale, M,
                  qb=None, qb_off=0, kb=None, bias=None, tk_cap=512):
    tq = min(512, M)
    tk = min(tk_cap, M)
    has_b = qb is not None
    has_bias = bias is not None
    dr = MLA_ROPE

    def body(*refs):
        refs = list(refs)
        bias_ref = refs.pop(6) if has_bias else None
        if has_b:
            qa_ref, ka_ref, v_ref, o_ref, do_ref, lse_ref, qb_ref, kb_ref, dqa_ref, dka_ref, dv_ref, dqb_ref, dkb_ref = refs
        else:
            qa_ref, ka_ref, v_ref, o_ref, do_ref, lse_ref, dqa_ref, dka_ref, dv_ref = refs
        hp = pl.program_id(0)
        i = pl.program_id(1)
        t0 = i * tq
        nkb = (t0 + tq + tk - 1) // tk
        n_full = nkb if has_bias else t0 // tk

        @pl.when(i == 0)
        def _():
            dka_ref[...] = jnp.zeros(dka_ref.shape, F32)
            dv_ref[...] = jnp.zeros(dv_ref.shape, F32)

        if has_b:
            @pl.when((i == 0) & (hp == 0))
            def _():
                dkb_ref[...] = jnp.zeros(dkb_ref.shape, F32)

        q1s = [qa_ref[:, hh * da:(hh + 1) * da].astype(BF16) for hh in range(2)]
        q2s = [qb_ref[:, hh * dr:(hh + 1) * dr].astype(BF16) if has_b else None for hh in range(2)]
        do_bfs = [do_ref[:, hh * dv:(hh + 1) * dv].astype(BF16) for hh in range(2)]
        rowdots = [jnp.sum(do_ref[:, hh * dv:(hh + 1) * dv] * o_ref[:, hh * dv:(hh + 1) * dv], axis=1, keepdims=True) for hh in range(2)]
        lses = [lse_ref[:, hh:hh + 1] for hh in range(2)]

        def step(kbi, carry, masked):
            ks = pl.multiple_of(kbi * tk, tk)
            k2 = kb_ref[pl.ds(ks, tk), 0:dr].astype(BF16) if has_b else None
            if has_bias:
                extra = bias_ref[:, pl.ds(ks, tk)]
            elif masked:
                delta = (t0 + lax.broadcasted_iota(jnp.int32, (tq, tk), 0)) - (ks + lax.broadcasted_iota(jnp.int32, (tq, tk), 1))
            new, dkb_part = [], None
            for hh, (dq1, dq2) in enumerate(carry):
                k1 = ka_ref[pl.ds(ks, tk), hh * da:(hh + 1) * da].astype(BF16)
                s = _dot(q1s[hh], k1, "nt")
                if has_b:
                    s = s + _dot(q2s[hh], k2, "nt")
                s = s * scale
                if has_bias:
                    s = s + extra
                elif masked:
                    s = jnp.where(delta >= 0, s, NEG)
                p = jnp.exp(s - lses[hh])
                vv = v_ref[pl.ds(ks, tk), hh * dv:(hh + 1) * dv].astype(BF16)
                dp = _dot(do_bfs[hh], vv, "nt")
                ds = (p * (dp - rowdots[hh]) * scale).astype(BF16)
                dq1 = dq1 + _dot(ds, k1, "nn")
                dka_ref[pl.ds(ks, tk), hh * da:(hh + 1) * da] += _dot(ds, q1s[hh], "tn")
                dv_ref[pl.ds(ks, tk), hh * dv:(hh + 1) * dv] += _dot(p.astype(BF16), do_bfs[hh], "tn")
                if has_b:
                    dq2 = dq2 + _dot(ds, k2, "nn")
                    part = _dot(ds, q2s[hh], "tn")
                    dkb_part = part if dkb_part is None else dkb_part + part
                new.append((dq1, dq2))
            if has_b:
                dkb_ref[pl.ds(ks, tk), 0:dr] += dkb_part
            return tuple(new)

        carry = tuple((jnp.zeros((tq, da), F32), jnp.zeros((tq, dr), F32)) for _ in range(2))
        carry = lax.fori_loop(0, n_full, functools.partial(step, masked=False), carry)
        carry = lax.fori_loop(n_full, nkb, functools.partial(step, masked=True), carry)
        dqa_ref[...] = jnp.concatenate([c[0] for c in carry], axis=1).astype(dqa_ref.dtype)
        if has_b:
            dqb_ref[...] = jnp.concatenate([c[1] for c in carry], axis=1).astype(dqb_ref.dtype)

    in_specs = [
        pl.BlockSpec((tq, 2 * da), lambda hp, i: (i, qa_off // (2 * da) + hp)),
        pl.BlockSpec((M, 2 * da), lambda hp, i: (0, ka_off // (2 * da) + hp)),
        pl.BlockSpec((M, 2 * dv), lambda hp, i: (0, v_off // (2 * dv) + hp)),
        pl.BlockSpec((tq, 2 * dv), lambda hp, i: (i, hp)),
        pl.BlockSpec((tq, 2 * dv), lambda hp, i: (i, hp)),
        pl.BlockSpec((None, tq, LANES), lambda hp, i: (hp, i, 0)),
    ]
    args = [qa, ka, v, o, do, lse]
    if has_bias:
        in_specs.append(pl.BlockSpec((tq, M), lambda hp, i: (i, 0)))
        args.append(bias)
    out_specs = [pl.BlockSpec((tq, 2 * da), lambda hp, i: (i, hp)),
                 pl.BlockSpec((M, 2 * da), lambda hp, i: (0, hp)),
                 pl.BlockSpec((M, 2 * dv), lambda hp, i: (0, hp))]
    out_shape = [jax.ShapeDtypeStruct((M, pairs * 2 * da), BF16),
                 jax.ShapeDtypeStruct((M, pairs * 2 * da), F32),
                 jax.ShapeDtypeStruct((M, pairs * 2 * dv), F32)]
    if has_b:
        in_specs += [pl.BlockSpec((tq, LANES), lambda hp, i: (i, qb_off // LANES + hp)),
                     pl.BlockSpec((M, LANES), lambda hp, i: (0, 0))]
        args += [qb, kb]
        out_specs += [pl.BlockSpec((tq, LANES), lambda hp, i: (i, hp)), pl.BlockSpec((M, LANES), lambda hp, i: (0, 0))]
        out_shape += [jax.ShapeDtypeStruct((M, pairs * LANES), F32), jax.ShapeDtypeStruct((M, LANES), F32)]
    return pl.pallas_call(
        body, name=name, out_shape=out_shape, grid=(pairs, M // tq), in_specs=in_specs, out_specs=out_specs,
        compiler_params=_cparams(("arbitrary", "arbitrary")),
    )(*args)


def ssm_scan(name, xcat, acat, *, M, reverse=False, hcat=None):
    C2 = xcat.shape[1]
    cb = LANES
    tb = min(256, M)
    nblk = M // tb
    with_da = hcat is not None
    nsub = 1
    wide = nsub * 2 * cb

    def body(*refs):
        if with_da:
            x_ref, a_ref, h_ref, o_ref, da_ref, p_ref = refs
        else:
            x_ref, a_ref, o_ref, p_ref = refs
        re = [slice(s * 2 * cb, s * 2 * cb + cb) for s in range(nsub)]
        im = [slice(s * 2 * cb + cb, (s + 1) * 2 * cb) for s in range(nsub)]
        ars, ais = [a_ref[:, c] for c in re], [a_ref[:, c] for c in im]
        row = lax.broadcasted_iota(jnp.int32, (tb, cb), 0)

        def logscan(xs):
            ps = list(zip(ars, ais))
            d = 1
            while d < tb:
                shift = tb - d if reverse else d
                keep = (row < tb - d) if reverse else (row >= d)
                nxt = []
                for (xr, xi), (pr, pi) in zip(xs, ps):
                    sr = jnp.where(keep, pltpu.roll(xr, shift, 0), 0.0)
                    si = jnp.where(keep, pltpu.roll(xi, shift, 0), 0.0)
                    nxt.append((xr + pr * sr - pi * si, xi + pr * si + pi * sr))
                xs = nxt
                ps = [(pr * pr - pi * pi, 2.0 * pr * pi) for pr, pi in ps]
                d *= 2
            return xs

        seed = row == (tb - 1 if reverse else 0)
        for s, (p0r, p0i) in enumerate(logscan([(jnp.where(seed, ar, 0.0), jnp.where(seed, ai, 0.0)) for ar, ai in zip(ars, ais)])):
            p_ref[:, re[s]] = p0r
            p_ref[:, im[s]] = p0i
        sub = lax.broadcasted_iota(jnp.int32, (8, cb), 0)
        edge = 0 if reverse else tb - 8
        pick = sub == (0 if reverse else 7)

        def blk(b, carry):
            bb = (nblk - 1 - b) if reverse else b
            t0 = pl.multiple_of(bb * tb, tb)
            te = pl.multiple_of(t0 + edge, 8)
            hs = logscan([(x_ref[pl.ds(t0, tb), re[s]], x_ref[pl.ds(t0, tb), im[s]]) for s in range(nsub)])
            new = []
            for s, ((hr, hi), (cr, ci)) in enumerate(zip(hs, carry)):
                pr, pi = p_ref[:, re[s]], p_ref[:, im[s]]
                o_ref[pl.ds(t0, tb), re[s]] = hr + pr * cr - pi * ci
                o_ref[pl.ds(t0, tb), im[s]] = hi + pr * ci + pi * cr
                new.append((jnp.sum(jnp.where(pick, o_ref[pl.ds(te, 8), re[s]], 0.0), axis=0, keepdims=True),
                            jnp.sum(jnp.where(pick, o_ref[pl.ds(te, 8), im[s]], 0.0), axis=0, keepdims=True)))
            return tuple(new)

        lax.fori_loop(0, nblk, blk, tuple((jnp.zeros((1, cb), F32), jnp.zeros((1, cb), F32)) for _ in range(nsub)))
        if with_da:
            first = lax.broadcasted_iota(jnp.int32, (M, cb), 0) >= 1
            for s in range(nsub):
                hpr = jnp.where(first, pltpu.roll(h_ref[:, re[s]], 1, 0), 0.0)
                hpi = jnp.where(first, pltpu.roll(h_ref[:, im[s]], 1, 0), 0.0)
                lr, li = o_ref[:, re[s]], o_ref[:, im[s]]
                da_ref[:, re[s]] = _colsum(lr * hpr + li * hpi)
                da_ref[:, im[s]] = _colsum(li * hpr - lr * hpi)

    blk_spec = pl.BlockSpec((M, wide), lambda j: (0, j))
    vec_spec = pl.BlockSpec((1, wide), lambda j: (0, j))
    in_specs = [blk_spec, vec_spec] + ([blk_spec] if with_da else [])
    out_specs = [blk_spec] + ([vec_spec] if with_da else [])
    out_shape = [jax.ShapeDtypeStruct((M, C2), F32)] + ([jax.ShapeDtypeStruct((1, C2), F32)] if with_da else [])
    args = [xcat, acat] + ([hcat] if with_da else [])
    res = pl.pallas_call(
        body, name=name, out_shape=out_shape, grid=(C2 // wide,), in_specs=in_specs, out_specs=out_specs,
        scratch_shapes=[pltpu.VMEM((tb, wide), F32)], compiler_params=_cparams(("parallel",)),
    )(*args)
    return res if with_da else res[0]


def ssm_diag_rows(name, m):
    assert LANES == 2 * SSM_STATE
    gp, c2 = m.shape
    groups = gp // SSM_GROUP
    st = SSM_STATE

    def body(m_ref, re_ref, im_ref):
        x = m_ref[...]
        odd = pl.program_id(0) % 2

        @pl.when(odd == 0)
        def _():
            re_ref[:, 0:st] = x[:, 0:st]
            im_ref[:, 0:st] = x[:, LANES:LANES + st]

        @pl.when(odd == 1)
        def _():
            re_ref[:, st:LANES] = x[:, st:LANES]
            im_ref[:, st:LANES] = x[:, LANES + st:2 * LANES]

    out = jax.ShapeDtypeStruct((SSM_GROUP, c2 // 2), F32)
    o_spec = pl.BlockSpec((SSM_GROUP, LANES), lambda g: (0, g // 2))
    return pl.pallas_call(
        body, name=name, out_shape=[out, out], grid=(groups,), in_specs=[pl.BlockSpec((SSM_GROUP, 2 * LANES), lambda g: (g, g // 2))],
        out_specs=[o_spec, o_spec], compiler_params=_cparams(("arbitrary",)),
    )(m)


def _ssm_param_fn(a_re, a_im, ldt, b_re, b_im):
    lr, li = jnp.minimum(a_re, -1e-4), a_im
    dt = jnp.exp(ldt)
    e, ang = jnp.exp(lr * dt), li * dt
    ar, ai = e * jnp.cos(ang), e * jnp.sin(ang)
    den = lr * lr + li * li
    nr, ni = ar - 1.0, ai
    cr, ci = (nr * lr + ni * li) / den, (ni * lr - nr * li) / den
    return ar, ai, cr * b_re - ci * b_im, cr * b_im + ci * b_re


def _whole(shape):
    return pl.BlockSpec(shape, functools.partial(lambda nd: (0,) * nd, nd=len(shape)))


def ssm_param_fwd(name, a_re, a_im, ldt, b_re, b_im):
    def body(*refs):
        res = _ssm_param_fn(*[r[...] for r in refs[:5]])
        for ref, val in zip(refs[5:], res):
            ref[...] = val

    ins = [a_re, a_im, ldt, b_re, b_im]
    outs = [a_re, a_re, b_re, b_re]
    return pl.pallas_call(
        body, name=name, out_shape=[jax.ShapeDtypeStruct(t.shape, F32) for t in outs],
        in_specs=[_whole(t.shape) for t in ins], out_specs=[_whole(t.shape) for t in outs], compiler_params=_cparams(),
    )(*ins)


def ssm_param_bwd(name, a_re, a_im, ldt, b_re, b_im, d_ar, d_ai, d_bbr, d_bbi):
    def body(*refs):
        _, vjp = jax.vjp(_ssm_param_fn, *[r[...] for r in refs[:5]])
        res = vjp(tuple(r[...] for r in refs[5:9]))
        for ref, val in zip(refs[9:], res):
            ref[...] = val

    ins = [a_re, a_im, ldt, b_re, b_im, d_ar, d_ai, d_bbr, d_bbi]
    outs = [a_re, a_im, ldt, b_re, b_im]
    return pl.pallas_call(
        body, name=name, out_shape=[jax.ShapeDtypeStruct(t.shape, F32) for t in outs],
        in_specs=[_whole(t.shape) for t in ins], out_specs=[_whole(t.shape) for t in outs], compiler_params=_cparams(),
    )(*ins)


ANY = pl.BlockSpec(memory_space=pl.ANY)


def _place():
    x, y, c = lax.axis_index("x"), lax.axis_index("y"), lax.axis_index("c")
    chips = [(1 - x, y), (x, 1 - y), (1 - x, 1 - y)]
    return x, y, c, chips


def cast_into_slot(name, w, kind, l, after):
    K, nn = w.shape[-2:]
    hr, hc = (K // 2, nn) if kind == "cols" else (K, nn // 2)
    tr = _pick(hr, max(16, min(512, ROW_BUDGET // (2 * hc * 6))), SUBLANES_BF16)
    nb = hr // tr

    def body(w_ref, after_ref, o_ref, after_out):
        o_ref[...] = w_ref[...].astype(BF16)

    if kind == "cols":
        in_spec = pl.BlockSpec((None, tr, hc), lambda h, i: (l, h * nb + i, 0))
    else:
        in_spec = pl.BlockSpec((None, tr, hc), lambda h, i: (l, i, h))
    return pl.pallas_call(
        body, name=name, out_shape=[jax.ShapeDtypeStruct((N_CHIPS, 2, hr, hc), BF16), jax.ShapeDtypeStruct(after.shape, after.dtype)],
        grid=(2, nb), in_specs=[in_spec, ANY],
        out_specs=[pl.BlockSpec((None, None, tr, hc), lambda h, i: (2 * lax.axis_index("x") + lax.axis_index("y"), h, i, 0)), ANY],
        input_output_aliases={1: 1}, compiler_params=_cparams(("arbitrary", "arbitrary")),
    )(w, after)


HBM_SPEC = pl.BlockSpec(memory_space=pltpu.HBM)
SEM_SPEC = pl.BlockSpec(memory_space=pltpu.SEMAPHORE)
SPLIT_PARAMS = pltpu.CompilerParams(has_side_effects=pltpu.SideEffectType.DATAFLOW_SIDE_EFFECTING)


def _in_hbm(t):
    return pltpu.with_memory_space_constraint(t, pltpu.HBM)


def split_start(name, plan, n, bufs, fresh, carrier):
    nb, nf = len(bufs), len(fresh)

    def body(*refs):
        outs = refs[nb + 1:]
        for i, (s, d, dev) in enumerate(plan(list(outs[2:2 + nb + nf]))):
            pltpu.make_async_remote_copy(src_ref=s, dst_ref=d, send_sem=outs[0].at[i], recv_sem=outs[1].at[i],
                                         device_id=dev, device_id_type=MESH).start()

    hbm = lambda t: pltpu.HBM(t.shape, t.dtype)
    res = pl.pallas_call(
        body, name=name,
        out_shape=(pltpu.SemaphoreType.DMA((n,)), pltpu.SemaphoreType.DMA((n,)), *[hbm(t) for t in bufs], *[hbm(t) for t in fresh], hbm(carrier)),
        in_specs=[HBM_SPEC] * (nb + 1), out_specs=(SEM_SPEC, SEM_SPEC) + (HBM_SPEC,) * (nb + nf + 1),
        input_output_aliases={**{i: 2 + i for i in range(nb)}, nb: 2 + nb + nf}, compiler_params=SPLIT_PARAMS,
    )(*[_in_hbm(t) for t in bufs], _in_hbm(carrier))
    return (res[0], res[1]), list(res[2:2 + nb]), list(res[2 + nb:2 + nb + nf]), res[2 + nb + nf]


def split_wait(name, plan, sems, bufs, carrier):
    nb = len(bufs)

    def body(*refs):
        for i, (s, d, dev) in enumerate(plan(list(refs[:nb]))):
            cp = pltpu.make_async_remote_copy(src_ref=s, dst_ref=d, send_sem=refs[nb].at[i], recv_sem=refs[nb + 1].at[i],
                                              device_id=dev, device_id_type=MESH)
            cp.wait_send()
            cp.wait_recv()

    hbm = lambda t: pltpu.HBM(t.shape, t.dtype)
    res = pl.pallas_call(
        body, name=name, out_shape=(*[hbm(t) for t in bufs], hbm(carrier)),
        in_specs=[HBM_SPEC] * nb + [SEM_SPEC, SEM_SPEC, HBM_SPEC], out_specs=(HBM_SPEC,) * (nb + 1),
        input_output_aliases={**{i: i for i in range(nb)}, nb + 2: nb}, compiler_params=SPLIT_PARAMS,
    )(*bufs, sems[0], sems[1], carrier)
    return list(res[:nb]), res[nb]


def _me_sib_chips():
    x, y, c, chips = _place()
    return 2 * x + y, c, (x, y, 1 - c), chips


def plan_gather_ici(refs):
    me, c, _, chips = _me_sib_chips()
    return [(r.at[me, c], r.at[me, c], (chip[0], chip[1], c)) for r in refs for chip in chips]


def plan_gather_pass(refs):
    _, c, sib, chips = _me_sib_chips()
    return [(r.at[2 * chip[0] + chip[1], c], r.at[2 * chip[0] + chip[1], c], sib) for r in refs for chip in chips]


def plan_pair(n_arrays):
    def plan(refs):
        _, c, sib, _ = _me_sib_chips()
        return [(refs[a].at[1 - c], refs[n_arrays + a], sib) for a in range(n_arrays)]
    return plan


def plan_chips(n_arrays):
    def plan(refs):
        _, c, _, chips = _me_sib_chips()
        return [(refs[a].at[2 * chip[0] + chip[1]], refs[n_arrays + a].at[k], (chip[0], chip[1], c))
                for a in range(n_arrays) for k, chip in enumerate(chips)]
    return plan


def plan_share(n_arrays):
    def plan(refs):
        _, _, sib, _ = _me_sib_chips()
        return [(refs[a], refs[n_arrays + a], sib) for a in range(n_arrays)]
    return plan


def swap_with_sibling(name, src, pick_other_half):
    shape = src.shape[1:] if pick_other_half else src.shape

    def body(src_ref, out_ref, ssem, rsem):
        x, y, c, _ = _place()
        cp = pltpu.make_async_remote_copy(src_ref=src_ref.at[1 - c] if pick_other_half else src_ref, dst_ref=out_ref,
                                          send_sem=ssem, recv_sem=rsem, device_id=(x, y, 1 - c), device_id_type=MESH)
        cp.start()
        cp.wait()

    return pl.pallas_call(
        body, name=name, out_shape=jax.ShapeDtypeStruct(shape, src.dtype), in_specs=[ANY], out_specs=ANY,
        scratch_shapes=[pltpu.SemaphoreType.DMA(()), pltpu.SemaphoreType.DMA(())],
    )(src)


def exchange_chips(name, src, per_chip):
    shape = src.shape[1:] if per_chip else src.shape

    def body(src_ref, out_ref, send_sems, recv_sems):
        x, y, c, chips = _place()
        cps = []
        for k, chip in enumerate(chips):
            s = src_ref.at[2 * chip[0] + chip[1]] if per_chip else src_ref
            cps.append(pltpu.make_async_remote_copy(src_ref=s, dst_ref=out_ref.at[k], send_sem=send_sems.at[k], recv_sem=recv_sems.at[k],
                                                    device_id=(chip[0], chip[1], c), device_id_type=MESH))
        for cp in cps:
            cp.start()
        for cp in cps:
            cp.wait()

    return pl.pallas_call(
        body, name=name, out_shape=jax.ShapeDtypeStruct((3,) + shape, src.dtype), in_specs=[ANY], out_specs=ANY,
        scratch_shapes=[pltpu.SemaphoreType.DMA((3,)), pltpu.SemaphoreType.DMA((3,))],
    )(src)


def pair_sum(name, p, got):
    _, _, rh, cw = p.shape
    tr = _pick(rh, max(16, min(512, ROW_BUDGET // (2 * cw * 10))), SUBLANES_BF16)

    def body(p_ref, got_ref, s_ref, own_ref):
        j = pl.program_id(1)
        tot = p_ref[...].astype(F32) + got_ref[...].astype(F32)
        s_ref[...] = tot.astype(BF16)

        @pl.when(j == 2 * lax.axis_index("x") + lax.axis_index("y"))
        def _():
            own_ref[...] = tot

    return pl.pallas_call(
        body, name=name, grid=(rh // tr, N_CHIPS),
        in_specs=[pl.BlockSpec((None, None, tr, cw), lambda i, j: (lax.axis_index("c"), j, i, 0)),
                  pl.BlockSpec((None, tr, cw), lambda i, j: (j, i, 0))],
        out_specs=[pl.BlockSpec((None, tr, cw), lambda i, j: (j, i, 0)),
                   pl.BlockSpec((tr, cw), lambda i, j: (i, 0))],
        out_shape=[jax.ShapeDtypeStruct((N_CHIPS, rh, cw), BF16), jax.ShapeDtypeStruct((rh, cw), F32)],
        compiler_params=_cparams(("arbitrary", "arbitrary")),
    )(p, got)


def chips_sum(name, own, parts):
    rh, cw = own.shape
    parts = parts.reshape(3 * rh, cw)
    return rowwise(name, lambda o, a, b, c: (((o + a.astype(F32)) + b.astype(F32)) + c.astype(F32),),
                   [(own, cw, 0), (parts, cw, 0, 0), (parts, cw, 0, rh), (parts, cw, 0, 2 * rh)], [], [(cw, F32)], M=rh)[0]


def all_reduce_small(buf):
    r = buf.shape[0]
    got = swap_with_sibling("ar_pair", buf, False)
    chip = rowwise("ar_pairsum", lambda a, b: (a + b,), [(buf, LANES, 0), (got, LANES, 0)], [], [(LANES, F32)], M=r)[0]
    parts = exchange_chips("ar_chips", chip, False).reshape(3 * r, LANES)
    return rowwise("ar_sum", lambda o, fx, fy, fxy: ((o + fy) + (fx + fxy),),
                   [(chip, LANES, 0), (parts, LANES, 0, 0), (parts, LANES, 0, r), (parts, LANES, 0, 2 * r)], [], [(LANES, F32)], M=r)[0]


def _adam_fn(w, g, m, v):
    m = ADAM_B1 * m + (1.0 - ADAM_B1) * g
    v = ADAM_B2 * v + (1.0 - ADAM_B2) * (g * g)
    m_hat = m / (1.0 - ADAM_B1 ** ADAM_STEP)
    v_hat = v / (1.0 - ADAM_B2 ** ADAM_STEP)
    return -ADAM_LR * (m_hat / (jnp.sqrt(v_hat) + ADAM_EPS) + ADAM_WD * w), m, v


def adamw(name, w, g, m, v):
    r, cw = w.shape
    return rowwise(name, _adam_fn, [(t, cw, 0) for t in (w, g, m, v)], [], [(cw, F32)] * 3, M=r)


def adamw_layers(name, w, m, v, mines, theirs, kind):
    L, K, nn = w.shape
    hr, hc = (K // 2, nn) if kind == "cols" else (K, nn // 2)
    tr = _pick(hr, max(8, min(256, MM_BUDGET // (2 * hc * 4 * (7 + 2 * L)))), 8)
    nb = hr // tr

    def body(*refs):
        w_ref, m_ref, v_ref = refs[:3]
        outs = refs[3 + 2 * L:]
        l, mine_here = pl.program_id(0), pl.program_id(1) == lax.axis_index("c")
        g = jnp.zeros((tr, hc), F32)
        for ll in range(L):
            g = jnp.where(l == ll, jnp.where(mine_here, refs[3 + ll][...], refs[3 + L + ll][...]), g)
        outs[0][...] = g
        outs[1][...], outs[2][...], outs[3][...] = _adam_fn(w_ref[...], g, m_ref[...], v_ref[...])

    if kind == "cols":
        full = pl.BlockSpec((None, tr, hc), lambda l, h, i: (l, h * nb + i, 0))
    else:
        full = pl.BlockSpec((None, tr, hc), lambda l, h, i: (l, i, h))
    def half_spec(ll, mine):
        def imap(l, h, i):
            here = (l == ll) & ((h == lax.axis_index("c")) == mine)
            return (jnp.where(here, i, 0), 0)
        return pl.BlockSpec((tr, hc), imap)

    halves = [half_spec(ll, True) for ll in range(L)] + [half_spec(ll, False) for ll in range(L)]
    return pl.pallas_call(
        body, name=name, grid=(L, 2, nb), in_specs=[full] * 3 + halves, out_specs=[full] * 4,
        out_shape=[jax.ShapeDtypeStruct((L, K, nn), F32)] * 4, compiler_params=_cparams(("parallel", "parallel", "parallel")),
    )(w, m, v, *mines, *theirs)


class Dims:
    def __init__(self, x, g_q, g_kv, g_out_mla, g_out_ssm, g_out_dil, ff):
        self.M, self.D = x.shape[-2], x.shape[-1]
        self.QL, self.KVL = g_q.shape[-1], g_kv.shape[-1]
        self.MW, self.SW, self.DW = g_out_mla.shape[-1], g_out_ssm.shape[-1], g_out_dil.shape[-1]
        self.H = self.MW // MLA_V
        self.FF = ff
        self.G = self.SW // SSM_GROUP
        self.C = self.G * SSM_STATE
        self.o_cq, self.o_u = 0, self.QL
        self.o_qd = self.o_u + self.SW
        self.o_kd = self.o_qd + self.DW
        self.o_vd = self.o_kd + self.DW
        self.o_ckv = self.o_vd + self.DW
        self.o_kr = self.o_ckv + self.KVL
        self.PW = -(-(self.o_kr + MLA_ROPE) // (4 * LANES)) * (4 * LANES)
        assert self.o_u % self.SW == 0 and self.o_qd % LANES == 0 and self.o_ckv % self.KVL == 0 and self.o_kr % LANES == 0
        assert self.H % 2 == 0 and self.DW % LANES == 0 and self.C % LANES == 0
        self.QW = self.H * (MLA_NOPE + MLA_ROPE)
        self.KVW = self.H * (MLA_NOPE + MLA_V)
        sizes = [self.QL, self.KVL, MLA_ROPE, self.SW, self.DW, self.DW, self.DW]
        starts = np.concatenate([[0], np.cumsum(sizes)[:-1]])
        self.ref_cols = {n: (int(s), int(z)) for n, s, z in zip(["cq", "ckv", "kr", "u", "qd", "kd", "vd"], starts, sizes)}
        self.INW = int(sum(sizes))
        self.new_order = ["cq", "u", "qd", "kd", "vd", "ckv", "kr"]
        src = np.concatenate([np.arange(self.ref_cols[n][0], self.ref_cols[n][0] + self.ref_cols[n][1]) for n in self.new_order])
        self.src_in = np.concatenate([src, -np.ones(self.PW - self.INW, np.int64)])
        self.src_q = self._heads_split(self.H, MLA_NOPE, MLA_ROPE)
        self.src_kv = self._heads_split(self.H, MLA_NOPE, MLA_V)

    @staticmethod
    def _heads_split(h, d1, d2):
        first = (np.arange(h)[:, None] * (d1 + d2) + np.arange(d1)[None, :]).reshape(-1)
        second = (np.arange(h)[:, None] * (d1 + d2) + d1 + np.arange(d2)[None, :]).reshape(-1)
        return np.concatenate([first, second])


def _regroup_in(dm, w):
    parts = [w[..., dm.ref_cols[n][0]:dm.ref_cols[n][0] + dm.ref_cols[n][1]] for n in dm.new_order]
    pad = dm.PW - dm.INW
    return jnp.concatenate(parts + [jnp.zeros(w.shape[:-1] + (pad,), w.dtype)], axis=-1)


def _ungroup_in(dm, w):
    off, pieces = 0, {}
    for n in dm.new_order:
        pieces[n] = w[..., off:off + dm.ref_cols[n][1]]
        off += dm.ref_cols[n][1]
    return jnp.concatenate([pieces[n] for n in ["cq", "ckv", "kr", "u", "qd", "kd", "vd"]], axis=-1)


def _split_heads(w, h, d1):
    t = w.reshape(w.shape[:-1] + (h, -1))
    return jnp.concatenate([t[..., :d1].reshape(w.shape[:-1] + (-1,)), t[..., d1:].reshape(w.shape[:-1] + (-1,))], axis=-1)


def _merge_heads(w, h, d1):
    a = w[..., :h * d1].reshape(w.shape[:-1] + (h, d1))
    b = w[..., h * d1:].reshape(w.shape[:-1] + (h, -1))
    return jnp.concatenate([a, b], axis=-1).reshape(w.shape[:-1] + (-1,))


def _cat_cols(re, im):
    r, c = re.shape
    return jnp.stack([re.reshape(r, c // LANES, LANES), im.reshape(r, c // LANES, LANES)], axis=2).reshape(r, 2 * c)


def _uncat_cols(cat):
    r, c2 = cat.shape
    t = cat.reshape(r, c2 // (2 * LANES), 2, LANES)
    return t[:, :, 0].reshape(r, c2 // 2), t[:, :, 1].reshape(r, c2 // 2)


def _block_diag(t, g):
    _, a, b = t.shape
    eye = jnp.eye(g, dtype=bool)[:, None, :, None]
    return jnp.where(eye, t[:, :, None, :], 0).reshape(g * a, g * b)


def _diag_blocks(m, g):
    a, b = m.shape[0] // g, m.shape[1] // g
    eye = jnp.eye(g, dtype=m.dtype)[:, None, :, None]
    return jnp.sum(m.reshape(g, a, g, b) * eye, axis=2)


def _rope_tables(dm):
    half = MLA_ROPE // 2
    inv_freq = ROPE_THETA ** (-jnp.arange(half, dtype=F32) / half)
    ang = jnp.arange(dm.M, dtype=F32)[:, None] * inv_freq[None, :]
    cos = jnp.concatenate([jnp.cos(ang), jnp.cos(ang)], axis=1)
    sin = jnp.concatenate([-jnp.sin(ang), jnp.sin(ang)], axis=1)
    return jnp.tile(cos, (1, dm.H)), jnp.tile(sin, (1, dm.H)), jnp.tile(cos, (1, LANES // MLA_ROPE)), jnp.tile(sin, (1, LANES // MLA_ROPE))


def _rope(x, cos, sin):
    return x * cos + _swap_halves(x, MLA_ROPE // 2) * sin


def _rope_t(d, cos, sin):
    return d * cos + _swap_halves(d * sin, MLA_ROPE // 2)


def _ssm_layer_params(dm, a_re, a_im, log_dt, b_re, b_im):
    flat = lambda t: t.reshape(1, dm.C)
    ldt = jnp.repeat(log_dt, SSM_STATE).reshape(1, dm.C)
    bt = lambda t: jnp.transpose(t, (2, 0, 1)).reshape(SSM_GROUP, dm.C)
    return flat(a_re), flat(a_im), ldt, bt(b_re), bt(b_im)


def layer_forward(dm, l, x, lw, sp, tabs, hook_q, hook_mid):
    M, D = dm.M, dm.D
    n = lambda s: f"{s}_l{l}"
    sv = {"x_in": x}
    h1 = rms_fwd(n("rms_mix"), x, D, 0, lw["g_mix"], M=M)
    proj = matmul(n("in_proj"), h1, lw["w_in"], "nn", M=M, N=dm.PW, K=D)
    sv.update(h1=h1, proj=proj)
    cqn = rms_fwd(n("rms_q"), proj, dm.QL, dm.o_cq, lw["g_q"], M=M)
    q = matmul(n("q_up"), cqn, lw["w_uq"], "nn", M=M, N=dm.QW, K=dm.QL)
    ckvn = rms_fwd(n("rms_kv"), proj, dm.KVL, dm.o_ckv, lw["g_kv"], M=M)
    kv = matmul(n("kv_up"), ckvn, lw["w_ukv"], "nn", M=M, N=dm.KVW, K=dm.KVL, out_dtype=BF16)
    cosq, sinq, cosk, sink = tabs[:4]
    nw = dm.H * MLA_NOPE

    def rope_fn(qb, kb, cq, sq, ck, sk):
        return jnp.concatenate([qb[:, :nw], _rope(qb[:, nw:], cq, sq)], axis=1), _rope(kb, ck, sk)

    pw = dm.H * MLA_ROPE
    q_bf, kpe = rowwise(n("rope"), rope_fn, [(q, dm.QW, 0), (proj, LANES, dm.o_kr), (cosq, pw, 0), (sinq, pw, 0), (cosk, LANES, 0), (sink, LANES, 0)],
                        [], [(dm.QW, BF16), (LANES, BF16)], M=M)
    mla_scale = (MLA_NOPE + MLA_ROPE) ** -0.5
    o_mla, lse_mla = attention_fwd(n("mla_fwd"), q_bf, 0, kv, 0, kv, nw, da=MLA_NOPE, dv=MLA_V, pairs=dm.H // 2, scale=mla_scale,
                                   M=M, qb=q_bf, qb_off=nw, kb=kpe)
    sv.update(cqn=cqn, ckvn=ckvn, q_bf=q_bf, kv=kv, kpe=kpe, o_mla=o_mla, lse_mla=lse_mla)
    bu = matmul(n("ssm_bu"), proj, sp["bcat"], "nn", M=M, N=2 * dm.C, K=dm.SW, a_off=(0, dm.o_u))
    hcat = ssm_scan(n("ssm_scan"), bu, sp["acat"], M=M)
    ylin = matmul(n("ssm_y"), hcat, sp["ccat"], "nn", M=M, N=dm.SW, K=2 * dm.C)
    yg = rowwise(n("ssm_gelu"), lambda y, u, d: (_gelu(y + d * u),), [(ylin, dm.SW, 0), (proj, dm.SW, dm.o_u)], [lw["d_skip"]],
                 [(dm.SW, BF16)], M=M)[0]
    z = matmul(n("ssm_glu"), yg, lw["w_glu"], "nn", w=("cols", 2 * dm.SW // N_CHIPS), M=M, N=2 * dm.SW, K=dm.SW)
    sw = dm.SW

    def glu_fn(zb, b):
        zz = zb + b
        return (zz[:, :sw] * jax.nn.sigmoid(zz[:, sw:]),)

    o_ssm = hook_q(rowwise(n("ssm_gate"), glu_fn, [(z, 2 * sw, 0)], [lw["b_glu"]], [(sw, F32)], M=M)[0])
    sv.update(hcat=hcat, ylin=ylin, yg=yg, z=z, o_ssm=o_ssm)
    o_dil, lse_dil = attention_fwd(n("dil_fwd"), proj, dm.o_qd, proj, dm.o_kd, proj, dm.o_vd, da=DIL_HEAD, dv=DIL_HEAD, pairs=dm.DW // LANES,
                                   scale=DIL_HEAD ** -0.5, M=M, bias=tabs[4])
    sv.update(o_dil=o_dil, lse_dil=lse_dil)
    yn = rowwise(n("out_norm"), lambda a, b, c, ga, gb, gc: (jnp.concatenate([_rms(a, ga), _rms(b, gb), _rms(c, gc)], axis=1),),
                 [(o_mla, dm.MW, 0), (o_ssm, dm.SW, 0), (o_dil, dm.DW, 0)], [lw["g_out_mla"], lw["g_out_ssm"], lw["g_out_dil"]],
                 [(D, BF16)], M=M)[0]
    yn = hook_mid(yn, lw)
    x_mid = matmul(n("out_proj"), yn, lw["w_o"], "nn", w=("rows", D // N_CHIPS), M=M, N=D, K=D, add=x)
    h2 = rms_fwd(n("rms_ffn"), x_mid, D, 0, lw["g_ffn"], M=M)
    ffs = dm.FF // N_CHIPS
    gate = matmul(n("ffn_gate"), h2, lw["w_gate"], "nn", w=("cols", ffs), M=M, N=dm.FF, K=D, out_dtype=BF16)
    up = matmul(n("ffn_up"), h2, lw["w_up"], "nn", w=("cols", ffs), M=M, N=dm.FF, K=D, out_dtype=BF16)

    def act_fn(gb, ub):
        gf = gb.astype(F32)
        return (gf * jax.nn.sigmoid(gf) * ub.astype(F32),)

    act = rowwise(n("ffn_act"), act_fn, [(gate, dm.FF, 0), (up, dm.FF, 0)], [], [(dm.FF, BF16)], M=M)[0]
    x_out = matmul(n("ffn_down"), act, lw["w_down"], "nn", w=("rows",ffs), M=M, N=D, K=dm.FF, add=x_mid)
    sv.update(yn=yn, x_mid=x_mid, h2=h2, gate=gate, up=up, act=act)
    return x_out, sv


def layer_backward(dm, l, dx, lw, sp, tabs, sv, hook_a, hook_m, hook_b):
    M, D = dm.M, dm.D
    n = lambda s: f"{s}_l{l}"
    g = {}
    ffs = dm.FF // N_CHIPS
    dact = matmul(n("ffn_down_dx"), dx, lw["w_down"], "nt", w=("rows", ffs), M=M, N=dm.FF, K=D, out_dtype=BF16)
    g["w_down"] = matmul(n("ffn_down_dw"), sv["act"], dx, "tn", M=dm.FF, N=D, K=M, out_dtype=BF16, into=("rows", ffs))

    def act_bwd(gb, ub, db):
        _, vjp = jax.vjp(lambda a, b: a * jax.nn.sigmoid(a) * b, gb.astype(F32), ub.astype(F32))
        return vjp(db.astype(F32))

    dgate, dup = rowwise(n("ffn_act_bwd"), act_bwd, [(sv["gate"], dm.FF, 0), (sv["up"], dm.FF, 0), (dact, dm.FF, 0)], [],
                         [(dm.FF, BF16), (dm.FF, BF16)], M=M)
    dh2 = matmul(n("ffn_gate_dx"), dgate, lw["w_gate"], "nt", w=("cols",ffs), M=M, N=D, K=dm.FF)
    dh2 = matmul(n("ffn_up_dx"), dup, lw["w_up"], "nt", w=("cols",ffs), M=M, N=D, K=dm.FF, add=dh2)
    g["w_gate"] = matmul(n("ffn_gate_dw"), sv["h2"], dgate, "tn", M=D, N=dm.FF, K=M, out_dtype=BF16, into=("cols", ffs))
    g["w_up"] = matmul(n("ffn_up_dw"), sv["h2"], dup, "tn", M=D, N=dm.FF, K=M, out_dtype=BF16, into=("cols", ffs))
    dx_mid, g["g_ffn"] = rms_bwd(n("rms_ffn_bwd"), sv["x_mid"], D, 0, lw["g_ffn"], dh2, dx, M=M)
    dx_mid = hook_a(dx_mid, g)
    dyn = matmul(n("out_proj_dx"), dx_mid, lw["w_o"], "nt", w=("rows", D // N_CHIPS), M=M, N=D, K=D)
    g["w_o"] = matmul(n("out_proj_dw"), sv["yn"], dx_mid, "tn", M=D, N=D, K=M, out_dtype=BF16, into=("rows", D // N_CHIPS))
    mw, sw, dw = dm.MW, dm.SW, dm.DW

    def out_norm_bwd(a, b, c, dy, ga, gb, gc):
        res, sums = [], []
        for t, gg, lo, hi in ((a, ga, 0, mw), (b, gb, mw, mw + sw), (c, gc, mw + sw, mw + sw + dw)):
            _, vjp = jax.vjp(_rms, t, gg)
            dt, dg = vjp(dy[:, lo:hi])
            res.append(dt)
            sums.append(dg)
        return res + sums

    do_mla, do_ssm, do_dil, g["g_out_mla"], g["g_out_ssm"], g["g_out_dil"] = rowwise(
        n("out_norm_bwd"), out_norm_bwd, [(sv["o_mla"], mw, 0), (sv["o_ssm"], sw, 0), (sv["o_dil"], dw, 0), (dyn, D, 0)],
        [lw["g_out_mla"], lw["g_out_ssm"], lw["g_out_dil"]], [(mw, F32), (sw, F32), (dw, F32)], [mw, sw, dw], M=M)
    proj = sv["proj"]
    dqd, dkd, dvd = attention_bwd(n("dil_bwd"), proj, dm.o_qd, proj, dm.o_kd, proj, dm.o_vd, sv["o_dil"], do_dil, sv["lse_dil"],
                                  da=DIL_HEAD, dv=DIL_HEAD, pairs=dw // LANES, scale=DIL_HEAD ** -0.5, M=M, bias=tabs[4], tk_cap=1024)
    do_ssm = hook_m(do_ssm, g)
    def glu_bwd(zb, db, b):
        _, vjp = jax.vjp(lambda zz, bb: (zz + bb)[:, :sw] * jax.nn.sigmoid((zz + bb)[:, sw:]), zb, b)
        return vjp(db)

    dz, g["b_glu"] = rowwise(n("ssm_gate_bwd"), glu_bwd, [(sv["z"], 2 * sw, 0), (do_ssm, sw, 0)], [lw["b_glu"]], [(2 * sw, BF16)], [2 * sw], M=M)
    dyg = matmul(n("ssm_glu_dx"), dz, lw["w_glu"], "nt", w=("cols", 2 * sw // N_CHIPS), M=M, N=sw, K=2 * sw)
    g["w_glu"] = matmul(n("ssm_glu_dw"), sv["yg"], dz, "tn", M=sw, N=2 * sw, K=M, out_dtype=BF16, into=("cols", 2 * sw // N_CHIPS))

    def gelu_bwd(y, u, dy, d):
        _, vjp = jax.vjp(lambda yy, uu, dd: _gelu(yy + dd * uu), y, u, d)
        return vjp(dy)

    dylin, du1, g["d_skip"] = rowwise(n("ssm_gelu_bwd"), gelu_bwd, [(sv["ylin"], sw, 0), (proj, sw, dm.o_u), (dyg, sw, 0)], [lw["d_skip"]],
                                      [(sw, BF16), (sw, F32)], [sw], M=M)
    seed = matmul(n("ssm_y_dx"), dylin, sp["ccat"], "nt", M=M, N=2 * dm.C, K=sw)
    d_ccat = matmul(n("ssm_y_dw"), dylin, sv["hcat"], "tn", M=sw, N=2 * dm.C, K=M)
    lam, d_acat = ssm_scan(n("ssm_scan_bwd"), seed, sp["acat_conj"], M=M, reverse=True, hcat=sv["hcat"])
    du = matmul(n("ssm_bu_dx"), lam, sp["bcat"], "nt", M=M, N=sw, K=2 * dm.C, add=du1, out_dtype=BF16)
    d_bcat = matmul(n("ssm_bu_dw"), proj, lam, "tn", M=sw, N=2 * dm.C, K=M, a_off=(0, dm.o_u))
    g["ssm_raw"] = (d_acat, d_bcat, d_ccat)
    nw = dm.H * MLA_NOPE
    dqn, dkn, dv_, dqp, dkp = attention_bwd(n("mla_bwd"), sv["q_bf"], 0, sv["kv"], 0, sv["kv"], nw, sv["o_mla"], do_mla, sv["lse_mla"],
                                            da=MLA_NOPE, dv=MLA_V, pairs=dm.H // 2, scale=(MLA_NOPE + MLA_ROPE) ** -0.5, M=M,
                                            qb=sv["q_bf"], qb_off=nw, kb=sv["kpe"])
    cosq, sinq, cosk, sink = tabs[:4]
    pw = dm.H * MLA_ROPE
    dqp_u, dkr = rowwise(n("rope_bwd"), lambda a, b, cq, sq, ck, sk: (_rope_t(a, cq, sq), _rope_t(b, ck, sk)),
                         [(dqp, pw, 0), (dkp, LANES, 0), (cosq, pw, 0), (sinq, pw, 0), (cosk, LANES, 0), (sink, LANES, 0)], [],
                         [(pw, BF16), (LANES, BF16)], M=M)
    dq = lane_concat(n("dq_cat"), [dqn, dqp_u], M=M)
    dkv = lane_concat(n("dkv_cat"), [dkn, dv_], M=M)
    dcqn = matmul(n("q_up_dx"), dq, lw["w_uq"], "nt", M=M, N=dm.QL, K=dm.QW)
    g["w_uq"] = matmul(n("q_up_dw"), sv["cqn"], dq, "tn", M=dm.QL, N=dm.QW, K=M, out_dtype=BF16)
    dckvn = matmul(n("kv_up_dx"), dkv, lw["w_ukv"], "nt", M=M, N=dm.KVL, K=dm.KVW)
    g["w_ukv"] = matmul(n("kv_up_dw"), sv["ckvn"], dkv, "tn", M=dm.KVL, N=dm.KVW, K=M, out_dtype=BF16)
    dcq, g["g_q"] = rms_bwd(n("rms_q_bwd"), proj, dm.QL, dm.o_cq, lw["g_q"], dcqn, M=M, out_dtype=BF16)
    dckv, g["g_kv"] = rms_bwd(n("rms_kv_bwd"), proj, dm.KVL, dm.o_ckv, lw["g_kv"], dckvn, M=M, out_dtype=BF16)
    dproj = hook_b(lane_concat(n("dproj_cat"), [dcq, du, dqd, dkd, dvd, dckv, dkr], M=M, pad_to=dm.PW), g)
    dh1 = matmul(n("in_proj_dx"), dproj, lw["w_in"], "nt", M=M, N=D, K=dm.PW)
    g["w_in"] = matmul(n("in_proj_dw"), sv["h1"], dproj, "tn", M=D, N=dm.PW, K=M, out_dtype=BF16)
    dx_in, g["g_mix"] = rms_bwd(n("rms_mix_bwd"), sv["x_in"], D, 0, lw["g_mix"], dh1, dx_mid, M=M)
    return dx_in, g


def layer_params(dm, small, l):
    lw = {k: small[k][l].reshape(1, -1) for k in ("g_mix", "g_q", "g_kv", "b_glu", "g_out_mla", "g_out_ssm", "g_out_dil", "g_ffn", "d_skip")}
    raw = _ssm_layer_params(dm, small["a_re"][l], small["a_im"][l], small["log_dt"][l], small["b_re"][l], small["b_im"][l])
    ar, ai, bbr, bbi = ssm_param_fwd(f"ssm_param_l{l}", *raw)
    g_ = dm.G
    bd = lambda t: _block_diag(jnp.transpose(t.reshape(SSM_GROUP, g_, SSM_STATE), (1, 0, 2)), g_)
    cd = lambda t: _block_diag(jnp.transpose(t, (0, 2, 1)), g_)
    cre, cim = cd(small["c_re"][l]), cd(small["c_im"][l])
    sp = {"acat": _cat_cols(ar, ai), "acat_conj": _cat_cols(ar, -ai),
          "bcat": _cat_cols(bd(bbr), bd(bbi)).astype(BF16),
          "ccat": _cat_cols(cre.T, -cim.T).T.astype(BF16)}
    return lw, sp, raw


def ssm_param_grads(dm, l, g, raw):
    d_acat, d_bcat, d_ccat_t = g.pop("ssm_raw")
    d_ar, d_ai = _uncat_cols(d_acat)
    dbr, dbi = ssm_diag_rows(f"ssm_db_l{l}", d_bcat)
    dcr, dci = ssm_diag_rows(f"ssm_dc_l{l}", d_ccat_t)
    g_ = dm.G
    da_re, da_im, dldt, db_re, db_im = ssm_param_bwd(f"ssm_param_bwd_l{l}", *raw, d_ar, d_ai, dbr, dbi)
    g["a_re"], g["a_im"] = da_re.reshape(g_, SSM_STATE), da_im.reshape(g_, SSM_STATE)
    g["log_dt"] = jnp.sum(dldt.reshape(g_, SSM_STATE), axis=1)
    from_rows = lambda t: jnp.transpose(t.reshape(SSM_GROUP, g_, SSM_STATE), (1, 2, 0))
    g["b_re"], g["b_im"] = from_rows(db_re), from_rows(db_im)
    g["c_re"] = jnp.transpose(dcr.reshape(SSM_GROUP, g_, SSM_STATE), (1, 0, 2))
    g["c_im"] = -jnp.transpose(dci.reshape(SSM_GROUP, g_, SSM_STATE), (1, 0, 2))
    g["d_skip"] = g["d_skip"].reshape(g_, SSM_GROUP)


def loss_and_grad(dm, h, target, g_final):
    D = dm.D

    def loss_fn(xb, tb, gb):
        y, vjp = jax.vjp(_rms, xb, gb)
        err = y - tb
        dxb, dg = vjp(err * (1.0 / D))
        part = 0.5 * jnp.sum(jnp.mean(err * err, axis=-1, keepdims=True), axis=0, keepdims=True)
        lane = lax.broadcasted_iota(jnp.int32, (1, LANES), 1)
        return dxb, dg, jnp.where(lane == 0, part, 0.0)

    return rowwise("loss", loss_fn, [(h, D, 0), (target, D, 0)], [g_final.reshape(1, D)], [(D, F32)], [D, LANES], M=dm.M)


KIND = {"w_in": "cols", "w_uq": "cols", "w_ukv": "cols", "w_glu": "cols", "w_o": "rows", "w_gate": "cols", "w_up": "cols", "w_down": "rows"}
SHARDED = list(KIND)
TRANSPOSED = ("w_in",)
GATHER_GROUPS = {"mixer": ["w_in", "w_uq", "w_ukv", "w_glu"], "rest": ["w_o", "w_gate", "w_up", "w_down"]}
REDUCE_GROUPS = {"ffn": ["w_gate", "w_up", "w_down"], "others": ["w_o", "w_in", "w_uq", "w_ukv", "w_glu"]}
SMALL = ["g_mix", "g_q", "g_kv", "a_re", "a_im", "b_re", "b_im", "c_re", "c_im", "d_skip", "log_dt", "b_glu",
         "g_out_mla", "g_out_ssm", "g_out_dil", "g_ffn", "g_final"]
ORDER = ["g_mix", "w_in", "g_q", "w_uq", "g_kv", "w_ukv", "a_re", "a_im", "b_re", "b_im", "c_re", "c_im", "d_skip", "log_dt",
         "w_glu", "b_glu", "g_out_mla", "g_out_ssm", "g_out_dil", "w_o", "g_ffn", "w_gate", "w_up", "w_down", "g_final"]


def kernel(x, g_mix, w_in, g_q, w_uq, g_kv, w_ukv, a_re, a_im, b_re, b_im, c_re, c_im, d_skip, log_dt, w_glu, b_glu, g_out_mla, g_out_ssm, g_out_dil, w_o, g_ffn, w_gate, w_up, w_down, g_final, loss_target, m_g_mix, m_w_in, m_g_q, m_w_uq, m_g_kv, m_w_ukv, m_a_re, m_a_im, m_b_re, m_b_im, m_c_re, m_c_im, m_d_skip, m_log_dt, m_w_glu, m_b_glu, m_g_out_mla, m_g_out_ssm, m_g_out_dil, m_w_o, m_g_ffn, m_w_gate, m_w_up, m_w_down, m_g_final, v_g_mix, v_w_in, v_g_q, v_w_uq, v_g_kv, v_w_ukv, v_a_re, v_a_im, v_b_re, v_b_im, v_c_re, v_c_im, v_d_skip, v_log_dt, v_w_glu, v_b_glu, v_g_out_mla, v_g_out_ssm, v_g_out_dil, v_w_o, v_g_ffn, v_w_gate, v_w_up, v_w_down, v_g_final):
    args = locals()
    w = {k: args[k] for k in ORDER}
    mom = {k: args["m_" + k] for k in ORDER}
    var = {k: args["v_" + k] for k in ORDER}
    dm = Dims(x, g_q, g_kv, g_out_mla, g_out_ssm, g_out_dil, w_gate.shape[-1] * N_CHIPS)
    L = g_mix.shape[0]

    small = {k: w[k] for k in SMALL}
    na = len(SHARDED)
    tabs = _rope_tables(dm) + (dilated_bias(dm.M),)
    sel ={"w_in": selection_matrices(dm.src_in, w_in.shape[-1]), "w_uq": selection_matrices(dm.src_q, w_uq.shape[-1]),
           "w_ukv": selection_matrices(dm.src_kv, w_ukv.shape[-1])}

    stored = lambda t, k: jnp.swapaxes(t, 1, 2) if k in TRANSPOSED else t
    store_kind = {k: "rows" if k in TRANSPOSED else KIND[k] for k in SHARDED}
    G = [{} for _ in range(L)]
    sems = {}

    def cast_group(l, grp, car):
        for k in GATHER_GROUPS[grp]:
            G[l][k], car = cast_into_slot(f"cast_{k}_l{l}", stored(w[k], k), store_kind[k], l, car)
        return car

    def gather_stage(stage, plan):
        def start(l, grp, car):
            names = GATHER_GROUPS[grp]
            sems[stage, l, grp], bufs, _, car = split_start(f"ag_{stage}_start_{grp}_l{l}", plan, 3 * len(names), [G[l][k] for k in names], [], car)
            G[l].update(zip(names, bufs))
            return car

        def wait(l, grp, car):
            names = GATHER_GROUPS[grp]
            bufs, car = split_wait(f"ag_{stage}_wait_{grp}_l{l}", plan, sems[stage, l, grp], [G[l][k] for k in names], car)
            G[l].update(zip(names, bufs))
            return car

        return start, wait

    ici_start, ici_wait = gather_stage("ici", plan_gather_ici)
    pass_start, pass_wait = gather_stage("pass", plan_gather_pass)

    car = tabs[0]
    first = [(l, grp) for l, grp in ((0, "mixer"), (0, "rest"), (1, "mixer")) if l < L]
    for l, grp in first:
        car = ici_start(l, grp, cast_group(l, grp, car))
    for l in range(L):
        for grp in GATHER_GROUPS:
            if (l, grp) not in first:
                car = cast_group(l, grp, car)
    tabs = (pass_wait(0, "mixer", pass_start(0, "mixer", ici_wait(0, "mixer", car))),) + tabs[1:]
    h = x.reshape(dm.M, dm.D)
    lws, sps, raws, saved = [], [], [], []
    for l in range(L):
        lw, sp, raw = layer_params(dm, small, l)
        for k in GATHER_GROUPS["mixer"]:
            lw[k] = regroup_cols(f"regroup_{k}_l{l}", G[l][k], sel[k], k in TRANSPOSED) if k in sel else G[l][k]

        def at_q(car, l=l):
            return pass_start(l, "rest", ici_wait(l, "rest", car))

        def at_mid(car, lw, l=l):
            car = pass_wait(l, "rest", car)
            lw.update({k: G[l][k] for k in GATHER_GROUPS["rest"]})
            if l + 1 < L:
                car = pass_wait(l + 1, "mixer", pass_start(l + 1, "mixer", ici_wait(l + 1, "mixer", car)))
                car = ici_start(l + 1, "rest", car)
            if l + 2 < L:
                car = ici_start(l + 2, "mixer", car)
            return car

        h, sv = layer_forward(dm, l, h, lw, sp, tabs, at_q, at_mid)
        lws.append(lw)
        sps.append(sp)
        raws.append(raw)
        saved.append(sv)
    dx, g_final_part, loss_part = loss_and_grad(dm, h, loss_target.reshape(dm.M, dm.D), w["g_final"])

    def reduce_begin(l, grp, g, car):
        names = REDUCE_GROUPS[grp]
        parts = [g.pop(k) for k in names]
        fresh = [jax.ShapeDtypeStruct(p.shape[1:], BF16) for p in parts]
        sm, parts, gots, car = split_start(f"rs_pair_start_{grp}_l{l}", plan_pair(len(names)), len(names), parts, fresh, car)
        return {"l": l, "grp": grp, "names": names, "sems": sm, "parts": parts, "gots": gots}, car

    def reduce_chips(st, car):
        names, tag, n = st["names"], f"{st['grp']}_l{st['l']}", len(st["names"])
        bufs, car = split_wait(f"rs_pair_wait_{tag}", plan_pair(n), st["sems"], st["parts"] + st["gots"], car)
        sums = [pair_sum(f"rs_pairsum_{a}_l{st['l']}", bufs[i], bufs[n + i]) for i, a in enumerate(names)]
        fresh = [jax.ShapeDtypeStruct((3,) + s.shape[1:], BF16) for s, _ in sums]
        st["sems"], st["s"], st["arrived"], car = split_start(f"rs_chips_start_{tag}", plan_chips(n), 3 * n, [s for s, _ in sums], fresh, car)
        st["own"] = [o for _, o in sums]
        return car

    def reduce_share(st, car):
        names, tag, n = st["names"], f"{st['grp']}_l{st['l']}", len(st["names"])
        bufs, car = split_wait(f"rs_chips_wait_{tag}", plan_chips(n), st["sems"], st["s"] + st["arrived"], car)
        mine = [chips_sum(f"rs_sum_{a}_l{st['l']}", st["own"][i], bufs[n + i]) for i, a in enumerate(names)]
        fresh = [jax.ShapeDtypeStruct(m_.shape, F32) for m_ in mine]
        st["sems"], st["mine"], st["theirs"], car = split_start(f"rs_share_start_{tag}", plan_share(n), n, mine, fresh, car)
        return car

    def reduce_end(st, car):
        n = len(st["names"])
        bufs, car = split_wait(f"rs_share_wait_{st['grp']}_l{st['l']}", plan_share(n), st["sems"], st["mine"] + st["theirs"], car)
        for i, k in enumerate(st["names"]):
            reduced[st["l"]][k] = (bufs[i], bufs[n + i])
        return car

    reduced, grads, prev_ffn, prev_oth = [{} for _ in range(L)], [None] * L, None, None
    for l in reversed(range(L)):
        mine = {}

        def at_a(car, g, l=l, mine=mine, pf=prev_ffn, po=prev_oth):
            mine["st"], car = reduce_begin(l, "ffn", g, car)
            if pf is not None:
                car = reduce_chips(po, reduce_share(pf, car))
            return car

        def at_m(car, g, mine=mine, pf=prev_ffn):
            car = reduce_chips(mine["st"], car)
            return car if pf is None else reduce_end(pf, car)

        def at_b(car, g, po=prev_oth):
            return car if po is None else reduce_share(po, car)

        dx, g = layer_backward(dm, l, dx, lws[l], sps[l], tabs, saved[l], at_a, at_m, at_b)
        ssm_param_grads(dm, l, g, raws[l])
        car = dx if l else loss_part
        if prev_oth is not None:
            car = reduce_end(prev_oth, car)
        for k in sel:
            g[k] = ungroup_cols(f"ungroup_{k}_l{l}", g[k], sel[k], k in TRANSPOSED)
        prev_ffn = mine["st"]
        prev_oth, car = reduce_begin(l, "others", g, car)
        if l:
            dx = car
        grads[l] = g
    car = reduce_end(prev_ffn, reduce_share(prev_ffn, reduce_chips(prev_oth, car)))
    loss_part = reduce_end(prev_oth, reduce_share(prev_oth, car))

    gsum = {}
    small_names = [k for k in SMALL if k != "g_final"]
    pieces = [jnp.stack([grads[l][k] for l in range(L)]).reshape(-1) for k in small_names] + [g_final_part.reshape(-1), loss_part.reshape(-1)]
    sizes = [int(p.shape[0]) for p in pieces]
    total = sum(sizes)
    rows = -(-total // (LANES * 16)) * 16
    pack = lambda ps: jnp.concatenate(ps + [jnp.zeros((rows * LANES - total,), F32)]).reshape(rows, LANES)
    red = all_reduce_small(pack(pieces))
    flat = red.reshape(-1)
    offs = np.concatenate([[0], np.cumsum(sizes)]).astype(int)
    names = small_names + ["g_final"]
    for i, k in enumerate(names):
        gsum[k] = flat[offs[i]:offs[i + 1]].reshape(w[k].shape)
    loss = flat[offs[len(names)]]

    delta, new_m, new_v = {}, {}, {}
    for k in SHARDED:
        res = adamw_layers(f"adam_{k}", stored(w[k], k), stored(mom[k], k), stored(var[k], k), [reduced[l][k][0] for l in range(L)],
                           [reduced[l][k][1] for l in range(L)], store_kind[k])
        gsum[k], delta[k], new_m[k], new_v[k] = (stored(t, k) for t in res)
    sm_sizes = sizes[:len(names)]
    sm_total = sum(sm_sizes)
    packs = lambda d: jnp.concatenate([d[k].reshape(-1) for k in names] + [jnp.zeros((rows * LANES - sm_total,), F32)]).reshape(rows, LANES)
    gs = jnp.concatenate([flat[:sm_total], jnp.zeros((rows * LANES - sm_total,), F32)]).reshape(rows, LANES)
    d_, m_, v_ = adamw("adam_small", packs(w), gs, packs(mom), packs(var))
    for i, k in enumerate(names):
        sl = slice(offs[i], offs[i + 1])
        delta[k], new_m[k], new_v[k] = (t.reshape(-1)[sl].reshape(w[k].shape) for t in (d_, m_, v_))

    return (loss, dx.reshape(x.shape), *[gsum[k] for k in ORDER], *[delta[k] for k in ORDER],
            *[new_m[k] for k in ORDER], *[new_v[k] for k in ORDER])
```
